```python
import jax, jax.numpy as jnp
from jax import lax
import numpy as np

D_MODEL = 2048
BATCH = 16
SEQ = 2048
DEPTH = 1

N_META = 16
NORM_EPS = 1e-6
D_FF = ((8 * D_MODEL // 3 + 255) // 256) * 256
RWKV_HEAD = 64
RWKV_HEADS = D_MODEL // RWKV_HEAD
RWKV_WIDTH = RWKV_HEADS * RWKV_HEAD
W_LORA = 96
A_LORA = 96
G_LORA = 256
GN_EPS = RWKV_HEAD * 1e-5
MLA_HEADS = D_MODEL // 128
Q_LORA = 512
KV_LORA = 512
NOPE_DIM = 128
ROPE_DIM = 64
V_DIM = 128
QK_DIM = NOPE_DIM + ROPE_DIM
ROPE_THETA = 10000.0
Q_BLOCK = 128
RWKV_COLS = 3 * RWKV_WIDTH + W_LORA + A_LORA + G_LORA
MLA_COLS = Q_LORA + KV_LORA + ROPE_DIM
GATE_COLS = 2 * D_MODEL
IN_COLS = RWKV_COLS + MLA_COLS + GATE_COLS

kernel_name = "macaron_rwkv7_mla_gated_hybrid"


def _split(x, sizes):
    idx = np.cumsum(sizes)[:-1].tolist()
    return jnp.split(x, idx, axis=-1)


def rms_norm(x, g):
    xf = x.astype(jnp.float32)
    y = xf * lax.rsqrt(jnp.mean(xf * xf, axis=-1, keepdims=True) + NORM_EPS)
    return (y * g.astype(jnp.float32)).astype(x.dtype)


def swiglu(h, w_gate, w_up, w_down):
    return (jax.nn.silu(h @ w_gate) * (h @ w_up)) @ w_down


def rope(x, cos, sin):
    x1, x2 = jnp.split(x.astype(jnp.float32), 2, axis=-1)
    c, s = cos[:, None, :], sin[:, None, :]
    return jnp.concatenate([x1 * c - x2 * s, x1 * s + x2 * c], axis=-1).astype(x.dtype)


def wkv7_scan(r, w, k, v, kk_neg, b):
    bsz, _, h, n = r.shape

    def step(S, inp):
        r_t, w_t, k_t, v_t, kn_t, b_t = inp
        sa = jnp.einsum("bhij,bhj->bhi", S, kn_t)
        S = S * w_t[:, :, None, :] + sa[..., None] * b_t[:, :, None, :] + v_t[..., None] * k_t[:, :, None, :]
        return S, jnp.einsum("bhij,bhj->bhi", S, r_t)

    xs = tuple(jnp.moveaxis(t, 1, 0) for t in (r, w, k, v, kk_neg, b))
    S0 = jnp.zeros((bsz, h, n, n), jnp.float32)
    _, y = lax.scan(step, S0, xs)
    return jnp.moveaxis(y, 0, 1)


def rwkv7_branch(p, mu, w0, w_up, a0, a_up, g_up, k_k, k_a, r_k, gn_w, gn_b):
    bsz, t, _ = p.shape
    prev = jnp.pad(p, ((0, 0), (1, 0), (0, 0)))[:, :-1]
    p = p + mu * (prev - p)
    r, k, v, xw, xa, xg = _split(p, (RWKV_WIDTH, RWKV_WIDTH, RWKV_WIDTH, W_LORA, A_LORA, G_LORA))
    w_pre = -jax.nn.softplus(-(w0 + jnp.tanh(xw) @ w_up)) - 0.5
    decay = jnp.exp(-jnp.exp(w_pre.astype(jnp.float32)))
    a = jax.nn.sigmoid(a0 + xa @ a_up)
    g = jax.nn.sigmoid(xg) @ g_up
    heads = lambda z: z.reshape(bsz, t, RWKV_HEADS, RWKV_HEAD).astype(jnp.float32)
    kk = heads(k * k_k)
    kk = kk * lax.rsqrt(jnp.maximum(jnp.sum(kk * kk, axis=-1, keepdims=True), 1e-24))
    k = k * (1.0 + (a - 1.0) * k_a)
    rh, kh, vh, ah, wh = heads(r), heads(k), heads(v), heads(a), heads(decay)
    y = wkv7_scan(rh, wh, kh, vh, -kk, kk * ah)
    mean = jnp.mean(y, axis=-1, keepdims=True)
    var = jnp.mean(jnp.square(y - mean), axis=-1, keepdims=True)
    y = ((y - mean) * lax.rsqrt(var + GN_EPS)).reshape(bsz, t, RWKV_WIDTH)
    y = y * gn_w.astype(jnp.float32) + gn_b.astype(jnp.float32)
    bonus = jnp.sum(rh * kh * r_k.astype(jnp.float32), axis=-1, keepdims=True) * vh
    y = y + bonus.reshape(bsz, t, RWKV_WIDTH)
    return (y * g.astype(jnp.float32)).astype(p.dtype)


def causal_block_attention(q, k, v):
    bsz, t, h, dq = q.shape
    scale = QK_DIM ** -0.5
    kpos = jnp.arange(t)

    def attend(qb, qpos):
        s = jnp.einsum("bqhd,bkhd->bhqk", qb, k).astype(jnp.float32) * scale
        s = jnp.where(kpos[None, :] <= qpos[:, None], s, -1e30)
        p = jax.nn.softmax(s, axis=-1).astype(v.dtype)
        return jnp.einsum("bhqk,bkhd->bqhd", p, v)

    meta_out = attend(q[:, :N_META], jnp.arange(N_META))
    n_real = t - N_META
    nblk = n_real // Q_BLOCK
    q_real = jnp.moveaxis(q[:, N_META:].reshape(bsz, nblk, Q_BLOCK, h, dq), 1, 0)

    def body(args):
        qb, i = args
        return attend(qb, N_META + i * Q_BLOCK + jnp.arange(Q_BLOCK))

    real = lax.map(body, (q_real, jnp.arange(nblk)))
    real = jnp.moveaxis(real, 0, 1).reshape(bsz, n_real, h, v.shape[-1])
    return jnp.concatenate([meta_out, real], axis=1)


def mla_branch(p, q_norm, w_uq, kv_norm, w_ukv, cos, sin):
    bsz, t, _ = p.shape
    c_q, c_kv, k_pe = _split(p, (Q_LORA, KV_LORA, ROPE_DIM))
    q = (rms_norm(c_q, q_norm) @ w_uq).reshape(bsz, t, MLA_HEADS, QK_DIM)
    q_nope, q_pe = _split(q, (NOPE_DIM, ROPE_DIM))
    kv = (rms_norm(c_kv, kv_norm) @ w_ukv).reshape(bsz, t, MLA_HEADS, NOPE_DIM + V_DIM)
    k_nope, v = _split(kv, (NOPE_DIM, V_DIM))
    q_pe = rope(q_pe, cos, sin)
    k_pe = rope(k_pe[:, :, None, :], cos, sin)
    q = jnp.concatenate([q_nope, q_pe], axis=-1)
    k = jnp.concatenate([k_nope, jnp.broadcast_to(k_pe, (bsz, t, MLA_HEADS, ROPE_DIM))], axis=-1)
    o = causal_block_attention(q, k, v)
    return o.reshape(bsz, t, MLA_HEADS * V_DIM)


def _fwd_setup_inputs(seed: int = 0) -> dict:
    key = jax.random.key(seed)
    ks = jax.random.split(key, 32)
    L, D = DEPTH, D_MODEL
    nrm = lambda k, shape, s: jax.random.normal(k, shape, jnp.float32) * s
    uni = lambda k, shape: jax.random.uniform(k, shape, jnp.float32)
    return {
        "x": nrm(ks[0], (BATCH, SEQ, D), 1.0),
        "meta_tokens": nrm(ks[1], (N_META, D), 1.0),
        "ffn1_norm": 1.0 + nrm(ks[2], (L, D), 0.02),
        "ffn1_w_gate": nrm(ks[3], (L, D, D_FF), D ** -0.5),
        "ffn1_w_up": nrm(ks[4], (L, D, D_FF), D ** -0.5),
        "ffn1_w_down": nrm(ks[5], (L, D_FF, D), D_FF ** -0.5),
        "mix_norm": 1.0 + nrm(ks[6], (L, D), 0.02),
        "w_in": nrm(ks[7], (L, D, IN_COLS), D ** -0.5),
        "tm_mu": uni(ks[8], (L, RWKV_COLS)),
        "w0": -6.5 + 5.0 * uni(ks[9], (L, RWKV_WIDTH)),
        "w_up": nrm(ks[10], (L, W_LORA, RWKV_WIDTH), W_LORA ** -0.5),
        "a0": nrm(ks[11], (L, RWKV_WIDTH), 0.1),
        "a_up": nrm(ks[12], (L, A_LORA, RWKV_WIDTH), A_LORA ** -0.5),
        "g_up": nrm(ks[13], (L, G_LORA, RWKV_WIDTH), G_LORA ** -0.5),
        "k_k": 0.85 + nrm(ks[14], (L, RWKV_WIDTH), 0.02),
        "k_a": 1.0 + nrm(ks[15], (L, RWKV_WIDTH), 0.02),
        "r_k": nrm(ks[16], (L, RWKV_HEADS, RWKV_HEAD), 0.1),
        "gn_w": 1.0 + nrm(ks[17], (L, RWKV_WIDTH), 0.02),
        "gn_b": nrm(ks[18], (L, RWKV_WIDTH), 0.01),
        "q_norm": 1.0 + nrm(ks[19], (L, Q_LORA), 0.02),
        "w_uq": nrm(ks[20], (L, Q_LORA, MLA_HEADS * QK_DIM), Q_LORA ** -0.5),
        "kv_norm": 1.0 + nrm(ks[21], (L, KV_LORA), 0.02),
        "w_ukv": nrm(ks[22], (L, KV_LORA, MLA_HEADS * (NOPE_DIM + V_DIM)), KV_LORA ** -0.5),
        "w_out": nrm(ks[23], (L, D, D), D ** -0.5),
        "ffn2_norm": 1.0 + nrm(ks[24], (L, D), 0.02),
        "ffn2_w_gate": nrm(ks[25], (L, D, D_FF), D ** -0.5),
        "ffn2_w_up": nrm(ks[26], (L, D, D_FF), D ** -0.5),
        "ffn2_w_down": nrm(ks[27], (L, D_FF, D), D_FF ** -0.5),
        "final_norm": 1.0 + nrm(ks[28], (D,), 0.02),
    }


def _fwd_reference(x, meta_tokens, ffn1_norm, ffn1_w_gate, ffn1_w_up, ffn1_w_down, mix_norm, w_in,
              tm_mu, w0, w_up, a0, a_up, g_up, k_k, k_a, r_k, gn_w, gn_b, q_norm, w_uq,
              kv_norm, w_ukv, w_out, ffn2_norm, ffn2_w_gate, ffn2_w_up, ffn2_w_down, final_norm):
    bsz = x.shape[0]
    h = jnp.concatenate([jnp.broadcast_to(meta_tokens.astype(x.dtype)[None], (bsz, N_META, D_MODEL)), x], axis=1)
    t = h.shape[1]
    pos = jnp.arange(t, dtype=jnp.float32)
    inv_freq = 1.0 / (ROPE_THETA ** (jnp.arange(0, ROPE_DIM, 2, dtype=jnp.float32) / ROPE_DIM))
    ang = pos[:, None] * inv_freq[None, :]
    cos, sin = jnp.cos(ang), jnp.sin(ang)

    for l in range(DEPTH):
        h = h + 0.5 * swiglu(rms_norm(h, ffn1_norm[l]), ffn1_w_gate[l], ffn1_w_up[l], ffn1_w_down[l])
        u = rms_norm(h, mix_norm[l])
        proj = u @ w_in[l]
        p_rwkv, p_mla, p_gate = _split(proj, (RWKV_COLS, MLA_COLS, GATE_COLS))
        y_a = rwkv7_branch(p_rwkv, tm_mu[l], w0[l], w_up[l], a0[l], a_up[l], g_up[l],
                           k_k[l], k_a[l], r_k[l], gn_w[l], gn_b[l])
        y_b = mla_branch(p_mla, q_norm[l], w_uq[l], kv_norm[l], w_ukv[l], cos, sin)
        g_a, g_b = jnp.split(jax.nn.sigmoid(p_gate), 2, axis=-1)
        h = h + (g_a * y_a + g_b * y_b) @ w_out[l]
        h = h + 0.5 * swiglu(rms_norm(h, ffn2_norm[l]), ffn2_w_gate[l], ffn2_w_up[l], ffn2_w_down[l])

    y = rms_norm(h, final_norm)[:, N_META:]
    return y


import jax as _jax
import jax.numpy as _jnp

TWIN_FORMAT = 'train_step'
FWD_PARAMS = ['x', 'meta_tokens', 'ffn1_norm', 'ffn1_w_gate', 'ffn1_w_up', 'ffn1_w_down', 'mix_norm', 'w_in', 'tm_mu', 'w0', 'w_up', 'a0', 'a_up', 'g_up', 'k_k', 'k_a', 'r_k', 'gn_w', 'gn_b', 'q_norm', 'w_uq', 'kv_norm', 'w_ukv', 'w_out', 'ffn2_norm', 'ffn2_w_gate', 'ffn2_w_up', 'ffn2_w_down', 'final_norm']
TWIN_WEIGHTS = ['meta_tokens', 'ffn1_norm', 'ffn1_w_gate', 'ffn1_w_up', 'ffn1_w_down', 'mix_norm', 'w_in', 'tm_mu', 'w0', 'w_up', 'a0', 'a_up', 'g_up', 'k_k', 'k_a', 'r_k', 'gn_w', 'gn_b', 'q_norm', 'w_uq', 'kv_norm', 'w_ukv', 'w_out', 'ffn2_norm', 'ffn2_w_gate', 'ffn2_w_up', 'ffn2_w_down', 'final_norm']
TWIN_DIFF_INPUT = 'x'
TWIN_INPUTS = ['x', 'meta_tokens', 'ffn1_norm', 'ffn1_w_gate', 'ffn1_w_up', 'ffn1_w_down', 'mix_norm', 'w_in', 'tm_mu', 'w0', 'w_up', 'a0', 'a_up', 'g_up', 'k_k', 'k_a', 'r_k', 'gn_w', 'gn_b', 'q_norm', 'w_uq', 'kv_norm', 'w_ukv', 'w_out', 'ffn2_norm', 'ffn2_w_gate', 'ffn2_w_up', 'ffn2_w_down', 'final_norm', 'loss_target', 'm_meta_tokens', 'm_ffn1_norm', 'm_ffn1_w_gate', 'm_ffn1_w_up', 'm_ffn1_w_down', 'm_mix_norm', 'm_w_in', 'm_tm_mu', 'm_w0', 'm_w_up', 'm_a0', 'm_a_up', 'm_g_up', 'm_k_k', 'm_k_a', 'm_r_k', 'm_gn_w', 'm_gn_b', 'm_q_norm', 'm_w_uq', 'm_kv_norm', 'm_w_ukv', 'm_w_out', 'm_ffn2_norm', 'm_ffn2_w_gate', 'm_ffn2_w_up', 'm_ffn2_w_down', 'm_final_norm', 'v_meta_tokens', 'v_ffn1_norm', 'v_ffn1_w_gate', 'v_ffn1_w_up', 'v_ffn1_w_down', 'v_mix_norm', 'v_w_in', 'v_tm_mu', 'v_w0', 'v_w_up', 'v_a0', 'v_a_up', 'v_g_up', 'v_k_k', 'v_k_a', 'v_r_k', 'v_gn_w', 'v_gn_b', 'v_q_norm', 'v_w_uq', 'v_kv_norm', 'v_w_ukv', 'v_w_out', 'v_ffn2_norm', 'v_ffn2_w_gate', 'v_ffn2_w_up', 'v_ffn2_w_down', 'v_final_norm']
TWIN_OUTPUTS = ['loss', 'grad_x', 'grad_meta_tokens', 'grad_ffn1_norm', 'grad_ffn1_w_gate', 'grad_ffn1_w_up', 'grad_ffn1_w_down', 'grad_mix_norm', 'grad_w_in', 'grad_tm_mu', 'grad_w0', 'grad_w_up', 'grad_a0', 'grad_a_up', 'grad_g_up', 'grad_k_k', 'grad_k_a', 'grad_r_k', 'grad_gn_w', 'grad_gn_b', 'grad_q_norm', 'grad_w_uq', 'grad_kv_norm', 'grad_w_ukv', 'grad_w_out', 'grad_ffn2_norm', 'grad_ffn2_w_gate', 'grad_ffn2_w_up', 'grad_ffn2_w_down', 'grad_final_norm', 'delta_meta_tokens', 'delta_ffn1_norm', 'delta_ffn1_w_gate', 'delta_ffn1_w_up', 'delta_ffn1_w_down', 'delta_mix_norm', 'delta_w_in', 'delta_tm_mu', 'delta_w0', 'delta_w_up', 'delta_a0', 'delta_a_up', 'delta_g_up', 'delta_k_k', 'delta_k_a', 'delta_r_k', 'delta_gn_w', 'delta_gn_b', 'delta_q_norm', 'delta_w_uq', 'delta_kv_norm', 'delta_w_ukv', 'delta_w_out', 'delta_ffn2_norm', 'delta_ffn2_w_gate', 'delta_ffn2_w_up', 'delta_ffn2_w_down', 'delta_final_norm', 'new_m_meta_tokens', 'new_m_ffn1_norm', 'new_m_ffn1_w_gate', 'new_m_ffn1_w_up', 'new_m_ffn1_w_down', 'new_m_mix_norm', 'new_m_w_in', 'new_m_tm_mu', 'new_m_w0', 'new_m_w_up', 'new_m_a0', 'new_m_a_up', 'new_m_g_up', 'new_m_k_k', 'new_m_k_a', 'new_m_r_k', 'new_m_gn_w', 'new_m_gn_b', 'new_m_q_norm', 'new_m_w_uq', 'new_m_kv_norm', 'new_m_w_ukv', 'new_m_w_out', 'new_m_ffn2_norm', 'new_m_ffn2_w_gate', 'new_m_ffn2_w_up', 'new_m_ffn2_w_down', 'new_m_final_norm', 'new_v_meta_tokens', 'new_v_ffn1_norm', 'new_v_ffn1_w_gate', 'new_v_ffn1_w_up', 'new_v_ffn1_w_down', 'new_v_mix_norm', 'new_v_w_in', 'new_v_tm_mu', 'new_v_w0', 'new_v_w_up', 'new_v_a0', 'new_v_a_up', 'new_v_g_up', 'new_v_k_k', 'new_v_k_a', 'new_v_r_k', 'new_v_gn_w', 'new_v_gn_b', 'new_v_q_norm', 'new_v_w_uq', 'new_v_kv_norm', 'new_v_w_ukv', 'new_v_w_out', 'new_v_ffn2_norm', 'new_v_ffn2_w_gate', 'new_v_ffn2_w_up', 'new_v_ffn2_w_down', 'new_v_final_norm']
TWIN_LEAF_KINDS = {'loss': 'loss', 'grad_x': 'grad_x', 'grad_meta_tokens': 'grad_w', 'grad_ffn1_norm': 'grad_w', 'grad_ffn1_w_gate': 'grad_w', 'grad_ffn1_w_up': 'grad_w', 'grad_ffn1_w_down': 'grad_w', 'grad_mix_norm': 'grad_w', 'grad_w_in': 'grad_w', 'grad_tm_mu': 'grad_w', 'grad_w0': 'grad_w', 'grad_w_up': 'grad_w', 'grad_a0': 'grad_w', 'grad_a_up': 'grad_w', 'grad_g_up': 'grad_w', 'grad_k_k': 'grad_w', 'grad_k_a': 'grad_w', 'grad_r_k': 'grad_w', 'grad_gn_w': 'grad_w', 'grad_gn_b': 'grad_w', 'grad_q_norm': 'grad_w', 'grad_w_uq': 'grad_w', 'grad_kv_norm': 'grad_w', 'grad_w_ukv': 'grad_w', 'grad_w_out': 'grad_w', 'grad_ffn2_norm': 'grad_w', 'grad_ffn2_w_gate': 'grad_w', 'grad_ffn2_w_up': 'grad_w', 'grad_ffn2_w_down': 'grad_w', 'grad_final_norm': 'grad_w', 'delta_meta_tokens': 'delta_w', 'delta_ffn1_norm': 'delta_w', 'delta_ffn1_w_gate': 'delta_w', 'delta_ffn1_w_up': 'delta_w', 'delta_ffn1_w_down': 'delta_w', 'delta_mix_norm': 'delta_w', 'delta_w_in': 'delta_w', 'delta_tm_mu': 'delta_w', 'delta_w0': 'delta_w', 'delta_w_up': 'delta_w', 'delta_a0': 'delta_w', 'delta_a_up': 'delta_w', 'delta_g_up': 'delta_w', 'delta_k_k': 'delta_w', 'delta_k_a': 'delta_w', 'delta_r_k': 'delta_w', 'delta_gn_w': 'delta_w', 'delta_gn_b': 'delta_w', 'delta_q_norm': 'delta_w', 'delta_w_uq': 'delta_w', 'delta_kv_norm': 'delta_w', 'delta_w_ukv': 'delta_w', 'delta_w_out': 'delta_w', 'delta_ffn2_norm': 'delta_w', 'delta_ffn2_w_gate': 'delta_w', 'delta_ffn2_w_up': 'delta_w', 'delta_ffn2_w_down': 'delta_w', 'delta_final_norm': 'delta_w', 'new_m_meta_tokens': 'new_m', 'new_m_ffn1_norm': 'new_m', 'new_m_ffn1_w_gate': 'new_m', 'new_m_ffn1_w_up': 'new_m', 'new_m_ffn1_w_down': 'new_m', 'new_m_mix_norm': 'new_m', 'new_m_w_in': 'new_m', 'new_m_tm_mu': 'new_m', 'new_m_w0': 'new_m', 'new_m_w_up': 'new_m', 'new_m_a0': 'new_m', 'new_m_a_up': 'new_m', 'new_m_g_up': 'new_m', 'new_m_k_k': 'new_m', 'new_m_k_a': 'new_m', 'new_m_r_k': 'new_m', 'new_m_gn_w': 'new_m', 'new_m_gn_b': 'new_m', 'new_m_q_norm': 'new_m', 'new_m_w_uq': 'new_m', 'new_m_kv_norm': 'new_m', 'new_m_w_ukv': 'new_m', 'new_m_w_out': 'new_m', 'new_m_ffn2_norm': 'new_m', 'new_m_ffn2_w_gate': 'new_m', 'new_m_ffn2_w_up': 'new_m', 'new_m_ffn2_w_down': 'new_m', 'new_m_final_norm': 'new_m', 'new_v_meta_tokens': 'new_v', 'new_v_ffn1_norm': 'new_v', 'new_v_ffn1_w_gate': 'new_v', 'new_v_ffn1_w_up': 'new_v', 'new_v_ffn1_w_down': 'new_v', 'new_v_mix_norm': 'new_v', 'new_v_w_in': 'new_v', 'new_v_tm_mu': 'new_v', 'new_v_w0': 'new_v', 'new_v_w_up': 'new_v', 'new_v_a0': 'new_v', 'new_v_a_up': 'new_v', 'new_v_g_up': 'new_v', 'new_v_k_k': 'new_v', 'new_v_k_a': 'new_v', 'new_v_r_k': 'new_v', 'new_v_gn_w': 'new_v', 'new_v_gn_b': 'new_v', 'new_v_q_norm': 'new_v', 'new_v_w_uq': 'new_v', 'new_v_kv_norm': 'new_v', 'new_v_w_ukv': 'new_v', 'new_v_w_out': 'new_v', 'new_v_ffn2_norm': 'new_v', 'new_v_ffn2_w_gate': 'new_v', 'new_v_ffn2_w_up': 'new_v', 'new_v_ffn2_w_down': 'new_v', 'new_v_final_norm': 'new_v'}


def _forward(args):
    return _fwd_reference(*[args[k] for k in FWD_PARAMS])


def _output_shape():
    out = _jax.eval_shape(lambda: _forward(_fwd_setup_inputs(0)))
    return out.shape, out.dtype

N_MICROBATCH = 1
ADAM_LR = 0.001
ADAM_B1 = 0.9
ADAM_B2 = 0.999
ADAM_EPS = 1e-08
ADAM_WD = 0.01
ADAM_STEP = 10
PER_EXAMPLE_BATCH_AXIS = {'x': 0, 'loss_target': 0}
SHARED_INPUTS = []
_WEIGHT_DTYPES = {'meta_tokens': _jnp.float32, 'ffn1_norm': _jnp.float32, 'ffn1_w_gate': _jnp.float32, 'ffn1_w_up': _jnp.float32, 'ffn1_w_down': _jnp.float32, 'mix_norm': _jnp.float32, 'w_in': _jnp.float32, 'tm_mu': _jnp.float32, 'w0': _jnp.float32, 'w_up': _jnp.float32, 'a0': _jnp.float32, 'a_up': _jnp.float32, 'g_up': _jnp.float32, 'k_k': _jnp.float32, 'k_a': _jnp.float32, 'r_k': _jnp.float32, 'gn_w': _jnp.float32, 'gn_b': _jnp.float32, 'q_norm': _jnp.float32, 'w_uq': _jnp.float32, 'kv_norm': _jnp.float32, 'w_ukv': _jnp.float32, 'w_out': _jnp.float32, 'ffn2_norm': _jnp.float32, 'ffn2_w_gate': _jnp.float32, 'ffn2_w_up': _jnp.float32, 'ffn2_w_down': _jnp.float32, 'final_norm': _jnp.float32}
MOMENT_SCALE = {'meta_tokens': 3.083688e-03, 'ffn1_norm': 4.138915e-02, 'ffn1_w_gate': 1.796390e-02, 'ffn1_w_up': 1.739695e-02, 'ffn1_w_down': 2.884479e-02, 'mix_norm': 5.020210e-02, 'w_in': 2.076714e-02, 'tm_mu': 4.231606e-02, 'w0': 9.985438e-03, 'w_up': 1.102326e-03, 'a0': 9.206593e-03, 'a_up': 8.759664e-03, 'g_up': 2.481711e-02, 'k_k': 3.622142e-02, 'k_a': 2.661762e-02, 'r_k': 5.554713e-02, 'gn_w': 2.422410e-02, 'gn_b': 2.426670e-02, 'q_norm': 1.413579e-02, 'w_uq': 5.887462e-03, 'kv_norm': 2.064404e-02, 'w_ukv': 6.872887e-03, 'w_out': 2.611706e-02, 'ffn2_norm': 3.386027e-02, 'ffn2_w_gate': 1.463915e-02, 'ffn2_w_up': 1.418558e-02, 'ffn2_w_down': 2.350244e-02, 'final_norm': 1.598080e+01}


def _to_microbatches(a, axis):
    t = _jnp.moveaxis(a, axis, 0)
    t = t.reshape((N_MICROBATCH, t.shape[0] // N_MICROBATCH) + t.shape[1:])
    return _jnp.moveaxis(t, 1, axis + 1)


def setup_inputs(seed: int = 0) -> dict:
    inp = _fwd_setup_inputs(seed)
    key = _jax.random.fold_in(_jax.random.key(seed), 7919)
    shape, _ = _output_shape()
    out = dict(inp)
    out["loss_target"] = _jax.random.normal(_jax.random.fold_in(key, 0), shape, _jnp.float32)
    for i, name in enumerate(TWIN_WEIGHTS):
        w = inp[name].astype(_jnp.float32)
        if MOMENT_SCALE is None:
            s = _jnp.sqrt(_jnp.mean(_jnp.square(w)) + 1e-30)
        else:
            s = MOMENT_SCALE[name]
        km, kv = _jax.random.split(_jax.random.fold_in(key, i + 1))
        out[name] = w
        out["m_" + name] = s * _jax.random.normal(km, w.shape, _jnp.float32)
        out["v_" + name] = (s * s) * _jax.random.uniform(kv, w.shape, _jnp.float32, 0.5, 1.5)
    if N_MICROBATCH > 1:
        for name, axis in PER_EXAMPLE_BATCH_AXIS.items():
            out[name] = _to_microbatches(out[name], axis)
    return {'x': out['x'], 'meta_tokens': out['meta_tokens'], 'ffn1_norm': out['ffn1_norm'], 'ffn1_w_gate': out['ffn1_w_gate'], 'ffn1_w_up': out['ffn1_w_up'], 'ffn1_w_down': out['ffn1_w_down'], 'mix_norm': out['mix_norm'], 'w_in': out['w_in'], 'tm_mu': out['tm_mu'], 'w0': out['w0'], 'w_up': out['w_up'], 'a0': out['a0'], 'a_up': out['a_up'], 'g_up': out['g_up'], 'k_k': out['k_k'], 'k_a': out['k_a'], 'r_k': out['r_k'], 'gn_w': out['gn_w'], 'gn_b': out['gn_b'], 'q_norm': out['q_norm'], 'w_uq': out['w_uq'], 'kv_norm': out['kv_norm'], 'w_ukv': out['w_ukv'], 'w_out': out['w_out'], 'ffn2_norm': out['ffn2_norm'], 'ffn2_w_gate': out['ffn2_w_gate'], 'ffn2_w_up': out['ffn2_w_up'], 'ffn2_w_down': out['ffn2_w_down'], 'final_norm': out['final_norm'], 'loss_target': out['loss_target'], 'm_meta_tokens': out['m_meta_tokens'], 'm_ffn1_norm': out['m_ffn1_norm'], 'm_ffn1_w_gate': out['m_ffn1_w_gate'], 'm_ffn1_w_up': out['m_ffn1_w_up'], 'm_ffn1_w_down': out['m_ffn1_w_down'], 'm_mix_norm': out['m_mix_norm'], 'm_w_in': out['m_w_in'], 'm_tm_mu': out['m_tm_mu'], 'm_w0': out['m_w0'], 'm_w_up': out['m_w_up'], 'm_a0': out['m_a0'], 'm_a_up': out['m_a_up'], 'm_g_up': out['m_g_up'], 'm_k_k': out['m_k_k'], 'm_k_a': out['m_k_a'], 'm_r_k': out['m_r_k'], 'm_gn_w': out['m_gn_w'], 'm_gn_b': out['m_gn_b'], 'm_q_norm': out['m_q_norm'], 'm_w_uq': out['m_w_uq'], 'm_kv_norm': out['m_kv_norm'], 'm_w_ukv': out['m_w_ukv'], 'm_w_out': out['m_w_out'], 'm_ffn2_norm': out['m_ffn2_norm'], 'm_ffn2_w_gate': out['m_ffn2_w_gate'], 'm_ffn2_w_up': out['m_ffn2_w_up'], 'm_ffn2_w_down': out['m_ffn2_w_down'], 'm_final_norm': out['m_final_norm'], 'v_meta_tokens': out['v_meta_tokens'], 'v_ffn1_norm': out['v_ffn1_norm'], 'v_ffn1_w_gate': out['v_ffn1_w_gate'], 'v_ffn1_w_up': out['v_ffn1_w_up'], 'v_ffn1_w_down': out['v_ffn1_w_down'], 'v_mix_norm': out['v_mix_norm'], 'v_w_in': out['v_w_in'], 'v_tm_mu': out['v_tm_mu'], 'v_w0': out['v_w0'], 'v_w_up': out['v_w_up'], 'v_a0': out['v_a0'], 'v_a_up': out['v_a_up'], 'v_g_up': out['v_g_up'], 'v_k_k': out['v_k_k'], 'v_k_a': out['v_k_a'], 'v_r_k': out['v_r_k'], 'v_gn_w': out['v_gn_w'], 'v_gn_b': out['v_gn_b'], 'v_q_norm': out['v_q_norm'], 'v_w_uq': out['v_w_uq'], 'v_kv_norm': out['v_kv_norm'], 'v_w_ukv': out['v_w_ukv'], 'v_w_out': out['v_w_out'], 'v_ffn2_norm': out['v_ffn2_norm'], 'v_ffn2_w_gate': out['v_ffn2_w_gate'], 'v_ffn2_w_up': out['v_ffn2_w_up'], 'v_ffn2_w_down': out['v_ffn2_w_down'], 'v_final_norm': out['v_final_norm']}


def _loss(weights, diff, rest, loss_target):
    with _jax.named_scope("forward"):
        args = {**rest, TWIN_DIFF_INPUT: diff, **{k: w.astype(_WEIGHT_DTYPES[k]) for k, w in weights.items()}}
        y = _forward(args)
    with _jax.named_scope("loss_head"):
        err = _jnp.square(y.astype(_jnp.float32) - loss_target)
        return 0.5 * _jnp.sum(_jnp.mean(err, axis=-1)) if err.ndim else 0.5 * err


def _adamw(w, g, m, v):
    m = ADAM_B1 * m + (1.0 - ADAM_B1) * g
    v = ADAM_B2 * v + (1.0 - ADAM_B2) * _jnp.square(g)
    m_hat = m / (1.0 - ADAM_B1 ** ADAM_STEP)
    v_hat = v / (1.0 - ADAM_B2 ** ADAM_STEP)
    delta = -ADAM_LR * (m_hat / (_jnp.sqrt(v_hat) + ADAM_EPS) + ADAM_WD * w)
    return delta, m, v


def reference(x, meta_tokens, ffn1_norm, ffn1_w_gate, ffn1_w_up, ffn1_w_down, mix_norm, w_in, tm_mu, w0, w_up, a0, a_up, g_up, k_k, k_a, r_k, gn_w, gn_b, q_norm, w_uq, kv_norm, w_ukv, w_out, ffn2_norm, ffn2_w_gate, ffn2_w_up, ffn2_w_down, final_norm, loss_target, m_meta_tokens, m_ffn1_norm, m_ffn1_w_gate, m_ffn1_w_up, m_ffn1_w_down, m_mix_norm, m_w_in, m_tm_mu, m_w0, m_w_up, m_a0, m_a_up, m_g_up, m_k_k, m_k_a, m_r_k, m_gn_w, m_gn_b, m_q_norm, m_w_uq, m_kv_norm, m_w_ukv, m_w_out, m_ffn2_norm, m_ffn2_w_gate, m_ffn2_w_up, m_ffn2_w_down, m_final_norm, v_meta_tokens, v_ffn1_norm, v_ffn1_w_gate, v_ffn1_w_up, v_ffn1_w_down, v_mix_norm, v_w_in, v_tm_mu, v_w0, v_w_up, v_a0, v_a_up, v_g_up, v_k_k, v_k_a, v_r_k, v_gn_w, v_gn_b, v_q_norm, v_w_uq, v_kv_norm, v_w_ukv, v_w_out, v_ffn2_norm, v_ffn2_w_gate, v_ffn2_w_up, v_ffn2_w_down, v_final_norm):
    given = dict(x=x, meta_tokens=meta_tokens, ffn1_norm=ffn1_norm, ffn1_w_gate=ffn1_w_gate, ffn1_w_up=ffn1_w_up, ffn1_w_down=ffn1_w_down, mix_norm=mix_norm, w_in=w_in, tm_mu=tm_mu, w0=w0, w_up=w_up, a0=a0, a_up=a_up, g_up=g_up, k_k=k_k, k_a=k_a, r_k=r_k, gn_w=gn_w, gn_b=gn_b, q_norm=q_norm, w_uq=w_uq, kv_norm=kv_norm, w_ukv=w_ukv, w_out=w_out, ffn2_norm=ffn2_norm, ffn2_w_gate=ffn2_w_gate, ffn2_w_up=ffn2_w_up, ffn2_w_down=ffn2_w_down, final_norm=final_norm, loss_target=loss_target, m_meta_tokens=m_meta_tokens, m_ffn1_norm=m_ffn1_norm, m_ffn1_w_gate=m_ffn1_w_gate, m_ffn1_w_up=m_ffn1_w_up, m_ffn1_w_down=m_ffn1_w_down, m_mix_norm=m_mix_norm, m_w_in=m_w_in, m_tm_mu=m_tm_mu, m_w0=m_w0, m_w_up=m_w_up, m_a0=m_a0, m_a_up=m_a_up, m_g_up=m_g_up, m_k_k=m_k_k, m_k_a=m_k_a, m_r_k=m_r_k, m_gn_w=m_gn_w, m_gn_b=m_gn_b, m_q_norm=m_q_norm, m_w_uq=m_w_uq, m_kv_norm=m_kv_norm, m_w_ukv=m_w_ukv, m_w_out=m_w_out, m_ffn2_norm=m_ffn2_norm, m_ffn2_w_gate=m_ffn2_w_gate, m_ffn2_w_up=m_ffn2_w_up, m_ffn2_w_down=m_ffn2_w_down, m_final_norm=m_final_norm, v_meta_tokens=v_meta_tokens, v_ffn1_norm=v_ffn1_norm, v_ffn1_w_gate=v_ffn1_w_gate, v_ffn1_w_up=v_ffn1_w_up, v_ffn1_w_down=v_ffn1_w_down, v_mix_norm=v_mix_norm, v_w_in=v_w_in, v_tm_mu=v_tm_mu, v_w0=v_w0, v_w_up=v_w_up, v_a0=v_a0, v_a_up=v_a_up, v_g_up=v_g_up, v_k_k=v_k_k, v_k_a=v_k_a, v_r_k=v_r_k, v_gn_w=v_gn_w, v_gn_b=v_gn_b, v_q_norm=v_q_norm, v_w_uq=v_w_uq, v_kv_norm=v_kv_norm, v_w_ukv=v_w_ukv, v_w_out=v_w_out, v_ffn2_norm=v_ffn2_norm, v_ffn2_w_gate=v_ffn2_w_gate, v_ffn2_w_up=v_ffn2_w_up, v_ffn2_w_down=v_ffn2_w_down, v_final_norm=v_final_norm)
    weights = {n: given[n] for n in TWIN_WEIGHTS}
    shared = {n: given[n] for n in SHARED_INPUTS}
    per_example = {n: given[n] for n in ['x']}
    grad_fn = _jax.value_and_grad(_loss, argnums=(0, 1))

    def one_microbatch(ex, loss_target):
        ex = dict(ex)
        diff = ex.pop(TWIN_DIFF_INPUT)
        return grad_fn(weights, diff, {**shared, **ex}, loss_target)

    if N_MICROBATCH == 1:
        loss, (grad_w, grad_x) = one_microbatch(per_example, given["loss_target"])
    else:
        def body(carry, xs):
            loss_sum, grad_sum = carry
            l_k, (gw_k, gx_k) = one_microbatch(xs[0], xs[1])
            with _jax.named_scope("update"):
                return (loss_sum + l_k, _jax.tree.map(_jnp.add, grad_sum, gw_k)), gx_k

        init = (_jnp.zeros((), _jnp.float32), _jax.tree.map(_jnp.zeros_like, weights))
        (loss, grad_w), grad_x = _jax.lax.scan(body, init, (per_example, given["loss_target"]))
    with _jax.named_scope("update"):
        delta_w, new_m, new_v = {}, {}, {}
        for n in TWIN_WEIGHTS:
            delta_w[n], new_m[n], new_v[n] = _adamw(weights[n], grad_w[n], given["m_" + n], given["v_" + n])
    return (loss, grad_x, *[grad_w[n] for n in TWIN_WEIGHTS], *[delta_w[n] for n in TWIN_WEIGHTS],
            *[new_m[n] for n in TWIN_WEIGHTS], *[new_v[n] for n in TWIN_WEIGHTS])
```

```python
import functools

import numpy as np
import jax
import jax.numpy as jnp
from jax import lax
from jax.experimental import pallas as pl
from jax.experimental.pallas import tpu as pltpu

F32 = jnp.float32
BF16 = jnp.bfloat16
MMD = BF16

NORM_EPS = 1e-6
RWKV_HEAD = 64
GN_EPS = RWKV_HEAD * 1e-5
NOPE_DIM = 128
ROPE_DIM = 64
V_DIM = 128
QK_DIM = NOPE_DIM + ROPE_DIM
ROPE_THETA = 10000.0
ADAM_LR = 0.001
ADAM_B1 = 0.9
ADAM_B2 = 0.999
ADAM_EPS = 1e-08
ADAM_WD = 0.01
ADAM_STEP = 10

LANES = 128
TCH = 64
N_DEV = 8
VMEM_LIMIT = 56 * 1024 * 1024
MESH = pl.DeviceIdType.MESH


def _tile(n, target, align):
    best = None
    for d in range(align, min(n, target) + 1, align):
        if n % d == 0:
            best = d
    return best if best is not None else n


def _params(sem=None):
    return pltpu.CompilerParams(dimension_semantics=sem, vmem_limit_bytes=VMEM_LIMIT)


def _mm(a, b, dims=((1,), (0,))):
    return lax.dot_general(a.astype(MMD), b.astype(MMD), (dims, ((), ())), preferred_element_type=F32)


@jax.custom_vjp
def mmdot(a, b):
    return _mm(a, b)


def _mmdot_fwd(a, b):
    return _mm(a, b), (a, b)


def _mmdot_bwd(res, g):
    a, b = res
    return _mm(g, b, ((1,), (1,))).astype(a.dtype), _mm(a, g, ((0,), (0,))).astype(b.dtype)


mmdot.defvjp(_mmdot_fwd, _mmdot_bwd)


def _dot2(x, m):
    hi = x.astype(BF16)
    lo = (x - hi.astype(F32)).astype(BF16)
    return (lax.dot_general(hi, m, (((1,), (0,)), ((), ())), preferred_element_type=F32)
            + lax.dot_general(lo, m, (((1,), (0,)), ((), ())), preferred_element_type=F32))


@jax.custom_vjp
def segsum(x, e, et):
    return _dot2(_dot2(x, e), et)


def _segsum_fwd(x, e, et):
    return segsum(x, e, et), (e, et)


def _segsum_bwd(res, g):
    e, et = res
    return segsum(g, e, et), jnp.zeros_like(e), jnp.zeros_like(et)


segsum.defvjp(_segsum_fwd, _segsum_bwd)


def _sigmoid(x):
    return 1.0 / (1.0 + jnp.exp(-x))


def _softplus(x):
    return jnp.maximum(x, 0.0) + jnp.log(1.0 + jnp.exp(-jnp.abs(x)))


def _rms(x, g):
    return x * lax.rsqrt(jnp.mean(x * x, axis=-1, keepdims=True) + NORM_EPS) * g


_DIMS = {"nn": ((1,), (0,)), "nt": ((1,), (1,)), "tn": ((0,), (0,))}


def matmul(pairs, mode, *, name, out_dtype=F32, res=None, alpha=1.0, tm=1088, tn=512, tk=1024):
    a0, b0 = pairs[0]
    if mode == "nn":
        (m, k), n = a0.shape, b0.shape[1]
    elif mode == "nt":
        (m, k), n = a0.shape, b0.shape[0]
    else:
        (k, m), n = a0.shape, b0.shape[1]
    tm = _tile(m, tm, 128 if mode == "tn" else 16)
    tn = _tile(n, tn, 128)
    tk = _tile(k, tk, 16 if mode == "tn" else 128)
    nk = k // tk
    npair = len(pairs)
    if mode == "tn":
        a_spec = pl.BlockSpec((tk, tm), lambda i, j, kk: (kk, i))
    else:
        a_spec = pl.BlockSpec((tm, tk), lambda i, j, kk: (i, kk))
    if mode == "nt":
        b_spec = pl.BlockSpec((tn, tk), lambda i, j, kk: (j, kk))
    else:
        b_spec = pl.BlockSpec((tk, tn), lambda i, j, kk: (kk, j))
    o_spec = pl.BlockSpec((tm, tn), lambda i, j, kk: (i, j))
    dims = _DIMS[mode]

    def body(*refs):
        ab = refs[:2 * npair]
        res_ref = refs[2 * npair] if res is not None else None
        o_ref, acc_ref = refs[-2], refs[-1]
        kk = pl.program_id(2)

        @pl.when(kk == 0)
        def _():
            acc_ref[...] = jnp.zeros_like(acc_ref)

        part = _mm(ab[0][...], ab[1][...], dims)
        for p in range(1, npair):
            part = part + _mm(ab[2 * p][...], ab[2 * p + 1][...], dims)
        acc_ref[...] += part

        @pl.when(kk == nk - 1)
        def _():
            out = acc_ref[...] * alpha if alpha != 1.0 else acc_ref[...]
            if res_ref is not None:
                out = res_ref[...].astype(F32) + out
            o_ref[...] = out.astype(o_ref.dtype)

    args, specs = [], []
    for a, b in pairs:
        args += [a, b]
        specs += [a_spec, b_spec]
    if res is not None:
        args.append(res)
        specs.append(o_spec)
    return pl.pallas_call(
        body, grid=(m // tm, n // tn, nk), in_specs=specs, out_specs=o_spec,
        out_shape=jax.ShapeDtypeStruct((m, n), out_dtype), scratch_shapes=[pltpu.VMEM((tm, tn), F32)],
        compiler_params=_params(("parallel", "parallel", "arbitrary")), name=name)(*args)


def tilek(fn, ins, outs, *, n_rows, tr, name):
    tr = _tile(n_rows, tr, 16)
    n_in = len(ins)
    in_specs = []
    for arr, kind in ins:
        if kind == "r":
            in_specs.append(pl.BlockSpec((tr, arr.shape[1]), lambda i: (i, 0)))
        else:
            in_specs.append(pl.BlockSpec(arr.shape, lambda i, nd=arr.ndim: (0,) * nd))
    out_specs, out_shape = [], []
    has_acc = False
    for o in outs:
        if o[0] == "r":
            out_specs.append(pl.BlockSpec((tr, o[1]), lambda i: (i, 0)))
            out_shape.append(jax.ShapeDtypeStruct((n_rows, o[1]), o[2]))
        else:
            has_acc = True
            out_specs.append(pl.BlockSpec(o[1], lambda i, nd=len(o[1]): (0,) * nd))
            out_shape.append(jax.ShapeDtypeStruct(o[1], F32))

    def body(*refs):
        i = pl.program_id(0)
        vals = fn(*[r[...] for r in refs[:n_in]])
        for o, r, v in zip(outs, refs[n_in:], vals):
            if o[0] == "r":
                r[...] = v.astype(r.dtype)
            else:
                @pl.when(i == 0)
                def _(r=r):
                    r[...] = jnp.zeros_like(r)

                r[...] += v

    return pl.pallas_call(
        body, grid=(n_rows // tr,), in_specs=in_specs, out_specs=out_specs, out_shape=out_shape,
        compiler_params=_params(("arbitrary",) if has_acc else ("parallel",)), name=name)(*[a for a, _ in ins])


def rms_fwd(x, g, name):
    n, d = x.shape
    return tilek(lambda xv, gv: (_rms(xv, gv),), [(x, "r"), (g, "f")], [("r", d, MMD)], n_rows=n, tr=256, name=name)[0]


def rms_bwd(x, g, dy, dres, name):
    n, d = x.shape

    def fn(xv, gv, dyv, drv):
        _, vjp = jax.vjp(_rms, xv, gv)
        dx, dg = vjp(dyv.astype(F32))
        return drv + dx, dg

    return tilek(fn, [(x, "r"), (g, "f"), (dy, "r"), (dres, "r")], [("r", d, F32), ("acc", (1, d))],
                 n_rows=n, tr=128, name=name)


def loss_head(h, tgt, mask, g, name):
    n, d = h.shape

    def fn(hv, tv, mv, gv):
        def lossf(hh, gg):
            e = (_rms(hh, gg) - tv) * mv
            s = jnp.sum(jnp.sum(e * e, axis=1, keepdims=True), axis=0, keepdims=True)
            return s * (0.5 / d)

        l, vjp = jax.vjp(lossf, hv, gv)
        dh, dg = vjp(jnp.ones((1, 1), F32))
        return dh, dg, jnp.broadcast_to(l, (1, LANES))

    return tilek(fn, [(h, "r"), (tgt, "r"), (mask, "r"), (g, "f")],
                 [("r", d, F32), ("acc", (1, d)), ("acc", (1, LANES))], n_rows=n, tr=128, name=name)


def ffn_up(hn, wg, wu, name):
    n, d = hn.shape
    f = wg.shape[1]
    tm, tn = _tile(n, 544, 16), _tile(f, 512, 128)

    def body(a_ref, g_ref, u_ref, og_ref, ou_ref, oa_ref):
        a = a_ref[...]
        g = _mm(a, g_ref[...])
        u = _mm(a, u_ref[...])
        og_ref[...] = g
        ou_ref[...] = u
        oa_ref[...] = (g * _sigmoid(g) * u).astype(oa_ref.dtype)

    o_spec = pl.BlockSpec((tm, tn), lambda i, j: (i, j))
    w_spec = pl.BlockSpec((d, tn), lambda i, j: (0, j))
    return pl.pallas_call(
        body, grid=(n // tm, f // tn), in_specs=[pl.BlockSpec((tm, d), lambda i, j: (i, 0)), w_spec, w_spec],
        out_specs=[o_spec, o_spec, o_spec],
        out_shape=[jax.ShapeDtypeStruct((n, f), F32), jax.ShapeDtypeStruct((n, f), F32), jax.ShapeDtypeStruct((n, f), MMD)],
        compiler_params=_params(("parallel", "parallel")), name=name)(hn, wg, wu)


def ffn_down_bwd(dh, wd, gate, up, name):
    n, d = dh.shape
    f = wd.shape[0]
    tm, tn = _tile(n, 544, 16), _tile(f, 512, 128)

    def body(dh_ref, w_ref, g_ref, u_ref, dg_ref, du_ref):
        da = 0.5 * _mm(dh_ref[...], w_ref[...], ((1,), (1,)))
        g, u = g_ref[...], u_ref[...]
        s = _sigmoid(g)
        dg_ref[...] = (da * u * (s * (1.0 + g * (1.0 - s)))).astype(dg_ref.dtype)
        du_ref[...] = (da * (g * s)).astype(du_ref.dtype)

    o_spec = pl.BlockSpec((tm, tn), lambda i, j: (i, j))
    return pl.pallas_call(
        body, grid=(n // tm, f // tn),
        in_specs=[pl.BlockSpec((tm, d), lambda i, j: (i, 0)), pl.BlockSpec((tn, d), lambda i, j: (j, 0)), o_spec, o_spec],
        out_specs=[o_spec, o_spec],
        out_shape=[jax.ShapeDtypeStruct((n, f), MMD), jax.ShapeDtypeStruct((n, f), MMD)],
        compiler_params=_params(("parallel", "parallel")), name=name)(dh, wd, gate, up)


def ffn_forward(h, g, wg, wu, wd, tag):
    hn = rms_fwd(h, g, f"{tag}_rms")
    gate, up, act = ffn_up(hn, wg, wu, f"{tag}_up")
    out = matmul([(act, wd)], "nn", res=h, alpha=0.5, name=f"{tag}_down")
    return out, (hn, gate, up, act)


def ffn_backward(dout, h, g, wg, wu, wd, saved, tag):
    hn, gate, up, act = saved
    dgate, dup = ffn_down_bwd(dout, wd, gate, up, f"{tag}_dact")
    dwd = matmul([(act, dout)], "tn", alpha=0.5, name=f"{tag}_dwd")
    dwg = matmul([(hn, dgate)], "tn", name=f"{tag}_dwg")
    dwu = matmul([(hn, dup)], "tn", name=f"{tag}_dwu")
    dhn = matmul([(dgate, wg), (dup, wu)], "nt", name=f"{tag}_dhn")
    dh, dg = rms_bwd(h, g, dhn, dout, f"{tag}_drms")
    return dh, dg, dwg, dwu, dwd


def lerp_fwd(p, mu, bl, t, name):
    n, w = p.shape
    cb = _tile(w, 256, 128)

    def body(p_ref, mu_ref, o_ref):
        x = p_ref[...]
        row = lax.broadcasted_iota(jnp.int32, x.shape, 0)
        prev = jnp.where(row == 0, 0.0, pltpu.roll(x, 1, 0))
        o_ref[...] = x + mu_ref[...] * (prev - x)

    spec = pl.BlockSpec((t, cb), lambda b, j: (b, j))
    return pl.pallas_call(
        body, grid=(bl, w // cb), in_specs=[spec, pl.BlockSpec((1, cb), lambda b, j: (0, j))], out_specs=spec,
        out_shape=jax.ShapeDtypeStruct((n, w), F32), compiler_params=_params(("parallel", "parallel")), name=name)(p, mu)


def lerp_bwd(p, mu, douts, bl, t, name):
    n, w = p.shape
    cb = _tile(w, 256, 128)
    nd = len(douts)

    def body(*refs):
        p_ref, mu_ref = refs[0], refs[1]
        dp_ref, dmu_ref = refs[2 + nd], refs[3 + nd]
        b = pl.program_id(1)
        x, m = p_ref[...], mu_ref[...]
        d = refs[2][...]
        for r in refs[3:2 + nd]:
            d = d + r[...]
        row = lax.broadcasted_iota(jnp.int32, x.shape, 0)
        prev = jnp.where(row == 0, 0.0, pltpu.roll(x, 1, 0))
        z = d * m
        nxt = jnp.where(row == t - 1, 0.0, pltpu.roll(z, t - 1, 0))
        dp_ref[...] = d - z + nxt

        @pl.when(b == 0)
        def _():
            dmu_ref[...] = jnp.zeros_like(dmu_ref)

        dmu_ref[...] += jnp.sum(d * (prev - x), axis=0, keepdims=True)

    spec = pl.BlockSpec((t, cb), lambda j, b: (b, j))
    cspec = pl.BlockSpec((1, cb), lambda j, b: (0, j))
    return pl.pallas_call(
        body, grid=(w // cb, bl), in_specs=[spec, cspec] + [spec] * nd, out_specs=[spec, cspec],
        out_shape=[jax.ShapeDtypeStruct((n, w), F32), jax.ShapeDtypeStruct((1, w), F32)],
        compiler_params=_params(("parallel", "arbitrary")), name=name)(p, mu, *douts)


def _prep(k, xw, xa, xg, w0, a0, k_k, k_a, w_up, a_up, g_up, e, et):
    w_pre = -_softplus(-(w0 + mmdot(jnp.tanh(xw), w_up))) - 0.5
    decay = jnp.exp(-jnp.exp(w_pre))
    a = _sigmoid(a0 + mmdot(xa, a_up))
    g = mmdot(_sigmoid(xg), g_up)
    kk = k * k_k
    kk = kk * lax.rsqrt(jnp.maximum(segsum(kk * kk, e, et), 1e-24))
    kmod = k * (1.0 + (a - 1.0) * k_a)
    return decay, kmod, -kk, kk * a, g


def _lora_parts(xl):
    return xl[:, :LANES], xl[:, LANES:2 * LANES], xl[:, 2 * LANES:]


def rwkv_prep_fwd(pk, pl_, prm, e, et, name):
    n, d = pk.shape
    small = [prm[k] for k in ("w0", "a0", "k_k", "k_a", "w_up", "a_up", "g_up")]
    ins = [(pk, "r"), (pl_, "r")] + [(s, "f") for s in small] + [(e, "f"), (et, "f")]
    return tilek(lambda k, xl, *rest: _prep(k, *_lora_parts(xl), *rest), ins, [("r", d, F32)] * 5, n_rows=n, tr=128, name=name)


def rwkv_prep_bwd(pk, pl_, prm, e, et, cts, name):
    n, d = pk.shape
    small = [prm[k] for k in ("w0", "a0", "k_k", "k_a", "w_up", "a_up", "g_up")]

    def fn(k, xl, w0, a0, k_k, k_a, w_up, a_up, g_up, ev, etv, dw, dkm1, dkm2, dkn, db, dg):
        _, vjp = jax.vjp(lambda *a: _prep(*a, ev, etv), k, *_lora_parts(xl), w0, a0, k_k, k_a, w_up, a_up, g_up)
        dk, dxw, dxa, dxg, *dsmall = vjp((dw, dkm1 + dkm2, dkn, db, dg))
        return (dk, jnp.concatenate([dxw, dxa, dxg], axis=1), *dsmall)

    ins = [(pk, "r"), (pl_, "r")] + [(s, "f") for s in small] + [(e, "f"), (et, "f")] + [(c, "r") for c in cts]
    outs = [("r", d, F32), ("r", pl_.shape[1], F32)] + [("acc", s.shape) for s in small]
    return tilek(fn, ins, outs, n_rows=n, tr=64, name=name)


def _post(y, r, km, v, g, pga, pgb, yb, gn_w, gn_b, r_k, e, et):
    inv = 1.0 / RWKV_HEAD
    yc = y - segsum(y, e, et) * inv
    var = segsum(yc * yc, e, et) * inv
    yn = yc * lax.rsqrt(var + GN_EPS) * gn_w + gn_b
    bonus = segsum(r * km * r_k, e, et) * v
    ya = (yn + bonus) * g
    return _sigmoid(pga) * ya + _sigmoid(pgb) * yb


def rwkv_post_fwd(acts, prm, e, et, name):
    n, d = acts[0].shape
    small = [prm[k] for k in ("gn_w", "gn_b", "r_k")]
    ins = [(a, "r") for a in acts] + [(s, "f") for s in small] + [(e, "f"), (et, "f")]
    return tilek(lambda *a: (_post(*a),), ins, [("r", d, MMD)], n_rows=n, tr=128, name=name)[0]


def rwkv_post_bwd(acts, prm, e, et, dm, name):
    n, d = acts[0].shape
    small = [prm[k] for k in ("gn_w", "gn_b", "r_k")]
    na = len(acts)

    def fn(*a):
        prim, ev, etv, dmv = a[:na + 3], a[na + 3], a[na + 4], a[na + 5]
        _, vjp = jax.vjp(lambda *z: _post(*z, ev, etv), *prim)
        return vjp(dmv.astype(F32))

    ins = [(x, "r") for x in acts] + [(s, "f") for s in small] + [(e, "f"), (et, "f"), (dm, "r")]
    outs = [("r", d, F32)] * na + [("acc", s.shape) for s in small]
    return tilek(fn, ins, outs, n_rows=n, tr=64, name=name)


def to_col(a, bl, t, d):
    p, c = d // LANES, t // TCH
    a = a.reshape(bl, c, TCH, p, 2, RWKV_HEAD).transpose(0, 3, 1, 5, 4, 2)
    return a.reshape(bl, p, c, RWKV_HEAD, LANES)


def to_row(a, bl, t, d):
    p, c = d // LANES, t // TCH
    a = a.reshape(bl, p, c, RWKV_HEAD, 2, TCH).transpose(0, 2, 5, 1, 4, 3)
    return a.reshape(bl * t, d)


def _pair_group(d):
    return min(4, d // LANES)


def scan_fwd(r, w, k, kn, b, v_col, ones_blk, bl, t, d, name):
    npair, nch = d // LANES, t // TCH
    pg = _pair_group(d)

    def body(r_ref, w_ref, k_ref, kn_ref, b_ref, v_ref, bm_ref, y_ref, hist_ref, s_ref):
        @pl.when(pl.program_id(2) == 0)
        def _():
            s_ref[...] = jnp.zeros_like(s_ref)

        y_ref[...] = jnp.zeros_like(y_ref)
        bm = bm_ref[...]
        lane = lax.broadcasted_iota(jnp.int32, (1, LANES), 1) & (TCH - 1)

        def step(ts, carry):
            sel = (lane == ts).astype(F32)
            for p in range(pg):
                cols = slice(p * LANES, (p + 1) * LANES)
                row = lambda ref: ref[ts, :, cols]
                s = s_ref[p]
                hist_ref[0, p, pl.ds(ts, 1)] = s[None]
                sa = _dot2(s * row(kn_ref), bm)
                vb = _dot2(v_ref[0, p, 0] * sel, bm)
                s = s * row(w_ref) + sa * row(b_ref) + vb * row(k_ref)
                s_ref[p] = s
                yb = _dot2(s * row(r_ref), bm)
                y_ref[0, p, 0] += yb * sel
            return carry

        lax.fori_loop(0, TCH, step, 0)

    row_spec = pl.BlockSpec((TCH, 1, pg * LANES), lambda bb, g, c: (bb * nch + c, 0, g))
    col_spec = pl.BlockSpec((1, pg, 1, RWKV_HEAD, LANES), lambda bb, g, c: (bb, g, c, 0, 0))
    hist_spec = pl.BlockSpec((1, pg, TCH, RWKV_HEAD, LANES), lambda bb, g, c: (bb, g, c, 0, 0))
    rows3 = [a.reshape(bl * t, 1, d) for a in (r, w, k, kn, b)]
    return pl.pallas_call(
        body, grid=(bl, npair // pg, nch),
        in_specs=[row_spec] * 5 + [col_spec, pl.BlockSpec((LANES, LANES), lambda bb, g, c: (0, 0))],
        out_specs=[col_spec, hist_spec],
        out_shape=[jax.ShapeDtypeStruct((bl, npair, nch, RWKV_HEAD, LANES), F32),
                   jax.ShapeDtypeStruct((bl, npair, t, RWKV_HEAD, LANES), F32)],
        scratch_shapes=[pltpu.VMEM((pg, RWKV_HEAD, LANES), F32)],
        compiler_params=_params(("parallel", "parallel", "arbitrary")), name=name)(*rows3, v_col, ones_blk)


def scan_bwd(r, w, k, kn, b, v_col, dy_col, hist, ones_blk, bl, t, d, name):
    npair, nch = d // LANES, t // TCH
    pg = _pair_group(d)

    def body(r_ref, w_ref, k_ref, kn_ref, b_ref, v_ref, dy_ref, hist_ref, bm_ref,
             dr_ref, dw_ref, dk_ref, dkn_ref, db_ref, dv_ref, ds_ref):
        @pl.when(pl.program_id(2) == 0)
        def _():
            ds_ref[...] = jnp.zeros_like(ds_ref)

        dv_ref[...] = jnp.zeros_like(dv_ref)
        bm = bm_ref[...]
        lane = lax.broadcasted_iota(jnp.int32, (1, LANES), 1) & (TCH - 1)
        colsum = lambda x: jnp.sum(x, axis=0, keepdims=True)

        def step(it, carry):
            ts = TCH - 1 - it
            sel = (lane == ts).astype(F32)
            for p in range(pg):
                cols = slice(p * LANES, (p + 1) * LANES)
                row = lambda ref: ref[ts, :, cols]
                r_, w_, k_, kn_, b_ = row(r_ref), row(w_ref), row(k_ref), row(kn_ref), row(b_ref)
                s_prev = hist_ref[0, p, pl.ds(ts, 1)][0]
                vb = _dot2(v_ref[0, p, 0] * sel, bm)
                dyb = _dot2(dy_ref[0, p, 0] * sel, bm)
                sa = _dot2(s_prev * kn_, bm)
                s_t = s_prev * w_ + sa * b_ + vb * k_
                ds = ds_ref[p] + dyb * r_
                dr_ref[ts, :, cols] = colsum(s_t * dyb)
                dk_ref[ts, :, cols] = colsum(ds * vb)
                db_ref[ts, :, cols] = colsum(ds * sa)
                dw_ref[ts, :, cols] = colsum(ds * s_prev)
                dvb = _dot2(ds * k_, bm)
                dv_ref[0, p, 0] += dvb * sel
                dsa = _dot2(ds * b_, bm)
                dkn_ref[ts, :, cols] = colsum(s_prev * dsa)
                ds_ref[p] = ds * w_ + dsa * kn_
            return carry

        lax.fori_loop(0, TCH, step, 0)

    row_spec = pl.BlockSpec((TCH, 1, pg * LANES), lambda bb, g, c: (bb * nch + nch - 1 - c, 0, g))
    col_spec = pl.BlockSpec((1, pg, 1, RWKV_HEAD, LANES), lambda bb, g, c: (bb, g, nch - 1 - c, 0, 0))
    hist_spec = pl.BlockSpec((1, pg, TCH, RWKV_HEAD, LANES), lambda bb, g, c: (bb, g, nch - 1 - c, 0, 0))
    row_shape = jax.ShapeDtypeStruct((bl * t, 1, d), F32)
    rows3 = [a.reshape(bl * t, 1, d) for a in (r, w, k, kn, b)]
    outs = pl.pallas_call(
        body, grid=(bl, npair // pg, nch),
        in_specs=[row_spec] * 5 + [col_spec, col_spec, hist_spec, pl.BlockSpec((LANES, LANES), lambda bb, g, c: (0, 0))],
        out_specs=[row_spec] * 5 + [col_spec],
        out_shape=[row_shape] * 5 + [jax.ShapeDtypeStruct((bl, npair, nch, RWKV_HEAD, LANES), F32)],
        scratch_shapes=[pltpu.VMEM((pg, RWKV_HEAD, LANES), F32)],
        compiler_params=_params(("parallel", "parallel", "arbitrary")), name=name)(
            *rows3, v_col, dy_col, hist, ones_blk)
    return [o.reshape(bl * t, d) for o in outs[:5]] + [outs[5]]


def _mla_norms(pm, gq, gkv):
    ql = gq.shape[1]
    kvl = gkv.shape[1]
    return _rms(pm[:, :ql], gq), _rms(pm[:, ql:ql + kvl], gkv)


def mla_prep_fwd(pm, gq, gkv, name):
    n = pm.shape[0]
    return tilek(_mla_norms, [(pm, "r"), (gq, "f"), (gkv, "f")],
                 [("r", gq.shape[1], MMD), ("r", gkv.shape[1], MMD)], n_rows=n, tr=256, name=name)


def mla_prep_bwd(pm, gq, gkv, dcq, dckv, dkpe, name):
    n, wm = pm.shape
    ql, kvl = gq.shape[1], gkv.shape[1]

    def fn(pmv, gqv, gkvv, d1, d2, d3):
        _, vjp1 = jax.vjp(_rms, pmv[:, :ql], gqv)
        _, vjp2 = jax.vjp(_rms, pmv[:, ql:ql + kvl], gkvv)
        dcq_in, dgq = vjp1(d1)
        dckv_in, dgkv = vjp2(d2)
        return jnp.concatenate([dcq_in, dckv_in, d3], axis=1), dgq, dgkv

    return tilek(fn, [(pm, "r"), (gq, "f"), (gkv, "f"), (dcq, "r"), (dckv, "r"), (dkpe, "r")],
                 [("r", wm, F32), ("acc", gq.shape), ("acc", gkv.shape)], n_rows=n, tr=128, name=name)


def _rope(x, c, s, first):
    sw = jnp.where(first, pltpu.roll(x, LANES - ROPE_DIM // 2, 1), pltpu.roll(x, ROPE_DIM // 2, 1))
    return x * c + sw * s


def _unrope(d, c, s, first):
    z = d * s
    sw = jnp.where(first, pltpu.roll(z, LANES - ROPE_DIM // 2, 1), pltpu.roll(z, ROPE_DIM // 2, 1))
    return d * c + sw


def attn_fwd(q, kv, pm, ct, st, bl, t, hm, name):
    n = q.shape[0]
    tq = LANES
    scale = QK_DIM ** -0.5
    kpe_blk = pm.shape[1] // LANES - 1

    def body(qn_ref, qpe_ref, kn_ref, v_ref, kpe_ref, ct_ref, st_ref, o_ref, lse_ref, kp_s, kn_s, v_s):
        h = pl.program_id(1)
        lane = lax.broadcasted_iota(jnp.int32, (1, LANES), 1)
        first = (lane & (ROPE_DIM - 1)) < ROPE_DIM // 2
        kp = _rope(kpe_ref[...], ct_ref[...], st_ref[...], first)
        kp_s[...] = jnp.where(h % 2 == 0, kp, pltpu.roll(kp, ROPE_DIM, 1)).astype(MMD)
        kn_s[...] = kn_ref[...].astype(MMD)
        v_s[...] = v_ref[...].astype(MMD)
        kpos = lax.broadcasted_iota(jnp.int32, (1, t), 1)

        def qtile(i, carry):
            rows = pl.ds(pl.multiple_of(i * tq, tq), tq)
            q2 = _rope(qpe_ref[rows, :], ct_ref[rows, :], st_ref[rows, :], first)
            s = (_mm(qn_ref[rows, :], kn_s[...], ((1,), (1,))) + _mm(q2, kp_s[...], ((1,), (1,)))) * scale
            qpos = i * tq + lax.broadcasted_iota(jnp.int32, (tq, 1), 0)
            s = jnp.where(kpos <= qpos, s, -1e30)
            m = jnp.max(s, axis=1, keepdims=True)
            p = jnp.exp(s - m)
            l = jnp.sum(p, axis=1, keepdims=True)
            o_ref[rows, :] = _mm(p, v_s[...]) / l
            lse_ref[0, 0, rows, :] = m + jnp.log(l)
            return carry

        lax.fori_loop(0, t // tq, qtile, 0)

    blk = lambda f: pl.BlockSpec((t, LANES), f)
    return pl.pallas_call(
        body, grid=(bl, hm),
        in_specs=[blk(lambda b, h: (b, h)), blk(lambda b, h: (b, hm + h // 2)), blk(lambda b, h: (b, h)),
                  blk(lambda b, h: (b, hm + h)), blk(lambda b, h: (b, kpe_blk)), blk(lambda b, h: (0, 0)), blk(lambda b, h: (0, 0))],
        out_specs=[blk(lambda b, h: (b, h)), pl.BlockSpec((1, 1, t, 1), lambda b, h: (b, h, 0, 0))],
        out_shape=[jax.ShapeDtypeStruct((n, hm * LANES), F32), jax.ShapeDtypeStruct((bl, hm, t, 1), F32)],
        scratch_shapes=[pltpu.VMEM((t, LANES), MMD)] * 3,
        compiler_params=_params(("parallel", "arbitrary")), name=name)(q, q, kv, kv, pm, ct, st)


def attn_bwd(q, kv, pm, o, do, lse, ct, st, bl, t, hm, name):
    n = q.shape[0]
    tq = LANES
    scale = QK_DIM ** -0.5
    kpe_blk = pm.shape[1] // LANES - 1

    def body(qn_ref, qpe_ref, kn_ref, v_ref, kpe_ref, o_ref, do_ref, lse_ref, ct_ref, st_ref,
             dqn_ref, dqpe_ref, dkn_ref, dv_ref, dkpe_ref, kp_s, kn_s, v_s, dkn_s, dkp_s, dv_s):
        h = pl.program_id(1)
        lane = lax.broadcasted_iota(jnp.int32, (1, LANES), 1)
        first = (lane & (ROPE_DIM - 1)) < ROPE_DIM // 2
        mine = (lane // ROPE_DIM) == (h % 2)
        kp = _rope(kpe_ref[...], ct_ref[...], st_ref[...], first)
        kp_s[...] = jnp.where(h % 2 == 0, kp, pltpu.roll(kp, ROPE_DIM, 1)).astype(MMD)
        kn_s[...] = kn_ref[...].astype(MMD)
        v_s[...] = v_ref[...].astype(MMD)
        dkn_s[...] = jnp.zeros_like(dkn_s)
        dkp_s[...] = jnp.zeros_like(dkp_s)
        dv_s[...] = jnp.zeros_like(dv_s)
        kpos = lax.broadcasted_iota(jnp.int32, (1, t), 1)

        @pl.when(h % 2 == 0)
        def _():
            dqpe_ref[...] = jnp.zeros_like(dqpe_ref)

        @pl.when(h == 0)
        def _():
            dkpe_ref[...] = jnp.zeros_like(dkpe_ref)

        def qtile(i, carry):
            rows = pl.ds(pl.multiple_of(i * tq, tq), tq)
            c_i, s_i = ct_ref[rows, :], st_ref[rows, :]
            q1 = qn_ref[rows, :].astype(MMD)
            q2 = _rope(qpe_ref[rows, :], c_i, s_i, first).astype(MMD)
            s = (_mm(q1, kn_s[...], ((1,), (1,))) + _mm(q2, kp_s[...], ((1,), (1,)))) * scale
            qpos = i * tq + lax.broadcasted_iota(jnp.int32, (tq, 1), 0)
            p = jnp.where(kpos <= qpos, jnp.exp(s - lse_ref[0, 0, rows, :]), 0.0)
            do_i = do_ref[rows, :]
            delta = jnp.sum(do_i * o_ref[rows, :], axis=1, keepdims=True)
            dp = _mm(do_i, v_s[...], ((1,), (1,)))
            ds = (p * (dp - delta) * scale).astype(MMD)
            dqn_ref[rows, :] = _mm(ds, kn_s[...])
            dq2 = jnp.where(mine, _mm(ds, kp_s[...]), 0.0)
            dqpe_ref[rows, :] += _unrope(dq2, c_i, s_i, first)
            dkn_s[...] += _mm(ds, q1, ((0,), (0,)))
            dkp_s[...] += _mm(ds, q2, ((0,), (0,)))
            dv_s[...] += _mm(p, do_i, ((0,), (0,)))
            return carry

        lax.fori_loop(0, t // tq, qtile, 0)
        dkn_ref[...] = dkn_s[...]
        dv_ref[...] = dv_s[...]
        dkp = jnp.where(mine, dkp_s[...], 0.0)
        dkp = jnp.where(h % 2 == 0, dkp, pltpu.roll(dkp, ROPE_DIM, 1))
        dkpe_ref[...] += _unrope(dkp, ct_ref[...], st_ref[...], first)

    blk = lambda f: pl.BlockSpec((t, LANES), f)
    hd = lambda b, h: (b, h)
    shp = lambda wd: jax.ShapeDtypeStruct((n, wd), F32)
    return pl.pallas_call(
        body, grid=(bl, hm),
        in_specs=[blk(hd), blk(lambda b, h: (b, hm + h // 2)), blk(hd), blk(lambda b, h: (b, hm + h)),
                  blk(lambda b, h: (b, kpe_blk)), blk(hd), blk(hd), pl.BlockSpec((1, 1, t, 1), lambda b, h: (b, h, 0, 0)),
                  blk(lambda b, h: (0, 0)), blk(lambda b, h: (0, 0))],
        out_specs=[blk(hd), blk(lambda b, h: (b, h // 2)), blk(hd), blk(hd), blk(lambda b, h: (b, 0))],
        out_shape=[shp(hm * LANES), shp(hm * ROPE_DIM), shp(hm * LANES), shp(hm * LANES), shp(LANES)],
        scratch_shapes=[pltpu.VMEM((t, LANES), MMD)] * 3 + [pltpu.VMEM((t, LANES), F32)] * 3,
        compiler_params=_params(("parallel", "arbitrary")), name=name)(q, q, kv, kv, pm, o, do, lse, ct, st)


def _peer(k):
    mx, my, mc = lax.axis_index("x"), lax.axis_index("y"), lax.axis_index("c")
    px = 1 - mx if k & 4 else mx
    py = 1 - my if k & 2 else my
    pc = 1 - mc if k & 1 else mc
    return (px, py, pc), 4 * px + 2 * py + pc


def all_gather(x, name):
    def body(x_ref, o_ref, send_sems, recv_sems, local_sem):
        _, me = _peer(0)
        local = pltpu.make_async_copy(x_ref, o_ref.at[me], local_sem)
        local.start()
        copies = []
        for k in range(1, N_DEV):
            dev, _ = _peer(k)
            cp = pltpu.make_async_remote_copy(src_ref=x_ref, dst_ref=o_ref.at[me], send_sem=send_sems.at[k - 1],
                                              recv_sem=recv_sems.at[k - 1], device_id=dev, device_id_type=MESH)
            cp.start()
            copies.append(cp)
        for cp in copies:
            cp.wait()
        local.wait()

    return pl.pallas_call(
        body, in_specs=[pl.BlockSpec(memory_space=pl.ANY)], out_specs=pl.BlockSpec(memory_space=pl.ANY),
        out_shape=jax.ShapeDtypeStruct((N_DEV,) + x.shape, x.dtype),
        scratch_shapes=[pltpu.SemaphoreType.DMA((N_DEV - 1,)), pltpu.SemaphoreType.DMA((N_DEV - 1,)), pltpu.SemaphoreType.DMA],
        name=name)(x)


def all_to_all(x, name):
    def body(x_ref, o_ref, send_sems, recv_sems, local_sem):
        _, me = _peer(0)
        local = pltpu.make_async_copy(x_ref.at[me], o_ref.at[me], local_sem)
        local.start()
        copies = []
        for k in range(1, N_DEV):
            dev, idx = _peer(k)
            cp = pltpu.make_async_remote_copy(src_ref=x_ref.at[idx], dst_ref=o_ref.at[me], send_sem=send_sems.at[k - 1],
                                              recv_sem=recv_sems.at[k - 1], device_id=dev, device_id_type=MESH)
            cp.start()
            copies.append(cp)
        for cp in copies:
            cp.wait()
        local.wait()

    return pl.pallas_call(
        body, in_specs=[pl.BlockSpec(memory_space=pl.ANY)], out_specs=pl.BlockSpec(memory_space=pl.ANY),
        out_shape=jax.ShapeDtypeStruct(x.shape, x.dtype),
        scratch_shapes=[pltpu.SemaphoreType.DMA((N_DEV - 1,)), pltpu.SemaphoreType.DMA((N_DEV - 1,)), pltpu.SemaphoreType.DMA],
        name=name)(x)


def sum_blocks(x, name):
    _, r, c = x.shape
    tr = _tile(r, 512, 16)

    def body(x_ref, o_ref):
        acc = x_ref[0].astype(F32)
        for i in range(1, N_DEV):
            acc = acc + x_ref[i].astype(F32)
        o_ref[...] = acc

    return pl.pallas_call(
        body, grid=(r // tr,), in_specs=[pl.BlockSpec((N_DEV, tr, c), lambda i: (0, i, 0))],
        out_specs=pl.BlockSpec((tr, c), lambda i: (i, 0)), out_shape=jax.ShapeDtypeStruct((r, c), F32),
        compiler_params=_params(("parallel",)), name=name)(x)


def _adamw(w, g, m, v):
    m = ADAM_B1 * m + (1.0 - ADAM_B1) * g
    v = ADAM_B2 * v + (1.0 - ADAM_B2) * jnp.square(g)
    m_hat = m / (1.0 - ADAM_B1 ** ADAM_STEP)
    v_hat = v / (1.0 - ADAM_B2 ** ADAM_STEP)
    delta = -ADAM_LR * (m_hat / (jnp.sqrt(v_hat) + ADAM_EPS) + ADAM_WD * w)
    return delta, m, v


def adamw(w, g, m, v, name):
    r, c = w.shape
    tr = _tile(r, 256, 8)
    spec = pl.BlockSpec((tr, c), lambda i: (i, 0))

    def body(w_ref, g_ref, m_ref, v_ref, d_ref, nm_ref, nv_ref):
        d_ref[...], nm_ref[...], nv_ref[...] = _adamw(w_ref[...], g_ref[...], m_ref[...], v_ref[...])

    return pl.pallas_call(
        body, grid=(r // tr,), in_specs=[spec] * 4, out_specs=[spec] * 3,
        out_shape=[jax.ShapeDtypeStruct((r, c), F32)] * 3, compiler_params=_params(("parallel",)), name=name)(w, g, m, v)


def batch_sum_rows(dh, bl, t, rows, name):
    d = dh.shape[1]

    def body(x_ref, o_ref):
        @pl.when(pl.program_id(0) == 0)
        def _():
            o_ref[...] = jnp.zeros_like(o_ref)

        o_ref[...] += x_ref[...]

    return pl.pallas_call(
        body, grid=(bl,), in_specs=[pl.BlockSpec((rows, d), lambda b: (b * (t // rows), 0))],
        out_specs=pl.BlockSpec((rows, d), lambda b: (0, 0)), out_shape=jax.ShapeDtypeStruct((rows, d), F32),
        compiler_params=_params(("arbitrary",)), name=name)(dh)


class Dims:
    def __init__(self, x, meta_full_cols, w_up, g_up, q_norm, kv_norm, d_ff):
        self.bl, self.seq, self.d = x.shape
        self.n_meta = 16
        self.t_real = self.n_meta + self.seq
        self.t = -(-self.t_real // LANES) * LANES
        self.n = self.bl * self.t
        self.f = d_ff
        self.wl, self.gl = w_up.shape[-2], g_up.shape[-2]
        self.ql, self.kvl = q_norm.shape[-1], kv_norm.shape[-1]
        self.hm = self.d // V_DIM
        self.in_cols = 5 * self.d + 2 * self.wl + self.gl + self.ql + self.kvl + ROPE_DIM


def _pad_cols(a, width):
    return jnp.pad(a, ((0, 0), (0, width - a.shape[1])))


def _pad_rows(a, rows):
    return jnp.pad(a, ((0, rows - a.shape[0]), (0, 0)))


def split_in(a, dm):
    d, wl, gl, ql, kvl = dm.d, dm.wl, dm.gl, dm.ql, dm.kvl
    o = 3 * d
    lora = jnp.concatenate([_pad_cols(a[:, o:o + wl], LANES), _pad_cols(a[:, o + wl:o + 2 * wl], LANES),
                            a[:, o + 2 * wl:o + 2 * wl + gl]], axis=1)
    o += 2 * wl + gl
    mla = _pad_cols(a[:, o:o + ql + kvl + ROPE_DIM], ql + kvl + LANES)
    o += ql + kvl + ROPE_DIM
    return dict(r=a[:, :d], k=a[:, d:2 * d], v=a[:, 2 * d:3 * d], l=lora, m=mla, ga=a[:, o:o + d], gb=a[:, o + d:o + 2 * d])


def merge_in(g, dm):
    wl, gl, ql, kvl = dm.wl, dm.gl, dm.ql, dm.kvl
    l, m = g["l"], g["m"]
    return jnp.concatenate([g["r"], g["k"], g["v"], l[:, :wl], l[:, LANES:LANES + wl], l[:, 2 * LANES:2 * LANES + gl],
                            m[:, :ql + kvl + ROPE_DIM], g["ga"], g["gb"]], axis=1)


def split_uq(w, dm):
    w3 = w.reshape(w.shape[0], dm.hm, QK_DIM)
    return jnp.concatenate([w3[:, :, :NOPE_DIM].reshape(w.shape[0], -1), w3[:, :, NOPE_DIM:].reshape(w.shape[0], -1)], axis=1)


def merge_uq(gn, gp, dm):
    r = gn.shape[0]
    return jnp.concatenate([gn.reshape(r, dm.hm, NOPE_DIM), gp.reshape(r, dm.hm, ROPE_DIM)], axis=2).reshape(r, -1)


def split_ukv(w, dm):
    w3 = w.reshape(w.shape[0], dm.hm, NOPE_DIM + V_DIM)
    return jnp.concatenate([w3[:, :, :NOPE_DIM].reshape(w.shape[0], -1), w3[:, :, NOPE_DIM:].reshape(w.shape[0], -1)], axis=1)


def merge_ukv(gk, gv, dm):
    r = gk.shape[0]
    return jnp.concatenate([gk.reshape(r, dm.hm, NOPE_DIM), gv.reshape(r, dm.hm, V_DIM)], axis=2).reshape(r, -1)


def head_matrices(d):
    heads = d // RWKV_HEAD
    e = (np.arange(d)[:, None] // RWKV_HEAD == np.arange(LANES)[None, :]) & (np.arange(LANES)[None, :] < heads)
    blk = np.arange(LANES)[:, None] // RWKV_HEAD == np.arange(LANES)[None, :] // RWKV_HEAD
    return jnp.asarray(e, BF16), jnp.asarray(e.T, BF16), jnp.asarray(blk, BF16)


def rope_tables(t):
    pos = jnp.arange(t, dtype=F32)
    inv_freq = 1.0 / (ROPE_THETA ** (jnp.arange(0, ROPE_DIM, 2, dtype=F32) / ROPE_DIM))
    ang = pos[:, None] * inv_freq[None, :]
    cos, sin = jnp.cos(ang), jnp.sin(ang)
    return jnp.tile(jnp.concatenate([cos, cos], axis=1), (1, 2)), jnp.tile(jnp.concatenate([-sin, sin], axis=1), (1, 2))


def local_step(dm, x, loss_target, meta, wt, sp):
    bl, t, n, d, hm = dm.bl, dm.t, dm.n, dm.d, dm.hm
    e, et, ones_blk = head_matrices(d)
    ct, st = rope_tables(t)
    padz = jnp.zeros((bl, t - dm.t_real, d), F32)
    h0 = jnp.concatenate([jnp.broadcast_to(meta[None], (bl, dm.n_meta, d)), x, padz], axis=1).reshape(n, d)
    tgt = jnp.concatenate([jnp.zeros((bl, dm.n_meta, d), F32), loss_target, padz], axis=1).reshape(n, d)
    tpos = jnp.arange(t)
    mask = jnp.tile(((tpos >= dm.n_meta) & (tpos < dm.t_real)).astype(F32), bl).reshape(n, 1)

    win = split_in(wt["w_in"], dm)
    mu = split_in(sp["tm_mu"], dm)
    wq, wkv = split_uq(wt["w_uq"], dm), split_ukv(wt["w_ukv"], dm)
    prm = dict(w0=sp["w0"], a0=sp["a0"], k_k=sp["k_k"], k_a=sp["k_a"], gn_w=sp["gn_w"], gn_b=sp["gn_b"], r_k=sp["r_k"],
               w_up=_pad_rows(wt["w_up"], LANES).astype(F32), a_up=_pad_rows(wt["a_up"], LANES).astype(F32),
               g_up=wt["g_up"].astype(F32))

    h1, ffn1 = ffn_forward(h0, sp["ffn1_norm"], wt["ffn1_w_gate"], wt["ffn1_w_up"], wt["ffn1_w_down"], "ffn1")
    u = rms_fwd(h1, sp["mix_norm"], "mix_rms")
    proj = {key: matmul([(u, win[key])], "nn", name=f"proj_{key}") for key in win}
    sh = {key: lerp_fwd(proj[key], mu[key], bl, t, f"shift_{key}") for key in ("r", "k", "v", "l")}
    decay, kmod, kneg, bvec, gate = rwkv_prep_fwd(sh["k"], sh["l"], prm, e, et, "rwkv_prep")
    v_col = to_col(sh["v"], bl, t, d)
    y_col, hist = scan_fwd(sh["r"], decay, kmod, kneg, bvec, v_col, ones_blk, bl, t, d, "wkv_scan")
    y = to_row(y_col, bl, t, d)
    cqn, ckvn = mla_prep_fwd(proj["m"], sp["q_norm"], sp["kv_norm"], "mla_norms")
    q = matmul([(cqn, wq)], "nn", name="mla_q")
    kv = matmul([(ckvn, wkv)], "nn", name="mla_kv")
    o, lse = attn_fwd(q, kv, proj["m"], ct, st, bl, t, hm, "mla_attn")
    post_in = [y, sh["r"], kmod, sh["v"], gate, proj["ga"], proj["gb"], o]
    mix = rwkv_post_fwd(post_in, prm, e, et, "mix_gate")
    h2 = matmul([(mix, wt["w_out"])], "nn", res=h1, name="out_proj")
    h3, ffn2 = ffn_forward(h2, sp["ffn2_norm"], wt["ffn2_w_gate"], wt["ffn2_w_up"], wt["ffn2_w_down"], "ffn2")
    dh3, d_final, loss = loss_head(h3, tgt, mask, sp["final_norm"], "loss_head")

    gw, gs = {}, {"final_norm": d_final}
    dh2, gs["ffn2_norm"], gw["ffn2_w_gate"], gw["ffn2_w_up"], gw["ffn2_w_down"] = ffn_backward(
        dh3, h2, sp["ffn2_norm"], wt["ffn2_w_gate"], wt["ffn2_w_up"], wt["ffn2_w_down"], ffn2, "ffn2")
    dmix = matmul([(dh2, wt["w_out"])], "nt", name="out_proj_dx")
    gw["w_out"] = matmul([(mix, dh2)], "tn", name="out_proj_dw")
    (dy, dr_p, dkm_p, dv_p, dgate, dpga, dpgb, do, gs["gn_w"], gs["gn_b"], gs["r_k"]) = rwkv_post_bwd(
        post_in, prm, e, et, dmix, "mix_gate_bwd")
    dqn, dqpe, dkn, dv_att, dkpe = attn_bwd(q, kv, proj["m"], o, do, lse, ct, st, bl, t, hm, "mla_attn_bwd")
    nq = hm * NOPE_DIM
    dcqn = matmul([(dqn, wq[:, :nq])], "nt", name="mla_q_dx1")
    dcqn = matmul([(dqpe, wq[:, nq:])], "nt", res=dcqn, name="mla_q_dx2")
    gw["w_uq"] = merge_uq(matmul([(cqn, dqn)], "tn", name="mla_q_dw1"), matmul([(cqn, dqpe)], "tn", name="mla_q_dw2"), dm)
    dckvn = matmul([(dkn, wkv[:, :nq]), (dv_att, wkv[:, nq:])], "nt", name="mla_kv_dx")
    gw["w_ukv"] = merge_ukv(matmul([(ckvn, dkn)], "tn", name="mla_kv_dw1"), matmul([(ckvn, dv_att)], "tn", name="mla_kv_dw2"), dm)
    dproj = {"ga": dpga, "gb": dpgb}
    dproj["m"], gs["q_norm"], gs["kv_norm"] = mla_prep_bwd(proj["m"], sp["q_norm"], sp["kv_norm"], dcqn, dckvn, dkpe, "mla_norms_bwd")
    dy_col = to_col(dy, bl, t, d)
    dr_s, ddecay, dk_s, dkneg, dbvec, dv_col = scan_bwd(sh["r"], decay, kmod, kneg, bvec, v_col, dy_col, hist, ones_blk,
                                                        bl, t, d, "wkv_scan_bwd")
    dv_s = to_row(dv_col, bl, t, d)
    (dsh_k, dsh_l, gs["w0"], gs["a0"], gs["k_k"], gs["k_a"], g_wup, g_aup, gw["g_up"]) = rwkv_prep_bwd(
        sh["k"], sh["l"], prm, e, et, [ddecay, dk_s, dkm_p, dkneg, dbvec, dgate], "rwkv_prep_bwd")
    gw["w_up"], gw["a_up"] = g_wup[:dm.wl], g_aup[:dm.wl]
    dmu = {}
    for key, cts in (("r", [dr_s, dr_p]), ("k", [dsh_k]), ("v", [dv_s, dv_p]), ("l", [dsh_l])):
        dproj[key], dmu[key] = lerp_bwd(proj[key], mu[key], cts, bl, t, f"shift_{key}_bwd")
    zero_m = jnp.zeros((1, proj["m"].shape[1]), F32)
    gs["tm_mu"] = merge_in(dict(dmu, m=zero_m, ga=zero_m[:, :0], gb=zero_m[:, :0]), dm)[:, :3 * d + 2 * dm.wl + dm.gl]
    wide = ("r", "k", "v", "ga", "gb")
    du = matmul([(dproj[key], win[key]) for key in wide], "nt", name="proj_dx", tk=512)
    du = matmul([(dproj["l"], win["l"])], "nt", res=du, name="proj_dx_l")
    du = matmul([(dproj["m"], win["m"])], "nt", res=du, name="proj_dx_m")
    gw["w_in"] = merge_in({key: matmul([(u, dproj[key])], "tn", name=f"proj_dw_{key}") for key in win}, dm)
    dh1, gs["mix_norm"] = rms_bwd(h1, sp["mix_norm"], du, dh2, "mix_rms_bwd")
    dh0, gs["ffn1_norm"], gw["ffn1_w_gate"], gw["ffn1_w_up"], gw["ffn1_w_down"] = ffn_backward(
        dh1, h0, sp["ffn1_norm"], wt["ffn1_w_gate"], wt["ffn1_w_up"], wt["ffn1_w_down"], ffn1, "ffn1")
    grad_x = dh0.reshape(bl, t, d)[:, dm.n_meta:dm.t_real]
    dmeta = batch_sum_rows(dh0, bl, t, dm.n_meta, "meta_grad")
    return loss, grad_x, dmeta, gw, gs


COL_SHARDED = ("ffn1_w_gate", "ffn1_w_up", "w_in", "w_up", "a_up", "g_up", "w_uq", "w_ukv", "ffn2_w_gate", "ffn2_w_up")
ROW_SHARDED = ("ffn1_w_down", "w_out", "ffn2_w_down")
MATRICES = ("ffn1_w_gate", "ffn1_w_up", "ffn1_w_down", "w_in", "w_up", "a_up", "g_up", "w_uq", "w_ukv", "w_out",
            "ffn2_w_gate", "ffn2_w_up", "ffn2_w_down")
SMALL = ("ffn1_norm", "mix_norm", "tm_mu", "w0", "a0", "k_k", "k_a", "r_k", "gn_w", "gn_b", "q_norm", "kv_norm",
         "ffn2_norm", "final_norm")
WEIGHTS = ("meta_tokens", "ffn1_norm", "ffn1_w_gate", "ffn1_w_up", "ffn1_w_down", "mix_norm", "w_in", "tm_mu", "w0", "w_up",
           "a0", "a_up", "g_up", "k_k", "k_a", "r_k", "gn_w", "gn_b", "q_norm", "w_uq", "kv_norm", "w_ukv", "w_out",
           "ffn2_norm", "ffn2_w_gate", "ffn2_w_up", "ffn2_w_down", "final_norm")
PACK_COLS = 1024
PACK_ALIGN = 16 * PACK_COLS


def _pack(parts):
    offs, o = [], 0
    for p in parts:
        offs.append(o)
        o += p.shape[1]
    total = -(-o // PACK_ALIGN) * PACK_ALIGN
    flat = jnp.concatenate(list(parts) + [jnp.zeros((parts[0].shape[0], total - o), parts[0].dtype)], axis=1)
    return flat.reshape(parts[0].shape[0], total // PACK_COLS, PACK_COLS), offs


def kernel(x, meta_tokens, ffn1_norm, ffn1_w_gate, ffn1_w_up, ffn1_w_down, mix_norm, w_in, tm_mu, w0, w_up, a0, a_up, g_up, k_k, k_a, r_k, gn_w, gn_b, q_norm, w_uq, kv_norm, w_ukv, w_out, ffn2_norm, ffn2_w_gate, ffn2_w_up, ffn2_w_down, final_norm, loss_target, m_meta_tokens, m_ffn1_norm, m_ffn1_w_gate, m_ffn1_w_up, m_ffn1_w_down, m_mix_norm, m_w_in, m_tm_mu, m_w0, m_w_up, m_a0, m_a_up, m_g_up, m_k_k, m_k_a, m_r_k, m_gn_w, m_gn_b, m_q_norm, m_w_uq, m_kv_norm, m_w_ukv, m_w_out, m_ffn2_norm, m_ffn2_w_gate, m_ffn2_w_up, m_ffn2_w_down, m_final_norm, v_meta_tokens, v_ffn1_norm, v_ffn1_w_gate, v_ffn1_w_up, v_ffn1_w_down, v_mix_norm, v_w_in, v_tm_mu, v_w0, v_w_up, v_a0, v_a_up, v_g_up, v_k_k, v_k_a, v_r_k, v_gn_w, v_gn_b, v_q_norm, v_w_uq, v_kv_norm, v_w_ukv, v_w_out, v_ffn2_norm, v_ffn2_w_gate, v_ffn2_w_up, v_ffn2_w_down, v_final_norm):
    args = dict(locals())
    wts = {k: args[k] for k in WEIGHTS}
    ms = {k: args["m_" + k] for k in WEIGHTS}
    vs = {k: args["v_" + k] for k in WEIGHTS}
    dm = Dims(x, None, w_up, g_up, q_norm, kv_norm, ffn1_w_down.shape[1] * N_DEV)

    shard2d = {k: wts[k].reshape(wts[k].shape[-2], wts[k].shape[-1]) for k in MATRICES}
    send, offs = _pack([shard2d[k].astype(MMD).reshape(1, -1) for k in MATRICES])
    got = all_gather(send[0], "gather_weights").reshape(N_DEV, -1)
    full = {}
    for k, o in zip(MATRICES, offs):
        r, c = shard2d[k].shape
        blk = got[:, o:o + r * c].reshape(N_DEV, r, c)
        full[k] = blk.transpose(1, 0, 2).reshape(r, N_DEV * c) if k in COL_SHARDED else blk.reshape(N_DEV * r, c)
    mr, mc = meta_tokens.shape
    meta = all_gather(meta_tokens, "gather_meta").transpose(1, 0, 2).reshape(mr, N_DEV * mc)
    small = {k: wts[k].reshape(1, -1) for k in SMALL}

    loss, grad_x, dmeta, gw, gs = local_step(dm, x, loss_target, meta, full, small)

    def blocks(k, g):
        r, c = shard2d[k].shape
        if k in COL_SHARDED:
            return g.reshape(r, N_DEV, c).transpose(1, 0, 2).reshape(N_DEV, r * c)
        return g.reshape(N_DEV, r * c)

    gsend, goffs = _pack([blocks(k, gw[k]).astype(MMD) for k in MATRICES]
                         + [dmeta.reshape(mr, N_DEV, mc).transpose(1, 0, 2).reshape(N_DEV, mr * mc).astype(MMD)])
    gsum = sum_blocks(all_to_all(gsend, "scatter_grads"), "sum_grads").reshape(-1)
    grads = {}
    for k, o in zip(MATRICES, goffs):
        r, c = shard2d[k].shape
        grads[k] = gsum[o:o + r * c].reshape(r, c)
    grads["meta_tokens"] = gsum[goffs[-1]:goffs[-1] + mr * mc].reshape(mr, mc)

    ssend, soffs = _pack([gs[k].reshape(1, -1) for k in SMALL] + [loss])
    ssum = sum_blocks(all_gather(ssend[0], "gather_small"), "sum_small").reshape(-1)
    for k, o in zip(SMALL, soffs):
        grads[k] = ssum[o:o + small[k].shape[1]]
    loss_total = ssum[soffs[-1]]

    delta, new_m, new_v = {}, {}, {}
    for k in MATRICES + ("meta_tokens",):
        shp = wts[k].shape
        to2d = lambda a: a.reshape(shp[-2], shp[-1])
        dlt, nm, nv = adamw(to2d(wts[k]), grads[k], to2d(ms[k]), to2d(vs[k]), f"adamw_{k}")
        delta[k], new_m[k], new_v[k] = dlt.reshape(shp), nm.reshape(shp), nv.reshape(shp)
        grads[k] = grads[k].reshape(shp)
    pw, _ = _pack([wts[k].reshape(1, -1) for k in SMALL])
    pm_, _ = _pack([ms[k].reshape(1, -1) for k in SMALL])
    pv, _ = _pack([vs[k].reshape(1, -1) for k in SMALL])
    pg, poffs = _pack([grads[k].reshape(1, -1) for k in SMALL])
    dlt, nm, nv = adamw(pw[0], pg[0], pm_[0], pv[0], "adamw_small")
    for k, o in zip(SMALL, poffs):
        shp, sz = wts[k].shape, small[k].shape[1]
        cut = lambda a: a.reshape(-1)[o:o + sz].reshape(shp)
        delta[k], new_m[k], new_v[k] = cut(dlt), cut(nm), cut(nv)
        grads[k] = grads[k].reshape(shp)

    return (loss_total, grad_x, *[grads[k] for k in WEIGHTS], *[delta[k] for k in WEIGHTS],
            *[new_m[k] for k in WEIGHTS], *[new_v[k] for k in WEIGHTS])
```

```python
import functools

import numpy as np
import jax
import jax.numpy as jnp
from jax import lax
from jax.experimental import pallas as pl
from jax.experimental.pallas import tpu as pltpu

F32 = jnp.float32
BF16 = jnp.bfloat16
MMD = BF16

NORM_EPS = 1e-6
RWKV_HEAD = 64
GN_EPS = RWKV_HEAD * 1e-5
NOPE_DIM = 128
ROPE_DIM = 64
V_DIM = 128
QK_DIM = NOPE_DIM + ROPE_DIM
ROPE_THETA = 10000.0
ADAM_LR = 0.001
ADAM_B1 = 0.9
ADAM_B2 = 0.999
ADAM_EPS = 1e-08
ADAM_WD = 0.01
ADAM_STEP = 10

LANES = 128
TCH = 64
N_DEV = 8
VMEM_LIMIT = 56 * 1024 * 1024
MESH = pl.DeviceIdType.MESH


def _tile(n, target, align):
    best = None
    for d in range(align, min(n, target) + 1, align):
        if n % d == 0:
            best = d
    return best if best is not None else n


def _params(sem=None):
    return pltpu.CompilerParams(dimension_semantics=sem, vmem_limit_bytes=VMEM_LIMIT)


def _mm(a, b, dims=((1,), (0,))):
    return lax.dot_general(a.astype(MMD), b.astype(MMD), (dims, ((), ())), preferred_element_type=F32)


@jax.custom_vjp
def mmdot(a, b):
    return _mm(a, b)


def _mmdot_fwd(a, b):
    return _mm(a, b), (a, b)


def _mmdot_bwd(res, g):
    a, b = res
    return _mm(g, b, ((1,), (1,))).astype(a.dtype), _mm(a, g, ((0,), (0,))).astype(b.dtype)


mmdot.defvjp(_mmdot_fwd, _mmdot_bwd)


def _dot2(x, m):
    hi = x.astype(BF16)
    lo = (x - hi.astype(F32)).astype(BF16)
    return (lax.dot_general(hi, m, (((1,), (0,)), ((), ())), preferred_element_type=F32)
            + lax.dot_general(lo, m, (((1,), (0,)), ((), ())), preferred_element_type=F32))


@jax.custom_vjp
def segsum(x, e, et):
    return _dot2(_dot2(x, e), et)


def _segsum_fwd(x, e, et):
    return segsum(x, e, et), (e, et)


def _segsum_bwd(res, g):
    e, et = res
    return segsum(g, e, et), jnp.zeros_like(e), jnp.zeros_like(et)


segsum.defvjp(_segsum_fwd, _segsum_bwd)


def _sigmoid(x):
    return 1.0 / (1.0 + jnp.exp(-x))


def _softplus(x):
    return jnp.maximum(x, 0.0) + jnp.log(1.0 + jnp.exp(-jnp.abs(x)))


def _rms(x, g):
    return x * lax.rsqrt(jnp.mean(x * x, axis=-1, keepdims=True) + NORM_EPS) * g


_DIMS = {"nn": ((1,), (0,)), "nt": ((1,), (1,)), "tn": ((0,), (0,))}


def matmul(pairs, mode, *, name, out_dtype=F32, res=None, alpha=1.0, tm=1088, tn=512, tk=1024):
    a0, b0 = pairs[0]
    if mode == "nn":
        (m, k), n = a0.shape, b0.shape[1]
    elif mode == "nt":
        (m, k), n = a0.shape, b0.shape[0]
    else:
        (k, m), n = a0.shape, b0.shape[1]
    tm = _tile(m, tm, 128 if mode == "tn" else 16)
    tn = _tile(n, tn, 128)
    tk = _tile(k, tk, 16 if mode == "tn" else 128)
    nk = k // tk
    npair = len(pairs)
    if mode == "tn":
        a_spec = pl.BlockSpec((tk, tm), lambda i, j, kk: (kk, i))
    else:
        a_spec = pl.BlockSpec((tm, tk), lambda i, j, kk: (i, kk))
    if mode == "nt":
        b_spec = pl.BlockSpec((tn, tk), lambda i, j, kk: (j, kk))
    else:
        b_spec = pl.BlockSpec((tk, tn), lambda i, j, kk: (kk, j))
    o_spec = pl.BlockSpec((tm, tn), lambda i, j, kk: (i, j))
    dims = _DIMS[mode]

    def body(*refs):
        ab = refs[:2 * npair]
        res_ref = refs[2 * npair] if res is not None else None
        o_ref, acc_ref = refs[-2], refs[-1]
        kk = pl.program_id(2)

        @pl.when(kk == 0)
        def _():
            acc_ref[...] = jnp.zeros_like(acc_ref)

        part = _mm(ab[0][...], ab[1][...], dims)
        for p in range(1, npair):
            part = part + _mm(ab[2 * p][...], ab[2 * p + 1][...], dims)
        acc_ref[...] += part

        @pl.when(kk == nk - 1)
        def _():
            out = acc_ref[...] * alpha if alpha != 1.0 else acc_ref[...]
            if res_ref is not None:
                out = res_ref[...].astype(F32) + out
            o_ref[...] = out.astype(o_ref.dtype)

    args, specs = [], []
    for a, b in pairs:
        args += [a, b]
        specs += [a_spec, b_spec]
    if res is not None:
        args.append(res)
        specs.append(o_spec)
    return pl.pallas_call(
        body, grid=(m // tm, n // tn, nk), in_specs=specs, out_specs=o_spec,
        out_shape=jax.ShapeDtypeStruct((m, n), out_dtype), scratch_shapes=[pltpu.VMEM((tm, tn), F32)],
        compiler_params=_params(("parallel", "parallel", "arbitrary")), name=name)(*args)


def tilek(fn, ins, outs, *, n_rows, tr, name):
    tr = _tile(n_rows, tr, 16)
    n_in = len(ins)
    in_specs = []
    for arr, kind in ins:
        if kind == "r":
            in_specs.append(pl.BlockSpec((tr, arr.shape[1]), lambda i: (i, 0)))
        else:
            in_specs.append(pl.BlockSpec(arr.shape, lambda i, nd=arr.ndim: (0,) * nd))
    out_specs, out_shape = [], []
    has_acc = False
    for o in outs:
        if o[0] == "r":
            out_specs.append(pl.BlockSpec((tr, o[1]), lambda i: (i, 0)))
            out_shape.append(jax.ShapeDtypeStruct((n_rows, o[1]), o[2]))
        else:
            has_acc = True
            out_specs.append(pl.BlockSpec(o[1], lambda i, nd=len(o[1]): (0,) * nd))
            out_shape.append(jax.ShapeDtypeStruct(o[1], F32))

    def body(*refs):
        i = pl.program_id(0)
        vals = fn(*[r[...] for r in refs[:n_in]])
        for o, r, v in zip(outs, refs[n_in:], vals):
            if o[0] == "r":
                r[...] = v.astype(r.dtype)
            else:
                @pl.when(i == 0)
                def _(r=r):
                    r[...] = jnp.zeros_like(r)

                r[...] += v

    return pl.pallas_call(
        body, grid=(n_rows // tr,), in_specs=in_specs, out_specs=out_specs, out_shape=out_shape,
        compiler_params=_params(("arbitrary",) if has_acc else ("parallel",)), name=name)(*[a for a, _ in ins])


def rms_fwd(x, g, name):
    n, d = x.shape
    return tilek(lambda xv, gv: (_rms(xv, gv),), [(x, "r"), (g, "f")], [("r", d, MMD)], n_rows=n, tr=256, name=name)[0]


def rms_bwd(x, g, dy, dres, name):
    n, d = x.shape

    def fn(xv, gv, dyv, drv):
        _, vjp = jax.vjp(_rms, xv, gv)
        dx, dg = vjp(dyv.astype(F32))
        return drv + dx, dg

    return tilek(fn, [(x, "r"), (g, "f"), (dy, "r"), (dres, "r")], [("r", d, F32), ("acc", (1, d))],
                 n_rows=n, tr=128, name=name)


def loss_head(h, tgt, mask, g, name):
    n, d = h.shape

    def fn(hv, tv, mv, gv):
        def lossf(hh, gg):
            e = (_rms(hh, gg) - tv) * mv
            s = jnp.sum(jnp.sum(e * e, axis=1, keepdims=True), axis=0, keepdims=True)
            return s * (0.5 / d)

        l, vjp = jax.vjp(lossf, hv, gv)
        dh, dg = vjp(jnp.ones((1, 1), F32))
        return dh, dg, jnp.broadcast_to(l, (1, LANES))

    return tilek(fn, [(h, "r"), (tgt, "r"), (mask, "r"), (g, "f")],
                 [("r", d, F32), ("acc", (1, d)), ("acc", (1, LANES))], n_rows=n, tr=128, name=name)


def ffn_up(hn, wg, wu, name):
    n, d = hn.shape
    f = wg.shape[0]
    tm, tn = _tile(n, 544, 16), _tile(f, 512, 128)

    def body(a_ref, g_ref, u_ref, og_ref, ou_ref, oa_ref):
        a = a_ref[...]
        g = _mm(a, g_ref[...], ((1,), (1,)))
        u = _mm(a, u_ref[...], ((1,), (1,)))
        og_ref[...] = g
        ou_ref[...] = u
        oa_ref[...] = (g * _sigmoid(g) * u).astype(oa_ref.dtype)

    o_spec = pl.BlockSpec((tm, tn), lambda i, j: (i, j))
    w_spec = pl.BlockSpec((tn, d), lambda i, j: (j, 0))
    return pl.pallas_call(
        body, grid=(n // tm, f // tn), in_specs=[pl.BlockSpec((tm, d), lambda i, j: (i, 0)), w_spec, w_spec],
        out_specs=[o_spec, o_spec, o_spec],
        out_shape=[jax.ShapeDtypeStruct((n, f), F32), jax.ShapeDtypeStruct((n, f), F32), jax.ShapeDtypeStruct((n, f), MMD)],
        compiler_params=_params(("parallel", "parallel")), name=name)(hn, wg, wu)


def ffn_down_bwd(dh, wd, gate, up, name):
    n, d = dh.shape
    f = wd.shape[0]
    tm, tn = _tile(n, 544, 16), _tile(f, 512, 128)

    def body(dh_ref, w_ref, g_ref, u_ref, dg_ref, du_ref):
        da = 0.5 * _mm(dh_ref[...], w_ref[...], ((1,), (1,)))
        g, u = g_ref[...], u_ref[...]
        s = _sigmoid(g)
        dg_ref[...] = (da * u * (s * (1.0 + g * (1.0 - s)))).astype(dg_ref.dtype)
        du_ref[...] = (da * (g * s)).astype(du_ref.dtype)

    o_spec = pl.BlockSpec((tm, tn), lambda i, j: (i, j))
    return pl.pallas_call(
        body, grid=(n // tm, f // tn),
        in_specs=[pl.BlockSpec((tm, d), lambda i, j: (i, 0)), pl.BlockSpec((tn, d), lambda i, j: (j, 0)), o_spec, o_spec],
        out_specs=[o_spec, o_spec],
        out_shape=[jax.ShapeDtypeStruct((n, f), MMD), jax.ShapeDtypeStruct((n, f), MMD)],
        compiler_params=_params(("parallel", "parallel")), name=name)(dh, wd, gate, up)


def ffn_forward(h, g, wg, wu, wd, tag):
    hn = rms_fwd(h, g, f"{tag}_rms")
    gate, up, act = ffn_up(hn, wg, wu, f"{tag}_up")
    out = matmul([(act, wd)], "nn", res=h, alpha=0.5, name=f"{tag}_down")
    return out, (hn, gate, up, act)


def ffn_backward(dout, h, g, wg, wu, wd, saved, tag):
    hn, gate, up, act = saved
    dgate, dup = ffn_down_bwd(dout, wd, gate, up, f"{tag}_dact")
    dwd = matmul([(act, dout)], "tn", alpha=0.5, name=f"{tag}_dwd")
    dwg = matmul([(dgate, hn)], "tn", name=f"{tag}_dwg")
    dwu = matmul([(dup, hn)], "tn", name=f"{tag}_dwu")
    dhn = matmul([(dgate, wg), (dup, wu)], "nn", name=f"{tag}_dhn")
    dh, dg = rms_bwd(h, g, dhn, dout, f"{tag}_drms")
    return dh, dg, dwg, dwu, dwd


def lerp_fwd(p, mu, bl, t, name):
    n, w = p.shape
    cb = _tile(w, 256, 128)

    def body(p_ref, mu_ref, o_ref):
        x = p_ref[...]
        row = lax.broadcasted_iota(jnp.int32, x.shape, 0)
        prev = jnp.where(row == 0, 0.0, pltpu.roll(x, 1, 0))
        o_ref[...] = x + mu_ref[...] * (prev - x)

    spec = pl.BlockSpec((t, cb), lambda b, j: (b, j))
    return pl.pallas_call(
        body, grid=(bl, w // cb), in_specs=[spec, pl.BlockSpec((1, cb), lambda b, j: (0, j))], out_specs=spec,
        out_shape=jax.ShapeDtypeStruct((n, w), F32), compiler_params=_params(("parallel", "parallel")), name=name)(p, mu)


def lerp_bwd(p, mu, douts, bl, t, name):
    n, w = p.shape
    cb = _tile(w, 256, 128)
    nd = len(douts)

    def body(*refs):
        p_ref, mu_ref = refs[0], refs[1]
        dp_ref, dmu_ref = refs[2 + nd], refs[3 + nd]
        b = pl.program_id(1)
        x, m = p_ref[...], mu_ref[...]
        d = refs[2][...]
        for r in refs[3:2 + nd]:
            d = d + r[...]
        row = lax.broadcasted_iota(jnp.int32, x.shape, 0)
        prev = jnp.where(row == 0, 0.0, pltpu.roll(x, 1, 0))
        z = d * m
        nxt = jnp.where(row == t - 1, 0.0, pltpu.roll(z, t - 1, 0))
        dp_ref[...] = d - z + nxt

        @pl.when(b == 0)
        def _():
            dmu_ref[...] = jnp.zeros_like(dmu_ref)

        dmu_ref[...] += jnp.sum(d * (prev - x), axis=0, keepdims=True)

    spec = pl.BlockSpec((t, cb), lambda j, b: (b, j))
    cspec = pl.BlockSpec((1, cb), lambda j, b: (0, j))
    return pl.pallas_call(
        body, grid=(w // cb, bl), in_specs=[spec, cspec] + [spec] * nd, out_specs=[spec, cspec],
        out_shape=[jax.ShapeDtypeStruct((n, w), F32), jax.ShapeDtypeStruct((1, w), F32)],
        compiler_params=_params(("parallel", "arbitrary")), name=name)(p, mu, *douts)


def _prep(k, xw, xa, xg, w0, a0, k_k, k_a, w_up, a_up, g_up, e, et):
    w_pre = -_softplus(-(w0 + mmdot(jnp.tanh(xw), w_up))) - 0.5
    decay = jnp.exp(-jnp.exp(w_pre))
    a = _sigmoid(a0 + mmdot(xa, a_up))
    g = mmdot(_sigmoid(xg), g_up)
    kk = k * k_k
    kk = kk * lax.rsqrt(jnp.maximum(segsum(kk * kk, e, et), 1e-24))
    kmod = k * (1.0 + (a - 1.0) * k_a)
    return decay, kmod, -kk, kk * a, g


def _lora_parts(xl):
    return xl[:, :LANES], xl[:, LANES:2 * LANES], xl[:, 2 * LANES:]


def rwkv_prep_fwd(pk, pl_, prm, e, et, name):
    n, d = pk.shape
    small = [prm[k] for k in ("w0", "a0", "k_k", "k_a", "w_up", "a_up", "g_up")]
    ins = [(pk, "r"), (pl_, "r")] + [(s, "f") for s in small] + [(e, "f"), (et, "f")]
    return tilek(lambda k, xl, *rest: _prep(k, *_lora_parts(xl), *rest), ins, [("r", d, F32)] * 5, n_rows=n, tr=128, name=name)


def rwkv_prep_bwd(pk, pl_, prm, e, et, cts, name):
    n, d = pk.shape
    small = [prm[k] for k in ("w0", "a0", "k_k", "k_a", "w_up", "a_up", "g_up")]

    def fn(k, xl, w0, a0, k_k, k_a, w_up, a_up, g_up, ev, etv, dw, dkm1, dkm2, dkn, db, dg):
        _, vjp = jax.vjp(lambda *a: _prep(*a, ev, etv), k, *_lora_parts(xl), w0, a0, k_k, k_a, w_up, a_up, g_up)
        dk, dxw, dxa, dxg, *dsmall = vjp((dw, dkm1 + dkm2, dkn, db, dg))
        return (dk, jnp.concatenate([dxw, dxa, dxg], axis=1), *dsmall)

    ins = [(pk, "r"), (pl_, "r")] + [(s, "f") for s in small] + [(e, "f"), (et, "f")] + [(c, "r") for c in cts]
    outs = [("r", d, F32), ("r", pl_.shape[1], F32)] + [("acc", s.shape) for s in small]
    return tilek(fn, ins, outs, n_rows=n, tr=64, name=name)


def _post(y, r, km, v, g, pga, pgb, yb, gn_w, gn_b, r_k, e, et):
    inv = 1.0 / RWKV_HEAD
    yc = y - segsum(y, e, et) * inv
    var = segsum(yc * yc, e, et) * inv
    yn = yc * lax.rsqrt(var + GN_EPS) * gn_w + gn_b
    bonus = segsum(r * km * r_k, e, et) * v
    ya = (yn + bonus) * g
    return _sigmoid(pga) * ya + _sigmoid(pgb) * yb


def rwkv_post_fwd(acts, prm, e, et, name):
    n, d = acts[0].shape
    small = [prm[k] for k in ("gn_w", "gn_b", "r_k")]
    ins = [(a, "r") for a in acts] + [(s, "f") for s in small] + [(e, "f"), (et, "f")]
    return tilek(lambda *a: (_post(*a),), ins, [("r", d, MMD)], n_rows=n, tr=128, name=name)[0]


def rwkv_post_bwd(acts, prm, e, et, dm, name):
    n, d = acts[0].shape
    small = [prm[k] for k in ("gn_w", "gn_b", "r_k")]
    na = len(acts)

    def fn(*a):
        prim, ev, etv, dmv = a[:na + 3], a[na + 3], a[na + 4], a[na + 5]
        _, vjp = jax.vjp(lambda *z: _post(*z, ev, etv), *prim)
        return vjp(dmv.astype(F32))

    ins = [(x, "r") for x in acts] + [(s, "f") for s in small] + [(e, "f"), (et, "f"), (dm, "r")]
    outs = [("r", d, F32)] * na + [("acc", s.shape) for s in small]
    return tilek(fn, ins, outs, n_rows=n, tr=64, name=name)


def to_col(a, bl, t, d):
    p, c = d // LANES, t // TCH
    a = a.reshape(bl, c, TCH, p, 2, RWKV_HEAD).transpose(0, 3, 1, 5, 4, 2)
    return a.reshape(bl, p, c, RWKV_HEAD, LANES)


def to_row(a, bl, t, d):
    p, c = d // LANES, t // TCH
    a = a.reshape(bl, p, c, RWKV_HEAD, 2, TCH).transpose(0, 2, 5, 1, 4, 3)
    return a.reshape(bl * t, d)


def _pair_group(d):
    return min(4, d // LANES)


def _head_sums(x, first_head):
    a = jnp.sum(jnp.where(first_head, x, 0.0), axis=1, keepdims=True)
    b = jnp.sum(jnp.where(first_head, 0.0, x), axis=1, keepdims=True)
    return jnp.where(first_head, a, b)


def scan_fwd(r, w, k, kn, b, v_col, bl, t, d, name):
    npair, nch = d // LANES, t // TCH
    pg = _pair_group(d)

    def body(r_ref, w_ref, k_ref, kn_ref, b_ref, v_ref, y_ref, hist_ref, s_ref):
        @pl.when(pl.program_id(2) == 0)
        def _():
            s_ref[...] = jnp.zeros_like(s_ref)

        y_ref[...] = jnp.zeros_like(y_ref)
        lane = lax.broadcasted_iota(jnp.int32, (1, LANES), 1)
        first_head = lane < RWKV_HEAD

        def step(ts, carry):
            sel = ((lane & (TCH - 1)) == ts).astype(F32)
            for p in range(pg):
                cols = slice(p * LANES, (p + 1) * LANES)
                row = lambda ref: ref[ts, :, cols]
                s = s_ref[p]
                hist_ref[0, p, pl.ds(ts, 1)] = s[None]
                sa = _head_sums(s * row(kn_ref), first_head)
                vb = _head_sums(v_ref[0, p, 0] * sel, first_head)
                s = s * row(w_ref) + sa * row(b_ref) + vb * row(k_ref)
                s_ref[p] = s
                yb = _head_sums(s * row(r_ref), first_head)
                y_ref[0, p, 0] += yb * sel
            return carry

        lax.fori_loop(0, TCH, step, 0)

    row_spec = pl.BlockSpec((TCH, 1, pg * LANES), lambda bb, g, c: (bb * nch + c, 0, g))
    col_spec = pl.BlockSpec((1, pg, 1, RWKV_HEAD, LANES), lambda bb, g, c: (bb, g, c, 0, 0))
    hist_spec = pl.BlockSpec((1, pg, TCH, RWKV_HEAD, LANES), lambda bb, g, c: (bb, g, c, 0, 0))
    rows3 = [a.reshape(bl * t, 1, d) for a in (r, w, k, kn, b)]
    return pl.pallas_call(
        body, grid=(bl, npair // pg, nch),
        in_specs=[row_spec] * 5 + [col_spec],
        out_specs=[col_spec, hist_spec],
        out_shape=[jax.ShapeDtypeStruct((bl, npair, nch, RWKV_HEAD, LANES), F32),
                   jax.ShapeDtypeStruct((bl, npair, t, RWKV_HEAD, LANES), F32)],
        scratch_shapes=[pltpu.VMEM((pg, RWKV_HEAD, LANES), F32)],
        compiler_params=_params(("parallel", "parallel", "arbitrary")), name=name)(*rows3, v_col)


def scan_bwd(r, w, k, kn, b, v_col, dy_col, hist, bl, t, d, name):
    npair, nch = d // LANES, t // TCH
    pg = _pair_group(d)

    def body(r_ref, w_ref, k_ref, kn_ref, b_ref, v_ref, dy_ref, hist_ref,
             dr_ref, dw_ref, dk_ref, dkn_ref, db_ref, dv_ref, ds_ref):
        @pl.when(pl.program_id(2) == 0)
        def _():
            ds_ref[...] = jnp.zeros_like(ds_ref)

        dv_ref[...] = jnp.zeros_like(dv_ref)
        lane = lax.broadcasted_iota(jnp.int32, (1, LANES), 1)
        first_head = lane < RWKV_HEAD
        colsum = lambda x: jnp.sum(x, axis=0, keepdims=True)

        def step(it, carry):
            ts = TCH - 1 - it
            sel = ((lane & (TCH - 1)) == ts).astype(F32)
            for p in range(pg):
                cols = slice(p * LANES, (p + 1) * LANES)
                row = lambda ref: ref[ts, :, cols]
                r_, w_, k_, kn_, b_ = row(r_ref), row(w_ref), row(k_ref), row(kn_ref), row(b_ref)
                s_prev = hist_ref[0, p, pl.ds(ts, 1)][0]
                vb = _head_sums(v_ref[0, p, 0] * sel, first_head)
                dyb = _head_sums(dy_ref[0, p, 0] * sel, first_head)
                sa = _head_sums(s_prev * kn_, first_head)
                s_t = s_prev * w_ + sa * b_ + vb * k_
                ds = ds_ref[p] + dyb * r_
                dr_ref[ts, :, cols] = colsum(s_t * dyb)
                dk_ref[ts, :, cols] = colsum(ds * vb)
                db_ref[ts, :, cols] = colsum(ds * sa)
                dw_ref[ts, :, cols] = colsum(ds * s_prev)
                dvb = _head_sums(ds * k_, first_head)
                dv_ref[0, p, 0] += dvb * sel
                dsa = _head_sums(ds * b_, first_head)
                dkn_ref[ts, :, cols] = colsum(s_prev * dsa)
                ds_ref[p] = ds * w_ + dsa * kn_
            return carry

        lax.fori_loop(0, TCH, step, 0)

    row_spec = pl.BlockSpec((TCH, 1, pg * LANES), lambda bb, g, c: (bb * nch + nch - 1 - c, 0, g))
    col_spec = pl.BlockSpec((1, pg, 1, RWKV_HEAD, LANES), lambda bb, g, c: (bb, g, nch - 1 - c, 0, 0))
    hist_spec = pl.BlockSpec((1, pg, TCH, RWKV_HEAD, LANES), lambda bb, g, c: (bb, g, nch - 1 - c, 0, 0))
    row_shape = jax.ShapeDtypeStruct((bl * t, 1, d), F32)
    rows3 = [a.reshape(bl * t, 1, d) for a in (r, w, k, kn, b)]
    outs = pl.pallas_call(
        body, grid=(bl, npair // pg, nch),
        in_specs=[row_spec] * 5 + [col_spec, col_spec, hist_spec],
        out_specs=[row_spec] * 5 + [col_spec],
        out_shape=[row_shape] * 5 + [jax.ShapeDtypeStruct((bl, npair, nch, RWKV_HEAD, LANES), F32)],
        scratch_shapes=[pltpu.VMEM((pg, RWKV_HEAD, LANES), F32)],
        compiler_params=_params(("parallel", "parallel", "arbitrary")), name=name)(
            *rows3, v_col, dy_col, hist)
    return [o.reshape(bl * t, d) for o in outs[:5]] + [outs[5]]


def _mla_norms(pm, gq, gkv):
    ql = gq.shape[1]
    kvl = gkv.shape[1]
    return _rms(pm[:, :ql], gq), _rms(pm[:, ql:ql + kvl], gkv)


def mla_prep_fwd(pm, gq, gkv, name):
    n = pm.shape[0]
    return tilek(_mla_norms, [(pm, "r"), (gq, "f"), (gkv, "f")],
                 [("r", gq.shape[1], MMD), ("r", gkv.shape[1], MMD)], n_rows=n, tr=256, name=name)


def mla_prep_bwd(pm, gq, gkv, dcq, dckv, dkpe, name):
    n, wm = pm.shape
    ql, kvl = gq.shape[1], gkv.shape[1]

    def fn(pmv, gqv, gkvv, d1, d2, d3):
        _, vjp1 = jax.vjp(_rms, pmv[:, :ql], gqv)
        _, vjp2 = jax.vjp(_rms, pmv[:, ql:ql + kvl], gkvv)
        dcq_in, dgq = vjp1(d1)
        dckv_in, dgkv = vjp2(d2)
        return jnp.concatenate([dcq_in, dckv_in, d3], axis=1), dgq, dgkv

    return tilek(fn, [(pm, "r"), (gq, "f"), (gkv, "f"), (dcq, "r"), (dckv, "r"), (dkpe, "r")],
                 [("r", wm, F32), ("acc", gq.shape), ("acc", gkv.shape)], n_rows=n, tr=128, name=name)


def _rope(x, c, s, first):
    sw = jnp.where(first, pltpu.roll(x, LANES - ROPE_DIM // 2, 1), pltpu.roll(x, ROPE_DIM // 2, 1))
    return x * c + sw * s


def _unrope(d, c, s, first):
    z = d * s
    sw = jnp.where(first, pltpu.roll(z, LANES - ROPE_DIM // 2, 1), pltpu.roll(z, ROPE_DIM // 2, 1))
    return d * c + sw


def attn_fwd(q, kv, pm, ct, st, bl, t, hm, name):
    n = q.shape[0]
    tq = LANES
    scale = QK_DIM ** -0.5
    kpe_blk = pm.shape[1] // LANES - 1

    def body(qn_ref, qpe_ref, kn_ref, v_ref, kpe_ref, ct_ref, st_ref, o_ref, lse_ref, kp_s, kn_s, v_s):
        h = pl.program_id(1)
        lane = lax.broadcasted_iota(jnp.int32, (1, LANES), 1)
        first = (lane & (ROPE_DIM - 1)) < ROPE_DIM // 2
        kp = _rope(kpe_ref[...], ct_ref[...], st_ref[...], first)
        kp_s[...] = jnp.where(h % 2 == 0, kp, pltpu.roll(kp, ROPE_DIM, 1)).astype(MMD)
        kn_s[...] = kn_ref[...].astype(MMD)
        v_s[...] = v_ref[...].astype(MMD)
        kpos = lax.broadcasted_iota(jnp.int32, (1, t), 1)

        def qtile(i, carry):
            rows = pl.ds(pl.multiple_of(i * tq, tq), tq)
            q2 = _rope(qpe_ref[rows, :], ct_ref[rows, :], st_ref[rows, :], first)
            s = (_mm(qn_ref[rows, :], kn_s[...], ((1,), (1,))) + _mm(q2, kp_s[...], ((1,), (1,)))) * scale
            qpos = i * tq + lax.broadcasted_iota(jnp.int32, (tq, 1), 0)
            s = jnp.where(kpos <= qpos, s, -1e30)
            m = jnp.max(s, axis=1, keepdims=True)
            p = jnp.exp(s - m)
            l = jnp.sum(p, axis=1, keepdims=True)
            o_ref[rows, :] = _mm(p, v_s[...]) / l
            lse_ref[0, 0, rows, :] = m + jnp.log(l)
            return carry

        lax.fori_loop(0, t // tq, qtile, 0)

    blk = lambda f: pl.BlockSpec((t, LANES), f)
    return pl.pallas_call(
        body, grid=(bl, hm),
        in_specs=[blk(lambda b, h: (b, h)), blk(lambda b, h: (b, hm + h // 2)), blk(lambda b, h: (b, h)),
                  blk(lambda b, h: (b, hm + h)), blk(lambda b, h: (b, kpe_blk)), blk(lambda b, h: (0, 0)), blk(lambda b, h: (0, 0))],
        out_specs=[blk(lambda b, h: (b, h)), pl.BlockSpec((1, 1, t, 1), lambda b, h: (b, h, 0, 0))],
        out_shape=[jax.ShapeDtypeStruct((n, hm * LANES), F32), jax.ShapeDtypeStruct((bl, hm, t, 1), F32)],
        scratch_shapes=[pltpu.VMEM((t, LANES), MMD)] * 3,
        compiler_params=_params(("parallel", "arbitrary")), name=name)(q, q, kv, kv, pm, ct, st)


def attn_bwd(q, kv, pm, o, do, lse, ct, st, bl, t, hm, name):
    n = q.shape[0]
    tq = LANES
    scale = QK_DIM ** -0.5
    kpe_blk = pm.shape[1] // LANES - 1

    def body(qn_ref, qpe_ref, kn_ref, v_ref, kpe_ref, o_ref, do_ref, lse_ref, ct_ref, st_ref,
             dqn_ref, dqpe_ref, dkn_ref, dv_ref, dkpe_ref, kp_s, kn_s, v_s, dkn_s, dkp_s, dv_s):
        h = pl.program_id(1)
        lane = lax.broadcasted_iota(jnp.int32, (1, LANES), 1)
        first = (lane & (ROPE_DIM - 1)) < ROPE_DIM // 2
        mine = (lane // ROPE_DIM) == (h % 2)
        kp = _rope(kpe_ref[...], ct_ref[...], st_ref[...], first)
        kp_s[...] = jnp.where(h % 2 == 0, kp, pltpu.roll(kp, ROPE_DIM, 1)).astype(MMD)
        kn_s[...] = kn_ref[...].astype(MMD)
        v_s[...] = v_ref[...].astype(MMD)
        dkn_s[...] = jnp.zeros_like(dkn_s)
        dkp_s[...] = jnp.zeros_like(dkp_s)
        dv_s[...] = jnp.zeros_like(dv_s)
        kpos = lax.broadcasted_iota(jnp.int32, (1, t), 1)

        @pl.when(h % 2 == 0)
        def _():
            dqpe_ref[...] = jnp.zeros_like(dqpe_ref)

        @pl.when(h == 0)
        def _():
            dkpe_ref[...] = jnp.zeros_like(dkpe_ref)

        def qtile(i, carry):
            rows = pl.ds(pl.multiple_of(i * tq, tq), tq)
            c_i, s_i = ct_ref[rows, :], st_ref[rows, :]
            q1 = qn_ref[rows, :].astype(MMD)
            q2 = _rope(qpe_ref[rows, :], c_i, s_i, first).astype(MMD)
            s = (_mm(q1, kn_s[...], ((1,), (1,))) + _mm(q2, kp_s[...], ((1,), (1,)))) * scale
            qpos = i * tq + lax.broadcasted_iota(jnp.int32, (tq, 1), 0)
            p = jnp.where(kpos <= qpos, jnp.exp(s - lse_ref[0, 0, rows, :]), 0.0)
            do_i = do_ref[rows, :]
            delta = jnp.sum(do_i * o_ref[rows, :], axis=1, keepdims=True)
            dp = _mm(do_i, v_s[...], ((1,), (1,)))
            ds = (p * (dp - delta) * scale).astype(MMD)
            dqn_ref[rows, :] = _mm(ds, kn_s[...])
            dq2 = jnp.where(mine, _mm(ds, kp_s[...]), 0.0)
            dqpe_ref[rows, :] += _unrope(dq2, c_i, s_i, first)
            dkn_s[...] += _mm(ds, q1, ((0,), (0,)))
            dkp_s[...] += _mm(ds, q2, ((0,), (0,)))
            dv_s[...] += _mm(p, do_i, ((0,), (0,)))
            return carry

        lax.fori_loop(0, t // tq, qtile, 0)
        dkn_ref[...] = dkn_s[...]
        dv_ref[...] = dv_s[...]
        dkp = jnp.where(mine, dkp_s[...], 0.0)
        dkp = jnp.where(h % 2 == 0, dkp, pltpu.roll(dkp, ROPE_DIM, 1))
        dkpe_ref[...] += _unrope(dkp, ct_ref[...], st_ref[...], first)

    blk = lambda f: pl.BlockSpec((t, LANES), f)
    hd = lambda b, h: (b, h)
    shp = lambda wd: jax.ShapeDtypeStruct((n, wd), F32)
    return pl.pallas_call(
        body, grid=(bl, hm),
        in_specs=[blk(hd), blk(lambda b, h: (b, hm + h // 2)), blk(hd), blk(lambda b, h: (b, hm + h)),
                  blk(lambda b, h: (b, kpe_blk)), blk(hd), blk(hd), pl.BlockSpec((1, 1, t, 1), lambda b, h: (b, h, 0, 0)),
                  blk(lambda b, h: (0, 0)), blk(lambda b, h: (0, 0))],
        out_specs=[blk(hd), blk(lambda b, h: (b, h // 2)), blk(hd), blk(hd), blk(lambda b, h: (b, 0))],
        out_shape=[shp(hm * LANES), shp(hm * ROPE_DIM), shp(hm * LANES), shp(hm * LANES), shp(LANES)],
        scratch_shapes=[pltpu.VMEM((t, LANES), MMD)] * 3 + [pltpu.VMEM((t, LANES), F32)] * 3,
        compiler_params=_params(("parallel", "arbitrary")), name=name)(q, q, kv, kv, pm, o, do, lse, ct, st)


def _peer(k):
    mx, my, mc = lax.axis_index("x"), lax.axis_index("y"), lax.axis_index("c")
    px = 1 - mx if k & 4 else mx
    py = 1 - my if k & 2 else my
    pc = 1 - mc if k & 1 else mc
    return (px, py, pc), 4 * px + 2 * py + pc


def all_gather(x, name):
    def body(x_ref, o_ref, send_sems, recv_sems, local_sem):
        _, me = _peer(0)
        local = pltpu.make_async_copy(x_ref, o_ref.at[me], local_sem)
        local.start()
        copies = []
        for k in range(1, N_DEV):
            dev, _ = _peer(k)
            cp = pltpu.make_async_remote_copy(src_ref=x_ref, dst_ref=o_ref.at[me], send_sem=send_sems.at[k - 1],
                                              recv_sem=recv_sems.at[k - 1], device_id=dev, device_id_type=MESH)
            cp.start()
            copies.append(cp)
        for cp in copies:
            cp.wait()
        local.wait()

    return pl.pallas_call(
        body, in_specs=[pl.BlockSpec(memory_space=pl.ANY)], out_specs=pl.BlockSpec(memory_space=pl.ANY),
        out_shape=jax.ShapeDtypeStruct((N_DEV,) + x.shape, x.dtype),
        scratch_shapes=[pltpu.SemaphoreType.DMA((N_DEV - 1,)), pltpu.SemaphoreType.DMA((N_DEV - 1,)), pltpu.SemaphoreType.DMA],
        name=name)(x)


def all_to_all(x, name):
    def body(x_ref, o_ref, send_sems, recv_sems, local_sem):
        _, me = _peer(0)
        local = pltpu.make_async_copy(x_ref.at[me], o_ref.at[me], local_sem)
        local.start()
        copies = []
        for k in range(1, N_DEV):
            dev, idx = _peer(k)
            cp = pltpu.make_async_remote_copy(src_ref=x_ref.at[idx], dst_ref=o_ref.at[me], send_sem=send_sems.at[k - 1],
                                              recv_sem=recv_sems.at[k - 1], device_id=dev, device_id_type=MESH)
            cp.start()
            copies.append(cp)
        for cp in copies:
            cp.wait()
        local.wait()

    return pl.pallas_call(
        body, in_specs=[pl.BlockSpec(memory_space=pl.ANY)], out_specs=pl.BlockSpec(memory_space=pl.ANY),
        out_shape=jax.ShapeDtypeStruct(x.shape, x.dtype),
        scratch_shapes=[pltpu.SemaphoreType.DMA((N_DEV - 1,)), pltpu.SemaphoreType.DMA((N_DEV - 1,)), pltpu.SemaphoreType.DMA],
        name=name)(x)


def sum_blocks(x, name):
    _, r, c = x.shape
    tr = _tile(r, 512, 16)

    def body(x_ref, o_ref):
        acc = x_ref[0].astype(F32)
        for i in range(1, N_DEV):
            acc = acc + x_ref[i].astype(F32)
        o_ref[...] = acc

    return pl.pallas_call(
        body, grid=(r // tr,), in_specs=[pl.BlockSpec((N_DEV, tr, c), lambda i: (0, i, 0))],
        out_specs=pl.BlockSpec((tr, c), lambda i: (i, 0)), out_shape=jax.ShapeDtypeStruct((r, c), F32),
        compiler_params=_params(("parallel",)), name=name)(x)


def _adamw(w, g, m, v):
    m = ADAM_B1 * m + (1.0 - ADAM_B1) * g
    v = ADAM_B2 * v + (1.0 - ADAM_B2) * jnp.square(g)
    m_hat = m / (1.0 - ADAM_B1 ** ADAM_STEP)
    v_hat = v / (1.0 - ADAM_B2 ** ADAM_STEP)
    delta = -ADAM_LR * (m_hat / (jnp.sqrt(v_hat) + ADAM_EPS) + ADAM_WD * w)
    return delta, m, v


def adamw(w, g, m, v, name):
    r, c = w.shape
    tr = _tile(r, 256, 8)
    spec = pl.BlockSpec((tr, c), lambda i: (i, 0))

    def body(w_ref, g_ref, m_ref, v_ref, d_ref, nm_ref, nv_ref):
        d_ref[...], nm_ref[...], nv_ref[...] = _adamw(w_ref[...], g_ref[...], m_ref[...], v_ref[...])

    return pl.pallas_call(
        body, grid=(r // tr,), in_specs=[spec] * 4, out_specs=[spec] * 3,
        out_shape=[jax.ShapeDtypeStruct((r, c), F32)] * 3, compiler_params=_params(("parallel",)), name=name)(w, g, m, v)


def batch_sum_rows(dh, bl, t, rows, name):
    d = dh.shape[1]

    def body(x_ref, o_ref):
        @pl.when(pl.program_id(0) == 0)
        def _():
            o_ref[...] = jnp.zeros_like(o_ref)

        o_ref[...] += x_ref[...]

    return pl.pallas_call(
        body, grid=(bl,), in_specs=[pl.BlockSpec((rows, d), lambda b: (b * (t // rows), 0))],
        out_specs=pl.BlockSpec((rows, d), lambda b: (0, 0)), out_shape=jax.ShapeDtypeStruct((rows, d), F32),
        compiler_params=_params(("arbitrary",)), name=name)(dh)


class Dims:
    def __init__(self, x, meta_full_cols, w_up, g_up, q_norm, kv_norm, d_ff):
        self.bl, self.seq, self.d = x.shape
        self.n_meta = 16
        self.t_real = self.n_meta + self.seq
        self.t = -(-self.t_real // LANES) * LANES
        self.n = self.bl * self.t
        self.f = d_ff
        self.wl, self.gl = w_up.shape[-2], g_up.shape[-2]
        self.ql, self.kvl = q_norm.shape[-1], kv_norm.shape[-1]
        self.hm = self.d // V_DIM
        self.in_cols = 5 * self.d + 2 * self.wl + self.gl + self.ql + self.kvl + ROPE_DIM


def _pad_cols(a, width):
    return jnp.pad(a, ((0, 0), (0, width - a.shape[1])))


def _pad_rows(a, rows):
    return jnp.pad(a, ((0, rows - a.shape[0]), (0, 0)))


def split_in(a, dm, axis=1):
    d, wl, gl, ql, kvl = dm.d, dm.wl, dm.gl, dm.ql, dm.kvl
    size = a.shape[axis]
    cut = lambda lo, hi: lax.slice_in_dim(a, min(lo, size), min(hi, size), axis=axis)

    def pad(p, width):
        cfg = [(0, 0)] * a.ndim
        cfg[axis] = (0, width - p.shape[axis])
        return jnp.pad(p, cfg)

    o = 3 * d
    lora = jnp.concatenate([pad(cut(o, o + wl), LANES), pad(cut(o + wl, o + 2 * wl), LANES),
                            cut(o + 2 * wl, o + 2 * wl + gl)], axis=axis)
    o += 2 * wl + gl
    mla = pad(cut(o, o + ql + kvl + ROPE_DIM), ql + kvl + LANES)
    o += ql + kvl + ROPE_DIM
    return dict(r=cut(0, d), k=cut(d, 2 * d), v=cut(2 * d, 3 * d), l=lora, m=mla, ga=cut(o, o + d), gb=cut(o + d, o + 2 * d))


def merge_in(g, dm, axis=1):
    wl, gl, ql, kvl = dm.wl, dm.gl, dm.ql, dm.kvl
    cut = lambda p, lo, hi: lax.slice_in_dim(p, lo, hi, axis=axis)
    l, m = g["l"], g["m"]
    return jnp.concatenate([g["r"], g["k"], g["v"], cut(l, 0, wl), cut(l, LANES, LANES + wl), cut(l, 2 * LANES, 2 * LANES + gl),
                            cut(m, 0, ql + kvl + ROPE_DIM), g["ga"], g["gb"]], axis=axis)


def split_uq(w, dm):
    w3 = w.reshape(w.shape[0], dm.hm, QK_DIM)
    return jnp.concatenate([w3[:, :, :NOPE_DIM].reshape(w.shape[0], -1), w3[:, :, NOPE_DIM:].reshape(w.shape[0], -1)], axis=1)


def merge_uq(gn, gp, dm):
    r = gn.shape[0]
    return jnp.concatenate([gn.reshape(r, dm.hm, NOPE_DIM), gp.reshape(r, dm.hm, ROPE_DIM)], axis=2).reshape(r, -1)


def split_ukv(w, dm):
    w3 = w.reshape(w.shape[0], dm.hm, NOPE_DIM + V_DIM)
    return jnp.concatenate([w3[:, :, :NOPE_DIM].reshape(w.shape[0], -1), w3[:, :, NOPE_DIM:].reshape(w.shape[0], -1)], axis=1)


def merge_ukv(gk, gv, dm):
    r = gk.shape[0]
    return jnp.concatenate([gk.reshape(r, dm.hm, NOPE_DIM), gv.reshape(r, dm.hm, V_DIM)], axis=2).reshape(r, -1)


def head_matrices(d):
    heads = d // RWKV_HEAD
    e = (np.arange(d)[:, None] // RWKV_HEAD == np.arange(LANES)[None, :]) & (np.arange(LANES)[None, :] < heads)
    return jnp.asarray(e, BF16), jnp.asarray(e.T, BF16)


def rope_tables(t):
    pos = jnp.arange(t, dtype=F32)
    inv_freq = 1.0 / (ROPE_THETA ** (jnp.arange(0, ROPE_DIM, 2, dtype=F32) / ROPE_DIM))
    ang = pos[:, None] * inv_freq[None, :]
    cos, sin = jnp.cos(ang), jnp.sin(ang)
    return jnp.tile(jnp.concatenate([cos, cos], axis=1), (1, 2)), jnp.tile(jnp.concatenate([-sin, sin], axis=1), (1, 2))


def local_step(dm, x, loss_target, meta, wt, sp):
    bl, t, n, d, hm = dm.bl, dm.t, dm.n, dm.d, dm.hm
    e, et = head_matrices(d)
    ct, st = rope_tables(t)
    padz = jnp.zeros((bl, t - dm.t_real, d), F32)
    h0 = jnp.concatenate([jnp.broadcast_to(meta[None], (bl, dm.n_meta, d)), x, padz], axis=1).reshape(n, d)
    tgt = jnp.concatenate([jnp.zeros((bl, dm.n_meta, d), F32), loss_target, padz], axis=1).reshape(n, d)
    tpos = jnp.arange(t)
    mask = jnp.tile(((tpos >= dm.n_meta) & (tpos < dm.t_real)).astype(F32), bl).reshape(n, 1)

    win = split_in(wt["w_in"], dm, axis=0)
    mu = split_in(sp["tm_mu"], dm)
    wq, wkv = split_uq(wt["w_uq"], dm), split_ukv(wt["w_ukv"], dm)
    prm = dict(w0=sp["w0"], a0=sp["a0"], k_k=sp["k_k"], k_a=sp["k_a"], gn_w=sp["gn_w"], gn_b=sp["gn_b"], r_k=sp["r_k"],
               w_up=_pad_rows(wt["w_up"], LANES).astype(F32), a_up=_pad_rows(wt["a_up"], LANES).astype(F32),
               g_up=wt["g_up"].astype(F32))

    h1, ffn1 = ffn_forward(h0, sp["ffn1_norm"], wt["ffn1_w_gate"], wt["ffn1_w_up"], wt["ffn1_w_down"], "ffn1")
    u = rms_fwd(h1, sp["mix_norm"], "mix_rms")
    proj = {key: matmul([(u, win[key])], "nt", name=f"proj_{key}") for key in win}
    sh = {key: lerp_fwd(proj[key], mu[key], bl, t, f"shift_{key}") for key in ("r", "k", "v", "l")}
    decay, kmod, kneg, bvec, gate = rwkv_prep_fwd(sh["k"], sh["l"], prm, e, et, "rwkv_prep")
    v_col = to_col(sh["v"], bl, t, d)
    y_col, hist = scan_fwd(sh["r"], decay, kmod, kneg, bvec, v_col, bl, t, d, "wkv_scan")
    y = to_row(y_col, bl, t, d)
    cqn, ckvn = mla_prep_fwd(proj["m"], sp["q_norm"], sp["kv_norm"], "mla_norms")
    q = matmul([(cqn, wq)], "nn", name="mla_q")
    kv = matmul([(ckvn, wkv)], "nn", name="mla_kv")
    o, lse = attn_fwd(q, kv, proj["m"], ct, st, bl, t, hm, "mla_attn")
    post_in = [y, sh["r"], kmod, sh["v"], gate, proj["ga"], proj["gb"], o]
    mix = rwkv_post_fwd(post_in, prm, e, et, "mix_gate")
    h2 = matmul([(mix, wt["w_out"])], "nn", res=h1, name="out_proj")
    h3, ffn2 = ffn_forward(h2, sp["ffn2_norm"], wt["ffn2_w_gate"], wt["ffn2_w_up"], wt["ffn2_w_down"], "ffn2")
    dh3, d_final, loss = loss_head(h3, tgt, mask, sp["final_norm"], "loss_head")

    gw, gs = {}, {"final_norm": d_final}
    dh2, gs["ffn2_norm"], gw["ffn2_w_gate"], gw["ffn2_w_up"], gw["ffn2_w_down"] = ffn_backward(
        dh3, h2, sp["ffn2_norm"], wt["ffn2_w_gate"], wt["ffn2_w_up"], wt["ffn2_w_down"], ffn2, "ffn2")
    dmix = matmul([(dh2, wt["w_out"])], "nt", name="out_proj_dx")
    gw["w_out"] = matmul([(mix, dh2)], "tn", name="out_proj_dw")
    (dy, dr_p, dkm_p, dv_p, dgate, dpga, dpgb, do, gs["gn_w"], gs["gn_b"], gs["r_k"]) = rwkv_post_bwd(
        post_in, prm, e, et, dmix, "mix_gate_bwd")
    dqn, dqpe, dkn, dv_att, dkpe = attn_bwd(q, kv, proj["m"], o, do, lse, ct, st, bl, t, hm, "mla_attn_bwd")
    nq = hm * NOPE_DIM
    dcqn = matmul([(dqn, wq[:, :nq])], "nt", name="mla_q_dx1")
    dcqn = matmul([(dqpe, wq[:, nq:])], "nt", res=dcqn, name="mla_q_dx2")
    gw["w_uq"] = merge_uq(matmul([(cqn, dqn)], "tn", name="mla_q_dw1"), matmul([(cqn, dqpe)], "tn", name="mla_q_dw2"), dm)
    dckvn = matmul([(dkn, wkv[:, :nq]), (dv_att, wkv[:, nq:])], "nt", name="mla_kv_dx")
    gw["w_ukv"] = merge_ukv(matmul([(ckvn, dkn)], "tn", name="mla_kv_dw1"), matmul([(ckvn, dv_att)], "tn", name="mla_kv_dw2"), dm)
    dproj = {"ga": dpga, "gb": dpgb}
    dproj["m"], gs["q_norm"], gs["kv_norm"] = mla_prep_bwd(proj["m"], sp["q_norm"], sp["kv_norm"], dcqn, dckvn, dkpe, "mla_norms_bwd")
    dy_col = to_col(dy, bl, t, d)
    dr_s, ddecay, dk_s, dkneg, dbvec, dv_col = scan_bwd(sh["r"], decay, kmod, kneg, bvec, v_col, dy_col, hist,
                                                        bl, t, d, "wkv_scan_bwd")
    dv_s = to_row(dv_col, bl, t, d)
    (dsh_k, dsh_l, gs["w0"], gs["a0"], gs["k_k"], gs["k_a"], g_wup, g_aup, gw["g_up"]) = rwkv_prep_bwd(
        sh["k"], sh["l"], prm, e, et, [ddecay, dk_s, dkm_p, dkneg, dbvec, dgate], "rwkv_prep_bwd")
    gw["w_up"], gw["a_up"] = g_wup[:dm.wl], g_aup[:dm.wl]
    dmu = {}
    for key, cts in (("r", [dr_s, dr_p]), ("k", [dsh_k]), ("v", [dv_s, dv_p]), ("l", [dsh_l])):
        dproj[key], dmu[key] = lerp_bwd(proj[key], mu[key], cts, bl, t, f"shift_{key}_bwd")
    zero_m = jnp.zeros((1, proj["m"].shape[1]), F32)
    gs["tm_mu"] = merge_in(dict(dmu, m=zero_m, ga=zero_m[:, :0], gb=zero_m[:, :0]), dm)[:, :3 * d + 2 * dm.wl + dm.gl]
    wide = ("r", "k", "v", "ga", "gb")
    du = matmul([(dproj[key], win[key]) for key in wide], "nn", name="proj_dx", tk=512)
    du = matmul([(dproj["l"], win["l"])], "nn", res=du, name="proj_dx_l")
    du = matmul([(dproj["m"], win["m"])], "nn", res=du, name="proj_dx_m")
    gw["w_in"] = merge_in({key: matmul([(dproj[key], u)], "tn", name=f"proj_dw_{key}") for key in win}, dm, axis=0)
    dh1, gs["mix_norm"] = rms_bwd(h1, sp["mix_norm"], du, dh2, "mix_rms_bwd")
    dh0, gs["ffn1_norm"], gw["ffn1_w_gate"], gw["ffn1_w_up"], gw["ffn1_w_down"] = ffn_backward(
        dh1, h0, sp["ffn1_norm"], wt["ffn1_w_gate"], wt["ffn1_w_up"], wt["ffn1_w_down"], ffn1, "ffn1")
    grad_x = dh0.reshape(bl, t, d)[:, dm.n_meta:dm.t_real]
    dmeta = batch_sum_rows(dh0, bl, t, dm.n_meta, "meta_grad")
    return loss, grad_x, dmeta, gw, gs


COL_SHARDED = ("ffn1_w_gate", "ffn1_w_up", "w_in", "w_up", "a_up", "g_up", "w_uq", "w_ukv", "ffn2_w_gate", "ffn2_w_up")
ROW_SHARDED = ("ffn1_w_down", "w_out", "ffn2_w_down")
TRANSPOSED = ("ffn1_w_gate", "ffn1_w_up", "w_in", "ffn2_w_gate", "ffn2_w_up")
MATRICES = ("ffn1_w_gate", "ffn1_w_up", "ffn1_w_down", "w_in", "w_up", "a_up", "g_up", "w_uq", "w_ukv", "w_out",
            "ffn2_w_gate", "ffn2_w_up", "ffn2_w_down")
SMALL = ("ffn1_norm", "mix_norm", "tm_mu", "w0", "a0", "k_k", "k_a", "r_k", "gn_w", "gn_b", "q_norm", "kv_norm",
         "ffn2_norm", "final_norm")
WEIGHTS = ("meta_tokens", "ffn1_norm", "ffn1_w_gate", "ffn1_w_up", "ffn1_w_down", "mix_norm", "w_in", "tm_mu", "w0", "w_up",
           "a0", "a_up", "g_up", "k_k", "k_a", "r_k", "gn_w", "gn_b", "q_norm", "w_uq", "kv_norm", "w_ukv", "w_out",
           "ffn2_norm", "ffn2_w_gate", "ffn2_w_up", "ffn2_w_down", "final_norm")
PACK_COLS = 1024
PACK_ALIGN = 16 * PACK_COLS


def _pack(parts):
    offs, o = [], 0
    for p in parts:
        offs.append(o)
        o += p.shape[1]
    total = -(-o // PACK_ALIGN) * PACK_ALIGN
    flat = jnp.concatenate(list(parts) + [jnp.zeros((parts[0].shape[0], total - o), parts[0].dtype)], axis=1)
    return flat.reshape(parts[0].shape[0], total // PACK_COLS, PACK_COLS), offs


def kernel(x, meta_tokens, ffn1_norm, ffn1_w_gate, ffn1_w_up, ffn1_w_down, mix_norm, w_in, tm_mu, w0, w_up, a0, a_up, g_up, k_k, k_a, r_k, gn_w, gn_b, q_norm, w_uq, kv_norm, w_ukv, w_out, ffn2_norm, ffn2_w_gate, ffn2_w_up, ffn2_w_down, final_norm, loss_target, m_meta_tokens, m_ffn1_norm, m_ffn1_w_gate, m_ffn1_w_up, m_ffn1_w_down, m_mix_norm, m_w_in, m_tm_mu, m_w0, m_w_up, m_a0, m_a_up, m_g_up, m_k_k, m_k_a, m_r_k, m_gn_w, m_gn_b, m_q_norm, m_w_uq, m_kv_norm, m_w_ukv, m_w_out, m_ffn2_norm, m_ffn2_w_gate, m_ffn2_w_up, m_ffn2_w_down, m_final_norm, v_meta_tokens, v_ffn1_norm, v_ffn1_w_gate, v_ffn1_w_up, v_ffn1_w_down, v_mix_norm, v_w_in, v_tm_mu, v_w0, v_w_up, v_a0, v_a_up, v_g_up, v_k_k, v_k_a, v_r_k, v_gn_w, v_gn_b, v_q_norm, v_w_uq, v_kv_norm, v_w_ukv, v_w_out, v_ffn2_norm, v_ffn2_w_gate, v_ffn2_w_up, v_ffn2_w_down, v_final_norm):
    args = dict(locals())
    wts = {k: args[k] for k in WEIGHTS}
    ms = {k: args["m_" + k] for k in WEIGHTS}
    vs = {k: args["v_" + k] for k in WEIGHTS}
    dm = Dims(x, None, w_up, g_up, q_norm, kv_norm, ffn1_w_down.shape[1] * N_DEV)

    shard2d = {k: wts[k].reshape(wts[k].shape[-2], wts[k].shape[-1]) for k in MATRICES}
    sent = {k: shard2d[k].T if k in TRANSPOSED else shard2d[k] for k in MATRICES}
    send, offs = _pack([sent[k].astype(MMD).reshape(1, -1) for k in MATRICES])
    got = all_gather(send[0], "gather_weights").reshape(N_DEV, -1)
    full = {}
    for k, o in zip(MATRICES, offs):
        r, c = sent[k].shape
        blk = got[:, o:o + r * c].reshape(N_DEV, r, c)
        by_rows = k in TRANSPOSED or k in ROW_SHARDED
        full[k] = blk.reshape(N_DEV * r, c) if by_rows else blk.transpose(1, 0, 2).reshape(r, N_DEV * c)
    mr, mc = meta_tokens.shape
    meta = all_gather(meta_tokens, "gather_meta").transpose(1, 0, 2).reshape(mr, N_DEV * mc)
    small = {k: wts[k].reshape(1, -1) for k in SMALL}

    loss, grad_x, dmeta, gw, gs = local_step(dm, x, loss_target, meta, full, small)

    def blocks(k, g):
        r, c = sent[k].shape
        if k in TRANSPOSED or k in ROW_SHARDED:
            return g.reshape(N_DEV, r * c)
        return g.reshape(r, N_DEV, c).transpose(1, 0, 2).reshape(N_DEV, r * c)

    gsend, goffs = _pack([blocks(k, gw[k]).astype(MMD) for k in MATRICES]
                         + [dmeta.reshape(mr, N_DEV, mc).transpose(1, 0, 2).reshape(N_DEV, mr * mc).astype(MMD)])
    gsum = sum_blocks(all_to_all(gsend, "scatter_grads"), "sum_grads").reshape(-1)
    grads = {}
    for k, o in zip(MATRICES, goffs):
        r, c = sent[k].shape
        g = gsum[o:o + r * c].reshape(r, c)
        grads[k] = g.T if k in TRANSPOSED else g
    grads["meta_tokens"] = gsum[goffs[-1]:goffs[-1] + mr * mc].reshape(mr, mc)

    ssend, soffs = _pack([gs[k].reshape(1, -1) for k in SMALL] + [loss])
    ssum = sum_blocks(all_gather(ssend[0], "gather_small"), "sum_small").reshape(-1)
    for k, o in zip(SMALL, soffs):
        grads[k] = ssum[o:o + small[k].shape[1]]
    loss_total = ssum[soffs[-1]]

    delta, new_m, new_v = {}, {}, {}
    for k in MATRICES + ("meta_tokens",):
        shp = wts[k].shape
        to2d = lambda a: a.reshape(shp[-2], shp[-1])
        dlt, nm, nv = adamw(to2d(wts[k]), grads[k], to2d(ms[k]), to2d(vs[k]), f"adamw_{k}")
        delta[k], new_m[k], new_v[k] = dlt.reshape(shp), nm.reshape(shp), nv.reshape(shp)
        grads[k] = grads[k].reshape(shp)
    pw, _ = _pack([wts[k].reshape(1, -1) for k in SMALL])
    pm_, _ = _pack([ms[k].reshape(1, -1) for k in SMALL])
    pv, _ = _pack([vs[k].reshape(1, -1) for k in SMALL])
    pg, poffs = _pack([grads[k].reshape(1, -1) for k in SMALL])
    dlt, nm, nv = adamw(pw[0], pg[0], pm_[0], pv[0], "adamw_small")
    for k, o in zip(SMALL, poffs):
        shp, sz = wts[k].shape, small[k].shape[1]
        cut = lambda a: a.reshape(-1)[o:o + sz].reshape(shp)
        delta[k], new_m[k], new_v[k] = cut(dlt), cut(nm), cut(nv)
        grads[k] = grads[k].reshape(shp)

    return (loss_total, grad_x, *[grads[k] for k in WEIGHTS], *[delta[k] for k in WEIGHTS],
            *[new_m[k] for k in WEIGHTS], *[new_v[k] for k in WEIGHTS])
```

```python
import functools

import numpy as np
import jax
import jax.numpy as jnp
from jax import lax
from jax.experimental import pallas as pl
from jax.experimental.pallas import tpu as pltpu

F32 = jnp.float32
BF16 = jnp.bfloat16
MMD = BF16

NORM_EPS = 1e-6
RWKV_HEAD = 64
GN_EPS = RWKV_HEAD * 1e-5
NOPE_DIM = 128
ROPE_DIM = 64
V_DIM = 128
QK_DIM = NOPE_DIM + ROPE_DIM
ROPE_THETA = 10000.0
ADAM_LR = 0.001
ADAM_B1 = 0.9
ADAM_B2 = 0.999
ADAM_EPS = 1e-08
ADAM_WD = 0.01
ADAM_STEP = 10

LANES = 128
TCH = 64
N_DEV = 8
VMEM_LIMIT = 56 * 1024 * 1024
MESH = pl.DeviceIdType.MESH


def _tile(n, target, align):
    best = None
    for d in range(align, min(n, target) + 1, align):
        if n % d == 0:
            best = d
    return best if best is not None else n


def _params(sem=None):
    return pltpu.CompilerParams(dimension_semantics=sem, vmem_limit_bytes=VMEM_LIMIT)


def _mm(a, b, dims=((1,), (0,))):
    return lax.dot_general(a.astype(MMD), b.astype(MMD), (dims, ((), ())), preferred_element_type=F32)


@jax.custom_vjp
def mmdot(a, b):
    return _mm(a, b)


def _mmdot_fwd(a, b):
    return _mm(a, b), (a, b)


def _mmdot_bwd(res, g):
    a, b = res
    return _mm(g, b, ((1,), (1,))).astype(a.dtype), _mm(a, g, ((0,), (0,))).astype(b.dtype)


mmdot.defvjp(_mmdot_fwd, _mmdot_bwd)


def _dot2(x, m):
    hi = x.astype(BF16)
    lo = (x - hi.astype(F32)).astype(BF16)
    return (lax.dot_general(hi, m, (((1,), (0,)), ((), ())), preferred_element_type=F32)
            + lax.dot_general(lo, m, (((1,), (0,)), ((), ())), preferred_element_type=F32))


@jax.custom_vjp
def segsum(x, e, et):
    return _dot2(_dot2(x, e), et)


def _segsum_fwd(x, e, et):
    return segsum(x, e, et), (e, et)


def _segsum_bwd(res, g):
    e, et = res
    return segsum(g, e, et), jnp.zeros_like(e), jnp.zeros_like(et)


segsum.defvjp(_segsum_fwd, _segsum_bwd)


def _sigmoid(x):
    return 1.0 / (1.0 + jnp.exp(-x))


def _softplus(x):
    return jnp.maximum(x, 0.0) + jnp.log(1.0 + jnp.exp(-jnp.abs(x)))


def _rms(x, g):
    return x * lax.rsqrt(jnp.mean(x * x, axis=-1, keepdims=True) + NORM_EPS) * g


_DIMS = {"nn": ((1,), (0,)), "nt": ((1,), (1,)), "tn": ((0,), (0,))}


def matmul(pairs, mode, *, name, out_dtype=F32, res=None, alpha=1.0, tm=1088, tn=512, tk=1024):
    a0, b0 = pairs[0]
    if mode == "nn":
        (m, k), n = a0.shape, b0.shape[1]
    elif mode == "nt":
        (m, k), n = a0.shape, b0.shape[0]
    else:
        (k, m), n = a0.shape, b0.shape[1]
    tm = _tile(m, tm, 128 if mode == "tn" else 16)
    tn = _tile(n, tn, 128)
    tk = _tile(k, tk, 16 if mode == "tn" else 128)
    nk = k // tk
    npair = len(pairs)
    if mode == "tn":
        a_spec = pl.BlockSpec((tk, tm), lambda i, j, kk: (kk, i))
    else:
        a_spec = pl.BlockSpec((tm, tk), lambda i, j, kk: (i, kk))
    if mode == "nt":
        b_spec = pl.BlockSpec((tn, tk), lambda i, j, kk: (j, kk))
    else:
        b_spec = pl.BlockSpec((tk, tn), lambda i, j, kk: (kk, j))
    o_spec = pl.BlockSpec((tm, tn), lambda i, j, kk: (i, j))
    dims = _DIMS[mode]

    def body(*refs):
        ab = refs[:2 * npair]
        res_ref = refs[2 * npair] if res is not None else None
        o_ref, acc_ref = refs[-2], refs[-1]
        kk = pl.program_id(2)

        @pl.when(kk == 0)
        def _():
            acc_ref[...] = jnp.zeros_like(acc_ref)

        part = _mm(ab[0][...], ab[1][...], dims)
        for p in range(1, npair):
            part = part + _mm(ab[2 * p][...], ab[2 * p + 1][...], dims)
        acc_ref[...] += part

        @pl.when(kk == nk - 1)
        def _():
            out = acc_ref[...] * alpha if alpha != 1.0 else acc_ref[...]
            if res_ref is not None:
                out = res_ref[...].astype(F32) + out
            o_ref[...] = out.astype(o_ref.dtype)

    args, specs = [], []
    for a, b in pairs:
        args += [a, b]
        specs += [a_spec, b_spec]
    if res is not None:
        args.append(res)
        specs.append(o_spec)
    return pl.pallas_call(
        body, grid=(m // tm, n // tn, nk), in_specs=specs, out_specs=o_spec,
        out_shape=jax.ShapeDtypeStruct((m, n), out_dtype), scratch_shapes=[pltpu.VMEM((tm, tn), F32)],
        compiler_params=_params(("parallel", "parallel", "arbitrary")), name=name)(*args)


def tilek(fn, ins, outs, *, n_rows, tr, name):
    tr = _tile(n_rows, tr, 16)
    n_in = len(ins)
    in_specs = []
    for arr, kind in ins:
        if kind == "r":
            in_specs.append(pl.BlockSpec((tr, arr.shape[1]), lambda i: (i, 0)))
        else:
            in_specs.append(pl.BlockSpec(arr.shape, lambda i, nd=arr.ndim: (0,) * nd))
    out_specs, out_shape = [], []
    has_acc = False
    for o in outs:
        if o[0] == "r":
            out_specs.append(pl.BlockSpec((tr, o[1]), lambda i: (i, 0)))
            out_shape.append(jax.ShapeDtypeStruct((n_rows, o[1]), o[2]))
        else:
            has_acc = True
            out_specs.append(pl.BlockSpec(o[1], lambda i, nd=len(o[1]): (0,) * nd))
            out_shape.append(jax.ShapeDtypeStruct(o[1], F32))

    def body(*refs):
        i = pl.program_id(0)
        vals = fn(*[r[...] for r in refs[:n_in]])
        for o, r, v in zip(outs, refs[n_in:], vals):
            if o[0] == "r":
                r[...] = v.astype(r.dtype)
            else:
                @pl.when(i == 0)
                def _(r=r):
                    r[...] = jnp.zeros_like(r)

                r[...] += v

    return pl.pallas_call(
        body, grid=(n_rows // tr,), in_specs=in_specs, out_specs=out_specs, out_shape=out_shape,
        compiler_params=_params(("arbitrary",) if has_acc else ("parallel",)), name=name)(*[a for a, _ in ins])


def rms_fwd(x, g, name):
    n, d = x.shape
    return tilek(lambda xv, gv: (_rms(xv, gv),), [(x, "r"), (g, "f")], [("r", d, MMD)], n_rows=n, tr=256, name=name)[0]


def rms_bwd(x, g, dy, dres, name):
    n, d = x.shape

    def fn(xv, gv, dyv, drv):
        _, vjp = jax.vjp(_rms, xv, gv)
        dx, dg = vjp(dyv.astype(F32))
        return drv + dx, dg

    return tilek(fn, [(x, "r"), (g, "f"), (dy, "r"), (dres, "r")], [("r", d, F32), ("acc", (1, d))],
                 n_rows=n, tr=128, name=name)


def loss_head(h, tgt, mask, g, name):
    n, d = h.shape

    def fn(hv, tv, mv, gv):
        def lossf(hh, gg):
            e = (_rms(hh, gg) - tv) * mv
            s = jnp.sum(jnp.sum(e * e, axis=1, keepdims=True), axis=0, keepdims=True)
            return s * (0.5 / d)

        l, vjp = jax.vjp(lossf, hv, gv)
        dh, dg = vjp(jnp.ones((1, 1), F32))
        return dh, dg, jnp.broadcast_to(l, (1, LANES))

    return tilek(fn, [(h, "r"), (tgt, "r"), (mask, "r"), (g, "f")],
                 [("r", d, F32), ("acc", (1, d)), ("acc", (1, LANES))], n_rows=n, tr=128, name=name)


def ffn_up(hn, wg, wu, name):
    n, d = hn.shape
    f = wg.shape[0]
    tm, tn = _tile(n, 544, 16), _tile(f, 512, 128)

    def body(a_ref, g_ref, u_ref, og_ref, ou_ref, oa_ref):
        a = a_ref[...]
        g = _mm(a, g_ref[...], ((1,), (1,)))
        u = _mm(a, u_ref[...], ((1,), (1,)))
        og_ref[...] = g
        ou_ref[...] = u
        oa_ref[...] = (g * _sigmoid(g) * u).astype(oa_ref.dtype)

    o_spec = pl.BlockSpec((tm, tn), lambda i, j: (i, j))
    w_spec = pl.BlockSpec((tn, d), lambda i, j: (j, 0))
    return pl.pallas_call(
        body, grid=(n // tm, f // tn), in_specs=[pl.BlockSpec((tm, d), lambda i, j: (i, 0)), w_spec, w_spec],
        out_specs=[o_spec, o_spec, o_spec],
        out_shape=[jax.ShapeDtypeStruct((n, f), F32), jax.ShapeDtypeStruct((n, f), F32), jax.ShapeDtypeStruct((n, f), MMD)],
        compiler_params=_params(("parallel", "parallel")), name=name)(hn, wg, wu)


def ffn_down_bwd(dh, wd, gate, up, name):
    n, d = dh.shape
    f = wd.shape[0]
    tm, tn = _tile(n, 544, 16), _tile(f, 512, 128)

    def body(dh_ref, w_ref, g_ref, u_ref, dg_ref, du_ref):
        da = 0.5 * _mm(dh_ref[...], w_ref[...], ((1,), (1,)))
        g, u = g_ref[...], u_ref[...]
        s = _sigmoid(g)
        dg_ref[...] = (da * u * (s * (1.0 + g * (1.0 - s)))).astype(dg_ref.dtype)
        du_ref[...] = (da * (g * s)).astype(du_ref.dtype)

    o_spec = pl.BlockSpec((tm, tn), lambda i, j: (i, j))
    return pl.pallas_call(
        body, grid=(n // tm, f // tn),
        in_specs=[pl.BlockSpec((tm, d), lambda i, j: (i, 0)), pl.BlockSpec((tn, d), lambda i, j: (j, 0)), o_spec, o_spec],
        out_specs=[o_spec, o_spec],
        out_shape=[jax.ShapeDtypeStruct((n, f), MMD), jax.ShapeDtypeStruct((n, f), MMD)],
        compiler_params=_params(("parallel", "parallel")), name=name)(dh, wd, gate, up)


def ffn_forward(h, g, wg, wu, wd, tag):
    hn = rms_fwd(h, g, f"{tag}_rms")
    gate, up, act = ffn_up(hn, wg, wu, f"{tag}_up")
    out = matmul([(act, wd)], "nn", res=h, alpha=0.5, name=f"{tag}_down")
    return out, (hn, gate, up, act)


def ffn_backward(dout, h, g, wg, wu, wd, saved, tag):
    hn, gate, up, act = saved
    dgate, dup = ffn_down_bwd(dout, wd, gate, up, f"{tag}_dact")
    dwd = matmul([(act, dout)], "tn", alpha=0.5, name=f"{tag}_dwd")
    dwg = matmul([(dgate, hn)], "tn", name=f"{tag}_dwg")
    dwu = matmul([(dup, hn)], "tn", name=f"{tag}_dwu")
    dhn = matmul([(dgate, wg), (dup, wu)], "nn", name=f"{tag}_dhn")
    dh, dg = rms_bwd(h, g, dhn, dout, f"{tag}_drms")
    return dh, dg, dwg, dwu, dwd


def lerp_fwd(p, mu, bl, t, name):
    n, w = p.shape
    cb = _tile(w, 256, 128)

    def body(p_ref, mu_ref, o_ref):
        x = p_ref[...]
        row = lax.broadcasted_iota(jnp.int32, x.shape, 0)
        prev = jnp.where(row == 0, 0.0, pltpu.roll(x, 1, 0))
        o_ref[...] = x + mu_ref[...] * (prev - x)

    spec = pl.BlockSpec((t, cb), lambda b, j: (b, j))
    return pl.pallas_call(
        body, grid=(bl, w // cb), in_specs=[spec, pl.BlockSpec((1, cb), lambda b, j: (0, j))], out_specs=spec,
        out_shape=jax.ShapeDtypeStruct((n, w), F32), compiler_params=_params(("parallel", "parallel")), name=name)(p, mu)


def lerp_bwd(p, mu, douts, bl, t, name):
    n, w = p.shape
    cb = _tile(w, 256, 128)
    nd = len(douts)

    def body(*refs):
        p_ref, mu_ref = refs[0], refs[1]
        dp_ref, dmu_ref = refs[2 + nd], refs[3 + nd]
        b = pl.program_id(1)
        x, m = p_ref[...], mu_ref[...]
        d = refs[2][...]
        for r in refs[3:2 + nd]:
            d = d + r[...]
        row = lax.broadcasted_iota(jnp.int32, x.shape, 0)
        prev = jnp.where(row == 0, 0.0, pltpu.roll(x, 1, 0))
        z = d * m
        nxt = jnp.where(row == t - 1, 0.0, pltpu.roll(z, t - 1, 0))
        dp_ref[...] = d - z + nxt

        @pl.when(b == 0)
        def _():
            dmu_ref[...] = jnp.zeros_like(dmu_ref)

        dmu_ref[...] += jnp.sum(d * (prev - x), axis=0, keepdims=True)

    spec = pl.BlockSpec((t, cb), lambda j, b: (b, j))
    cspec = pl.BlockSpec((1, cb), lambda j, b: (0, j))
    return pl.pallas_call(
        body, grid=(w // cb, bl), in_specs=[spec, cspec] + [spec] * nd, out_specs=[spec, cspec],
        out_shape=[jax.ShapeDtypeStruct((n, w), F32), jax.ShapeDtypeStruct((1, w), F32)],
        compiler_params=_params(("parallel", "arbitrary")), name=name)(p, mu, *douts)


def _prep(k, xw, xa, xg, w0, a0, k_k, k_a, w_up, a_up, g_up, e, et):
    w_pre = -_softplus(-(w0 + mmdot(jnp.tanh(xw), w_up))) - 0.5
    decay = jnp.exp(-jnp.exp(w_pre))
    a = _sigmoid(a0 + mmdot(xa, a_up))
    g = mmdot(_sigmoid(xg), g_up)
    kk = k * k_k
    kk = kk * lax.rsqrt(jnp.maximum(segsum(kk * kk, e, et), 1e-24))
    kmod = k * (1.0 + (a - 1.0) * k_a)
    return decay, kmod, -kk, kk * a, g


def _lora_parts(xl):
    return xl[:, :LANES], xl[:, LANES:2 * LANES], xl[:, 2 * LANES:]


def rwkv_prep_fwd(pk, pl_, prm, e, et, name):
    n, d = pk.shape
    small = [prm[k] for k in ("w0", "a0", "k_k", "k_a", "w_up", "a_up", "g_up")]
    ins = [(pk, "r"), (pl_, "r")] + [(s, "f") for s in small] + [(e, "f"), (et, "f")]
    return tilek(lambda k, xl, *rest: _prep(k, *_lora_parts(xl), *rest), ins, [("r", d, F32)] * 5, n_rows=n, tr=128, name=name)


def rwkv_prep_bwd(pk, pl_, prm, e, et, cts, name):
    n, d = pk.shape
    small = [prm[k] for k in ("w0", "a0", "k_k", "k_a", "w_up", "a_up", "g_up")]

    def fn(k, xl, w0, a0, k_k, k_a, w_up, a_up, g_up, ev, etv, dw, dkm1, dkm2, dkn, db, dg):
        _, vjp = jax.vjp(lambda *a: _prep(*a, ev, etv), k, *_lora_parts(xl), w0, a0, k_k, k_a, w_up, a_up, g_up)
        dk, dxw, dxa, dxg, *dsmall = vjp((dw, dkm1 + dkm2, dkn, db, dg))
        return (dk, jnp.concatenate([dxw, dxa, dxg], axis=1), *dsmall)

    ins = [(pk, "r"), (pl_, "r")] + [(s, "f") for s in small] + [(e, "f"), (et, "f")] + [(c, "r") for c in cts]
    outs = [("r", d, F32), ("r", pl_.shape[1], F32)] + [("acc", s.shape) for s in small]
    return tilek(fn, ins, outs, n_rows=n, tr=64, name=name)


def _post(y, r, km, v, g, pga, pgb, yb, gn_w, gn_b, r_k, e, et):
    inv = 1.0 / RWKV_HEAD
    yc = y - segsum(y, e, et) * inv
    var = segsum(yc * yc, e, et) * inv
    yn = yc * lax.rsqrt(var + GN_EPS) * gn_w + gn_b
    bonus = segsum(r * km * r_k, e, et) * v
    ya = (yn + bonus) * g
    return _sigmoid(pga) * ya + _sigmoid(pgb) * yb


def rwkv_post_fwd(acts, prm, e, et, name):
    n, d = acts[0].shape
    small = [prm[k] for k in ("gn_w", "gn_b", "r_k")]
    ins = [(a, "r") for a in acts] + [(s, "f") for s in small] + [(e, "f"), (et, "f")]
    return tilek(lambda *a: (_post(*a),), ins, [("r", d, MMD)], n_rows=n, tr=128, name=name)[0]


def rwkv_post_bwd(acts, prm, e, et, dm, name):
    n, d = acts[0].shape
    small = [prm[k] for k in ("gn_w", "gn_b", "r_k")]
    na = len(acts)

    def fn(*a):
        prim, ev, etv, dmv = a[:na + 3], a[na + 3], a[na + 4], a[na + 5]
        _, vjp = jax.vjp(lambda *z: _post(*z, ev, etv), *prim)
        return vjp(dmv.astype(F32))

    ins = [(x, "r") for x in acts] + [(s, "f") for s in small] + [(e, "f"), (et, "f"), (dm, "r")]
    outs = [("r", d, F32)] * na + [("acc", s.shape) for s in small]
    return tilek(fn, ins, outs, n_rows=n, tr=64, name=name)


def to_col(a, bl, t, d):
    p, c = d // LANES, t // TCH
    a = a.reshape(bl, c, TCH, p, 2, RWKV_HEAD).transpose(0, 3, 1, 5, 4, 2)
    return a.reshape(bl, p, c, RWKV_HEAD, LANES)


def to_row(a, bl, t, d):
    p, c = d // LANES, t // TCH
    a = a.reshape(bl, p, c, RWKV_HEAD, 2, TCH).transpose(0, 2, 5, 1, 4, 3)
    return a.reshape(bl * t, d)


def _pair_group(d):
    return min(4, d // LANES)


def _head_sums(x, first_head):
    a = jnp.sum(jnp.where(first_head, x, 0.0), axis=1, keepdims=True)
    b = jnp.sum(jnp.where(first_head, 0.0, x), axis=1, keepdims=True)
    return jnp.where(first_head, a, b)


def scan_fwd(r, w, k, kn, b, v_col, bl, t, d, name):
    npair, nch = d // LANES, t // TCH
    pg = _pair_group(d)

    def body(r_ref, w_ref, k_ref, kn_ref, b_ref, v_ref, y_ref, hist_ref, s_ref):
        @pl.when(pl.program_id(2) == 0)
        def _():
            s_ref[...] = jnp.zeros_like(s_ref)

        y_ref[...] = jnp.zeros_like(y_ref)
        lane = lax.broadcasted_iota(jnp.int32, (1, LANES), 1)
        first_head = lane < RWKV_HEAD

        def step(ts, carry):
            sel = ((lane & (TCH - 1)) == ts).astype(F32)
            sel_prev = ((lane & (TCH - 1)) == ts - 1).astype(F32)
            prev = jnp.maximum(ts - 1, 0)
            for p in range(pg):
                cols = slice(p * LANES, (p + 1) * LANES)
                row = lambda ref: ref[ts, :, cols]
                s = s_ref[p]
                hist_ref[0, p, pl.ds(ts, 1)] = s[None]
                y_ref[0, p, 0] += _head_sums(s * r_ref[prev, :, cols], first_head) * sel_prev
                sa = _head_sums(s * row(kn_ref), first_head)
                vb = _head_sums(v_ref[0, p, 0] * sel, first_head)
                s_ref[p] = s * row(w_ref) + sa * row(b_ref) + vb * row(k_ref)
            return carry

        lax.fori_loop(0, TCH, step, 0)
        last = ((lane & (TCH - 1)) == TCH - 1).astype(F32)
        for p in range(pg):
            cols = slice(p * LANES, (p + 1) * LANES)
            y_ref[0, p, 0] += _head_sums(s_ref[p] * r_ref[TCH - 1, :, cols], first_head) * last

    row_spec = pl.BlockSpec((TCH, 1, pg * LANES), lambda bb, g, c: (bb * nch + c, 0, g))
    col_spec = pl.BlockSpec((1, pg, 1, RWKV_HEAD, LANES), lambda bb, g, c: (bb, g, c, 0, 0))
    hist_spec = pl.BlockSpec((1, pg, TCH, RWKV_HEAD, LANES), lambda bb, g, c: (bb, g, c, 0, 0))
    rows3 = [a.reshape(bl * t, 1, d) for a in (r, w, k, kn, b)]
    return pl.pallas_call(
        body, grid=(bl, npair // pg, nch),
        in_specs=[row_spec] * 5 + [col_spec],
        out_specs=[col_spec, hist_spec],
        out_shape=[jax.ShapeDtypeStruct((bl, npair, nch, RWKV_HEAD, LANES), F32),
                   jax.ShapeDtypeStruct((bl, npair, t, RWKV_HEAD, LANES), F32)],
        scratch_shapes=[pltpu.VMEM((pg, RWKV_HEAD, LANES), F32)],
        compiler_params=_params(("parallel", "parallel", "arbitrary")), name=name)(*rows3, v_col)


def scan_bwd(r, w, k, kn, b, v_col, dy_col, hist, bl, t, d, name):
    npair, nch = d // LANES, t // TCH
    pg = _pair_group(d)

    def body(r_ref, w_ref, k_ref, kn_ref, b_ref, v_ref, dy_ref, hist_ref,
             dr_ref, dw_ref, dk_ref, dkn_ref, db_ref, dv_ref, ds_ref, dyb_ref):
        @pl.when(pl.program_id(2) == 0)
        def _():
            ds_ref[...] = jnp.zeros_like(ds_ref)

        dv_ref[...] = jnp.zeros_like(dv_ref)
        lane = lax.broadcasted_iota(jnp.int32, (1, LANES), 1)
        first_head = lane < RWKV_HEAD
        colsum = lambda x: jnp.sum(x, axis=0, keepdims=True)
        last = ((lane & (TCH - 1)) == TCH - 1).astype(F32)
        for p in range(pg):
            cols = slice(p * LANES, (p + 1) * LANES)
            dyb = _head_sums(dy_ref[0, p, 0] * last, first_head)
            dyb_ref[p] = dyb
            ds_ref[p] += dyb * r_ref[TCH - 1, :, cols]

        def step(it, carry):
            ts = TCH - 1 - it
            sel = ((lane & (TCH - 1)) == ts).astype(F32)
            sel_prev = ((lane & (TCH - 1)) == ts - 1).astype(F32)
            prev = jnp.maximum(ts - 1, 0)
            for p in range(pg):
                cols = slice(p * LANES, (p + 1) * LANES)
                row = lambda ref: ref[ts, :, cols]
                r_, w_, k_, kn_, b_ = row(r_ref), row(w_ref), row(k_ref), row(kn_ref), row(b_ref)
                s_prev = hist_ref[0, p, pl.ds(ts, 1)][0]
                ds = ds_ref[p]
                dyb = dyb_ref[p]
                dsa = _head_sums(ds * b_, first_head)
                dvb = _head_sums(ds * k_, first_head)
                vb = _head_sums(v_ref[0, p, 0] * sel, first_head)
                sa = _head_sums(s_prev * kn_, first_head)
                dyb_prev = _head_sums(dy_ref[0, p, 0] * sel_prev, first_head)
                s_t = s_prev * w_ + sa * b_ + vb * k_
                dr_ref[ts, :, cols] = colsum(s_t * dyb)
                dk_ref[ts, :, cols] = colsum(ds * vb)
                db_ref[ts, :, cols] = colsum(ds * sa)
                dw_ref[ts, :, cols] = colsum(ds * s_prev)
                dv_ref[0, p, 0] += dvb * sel
                dkn_ref[ts, :, cols] = colsum(s_prev * dsa)
                ds_ref[p] = ds * w_ + dsa * kn_ + dyb_prev * r_ref[prev, :, cols]
                dyb_ref[p] = dyb_prev
            return carry

        lax.fori_loop(0, TCH, step, 0)

    row_spec = pl.BlockSpec((TCH, 1, pg * LANES), lambda bb, g, c: (bb * nch + nch - 1 - c, 0, g))
    col_spec = pl.BlockSpec((1, pg, 1, RWKV_HEAD, LANES), lambda bb, g, c: (bb, g, nch - 1 - c, 0, 0))
    hist_spec = pl.BlockSpec((1, pg, TCH, RWKV_HEAD, LANES), lambda bb, g, c: (bb, g, nch - 1 - c, 0, 0))
    row_shape = jax.ShapeDtypeStruct((bl * t, 1, d), F32)
    rows3 = [a.reshape(bl * t, 1, d) for a in (r, w, k, kn, b)]
    outs = pl.pallas_call(
        body, grid=(bl, npair // pg, nch),
        in_specs=[row_spec] * 5 + [col_spec, col_spec, hist_spec],
        out_specs=[row_spec] * 5 + [col_spec],
        out_shape=[row_shape] * 5 + [jax.ShapeDtypeStruct((bl, npair, nch, RWKV_HEAD, LANES), F32)],
        scratch_shapes=[pltpu.VMEM((pg, RWKV_HEAD, LANES), F32)] * 2,
        compiler_params=_params(("parallel", "parallel", "arbitrary")), name=name)(
            *rows3, v_col, dy_col, hist)
    return [o.reshape(bl * t, d) for o in outs[:5]] + [outs[5]]


def _mla_norms(pm, gq, gkv):
    ql = gq.shape[1]
    kvl = gkv.shape[1]
    return _rms(pm[:, :ql], gq), _rms(pm[:, ql:ql + kvl], gkv)


def mla_prep_fwd(pm, gq, gkv, name):
    n = pm.shape[0]
    return tilek(_mla_norms, [(pm, "r"), (gq, "f"), (gkv, "f")],
                 [("r", gq.shape[1], MMD), ("r", gkv.shape[1], MMD)], n_rows=n, tr=256, name=name)


def mla_prep_bwd(pm, gq, gkv, dcq, dckv, dkpe, name):
    n, wm = pm.shape
    ql, kvl = gq.shape[1], gkv.shape[1]

    def fn(pmv, gqv, gkvv, d1, d2, d3):
        _, vjp1 = jax.vjp(_rms, pmv[:, :ql], gqv)
        _, vjp2 = jax.vjp(_rms, pmv[:, ql:ql + kvl], gkvv)
        dcq_in, dgq = vjp1(d1)
        dckv_in, dgkv = vjp2(d2)
        return jnp.concatenate([dcq_in, dckv_in, d3], axis=1), dgq, dgkv

    return tilek(fn, [(pm, "r"), (gq, "f"), (gkv, "f"), (dcq, "r"), (dckv, "r"), (dkpe, "r")],
                 [("r", wm, F32), ("acc", gq.shape), ("acc", gkv.shape)], n_rows=n, tr=128, name=name)


def _rope(x, c, s, first):
    sw = jnp.where(first, pltpu.roll(x, LANES - ROPE_DIM // 2, 1), pltpu.roll(x, ROPE_DIM // 2, 1))
    return x * c + sw * s


def _unrope(d, c, s, first):
    z = d * s
    sw = jnp.where(first, pltpu.roll(z, LANES - ROPE_DIM // 2, 1), pltpu.roll(z, ROPE_DIM // 2, 1))
    return d * c + sw


def attn_fwd(q, kv, pm, ct, st, bl, t, hm, name):
    n = q.shape[0]
    tq = LANES
    scale = QK_DIM ** -0.5
    kpe_blk = pm.shape[1] // LANES - 1

    def body(qn_ref, qpe_ref, kn_ref, v_ref, kpe_ref, ct_ref, st_ref, o_ref, lse_ref, kp_s, kn_s, v_s):
        h = pl.program_id(1)
        lane = lax.broadcasted_iota(jnp.int32, (1, LANES), 1)
        first = (lane & (ROPE_DIM - 1)) < ROPE_DIM // 2
        kp = _rope(kpe_ref[...], ct_ref[...], st_ref[...], first)
        kp_s[...] = jnp.where(h % 2 == 0, kp, pltpu.roll(kp, ROPE_DIM, 1)).astype(MMD)
        kn_s[...] = kn_ref[...].astype(MMD)
        v_s[...] = v_ref[...].astype(MMD)
        kpos = lax.broadcasted_iota(jnp.int32, (1, t), 1)

        def qtile(i, carry):
            rows = pl.ds(pl.multiple_of(i * tq, tq), tq)
            q2 = _rope(qpe_ref[rows, :], ct_ref[rows, :], st_ref[rows, :], first)
            s = (_mm(qn_ref[rows, :], kn_s[...], ((1,), (1,))) + _mm(q2, kp_s[...], ((1,), (1,)))) * scale
            qpos = i * tq + lax.broadcasted_iota(jnp.int32, (tq, 1), 0)
            s = jnp.where(kpos <= qpos, s, -1e30)
            m = jnp.max(s, axis=1, keepdims=True)
            p = jnp.exp(s - m)
            l = jnp.sum(p, axis=1, keepdims=True)
            o_ref[rows, :] = _mm(p, v_s[...]) / l
            lse_ref[0, 0, rows, :] = m + jnp.log(l)
            return carry

        lax.fori_loop(0, t // tq, qtile, 0)

    blk = lambda f: pl.BlockSpec((t, LANES), f)
    return pl.pallas_call(
        body, grid=(bl, hm),
        in_specs=[blk(lambda b, h: (b, h)), blk(lambda b, h: (b, hm + h // 2)), blk(lambda b, h: (b, h)),
                  blk(lambda b, h: (b, hm + h)), blk(lambda b, h: (b, kpe_blk)), blk(lambda b, h: (0, 0)), blk(lambda b, h: (0, 0))],
        out_specs=[blk(lambda b, h: (b, h)), pl.BlockSpec((1, 1, t, 1), lambda b, h: (b, h, 0, 0))],
        out_shape=[jax.ShapeDtypeStruct((n, hm * LANES), F32), jax.ShapeDtypeStruct((bl, hm, t, 1), F32)],
        scratch_shapes=[pltpu.VMEM((t, LANES), MMD)] * 3,
        compiler_params=_params(("parallel", "arbitrary")), name=name)(q, q, kv, kv, pm, ct, st)


def attn_bwd(q, kv, pm, o, do, lse, ct, st, bl, t, hm, name):
    n = q.shape[0]
    tq = LANES
    scale = QK_DIM ** -0.5
    kpe_blk = pm.shape[1] // LANES - 1

    def body(qn_ref, qpe_ref, kn_ref, v_ref, kpe_ref, o_ref, do_ref, lse_ref, ct_ref, st_ref,
             dqn_ref, dqpe_ref, dkn_ref, dv_ref, dkpe_ref, kp_s, kn_s, v_s, dkn_s, dkp_s, dv_s):
        h = pl.program_id(1)
        lane = lax.broadcasted_iota(jnp.int32, (1, LANES), 1)
        first = (lane & (ROPE_DIM - 1)) < ROPE_DIM // 2
        mine = (lane // ROPE_DIM) == (h % 2)
        kp = _rope(kpe_ref[...], ct_ref[...], st_ref[...], first)
        kp_s[...] = jnp.where(h % 2 == 0, kp, pltpu.roll(kp, ROPE_DIM, 1)).astype(MMD)
        kn_s[...] = kn_ref[...].astype(MMD)
        v_s[...] = v_ref[...].astype(MMD)
        dkn_s[...] = jnp.zeros_like(dkn_s)
        dkp_s[...] = jnp.zeros_like(dkp_s)
        dv_s[...] = jnp.zeros_like(dv_s)
        kpos = lax.broadcasted_iota(jnp.int32, (1, t), 1)

        @pl.when(h % 2 == 0)
        def _():
            dqpe_ref[...] = jnp.zeros_like(dqpe_ref)

        @pl.when(h == 0)
        def _():
            dkpe_ref[...] = jnp.zeros_like(dkpe_ref)

        def qtile(i, carry):
            rows = pl.ds(pl.multiple_of(i * tq, tq), tq)
            c_i, s_i = ct_ref[rows, :], st_ref[rows, :]
            q1 = qn_ref[rows, :].astype(MMD)
            q2 = _rope(qpe_ref[rows, :], c_i, s_i, first).astype(MMD)
            s = (_mm(q1, kn_s[...], ((1,), (1,))) + _mm(q2, kp_s[...], ((1,), (1,)))) * scale
            qpos = i * tq + lax.broadcasted_iota(jnp.int32, (tq, 1), 0)
            p = jnp.where(kpos <= qpos, jnp.exp(s - lse_ref[0, 0, rows, :]), 0.0)
            do_i = do_ref[rows, :]
            delta = jnp.sum(do_i * o_ref[rows, :], axis=1, keepdims=True)
            dp = _mm(do_i, v_s[...], ((1,), (1,)))
            ds = (p * (dp - delta) * scale).astype(MMD)
            dqn_ref[rows, :] = _mm(ds, kn_s[...])
            dq2 = jnp.where(mine, _mm(ds, kp_s[...]), 0.0)
            dqpe_ref[rows, :] += _unrope(dq2, c_i, s_i, first)
            dkn_s[...] += _mm(ds, q1, ((0,), (0,)))
            dkp_s[...] += _mm(ds, q2, ((0,), (0,)))
            dv_s[...] += _mm(p, do_i, ((0,), (0,)))
            return carry

        lax.fori_loop(0, t // tq, qtile, 0)
        dkn_ref[...] = dkn_s[...]
        dv_ref[...] = dv_s[...]
        dkp = jnp.where(mine, dkp_s[...], 0.0)
        dkp = jnp.where(h % 2 == 0, dkp, pltpu.roll(dkp, ROPE_DIM, 1))
        dkpe_ref[...] += _unrope(dkp, ct_ref[...], st_ref[...], first)

    blk = lambda f: pl.BlockSpec((t, LANES), f)
    hd = lambda b, h: (b, h)
    shp = lambda wd: jax.ShapeDtypeStruct((n, wd), F32)
    return pl.pallas_call(
        body, grid=(bl, hm),
        in_specs=[blk(hd), blk(lambda b, h: (b, hm + h // 2)), blk(hd), blk(lambda b, h: (b, hm + h)),
                  blk(lambda b, h: (b, kpe_blk)), blk(hd), blk(hd), pl.BlockSpec((1, 1, t, 1), lambda b, h: (b, h, 0, 0)),
                  blk(lambda b, h: (0, 0)), blk(lambda b, h: (0, 0))],
        out_specs=[blk(hd), blk(lambda b, h: (b, h // 2)), blk(hd), blk(hd), blk(lambda b, h: (b, 0))],
        out_shape=[shp(hm * LANES), shp(hm * ROPE_DIM), shp(hm * LANES), shp(hm * LANES), shp(LANES)],
        scratch_shapes=[pltpu.VMEM((t, LANES), MMD)] * 3 + [pltpu.VMEM((t, LANES), F32)] * 3,
        compiler_params=_params(("parallel", "arbitrary")), name=name)(q, q, kv, kv, pm, o, do, lse, ct, st)


def _peer(k):
    mx, my, mc = lax.axis_index("x"), lax.axis_index("y"), lax.axis_index("c")
    px = 1 - mx if k & 4 else mx
    py = 1 - my if k & 2 else my
    pc = 1 - mc if k & 1 else mc
    return (px, py, pc), 4 * px + 2 * py + pc


def all_gather(x, name):
    def body(x_ref, o_ref, send_sems, recv_sems, local_sem):
        _, me = _peer(0)
        local = pltpu.make_async_copy(x_ref, o_ref.at[me], local_sem)
        local.start()
        copies = []
        for k in range(1, N_DEV):
            dev, _ = _peer(k)
            cp = pltpu.make_async_remote_copy(src_ref=x_ref, dst_ref=o_ref.at[me], send_sem=send_sems.at[k - 1],
                                              recv_sem=recv_sems.at[k - 1], device_id=dev, device_id_type=MESH)
            cp.start()
            copies.append(cp)
        for cp in copies:
            cp.wait()
        local.wait()

    return pl.pallas_call(
        body, in_specs=[pl.BlockSpec(memory_space=pl.ANY)], out_specs=pl.BlockSpec(memory_space=pl.ANY),
        out_shape=jax.ShapeDtypeStruct((N_DEV,) + x.shape, x.dtype),
        scratch_shapes=[pltpu.SemaphoreType.DMA((N_DEV - 1,)), pltpu.SemaphoreType.DMA((N_DEV - 1,)), pltpu.SemaphoreType.DMA],
        name=name)(x)


def all_to_all(x, name):
    def body(x_ref, o_ref, send_sems, recv_sems, local_sem):
        _, me = _peer(0)
        local = pltpu.make_async_copy(x_ref.at[me], o_ref.at[me], local_sem)
        local.start()
        copies = []
        for k in range(1, N_DEV):
            dev, idx = _peer(k)
            cp = pltpu.make_async_remote_copy(src_ref=x_ref.at[idx], dst_ref=o_ref.at[me], send_sem=send_sems.at[k - 1],
                                              recv_sem=recv_sems.at[k - 1], device_id=dev, device_id_type=MESH)
            cp.start()
            copies.append(cp)
        for cp in copies:
            cp.wait()
        local.wait()

    return pl.pallas_call(
        body, in_specs=[pl.BlockSpec(memory_space=pl.ANY)], out_specs=pl.BlockSpec(memory_space=pl.ANY),
        out_shape=jax.ShapeDtypeStruct(x.shape, x.dtype),
        scratch_shapes=[pltpu.SemaphoreType.DMA((N_DEV - 1,)), pltpu.SemaphoreType.DMA((N_DEV - 1,)), pltpu.SemaphoreType.DMA],
        name=name)(x)


def sum_blocks(x, name):
    _, r, c = x.shape
    tr = _tile(r, max(16, (4 << 20) // (N_DEV * c * x.dtype.itemsize)), 16)

    def body(x_ref, o_ref):
        acc = x_ref[0].astype(F32)
        for i in range(1, N_DEV):
            acc = acc + x_ref[i].astype(F32)
        o_ref[...] = acc

    return pl.pallas_call(
        body, grid=(r // tr,), in_specs=[pl.BlockSpec((N_DEV, tr, c), lambda i: (0, i, 0))],
        out_specs=pl.BlockSpec((tr, c), lambda i: (i, 0)), out_shape=jax.ShapeDtypeStruct((r, c), F32),
        compiler_params=_params(("parallel",)), name=name)(x)


def _adamw(w, g, m, v):
    m = ADAM_B1 * m + (1.0 - ADAM_B1) * g
    v = ADAM_B2 * v + (1.0 - ADAM_B2) * jnp.square(g)
    m_hat = m / (1.0 - ADAM_B1 ** ADAM_STEP)
    v_hat = v / (1.0 - ADAM_B2 ** ADAM_STEP)
    delta = -ADAM_LR * (m_hat / (jnp.sqrt(v_hat) + ADAM_EPS) + ADAM_WD * w)
    return delta, m, v


def adamw(w, g, m, v, name):
    r, c = w.shape
    tr = _tile(r, 256, 8)
    spec = pl.BlockSpec((tr, c), lambda i: (i, 0))

    def body(w_ref, g_ref, m_ref, v_ref, d_ref, nm_ref, nv_ref):
        d_ref[...], nm_ref[...], nv_ref[...] = _adamw(w_ref[...], g_ref[...], m_ref[...], v_ref[...])

    return pl.pallas_call(
        body, grid=(r // tr,), in_specs=[spec] * 4, out_specs=[spec] * 3,
        out_shape=[jax.ShapeDtypeStruct((r, c), F32)] * 3, compiler_params=_params(("parallel",)), name=name)(w, g, m, v)


def batch_sum_rows(dh, bl, t, rows, name):
    d = dh.shape[1]

    def body(x_ref, o_ref):
        @pl.when(pl.program_id(0) == 0)
        def _():
            o_ref[...] = jnp.zeros_like(o_ref)

        o_ref[...] += x_ref[...]

    return pl.pallas_call(
        body, grid=(bl,), in_specs=[pl.BlockSpec((rows, d), lambda b: (b * (t // rows), 0))],
        out_specs=pl.BlockSpec((rows, d), lambda b: (0, 0)), out_shape=jax.ShapeDtypeStruct((rows, d), F32),
        compiler_params=_params(("arbitrary",)), name=name)(dh)


class Dims:
    def __init__(self, x, meta_full_cols, w_up, g_up, q_norm, kv_norm, d_ff):
        self.bl, self.seq, self.d = x.shape
        self.n_meta = 16
        self.t_real = self.n_meta + self.seq
        self.t = -(-self.t_real // LANES) * LANES
        self.n = self.bl * self.t
        self.f = d_ff
        self.wl, self.gl = w_up.shape[-2], g_up.shape[-2]
        self.ql, self.kvl = q_norm.shape[-1], kv_norm.shape[-1]
        self.hm = self.d // V_DIM
        self.in_cols = 5 * self.d + 2 * self.wl + self.gl + self.ql + self.kvl + ROPE_DIM


def _pad_cols(a, width):
    return jnp.pad(a, ((0, 0), (0, width - a.shape[1])))


def _pad_rows(a, rows):
    return jnp.pad(a, ((0, rows - a.shape[0]), (0, 0)))


def split_in(a, dm, axis=1):
    d, wl, gl, ql, kvl = dm.d, dm.wl, dm.gl, dm.ql, dm.kvl
    size = a.shape[axis]
    cut = lambda lo, hi: lax.slice_in_dim(a, min(lo, size), min(hi, size), axis=axis)

    def pad(p, width):
        cfg = [(0, 0)] * a.ndim
        cfg[axis] = (0, width - p.shape[axis])
        return jnp.pad(p, cfg)

    o = 3 * d
    lora = jnp.concatenate([pad(cut(o, o + wl), LANES), pad(cut(o + wl, o + 2 * wl), LANES),
                            cut(o + 2 * wl, o + 2 * wl + gl)], axis=axis)
    o += 2 * wl + gl
    mla = pad(cut(o, o + ql + kvl + ROPE_DIM), ql + kvl + LANES)
    o += ql + kvl + ROPE_DIM
    return dict(r=cut(0, d), k=cut(d, 2 * d), v=cut(2 * d, 3 * d), l=lora, m=mla, ga=cut(o, o + d), gb=cut(o + d, o + 2 * d))


def merge_in(g, dm, axis=1):
    wl, gl, ql, kvl = dm.wl, dm.gl, dm.ql, dm.kvl
    cut = lambda p, lo, hi: lax.slice_in_dim(p, lo, hi, axis=axis)
    l, m = g["l"], g["m"]
    return jnp.concatenate([g["r"], g["k"], g["v"], cut(l, 0, wl), cut(l, LANES, LANES + wl), cut(l, 2 * LANES, 2 * LANES + gl),
                            cut(m, 0, ql + kvl + ROPE_DIM), g["ga"], g["gb"]], axis=axis)


def split_uq(w, dm):
    w3 = w.reshape(w.shape[0], dm.hm, QK_DIM)
    return jnp.concatenate([w3[:, :, :NOPE_DIM].reshape(w.shape[0], -1), w3[:, :, NOPE_DIM:].reshape(w.shape[0], -1)], axis=1)


def merge_uq(gn, gp, dm):
    r = gn.shape[0]
    return jnp.concatenate([gn.reshape(r, dm.hm, NOPE_DIM), gp.reshape(r, dm.hm, ROPE_DIM)], axis=2).reshape(r, -1)


def split_ukv(w, dm):
    w3 = w.reshape(w.shape[0], dm.hm, NOPE_DIM + V_DIM)
    return jnp.concatenate([w3[:, :, :NOPE_DIM].reshape(w.shape[0], -1), w3[:, :, NOPE_DIM:].reshape(w.shape[0], -1)], axis=1)


def merge_ukv(gk, gv, dm):
    r = gk.shape[0]
    return jnp.concatenate([gk.reshape(r, dm.hm, NOPE_DIM), gv.reshape(r, dm.hm, V_DIM)], axis=2).reshape(r, -1)


def head_matrices(d):
    heads = d // RWKV_HEAD
    e = (np.arange(d)[:, None] // RWKV_HEAD == np.arange(LANES)[None, :]) & (np.arange(LANES)[None, :] < heads)
    return jnp.asarray(e, BF16), jnp.asarray(e.T, BF16)


def rope_tables(t):
    pos = jnp.arange(t, dtype=F32)
    inv_freq = 1.0 / (ROPE_THETA ** (jnp.arange(0, ROPE_DIM, 2, dtype=F32) / ROPE_DIM))
    ang = pos[:, None] * inv_freq[None, :]
    cos, sin = jnp.cos(ang), jnp.sin(ang)
    return jnp.tile(jnp.concatenate([cos, cos], axis=1), (1, 2)), jnp.tile(jnp.concatenate([-sin, sin], axis=1), (1, 2))


def local_step(dm, x, loss_target, meta, wt, sp):
    bl, t, n, d, hm = dm.bl, dm.t, dm.n, dm.d, dm.hm
    e, et = head_matrices(d)
    ct, st = rope_tables(t)
    padz = jnp.zeros((bl, t - dm.t_real, d), F32)
    h0 = jnp.concatenate([jnp.broadcast_to(meta[None], (bl, dm.n_meta, d)), x, padz], axis=1).reshape(n, d)
    tgt = jnp.concatenate([jnp.zeros((bl, dm.n_meta, d), F32), loss_target, padz], axis=1).reshape(n, d)
    tpos = jnp.arange(t)
    mask = jnp.tile(((tpos >= dm.n_meta) & (tpos < dm.t_real)).astype(F32), bl).reshape(n, 1)

    win = split_in(wt["w_in"], dm, axis=0)
    mu = split_in(sp["tm_mu"], dm)
    wq, wkv = split_uq(wt["w_uq"], dm), split_ukv(wt["w_ukv"], dm)
    prm = dict(w0=sp["w0"], a0=sp["a0"], k_k=sp["k_k"], k_a=sp["k_a"], gn_w=sp["gn_w"], gn_b=sp["gn_b"], r_k=sp["r_k"],
               w_up=_pad_rows(wt["w_up"], LANES).astype(F32), a_up=_pad_rows(wt["a_up"], LANES).astype(F32),
               g_up=wt["g_up"].astype(F32))

    h1, ffn1 = ffn_forward(h0, sp["ffn1_norm"], wt["ffn1_w_gate"], wt["ffn1_w_up"], wt["ffn1_w_down"], "ffn1")
    u = rms_fwd(h1, sp["mix_norm"], "mix_rms")
    proj = {key: matmul([(u, win[key])], "nt", name=f"proj_{key}") for key in win}
    sh = {key: lerp_fwd(proj[key], mu[key], bl, t, f"shift_{key}") for key in ("r", "k", "v", "l")}
    decay, kmod, kneg, bvec, gate = rwkv_prep_fwd(sh["k"], sh["l"], prm, e, et, "rwkv_prep")
    v_col = to_col(sh["v"], bl, t, d)
    y_col, hist = scan_fwd(sh["r"], decay, kmod, kneg, bvec, v_col, bl, t, d, "wkv_scan")
    y = to_row(y_col, bl, t, d)
    cqn, ckvn = mla_prep_fwd(proj["m"], sp["q_norm"], sp["kv_norm"], "mla_norms")
    q = matmul([(cqn, wq)], "nn", name="mla_q")
    kv = matmul([(ckvn, wkv)], "nn", name="mla_kv")
    o, lse = attn_fwd(q, kv, proj["m"], ct, st, bl, t, hm, "mla_attn")
    post_in = [y, sh["r"], kmod, sh["v"], gate, proj["ga"], proj["gb"], o]
    mix = rwkv_post_fwd(post_in, prm, e, et, "mix_gate")
    h2 = matmul([(mix, wt["w_out"])], "nn", res=h1, name="out_proj")
    h3, ffn2 = ffn_forward(h2, sp["ffn2_norm"], wt["ffn2_w_gate"], wt["ffn2_w_up"], wt["ffn2_w_down"], "ffn2")
    dh3, d_final, loss = loss_head(h3, tgt, mask, sp["final_norm"], "loss_head")

    gw, gs = {}, {"final_norm": d_final}
    dh2, gs["ffn2_norm"], gw["ffn2_w_gate"], gw["ffn2_w_up"], gw["ffn2_w_down"] = ffn_backward(
        dh3, h2, sp["ffn2_norm"], wt["ffn2_w_gate"], wt["ffn2_w_up"], wt["ffn2_w_down"], ffn2, "ffn2")
    dmix = matmul([(dh2, wt["w_out"])], "nt", name="out_proj_dx")
    gw["w_out"] = matmul([(mix, dh2)], "tn", name="out_proj_dw")
    (dy, dr_p, dkm_p, dv_p, dgate, dpga, dpgb, do, gs["gn_w"], gs["gn_b"], gs["r_k"]) = rwkv_post_bwd(
        post_in, prm, e, et, dmix, "mix_gate_bwd")
    dqn, dqpe, dkn, dv_att, dkpe = attn_bwd(q, kv, proj["m"], o, do, lse, ct, st, bl, t, hm, "mla_attn_bwd")
    nq = hm * NOPE_DIM
    dcqn = matmul([(dqn, wq[:, :nq])], "nt", name="mla_q_dx1")
    dcqn = matmul([(dqpe, wq[:, nq:])], "nt", res=dcqn, name="mla_q_dx2")
    gw["w_uq"] = merge_uq(matmul([(cqn, dqn)], "tn", name="mla_q_dw1"), matmul([(cqn, dqpe)], "tn", name="mla_q_dw2"), dm)
    dckvn = matmul([(dkn, wkv[:, :nq]), (dv_att, wkv[:, nq:])], "nt", name="mla_kv_dx")
    gw["w_ukv"] = merge_ukv(matmul([(ckvn, dkn)], "tn", name="mla_kv_dw1"), matmul([(ckvn, dv_att)], "tn", name="mla_kv_dw2"), dm)
    dproj = {"ga": dpga, "gb": dpgb}
    dproj["m"], gs["q_norm"], gs["kv_norm"] = mla_prep_bwd(proj["m"], sp["q_norm"], sp["kv_norm"], dcqn, dckvn, dkpe, "mla_norms_bwd")
    dy_col = to_col(dy, bl, t, d)
    dr_s, ddecay, dk_s, dkneg, dbvec, dv_col = scan_bwd(sh["r"], decay, kmod, kneg, bvec, v_col, dy_col, hist,
                                                        bl, t, d, "wkv_scan_bwd")
    dv_s = to_row(dv_col, bl, t, d)
    (dsh_k, dsh_l, gs["w0"], gs["a0"], gs["k_k"], gs["k_a"], g_wup, g_aup, gw["g_up"]) = rwkv_prep_bwd(
        sh["k"], sh["l"], prm, e, et, [ddecay, dk_s, dkm_p, dkneg, dbvec, dgate], "rwkv_prep_bwd")
    gw["w_up"], gw["a_up"] = g_wup[:dm.wl], g_aup[:dm.wl]
    dmu = {}
    for key, cts in (("r", [dr_s, dr_p]), ("k", [dsh_k]), ("v", [dv_s, dv_p]), ("l", [dsh_l])):
        dproj[key], dmu[key] = lerp_bwd(proj[key], mu[key], cts, bl, t, f"shift_{key}_bwd")
    zero_m = jnp.zeros((1, proj["m"].shape[1]), F32)
    gs["tm_mu"] = merge_in(dict(dmu, m=zero_m, ga=zero_m[:, :0], gb=zero_m[:, :0]), dm)[:, :3 * d + 2 * dm.wl + dm.gl]
    wide = ("r", "k", "v", "ga", "gb")
    du = matmul([(dproj[key], win[key]) for key in wide], "nn", name="proj_dx", tk=512)
    du = matmul([(dproj["l"], win["l"])], "nn", res=du, name="proj_dx_l")
    du = matmul([(dproj["m"], win["m"])], "nn", res=du, name="proj_dx_m")
    gw["w_in"] = merge_in({key: matmul([(dproj[key], u)], "tn", name=f"proj_dw_{key}") for key in win}, dm, axis=0)
    dh1, gs["mix_norm"] = rms_bwd(h1, sp["mix_norm"], du, dh2, "mix_rms_bwd")
    dh0, gs["ffn1_norm"], gw["ffn1_w_gate"], gw["ffn1_w_up"], gw["ffn1_w_down"] = ffn_backward(
        dh1, h0, sp["ffn1_norm"], wt["ffn1_w_gate"], wt["ffn1_w_up"], wt["ffn1_w_down"], ffn1, "ffn1")
    grad_x = dh0.reshape(bl, t, d)[:, dm.n_meta:dm.t_real]
    dmeta = batch_sum_rows(dh0, bl, t, dm.n_meta, "meta_grad")
    return loss, grad_x, dmeta, gw, gs


COL_SHARDED = ("ffn1_w_gate", "ffn1_w_up", "w_in", "w_up", "a_up", "g_up", "w_uq", "w_ukv", "ffn2_w_gate", "ffn2_w_up")
ROW_SHARDED = ("ffn1_w_down", "w_out", "ffn2_w_down")
TRANSPOSED = ("ffn1_w_gate", "ffn1_w_up", "w_in", "ffn2_w_gate", "ffn2_w_up")
WIDE = ("ffn1_w_gate", "ffn1_w_up", "ffn1_w_down", "w_in", "w_out", "ffn2_w_gate", "ffn2_w_up", "ffn2_w_down")
NARROW = ("w_up", "a_up", "g_up", "w_uq", "w_ukv")
MATRICES = ("ffn1_w_gate", "ffn1_w_up", "ffn1_w_down", "w_in", "w_up", "a_up", "g_up", "w_uq", "w_ukv", "w_out",
            "ffn2_w_gate", "ffn2_w_up", "ffn2_w_down")
SMALL = ("ffn1_norm", "mix_norm", "tm_mu", "w0", "a0", "k_k", "k_a", "r_k", "gn_w", "gn_b", "q_norm", "kv_norm",
         "ffn2_norm", "final_norm")
WEIGHTS = ("meta_tokens", "ffn1_norm", "ffn1_w_gate", "ffn1_w_up", "ffn1_w_down", "mix_norm", "w_in", "tm_mu", "w0", "w_up",
           "a0", "a_up", "g_up", "k_k", "k_a", "r_k", "gn_w", "gn_b", "q_norm", "w_uq", "kv_norm", "w_ukv", "w_out",
           "ffn2_norm", "ffn2_w_gate", "ffn2_w_up", "ffn2_w_down", "final_norm")
PACK_COLS = 1024
PACK_ALIGN = 16 * PACK_COLS


def _pack(parts):
    offs, o = [], 0
    for p in parts:
        offs.append(o)
        o += p.shape[1]
    total = -(-o // PACK_ALIGN) * PACK_ALIGN
    flat = jnp.concatenate(list(parts) + [jnp.zeros((parts[0].shape[0], total - o), parts[0].dtype)], axis=1)
    return flat.reshape(parts[0].shape[0], total // PACK_COLS, PACK_COLS), offs


def kernel(x, meta_tokens, ffn1_norm, ffn1_w_gate, ffn1_w_up, ffn1_w_down, mix_norm, w_in, tm_mu, w0, w_up, a0, a_up, g_up, k_k, k_a, r_k, gn_w, gn_b, q_norm, w_uq, kv_norm, w_ukv, w_out, ffn2_norm, ffn2_w_gate, ffn2_w_up, ffn2_w_down, final_norm, loss_target, m_meta_tokens, m_ffn1_norm, m_ffn1_w_gate, m_ffn1_w_up, m_ffn1_w_down, m_mix_norm, m_w_in, m_tm_mu, m_w0, m_w_up, m_a0, m_a_up, m_g_up, m_k_k, m_k_a, m_r_k, m_gn_w, m_gn_b, m_q_norm, m_w_uq, m_kv_norm, m_w_ukv, m_w_out, m_ffn2_norm, m_ffn2_w_gate, m_ffn2_w_up, m_ffn2_w_down, m_final_norm, v_meta_tokens, v_ffn1_norm, v_ffn1_w_gate, v_ffn1_w_up, v_ffn1_w_down, v_mix_norm, v_w_in, v_tm_mu, v_w0, v_w_up, v_a0, v_a_up, v_g_up, v_k_k, v_k_a, v_r_k, v_gn_w, v_gn_b, v_q_norm, v_w_uq, v_kv_norm, v_w_ukv, v_w_out, v_ffn2_norm, v_ffn2_w_gate, v_ffn2_w_up, v_ffn2_w_down, v_final_norm):
    args = dict(locals())
    wts = {k: args[k] for k in WEIGHTS}
    ms = {k: args["m_" + k] for k in WEIGHTS}
    vs = {k: args["v_" + k] for k in WEIGHTS}
    dm = Dims(x, None, w_up, g_up, q_norm, kv_norm, ffn1_w_down.shape[1] * N_DEV)

    shard2d = {k: wts[k].reshape(wts[k].shape[-2], wts[k].shape[-1]) for k in MATRICES}
    sent = {k: shard2d[k].T if k in TRANSPOSED else shard2d[k] for k in MATRICES}
    wide_rows = np.cumsum([0] + [sent[k].shape[0] for k in WIDE])
    got_wide = all_gather(jnp.concatenate([sent[k].astype(MMD) for k in WIDE], axis=0), "gather_wide")
    full = {k: got_wide[:, lo:hi].reshape(-1, dm.d) for k, lo, hi in zip(WIDE, wide_rows[:-1], wide_rows[1:])}
    send, offs = _pack([sent[k].astype(MMD).reshape(1, -1) for k in NARROW])
    got = all_gather(send[0], "gather_narrow").reshape(N_DEV, -1)
    for k, o in zip(NARROW, offs):
        r, c = sent[k].shape
        full[k] = got[:, o:o + r * c].reshape(N_DEV, r, c).transpose(1, 0, 2).reshape(r, N_DEV * c)
    mr, mc = meta_tokens.shape
    meta = all_gather(meta_tokens, "gather_meta").transpose(1, 0, 2).reshape(mr, N_DEV * mc)
    small = {k: wts[k].reshape(1, -1) for k in SMALL}

    loss, grad_x, dmeta, gw, gs = local_step(dm, x, loss_target, meta, full, small)

    gwide = jnp.concatenate([gw[k].reshape(N_DEV, sent[k].shape[0], dm.d) for k in WIDE], axis=1).astype(MMD)
    gsum_wide = sum_blocks(all_to_all(gwide, "scatter_wide"), "sum_wide")
    grads = {}
    for k, lo, hi in zip(WIDE, wide_rows[:-1], wide_rows[1:]):
        grads[k] = gsum_wide[lo:hi].T if k in TRANSPOSED else gsum_wide[lo:hi]

    def blocks(k, g):
        r, c = sent[k].shape
        return g.reshape(r, N_DEV, c).transpose(1, 0, 2).reshape(N_DEV, r * c)

    gsend, goffs = _pack([blocks(k, gw[k]).astype(MMD) for k in NARROW]
                         + [dmeta.reshape(mr, N_DEV, mc).transpose(1, 0, 2).reshape(N_DEV, mr * mc).astype(MMD)])
    gsum = sum_blocks(all_to_all(gsend, "scatter_narrow"), "sum_narrow").reshape(-1)
    for k, o in zip(NARROW, goffs):
        r, c = sent[k].shape
        grads[k] = gsum[o:o + r * c].reshape(r, c)
    grads["meta_tokens"] = gsum[goffs[-1]:goffs[-1] + mr * mc].reshape(mr, mc)

    ssend, soffs = _pack([gs[k].reshape(1, -1) for k in SMALL] + [loss])
    ssum = sum_blocks(all_gather(ssend[0], "gather_small"), "sum_small").reshape(-1)
    for k, o in zip(SMALL, soffs):
        grads[k] = ssum[o:o + small[k].shape[1]]
    loss_total = ssum[soffs[-1]]

    delta, new_m, new_v = {}, {}, {}
    for k in MATRICES + ("meta_tokens",):
        shp = wts[k].shape
        to2d = lambda a: a.reshape(shp[-2], shp[-1])
        dlt, nm, nv = adamw(to2d(wts[k]), grads[k], to2d(ms[k]), to2d(vs[k]), f"adamw_{k}")
        delta[k], new_m[k], new_v[k] = dlt.reshape(shp), nm.reshape(shp), nv.reshape(shp)
        grads[k] = grads[k].reshape(shp)
    pw, _ = _pack([wts[k].reshape(1, -1) for k in SMALL])
    pm_, _ = _pack([ms[k].reshape(1, -1) for k in SMALL])
    pv, _ = _pack([vs[k].reshape(1, -1) for k in SMALL])
    pg, poffs = _pack([grads[k].reshape(1, -1) for k in SMALL])
    dlt, nm, nv = adamw(pw[0], pg[0], pm_[0], pv[0], "adamw_small")
    for k, o in zip(SMALL, poffs):
        shp, sz = wts[k].shape, small[k].shape[1]
        cut = lambda a: a.reshape(-1)[o:o + sz].reshape(shp)
        delta[k], new_m[k], new_v[k] = cut(dlt), cut(nm), cut(nv)
        grads[k] = grads[k].reshape(shp)

    return (loss_total, grad_x, *[grads[k] for k in WEIGHTS], *[delta[k] for k in WEIGHTS],
            *[new_m[k] for k in WEIGHTS], *[new_v[k] for k in WEIGHTS])
```

```python
import functools

import numpy as np
import jax
import jax.numpy as jnp
from jax import lax
from jax.experimental import pallas as pl
from jax.experimental.pallas import tpu as pltpu

F32 = jnp.float32
BF16 = jnp.bfloat16
MMD = BF16

NORM_EPS = 1e-6
RWKV_HEAD = 64
GN_EPS = RWKV_HEAD * 1e-5
NOPE_DIM = 128
ROPE_DIM = 64
V_DIM = 128
QK_DIM = NOPE_DIM + ROPE_DIM
ROPE_THETA = 10000.0
ADAM_LR = 0.001
ADAM_B1 = 0.9
ADAM_B2 = 0.999
ADAM_EPS = 1e-08
ADAM_WD = 0.01
ADAM_STEP = 10

LANES = 128
TCH = 64
SCAN_PAIRS = 8
SCAN_FWD_STEPS = 32
SCAN_BWD_STEPS = 16
N_DEV = 8
VMEM_LIMIT = 56 * 1024 * 1024
MESH = pl.DeviceIdType.MESH


def _tile(n, target, align):
    best = None
    for d in range(align, min(n, target) + 1, align):
        if n % d == 0:
            best = d
    return best if best is not None else n


def _params(sem=None):
    return pltpu.CompilerParams(dimension_semantics=sem, vmem_limit_bytes=VMEM_LIMIT)


def _mm(a, b, dims=((1,), (0,))):
    return lax.dot_general(a.astype(MMD), b.astype(MMD), (dims, ((), ())), preferred_element_type=F32)


@jax.custom_vjp
def mmdot(a, b):
    return _mm(a, b)


def _mmdot_fwd(a, b):
    return _mm(a, b), (a, b)


def _mmdot_bwd(res, g):
    a, b = res
    return _mm(g, b, ((1,), (1,))).astype(a.dtype), _mm(a, g, ((0,), (0,))).astype(b.dtype)


mmdot.defvjp(_mmdot_fwd, _mmdot_bwd)


def _dot2(x, m):
    hi = x.astype(BF16)
    lo = (x - hi.astype(F32)).astype(BF16)
    return (lax.dot_general(hi, m, (((1,), (0,)), ((), ())), preferred_element_type=F32)
            + lax.dot_general(lo, m, (((1,), (0,)), ((), ())), preferred_element_type=F32))


@jax.custom_vjp
def segsum(x, e, et):
    return _dot2(_dot2(x, e), et)


def _segsum_fwd(x, e, et):
    return segsum(x, e, et), (e, et)


def _segsum_bwd(res, g):
    e, et = res
    return segsum(g, e, et), jnp.zeros_like(e), jnp.zeros_like(et)


segsum.defvjp(_segsum_fwd, _segsum_bwd)


def _sigmoid(x):
    return 1.0 / (1.0 + jnp.exp(-x))


def _softplus(x):
    return jnp.maximum(x, 0.0) + jnp.log(1.0 + jnp.exp(-jnp.abs(x)))


def _rms(x, g):
    return x * lax.rsqrt(jnp.mean(x * x, axis=-1, keepdims=True) + NORM_EPS) * g


_DIMS = {"nn": ((1,), (0,)), "nt": ((1,), (1,)), "tn": ((0,), (0,))}


def matmul(pairs, mode, *, name, out_dtype=F32, res=None, alpha=1.0, tm=1088, tn=512, tk=1024):
    a0, b0 = pairs[0]
    if mode == "nn":
        (m, k), n = a0.shape, b0.shape[1]
    elif mode == "nt":
        (m, k), n = a0.shape, b0.shape[0]
    else:
        (k, m), n = a0.shape, b0.shape[1]
    tm = _tile(m, tm, 128 if mode == "tn" else 16)
    tn = _tile(n, 2048 if mode == "tn" else tn, 128)
    tk = _tile(k, tk, 16 if mode == "tn" else 128)
    nk = k // tk
    npair = len(pairs)
    if mode == "tn":
        a_spec = pl.BlockSpec((tk, tm), lambda i, j, kk: (kk, i))
    else:
        a_spec = pl.BlockSpec((tm, tk), lambda i, j, kk: (i, kk))
    if mode == "nt":
        b_spec = pl.BlockSpec((tn, tk), lambda i, j, kk: (j, kk))
    else:
        b_spec = pl.BlockSpec((tk, tn), lambda i, j, kk: (kk, j))
    o_spec = pl.BlockSpec((tm, tn), lambda i, j, kk: (i, j))
    dims = _DIMS[mode]

    def body(*refs):
        ab = refs[:2 * npair]
        res_ref = refs[2 * npair] if res is not None else None
        o_ref, acc_ref = refs[-2], refs[-1]
        kk = pl.program_id(2)

        @pl.when(kk == 0)
        def _():
            acc_ref[...] = jnp.zeros_like(acc_ref)

        part = _mm(ab[0][...], ab[1][...], dims)
        for p in range(1, npair):
            part = part + _mm(ab[2 * p][...], ab[2 * p + 1][...], dims)
        acc_ref[...] += part

        @pl.when(kk == nk - 1)
        def _():
            out = acc_ref[...] * alpha if alpha != 1.0 else acc_ref[...]
            if res_ref is not None:
                out = res_ref[...].astype(F32) + out
            o_ref[...] = out.astype(o_ref.dtype)

    args, specs = [], []
    for a, b in pairs:
        args += [a, b]
        specs += [a_spec, b_spec]
    if res is not None:
        args.append(res)
        specs.append(o_spec)
    return pl.pallas_call(
        body, grid=(m // tm, n // tn, nk), in_specs=specs, out_specs=o_spec,
        out_shape=jax.ShapeDtypeStruct((m, n), out_dtype), scratch_shapes=[pltpu.VMEM((tm, tn), F32)],
        compiler_params=_params(("parallel", "parallel", "arbitrary")), name=name)(*args)


def tilek(fn, ins, outs, *, n_rows, tr, name):
    tr = _tile(n_rows, tr, 16)
    n_in = len(ins)
    in_specs = []
    for arr, kind in ins:
        if kind == "r":
            in_specs.append(pl.BlockSpec((tr, arr.shape[1]), lambda i: (i, 0)))
        else:
            in_specs.append(pl.BlockSpec(arr.shape, lambda i, nd=arr.ndim: (0,) * nd))
    out_specs, out_shape = [], []
    has_acc = False
    for o in outs:
        if o[0] == "r":
            out_specs.append(pl.BlockSpec((tr, o[1]), lambda i: (i, 0)))
            out_shape.append(jax.ShapeDtypeStruct((n_rows, o[1]), o[2]))
        else:
            has_acc = True
            out_specs.append(pl.BlockSpec(o[1], lambda i, nd=len(o[1]): (0,) * nd))
            out_shape.append(jax.ShapeDtypeStruct(o[1], F32))

    def body(*refs):
        i = pl.program_id(0)
        vals = fn(*[r[...] for r in refs[:n_in]])
        for o, r, v in zip(outs, refs[n_in:], vals):
            if o[0] == "r":
                r[...] = v.astype(r.dtype)
            else:
                @pl.when(i == 0)
                def _(r=r):
                    r[...] = jnp.zeros_like(r)

                r[...] += v

    return pl.pallas_call(
        body, grid=(n_rows // tr,), in_specs=in_specs, out_specs=out_specs, out_shape=out_shape,
        compiler_params=_params(("arbitrary",) if has_acc else ("parallel",)), name=name)(*[a for a, _ in ins])


def rms_fwd(x, g, name):
    n, d = x.shape
    return tilek(lambda xv, gv: (_rms(xv, gv),), [(x, "r"), (g, "f")], [("r", d, MMD)], n_rows=n, tr=256, name=name)[0]


def rms_bwd(x, g, dy, dres, name):
    n, d = x.shape

    def fn(xv, gv, dyv, drv):
        _, vjp = jax.vjp(_rms, xv, gv)
        dx, dg = vjp(dyv.astype(F32))
        return drv + dx, dg

    return tilek(fn, [(x, "r"), (g, "f"), (dy, "r"), (dres, "r")], [("r", d, F32), ("acc", (1, d))],
                 n_rows=n, tr=128, name=name)


def loss_head(h, tgt, mask, g, name):
    n, d = h.shape

    def fn(hv, tv, mv, gv):
        def lossf(hh, gg):
            e = (_rms(hh, gg) - tv) * mv
            s = jnp.sum(jnp.sum(e * e, axis=1, keepdims=True), axis=0, keepdims=True)
            return s * (0.5 / d)

        l, vjp = jax.vjp(lossf, hv, gv)
        dh, dg = vjp(jnp.ones((1, 1), F32))
        return dh, dg, jnp.broadcast_to(l, (1, LANES))

    return tilek(fn, [(h, "r"), (tgt, "r"), (mask, "r"), (g, "f")],
                 [("r", d, F32), ("acc", (1, d)), ("acc", (1, LANES))], n_rows=n, tr=128, name=name)


def ffn_up(hn, wg, wu, name):
    n, d = hn.shape
    f = wg.shape[0]
    tm, tn = _tile(n, 544, 16), _tile(f, 512, 128)

    def body(a_ref, g_ref, u_ref, og_ref, ou_ref, oa_ref):
        a = a_ref[...]
        g = _mm(a, g_ref[...], ((1,), (1,)))
        u = _mm(a, u_ref[...], ((1,), (1,)))
        og_ref[...] = g
        ou_ref[...] = u
        oa_ref[...] = (g * _sigmoid(g) * u).astype(oa_ref.dtype)

    o_spec = pl.BlockSpec((tm, tn), lambda i, j: (i, j))
    w_spec = pl.BlockSpec((tn, d), lambda i, j: (j, 0))
    return pl.pallas_call(
        body, grid=(n // tm, f // tn), in_specs=[pl.BlockSpec((tm, d), lambda i, j: (i, 0)), w_spec, w_spec],
        out_specs=[o_spec, o_spec, o_spec],
        out_shape=[jax.ShapeDtypeStruct((n, f), F32), jax.ShapeDtypeStruct((n, f), F32), jax.ShapeDtypeStruct((n, f), MMD)],
        compiler_params=_params(("parallel", "parallel")), name=name)(hn, wg, wu)


def ffn_down_bwd(dh, wd, gate, up, name):
    n, d = dh.shape
    f = wd.shape[0]
    tm, tn = _tile(n, 544, 16), _tile(f, 512, 128)

    def body(dh_ref, w_ref, g_ref, u_ref, dg_ref, du_ref):
        da = 0.5 * _mm(dh_ref[...], w_ref[...], ((1,), (1,)))
        g, u = g_ref[...], u_ref[...]
        s = _sigmoid(g)
        dg_ref[...] = (da * u * (s * (1.0 + g * (1.0 - s)))).astype(dg_ref.dtype)
        du_ref[...] = (da * (g * s)).astype(du_ref.dtype)

    o_spec = pl.BlockSpec((tm, tn), lambda i, j: (i, j))
    return pl.pallas_call(
        body, grid=(n // tm, f // tn),
        in_specs=[pl.BlockSpec((tm, d), lambda i, j: (i, 0)), pl.BlockSpec((tn, d), lambda i, j: (j, 0)), o_spec, o_spec],
        out_specs=[o_spec, o_spec],
        out_shape=[jax.ShapeDtypeStruct((n, f), MMD), jax.ShapeDtypeStruct((n, f), MMD)],
        compiler_params=_params(("parallel", "parallel")), name=name)(dh, wd, gate, up)


def ffn_forward(h, g, wg, wu, wd, tag):
    hn = rms_fwd(h, g, f"{tag}_rms")
    gate, up, act = ffn_up(hn, wg, wu, f"{tag}_up")
    out = matmul([(act, wd)], "nn", res=h, alpha=0.5, name=f"{tag}_down")
    return out, (hn, gate, up, act)


def ffn_backward(dout, h, g, wg, wu, wd, saved, tag):
    hn, gate, up, act = saved
    dgate, dup = ffn_down_bwd(dout, wd, gate, up, f"{tag}_dact")
    dwd = matmul([(act, dout)], "tn", alpha=0.5, name=f"{tag}_dwd")
    dwg = matmul([(dgate, hn)], "tn", name=f"{tag}_dwg")
    dwu = matmul([(dup, hn)], "tn", name=f"{tag}_dwu")
    dhn = matmul([(dgate, wg), (dup, wu)], "nn", name=f"{tag}_dhn")
    dh, dg = rms_bwd(h, g, dhn, dout, f"{tag}_drms")
    return dh, dg, dwg, dwu, dwd


def lerp_fwd(p, mu, bl, t, name):
    n, w = p.shape
    cb = _tile(w, 256, 128)

    def body(p_ref, mu_ref, o_ref):
        x = p_ref[...]
        row = lax.broadcasted_iota(jnp.int32, x.shape, 0)
        prev = jnp.where(row == 0, 0.0, pltpu.roll(x, 1, 0))
        o_ref[...] = x + mu_ref[...] * (prev - x)

    spec = pl.BlockSpec((t, cb), lambda b, j: (b, j))
    return pl.pallas_call(
        body, grid=(bl, w // cb), in_specs=[spec, pl.BlockSpec((1, cb), lambda b, j: (0, j))], out_specs=spec,
        out_shape=jax.ShapeDtypeStruct((n, w), F32), compiler_params=_params(("parallel", "parallel")), name=name)(p, mu)


def lerp_bwd(p, mu, douts, bl, t, name):
    n, w = p.shape
    cb = _tile(w, 256, 128)
    nd = len(douts)

    def body(*refs):
        p_ref, mu_ref = refs[0], refs[1]
        dp_ref, dmu_ref = refs[2 + nd], refs[3 + nd]
        b = pl.program_id(1)
        x, m = p_ref[...], mu_ref[...]
        d = refs[2][...]
        for r in refs[3:2 + nd]:
            d = d + r[...]
        row = lax.broadcasted_iota(jnp.int32, x.shape, 0)
        prev = jnp.where(row == 0, 0.0, pltpu.roll(x, 1, 0))
        z = d * m
        nxt = jnp.where(row == t - 1, 0.0, pltpu.roll(z, t - 1, 0))
        dp_ref[...] = d - z + nxt

        @pl.when(b == 0)
        def _():
            dmu_ref[...] = jnp.zeros_like(dmu_ref)

        dmu_ref[...] += jnp.sum(d * (prev - x), axis=0, keepdims=True)

    spec = pl.BlockSpec((t, cb), lambda j, b: (b, j))
    cspec = pl.BlockSpec((1, cb), lambda j, b: (0, j))
    return pl.pallas_call(
        body, grid=(w // cb, bl), in_specs=[spec, cspec] + [spec] * nd, out_specs=[spec, cspec],
        out_shape=[jax.ShapeDtypeStruct((n, w), F32), jax.ShapeDtypeStruct((1, w), F32)],
        compiler_params=_params(("parallel", "arbitrary")), name=name)(p, mu, *douts)


def _prep(k, xw, xa, xg, w0, a0, k_k, k_a, w_up, a_up, g_up, e, et):
    w_pre = -_softplus(-(w0 + mmdot(jnp.tanh(xw), w_up))) - 0.5
    decay = jnp.exp(-jnp.exp(w_pre))
    a = _sigmoid(a0 + mmdot(xa, a_up))
    g = mmdot(_sigmoid(xg), g_up)
    kk = k * k_k
    kk = kk * lax.rsqrt(jnp.maximum(segsum(kk * kk, e, et), 1e-24))
    kmod = k * (1.0 + (a - 1.0) * k_a)
    return decay, kmod, -kk, kk * a, g


def _lora_parts(xl):
    return xl[:, :LANES], xl[:, LANES:2 * LANES], xl[:, 2 * LANES:]


def rwkv_prep_fwd(pk, pl_, prm, e, et, name):
    n, d = pk.shape
    small = [prm[k] for k in ("w0", "a0", "k_k", "k_a", "w_up", "a_up", "g_up")]
    ins = [(pk, "r"), (pl_, "r")] + [(s, "f") for s in small] + [(e, "f"), (et, "f")]
    return tilek(lambda k, xl, *rest: _prep(k, *_lora_parts(xl), *rest), ins, [("r", d, F32)] * 5, n_rows=n, tr=128, name=name)


def rwkv_prep_bwd(pk, pl_, prm, e, et, cts, name):
    n, d = pk.shape
    small = [prm[k] for k in ("w0", "a0", "k_k", "k_a", "w_up", "a_up", "g_up")]

    def fn(k, xl, w0, a0, k_k, k_a, w_up, a_up, g_up, ev, etv, dw, dkm1, dkm2, dkn, db, dg):
        _, vjp = jax.vjp(lambda *a: _prep(*a, ev, etv), k, *_lora_parts(xl), w0, a0, k_k, k_a, w_up, a_up, g_up)
        dk, dxw, dxa, dxg, *dsmall = vjp((dw, dkm1 + dkm2, dkn, db, dg))
        return (dk, jnp.concatenate([dxw, dxa, dxg], axis=1), *dsmall)

    ins = [(pk, "r"), (pl_, "r")] + [(s, "f") for s in small] + [(e, "f"), (et, "f")] + [(c, "r") for c in cts]
    outs = [("r", d, F32), ("r", pl_.shape[1], F32)] + [("acc", s.shape) for s in small]
    return tilek(fn, ins, outs, n_rows=n, tr=64, name=name)


def _post(y, r, km, v, g, pga, pgb, yb, gn_w, gn_b, r_k, e, et):
    inv = 1.0 / RWKV_HEAD
    yc = y - segsum(y, e, et) * inv
    var = segsum(yc * yc, e, et) * inv
    yn = yc * lax.rsqrt(var + GN_EPS) * gn_w + gn_b
    bonus = segsum(r * km * r_k, e, et) * v
    ya = (yn + bonus) * g
    return _sigmoid(pga) * ya + _sigmoid(pgb) * yb


def rwkv_post_fwd(acts, prm, e, et, name):
    n, d = acts[0].shape
    small = [prm[k] for k in ("gn_w", "gn_b", "r_k")]
    ins = [(a, "r") for a in acts] + [(s, "f") for s in small] + [(e, "f"), (et, "f")]
    return tilek(lambda *a: (_post(*a),), ins, [("r", d, MMD)], n_rows=n, tr=128, name=name)[0]


def rwkv_post_bwd(acts, prm, e, et, dm, name):
    n, d = acts[0].shape
    small = [prm[k] for k in ("gn_w", "gn_b", "r_k")]
    na = len(acts)

    def fn(*a):
        prim, ev, etv, dmv = a[:na + 3], a[na + 3], a[na + 4], a[na + 5]
        _, vjp = jax.vjp(lambda *z: _post(*z, ev, etv), *prim)
        return vjp(dmv.astype(F32))

    ins = [(x, "r") for x in acts] + [(s, "f") for s in small] + [(e, "f"), (et, "f"), (dm, "r")]
    outs = [("r", d, F32)] * na + [("acc", s.shape) for s in small]
    return tilek(fn, ins, outs, n_rows=n, tr=64, name=name)


def to_row(a, bl, t, d):
    p, c = d // LANES, t // TCH
    a = a.reshape(bl, p, c, RWKV_HEAD, 2, TCH).transpose(0, 2, 5, 1, 4, 3)
    return a.reshape(bl * t, d)


def _head_sums(x, first_head):
    a = jnp.sum(jnp.where(first_head, x, 0.0), axis=1, keepdims=True)
    b = jnp.sum(jnp.where(first_head, 0.0, x), axis=1, keepdims=True)
    return jnp.where(first_head, a, b)


def _split2(x):
    hi = x.astype(BF16)
    return hi, (x - hi.astype(F32)).astype(BF16)


def _spread(row, eye2):
    hi, lo = _split2(row)
    return eye2 * hi, eye2 * lo


def _ones_dot(tiles, ones_blk):
    hi = jnp.concatenate([t[0] for t in tiles], axis=0)
    lo = jnp.concatenate([t[1] for t in tiles], axis=0)
    dims = (((1,), (0,)), ((), ()))
    res = (lax.dot_general(hi, ones_blk, dims, preferred_element_type=F32)
           + lax.dot_general(lo, ones_blk, dims, preferred_element_type=F32))
    return [res[i * RWKV_HEAD:(i + 1) * RWKV_HEAD] for i in range(len(tiles))]


def _scan_consts():
    lane = lax.broadcasted_iota(jnp.int32, (1, LANES), 1)
    rows = lax.broadcasted_iota(jnp.int32, (RWKV_HEAD, LANES), 0)
    cols = lax.broadcasted_iota(jnp.int32, (RWKV_HEAD, LANES), 1)
    eye2 = ((cols & (RWKV_HEAD - 1)) == rows).astype(BF16)
    r2 = lax.broadcasted_iota(jnp.int32, (LANES, LANES), 0)
    c2 = lax.broadcasted_iota(jnp.int32, (LANES, LANES), 1)
    ones_blk = ((r2 // RWKV_HEAD) == (c2 // RWKV_HEAD)).astype(BF16)
    return lane, lane < RWKV_HEAD, eye2, ones_blk


def scan_forward(r, w, k, kn, b, v, bl, t, d, name, pg, hch):
    npair, nst, nsub = d // LANES, t // hch, TCH // hch

    def body(r_ref, w_ref, k_ref, kn_ref, b_ref, v_ref, y_ref, hist_ref, s_ref, vb_ref):
        c = pl.program_id(2)
        off = (c % nsub) * hch
        lane, first_head, eye2, ones_blk = _scan_consts()
        step_lane = lane & (TCH - 1)
        eye2f = eye2.astype(F32)

        @pl.when(c == 0)
        def _():
            s_ref[...] = jnp.zeros_like(s_ref)

        @pl.when(c % nsub == 0)
        def _():
            y_ref[...] = jnp.zeros_like(y_ref)

        for p in range(pg):
            cols = slice(p * LANES, (p + 1) * LANES)
            vb_ref[p] = _head_sums(eye2f * v_ref[0, :, cols], first_head)

        def step(ts, carry):
            prev, nxt = jnp.maximum(ts - 1, 0), jnp.minimum(ts + 1, hch - 1)
            sel_prev = jnp.where(ts > 0, (step_lane == off + ts - 1).astype(F32), 0.0)
            states, tiles = [], []
            for p in range(pg):
                cols = slice(p * LANES, (p + 1) * LANES)
                s = s_ref[p]
                hist_ref[0, p, pl.ds(ts, 1)] = s[None]
                states.append(s)
                tiles.append(_split2(s * r_ref[prev, :, cols]))
                tiles.append(_spread(v_ref[nxt, :, cols], eye2))
            res = _ones_dot(tiles, ones_blk)
            for p in range(pg):
                cols = slice(p * LANES, (p + 1) * LANES)
                s = states[p]
                sa = _head_sums(s * kn_ref[ts, :, cols], first_head)
                s_ref[p] = s * w_ref[ts, :, cols] + sa * b_ref[ts, :, cols] + vb_ref[p] * k_ref[ts, :, cols]
            for p in range(pg):
                y_ref[0, p, 0] += res[2 * p] * sel_prev
                vb_ref[p] = res[2 * p + 1]
            return carry

        lax.fori_loop(0, hch, step, 0)
        last = (step_lane == off + hch - 1).astype(F32)
        for p in range(pg):
            cols = slice(p * LANES, (p + 1) * LANES)
            y_ref[0, p, 0] += _head_sums(s_ref[p] * r_ref[hch - 1, :, cols], first_head) * last

    row_spec = pl.BlockSpec((hch, 1, pg * LANES), lambda bb, g, c: (bb * nst + c, 0, g))
    col_spec = pl.BlockSpec((1, pg, 1, RWKV_HEAD, LANES), lambda bb, g, c: (bb, g, c // nsub, 0, 0))
    hist_spec = pl.BlockSpec((1, pg, hch, RWKV_HEAD, LANES), lambda bb, g, c: (bb, g, c, 0, 0))
    rows3 = [a.reshape(bl * t, 1, d) for a in (r, w, k, kn, b, v)]
    return pl.pallas_call(
        body, grid=(bl, npair // pg, nst), in_specs=[row_spec] * 6, out_specs=[col_spec, hist_spec],
        out_shape=[jax.ShapeDtypeStruct((bl, npair, t // TCH, RWKV_HEAD, LANES), F32),
                   jax.ShapeDtypeStruct((bl, npair, t, RWKV_HEAD, LANES), F32)],
        scratch_shapes=[pltpu.VMEM((pg, RWKV_HEAD, LANES), F32)] * 2,
        compiler_params=_params(("parallel", "parallel", "arbitrary")), name=name)(*rows3)


def scan_backward(r, w, k, kn, b, v, dy, hist, bl, t, d, name, pg, hch):
    npair, nst, nsub = d // LANES, t // hch, TCH // hch

    def body(r_ref, w_ref, k_ref, kn_ref, b_ref, v_ref, dy_ref, hist_ref,
             dr_ref, dw_ref, dk_ref, dkn_ref, db_ref, dv_ref, ds_ref, cur_ref):
        c = pl.program_id(2)
        sub = (nst - 1 - c) % nsub
        off = sub * hch
        lane, first_head, eye2, ones_blk = _scan_consts()
        step_lane = lane & (TCH - 1)
        eye2f = eye2.astype(F32)
        colsum = lambda x: jnp.sum(x, axis=0, keepdims=True)

        @pl.when(c == 0)
        def _():
            ds_ref[...] = jnp.zeros_like(ds_ref)

        @pl.when(sub == nsub - 1)
        def _():
            dv_ref[...] = jnp.zeros_like(dv_ref)

        for p in range(pg):
            cols = slice(p * LANES, (p + 1) * LANES)
            dyb = _head_sums(eye2f * dy_ref[hch - 1, :, cols], first_head)
            cur_ref[0, p] = _head_sums(eye2f * v_ref[hch - 1, :, cols], first_head)
            cur_ref[1, p] = dyb
            cur_ref[2, p] = _head_sums(hist_ref[0, p, hch - 1] * kn_ref[hch - 1, :, cols], first_head)
            ds_ref[p] += dyb * r_ref[hch - 1, :, cols]

        def step(it, carry):
            ts = hch - 1 - it
            prev = jnp.maximum(ts - 1, 0)
            has_prev = jnp.where(ts > 0, 1.0, 0.0)
            sel = (step_lane == off + ts).astype(F32)
            grads, tiles = [], []
            for p in range(pg):
                cols = slice(p * LANES, (p + 1) * LANES)
                ds = ds_ref[p]
                grads.append(ds)
                tiles.append(_spread(v_ref[prev, :, cols], eye2))
                tiles.append(_spread(dy_ref[prev, :, cols], eye2))
                tiles.append(_split2(hist_ref[0, p, pl.ds(prev, 1)][0] * kn_ref[prev, :, cols]))
                tiles.append(_split2(ds * k_ref[ts, :, cols]))
            res = _ones_dot(tiles, ones_blk)
            for p in range(pg):
                cols = slice(p * LANES, (p + 1) * LANES)
                row = lambda ref: ref[ts, :, cols]
                ds = grads[p]
                w_, kn_, b_ = row(w_ref), row(kn_ref), row(b_ref)
                dsa = _head_sums(ds * b_, first_head)
                s_prev = hist_ref[0, p, pl.ds(ts, 1)][0]
                vb, dyb, sa = cur_ref[0, p], cur_ref[1, p], cur_ref[2, p]
                s_t = s_prev * w_ + sa * b_ + vb * row(k_ref)
                dr_ref[ts, :, cols] = colsum(s_t * dyb)
                dk_ref[ts, :, cols] = colsum(ds * vb)
                db_ref[ts, :, cols] = colsum(ds * sa)
                dw_ref[ts, :, cols] = colsum(ds * s_prev)
                dkn_ref[ts, :, cols] = colsum(s_prev * dsa)
                ds_ref[p] = ds * w_ + dsa * kn_ + (res[4 * p + 1] * has_prev) * r_ref[prev, :, cols]
            for p in range(pg):
                cur_ref[0, p] = res[4 * p]
                cur_ref[1, p] = res[4 * p + 1]
                cur_ref[2, p] = res[4 * p + 2]
                dv_ref[0, p, 0] += res[4 * p + 3] * sel
            return carry

        lax.fori_loop(0, hch, step, 0)

    row_spec = pl.BlockSpec((hch, 1, pg * LANES), lambda bb, g, c: (bb * nst + nst - 1 - c, 0, g))
    col_spec = pl.BlockSpec((1, pg, 1, RWKV_HEAD, LANES), lambda bb, g, c: (bb, g, (nst - 1 - c) // nsub, 0, 0))
    hist_spec = pl.BlockSpec((1, pg, hch, RWKV_HEAD, LANES), lambda bb, g, c: (bb, g, nst - 1 - c, 0, 0))
    row_shape = jax.ShapeDtypeStruct((bl * t, 1, d), F32)
    rows3 = [a.reshape(bl * t, 1, d) for a in (r, w, k, kn, b, v, dy)]
    outs = pl.pallas_call(
        body, grid=(bl, npair // pg, nst), in_specs=[row_spec] * 7 + [hist_spec], out_specs=[row_spec] * 5 + [col_spec],
        out_shape=[row_shape] * 5 + [jax.ShapeDtypeStruct((bl, npair, t // TCH, RWKV_HEAD, LANES), F32)],
        scratch_shapes=[pltpu.VMEM((pg, RWKV_HEAD, LANES), F32), pltpu.VMEM((3, pg, RWKV_HEAD, LANES), F32)],
        compiler_params=_params(("parallel", "parallel", "arbitrary")), name=name)(*rows3, hist)
    return [o.reshape(bl * t, d) for o in outs[:5]] + [outs[5]]


def _mla_norms(pm, gq, gkv):
    ql = gq.shape[1]
    kvl = gkv.shape[1]
    return _rms(pm[:, :ql], gq), _rms(pm[:, ql:ql + kvl], gkv)


def mla_prep_fwd(pm, gq, gkv, name):
    n = pm.shape[0]
    return tilek(_mla_norms, [(pm, "r"), (gq, "f"), (gkv, "f")],
                 [("r", gq.shape[1], MMD), ("r", gkv.shape[1], MMD)], n_rows=n, tr=256, name=name)


def mla_prep_bwd(pm, gq, gkv, dcq, dckv, dkpe, name):
    n, wm = pm.shape
    ql, kvl = gq.shape[1], gkv.shape[1]

    def fn(pmv, gqv, gkvv, d1, d2, d3):
        _, vjp1 = jax.vjp(_rms, pmv[:, :ql], gqv)
        _, vjp2 = jax.vjp(_rms, pmv[:, ql:ql + kvl], gkvv)
        dcq_in, dgq = vjp1(d1)
        dckv_in, dgkv = vjp2(d2)
        return jnp.concatenate([dcq_in, dckv_in, d3], axis=1), dgq, dgkv

    return tilek(fn, [(pm, "r"), (gq, "f"), (gkv, "f"), (dcq, "r"), (dckv, "r"), (dkpe, "r")],
                 [("r", wm, F32), ("acc", gq.shape), ("acc", gkv.shape)], n_rows=n, tr=128, name=name)


def _rope(x, c, s, first):
    sw = jnp.where(first, pltpu.roll(x, LANES - ROPE_DIM // 2, 1), pltpu.roll(x, ROPE_DIM // 2, 1))
    return x * c + sw * s


def _unrope(d, c, s, first):
    z = d * s
    sw = jnp.where(first, pltpu.roll(z, LANES - ROPE_DIM // 2, 1), pltpu.roll(z, ROPE_DIM // 2, 1))
    return d * c + sw


def attn_fwd(q, kv, pm, ct, st, bl, t, hm, name):
    n = q.shape[0]
    tq = LANES
    scale = QK_DIM ** -0.5
    kpe_blk = pm.shape[1] // LANES - 1

    def body(qn_ref, qpe_ref, kn_ref, v_ref, kpe_ref, ct_ref, st_ref, o_ref, lse_ref, kp_s, kn_s, v_s):
        h = pl.program_id(1)
        lane = lax.broadcasted_iota(jnp.int32, (1, LANES), 1)
        first = (lane & (ROPE_DIM - 1)) < ROPE_DIM // 2
        kp = _rope(kpe_ref[...], ct_ref[...], st_ref[...], first)
        kp_s[...] = jnp.where(h % 2 == 0, kp, pltpu.roll(kp, ROPE_DIM, 1)).astype(MMD)
        kn_s[...] = kn_ref[...].astype(MMD)
        v_s[...] = v_ref[...].astype(MMD)
        kpos = lax.broadcasted_iota(jnp.int32, (1, t), 1)

        def qtile(i, carry):
            rows = pl.ds(pl.multiple_of(i * tq, tq), tq)
            q2 = _rope(qpe_ref[rows, :], ct_ref[rows, :], st_ref[rows, :], first)
            s = (_mm(qn_ref[rows, :], kn_s[...], ((1,), (1,))) + _mm(q2, kp_s[...], ((1,), (1,)))) * scale
            qpos = i * tq + lax.broadcasted_iota(jnp.int32, (tq, 1), 0)
            s = jnp.where(kpos <= qpos, s, -1e30)
            m = jnp.max(s, axis=1, keepdims=True)
            p = jnp.exp(s - m)
            l = jnp.sum(p, axis=1, keepdims=True)
            o_ref[rows, :] = _mm(p, v_s[...]) / l
            lse_ref[0, 0, rows, :] = m + jnp.log(l)
            return carry

        lax.fori_loop(0, t // tq, qtile, 0)

    blk = lambda f: pl.BlockSpec((t, LANES), f)
    return pl.pallas_call(
        body, grid=(bl, hm),
        in_specs=[blk(lambda b, h: (b, h)), blk(lambda b, h: (b, hm + h // 2)), blk(lambda b, h: (b, h)),
                  blk(lambda b, h: (b, hm + h)), blk(lambda b, h: (b, kpe_blk)), blk(lambda b, h: (0, 0)), blk(lambda b, h: (0, 0))],
        out_specs=[blk(lambda b, h: (b, h)), pl.BlockSpec((1, 1, t, 1), lambda b, h: (b, h, 0, 0))],
        out_shape=[jax.ShapeDtypeStruct((n, hm * LANES), F32), jax.ShapeDtypeStruct((bl, hm, t, 1), F32)],
        scratch_shapes=[pltpu.VMEM((t, LANES), MMD)] * 3,
        compiler_params=_params(("parallel", "arbitrary")), name=name)(q, q, kv, kv, pm, ct, st)


def attn_bwd(q, kv, pm, o, do, lse, ct, st, bl, t, hm, name):
    n = q.shape[0]
    tq = LANES
    scale = QK_DIM ** -0.5
    kpe_blk = pm.shape[1] // LANES - 1

    def body(qn_ref, qpe_ref, kn_ref, v_ref, kpe_ref, o_ref, do_ref, lse_ref, ct_ref, st_ref,
             dqn_ref, dqpe_ref, dkn_ref, dv_ref, dkpe_ref, kp_s, kn_s, v_s, dkn_s, dkp_s, dv_s):
        h = pl.program_id(1)
        lane = lax.broadcasted_iota(jnp.int32, (1, LANES), 1)
        first = (lane & (ROPE_DIM - 1)) < ROPE_DIM // 2
        mine = (lane // ROPE_DIM) == (h % 2)
        kp = _rope(kpe_ref[...], ct_ref[...], st_ref[...], first)
        kp_s[...] = jnp.where(h % 2 == 0, kp, pltpu.roll(kp, ROPE_DIM, 1)).astype(MMD)
        kn_s[...] = kn_ref[...].astype(MMD)
        v_s[...] = v_ref[...].astype(MMD)
        dkn_s[...] = jnp.zeros_like(dkn_s)
        dkp_s[...] = jnp.zeros_like(dkp_s)
        dv_s[...] = jnp.zeros_like(dv_s)
        kpos = lax.broadcasted_iota(jnp.int32, (1, t), 1)

        @pl.when(h % 2 == 0)
        def _():
            dqpe_ref[...] = jnp.zeros_like(dqpe_ref)

        @pl.when(h == 0)
        def _():
            dkpe_ref[...] = jnp.zeros_like(dkpe_ref)

        def qtile(i, carry):
            rows = pl.ds(pl.multiple_of(i * tq, tq), tq)
            c_i, s_i = ct_ref[rows, :], st_ref[rows, :]
            q1 = qn_ref[rows, :].astype(MMD)
            q2 = _rope(qpe_ref[rows, :], c_i, s_i, first).astype(MMD)
            s = (_mm(q1, kn_s[...], ((1,), (1,))) + _mm(q2, kp_s[...], ((1,), (1,)))) * scale
            qpos = i * tq + lax.broadcasted_iota(jnp.int32, (tq, 1), 0)
            p = jnp.where(kpos <= qpos, jnp.exp(s - lse_ref[0, 0, rows, :]), 0.0)
            do_i = do_ref[rows, :]
            delta = jnp.sum(do_i * o_ref[rows, :], axis=1, keepdims=True)
            dp = _mm(do_i, v_s[...], ((1,), (1,)))
            ds = (p * (dp - delta) * scale).astype(MMD)
            dqn_ref[rows, :] = _mm(ds, kn_s[...])
            dq2 = jnp.where(mine, _mm(ds, kp_s[...]), 0.0)
            dqpe_ref[rows, :] += _unrope(dq2, c_i, s_i, first)
            dkn_s[...] += _mm(ds, q1, ((0,), (0,)))
            dkp_s[...] += _mm(ds, q2, ((0,), (0,)))
            dv_s[...] += _mm(p, do_i, ((0,), (0,)))
            return carry

        lax.fori_loop(0, t // tq, qtile, 0)
        dkn_ref[...] = dkn_s[...]
        dv_ref[...] = dv_s[...]
        dkp = jnp.where(mine, dkp_s[...], 0.0)
        dkp = jnp.where(h % 2 == 0, dkp, pltpu.roll(dkp, ROPE_DIM, 1))
        dkpe_ref[...] += _unrope(dkp, ct_ref[...], st_ref[...], first)

    blk = lambda f: pl.BlockSpec((t, LANES), f)
    hd = lambda b, h: (b, h)
    shp = lambda wd: jax.ShapeDtypeStruct((n, wd), F32)
    return pl.pallas_call(
        body, grid=(bl, hm),
        in_specs=[blk(hd), blk(lambda b, h: (b, hm + h // 2)), blk(hd), blk(lambda b, h: (b, hm + h)),
                  blk(lambda b, h: (b, kpe_blk)), blk(hd), blk(hd), pl.BlockSpec((1, 1, t, 1), lambda b, h: (b, h, 0, 0)),
                  blk(lambda b, h: (0, 0)), blk(lambda b, h: (0, 0))],
        out_specs=[blk(hd), blk(lambda b, h: (b, h // 2)), blk(hd), blk(hd), blk(lambda b, h: (b, 0))],
        out_shape=[shp(hm * LANES), shp(hm * ROPE_DIM), shp(hm * LANES), shp(hm * LANES), shp(LANES)],
        scratch_shapes=[pltpu.VMEM((t, LANES), MMD)] * 3 + [pltpu.VMEM((t, LANES), F32)] * 3,
        compiler_params=_params(("parallel", "arbitrary")), name=name)(q, q, kv, kv, pm, o, do, lse, ct, st)


def _peer(k):
    mx, my, mc = lax.axis_index("x"), lax.axis_index("y"), lax.axis_index("c")
    px = 1 - mx if k & 4 else mx
    py = 1 - my if k & 2 else my
    pc = 1 - mc if k & 1 else mc
    return (px, py, pc), 4 * px + 2 * py + pc


def all_gather(x, name):
    def body(x_ref, o_ref, send_sems, recv_sems, local_sem):
        _, me = _peer(0)
        local = pltpu.make_async_copy(x_ref, o_ref.at[me], local_sem)
        local.start()
        copies = []
        for k in range(1, N_DEV):
            dev, _ = _peer(k)
            cp = pltpu.make_async_remote_copy(src_ref=x_ref, dst_ref=o_ref.at[me], send_sem=send_sems.at[k - 1],
                                              recv_sem=recv_sems.at[k - 1], device_id=dev, device_id_type=MESH)
            cp.start()
            copies.append(cp)
        for cp in copies:
            cp.wait()
        local.wait()

    return pl.pallas_call(
        body, in_specs=[pl.BlockSpec(memory_space=pl.ANY)], out_specs=pl.BlockSpec(memory_space=pl.ANY),
        out_shape=jax.ShapeDtypeStruct((N_DEV,) + x.shape, x.dtype),
        scratch_shapes=[pltpu.SemaphoreType.DMA((N_DEV - 1,)), pltpu.SemaphoreType.DMA((N_DEV - 1,)), pltpu.SemaphoreType.DMA],
        name=name)(x)


def all_to_all(x, name):
    def body(x_ref, o_ref, send_sems, recv_sems, local_sem):
        _, me = _peer(0)
        local = pltpu.make_async_copy(x_ref.at[me], o_ref.at[me], local_sem)
        local.start()
        copies = []
        for k in range(1, N_DEV):
            dev, idx = _peer(k)
            cp = pltpu.make_async_remote_copy(src_ref=x_ref.at[idx], dst_ref=o_ref.at[me], send_sem=send_sems.at[k - 1],
                                              recv_sem=recv_sems.at[k - 1], device_id=dev, device_id_type=MESH)
            cp.start()
            copies.append(cp)
        for cp in copies:
            cp.wait()
        local.wait()

    return pl.pallas_call(
        body, in_specs=[pl.BlockSpec(memory_space=pl.ANY)], out_specs=pl.BlockSpec(memory_space=pl.ANY),
        out_shape=jax.ShapeDtypeStruct(x.shape, x.dtype),
        scratch_shapes=[pltpu.SemaphoreType.DMA((N_DEV - 1,)), pltpu.SemaphoreType.DMA((N_DEV - 1,)), pltpu.SemaphoreType.DMA],
        name=name)(x)


def sum_blocks(x, name):
    _, r, c = x.shape
    tr = _tile(r, max(16, (4 << 20) // (N_DEV * c * x.dtype.itemsize)), 16)

    def body(x_ref, o_ref):
        acc = x_ref[0].astype(F32)
        for i in range(1, N_DEV):
            acc = acc + x_ref[i].astype(F32)
        o_ref[...] = acc

    return pl.pallas_call(
        body, grid=(r // tr,), in_specs=[pl.BlockSpec((N_DEV, tr, c), lambda i: (0, i, 0))],
        out_specs=pl.BlockSpec((tr, c), lambda i: (i, 0)), out_shape=jax.ShapeDtypeStruct((r, c), F32),
        compiler_params=_params(("parallel",)), name=name)(x)


def _adamw(w, g, m, v):
    m = ADAM_B1 * m + (1.0 - ADAM_B1) * g
    v = ADAM_B2 * v + (1.0 - ADAM_B2) * jnp.square(g)
    m_hat = m / (1.0 - ADAM_B1 ** ADAM_STEP)
    v_hat = v / (1.0 - ADAM_B2 ** ADAM_STEP)
    delta = -ADAM_LR * (m_hat / (jnp.sqrt(v_hat) + ADAM_EPS) + ADAM_WD * w)
    return delta, m, v


def adamw(w, g, m, v, name):
    r, c = w.shape
    tr = _tile(r, 256, 8)
    spec = pl.BlockSpec((tr, c), lambda i: (i, 0))

    def body(w_ref, g_ref, m_ref, v_ref, d_ref, nm_ref, nv_ref):
        d_ref[...], nm_ref[...], nv_ref[...] = _adamw(w_ref[...], g_ref[...], m_ref[...], v_ref[...])

    return pl.pallas_call(
        body, grid=(r // tr,), in_specs=[spec] * 4, out_specs=[spec] * 3,
        out_shape=[jax.ShapeDtypeStruct((r, c), F32)] * 3, compiler_params=_params(("parallel",)), name=name)(w, g, m, v)


def batch_sum_rows(dh, bl, t, rows, name):
    d = dh.shape[1]

    def body(x_ref, o_ref):
        @pl.when(pl.program_id(0) == 0)
        def _():
            o_ref[...] = jnp.zeros_like(o_ref)

        o_ref[...] += x_ref[...]

    return pl.pallas_call(
        body, grid=(bl,), in_specs=[pl.BlockSpec((rows, d), lambda b: (b * (t // rows), 0))],
        out_specs=pl.BlockSpec((rows, d), lambda b: (0, 0)), out_shape=jax.ShapeDtypeStruct((rows, d), F32),
        compiler_params=_params(("arbitrary",)), name=name)(dh)


class Dims:
    def __init__(self, x, meta_full_cols, w_up, g_up, q_norm, kv_norm, d_ff):
        self.bl, self.seq, self.d = x.shape
        self.n_meta = 16
        self.t_real = self.n_meta + self.seq
        self.t = -(-self.t_real // LANES) * LANES
        self.n = self.bl * self.t
        self.f = d_ff
        self.wl, self.gl = w_up.shape[-2], g_up.shape[-2]
        self.ql, self.kvl = q_norm.shape[-1], kv_norm.shape[-1]
        self.hm = self.d // V_DIM
        self.in_cols = 5 * self.d + 2 * self.wl + self.gl + self.ql + self.kvl + ROPE_DIM


def _pad_cols(a, width):
    return jnp.pad(a, ((0, 0), (0, width - a.shape[1])))


def _pad_rows(a, rows):
    return jnp.pad(a, ((0, rows - a.shape[0]), (0, 0)))


def split_in(a, dm, axis=1):
    d, wl, gl, ql, kvl = dm.d, dm.wl, dm.gl, dm.ql, dm.kvl
    size = a.shape[axis]
    cut = lambda lo, hi: lax.slice_in_dim(a, min(lo, size), min(hi, size), axis=axis)

    def pad(p, width):
        cfg = [(0, 0)] * a.ndim
        cfg[axis] = (0, width - p.shape[axis])
        return jnp.pad(p, cfg)

    o = 3 * d
    lora = jnp.concatenate([pad(cut(o, o + wl), LANES), pad(cut(o + wl, o + 2 * wl), LANES),
                            cut(o + 2 * wl, o + 2 * wl + gl)], axis=axis)
    o += 2 * wl + gl
    mla = pad(cut(o, o + ql + kvl + ROPE_DIM), ql + kvl + LANES)
    o += ql + kvl + ROPE_DIM
    return dict(r=cut(0, d), k=cut(d, 2 * d), v=cut(2 * d, 3 * d), l=lora, m=mla, ga=cut(o, o + d), gb=cut(o + d, o + 2 * d))


def merge_in(g, dm, axis=1):
    wl, gl, ql, kvl = dm.wl, dm.gl, dm.ql, dm.kvl
    cut = lambda p, lo, hi: lax.slice_in_dim(p, lo, hi, axis=axis)
    l, m = g["l"], g["m"]
    return jnp.concatenate([g["r"], g["k"], g["v"], cut(l, 0, wl), cut(l, LANES, LANES + wl), cut(l, 2 * LANES, 2 * LANES + gl),
                            cut(m, 0, ql + kvl + ROPE_DIM), g["ga"], g["gb"]], axis=axis)


def split_uq(w, dm):
    w3 = w.reshape(w.shape[0], dm.hm, QK_DIM)
    return jnp.concatenate([w3[:, :, :NOPE_DIM].reshape(w.shape[0], -1), w3[:, :, NOPE_DIM:].reshape(w.shape[0], -1)], axis=1)


def merge_uq(gn, gp, dm):
    r = gn.shape[0]
    return jnp.concatenate([gn.reshape(r, dm.hm, NOPE_DIM), gp.reshape(r, dm.hm, ROPE_DIM)], axis=2).reshape(r, -1)


def split_ukv(w, dm):
    w3 = w.reshape(w.shape[0], dm.hm, NOPE_DIM + V_DIM)
    return jnp.concatenate([w3[:, :, :NOPE_DIM].reshape(w.shape[0], -1), w3[:, :, NOPE_DIM:].reshape(w.shape[0], -1)], axis=1)


def merge_ukv(gk, gv, dm):
    r = gk.shape[0]
    return jnp.concatenate([gk.reshape(r, dm.hm, NOPE_DIM), gv.reshape(r, dm.hm, V_DIM)], axis=2).reshape(r, -1)


def head_matrices(d):
    heads = d // RWKV_HEAD
    e = (np.arange(d)[:, None] // RWKV_HEAD == np.arange(LANES)[None, :]) & (np.arange(LANES)[None, :] < heads)
    return jnp.asarray(e, BF16), jnp.asarray(e.T, BF16)


def rope_tables(t):
    pos = jnp.arange(t, dtype=F32)
    inv_freq = 1.0 / (ROPE_THETA ** (jnp.arange(0, ROPE_DIM, 2, dtype=F32) / ROPE_DIM))
    ang = pos[:, None] * inv_freq[None, :]
    cos, sin = jnp.cos(ang), jnp.sin(ang)
    return jnp.tile(jnp.concatenate([cos, cos], axis=1), (1, 2)), jnp.tile(jnp.concatenate([-sin, sin], axis=1), (1, 2))


def local_step(dm, x, loss_target, meta, wt, sp):
    bl, t, n, d, hm = dm.bl, dm.t, dm.n, dm.d, dm.hm
    e, et = head_matrices(d)
    ct, st = rope_tables(t)
    padz = jnp.zeros((bl, t - dm.t_real, d), F32)
    h0 = jnp.concatenate([jnp.broadcast_to(meta[None], (bl, dm.n_meta, d)), x, padz], axis=1).reshape(n, d)
    tgt = jnp.concatenate([jnp.zeros((bl, dm.n_meta, d), F32), loss_target, padz], axis=1).reshape(n, d)
    tpos = jnp.arange(t)
    mask = jnp.tile(((tpos >= dm.n_meta) & (tpos < dm.t_real)).astype(F32), bl).reshape(n, 1)

    win = split_in(wt["w_in"], dm, axis=0)
    mu = split_in(sp["tm_mu"], dm)
    wq, wkv = split_uq(wt["w_uq"], dm), split_ukv(wt["w_ukv"], dm)
    prm = dict(w0=sp["w0"], a0=sp["a0"], k_k=sp["k_k"], k_a=sp["k_a"], gn_w=sp["gn_w"], gn_b=sp["gn_b"], r_k=sp["r_k"],
               w_up=_pad_rows(wt["w_up"], LANES).astype(F32), a_up=_pad_rows(wt["a_up"], LANES).astype(F32),
               g_up=wt["g_up"].astype(F32))

    h1, ffn1 = ffn_forward(h0, sp["ffn1_norm"], wt["ffn1_w_gate"], wt["ffn1_w_up"], wt["ffn1_w_down"], "ffn1")
    u = rms_fwd(h1, sp["mix_norm"], "mix_rms")
    proj = {key: matmul([(u, win[key])], "nt", name=f"proj_{key}") for key in win}
    sh = {key: lerp_fwd(proj[key], mu[key], bl, t, f"shift_{key}") for key in ("r", "k", "v", "l")}
    decay, kmod, kneg, bvec, gate = rwkv_prep_fwd(sh["k"], sh["l"], prm, e, et, "rwkv_prep")
    pairs = min(SCAN_PAIRS, d // LANES)
    y_col, hist = scan_forward(sh["r"], decay, kmod, kneg, bvec, sh["v"], bl, t, d, "wkv_scan", pairs, SCAN_FWD_STEPS)
    y = to_row(y_col, bl, t, d)
    cqn, ckvn = mla_prep_fwd(proj["m"], sp["q_norm"], sp["kv_norm"], "mla_norms")
    q = matmul([(cqn, wq)], "nn", name="mla_q")
    kv = matmul([(ckvn, wkv)], "nn", name="mla_kv")
    o, lse = attn_fwd(q, kv, proj["m"], ct, st, bl, t, hm, "mla_attn")
    post_in = [y, sh["r"], kmod, sh["v"], gate, proj["ga"], proj["gb"], o]
    mix = rwkv_post_fwd(post_in, prm, e, et, "mix_gate")
    h2 = matmul([(mix, wt["w_out"])], "nn", res=h1, name="out_proj")
    h3, ffn2 = ffn_forward(h2, sp["ffn2_norm"], wt["ffn2_w_gate"], wt["ffn2_w_up"], wt["ffn2_w_down"], "ffn2")
    dh3, d_final, loss = loss_head(h3, tgt, mask, sp["final_norm"], "loss_head")

    gw, gs = {}, {"final_norm": d_final}
    dh2, gs["ffn2_norm"], gw["ffn2_w_gate"], gw["ffn2_w_up"], gw["ffn2_w_down"] = ffn_backward(
        dh3, h2, sp["ffn2_norm"], wt["ffn2_w_gate"], wt["ffn2_w_up"], wt["ffn2_w_down"], ffn2, "ffn2")
    dmix = matmul([(dh2, wt["w_out"])], "nt", name="out_proj_dx")
    gw["w_out"] = matmul([(mix, dh2)], "tn", name="out_proj_dw")
    (dy, dr_p, dkm_p, dv_p, dgate, dpga, dpgb, do, gs["gn_w"], gs["gn_b"], gs["r_k"]) = rwkv_post_bwd(
        post_in, prm, e, et, dmix, "mix_gate_bwd")
    dqn, dqpe, dkn, dv_att, dkpe = attn_bwd(q, kv, proj["m"], o, do, lse, ct, st, bl, t, hm, "mla_attn_bwd")
    nq = hm * NOPE_DIM
    dcqn = matmul([(dqn, wq[:, :nq])], "nt", name="mla_q_dx1")
    dcqn = matmul([(dqpe, wq[:, nq:])], "nt", res=dcqn, name="mla_q_dx2")
    gw["w_uq"] = merge_uq(matmul([(cqn, dqn)], "tn", name="mla_q_dw1"), matmul([(cqn, dqpe)], "tn", name="mla_q_dw2"), dm)
    dckvn = matmul([(dkn, wkv[:, :nq]), (dv_att, wkv[:, nq:])], "nt", name="mla_kv_dx")
    gw["w_ukv"] = merge_ukv(matmul([(ckvn, dkn)], "tn", name="mla_kv_dw1"), matmul([(ckvn, dv_att)], "tn", name="mla_kv_dw2"), dm)
    dproj = {"ga": dpga, "gb": dpgb}
    dproj["m"], gs["q_norm"], gs["kv_norm"] = mla_prep_bwd(proj["m"], sp["q_norm"], sp["kv_norm"], dcqn, dckvn, dkpe, "mla_norms_bwd")
    dr_s, ddecay, dk_s, dkneg, dbvec, dv_col = scan_backward(sh["r"], decay, kmod, kneg, bvec, sh["v"], dy, hist,
                                                             bl, t, d, "wkv_scan_bwd", pairs, SCAN_BWD_STEPS)
    dv_s = to_row(dv_col, bl, t, d)
    (dsh_k, dsh_l, gs["w0"], gs["a0"], gs["k_k"], gs["k_a"], g_wup, g_aup, gw["g_up"]) = rwkv_prep_bwd(
        sh["k"], sh["l"], prm, e, et, [ddecay, dk_s, dkm_p, dkneg, dbvec, dgate], "rwkv_prep_bwd")
    gw["w_up"], gw["a_up"] = g_wup[:dm.wl], g_aup[:dm.wl]
    dmu = {}
    for key, cts in (("r", [dr_s, dr_p]), ("k", [dsh_k]), ("v", [dv_s, dv_p]), ("l", [dsh_l])):
        dproj[key], dmu[key] = lerp_bwd(proj[key], mu[key], cts, bl, t, f"shift_{key}_bwd")
    zero_m = jnp.zeros((1, proj["m"].shape[1]), F32)
    gs["tm_mu"] = merge_in(dict(dmu, m=zero_m, ga=zero_m[:, :0], gb=zero_m[:, :0]), dm)[:, :3 * d + 2 * dm.wl + dm.gl]
    wide = ("r", "k", "v", "ga", "gb")
    du = matmul([(dproj[key], win[key]) for key in wide], "nn", name="proj_dx", tk=512)
    du = matmul([(dproj["l"], win["l"])], "nn", res=du, name="proj_dx_l")
    du = matmul([(dproj["m"], win["m"])], "nn", res=du, name="proj_dx_m")
    gw["w_in"] = merge_in({key: matmul([(dproj[key], u)], "tn", name=f"proj_dw_{key}") for key in win}, dm, axis=0)
    dh1, gs["mix_norm"] = rms_bwd(h1, sp["mix_norm"], du, dh2, "mix_rms_bwd")
    dh0, gs["ffn1_norm"], gw["ffn1_w_gate"], gw["ffn1_w_up"], gw["ffn1_w_down"] = ffn_backward(
        dh1, h0, sp["ffn1_norm"], wt["ffn1_w_gate"], wt["ffn1_w_up"], wt["ffn1_w_down"], ffn1, "ffn1")
    grad_x = dh0.reshape(bl, t, d)[:, dm.n_meta:dm.t_real]
    dmeta = batch_sum_rows(dh0, bl, t, dm.n_meta, "meta_grad")
    return loss, grad_x, dmeta, gw, gs


COL_SHARDED = ("ffn1_w_gate", "ffn1_w_up", "w_in", "w_up", "a_up", "g_up", "w_uq", "w_ukv", "ffn2_w_gate", "ffn2_w_up")
ROW_SHARDED = ("ffn1_w_down", "w_out", "ffn2_w_down")
TRANSPOSED = ("ffn1_w_gate", "ffn1_w_up", "w_in", "ffn2_w_gate", "ffn2_w_up")
WIDE = ("ffn1_w_gate", "ffn1_w_up", "ffn1_w_down", "w_in", "w_out", "ffn2_w_gate", "ffn2_w_up", "ffn2_w_down")
NARROW = ("w_up", "a_up", "g_up", "w_uq", "w_ukv")
MATRICES = ("ffn1_w_gate", "ffn1_w_up", "ffn1_w_down", "w_in", "w_up", "a_up", "g_up", "w_uq", "w_ukv", "w_out",
            "ffn2_w_gate", "ffn2_w_up", "ffn2_w_down")
SMALL = ("ffn1_norm", "mix_norm", "tm_mu", "w0", "a0", "k_k", "k_a", "r_k", "gn_w", "gn_b", "q_norm", "kv_norm",
         "ffn2_norm", "final_norm")
WEIGHTS = ("meta_tokens", "ffn1_norm", "ffn1_w_gate", "ffn1_w_up", "ffn1_w_down", "mix_norm", "w_in", "tm_mu", "w0", "w_up",
           "a0", "a_up", "g_up", "k_k", "k_a", "r_k", "gn_w", "gn_b", "q_norm", "w_uq", "kv_norm", "w_ukv", "w_out",
           "ffn2_norm", "ffn2_w_gate", "ffn2_w_up", "ffn2_w_down", "final_norm")
PACK_COLS = 1024
PACK_ALIGN = 16 * PACK_COLS


def _pack(parts):
    offs, o = [], 0
    for p in parts:
        offs.append(o)
        o += p.shape[1]
    total = -(-o // PACK_ALIGN) * PACK_ALIGN
    flat = jnp.concatenate(list(parts) + [jnp.zeros((parts[0].shape[0], total - o), parts[0].dtype)], axis=1)
    return flat.reshape(parts[0].shape[0], total // PACK_COLS, PACK_COLS), offs


def kernel(x, meta_tokens, ffn1_norm, ffn1_w_gate, ffn1_w_up, ffn1_w_down, mix_norm, w_in, tm_mu, w0, w_up, a0, a_up, g_up, k_k, k_a, r_k, gn_w, gn_b, q_norm, w_uq, kv_norm, w_ukv, w_out, ffn2_norm, ffn2_w_gate, ffn2_w_up, ffn2_w_down, final_norm, loss_target, m_meta_tokens, m_ffn1_norm, m_ffn1_w_gate, m_ffn1_w_up, m_ffn1_w_down, m_mix_norm, m_w_in, m_tm_mu, m_w0, m_w_up, m_a0, m_a_up, m_g_up, m_k_k, m_k_a, m_r_k, m_gn_w, m_gn_b, m_q_norm, m_w_uq, m_kv_norm, m_w_ukv, m_w_out, m_ffn2_norm, m_ffn2_w_gate, m_ffn2_w_up, m_ffn2_w_down, m_final_norm, v_meta_tokens, v_ffn1_norm, v_ffn1_w_gate, v_ffn1_w_up, v_ffn1_w_down, v_mix_norm, v_w_in, v_tm_mu, v_w0, v_w_up, v_a0, v_a_up, v_g_up, v_k_k, v_k_a, v_r_k, v_gn_w, v_gn_b, v_q_norm, v_w_uq, v_kv_norm, v_w_ukv, v_w_out, v_ffn2_norm, v_ffn2_w_gate, v_ffn2_w_up, v_ffn2_w_down, v_final_norm):
    args = dict(locals())
    wts = {k: args[k] for k in WEIGHTS}
    ms = {k: args["m_" + k] for k in WEIGHTS}
    vs = {k: args["v_" + k] for k in WEIGHTS}
    dm = Dims(x, None, w_up, g_up, q_norm, kv_norm, ffn1_w_down.shape[1] * N_DEV)

    shard2d = {k: wts[k].reshape(wts[k].shape[-2], wts[k].shape[-1]) for k in MATRICES}
    sent = {k: shard2d[k].T if k in TRANSPOSED else shard2d[k] for k in MATRICES}
    wide_rows = np.cumsum([0] + [sent[k].shape[0] for k in WIDE])
    got_wide = all_gather(jnp.concatenate([sent[k].astype(MMD) for k in WIDE], axis=0), "gather_wide")
    full = {k: got_wide[:, lo:hi].reshape(-1, dm.d) for k, lo, hi in zip(WIDE, wide_rows[:-1], wide_rows[1:])}
    send, offs = _pack([sent[k].astype(MMD).reshape(1, -1) for k in NARROW])
    got = all_gather(send[0], "gather_narrow").reshape(N_DEV, -1)
    for k, o in zip(NARROW, offs):
        r, c = sent[k].shape
        full[k] = got[:, o:o + r * c].reshape(N_DEV, r, c).transpose(1, 0, 2).reshape(r, N_DEV * c)
    mr, mc = meta_tokens.shape
    meta = all_gather(meta_tokens, "gather_meta").transpose(1, 0, 2).reshape(mr, N_DEV * mc)
    small = {k: wts[k].reshape(1, -1) for k in SMALL}

    loss, grad_x, dmeta, gw, gs = local_step(dm, x, loss_target, meta, full, small)

    gwide = jnp.concatenate([gw[k].reshape(N_DEV, sent[k].shape[0], dm.d) for k in WIDE], axis=1).astype(MMD)
    gsum_wide = sum_blocks(all_to_all(gwide, "scatter_wide"), "sum_wide")
    grads = {}
    for k, lo, hi in zip(WIDE, wide_rows[:-1], wide_rows[1:]):
        grads[k] = gsum_wide[lo:hi].T if k in TRANSPOSED else gsum_wide[lo:hi]

    def blocks(k, g):
        r, c = sent[k].shape
        return g.reshape(r, N_DEV, c).transpose(1, 0, 2).reshape(N_DEV, r * c)

    gsend, goffs = _pack([blocks(k, gw[k]).astype(MMD) for k in NARROW]
                         + [dmeta.reshape(mr, N_DEV, mc).transpose(1, 0, 2).reshape(N_DEV, mr * mc).astype(MMD)])
    gsum = sum_blocks(all_to_all(gsend, "scatter_narrow"), "sum_narrow").reshape(-1)
    for k, o in zip(NARROW, goffs):
        r, c = sent[k].shape
        grads[k] = gsum[o:o + r * c].reshape(r, c)
    grads["meta_tokens"] = gsum[goffs[-1]:goffs[-1] + mr * mc].reshape(mr, mc)

    ssend, soffs = _pack([gs[k].reshape(1, -1) for k in SMALL] + [loss])
    ssum = sum_blocks(all_gather(ssend[0], "gather_small"), "sum_small").reshape(-1)
    for k, o in zip(SMALL, soffs):
        grads[k] = ssum[o:o + small[k].shape[1]]
    loss_total = ssum[soffs[-1]]

    delta, new_m, new_v = {}, {}, {}
    for k in MATRICES + ("meta_tokens",):
        shp = wts[k].shape
        to2d = lambda a: a.reshape(shp[-2], shp[-1])
        dlt, nm, nv = adamw(to2d(wts[k]), grads[k], to2d(ms[k]), to2d(vs[k]), f"adamw_{k}")
        delta[k], new_m[k], new_v[k] = dlt.reshape(shp), nm.reshape(shp), nv.reshape(shp)
        grads[k] = grads[k].reshape(shp)
    pw, _ = _pack([wts[k].reshape(1, -1) for k in SMALL])
    pm_, _ = _pack([ms[k].reshape(1, -1) for k in SMALL])
    pv, _ = _pack([vs[k].reshape(1, -1) for k in SMALL])
    pg, poffs = _pack([grads[k].reshape(1, -1) for k in SMALL])
    dlt, nm, nv = adamw(pw[0], pg[0], pm_[0], pv[0], "adamw_small")
    for k, o in zip(SMALL, poffs):
        shp, sz = wts[k].shape, small[k].shape[1]
        cut = lambda a: a.reshape(-1)[o:o + sz].reshape(shp)
        delta[k], new_m[k], new_v[k] = cut(dlt), cut(nm), cut(nv)
        grads[k] = grads[k].reshape(shp)

    return (loss_total, grad_x, *[grads[k] for k in WEIGHTS], *[delta[k] for k in WEIGHTS],
            *[new_m[k] for k in WEIGHTS], *[new_v[k] for k in WEIGHTS])
```

```python
import functools

import numpy as np
import jax
import jax.numpy as jnp
from jax import lax
from jax.experimental import pallas as pl
from jax.experimental.pallas import tpu as pltpu

F32 = jnp.float32
BF16 = jnp.bfloat16
MMD = BF16

NORM_EPS = 1e-6
RWKV_HEAD = 64
GN_EPS = RWKV_HEAD * 1e-5
NOPE_DIM = 128
ROPE_DIM = 64
V_DIM = 128
QK_DIM = NOPE_DIM + ROPE_DIM
ROPE_THETA = 10000.0
ADAM_LR = 0.001
ADAM_B1 = 0.9
ADAM_B2 = 0.999
ADAM_EPS = 1e-08
ADAM_WD = 0.01
ADAM_STEP = 10

LANES = 128
TCH = 64
SCAN_PAIRS = 8
SCAN_FWD_STEPS = 32
SCAN_BWD_STEPS = 16
N_DEV = 8
VMEM_LIMIT = 56 * 1024 * 1024
MESH = pl.DeviceIdType.MESH


def _tile(n, target, align):
    best = None
    for d in range(align, min(n, target) + 1, align):
        if n % d == 0:
            best = d
    return best if best is not None else n


def _params(sem=None):
    return pltpu.CompilerParams(dimension_semantics=sem, vmem_limit_bytes=VMEM_LIMIT)


def _mm(a, b, dims=((1,), (0,))):
    return lax.dot_general(a.astype(MMD), b.astype(MMD), (dims, ((), ())), preferred_element_type=F32)


@jax.custom_vjp
def mmdot(a, b):
    return _mm(a, b)


def _mmdot_fwd(a, b):
    return _mm(a, b), (a, b)


def _mmdot_bwd(res, g):
    a, b = res
    return _mm(g, b, ((1,), (1,))).astype(a.dtype), _mm(a, g, ((0,), (0,))).astype(b.dtype)


mmdot.defvjp(_mmdot_fwd, _mmdot_bwd)


def _dot2(x, m):
    hi = x.astype(BF16)
    lo = (x - hi.astype(F32)).astype(BF16)
    return (lax.dot_general(hi, m, (((1,), (0,)), ((), ())), preferred_element_type=F32)
            + lax.dot_general(lo, m, (((1,), (0,)), ((), ())), preferred_element_type=F32))


@jax.custom_vjp
def segsum(x, e, et):
    return _dot2(_dot2(x, e), et)


def _segsum_fwd(x, e, et):
    return segsum(x, e, et), (e, et)


def _segsum_bwd(res, g):
    e, et = res
    return segsum(g, e, et), jnp.zeros_like(e), jnp.zeros_like(et)


segsum.defvjp(_segsum_fwd, _segsum_bwd)


def _sigmoid(x):
    return 1.0 / (1.0 + jnp.exp(-x))


def _softplus(x):
    return jnp.maximum(x, 0.0) + jnp.log(1.0 + jnp.exp(-jnp.abs(x)))


def _rms(x, g):
    return x * lax.rsqrt(jnp.mean(x * x, axis=-1, keepdims=True) + NORM_EPS) * g


_DIMS = {"nn": ((1,), (0,)), "nt": ((1,), (1,)), "tn": ((0,), (0,))}


def matmul(pairs, mode, *, name, out_dtype=F32, res=None, alpha=1.0, tm=1088, tn=512, tk=1024):
    a0, b0 = pairs[0]
    if mode == "nn":
        (m, k), n = a0.shape, b0.shape[1]
    elif mode == "nt":
        (m, k), n = a0.shape, b0.shape[0]
    else:
        (k, m), n = a0.shape, b0.shape[1]
    tm = _tile(m, tm, 128 if mode == "tn" else 16)
    tn = _tile(n, 2048 if mode == "tn" else tn, 128)
    tk = _tile(k, tk, 16 if mode == "tn" else 128)
    nk = k // tk
    npair = len(pairs)
    if mode == "tn":
        a_spec = pl.BlockSpec((tk, tm), lambda i, j, kk: (kk, i))
    else:
        a_spec = pl.BlockSpec((tm, tk), lambda i, j, kk: (i, kk))
    if mode == "nt":
        b_spec = pl.BlockSpec((tn, tk), lambda i, j, kk: (j, kk))
    else:
        b_spec = pl.BlockSpec((tk, tn), lambda i, j, kk: (kk, j))
    o_spec = pl.BlockSpec((tm, tn), lambda i, j, kk: (i, j))
    dims = _DIMS[mode]

    def body(*refs):
        ab = refs[:2 * npair]
        res_ref = refs[2 * npair] if res is not None else None
        o_ref, acc_ref = refs[-2], refs[-1]
        kk = pl.program_id(2)

        @pl.when(kk == 0)
        def _():
            acc_ref[...] = jnp.zeros_like(acc_ref)

        part = _mm(ab[0][...], ab[1][...], dims)
        for p in range(1, npair):
            part = part + _mm(ab[2 * p][...], ab[2 * p + 1][...], dims)
        acc_ref[...] += part

        @pl.when(kk == nk - 1)
        def _():
            out = acc_ref[...] * alpha if alpha != 1.0 else acc_ref[...]
            if res_ref is not None:
                out = res_ref[...].astype(F32) + out
            o_ref[...] = out.astype(o_ref.dtype)

    args, specs = [], []
    for a, b in pairs:
        args += [a, b]
        specs += [a_spec, b_spec]
    if res is not None:
        args.append(res)
        specs.append(o_spec)
    return pl.pallas_call(
        body, grid=(m // tm, n // tn, nk), in_specs=specs, out_specs=o_spec,
        out_shape=jax.ShapeDtypeStruct((m, n), out_dtype), scratch_shapes=[pltpu.VMEM((tm, tn), F32)],
        compiler_params=_params(("parallel", "parallel", "arbitrary")), name=name)(*args)


def tilek(fn, ins, outs, *, n_rows, tr, name):
    tr = _tile(n_rows, tr, 16)
    n_in = len(ins)
    in_specs = []
    for arr, kind in ins:
        if kind == "r":
            in_specs.append(pl.BlockSpec((tr, arr.shape[1]), lambda i: (i, 0)))
        else:
            in_specs.append(pl.BlockSpec(arr.shape, lambda i, nd=arr.ndim: (0,) * nd))
    out_specs, out_shape = [], []
    has_acc = False
    for o in outs:
        if o[0] == "r":
            out_specs.append(pl.BlockSpec((tr, o[1]), lambda i: (i, 0)))
            out_shape.append(jax.ShapeDtypeStruct((n_rows, o[1]), o[2]))
        else:
            has_acc = True
            out_specs.append(pl.BlockSpec(o[1], lambda i, nd=len(o[1]): (0,) * nd))
            out_shape.append(jax.ShapeDtypeStruct(o[1], F32))

    def body(*refs):
        i = pl.program_id(0)
        vals = fn(*[r[...] for r in refs[:n_in]])
        for o, r, v in zip(outs, refs[n_in:], vals):
            if o[0] == "r":
                r[...] = v.astype(r.dtype)
            else:
                @pl.when(i == 0)
                def _(r=r):
                    r[...] = jnp.zeros_like(r)

                r[...] += v

    return pl.pallas_call(
        body, grid=(n_rows // tr,), in_specs=in_specs, out_specs=out_specs, out_shape=out_shape,
        compiler_params=_params(("arbitrary",) if has_acc else ("parallel",)), name=name)(*[a for a, _ in ins])


def rms_fwd(x, g, name):
    n, d = x.shape
    return tilek(lambda xv, gv: (_rms(xv, gv),), [(x, "r"), (g, "f")], [("r", d, MMD)], n_rows=n, tr=256, name=name)[0]


def rms_bwd(x, g, dy, dres, name):
    n, d = x.shape

    def fn(xv, gv, dyv, drv):
        _, vjp = jax.vjp(_rms, xv, gv)
        dx, dg = vjp(dyv.astype(F32))
        return drv + dx, dg

    return tilek(fn, [(x, "r"), (g, "f"), (dy, "r"), (dres, "r")], [("r", d, F32), ("acc", (1, d))],
                 n_rows=n, tr=128, name=name)


def loss_head(h, tgt, mask, g, name):
    n, d = h.shape

    def fn(hv, tv, mv, gv):
        def lossf(hh, gg):
            e = (_rms(hh, gg) - tv) * mv
            s = jnp.sum(jnp.sum(e * e, axis=1, keepdims=True), axis=0, keepdims=True)
            return s * (0.5 / d)

        l, vjp = jax.vjp(lossf, hv, gv)
        dh, dg = vjp(jnp.ones((1, 1), F32))
        return dh, dg, jnp.broadcast_to(l, (1, LANES))

    return tilek(fn, [(h, "r"), (tgt, "r"), (mask, "r"), (g, "f")],
                 [("r", d, F32), ("acc", (1, d)), ("acc", (1, LANES))], n_rows=n, tr=128, name=name)


def ffn_up(hn, wg, wu, name):
    n, d = hn.shape
    f = wg.shape[0]
    tm, tn = _tile(n, 544, 16), _tile(f, 512, 128)

    def body(a_ref, g_ref, u_ref, og_ref, ou_ref, oa_ref):
        a = a_ref[...]
        g = _mm(a, g_ref[...], ((1,), (1,)))
        u = _mm(a, u_ref[...], ((1,), (1,)))
        og_ref[...] = g
        ou_ref[...] = u
        oa_ref[...] = (g * _sigmoid(g) * u).astype(oa_ref.dtype)

    o_spec = pl.BlockSpec((tm, tn), lambda i, j: (i, j))
    w_spec = pl.BlockSpec((tn, d), lambda i, j: (j, 0))
    return pl.pallas_call(
        body, grid=(n // tm, f // tn), in_specs=[pl.BlockSpec((tm, d), lambda i, j: (i, 0)), w_spec, w_spec],
        out_specs=[o_spec, o_spec, o_spec],
        out_shape=[jax.ShapeDtypeStruct((n, f), F32), jax.ShapeDtypeStruct((n, f), F32), jax.ShapeDtypeStruct((n, f), MMD)],
        compiler_params=_params(("parallel", "parallel")), name=name)(hn, wg, wu)


def ffn_down_bwd(dh, wd, gate, up, name):
    n, d = dh.shape
    f = wd.shape[0]
    tm, tn = _tile(n, 544, 16), _tile(f, 512, 128)

    def body(dh_ref, w_ref, g_ref, u_ref, dg_ref, du_ref):
        da = 0.5 * _mm(dh_ref[...], w_ref[...], ((1,), (1,)))
        g, u = g_ref[...], u_ref[...]
        s = _sigmoid(g)
        dg_ref[...] = (da * u * (s * (1.0 + g * (1.0 - s)))).astype(dg_ref.dtype)
        du_ref[...] = (da * (g * s)).astype(du_ref.dtype)

    o_spec = pl.BlockSpec((tm, tn), lambda i, j: (i, j))
    return pl.pallas_call(
        body, grid=(n // tm, f // tn),
        in_specs=[pl.BlockSpec((tm, d), lambda i, j: (i, 0)), pl.BlockSpec((tn, d), lambda i, j: (j, 0)), o_spec, o_spec],
        out_specs=[o_spec, o_spec],
        out_shape=[jax.ShapeDtypeStruct((n, f), MMD), jax.ShapeDtypeStruct((n, f), MMD)],
        compiler_params=_params(("parallel", "parallel")), name=name)(dh, wd, gate, up)


def ffn_forward(h, g, wg, wu, wd, tag):
    hn = rms_fwd(h, g, f"{tag}_rms")
    gate, up, act = ffn_up(hn, wg, wu, f"{tag}_up")
    out = matmul([(act, wd)], "nn", res=h, alpha=0.5, name=f"{tag}_down")
    return out, (hn, gate, up, act)


def ffn_backward(dout, h, g, wg, wu, wd, saved, tag):
    hn, gate, up, act = saved
    dgate, dup = ffn_down_bwd(dout, wd, gate, up, f"{tag}_dact")
    dwd = matmul([(act, dout)], "tn", alpha=0.5, name=f"{tag}_dwd")
    dwg = matmul([(dgate, hn)], "tn", name=f"{tag}_dwg")
    dwu = matmul([(dup, hn)], "tn", name=f"{tag}_dwu")
    dhn = matmul([(dgate, wg), (dup, wu)], "nn", name=f"{tag}_dhn")
    dh, dg = rms_bwd(h, g, dhn, dout, f"{tag}_drms")
    return dh, dg, dwg, dwu, dwd


def lerp_fwd(p, mu, bl, t, name):
    n, w = p.shape
    cb = _tile(w, 256, 128)

    def body(p_ref, mu_ref, o_ref):
        x = p_ref[...]
        row = lax.broadcasted_iota(jnp.int32, x.shape, 0)
        prev = jnp.where(row == 0, 0.0, pltpu.roll(x, 1, 0))
        o_ref[...] = x + mu_ref[...] * (prev - x)

    spec = pl.BlockSpec((t, cb), lambda b, j: (b, j))
    return pl.pallas_call(
        body, grid=(bl, w // cb), in_specs=[spec, pl.BlockSpec((1, cb), lambda b, j: (0, j))], out_specs=spec,
        out_shape=jax.ShapeDtypeStruct((n, w), F32), compiler_params=_params(("parallel", "parallel")), name=name)(p, mu)


def lerp_bwd(p, mu, douts, bl, t, name):
    n, w = p.shape
    cb = _tile(w, 256, 128)
    nd = len(douts)

    def body(*refs):
        p_ref, mu_ref = refs[0], refs[1]
        dp_ref, dmu_ref = refs[2 + nd], refs[3 + nd]
        b = pl.program_id(1)
        x, m = p_ref[...], mu_ref[...]
        d = refs[2][...]
        for r in refs[3:2 + nd]:
            d = d + r[...]
        row = lax.broadcasted_iota(jnp.int32, x.shape, 0)
        prev = jnp.where(row == 0, 0.0, pltpu.roll(x, 1, 0))
        z = d * m
        nxt = jnp.where(row == t - 1, 0.0, pltpu.roll(z, t - 1, 0))
        dp_ref[...] = d - z + nxt

        @pl.when(b == 0)
        def _():
            dmu_ref[...] = jnp.zeros_like(dmu_ref)

        dmu_ref[...] += jnp.sum(d * (prev - x), axis=0, keepdims=True)

    spec = pl.BlockSpec((t, cb), lambda j, b: (b, j))
    cspec = pl.BlockSpec((1, cb), lambda j, b: (0, j))
    return pl.pallas_call(
        body, grid=(w // cb, bl), in_specs=[spec, cspec] + [spec] * nd, out_specs=[spec, cspec],
        out_shape=[jax.ShapeDtypeStruct((n, w), F32), jax.ShapeDtypeStruct((1, w), F32)],
        compiler_params=_params(("parallel", "arbitrary")), name=name)(p, mu, *douts)


def _prep(k, xw, xa, xg, w0, a0, k_k, k_a, w_up, a_up, g_up, e, et):
    w_pre = -_softplus(-(w0 + mmdot(jnp.tanh(xw), w_up))) - 0.5
    decay = jnp.exp(-jnp.exp(w_pre))
    a = _sigmoid(a0 + mmdot(xa, a_up))
    g = mmdot(_sigmoid(xg), g_up)
    kk = k * k_k
    kk = kk * lax.rsqrt(jnp.maximum(segsum(kk * kk, e, et), 1e-24))
    kmod = k * (1.0 + (a - 1.0) * k_a)
    return decay, kmod, -kk, kk * a, g


def _lora_parts(xl):
    return xl[:, :LANES], xl[:, LANES:2 * LANES], xl[:, 2 * LANES:]


def rwkv_prep_fwd(pk, pl_, prm, e, et, name):
    n, d = pk.shape
    small = [prm[k] for k in ("w0", "a0", "k_k", "k_a", "w_up", "a_up", "g_up")]
    ins = [(pk, "r"), (pl_, "r")] + [(s, "f") for s in small] + [(e, "f"), (et, "f")]
    return tilek(lambda k, xl, *rest: _prep(k, *_lora_parts(xl), *rest), ins, [("r", d, F32)] * 5, n_rows=n, tr=128, name=name)


def rwkv_prep_bwd(pk, pl_, prm, e, et, cts, name):
    n, d = pk.shape
    small = [prm[k] for k in ("w0", "a0", "k_k", "k_a", "w_up", "a_up", "g_up")]

    def fn(k, xl, w0, a0, k_k, k_a, w_up, a_up, g_up, ev, etv, dw, dkm1, dkm2, dkn, db, dg):
        _, vjp = jax.vjp(lambda *a: _prep(*a, ev, etv), k, *_lora_parts(xl), w0, a0, k_k, k_a, w_up, a_up, g_up)
        dk, dxw, dxa, dxg, *dsmall = vjp((dw, dkm1 + dkm2, dkn, db, dg))
        return (dk, jnp.concatenate([dxw, dxa, dxg], axis=1), *dsmall)

    ins = [(pk, "r"), (pl_, "r")] + [(s, "f") for s in small] + [(e, "f"), (et, "f")] + [(c, "r") for c in cts]
    outs = [("r", d, F32), ("r", pl_.shape[1], F32)] + [("acc", s.shape) for s in small]
    return tilek(fn, ins, outs, n_rows=n, tr=64, name=name)


def _post(y, r, km, v, g, pga, pgb, yb, gn_w, gn_b, r_k, e, et):
    inv = 1.0 / RWKV_HEAD
    yc = y - segsum(y, e, et) * inv
    var = segsum(yc * yc, e, et) * inv
    yn = yc * lax.rsqrt(var + GN_EPS) * gn_w + gn_b
    bonus = segsum(r * km * r_k, e, et) * v
    ya = (yn + bonus) * g
    return _sigmoid(pga) * ya + _sigmoid(pgb) * yb


def rwkv_post_fwd(acts, prm, e, et, name):
    n, d = acts[0].shape
    small = [prm[k] for k in ("gn_w", "gn_b", "r_k")]
    ins = [(a, "r") for a in acts] + [(s, "f") for s in small] + [(e, "f"), (et, "f")]
    return tilek(lambda *a: (_post(*a),), ins, [("r", d, MMD)], n_rows=n, tr=128, name=name)[0]


def rwkv_post_bwd(acts, prm, e, et, dm, name):
    n, d = acts[0].shape
    small = [prm[k] for k in ("gn_w", "gn_b", "r_k")]
    na = len(acts)

    def fn(*a):
        prim, ev, etv, dmv = a[:na + 3], a[na + 3], a[na + 4], a[na + 5]
        _, vjp = jax.vjp(lambda *z: _post(*z, ev, etv), *prim)
        return vjp(dmv.astype(F32))

    ins = [(x, "r") for x in acts] + [(s, "f") for s in small] + [(e, "f"), (et, "f"), (dm, "r")]
    outs = [("r", d, F32)] * na + [("acc", s.shape) for s in small]
    return tilek(fn, ins, outs, n_rows=n, tr=64, name=name)


def to_row(a, bl, t, d):
    p, c = d // LANES, t // TCH
    a = a.reshape(bl, p, c, RWKV_HEAD, 2, TCH).transpose(0, 2, 5, 1, 4, 3)
    return a.reshape(bl * t, d)


def _head_sums(x, first_head):
    a = jnp.sum(jnp.where(first_head, x, 0.0), axis=1, keepdims=True)
    b = jnp.sum(jnp.where(first_head, 0.0, x), axis=1, keepdims=True)
    return jnp.where(first_head, a, b)


def _split2(x):
    hi = x.astype(BF16)
    return hi, (x - hi.astype(F32)).astype(BF16)


def _spread(row, eye2):
    hi, lo = _split2(row)
    return eye2 * hi, eye2 * lo


def _ones_dot(tiles, ones_blk):
    hi = jnp.concatenate([t[0] for t in tiles], axis=0)
    lo = jnp.concatenate([t[1] for t in tiles], axis=0)
    dims = (((1,), (0,)), ((), ()))
    res = (lax.dot_general(hi, ones_blk, dims, preferred_element_type=F32)
           + lax.dot_general(lo, ones_blk, dims, preferred_element_type=F32))
    return [res[i * RWKV_HEAD:(i + 1) * RWKV_HEAD] for i in range(len(tiles))]


def _scan_consts():
    lane = lax.broadcasted_iota(jnp.int32, (1, LANES), 1)
    rows = lax.broadcasted_iota(jnp.int32, (RWKV_HEAD, LANES), 0)
    cols = lax.broadcasted_iota(jnp.int32, (RWKV_HEAD, LANES), 1)
    eye2 = ((cols & (RWKV_HEAD - 1)) == rows).astype(BF16)
    r2 = lax.broadcasted_iota(jnp.int32, (LANES, LANES), 0)
    c2 = lax.broadcasted_iota(jnp.int32, (LANES, LANES), 1)
    ones_blk = ((r2 // RWKV_HEAD) == (c2 // RWKV_HEAD)).astype(BF16)
    return lane, lane < RWKV_HEAD, eye2, ones_blk


def scan_forward(r, w, k, kn, b, v, bl, t, d, name, pg, hch):
    npair, nst, nsub = d // LANES, t // hch, TCH // hch

    def body(r_ref, w_ref, k_ref, kn_ref, b_ref, v_ref, y_ref, hist_ref, s_ref, vb_ref):
        c = pl.program_id(2)
        off = (c % nsub) * hch
        lane, first_head, eye2, ones_blk = _scan_consts()
        step_lane = lane & (TCH - 1)
        eye2f = eye2.astype(F32)

        @pl.when(c == 0)
        def _():
            s_ref[...] = jnp.zeros_like(s_ref)

        @pl.when(c % nsub == 0)
        def _():
            y_ref[...] = jnp.zeros_like(y_ref)

        for p in range(pg):
            cols = slice(p * LANES, (p + 1) * LANES)
            vb_ref[p] = _head_sums(eye2f * v_ref[0, :, cols], first_head)

        def step(ts, carry):
            prev, nxt = jnp.maximum(ts - 1, 0), jnp.minimum(ts + 1, hch - 1)
            sel_prev = jnp.where(ts > 0, (step_lane == off + ts - 1).astype(F32), 0.0)
            states, tiles = [], []
            for p in range(pg):
                cols = slice(p * LANES, (p + 1) * LANES)
                s = s_ref[p]
                hist_ref[0, p, pl.ds(ts, 1)] = s[None]
                states.append(s)
                tiles.append(_split2(s * r_ref[prev, :, cols]))
                tiles.append(_spread(v_ref[nxt, :, cols], eye2))
            res = _ones_dot(tiles, ones_blk)
            for p in range(pg):
                cols = slice(p * LANES, (p + 1) * LANES)
                s = states[p]
                sa = _head_sums(s * kn_ref[ts, :, cols], first_head)
                s_ref[p] = s * w_ref[ts, :, cols] + sa * b_ref[ts, :, cols] + vb_ref[p] * k_ref[ts, :, cols]
            for p in range(pg):
                y_ref[0, p, 0] += res[2 * p] * sel_prev
                vb_ref[p] = res[2 * p + 1]
            return carry

        lax.fori_loop(0, hch, step, 0)
        last = (step_lane == off + hch - 1).astype(F32)
        for p in range(pg):
            cols = slice(p * LANES, (p + 1) * LANES)
            y_ref[0, p, 0] += _head_sums(s_ref[p] * r_ref[hch - 1, :, cols], first_head) * last

    row_spec = pl.BlockSpec((hch, 1, pg * LANES), lambda bb, g, c: (bb * nst + c, 0, g))
    col_spec = pl.BlockSpec((1, pg, 1, RWKV_HEAD, LANES), lambda bb, g, c: (bb, g, c // nsub, 0, 0))
    hist_spec = pl.BlockSpec((1, pg, hch, RWKV_HEAD, LANES), lambda bb, g, c: (bb, g, c, 0, 0))
    rows3 = [a.reshape(bl * t, 1, d) for a in (r, w, k, kn, b, v)]
    return pl.pallas_call(
        body, grid=(bl, npair // pg, nst), in_specs=[row_spec] * 6, out_specs=[col_spec, hist_spec],
        out_shape=[jax.ShapeDtypeStruct((bl, npair, t // TCH, RWKV_HEAD, LANES), F32),
                   jax.ShapeDtypeStruct((bl, npair, t, RWKV_HEAD, LANES), F32)],
        scratch_shapes=[pltpu.VMEM((pg, RWKV_HEAD, LANES), F32)] * 2,
        compiler_params=_params(("parallel", "parallel", "arbitrary")), name=name)(*rows3)


def scan_backward(r, w, k, kn, b, v, dy, hist, bl, t, d, name, pg, hch):
    npair, nst, nsub = d // LANES, t // hch, TCH // hch

    def body(r_ref, w_ref, k_ref, kn_ref, b_ref, v_ref, dy_ref, hist_ref,
             dr_ref, dw_ref, dk_ref, dkn_ref, db_ref, dv_ref, ds_ref, cur_ref):
        c = pl.program_id(2)
        sub = (nst - 1 - c) % nsub
        off = sub * hch
        lane, first_head, eye2, ones_blk = _scan_consts()
        step_lane = lane & (TCH - 1)
        eye2f = eye2.astype(F32)
        colsum = lambda x: jnp.sum(x, axis=0, keepdims=True)

        @pl.when(c == 0)
        def _():
            ds_ref[...] = jnp.zeros_like(ds_ref)

        @pl.when(sub == nsub - 1)
        def _():
            dv_ref[...] = jnp.zeros_like(dv_ref)

        for p in range(pg):
            cols = slice(p * LANES, (p + 1) * LANES)
            dyb = _head_sums(eye2f * dy_ref[hch - 1, :, cols], first_head)
            cur_ref[0, p] = _head_sums(eye2f * v_ref[hch - 1, :, cols], first_head)
            cur_ref[1, p] = dyb
            cur_ref[2, p] = _head_sums(hist_ref[0, p, hch - 1] * kn_ref[hch - 1, :, cols], first_head)
            ds_ref[p] += dyb * r_ref[hch - 1, :, cols]

        def step(it, carry):
            ts = hch - 1 - it
            prev = jnp.maximum(ts - 1, 0)
            has_prev = jnp.where(ts > 0, 1.0, 0.0)
            sel = (step_lane == off + ts).astype(F32)
            grads, tiles = [], []
            for p in range(pg):
                cols = slice(p * LANES, (p + 1) * LANES)
                ds = ds_ref[p]
                grads.append(ds)
                tiles.append(_spread(v_ref[prev, :, cols], eye2))
                tiles.append(_spread(dy_ref[prev, :, cols], eye2))
                tiles.append(_split2(hist_ref[0, p, pl.ds(prev, 1)][0] * kn_ref[prev, :, cols]))
                tiles.append(_split2(ds * k_ref[ts, :, cols]))
            res = _ones_dot(tiles, ones_blk)
            for p in range(pg):
                cols = slice(p * LANES, (p + 1) * LANES)
                row = lambda ref: ref[ts, :, cols]
                ds = grads[p]
                w_, kn_, b_ = row(w_ref), row(kn_ref), row(b_ref)
                dsa = _head_sums(ds * b_, first_head)
                s_prev = hist_ref[0, p, pl.ds(ts, 1)][0]
                vb, dyb, sa = cur_ref[0, p], cur_ref[1, p], cur_ref[2, p]
                s_t = s_prev * w_ + sa * b_ + vb * row(k_ref)
                dr_ref[ts, :, cols] = colsum(s_t * dyb)
                dk_ref[ts, :, cols] = colsum(ds * vb)
                db_ref[ts, :, cols] = colsum(ds * sa)
                dw_ref[ts, :, cols] = colsum(ds * s_prev)
                dkn_ref[ts, :, cols] = colsum(s_prev * dsa)
                ds_ref[p] = ds * w_ + dsa * kn_ + (res[4 * p + 1] * has_prev) * r_ref[prev, :, cols]
            for p in range(pg):
                cur_ref[0, p] = res[4 * p]
                cur_ref[1, p] = res[4 * p + 1]
                cur_ref[2, p] = res[4 * p + 2]
                dv_ref[0, p, 0] += res[4 * p + 3] * sel
            return carry

        lax.fori_loop(0, hch, step, 0)

    row_spec = pl.BlockSpec((hch, 1, pg * LANES), lambda bb, g, c: (bb * nst + nst - 1 - c, 0, g))
    col_spec = pl.BlockSpec((1, pg, 1, RWKV_HEAD, LANES), lambda bb, g, c: (bb, g, (nst - 1 - c) // nsub, 0, 0))
    hist_spec = pl.BlockSpec((1, pg, hch, RWKV_HEAD, LANES), lambda bb, g, c: (bb, g, nst - 1 - c, 0, 0))
    row_shape = jax.ShapeDtypeStruct((bl * t, 1, d), F32)
    rows3 = [a.reshape(bl * t, 1, d) for a in (r, w, k, kn, b, v, dy)]
    outs = pl.pallas_call(
        body, grid=(bl, npair // pg, nst), in_specs=[row_spec] * 7 + [hist_spec], out_specs=[row_spec] * 5 + [col_spec],
        out_shape=[row_shape] * 5 + [jax.ShapeDtypeStruct((bl, npair, t // TCH, RWKV_HEAD, LANES), F32)],
        scratch_shapes=[pltpu.VMEM((pg, RWKV_HEAD, LANES), F32), pltpu.VMEM((3, pg, RWKV_HEAD, LANES), F32)],
        compiler_params=_params(("parallel", "parallel", "arbitrary")), name=name)(*rows3, hist)
    return [o.reshape(bl * t, d) for o in outs[:5]] + [outs[5]]


def _mla_norms(pm, gq, gkv):
    ql = gq.shape[1]
    kvl = gkv.shape[1]
    return _rms(pm[:, :ql], gq), _rms(pm[:, ql:ql + kvl], gkv)


def mla_prep_fwd(pm, gq, gkv, name):
    n = pm.shape[0]
    return tilek(_mla_norms, [(pm, "r"), (gq, "f"), (gkv, "f")],
                 [("r", gq.shape[1], MMD), ("r", gkv.shape[1], MMD)], n_rows=n, tr=256, name=name)


def mla_prep_bwd(pm, gq, gkv, dcq, dckv, dkpe, name):
    n, wm = pm.shape
    ql, kvl = gq.shape[1], gkv.shape[1]

    def fn(pmv, gqv, gkvv, d1, d2, d3):
        _, vjp1 = jax.vjp(_rms, pmv[:, :ql], gqv)
        _, vjp2 = jax.vjp(_rms, pmv[:, ql:ql + kvl], gkvv)
        dcq_in, dgq = vjp1(d1)
        dckv_in, dgkv = vjp2(d2)
        return jnp.concatenate([dcq_in, dckv_in, d3], axis=1), dgq, dgkv

    return tilek(fn, [(pm, "r"), (gq, "f"), (gkv, "f"), (dcq, "r"), (dckv, "r"), (dkpe, "r")],
                 [("r", wm, F32), ("acc", gq.shape), ("acc", gkv.shape)], n_rows=n, tr=128, name=name)


def _rope(x, c, s, first):
    sw = jnp.where(first, pltpu.roll(x, LANES - ROPE_DIM // 2, 1), pltpu.roll(x, ROPE_DIM // 2, 1))
    return x * c + sw * s


def _unrope(d, c, s, first):
    z = d * s
    sw = jnp.where(first, pltpu.roll(z, LANES - ROPE_DIM // 2, 1), pltpu.roll(z, ROPE_DIM // 2, 1))
    return d * c + sw


def _causal_segments(n_tiles, parts=4):
    bounds = sorted({round(n_tiles * s / parts) for s in range(parts + 1)})
    return list(zip(bounds[:-1], bounds[1:]))


def attn_fwd(q, kv, pm, ct, st, bl, t, hm, name):
    n = q.shape[0]
    tq = LANES
    scale = QK_DIM ** -0.5
    kpe_blk = pm.shape[1] // LANES - 1

    def body(qn_ref, qpe_ref, kn_ref, v_ref, kpe_ref, ct_ref, st_ref, o_ref, lse_ref, kp_s, kn_s, v_s):
        h = pl.program_id(1)
        lane = lax.broadcasted_iota(jnp.int32, (1, LANES), 1)
        first = (lane & (ROPE_DIM - 1)) < ROPE_DIM // 2
        kp = _rope(kpe_ref[...], ct_ref[...], st_ref[...], first)
        kp_s[...] = jnp.where(h % 2 == 0, kp, pltpu.roll(kp, ROPE_DIM, 1)).astype(MMD)
        kn_s[...] = kn_ref[...].astype(MMD)
        v_s[...] = v_ref[...].astype(MMD)
        def segment(lo, hi):
            ext = hi * tq
            kpos = lax.broadcasted_iota(jnp.int32, (1, ext), 1)

            def qtile(i, carry):
                rows = pl.ds(pl.multiple_of(i * tq, tq), tq)
                q2 = _rope(qpe_ref[rows, :], ct_ref[rows, :], st_ref[rows, :], first)
                s = (_mm(qn_ref[rows, :], kn_s[:ext, :], ((1,), (1,))) + _mm(q2, kp_s[:ext, :], ((1,), (1,)))) * scale
                qpos = i * tq + lax.broadcasted_iota(jnp.int32, (tq, 1), 0)
                s = jnp.where(kpos <= qpos, s, -1e30)
                m = jnp.max(s, axis=1, keepdims=True)
                p = jnp.exp(s - m)
                l = jnp.sum(p, axis=1, keepdims=True)
                o_ref[rows, :] = _mm(p, v_s[:ext, :]) / l
                lse_ref[0, 0, rows, :] = m + jnp.log(l)
                return carry

            lax.fori_loop(lo, hi, qtile, 0)

        for lo, hi in _causal_segments(t // tq):
            segment(lo, hi)

    blk = lambda f: pl.BlockSpec((t, LANES), f)
    return pl.pallas_call(
        body, grid=(bl, hm),
        in_specs=[blk(lambda b, h: (b, h)), blk(lambda b, h: (b, hm + h // 2)), blk(lambda b, h: (b, h)),
                  blk(lambda b, h: (b, hm + h)), blk(lambda b, h: (b, kpe_blk)), blk(lambda b, h: (0, 0)), blk(lambda b, h: (0, 0))],
        out_specs=[blk(lambda b, h: (b, h)), pl.BlockSpec((1, 1, t, 1), lambda b, h: (b, h, 0, 0))],
        out_shape=[jax.ShapeDtypeStruct((n, hm * LANES), F32), jax.ShapeDtypeStruct((bl, hm, t, 1), F32)],
        scratch_shapes=[pltpu.VMEM((t, LANES), MMD)] * 3,
        compiler_params=_params(("parallel", "arbitrary")), name=name)(q, q, kv, kv, pm, ct, st)


def attn_bwd(q, kv, pm, o, do, lse, ct, st, bl, t, hm, name):
    n = q.shape[0]
    tq = LANES
    scale = QK_DIM ** -0.5
    kpe_blk = pm.shape[1] // LANES - 1

    def body(qn_ref, qpe_ref, kn_ref, v_ref, kpe_ref, o_ref, do_ref, lse_ref, ct_ref, st_ref,
             dqn_ref, dqpe_ref, dkn_ref, dv_ref, dkpe_ref, kp_s, kn_s, v_s, dkn_s, dkp_s, dv_s):
        h = pl.program_id(1)
        lane = lax.broadcasted_iota(jnp.int32, (1, LANES), 1)
        first = (lane & (ROPE_DIM - 1)) < ROPE_DIM // 2
        mine = (lane // ROPE_DIM) == (h % 2)
        kp = _rope(kpe_ref[...], ct_ref[...], st_ref[...], first)
        kp_s[...] = jnp.where(h % 2 == 0, kp, pltpu.roll(kp, ROPE_DIM, 1)).astype(MMD)
        kn_s[...] = kn_ref[...].astype(MMD)
        v_s[...] = v_ref[...].astype(MMD)
        dkn_s[...] = jnp.zeros_like(dkn_s)
        dkp_s[...] = jnp.zeros_like(dkp_s)
        dv_s[...] = jnp.zeros_like(dv_s)
        @pl.when(h % 2 == 0)
        def _():
            dqpe_ref[...] = jnp.zeros_like(dqpe_ref)

        @pl.when(h == 0)
        def _():
            dkpe_ref[...] = jnp.zeros_like(dkpe_ref)

        def segment(lo, hi):
            ext = hi * tq
            kpos = lax.broadcasted_iota(jnp.int32, (1, ext), 1)

            def qtile(i, carry):
                rows = pl.ds(pl.multiple_of(i * tq, tq), tq)
                c_i, s_i = ct_ref[rows, :], st_ref[rows, :]
                q1 = qn_ref[rows, :].astype(MMD)
                q2 = _rope(qpe_ref[rows, :], c_i, s_i, first).astype(MMD)
                s = (_mm(q1, kn_s[:ext, :], ((1,), (1,))) + _mm(q2, kp_s[:ext, :], ((1,), (1,)))) * scale
                qpos = i * tq + lax.broadcasted_iota(jnp.int32, (tq, 1), 0)
                p = jnp.where(kpos <= qpos, jnp.exp(s - lse_ref[0, 0, rows, :]), 0.0)
                do_i = do_ref[rows, :]
                delta = jnp.sum(do_i * o_ref[rows, :], axis=1, keepdims=True)
                dp = _mm(do_i, v_s[:ext, :], ((1,), (1,)))
                ds = (p * (dp - delta) * scale).astype(MMD)
                dqn_ref[rows, :] = _mm(ds, kn_s[:ext, :])
                dq2 = jnp.where(mine, _mm(ds, kp_s[:ext, :]), 0.0)
                dqpe_ref[rows, :] += _unrope(dq2, c_i, s_i, first)
                dkn_s[:ext, :] += _mm(ds, q1, ((0,), (0,)))
                dkp_s[:ext, :] += _mm(ds, q2, ((0,), (0,)))
                dv_s[:ext, :] += _mm(p, do_i, ((0,), (0,)))
                return carry

            lax.fori_loop(lo, hi, qtile, 0)

        for lo, hi in _causal_segments(t // tq):
            segment(lo, hi)
        dkn_ref[...] = dkn_s[...]
        dv_ref[...] = dv_s[...]
        dkp = jnp.where(mine, dkp_s[...], 0.0)
        dkp = jnp.where(h % 2 == 0, dkp, pltpu.roll(dkp, ROPE_DIM, 1))
        dkpe_ref[...] += _unrope(dkp, ct_ref[...], st_ref[...], first)

    blk = lambda f: pl.BlockSpec((t, LANES), f)
    hd = lambda b, h: (b, h)
    shp = lambda wd: jax.ShapeDtypeStruct((n, wd), F32)
    return pl.pallas_call(
        body, grid=(bl, hm),
        in_specs=[blk(hd), blk(lambda b, h: (b, hm + h // 2)), blk(hd), blk(lambda b, h: (b, hm + h)),
                  blk(lambda b, h: (b, kpe_blk)), blk(hd), blk(hd), pl.BlockSpec((1, 1, t, 1), lambda b, h: (b, h, 0, 0)),
                  blk(lambda b, h: (0, 0)), blk(lambda b, h: (0, 0))],
        out_specs=[blk(hd), blk(lambda b, h: (b, h // 2)), blk(hd), blk(hd), blk(lambda b, h: (b, 0))],
        out_shape=[shp(hm * LANES), shp(hm * ROPE_DIM), shp(hm * LANES), shp(hm * LANES), shp(LANES)],
        scratch_shapes=[pltpu.VMEM((t, LANES), MMD)] * 3 + [pltpu.VMEM((t, LANES), F32)] * 3,
        compiler_params=_params(("parallel", "arbitrary")), name=name)(q, q, kv, kv, pm, o, do, lse, ct, st)


def _peer(k):
    mx, my, mc = lax.axis_index("x"), lax.axis_index("y"), lax.axis_index("c")
    px = 1 - mx if k & 4 else mx
    py = 1 - my if k & 2 else my
    pc = 1 - mc if k & 1 else mc
    return (px, py, pc), 4 * px + 2 * py + pc


def all_gather(x, name):
    def body(x_ref, o_ref, send_sems, recv_sems, local_sem):
        _, me = _peer(0)
        local = pltpu.make_async_copy(x_ref, o_ref.at[me], local_sem)
        local.start()
        copies = []
        for k in range(1, N_DEV):
            dev, _ = _peer(k)
            cp = pltpu.make_async_remote_copy(src_ref=x_ref, dst_ref=o_ref.at[me], send_sem=send_sems.at[k - 1],
                                              recv_sem=recv_sems.at[k - 1], device_id=dev, device_id_type=MESH)
            cp.start()
            copies.append(cp)
        for cp in copies:
            cp.wait()
        local.wait()

    return pl.pallas_call(
        body, in_specs=[pl.BlockSpec(memory_space=pl.ANY)], out_specs=pl.BlockSpec(memory_space=pl.ANY),
        out_shape=jax.ShapeDtypeStruct((N_DEV,) + x.shape, x.dtype),
        scratch_shapes=[pltpu.SemaphoreType.DMA((N_DEV - 1,)), pltpu.SemaphoreType.DMA((N_DEV - 1,)), pltpu.SemaphoreType.DMA],
        name=name)(x)


def all_to_all(x, name):
    def body(x_ref, o_ref, send_sems, recv_sems, local_sem):
        _, me = _peer(0)
        local = pltpu.make_async_copy(x_ref.at[me], o_ref.at[me], local_sem)
        local.start()
        copies = []
        for k in range(1, N_DEV):
            dev, idx = _peer(k)
            cp = pltpu.make_async_remote_copy(src_ref=x_ref.at[idx], dst_ref=o_ref.at[me], send_sem=send_sems.at[k - 1],
                                              recv_sem=recv_sems.at[k - 1], device_id=dev, device_id_type=MESH)
            cp.start()
            copies.append(cp)
        for cp in copies:
            cp.wait()
        local.wait()

    return pl.pallas_call(
        body, in_specs=[pl.BlockSpec(memory_space=pl.ANY)], out_specs=pl.BlockSpec(memory_space=pl.ANY),
        out_shape=jax.ShapeDtypeStruct(x.shape, x.dtype),
        scratch_shapes=[pltpu.SemaphoreType.DMA((N_DEV - 1,)), pltpu.SemaphoreType.DMA((N_DEV - 1,)), pltpu.SemaphoreType.DMA],
        name=name)(x)


def _chips():
    mx, my, mc = lax.axis_index("x"), lax.axis_index("y"), lax.axis_index("c")
    return (mx, my, mc), (mx, my, 1 - mc), [(1 - mx, my), (mx, 1 - my), (1 - mx, 1 - my)]


def all_gather_two_level(x, name):
    def body(x_ref, o_ref, send_sems, recv_sems, local_sem):
        me, sibling, chips = _chips()
        blk = lambda px, py, pc: o_ref.at[4 * px + 2 * py + pc]

        def copy(k, block, to, src=None):
            return pltpu.make_async_remote_copy(src_ref=blk(*block) if src is None else src, dst_ref=blk(*block),
                                                send_sem=send_sems.at[k], recv_sem=recv_sems.at[k], device_id=to,
                                                device_id_type=MESH)

        mine = pltpu.make_async_copy(x_ref, blk(*me), local_sem)
        mine.start()
        first = [copy(0, me, sibling, src=x_ref)] + [copy(1 + j, me, (*chip, me[2]), src=x_ref) for j, chip in enumerate(chips)]
        for cp in first:
            cp.start()
        passed = [copy(4 + j, (*chip, me[2]), sibling) for j, chip in enumerate(chips)]
        for j, chip in enumerate(chips):
            copy(1 + j, (*chip, me[2]), me).wait_recv()
            passed[j].start()
        copy(0, sibling, me).wait_recv()
        for j, chip in enumerate(chips):
            copy(4 + j, (*chip, 1 - me[2]), me).wait_recv()
        for cp in first + passed:
            cp.wait_send()
        mine.wait()

    return pl.pallas_call(
        body, in_specs=[pl.BlockSpec(memory_space=pl.ANY)], out_specs=pl.BlockSpec(memory_space=pl.ANY),
        out_shape=jax.ShapeDtypeStruct((N_DEV,) + x.shape, x.dtype),
        scratch_shapes=[pltpu.SemaphoreType.DMA((N_DEV - 1,)), pltpu.SemaphoreType.DMA((N_DEV - 1,)), pltpu.SemaphoreType.DMA],
        name=name)(x)


def exchange_sibling(x, name):
    def body(x_ref, o_ref, send_sems, recv_sems):
        me, sibling, _ = _chips()
        copies = []
        for q in range(N_DEV // 2):
            cp = pltpu.make_async_remote_copy(src_ref=x_ref.at[2 * q + 1 - me[2]], dst_ref=o_ref.at[q], send_sem=send_sems.at[q],
                                              recv_sem=recv_sems.at[q], device_id=sibling, device_id_type=MESH)
            cp.start()
            copies.append(cp)
        for cp in copies:
            cp.wait()

    return pl.pallas_call(
        body, in_specs=[pl.BlockSpec(memory_space=pl.ANY)], out_specs=pl.BlockSpec(memory_space=pl.ANY),
        out_shape=jax.ShapeDtypeStruct((N_DEV // 2,) + x.shape[1:], x.dtype),
        scratch_shapes=[pltpu.SemaphoreType.DMA((N_DEV // 2,)), pltpu.SemaphoreType.DMA((N_DEV // 2,))], name=name)(x)


def exchange_chips(x, name):
    def body(x_ref, o_ref, send_sems, recv_sems, local_sem):
        me, _, chips = _chips()
        here = 2 * me[0] + me[1]
        local = pltpu.make_async_copy(x_ref.at[here], o_ref.at[here], local_sem)
        local.start()
        copies = []
        for j, (px, py) in enumerate(chips):
            cp = pltpu.make_async_remote_copy(src_ref=x_ref.at[2 * px + py], dst_ref=o_ref.at[here], send_sem=send_sems.at[j],
                                              recv_sem=recv_sems.at[j], device_id=(px, py, me[2]), device_id_type=MESH)
            cp.start()
            copies.append(cp)
        for cp in copies:
            cp.wait()
        local.wait()

    return pl.pallas_call(
        body, in_specs=[pl.BlockSpec(memory_space=pl.ANY)], out_specs=pl.BlockSpec(memory_space=pl.ANY),
        out_shape=jax.ShapeDtypeStruct(x.shape, x.dtype),
        scratch_shapes=[pltpu.SemaphoreType.DMA((3,)), pltpu.SemaphoreType.DMA((3,)), pltpu.SemaphoreType.DMA], name=name)(x)


def add_blocks(a, b, name):
    q, r, c = a.shape
    tr = _tile(r, max(16, (2 << 20) // (c * a.dtype.itemsize)), 16)
    spec = pl.BlockSpec((1, tr, c), lambda i, j: (i, j, 0))

    def body(a_ref, b_ref, o_ref):
        o_ref[...] = (a_ref[...].astype(F32) + b_ref[...].astype(F32)).astype(o_ref.dtype)

    return pl.pallas_call(
        body, grid=(q, r // tr), in_specs=[spec, spec], out_specs=spec, out_shape=jax.ShapeDtypeStruct(a.shape, a.dtype),
        compiler_params=_params(("parallel", "parallel")), name=name)(a, b)


def reduce_scatter_two_level(x, tag):
    q = N_DEV // 2
    from_sibling = exchange_sibling(x, f"{tag}_sibling")
    mine = lax.dynamic_index_in_dim(x.reshape((q, 2) + x.shape[1:]), lax.axis_index("c"), axis=1, keepdims=False)
    chip_sums = add_blocks(mine, from_sibling, f"{tag}_pair_sum")
    return sum_blocks(exchange_chips(chip_sums, f"{tag}_chips"), f"{tag}_sum")


def sum_blocks(x, name):
    nb, r, c = x.shape
    tr = _tile(r, max(16, (4 << 20) // (nb * c * x.dtype.itemsize)), 16)

    def body(x_ref, o_ref):
        acc = x_ref[0].astype(F32)
        for i in range(1, nb):
            acc = acc + x_ref[i].astype(F32)
        o_ref[...] = acc

    return pl.pallas_call(
        body, grid=(r // tr,), in_specs=[pl.BlockSpec((nb, tr, c), lambda i: (0, i, 0))],
        out_specs=pl.BlockSpec((tr, c), lambda i: (i, 0)), out_shape=jax.ShapeDtypeStruct((r, c), F32),
        compiler_params=_params(("parallel",)), name=name)(x)


def _adamw(w, g, m, v):
    m = ADAM_B1 * m + (1.0 - ADAM_B1) * g
    v = ADAM_B2 * v + (1.0 - ADAM_B2) * jnp.square(g)
    m_hat = m / (1.0 - ADAM_B1 ** ADAM_STEP)
    v_hat = v / (1.0 - ADAM_B2 ** ADAM_STEP)
    delta = -ADAM_LR * (m_hat / (jnp.sqrt(v_hat) + ADAM_EPS) + ADAM_WD * w)
    return delta, m, v


def adamw(w, g, m, v, name):
    r, c = w.shape
    tr = _tile(r, 256, 8)
    spec = pl.BlockSpec((tr, c), lambda i: (i, 0))

    def body(w_ref, g_ref, m_ref, v_ref, d_ref, nm_ref, nv_ref):
        d_ref[...], nm_ref[...], nv_ref[...] = _adamw(w_ref[...], g_ref[...], m_ref[...], v_ref[...])

    return pl.pallas_call(
        body, grid=(r // tr,), in_specs=[spec] * 4, out_specs=[spec] * 3,
        out_shape=[jax.ShapeDtypeStruct((r, c), F32)] * 3, compiler_params=_params(("parallel",)), name=name)(w, g, m, v)


def batch_sum_rows(dh, bl, t, rows, name):
    d = dh.shape[1]

    def body(x_ref, o_ref):
        @pl.when(pl.program_id(0) == 0)
        def _():
            o_ref[...] = jnp.zeros_like(o_ref)

        o_ref[...] += x_ref[...]

    return pl.pallas_call(
        body, grid=(bl,), in_specs=[pl.BlockSpec((rows, d), lambda b: (b * (t // rows), 0))],
        out_specs=pl.BlockSpec((rows, d), lambda b: (0, 0)), out_shape=jax.ShapeDtypeStruct((rows, d), F32),
        compiler_params=_params(("arbitrary",)), name=name)(dh)


class Dims:
    def __init__(self, x, meta_full_cols, w_up, g_up, q_norm, kv_norm, d_ff):
        self.bl, self.seq, self.d = x.shape
        self.n_meta = 16
        self.t_real = self.n_meta + self.seq
        self.t = -(-self.t_real // LANES) * LANES
        self.n = self.bl * self.t
        self.f = d_ff
        self.wl, self.gl = w_up.shape[-2], g_up.shape[-2]
        self.ql, self.kvl = q_norm.shape[-1], kv_norm.shape[-1]
        self.hm = self.d // V_DIM
        self.in_cols = 5 * self.d + 2 * self.wl + self.gl + self.ql + self.kvl + ROPE_DIM


def _pad_cols(a, width):
    return jnp.pad(a, ((0, 0), (0, width - a.shape[1])))


def _pad_rows(a, rows):
    return jnp.pad(a, ((0, rows - a.shape[0]), (0, 0)))


def split_in(a, dm, axis=1):
    d, wl, gl, ql, kvl = dm.d, dm.wl, dm.gl, dm.ql, dm.kvl
    size = a.shape[axis]
    cut = lambda lo, hi: lax.slice_in_dim(a, min(lo, size), min(hi, size), axis=axis)

    def pad(p, width):
        cfg = [(0, 0)] * a.ndim
        cfg[axis] = (0, width - p.shape[axis])
        return jnp.pad(p, cfg)

    o = 3 * d
    lora = jnp.concatenate([pad(cut(o, o + wl), LANES), pad(cut(o + wl, o + 2 * wl), LANES),
                            cut(o + 2 * wl, o + 2 * wl + gl)], axis=axis)
    o += 2 * wl + gl
    mla = pad(cut(o, o + ql + kvl + ROPE_DIM), ql + kvl + LANES)
    o += ql + kvl + ROPE_DIM
    return dict(r=cut(0, d), k=cut(d, 2 * d), v=cut(2 * d, 3 * d), l=lora, m=mla, ga=cut(o, o + d), gb=cut(o + d, o + 2 * d))


def merge_in(g, dm, axis=1):
    wl, gl, ql, kvl = dm.wl, dm.gl, dm.ql, dm.kvl
    cut = lambda p, lo, hi: lax.slice_in_dim(p, lo, hi, axis=axis)
    l, m = g["l"], g["m"]
    return jnp.concatenate([g["r"], g["k"], g["v"], cut(l, 0, wl), cut(l, LANES, LANES + wl), cut(l, 2 * LANES, 2 * LANES + gl),
                            cut(m, 0, ql + kvl + ROPE_DIM), g["ga"], g["gb"]], axis=axis)


def split_uq(w, dm):
    w3 = w.reshape(w.shape[0], dm.hm, QK_DIM)
    return jnp.concatenate([w3[:, :, :NOPE_DIM].reshape(w.shape[0], -1), w3[:, :, NOPE_DIM:].reshape(w.shape[0], -1)], axis=1)


def merge_uq(gn, gp, dm):
    r = gn.shape[0]
    return jnp.concatenate([gn.reshape(r, dm.hm, NOPE_DIM), gp.reshape(r, dm.hm, ROPE_DIM)], axis=2).reshape(r, -1)


def split_ukv(w, dm):
    w3 = w.reshape(w.shape[0], dm.hm, NOPE_DIM + V_DIM)
    return jnp.concatenate([w3[:, :, :NOPE_DIM].reshape(w.shape[0], -1), w3[:, :, NOPE_DIM:].reshape(w.shape[0], -1)], axis=1)


def merge_ukv(gk, gv, dm):
    r = gk.shape[0]
    return jnp.concatenate([gk.reshape(r, dm.hm, NOPE_DIM), gv.reshape(r, dm.hm, V_DIM)], axis=2).reshape(r, -1)


def head_matrices(d):
    heads = d // RWKV_HEAD
    e = (np.arange(d)[:, None] // RWKV_HEAD == np.arange(LANES)[None, :]) & (np.arange(LANES)[None, :] < heads)
    return jnp.asarray(e, BF16), jnp.asarray(e.T, BF16)


def rope_tables(t):
    pos = jnp.arange(t, dtype=F32)
    inv_freq = 1.0 / (ROPE_THETA ** (jnp.arange(0, ROPE_DIM, 2, dtype=F32) / ROPE_DIM))
    ang = pos[:, None] * inv_freq[None, :]
    cos, sin = jnp.cos(ang), jnp.sin(ang)
    return jnp.tile(jnp.concatenate([cos, cos], axis=1), (1, 2)), jnp.tile(jnp.concatenate([-sin, sin], axis=1), (1, 2))


def local_step(dm, x, loss_target, meta, wt, sp):
    bl, t, n, d, hm = dm.bl, dm.t, dm.n, dm.d, dm.hm
    e, et = head_matrices(d)
    ct, st = rope_tables(t)
    padz = jnp.zeros((bl, t - dm.t_real, d), F32)
    h0 = jnp.concatenate([jnp.broadcast_to(meta[None], (bl, dm.n_meta, d)), x, padz], axis=1).reshape(n, d)
    tgt = jnp.concatenate([jnp.zeros((bl, dm.n_meta, d), F32), loss_target, padz], axis=1).reshape(n, d)
    tpos = jnp.arange(t)
    mask = jnp.tile(((tpos >= dm.n_meta) & (tpos < dm.t_real)).astype(F32), bl).reshape(n, 1)

    win = split_in(wt["w_in"], dm, axis=0)
    mu = split_in(sp["tm_mu"], dm)
    wq, wkv = split_uq(wt["w_uq"], dm), split_ukv(wt["w_ukv"], dm)
    prm = dict(w0=sp["w0"], a0=sp["a0"], k_k=sp["k_k"], k_a=sp["k_a"], gn_w=sp["gn_w"], gn_b=sp["gn_b"], r_k=sp["r_k"],
               w_up=_pad_rows(wt["w_up"], LANES).astype(F32), a_up=_pad_rows(wt["a_up"], LANES).astype(F32),
               g_up=wt["g_up"].astype(F32))

    h1, ffn1 = ffn_forward(h0, sp["ffn1_norm"], wt["ffn1_w_gate"], wt["ffn1_w_up"], wt["ffn1_w_down"], "ffn1")
    u = rms_fwd(h1, sp["mix_norm"], "mix_rms")
    proj = {key: matmul([(u, win[key])], "nt", name=f"proj_{key}") for key in win}
    sh = {key: lerp_fwd(proj[key], mu[key], bl, t, f"shift_{key}") for key in ("r", "k", "v", "l")}
    decay, kmod, kneg, bvec, gate = rwkv_prep_fwd(sh["k"], sh["l"], prm, e, et, "rwkv_prep")
    pairs = min(SCAN_PAIRS, d // LANES)
    y_col, hist = scan_forward(sh["r"], decay, kmod, kneg, bvec, sh["v"], bl, t, d, "wkv_scan", pairs, SCAN_FWD_STEPS)
    y = to_row(y_col, bl, t, d)
    cqn, ckvn = mla_prep_fwd(proj["m"], sp["q_norm"], sp["kv_norm"], "mla_norms")
    q = matmul([(cqn, wq)], "nn", name="mla_q")
    kv = matmul([(ckvn, wkv)], "nn", name="mla_kv")
    o, lse = attn_fwd(q, kv, proj["m"], ct, st, bl, t, hm, "mla_attn")
    post_in = [y, sh["r"], kmod, sh["v"], gate, proj["ga"], proj["gb"], o]
    mix = rwkv_post_fwd(post_in, prm, e, et, "mix_gate")
    h2 = matmul([(mix, wt["w_out"])], "nn", res=h1, name="out_proj")
    h3, ffn2 = ffn_forward(h2, sp["ffn2_norm"], wt["ffn2_w_gate"], wt["ffn2_w_up"], wt["ffn2_w_down"], "ffn2")
    dh3, d_final, loss = loss_head(h3, tgt, mask, sp["final_norm"], "loss_head")

    gw, gs = {}, {"final_norm": d_final}
    dh2, gs["ffn2_norm"], gw["ffn2_w_gate"], gw["ffn2_w_up"], gw["ffn2_w_down"] = ffn_backward(
        dh3, h2, sp["ffn2_norm"], wt["ffn2_w_gate"], wt["ffn2_w_up"], wt["ffn2_w_down"], ffn2, "ffn2")
    dmix = matmul([(dh2, wt["w_out"])], "nt", name="out_proj_dx")
    gw["w_out"] = matmul([(mix, dh2)], "tn", name="out_proj_dw")
    (dy, dr_p, dkm_p, dv_p, dgate, dpga, dpgb, do, gs["gn_w"], gs["gn_b"], gs["r_k"]) = rwkv_post_bwd(
        post_in, prm, e, et, dmix, "mix_gate_bwd")
    dqn, dqpe, dkn, dv_att, dkpe = attn_bwd(q, kv, proj["m"], o, do, lse, ct, st, bl, t, hm, "mla_attn_bwd")
    nq = hm * NOPE_DIM
    dcqn = matmul([(dqn, wq[:, :nq])], "nt", name="mla_q_dx1")
    dcqn = matmul([(dqpe, wq[:, nq:])], "nt", res=dcqn, name="mla_q_dx2")
    gw["w_uq"] = merge_uq(matmul([(cqn, dqn)], "tn", name="mla_q_dw1"), matmul([(cqn, dqpe)], "tn", name="mla_q_dw2"), dm)
    dckvn = matmul([(dkn, wkv[:, :nq]), (dv_att, wkv[:, nq:])], "nt", name="mla_kv_dx")
    gw["w_ukv"] = merge_ukv(matmul([(ckvn, dkn)], "tn", name="mla_kv_dw1"), matmul([(ckvn, dv_att)], "tn", name="mla_kv_dw2"), dm)
    dproj = {"ga": dpga, "gb": dpgb}
    dproj["m"], gs["q_norm"], gs["kv_norm"] = mla_prep_bwd(proj["m"], sp["q_norm"], sp["kv_norm"], dcqn, dckvn, dkpe, "mla_norms_bwd")
    dr_s, ddecay, dk_s, dkneg, dbvec, dv_col = scan_backward(sh["r"], decay, kmod, kneg, bvec, sh["v"], dy, hist,
                                                             bl, t, d, "wkv_scan_bwd", pairs, SCAN_BWD_STEPS)
    dv_s = to_row(dv_col, bl, t, d)
    (dsh_k, dsh_l, gs["w0"], gs["a0"], gs["k_k"], gs["k_a"], g_wup, g_aup, gw["g_up"]) = rwkv_prep_bwd(
        sh["k"], sh["l"], prm, e, et, [ddecay, dk_s, dkm_p, dkneg, dbvec, dgate], "rwkv_prep_bwd")
    gw["w_up"], gw["a_up"] = g_wup[:dm.wl], g_aup[:dm.wl]
    dmu = {}
    for key, cts in (("r", [dr_s, dr_p]), ("k", [dsh_k]), ("v", [dv_s, dv_p]), ("l", [dsh_l])):
        dproj[key], dmu[key] = lerp_bwd(proj[key], mu[key], cts, bl, t, f"shift_{key}_bwd")
    zero_m = jnp.zeros((1, proj["m"].shape[1]), F32)
    gs["tm_mu"] = merge_in(dict(dmu, m=zero_m, ga=zero_m[:, :0], gb=zero_m[:, :0]), dm)[:, :3 * d + 2 * dm.wl + dm.gl]
    wide = ("r", "k", "v", "ga", "gb")
    du = matmul([(dproj[key], win[key]) for key in wide], "nn", name="proj_dx", tk=512)
    du = matmul([(dproj["l"], win["l"])], "nn", res=du, name="proj_dx_l")
    du = matmul([(dproj["m"], win["m"])], "nn", res=du, name="proj_dx_m")
    gw["w_in"] = merge_in({key: matmul([(dproj[key], u)], "tn", name=f"proj_dw_{key}") for key in win}, dm, axis=0)
    dh1, gs["mix_norm"] = rms_bwd(h1, sp["mix_norm"], du, dh2, "mix_rms_bwd")
    dh0, gs["ffn1_norm"], gw["ffn1_w_gate"], gw["ffn1_w_up"], gw["ffn1_w_down"] = ffn_backward(
        dh1, h0, sp["ffn1_norm"], wt["ffn1_w_gate"], wt["ffn1_w_up"], wt["ffn1_w_down"], ffn1, "ffn1")
    grad_x = dh0.reshape(bl, t, d)[:, dm.n_meta:dm.t_real]
    dmeta = batch_sum_rows(dh0, bl, t, dm.n_meta, "meta_grad")
    return loss, grad_x, dmeta, gw, gs


COL_SHARDED = ("ffn1_w_gate", "ffn1_w_up", "w_in", "w_up", "a_up", "g_up", "w_uq", "w_ukv", "ffn2_w_gate", "ffn2_w_up")
ROW_SHARDED = ("ffn1_w_down", "w_out", "ffn2_w_down")
TRANSPOSED = ("ffn1_w_gate", "ffn1_w_up", "w_in", "ffn2_w_gate", "ffn2_w_up")
WIDE = ("ffn1_w_gate", "ffn1_w_up", "ffn1_w_down", "w_in", "w_out", "ffn2_w_gate", "ffn2_w_up", "ffn2_w_down")
NARROW = ("w_up", "a_up", "g_up", "w_uq", "w_ukv")
MATRICES = ("ffn1_w_gate", "ffn1_w_up", "ffn1_w_down", "w_in", "w_up", "a_up", "g_up", "w_uq", "w_ukv", "w_out",
            "ffn2_w_gate", "ffn2_w_up", "ffn2_w_down")
SMALL = ("ffn1_norm", "mix_norm", "tm_mu", "w0", "a0", "k_k", "k_a", "r_k", "gn_w", "gn_b", "q_norm", "kv_norm",
         "ffn2_norm", "final_norm")
WEIGHTS = ("meta_tokens", "ffn1_norm", "ffn1_w_gate", "ffn1_w_up", "ffn1_w_down", "mix_norm", "w_in", "tm_mu", "w0", "w_up",
           "a0", "a_up", "g_up", "k_k", "k_a", "r_k", "gn_w", "gn_b", "q_norm", "w_uq", "kv_norm", "w_ukv", "w_out",
           "ffn2_norm", "ffn2_w_gate", "ffn2_w_up", "ffn2_w_down", "final_norm")
PACK_COLS = 1024
PACK_ALIGN = 16 * PACK_COLS


def _pack(parts):
    offs, o = [], 0
    for p in parts:
        offs.append(o)
        o += p.shape[1]
    total = -(-o // PACK_ALIGN) * PACK_ALIGN
    flat = jnp.concatenate(list(parts) + [jnp.zeros((parts[0].shape[0], total - o), parts[0].dtype)], axis=1)
    return flat.reshape(parts[0].shape[0], total // PACK_COLS, PACK_COLS), offs


def kernel(x, meta_tokens, ffn1_norm, ffn1_w_gate, ffn1_w_up, ffn1_w_down, mix_norm, w_in, tm_mu, w0, w_up, a0, a_up, g_up, k_k, k_a, r_k, gn_w, gn_b, q_norm, w_uq, kv_norm, w_ukv, w_out, ffn2_norm, ffn2_w_gate, ffn2_w_up, ffn2_w_down, final_norm, loss_target, m_meta_tokens, m_ffn1_norm, m_ffn1_w_gate, m_ffn1_w_up, m_ffn1_w_down, m_mix_norm, m_w_in, m_tm_mu, m_w0, m_w_up, m_a0, m_a_up, m_g_up, m_k_k, m_k_a, m_r_k, m_gn_w, m_gn_b, m_q_norm, m_w_uq, m_kv_norm, m_w_ukv, m_w_out, m_ffn2_norm, m_ffn2_w_gate, m_ffn2_w_up, m_ffn2_w_down, m_final_norm, v_meta_tokens, v_ffn1_norm, v_ffn1_w_gate, v_ffn1_w_up, v_ffn1_w_down, v_mix_norm, v_w_in, v_tm_mu, v_w0, v_w_up, v_a0, v_a_up, v_g_up, v_k_k, v_k_a, v_r_k, v_gn_w, v_gn_b, v_q_norm, v_w_uq, v_kv_norm, v_w_ukv, v_w_out, v_ffn2_norm, v_ffn2_w_gate, v_ffn2_w_up, v_ffn2_w_down, v_final_norm):
    args = dict(locals())
    wts = {k: args[k] for k in WEIGHTS}
    ms = {k: args["m_" + k] for k in WEIGHTS}
    vs = {k: args["v_" + k] for k in WEIGHTS}
    dm = Dims(x, None, w_up, g_up, q_norm, kv_norm, ffn1_w_down.shape[1] * N_DEV)

    shard2d = {k: wts[k].reshape(wts[k].shape[-2], wts[k].shape[-1]) for k in MATRICES}
    sent = {k: shard2d[k].T if k in TRANSPOSED else shard2d[k] for k in MATRICES}
    wide_rows = np.cumsum([0] + [sent[k].shape[0] for k in WIDE])
    got_wide = all_gather_two_level(jnp.concatenate([sent[k].astype(MMD) for k in WIDE], axis=0), "gather_wide")
    full = {k: got_wide[:, lo:hi].reshape(-1, dm.d) for k, lo, hi in zip(WIDE, wide_rows[:-1], wide_rows[1:])}
    send, offs = _pack([sent[k].astype(MMD).reshape(1, -1) for k in NARROW])
    got = all_gather(send[0], "gather_narrow").reshape(N_DEV, -1)
    for k, o in zip(NARROW, offs):
        r, c = sent[k].shape
        full[k] = got[:, o:o + r * c].reshape(N_DEV, r, c).transpose(1, 0, 2).reshape(r, N_DEV * c)
    mr, mc = meta_tokens.shape
    meta = all_gather(meta_tokens, "gather_meta").transpose(1, 0, 2).reshape(mr, N_DEV * mc)
    small = {k: wts[k].reshape(1, -1) for k in SMALL}

    loss, grad_x, dmeta, gw, gs = local_step(dm, x, loss_target, meta, full, small)

    gwide = jnp.concatenate([gw[k].reshape(N_DEV, sent[k].shape[0], dm.d) for k in WIDE], axis=1).astype(MMD)
    gsum_wide = reduce_scatter_two_level(gwide, "scatter_wide")
    grads = {}
    for k, lo, hi in zip(WIDE, wide_rows[:-1], wide_rows[1:]):
        grads[k] = gsum_wide[lo:hi].T if k in TRANSPOSED else gsum_wide[lo:hi]

    def blocks(k, g):
        r, c = sent[k].shape
        return g.reshape(r, N_DEV, c).transpose(1, 0, 2).reshape(N_DEV, r * c)

    gsend, goffs = _pack([blocks(k, gw[k]).astype(MMD) for k in NARROW]
                         + [dmeta.reshape(mr, N_DEV, mc).transpose(1, 0, 2).reshape(N_DEV, mr * mc).astype(MMD)])
    gsum = sum_blocks(all_to_all(gsend, "scatter_narrow"), "sum_narrow").reshape(-1)
    for k, o in zip(NARROW, goffs):
        r, c = sent[k].shape
        grads[k] = gsum[o:o + r * c].reshape(r, c)
    grads["meta_tokens"] = gsum[goffs[-1]:goffs[-1] + mr * mc].reshape(mr, mc)

    ssend, soffs = _pack([gs[k].reshape(1, -1) for k in SMALL] + [loss])
    ssum = sum_blocks(all_gather(ssend[0], "gather_small"), "sum_small").reshape(-1)
    for k, o in zip(SMALL, soffs):
        grads[k] = ssum[o:o + small[k].shape[1]]
    loss_total = ssum[soffs[-1]]

    delta, new_m, new_v = {}, {}, {}
    for k in MATRICES + ("meta_tokens",):
        shp = wts[k].shape
        to2d = lambda a: a.reshape(shp[-2], shp[-1])
        dlt, nm, nv = adamw(to2d(wts[k]), grads[k], to2d(ms[k]), to2d(vs[k]), f"adamw_{k}")
        delta[k], new_m[k], new_v[k] = dlt.reshape(shp), nm.reshape(shp), nv.reshape(shp)
        grads[k] = grads[k].reshape(shp)
    pw, _ = _pack([wts[k].reshape(1, -1) for k in SMALL])
    pm_, _ = _pack([ms[k].reshape(1, -1) for k in SMALL])
    pv, _ = _pack([vs[k].reshape(1, -1) for k in SMALL])
    pg, poffs = _pack([grads[k].reshape(1, -1) for k in SMALL])
    dlt, nm, nv = adamw(pw[0], pg[0], pm_[0], pv[0], "adamw_small")
    for k, o in zip(SMALL, poffs):
        shp, sz = wts[k].shape, small[k].shape[1]
        cut = lambda a: a.reshape(-1)[o:o + sz].reshape(shp)
        delta[k], new_m[k], new_v[k] = cut(dlt), cut(nm), cut(nv)
        grads[k] = grads[k].reshape(shp)

    return (loss_total, grad_x, *[grads[k] for k in WEIGHTS], *[delta[k] for k in WEIGHTS],
            *[new_m[k] for k in WEIGHTS], *[new_v[k] for k in WEIGHTS])
```

```python
import functools

import numpy as np
import jax
import jax.numpy as jnp
from jax import lax
from jax.experimental import pallas as pl
from jax.experimental.pallas import tpu as pltpu

F32 = jnp.float32
BF16 = jnp.bfloat16
MMD = BF16

NORM_EPS = 1e-6
RWKV_HEAD = 64
GN_EPS = RWKV_HEAD * 1e-5
NOPE_DIM = 128
ROPE_DIM = 64
V_DIM = 128
QK_DIM = NOPE_DIM + ROPE_DIM
ROPE_THETA = 10000.0
ADAM_LR = 0.001
ADAM_B1 = 0.9
ADAM_B2 = 0.999
ADAM_EPS = 1e-08
ADAM_WD = 0.01
ADAM_STEP = 10

LANES = 128
SCAN_PAIRS = 8
SCAN_FWD_STEPS = 32
SCAN_BWD_STEPS = 16
N_DEV = 8
VMEM_LIMIT = 56 * 1024 * 1024
MESH = pl.DeviceIdType.MESH


def _tile(n, target, align):
    best = None
    for d in range(align, min(n, target) + 1, align):
        if n % d == 0:
            best = d
    return best if best is not None else n


def _params(sem=None):
    return pltpu.CompilerParams(dimension_semantics=sem, vmem_limit_bytes=VMEM_LIMIT)


def _mm(a, b, dims=((1,), (0,))):
    return lax.dot_general(a.astype(MMD), b.astype(MMD), (dims, ((), ())), preferred_element_type=F32)


@jax.custom_vjp
def mmdot(a, b):
    return _mm(a, b)


def _mmdot_fwd(a, b):
    return _mm(a, b), (a, b)


def _mmdot_bwd(res, g):
    a, b = res
    return _mm(g, b, ((1,), (1,))).astype(a.dtype), _mm(a, g, ((0,), (0,))).astype(b.dtype)


mmdot.defvjp(_mmdot_fwd, _mmdot_bwd)


def _dot2(x, m):
    hi = x.astype(BF16)
    lo = (x - hi.astype(F32)).astype(BF16)
    return (lax.dot_general(hi, m, (((1,), (0,)), ((), ())), preferred_element_type=F32)
            + lax.dot_general(lo, m, (((1,), (0,)), ((), ())), preferred_element_type=F32))


@jax.custom_vjp
def segsum(x, e, et):
    return _dot2(_dot2(x, e), et)


def _segsum_fwd(x, e, et):
    return segsum(x, e, et), (e, et)


def _segsum_bwd(res, g):
    e, et = res
    return segsum(g, e, et), jnp.zeros_like(e), jnp.zeros_like(et)


segsum.defvjp(_segsum_fwd, _segsum_bwd)


def _sigmoid(x):
    return 1.0 / (1.0 + jnp.exp(-x))


def _softplus(x):
    return jnp.maximum(x, 0.0) + jnp.log(1.0 + jnp.exp(-jnp.abs(x)))


def _rms(x, g):
    return x * lax.rsqrt(jnp.mean(x * x, axis=-1, keepdims=True) + NORM_EPS) * g


_DIMS = {"nn": ((1,), (0,)), "nt": ((1,), (1,)), "tn": ((0,), (0,))}


def matmul(pairs, mode, *, name, out_dtype=F32, res=None, alpha=1.0, tm=1088, tn=1024, tk=2048):
    a0, b0 = pairs[0]
    if mode == "nn":
        (m, k), n = a0.shape, b0.shape[1]
    elif mode == "nt":
        (m, k), n = a0.shape, b0.shape[0]
    else:
        (k, m), n = a0.shape, b0.shape[1]
    tm = _tile(m, tm, 128 if mode == "tn" else 16)
    tn = _tile(n, 2048 if mode == "tn" else tn, 128)
    tk = _tile(k, min(tk, 1024), 16) if mode == "tn" else _tile(k, tk, 128)
    nk = k // tk
    npair = len(pairs)
    if mode == "tn":
        a_spec = pl.BlockSpec((tk, tm), lambda i, j, kk: (kk, i))
    else:
        a_spec = pl.BlockSpec((tm, tk), lambda i, j, kk: (i, kk))
    if mode == "nt":
        b_spec = pl.BlockSpec((tn, tk), lambda i, j, kk: (j, kk))
    else:
        b_spec = pl.BlockSpec((tk, tn), lambda i, j, kk: (kk, j))
    o_spec = pl.BlockSpec((tm, tn), lambda i, j, kk: (i, j))
    dims = _DIMS[mode]

    def body(*refs):
        ab = refs[:2 * npair]
        res_ref = refs[2 * npair] if res is not None else None
        o_ref, acc_ref = refs[-2], refs[-1]
        kk = pl.program_id(2)

        @pl.when(kk == 0)
        def _():
            acc_ref[...] = jnp.zeros_like(acc_ref)

        part = _mm(ab[0][...], ab[1][...], dims)
        for p in range(1, npair):
            part = part + _mm(ab[2 * p][...], ab[2 * p + 1][...], dims)
        acc_ref[...] += part

        @pl.when(kk == nk - 1)
        def _():
            out = acc_ref[...] * alpha if alpha != 1.0 else acc_ref[...]
            if res_ref is not None:
                out = res_ref[...].astype(F32) + out
            o_ref[...] = out.astype(o_ref.dtype)

    args, specs = [], []
    for a, b in pairs:
        args += [a, b]
        specs += [a_spec, b_spec]
    if res is not None:
        args.append(res)
        specs.append(o_spec)
    return pl.pallas_call(
        body, grid=(m // tm, n // tn, nk), in_specs=specs, out_specs=o_spec,
        out_shape=jax.ShapeDtypeStruct((m, n), out_dtype), scratch_shapes=[pltpu.VMEM((tm, tn), F32)],
        compiler_params=_params(("parallel", "parallel", "arbitrary")), name=name)(*args)


def tilek(fn, ins, outs, *, n_rows, tr, name):
    tr = _tile(n_rows, tr, 16)
    n_in = len(ins)
    in_specs = []
    for arr, kind in ins:
        if kind == "r":
            in_specs.append(pl.BlockSpec((tr, arr.shape[1]), lambda i: (i, 0)))
        else:
            in_specs.append(pl.BlockSpec(arr.shape, lambda i, nd=arr.ndim: (0,) * nd))
    out_specs, out_shape = [], []
    has_acc = False
    for o in outs:
        if o[0] == "r":
            out_specs.append(pl.BlockSpec((tr, o[1]), lambda i: (i, 0)))
            out_shape.append(jax.ShapeDtypeStruct((n_rows, o[1]), o[2]))
        else:
            has_acc = True
            out_specs.append(pl.BlockSpec(o[1], lambda i, nd=len(o[1]): (0,) * nd))
            out_shape.append(jax.ShapeDtypeStruct(o[1], F32))

    def body(*refs):
        i = pl.program_id(0)
        vals = fn(*[r[...] for r in refs[:n_in]])
        for o, r, v in zip(outs, refs[n_in:], vals):
            if o[0] == "r":
                r[...] = v.astype(r.dtype)
            else:
                @pl.when(i == 0)
                def _(r=r):
                    r[...] = jnp.zeros_like(r)

                r[...] += v

    return pl.pallas_call(
        body, grid=(n_rows // tr,), in_specs=in_specs, out_specs=out_specs, out_shape=out_shape,
        compiler_params=_params(("arbitrary",) if has_acc else ("parallel",)), name=name)(*[a for a, _ in ins])


def rms_fwd(x, g, name):
    n, d = x.shape
    return tilek(lambda xv, gv: (_rms(xv, gv),), [(x, "r"), (g, "f")], [("r", d, MMD)], n_rows=n, tr=256, name=name)[0]


def rms_bwd(x, g, dy, dres, name):
    n, d = x.shape

    def fn(xv, gv, dyv, drv):
        _, vjp = jax.vjp(_rms, xv, gv)
        dx, dg = vjp(dyv.astype(F32))
        return drv + dx, dg

    return tilek(fn, [(x, "r"), (g, "f"), (dy, "r"), (dres, "r")], [("r", d, F32), ("acc", (1, d))],
                 n_rows=n, tr=128, name=name)


def loss_head(h, tgt, mask, g, name):
    n, d = h.shape

    def fn(hv, tv, mv, gv):
        def lossf(hh, gg):
            e = (_rms(hh, gg) - tv) * mv
            s = jnp.sum(jnp.sum(e * e, axis=1, keepdims=True), axis=0, keepdims=True)
            return s * (0.5 / d)

        l, vjp = jax.vjp(lossf, hv, gv)
        dh, dg = vjp(jnp.ones((1, 1), F32))
        return dh, dg, jnp.broadcast_to(l, (1, LANES))

    return tilek(fn, [(h, "r"), (tgt, "r"), (mask, "r"), (g, "f")],
                 [("r", d, F32), ("acc", (1, d)), ("acc", (1, LANES))], n_rows=n, tr=128, name=name)


def ffn_up(hn, wg, wu, name):
    n, d = hn.shape
    f = wg.shape[0]
    tm, tn = _tile(n, 544, 16), _tile(f, 512, 128)

    def body(a_ref, g_ref, u_ref, og_ref, ou_ref, oa_ref):
        a = a_ref[...]
        g = _mm(a, g_ref[...], ((1,), (1,)))
        u = _mm(a, u_ref[...], ((1,), (1,)))
        og_ref[...] = g
        ou_ref[...] = u
        oa_ref[...] = (g * _sigmoid(g) * u).astype(oa_ref.dtype)

    o_spec = pl.BlockSpec((tm, tn), lambda i, j: (i, j))
    w_spec = pl.BlockSpec((tn, d), lambda i, j: (j, 0))
    return pl.pallas_call(
        body, grid=(n // tm, f // tn), in_specs=[pl.BlockSpec((tm, d), lambda i, j: (i, 0)), w_spec, w_spec],
        out_specs=[o_spec, o_spec, o_spec],
        out_shape=[jax.ShapeDtypeStruct((n, f), F32), jax.ShapeDtypeStruct((n, f), F32), jax.ShapeDtypeStruct((n, f), MMD)],
        compiler_params=_params(("parallel", "parallel")), name=name)(hn, wg, wu)


def ffn_down_bwd(dh, wd, gate, up, name):
    n, d = dh.shape
    f = wd.shape[0]
    tm, tn = _tile(n, 544, 16), _tile(f, 1024, 128)

    def body(dh_ref, w_ref, g_ref, u_ref, dg_ref, du_ref):
        da = 0.5 * _mm(dh_ref[...], w_ref[...], ((1,), (1,)))
        g, u = g_ref[...], u_ref[...]
        s = _sigmoid(g)
        dg_ref[...] = (da * u * (s * (1.0 + g * (1.0 - s)))).astype(dg_ref.dtype)
        du_ref[...] = (da * (g * s)).astype(du_ref.dtype)

    o_spec = pl.BlockSpec((tm, tn), lambda i, j: (i, j))
    return pl.pallas_call(
        body, grid=(n // tm, f // tn),
        in_specs=[pl.BlockSpec((tm, d), lambda i, j: (i, 0)), pl.BlockSpec((tn, d), lambda i, j: (j, 0)), o_spec, o_spec],
        out_specs=[o_spec, o_spec],
        out_shape=[jax.ShapeDtypeStruct((n, f), MMD), jax.ShapeDtypeStruct((n, f), MMD)],
        compiler_params=_params(("parallel", "parallel")), name=name)(dh, wd, gate, up)


def ffn_forward(h, g, wg, wu, wd, tag):
    hn = rms_fwd(h, g, f"{tag}_rms")
    gate, up, act = ffn_up(hn, wg, wu, f"{tag}_up")
    out = matmul([(act, wd)], "nn", res=h, alpha=0.5, name=f"{tag}_down")
    return out, (hn, gate, up, act)


def ffn_backward(dout, h, g, wg, wu, wd, saved, tag):
    hn, gate, up, act = saved
    dgate, dup = ffn_down_bwd(dout, wd, gate, up, f"{tag}_dact")
    dwd = matmul([(act, dout)], "tn", alpha=0.5, name=f"{tag}_dwd")
    dwg = matmul([(dgate, hn)], "tn", name=f"{tag}_dwg")
    dwu = matmul([(dup, hn)], "tn", name=f"{tag}_dwu")
    dhn = matmul([(dgate, wg), (dup, wu)], "nn", name=f"{tag}_dhn")
    dh, dg = rms_bwd(h, g, dhn, dout, f"{tag}_drms")
    return dh, dg, dwg, dwu, dwd


def lerp_fwd(p, mu, bl, t, name):
    n, w = p.shape
    cb = _tile(w, 256, 128)

    def body(p_ref, mu_ref, o_ref):
        x = p_ref[...]
        row = lax.broadcasted_iota(jnp.int32, x.shape, 0)
        prev = jnp.where(row == 0, 0.0, pltpu.roll(x, 1, 0))
        o_ref[...] = x + mu_ref[...] * (prev - x)

    spec = pl.BlockSpec((t, cb), lambda b, j: (b, j))
    return pl.pallas_call(
        body, grid=(bl, w // cb), in_specs=[spec, pl.BlockSpec((1, cb), lambda b, j: (0, j))], out_specs=spec,
        out_shape=jax.ShapeDtypeStruct((n, w), F32), compiler_params=_params(("parallel", "parallel")), name=name)(p, mu)


def lerp_bwd(p, mu, douts, bl, t, name):
    n, w = p.shape
    cb = _tile(w, 256, 128)
    nd = len(douts)

    def body(*refs):
        p_ref, mu_ref = refs[0], refs[1]
        dp_ref, dmu_ref = refs[2 + nd], refs[3 + nd]
        b = pl.program_id(1)
        x, m = p_ref[...], mu_ref[...]
        d = refs[2][...]
        for r in refs[3:2 + nd]:
            d = d + r[...]
        row = lax.broadcasted_iota(jnp.int32, x.shape, 0)
        prev = jnp.where(row == 0, 0.0, pltpu.roll(x, 1, 0))
        z = d * m
        nxt = jnp.where(row == t - 1, 0.0, pltpu.roll(z, t - 1, 0))
        dp_ref[...] = d - z + nxt

        @pl.when(b == 0)
        def _():
            dmu_ref[...] = jnp.zeros_like(dmu_ref)

        dmu_ref[...] += jnp.sum(d * (prev - x), axis=0, keepdims=True)

    spec = pl.BlockSpec((t, cb), lambda j, b: (b, j))
    cspec = pl.BlockSpec((1, cb), lambda j, b: (0, j))
    return pl.pallas_call(
        body, grid=(w // cb, bl), in_specs=[spec, cspec] + [spec] * nd, out_specs=[spec, cspec],
        out_shape=[jax.ShapeDtypeStruct((n, w), F32), jax.ShapeDtypeStruct((1, w), F32)],
        compiler_params=_params(("parallel", "arbitrary")), name=name)(p, mu, *douts)


def _prep(k, xw, xa, xg, w0, a0, k_k, k_a, w_up, a_up, g_up, e, et):
    w_pre = -_softplus(-(w0 + mmdot(jnp.tanh(xw), w_up))) - 0.5
    decay = jnp.exp(-jnp.exp(w_pre))
    a = _sigmoid(a0 + mmdot(xa, a_up))
    g = mmdot(_sigmoid(xg), g_up)
    kk = k * k_k
    kk = kk * lax.rsqrt(jnp.maximum(segsum(kk * kk, e, et), 1e-24))
    kmod = k * (1.0 + (a - 1.0) * k_a)
    return decay, kmod, -kk, kk * a, g


def _lora_parts(xl):
    return xl[:, :LANES], xl[:, LANES:2 * LANES], xl[:, 2 * LANES:]


def rwkv_prep_fwd(pk, pl_, prm, e, et, name):
    n, d = pk.shape
    small = [prm[k] for k in ("w0", "a0", "k_k", "k_a", "w_up", "a_up", "g_up")]
    ins = [(pk, "r"), (pl_, "r")] + [(s, "f") for s in small] + [(e, "f"), (et, "f")]
    return tilek(lambda k, xl, *rest: _prep(k, *_lora_parts(xl), *rest), ins, [("r", d, F32)] * 5, n_rows=n, tr=128, name=name)


def rwkv_prep_bwd(pk, pl_, prm, e, et, cts, name):
    n, d = pk.shape
    small = [prm[k] for k in ("w0", "a0", "k_k", "k_a", "w_up", "a_up", "g_up")]

    def fn(k, xl, w0, a0, k_k, k_a, w_up, a_up, g_up, ev, etv, dw, dkm1, dkm2, dkn, db, dg):
        _, vjp = jax.vjp(lambda *a: _prep(*a, ev, etv), k, *_lora_parts(xl), w0, a0, k_k, k_a, w_up, a_up, g_up)
        dk, dxw, dxa, dxg, *dsmall = vjp((dw, dkm1 + dkm2, dkn, db, dg))
        return (dk, jnp.concatenate([dxw, dxa, dxg], axis=1), *dsmall)

    ins = [(pk, "r"), (pl_, "r")] + [(s, "f") for s in small] + [(e, "f"), (et, "f")] + [(c, "r") for c in cts]
    outs = [("r", d, F32), ("r", pl_.shape[1], F32)] + [("acc", s.shape) for s in small]
    return tilek(fn, ins, outs, n_rows=n, tr=64, name=name)


def _post(y, r, km, v, g, pga, pgb, yb, gn_w, gn_b, r_k, e, et):
    inv = 1.0 / RWKV_HEAD
    yc = y - segsum(y, e, et) * inv
    var = segsum(yc * yc, e, et) * inv
    yn = yc * lax.rsqrt(var + GN_EPS) * gn_w + gn_b
    bonus = segsum(r * km * r_k, e, et) * v
    ya = (yn + bonus) * g
    return _sigmoid(pga) * ya + _sigmoid(pgb) * yb


def rwkv_post_fwd(acts, prm, e, et, name):
    n, d = acts[0].shape
    small = [prm[k] for k in ("gn_w", "gn_b", "r_k")]
    ins = [(a, "r") for a in acts] + [(s, "f") for s in small] + [(e, "f"), (et, "f")]
    return tilek(lambda *a: (_post(*a),), ins, [("r", d, MMD)], n_rows=n, tr=128, name=name)[0]


def rwkv_post_bwd(acts, prm, e, et, dm, name):
    n, d = acts[0].shape
    small = [prm[k] for k in ("gn_w", "gn_b", "r_k")]
    na = len(acts)

    def fn(*a):
        prim, ev, etv, dmv = a[:na + 3], a[na + 3], a[na + 4], a[na + 5]
        _, vjp = jax.vjp(lambda *z: _post(*z, ev, etv), *prim)
        return vjp(dmv.astype(F32))

    ins = [(x, "r") for x in acts] + [(s, "f") for s in small] + [(e, "f"), (et, "f"), (dm, "r")]
    outs = [("r", d, F32)] * na + [("acc", s.shape) for s in small]
    return tilek(fn, ins, outs, n_rows=n, tr=64, name=name)


def _head_sums(x, first_head):
    a = jnp.sum(jnp.where(first_head, x, 0.0), axis=1, keepdims=True)
    b = jnp.sum(jnp.where(first_head, 0.0, x), axis=1, keepdims=True)
    return jnp.where(first_head, a, b)


def _round1(x):
    return (x.astype(BF16), None) if MMD == BF16 else _split2(x)


def _split2(x):
    hi = x.astype(BF16)
    return hi, (x - hi.astype(F32)).astype(BF16)


def _spread(row, eye2):
    hi, lo = _split2(row)
    return eye2 * hi, eye2 * lo


def _ones_dot(tiles, ones_blk):
    dims = (((1,), (0,)), ((), ()))
    res = lax.dot_general(jnp.concatenate([t[0] for t in tiles], axis=0), ones_blk, dims, preferred_element_type=F32)
    out = [res[i * RWKV_HEAD:(i + 1) * RWKV_HEAD] for i in range(len(tiles))]
    two_term = [i for i, t in enumerate(tiles) if t[1] is not None]
    if two_term:
        low = lax.dot_general(jnp.concatenate([tiles[i][1] for i in two_term], axis=0), ones_blk, dims,
                              preferred_element_type=F32)
        for n, i in enumerate(two_term):
            out[i] = out[i] + low[n * RWKV_HEAD:(n + 1) * RWKV_HEAD]
    return out


def _scan_consts():
    lane = lax.broadcasted_iota(jnp.int32, (1, LANES), 1)
    rows = lax.broadcasted_iota(jnp.int32, (RWKV_HEAD, LANES), 0)
    cols = lax.broadcasted_iota(jnp.int32, (RWKV_HEAD, LANES), 1)
    eye2 = ((cols & (RWKV_HEAD - 1)) == rows).astype(BF16)
    r2 = lax.broadcasted_iota(jnp.int32, (LANES, LANES), 0)
    c2 = lax.broadcasted_iota(jnp.int32, (LANES, LANES), 1)
    ones_blk = ((r2 // RWKV_HEAD) == (c2 // RWKV_HEAD)).astype(BF16)
    return lane, lane < RWKV_HEAD, eye2, ones_blk


def scan_forward(r, w, k, kn, b, v, bl, t, d, name, pg, hch):
    npair, nst = d // LANES, t // hch

    def body(r_ref, w_ref, k_ref, kn_ref, b_ref, v_ref, y_ref, hist_ref, s_ref, vb_ref):
        _, first_head, eye2, ones_blk = _scan_consts()
        eye2f = eye2.astype(F32)
        diag = lambda tile: jnp.sum(tile * eye2f, axis=0, keepdims=True)

        @pl.when(pl.program_id(2) == 0)
        def _():
            s_ref[...] = jnp.zeros_like(s_ref)

        pair_cols = [slice(p * LANES, (p + 1) * LANES) for p in range(pg)]
        for p, tile in enumerate(_ones_dot([_spread(v_ref[0, :, cols], eye2) for cols in pair_cols], ones_blk)):
            vb_ref[p] = tile

        def step(ts, carry):
            prev, nxt = jnp.maximum(ts - 1, 0), jnp.minimum(ts + 1, hch - 1)
            states, tiles = [], []
            for p in range(pg):
                cols = slice(p * LANES, (p + 1) * LANES)
                s = s_ref[p]
                hist_ref[0, p, pl.ds(ts, 1)] = s[None]
                states.append(s)
                tiles.append(_round1(s * r_ref[prev, :, cols]))
                tiles.append(_spread(v_ref[nxt, :, cols], eye2))
            res = _ones_dot(tiles, ones_blk)
            for p in range(pg):
                cols = slice(p * LANES, (p + 1) * LANES)
                s = states[p]
                sa = _head_sums(s * kn_ref[ts, :, cols], first_head)
                s_ref[p] = s * w_ref[ts, :, cols] + sa * b_ref[ts, :, cols] + vb_ref[p] * k_ref[ts, :, cols]
            for p in range(pg):
                cols = slice(p * LANES, (p + 1) * LANES)
                y_ref[prev, :, cols] = diag(res[2 * p])
                vb_ref[p] = res[2 * p + 1]
            return carry

        lax.fori_loop(0, hch, step, 0)
        last = _ones_dot([_round1(s_ref[p] * r_ref[hch - 1, :, cols]) for p, cols in enumerate(pair_cols)], ones_blk)
        for p, cols in enumerate(pair_cols):
            y_ref[hch - 1, :, cols] = diag(last[p])

    row_spec = pl.BlockSpec((hch, 1, pg * LANES), lambda bb, g, c: (bb * nst + c, 0, g))
    hist_spec = pl.BlockSpec((1, pg, hch, RWKV_HEAD, LANES), lambda bb, g, c: (bb, g, c, 0, 0))
    rows3 = [a.reshape(bl * t, 1, d) for a in (r, w, k, kn, b, v)]
    y, hist = pl.pallas_call(
        body, grid=(bl, npair // pg, nst), in_specs=[row_spec] * 6, out_specs=[row_spec, hist_spec],
        out_shape=[jax.ShapeDtypeStruct((bl * t, 1, d), F32), jax.ShapeDtypeStruct((bl, npair, t, RWKV_HEAD, LANES), F32)],
        scratch_shapes=[pltpu.VMEM((pg, RWKV_HEAD, LANES), F32)] * 2,
        compiler_params=_params(("parallel", "parallel", "arbitrary")), name=name)(*rows3)
    return y.reshape(bl * t, d), hist


def scan_backward(r, w, k, kn, b, v, dy, hist, bl, t, d, name, pg, hch):
    npair, nst = d // LANES, t // hch

    def body(r_ref, w_ref, k_ref, kn_ref, b_ref, v_ref, dy_ref, hist_ref,
             dr_ref, dw_ref, dk_ref, dkn_ref, db_ref, dv_ref, ds_ref, cur_ref):
        _, first_head, eye2, ones_blk = _scan_consts()
        eye2f = eye2.astype(F32)
        colsum = lambda x: jnp.sum(x, axis=0, keepdims=True)

        @pl.when(pl.program_id(2) == 0)
        def _():
            ds_ref[...] = jnp.zeros_like(ds_ref)

        tiles = []
        for p in range(pg):
            cols = slice(p * LANES, (p + 1) * LANES)
            tiles += [_spread(v_ref[hch - 1, :, cols], eye2), _spread(dy_ref[hch - 1, :, cols], eye2),
                      _split2(hist_ref[0, p, hch - 1] * kn_ref[hch - 1, :, cols])]
        first = _ones_dot(tiles, ones_blk)
        for p in range(pg):
            cols = slice(p * LANES, (p + 1) * LANES)
            row = lambda ref: ref[hch - 1, :, cols]
            s_prev = hist_ref[0, p, hch - 1]
            vb, dyb, sa = first[3 * p], first[3 * p + 1], first[3 * p + 2]
            cur_ref[0, p], cur_ref[1, p] = vb, sa
            dr_ref[hch - 1, :, cols] = colsum((s_prev * row(w_ref) + sa * row(b_ref) + vb * row(k_ref)) * dyb)
            ds_ref[p] += dyb * row(r_ref)

        def step(it, carry):
            ts = hch - 1 - it
            prev = jnp.maximum(ts - 1, 0)
            has_prev = ts > 0
            grads, tiles = [], []
            for p in range(pg):
                cols = slice(p * LANES, (p + 1) * LANES)
                ds = ds_ref[p]
                grads.append(ds)
                tiles.append(_spread(v_ref[prev, :, cols], eye2))
                tiles.append(_spread(dy_ref[prev, :, cols], eye2))
                tiles.append(_split2(hist_ref[0, p, pl.ds(prev, 1)][0] * kn_ref[prev, :, cols]))
                tiles.append(_round1(ds * k_ref[ts, :, cols]))
            res = _ones_dot(tiles, ones_blk)
            for p in range(pg):
                cols = slice(p * LANES, (p + 1) * LANES)
                row = lambda ref: ref[ts, :, cols]
                ds = grads[p]
                w_, kn_, b_ = row(w_ref), row(kn_ref), row(b_ref)
                dsa = _head_sums(ds * b_, first_head)
                s_prev = hist_ref[0, p, pl.ds(ts, 1)][0]
                vb, sa, dyb_prev = cur_ref[0, p], cur_ref[1, p], res[4 * p + 1]
                dk_ref[ts, :, cols] = colsum(ds * vb)
                db_ref[ts, :, cols] = colsum(ds * sa)
                dw_ref[ts, :, cols] = colsum(ds * s_prev)
                dkn_ref[ts, :, cols] = colsum(s_prev * dsa)
                dv_ref[ts, :, cols] = colsum(res[4 * p + 3] * eye2f)
                dr_ref[prev, :, cols] = jnp.where(has_prev, colsum(s_prev * dyb_prev), dr_ref[prev, :, cols])
                ds_ref[p] = ds * w_ + dsa * kn_ + jnp.where(has_prev, dyb_prev, 0.0) * r_ref[prev, :, cols]
            for p in range(pg):
                cur_ref[0, p] = res[4 * p]
                cur_ref[1, p] = res[4 * p + 2]
            return carry

        lax.fori_loop(0, hch, step, 0)

    row_spec = pl.BlockSpec((hch, 1, pg * LANES), lambda bb, g, c: (bb * nst + nst - 1 - c, 0, g))
    hist_spec = pl.BlockSpec((1, pg, hch, RWKV_HEAD, LANES), lambda bb, g, c: (bb, g, nst - 1 - c, 0, 0))
    row_shape = jax.ShapeDtypeStruct((bl * t, 1, d), F32)
    rows3 = [a.reshape(bl * t, 1, d) for a in (r, w, k, kn, b, v, dy)]
    outs = pl.pallas_call(
        body, grid=(bl, npair // pg, nst), in_specs=[row_spec] * 7 + [hist_spec], out_specs=[row_spec] * 6,
        out_shape=[row_shape] * 6,
        scratch_shapes=[pltpu.VMEM((pg, RWKV_HEAD, LANES), F32), pltpu.VMEM((2, pg, RWKV_HEAD, LANES), F32)],
        compiler_params=_params(("parallel", "parallel", "arbitrary")), name=name)(*rows3, hist)
    return [o.reshape(bl * t, d) for o in outs]


def _mla_norms(pm, gq, gkv):
    ql = gq.shape[1]
    kvl = gkv.shape[1]
    return _rms(pm[:, :ql], gq), _rms(pm[:, ql:ql + kvl], gkv)


def mla_prep_fwd(pm, gq, gkv, name):
    n = pm.shape[0]
    return tilek(_mla_norms, [(pm, "r"), (gq, "f"), (gkv, "f")],
                 [("r", gq.shape[1], MMD), ("r", gkv.shape[1], MMD)], n_rows=n, tr=256, name=name)


def mla_prep_bwd(pm, gq, gkv, dcq, dckv, dkpe, name):
    n, wm = pm.shape
    ql, kvl = gq.shape[1], gkv.shape[1]

    def fn(pmv, gqv, gkvv, d1, d2, d3):
        _, vjp1 = jax.vjp(_rms, pmv[:, :ql], gqv)
        _, vjp2 = jax.vjp(_rms, pmv[:, ql:ql + kvl], gkvv)
        dcq_in, dgq = vjp1(d1)
        dckv_in, dgkv = vjp2(d2)
        return jnp.concatenate([dcq_in, dckv_in, d3], axis=1), dgq, dgkv

    return tilek(fn, [(pm, "r"), (gq, "f"), (gkv, "f"), (dcq, "r"), (dckv, "r"), (dkpe, "r")],
                 [("r", wm, F32), ("acc", gq.shape), ("acc", gkv.shape)], n_rows=n, tr=128, name=name)


def _rope(x, c, s, first):
    sw = jnp.where(first, pltpu.roll(x, LANES - ROPE_DIM // 2, 1), pltpu.roll(x, ROPE_DIM // 2, 1))
    return x * c + sw * s


def _unrope(d, c, s, first):
    z = d * s
    sw = jnp.where(first, pltpu.roll(z, LANES - ROPE_DIM // 2, 1), pltpu.roll(z, ROPE_DIM // 2, 1))
    return d * c + sw


def _causal_segments(n_tiles, parts=4):
    bounds = sorted({round(n_tiles * s / parts) for s in range(parts + 1)})
    return list(zip(bounds[:-1], bounds[1:]))


def attn_fwd(q, kv, pm, ct, st, bl, t, hm, name):
    n = q.shape[0]
    tq = LANES
    scale = QK_DIM ** -0.5
    kpe_blk = pm.shape[1] // LANES - 1

    def body(qn_ref, qpe_ref, kn_ref, v_ref, kpe_ref, ct_ref, st_ref, o_ref, lse_ref, kp_s, kn_s, v_s):
        h = pl.program_id(1)
        lane = lax.broadcasted_iota(jnp.int32, (1, LANES), 1)
        first = (lane & (ROPE_DIM - 1)) < ROPE_DIM // 2
        kp = _rope(kpe_ref[...], ct_ref[...], st_ref[...], first)
        kp_s[...] = jnp.where(h % 2 == 0, kp, pltpu.roll(kp, ROPE_DIM, 1)).astype(MMD)
        kn_s[...] = kn_ref[...].astype(MMD)
        v_s[...] = v_ref[...].astype(MMD)
        def segment(lo, hi):
            ext = hi * tq
            kpos = lax.broadcasted_iota(jnp.int32, (1, ext), 1)

            def qtile(i, carry):
                rows = pl.ds(pl.multiple_of(i * tq, tq), tq)
                q2 = _rope(qpe_ref[rows, :], ct_ref[rows, :], st_ref[rows, :], first)
                s = (_mm(qn_ref[rows, :], kn_s[:ext, :], ((1,), (1,))) + _mm(q2, kp_s[:ext, :], ((1,), (1,)))) * scale
                qpos = i * tq + lax.broadcasted_iota(jnp.int32, (tq, 1), 0)
                s = jnp.where(kpos <= qpos, s, -1e30)
                m = jnp.max(s, axis=1, keepdims=True)
                p = jnp.exp(s - m)
                l = jnp.sum(p, axis=1, keepdims=True)
                o_ref[rows, :] = _mm(p, v_s[:ext, :]) / l
                lse_ref[0, 0, rows, :] = m + jnp.log(l)
                return carry

            lax.fori_loop(lo, hi, qtile, 0)

        for lo, hi in _causal_segments(t // tq):
            segment(lo, hi)

    blk = lambda f: pl.BlockSpec((t, LANES), f)
    return pl.pallas_call(
        body, grid=(bl, hm),
        in_specs=[blk(lambda b, h: (b, h)), blk(lambda b, h: (b, hm + h // 2)), blk(lambda b, h: (b, h)),
                  blk(lambda b, h: (b, hm + h)), blk(lambda b, h: (b, kpe_blk)), blk(lambda b, h: (0, 0)), blk(lambda b, h: (0, 0))],
        out_specs=[blk(lambda b, h: (b, h)), pl.BlockSpec((1, 1, t, 1), lambda b, h: (b, h, 0, 0))],
        out_shape=[jax.ShapeDtypeStruct((n, hm * LANES), F32), jax.ShapeDtypeStruct((bl, hm, t, 1), F32)],
        scratch_shapes=[pltpu.VMEM((t, LANES), MMD)] * 3,
        compiler_params=_params(("parallel", "arbitrary")), name=name)(q, q, kv, kv, pm, ct, st)


def attn_bwd(q, kv, pm, o, do, lse, ct, st, bl, t, hm, name):
    n = q.shape[0]
    tq = LANES
    scale = QK_DIM ** -0.5
    kpe_blk = pm.shape[1] // LANES - 1

    def body(qn_ref, qpe_ref, kn_ref, v_ref, kpe_ref, o_ref, do_ref, lse_ref, ct_ref, st_ref,
             dqn_ref, dqpe_ref, dkn_ref, dv_ref, dkpe_ref, kp_s, kn_s, v_s, dkn_s, dkp_s, dv_s):
        h = pl.program_id(1)
        lane = lax.broadcasted_iota(jnp.int32, (1, LANES), 1)
        first = (lane & (ROPE_DIM - 1)) < ROPE_DIM // 2
        mine = (lane // ROPE_DIM) == (h % 2)
        kp = _rope(kpe_ref[...], ct_ref[...], st_ref[...], first)
        kp_s[...] = jnp.where(h % 2 == 0, kp, pltpu.roll(kp, ROPE_DIM, 1)).astype(MMD)
        kn_s[...] = kn_ref[...].astype(MMD)
        v_s[...] = v_ref[...].astype(MMD)
        dkn_s[...] = jnp.zeros_like(dkn_s)
        dkp_s[...] = jnp.zeros_like(dkp_s)
        dv_s[...] = jnp.zeros_like(dv_s)
        @pl.when(h % 2 == 0)
        def _():
            dqpe_ref[...] = jnp.zeros_like(dqpe_ref)

        @pl.when(h == 0)
        def _():
            dkpe_ref[...] = jnp.zeros_like(dkpe_ref)

        def segment(lo, hi):
            ext = hi * tq
            kpos = lax.broadcasted_iota(jnp.int32, (1, ext), 1)

            def qtile(i, carry):
                rows = pl.ds(pl.multiple_of(i * tq, tq), tq)
                c_i, s_i = ct_ref[rows, :], st_ref[rows, :]
                q1 = qn_ref[rows, :].astype(MMD)
                q2 = _rope(qpe_ref[rows, :], c_i, s_i, first).astype(MMD)
                s = (_mm(q1, kn_s[:ext, :], ((1,), (1,))) + _mm(q2, kp_s[:ext, :], ((1,), (1,)))) * scale
                qpos = i * tq + lax.broadcasted_iota(jnp.int32, (tq, 1), 0)
                p = jnp.where(kpos <= qpos, jnp.exp(s - lse_ref[0, 0, rows, :]), 0.0)
                do_i = do_ref[rows, :]
                delta = jnp.sum(do_i * o_ref[rows, :], axis=1, keepdims=True)
                dp = _mm(do_i, v_s[:ext, :], ((1,), (1,)))
                ds = (p * (dp - delta) * scale).astype(MMD)
                dqn_ref[rows, :] = _mm(ds, kn_s[:ext, :])
                dq2 = jnp.where(mine, _mm(ds, kp_s[:ext, :]), 0.0)
                dqpe_ref[rows, :] += _unrope(dq2, c_i, s_i, first)
                dkn_s[:ext, :] += _mm(ds, q1, ((0,), (0,)))
                dkp_s[:ext, :] += _mm(ds, q2, ((0,), (0,)))
                dv_s[:ext, :] += _mm(p, do_i, ((0,), (0,)))
                return carry

            lax.fori_loop(lo, hi, qtile, 0)

        for lo, hi in _causal_segments(t // tq):
            segment(lo, hi)
        dkn_ref[...] = dkn_s[...]
        dv_ref[...] = dv_s[...]
        dkp = jnp.where(mine, dkp_s[...], 0.0)
        dkp = jnp.where(h % 2 == 0, dkp, pltpu.roll(dkp, ROPE_DIM, 1))
        dkpe_ref[...] += _unrope(dkp, ct_ref[...], st_ref[...], first)

    blk = lambda f: pl.BlockSpec((t, LANES), f)
    hd = lambda b, h: (b, h)
    shp = lambda wd: jax.ShapeDtypeStruct((n, wd), F32)
    return pl.pallas_call(
        body, grid=(bl, hm),
        in_specs=[blk(hd), blk(lambda b, h: (b, hm + h // 2)), blk(hd), blk(lambda b, h: (b, hm + h)),
                  blk(lambda b, h: (b, kpe_blk)), blk(hd), blk(hd), pl.BlockSpec((1, 1, t, 1), lambda b, h: (b, h, 0, 0)),
                  blk(lambda b, h: (0, 0)), blk(lambda b, h: (0, 0))],
        out_specs=[blk(hd), blk(lambda b, h: (b, h // 2)), blk(hd), blk(hd), blk(lambda b, h: (b, 0))],
        out_shape=[shp(hm * LANES), shp(hm * ROPE_DIM), shp(hm * LANES), shp(hm * LANES), shp(LANES)],
        scratch_shapes=[pltpu.VMEM((t, LANES), MMD)] * 3 + [pltpu.VMEM((t, LANES), F32)] * 3,
        compiler_params=_params(("parallel", "arbitrary")), name=name)(q, q, kv, kv, pm, o, do, lse, ct, st)


def _peer(k):
    mx, my, mc = lax.axis_index("x"), lax.axis_index("y"), lax.axis_index("c")
    px = 1 - mx if k & 4 else mx
    py = 1 - my if k & 2 else my
    pc = 1 - mc if k & 1 else mc
    return (px, py, pc), 4 * px + 2 * py + pc


def all_gather(x, name):
    def body(x_ref, o_ref, send_sems, recv_sems, local_sem):
        _, me = _peer(0)
        local = pltpu.make_async_copy(x_ref, o_ref.at[me], local_sem)
        local.start()
        copies = []
        for k in range(1, N_DEV):
            dev, _ = _peer(k)
            cp = pltpu.make_async_remote_copy(src_ref=x_ref, dst_ref=o_ref.at[me], send_sem=send_sems.at[k - 1],
                                              recv_sem=recv_sems.at[k - 1], device_id=dev, device_id_type=MESH)
            cp.start()
            copies.append(cp)
        for cp in copies:
            cp.wait()
        local.wait()

    return pl.pallas_call(
        body, in_specs=[pl.BlockSpec(memory_space=pl.ANY)], out_specs=pl.BlockSpec(memory_space=pl.ANY),
        out_shape=jax.ShapeDtypeStruct((N_DEV,) + x.shape, x.dtype),
        scratch_shapes=[pltpu.SemaphoreType.DMA((N_DEV - 1,)), pltpu.SemaphoreType.DMA((N_DEV - 1,)), pltpu.SemaphoreType.DMA],
        name=name)(x)


def all_to_all(x, name):
    def body(x_ref, o_ref, send_sems, recv_sems, local_sem):
        _, me = _peer(0)
        local = pltpu.make_async_copy(x_ref.at[me], o_ref.at[me], local_sem)
        local.start()
        copies = []
        for k in range(1, N_DEV):
            dev, idx = _peer(k)
            cp = pltpu.make_async_remote_copy(src_ref=x_ref.at[idx], dst_ref=o_ref.at[me], send_sem=send_sems.at[k - 1],
                                              recv_sem=recv_sems.at[k - 1], device_id=dev, device_id_type=MESH)
            cp.start()
            copies.append(cp)
        for cp in copies:
            cp.wait()
        local.wait()

    return pl.pallas_call(
        body, in_specs=[pl.BlockSpec(memory_space=pl.ANY)], out_specs=pl.BlockSpec(memory_space=pl.ANY),
        out_shape=jax.ShapeDtypeStruct(x.shape, x.dtype),
        scratch_shapes=[pltpu.SemaphoreType.DMA((N_DEV - 1,)), pltpu.SemaphoreType.DMA((N_DEV - 1,)), pltpu.SemaphoreType.DMA],
        name=name)(x)


def _chips():
    mx, my, mc = lax.axis_index("x"), lax.axis_index("y"), lax.axis_index("c")
    return (mx, my, mc), (mx, my, 1 - mc), [(1 - mx, my), (mx, 1 - my), (1 - mx, 1 - my)]


def all_gather_two_level(x, name):
    def body(x_ref, o_ref, send_sems, recv_sems, local_sem):
        me, sibling, chips = _chips()
        blk = lambda px, py, pc: o_ref.at[4 * px + 2 * py + pc]

        def copy(k, block, to, src=None):
            return pltpu.make_async_remote_copy(src_ref=blk(*block) if src is None else src, dst_ref=blk(*block),
                                                send_sem=send_sems.at[k], recv_sem=recv_sems.at[k], device_id=to,
                                                device_id_type=MESH)

        mine = pltpu.make_async_copy(x_ref, blk(*me), local_sem)
        mine.start()
        first = [copy(0, me, sibling, src=x_ref)] + [copy(1 + j, me, (*chip, me[2]), src=x_ref) for j, chip in enumerate(chips)]
        for cp in first:
            cp.start()
        passed = [copy(4 + j, (*chip, me[2]), sibling) for j, chip in enumerate(chips)]
        for j, chip in enumerate(chips):
            copy(1 + j, (*chip, me[2]), me).wait_recv()
            passed[j].start()
        copy(0, sibling, me).wait_recv()
        for j, chip in enumerate(chips):
            copy(4 + j, (*chip, 1 - me[2]), me).wait_recv()
        for cp in first + passed:
            cp.wait_send()
        mine.wait()

    return pl.pallas_call(
        body, in_specs=[pl.BlockSpec(memory_space=pl.ANY)], out_specs=pl.BlockSpec(memory_space=pl.ANY),
        out_shape=jax.ShapeDtypeStruct((N_DEV,) + x.shape, x.dtype),
        scratch_shapes=[pltpu.SemaphoreType.DMA((N_DEV - 1,)), pltpu.SemaphoreType.DMA((N_DEV - 1,)), pltpu.SemaphoreType.DMA],
        name=name)(x)


def exchange_sibling(x, name):
    def body(x_ref, o_ref, send_sems, recv_sems):
        me, sibling, _ = _chips()
        copies = []
        for q in range(N_DEV // 2):
            cp = pltpu.make_async_remote_copy(src_ref=x_ref.at[2 * q + 1 - me[2]], dst_ref=o_ref.at[q], send_sem=send_sems.at[q],
                                              recv_sem=recv_sems.at[q], device_id=sibling, device_id_type=MESH)
            cp.start()
            copies.append(cp)
        for cp in copies:
            cp.wait()

    return pl.pallas_call(
        body, in_specs=[pl.BlockSpec(memory_space=pl.ANY)], out_specs=pl.BlockSpec(memory_space=pl.ANY),
        out_shape=jax.ShapeDtypeStruct((N_DEV // 2,) + x.shape[1:], x.dtype),
        scratch_shapes=[pltpu.SemaphoreType.DMA((N_DEV // 2,)), pltpu.SemaphoreType.DMA((N_DEV // 2,))], name=name)(x)


def exchange_chips(x, name):
    def body(x_ref, o_ref, send_sems, recv_sems, local_sem):
        me, _, chips = _chips()
        here = 2 * me[0] + me[1]
        local = pltpu.make_async_copy(x_ref.at[here], o_ref.at[here], local_sem)
        local.start()
        copies = []
        for j, (px, py) in enumerate(chips):
            cp = pltpu.make_async_remote_copy(src_ref=x_ref.at[2 * px + py], dst_ref=o_ref.at[here], send_sem=send_sems.at[j],
                                              recv_sem=recv_sems.at[j], device_id=(px, py, me[2]), device_id_type=MESH)
            cp.start()
            copies.append(cp)
        for cp in copies:
            cp.wait()
        local.wait()

    return pl.pallas_call(
        body, in_specs=[pl.BlockSpec(memory_space=pl.ANY)], out_specs=pl.BlockSpec(memory_space=pl.ANY),
        out_shape=jax.ShapeDtypeStruct(x.shape, x.dtype),
        scratch_shapes=[pltpu.SemaphoreType.DMA((3,)), pltpu.SemaphoreType.DMA((3,)), pltpu.SemaphoreType.DMA], name=name)(x)


def add_blocks(a, b, name):
    q, r, c = a.shape
    tr = _tile(r, max(16, (2 << 20) // (c * a.dtype.itemsize)), 16)
    spec = pl.BlockSpec((1, tr, c), lambda i, j: (i, j, 0))

    def body(a_ref, b_ref, o_ref):
        o_ref[...] = (a_ref[...].astype(F32) + b_ref[...].astype(F32)).astype(o_ref.dtype)

    return pl.pallas_call(
        body, grid=(q, r // tr), in_specs=[spec, spec], out_specs=spec, out_shape=jax.ShapeDtypeStruct(a.shape, a.dtype),
        compiler_params=_params(("parallel", "parallel")), name=name)(a, b)


def reduce_scatter_two_level(x, tag):
    q = N_DEV // 2
    from_sibling = exchange_sibling(x, f"{tag}_sibling")
    mine = lax.dynamic_index_in_dim(x.reshape((q, 2) + x.shape[1:]), lax.axis_index("c"), axis=1, keepdims=False)
    chip_sums = add_blocks(mine, from_sibling, f"{tag}_pair_sum")
    return sum_blocks(exchange_chips(chip_sums, f"{tag}_chips"), f"{tag}_sum")


def sum_blocks(x, name):
    nb, r, c = x.shape
    tr = _tile(r, max(16, (4 << 20) // (nb * c * x.dtype.itemsize)), 16)

    def body(x_ref, o_ref):
        acc = x_ref[0].astype(F32)
        for i in range(1, nb):
            acc = acc + x_ref[i].astype(F32)
        o_ref[...] = acc

    return pl.pallas_call(
        body, grid=(r // tr,), in_specs=[pl.BlockSpec((nb, tr, c), lambda i: (0, i, 0))],
        out_specs=pl.BlockSpec((tr, c), lambda i: (i, 0)), out_shape=jax.ShapeDtypeStruct((r, c), F32),
        compiler_params=_params(("parallel",)), name=name)(x)


def _adamw(w, g, m, v):
    m = ADAM_B1 * m + (1.0 - ADAM_B1) * g
    v = ADAM_B2 * v + (1.0 - ADAM_B2) * jnp.square(g)
    m_hat = m / (1.0 - ADAM_B1 ** ADAM_STEP)
    v_hat = v / (1.0 - ADAM_B2 ** ADAM_STEP)
    delta = -ADAM_LR * (m_hat / (jnp.sqrt(v_hat) + ADAM_EPS) + ADAM_WD * w)
    return delta, m, v


def adamw(w, g, m, v, name):
    r, c = w.shape
    tr = _tile(r, 256, 8)
    spec = pl.BlockSpec((tr, c), lambda i: (i, 0))

    def body(w_ref, g_ref, m_ref, v_ref, d_ref, nm_ref, nv_ref):
        d_ref[...], nm_ref[...], nv_ref[...] = _adamw(w_ref[...], g_ref[...], m_ref[...], v_ref[...])

    return pl.pallas_call(
        body, grid=(r // tr,), in_specs=[spec] * 4, out_specs=[spec] * 3,
        out_shape=[jax.ShapeDtypeStruct((r, c), F32)] * 3, compiler_params=_params(("parallel",)), name=name)(w, g, m, v)


def batch_sum_rows(dh, bl, t, rows, name):
    d = dh.shape[1]

    def body(x_ref, o_ref):
        @pl.when(pl.program_id(0) == 0)
        def _():
            o_ref[...] = jnp.zeros_like(o_ref)

        o_ref[...] += x_ref[...]

    return pl.pallas_call(
        body, grid=(bl,), in_specs=[pl.BlockSpec((rows, d), lambda b: (b * (t // rows), 0))],
        out_specs=pl.BlockSpec((rows, d), lambda b: (0, 0)), out_shape=jax.ShapeDtypeStruct((rows, d), F32),
        compiler_params=_params(("arbitrary",)), name=name)(dh)


class Dims:
    def __init__(self, x, meta_full_cols, w_up, g_up, q_norm, kv_norm, d_ff):
        self.bl, self.seq, self.d = x.shape
        self.n_meta = 16
        self.t_real = self.n_meta + self.seq
        self.t = -(-self.t_real // LANES) * LANES
        self.n = self.bl * self.t
        self.f = d_ff
        self.wl, self.gl = w_up.shape[-2], g_up.shape[-2]
        self.ql, self.kvl = q_norm.shape[-1], kv_norm.shape[-1]
        self.hm = self.d // V_DIM
        self.in_cols = 5 * self.d + 2 * self.wl + self.gl + self.ql + self.kvl + ROPE_DIM


def _pad_cols(a, width):
    return jnp.pad(a, ((0, 0), (0, width - a.shape[1])))


def _pad_rows(a, rows):
    return jnp.pad(a, ((0, rows - a.shape[0]), (0, 0)))


def split_in(a, dm, axis=1):
    d, wl, gl, ql, kvl = dm.d, dm.wl, dm.gl, dm.ql, dm.kvl
    size = a.shape[axis]
    cut = lambda lo, hi: lax.slice_in_dim(a, min(lo, size), min(hi, size), axis=axis)

    def pad(p, width):
        cfg = [(0, 0)] * a.ndim
        cfg[axis] = (0, width - p.shape[axis])
        return jnp.pad(p, cfg)

    o = 3 * d
    lora = jnp.concatenate([pad(cut(o, o + wl), LANES), pad(cut(o + wl, o + 2 * wl), LANES),
                            cut(o + 2 * wl, o + 2 * wl + gl)], axis=axis)
    o += 2 * wl + gl
    mla = pad(cut(o, o + ql + kvl + ROPE_DIM), ql + kvl + LANES)
    o += ql + kvl + ROPE_DIM
    return dict(r=cut(0, d), k=cut(d, 2 * d), v=cut(2 * d, 3 * d), l=lora, m=mla, ga=cut(o, o + d), gb=cut(o + d, o + 2 * d))


def merge_in(g, dm, axis=1):
    wl, gl, ql, kvl = dm.wl, dm.gl, dm.ql, dm.kvl
    cut = lambda p, lo, hi: lax.slice_in_dim(p, lo, hi, axis=axis)
    l, m = g["l"], g["m"]
    return jnp.concatenate([g["r"], g["k"], g["v"], cut(l, 0, wl), cut(l, LANES, LANES + wl), cut(l, 2 * LANES, 2 * LANES + gl),
                            cut(m, 0, ql + kvl + ROPE_DIM), g["ga"], g["gb"]], axis=axis)


def split_uq(w, dm):
    w3 = w.reshape(w.shape[0], dm.hm, QK_DIM)
    return jnp.concatenate([w3[:, :, :NOPE_DIM].reshape(w.shape[0], -1), w3[:, :, NOPE_DIM:].reshape(w.shape[0], -1)], axis=1)


def merge_uq(gn, gp, dm):
    r = gn.shape[0]
    return jnp.concatenate([gn.reshape(r, dm.hm, NOPE_DIM), gp.reshape(r, dm.hm, ROPE_DIM)], axis=2).reshape(r, -1)


def split_ukv(w, dm):
    w3 = w.reshape(w.shape[0], dm.hm, NOPE_DIM + V_DIM)
    return jnp.concatenate([w3[:, :, :NOPE_DIM].reshape(w.shape[0], -1), w3[:, :, NOPE_DIM:].reshape(w.shape[0], -1)], axis=1)


def merge_ukv(gk, gv, dm):
    r = gk.shape[0]
    return jnp.concatenate([gk.reshape(r, dm.hm, NOPE_DIM), gv.reshape(r, dm.hm, V_DIM)], axis=2).reshape(r, -1)


def head_matrices(d):
    heads = d // RWKV_HEAD
    e = (np.arange(d)[:, None] // RWKV_HEAD == np.arange(LANES)[None, :]) & (np.arange(LANES)[None, :] < heads)
    return jnp.asarray(e, BF16), jnp.asarray(e.T, BF16)


def rope_tables(t):
    pos = jnp.arange(t, dtype=F32)
    inv_freq = 1.0 / (ROPE_THETA ** (jnp.arange(0, ROPE_DIM, 2, dtype=F32) / ROPE_DIM))
    ang = pos[:, None] * inv_freq[None, :]
    cos, sin = jnp.cos(ang), jnp.sin(ang)
    return jnp.tile(jnp.concatenate([cos, cos], axis=1), (1, 2)), jnp.tile(jnp.concatenate([-sin, sin], axis=1), (1, 2))


def local_step(dm, x, loss_target, meta, wt, sp):
    bl, t, n, d, hm = dm.bl, dm.t, dm.n, dm.d, dm.hm
    e, et = head_matrices(d)
    ct, st = rope_tables(t)
    padz = jnp.zeros((bl, t - dm.t_real, d), F32)
    h0 = jnp.concatenate([jnp.broadcast_to(meta[None], (bl, dm.n_meta, d)), x, padz], axis=1).reshape(n, d)
    tgt = jnp.concatenate([jnp.zeros((bl, dm.n_meta, d), F32), loss_target, padz], axis=1).reshape(n, d)
    tpos = jnp.arange(t)
    mask = jnp.tile(((tpos >= dm.n_meta) & (tpos < dm.t_real)).astype(F32), bl).reshape(n, 1)

    win = split_in(wt["w_in"], dm, axis=0)
    mu = split_in(sp["tm_mu"], dm)
    wq, wkv = split_uq(wt["w_uq"], dm), split_ukv(wt["w_ukv"], dm)
    prm = dict(w0=sp["w0"], a0=sp["a0"], k_k=sp["k_k"], k_a=sp["k_a"], gn_w=sp["gn_w"], gn_b=sp["gn_b"], r_k=sp["r_k"],
               w_up=_pad_rows(wt["w_up"], LANES).astype(F32), a_up=_pad_rows(wt["a_up"], LANES).astype(F32),
               g_up=wt["g_up"].astype(F32))

    h1, ffn1 = ffn_forward(h0, sp["ffn1_norm"], wt["ffn1_w_gate"], wt["ffn1_w_up"], wt["ffn1_w_down"], "ffn1")
    u = rms_fwd(h1, sp["mix_norm"], "mix_rms")
    proj = {key: matmul([(u, win[key])], "nt", name=f"proj_{key}") for key in win}
    sh = {key: lerp_fwd(proj[key], mu[key], bl, t, f"shift_{key}") for key in ("r", "k", "v", "l")}
    decay, kmod, kneg, bvec, gate = rwkv_prep_fwd(sh["k"], sh["l"], prm, e, et, "rwkv_prep")
    pairs = min(SCAN_PAIRS, d // LANES)
    y, hist = scan_forward(sh["r"], decay, kmod, kneg, bvec, sh["v"], bl, t, d, "wkv_scan", pairs, SCAN_FWD_STEPS)
    cqn, ckvn = mla_prep_fwd(proj["m"], sp["q_norm"], sp["kv_norm"], "mla_norms")
    q = matmul([(cqn, wq)], "nn", name="mla_q")
    kv = matmul([(ckvn, wkv)], "nn", name="mla_kv")
    o, lse = attn_fwd(q, kv, proj["m"], ct, st, bl, t, hm, "mla_attn")
    post_in = [y, sh["r"], kmod, sh["v"], gate, proj["ga"], proj["gb"], o]
    mix = rwkv_post_fwd(post_in, prm, e, et, "mix_gate")
    h2 = matmul([(mix, wt["w_out"])], "nn", res=h1, name="out_proj")
    h3, ffn2 = ffn_forward(h2, sp["ffn2_norm"], wt["ffn2_w_gate"], wt["ffn2_w_up"], wt["ffn2_w_down"], "ffn2")
    dh3, d_final, loss = loss_head(h3, tgt, mask, sp["final_norm"], "loss_head")

    gw, gs = {}, {"final_norm": d_final}
    dh2, gs["ffn2_norm"], gw["ffn2_w_gate"], gw["ffn2_w_up"], gw["ffn2_w_down"] = ffn_backward(
        dh3, h2, sp["ffn2_norm"], wt["ffn2_w_gate"], wt["ffn2_w_up"], wt["ffn2_w_down"], ffn2, "ffn2")
    dmix = matmul([(dh2, wt["w_out"])], "nt", name="out_proj_dx")
    gw["w_out"] = matmul([(mix, dh2)], "tn", name="out_proj_dw")
    (dy, dr_p, dkm_p, dv_p, dgate, dpga, dpgb, do, gs["gn_w"], gs["gn_b"], gs["r_k"]) = rwkv_post_bwd(
        post_in, prm, e, et, dmix, "mix_gate_bwd")
    dqn, dqpe, dkn, dv_att, dkpe = attn_bwd(q, kv, proj["m"], o, do, lse, ct, st, bl, t, hm, "mla_attn_bwd")
    nq = hm * NOPE_DIM
    dcqn = matmul([(dqn, wq[:, :nq])], "nt", name="mla_q_dx1")
    dcqn = matmul([(dqpe, wq[:, nq:])], "nt", res=dcqn, name="mla_q_dx2")
    gw["w_uq"] = merge_uq(matmul([(cqn, dqn)], "tn", name="mla_q_dw1"), matmul([(cqn, dqpe)], "tn", name="mla_q_dw2"), dm)
    dckvn = matmul([(dkn, wkv[:, :nq]), (dv_att, wkv[:, nq:])], "nt", name="mla_kv_dx", tk=1024)
    gw["w_ukv"] = merge_ukv(matmul([(ckvn, dkn)], "tn", name="mla_kv_dw1"), matmul([(ckvn, dv_att)], "tn", name="mla_kv_dw2"), dm)
    dproj = {"ga": dpga, "gb": dpgb}
    dproj["m"], gs["q_norm"], gs["kv_norm"] = mla_prep_bwd(proj["m"], sp["q_norm"], sp["kv_norm"], dcqn, dckvn, dkpe, "mla_norms_bwd")
    dr_s, ddecay, dk_s, dkneg, dbvec, dv_s = scan_backward(sh["r"], decay, kmod, kneg, bvec, sh["v"], dy, hist,
                                                           bl, t, d, "wkv_scan_bwd", pairs, SCAN_BWD_STEPS)
    (dsh_k, dsh_l, gs["w0"], gs["a0"], gs["k_k"], gs["k_a"], g_wup, g_aup, gw["g_up"]) = rwkv_prep_bwd(
        sh["k"], sh["l"], prm, e, et, [ddecay, dk_s, dkm_p, dkneg, dbvec, dgate], "rwkv_prep_bwd")
    gw["w_up"], gw["a_up"] = g_wup[:dm.wl], g_aup[:dm.wl]
    dmu = {}
    for key, cts in (("r", [dr_s, dr_p]), ("k", [dsh_k]), ("v", [dv_s, dv_p]), ("l", [dsh_l])):
        dproj[key], dmu[key] = lerp_bwd(proj[key], mu[key], cts, bl, t, f"shift_{key}_bwd")
    zero_m = jnp.zeros((1, proj["m"].shape[1]), F32)
    gs["tm_mu"] = merge_in(dict(dmu, m=zero_m, ga=zero_m[:, :0], gb=zero_m[:, :0]), dm)[:, :3 * d + 2 * dm.wl + dm.gl]
    wide = ("r", "k", "v", "ga", "gb")
    du = matmul([(dproj[key], win[key]) for key in wide], "nn", name="proj_dx", tn=512, tk=512)
    du = matmul([(dproj["l"], win["l"])], "nn", res=du, name="proj_dx_l")
    du = matmul([(dproj["m"], win["m"])], "nn", res=du, name="proj_dx_m")
    gw["w_in"] = merge_in({key: matmul([(dproj[key], u)], "tn", name=f"proj_dw_{key}") for key in win}, dm, axis=0)
    dh1, gs["mix_norm"] = rms_bwd(h1, sp["mix_norm"], du, dh2, "mix_rms_bwd")
    dh0, gs["ffn1_norm"], gw["ffn1_w_gate"], gw["ffn1_w_up"], gw["ffn1_w_down"] = ffn_backward(
        dh1, h0, sp["ffn1_norm"], wt["ffn1_w_gate"], wt["ffn1_w_up"], wt["ffn1_w_down"], ffn1, "ffn1")
    grad_x = dh0.reshape(bl, t, d)[:, dm.n_meta:dm.t_real]
    dmeta = batch_sum_rows(dh0, bl, t, dm.n_meta, "meta_grad")
    return loss, grad_x, dmeta, gw, gs


COL_SHARDED = ("ffn1_w_gate", "ffn1_w_up", "w_in", "w_up", "a_up", "g_up", "w_uq", "w_ukv", "ffn2_w_gate", "ffn2_w_up")
ROW_SHARDED = ("ffn1_w_down", "w_out", "ffn2_w_down")
TRANSPOSED = ("ffn1_w_gate", "ffn1_w_up", "w_in", "ffn2_w_gate", "ffn2_w_up")
WIDE = ("ffn1_w_gate", "ffn1_w_up", "ffn1_w_down", "w_in", "w_out", "ffn2_w_gate", "ffn2_w_up", "ffn2_w_down")
NARROW = ("w_up", "a_up", "g_up", "w_uq", "w_ukv")
MATRICES = ("ffn1_w_gate", "ffn1_w_up", "ffn1_w_down", "w_in", "w_up", "a_up", "g_up", "w_uq", "w_ukv", "w_out",
            "ffn2_w_gate", "ffn2_w_up", "ffn2_w_down")
SMALL = ("ffn1_norm", "mix_norm", "tm_mu", "w0", "a0", "k_k", "k_a", "r_k", "gn_w", "gn_b", "q_norm", "kv_norm",
         "ffn2_norm", "final_norm")
WEIGHTS = ("meta_tokens", "ffn1_norm", "ffn1_w_gate", "ffn1_w_up", "ffn1_w_down", "mix_norm", "w_in", "tm_mu", "w0", "w_up",
           "a0", "a_up", "g_up", "k_k", "k_a", "r_k", "gn_w", "gn_b", "q_norm", "w_uq", "kv_norm", "w_ukv", "w_out",
           "ffn2_norm", "ffn2_w_gate", "ffn2_w_up", "ffn2_w_down", "final_norm")
PACK_COLS = 1024
PACK_ALIGN = 16 * PACK_COLS


def _pack(parts):
    offs, o = [], 0
    for p in parts:
        offs.append(o)
        o += p.shape[1]
    total = -(-o // PACK_ALIGN) * PACK_ALIGN
    flat = jnp.concatenate(list(parts) + [jnp.zeros((parts[0].shape[0], total - o), parts[0].dtype)], axis=1)
    return flat.reshape(parts[0].shape[0], total // PACK_COLS, PACK_COLS), offs


def kernel(x, meta_tokens, ffn1_norm, ffn1_w_gate, ffn1_w_up, ffn1_w_down, mix_norm, w_in, tm_mu, w0, w_up, a0, a_up, g_up, k_k, k_a, r_k, gn_w, gn_b, q_norm, w_uq, kv_norm, w_ukv, w_out, ffn2_norm, ffn2_w_gate, ffn2_w_up, ffn2_w_down, final_norm, loss_target, m_meta_tokens, m_ffn1_norm, m_ffn1_w_gate, m_ffn1_w_up, m_ffn1_w_down, m_mix_norm, m_w_in, m_tm_mu, m_w0, m_w_up, m_a0, m_a_up, m_g_up, m_k_k, m_k_a, m_r_k, m_gn_w, m_gn_b, m_q_norm, m_w_uq, m_kv_norm, m_w_ukv, m_w_out, m_ffn2_norm, m_ffn2_w_gate, m_ffn2_w_up, m_ffn2_w_down, m_final_norm, v_meta_tokens, v_ffn1_norm, v_ffn1_w_gate, v_ffn1_w_up, v_ffn1_w_down, v_mix_norm, v_w_in, v_tm_mu, v_w0, v_w_up, v_a0, v_a_up, v_g_up, v_k_k, v_k_a, v_r_k, v_gn_w, v_gn_b, v_q_norm, v_w_uq, v_kv_norm, v_w_ukv, v_w_out, v_ffn2_norm, v_ffn2_w_gate, v_ffn2_w_up, v_ffn2_w_down, v_final_norm):
    args = dict(locals())
    wts = {k: args[k] for k in WEIGHTS}
    ms = {k: args["m_" + k] for k in WEIGHTS}
    vs = {k: args["v_" + k] for k in WEIGHTS}
    dm = Dims(x, None, w_up, g_up, q_norm, kv_norm, ffn1_w_down.shape[1] * N_DEV)

    shard2d = {k: wts[k].reshape(wts[k].shape[-2], wts[k].shape[-1]) for k in MATRICES}
    sent = {k: shard2d[k].T if k in TRANSPOSED else shard2d[k] for k in MATRICES}
    wide_rows = np.cumsum([0] + [sent[k].shape[0] for k in WIDE])
    got_wide = all_gather_two_level(jnp.concatenate([sent[k].astype(MMD) for k in WIDE], axis=0), "gather_wide")
    full = {k: got_wide[:, lo:hi].reshape(-1, dm.d) for k, lo, hi in zip(WIDE, wide_rows[:-1], wide_rows[1:])}
    send, offs = _pack([sent[k].astype(MMD).reshape(1, -1) for k in NARROW])
    got = all_gather(send[0], "gather_narrow").reshape(N_DEV, -1)
    for k, o in zip(NARROW, offs):
        r, c = sent[k].shape
        full[k] = got[:, o:o + r * c].reshape(N_DEV, r, c).transpose(1, 0, 2).reshape(r, N_DEV * c)
    mr, mc = meta_tokens.shape
    meta = all_gather(meta_tokens, "gather_meta").transpose(1, 0, 2).reshape(mr, N_DEV * mc)
    small = {k: wts[k].reshape(1, -1) for k in SMALL}

    loss, grad_x, dmeta, gw, gs = local_step(dm, x, loss_target, meta, full, small)

    gwide = jnp.concatenate([gw[k].reshape(N_DEV, sent[k].shape[0], dm.d) for k in WIDE], axis=1).astype(MMD)
    gsum_wide = reduce_scatter_two_level(gwide, "scatter_wide")
    grads = {}
    for k, lo, hi in zip(WIDE, wide_rows[:-1], wide_rows[1:]):
        grads[k] = gsum_wide[lo:hi].T if k in TRANSPOSED else gsum_wide[lo:hi]

    def blocks(k, g):
        r, c = sent[k].shape
        return g.reshape(r, N_DEV, c).transpose(1, 0, 2).reshape(N_DEV, r * c)

    gsend, goffs = _pack([blocks(k, gw[k]).astype(MMD) for k in NARROW]
                         + [dmeta.reshape(mr, N_DEV, mc).transpose(1, 0, 2).reshape(N_DEV, mr * mc).astype(MMD)])
    gsum = sum_blocks(all_to_all(gsend, "scatter_narrow"), "sum_narrow").reshape(-1)
    for k, o in zip(NARROW, goffs):
        r, c = sent[k].shape
        grads[k] = gsum[o:o + r * c].reshape(r, c)
    grads["meta_tokens"] = gsum[goffs[-1]:goffs[-1] + mr * mc].reshape(mr, mc)

    ssend, soffs = _pack([gs[k].reshape(1, -1) for k in SMALL] + [loss])
    ssum = sum_blocks(all_gather(ssend[0], "gather_small"), "sum_small").reshape(-1)
    for k, o in zip(SMALL, soffs):
        grads[k] = ssum[o:o + small[k].shape[1]]
    loss_total = ssum[soffs[-1]]

    delta, new_m, new_v = {}, {}, {}
    for k in MATRICES + ("meta_tokens",):
        shp = wts[k].shape
        to2d = lambda a: a.reshape(shp[-2], shp[-1])
        dlt, nm, nv = adamw(to2d(wts[k]), grads[k], to2d(ms[k]), to2d(vs[k]), f"adamw_{k}")
        delta[k], new_m[k], new_v[k] = dlt.reshape(shp), nm.reshape(shp), nv.reshape(shp)
        grads[k] = grads[k].reshape(shp)
    pw, _ = _pack([wts[k].reshape(1, -1) for k in SMALL])
    pm_, _ = _pack([ms[k].reshape(1, -1) for k in SMALL])
    pv, _ = _pack([vs[k].reshape(1, -1) for k in SMALL])
    pg, poffs = _pack([grads[k].reshape(1, -1) for k in SMALL])
    dlt, nm, nv = adamw(pw[0], pg[0], pm_[0], pv[0], "adamw_small")
    for k, o in zip(SMALL, poffs):
        shp, sz = wts[k].shape, small[k].shape[1]
        cut = lambda a: a.reshape(-1)[o:o + sz].reshape(shp)
        delta[k], new_m[k], new_v[k] = cut(dlt), cut(nm), cut(nv)
        grads[k] = grads[k].reshape(shp)

    return (loss_total, grad_x, *[grads[k] for k in WEIGHTS], *[delta[k] for k in WEIGHTS],
            *[new_m[k] for k in WEIGHTS], *[new_v[k] for k in WEIGHTS])
```

```python
import functools

import numpy as np
import jax
import jax.numpy as jnp
from jax import lax
from jax.experimental import pallas as pl
from jax.experimental.pallas import tpu as pltpu

F32 = jnp.float32
BF16 = jnp.bfloat16
MMD = BF16

NORM_EPS = 1e-6
RWKV_HEAD = 64
GN_EPS = RWKV_HEAD * 1e-5
NOPE_DIM = 128
ROPE_DIM = 64
V_DIM = 128
QK_DIM = NOPE_DIM + ROPE_DIM
ROPE_THETA = 10000.0
ADAM_LR = 0.001
ADAM_B1 = 0.9
ADAM_B2 = 0.999
ADAM_EPS = 1e-08
ADAM_WD = 0.01
ADAM_STEP = 10

LANES = 128
SCAN_PAIRS = 8
SCAN_FWD_STEPS = 32
SCAN_BWD_STEPS = 16
N_DEV = 8
VMEM_LIMIT = 56 * 1024 * 1024
MESH = pl.DeviceIdType.MESH


def _tile(n, target, align):
    best = None
    for d in range(align, min(n, target) + 1, align):
        if n % d == 0:
            best = d
    return best if best is not None else n


def _params(sem=None):
    return pltpu.CompilerParams(dimension_semantics=sem, vmem_limit_bytes=VMEM_LIMIT)


def _mm(a, b, dims=((1,), (0,))):
    return lax.dot_general(a.astype(MMD), b.astype(MMD), (dims, ((), ())), preferred_element_type=F32)


@jax.custom_vjp
def mmdot(a, b):
    return _mm(a, b)


def _mmdot_fwd(a, b):
    return _mm(a, b), (a, b)


def _mmdot_bwd(res, g):
    a, b = res
    return _mm(g, b, ((1,), (1,))).astype(a.dtype), _mm(a, g, ((0,), (0,))).astype(b.dtype)


mmdot.defvjp(_mmdot_fwd, _mmdot_bwd)


def _dot2(x, m):
    hi = x.astype(BF16)
    lo = (x - hi.astype(F32)).astype(BF16)
    return (lax.dot_general(hi, m, (((1,), (0,)), ((), ())), preferred_element_type=F32)
            + lax.dot_general(lo, m, (((1,), (0,)), ((), ())), preferred_element_type=F32))


@jax.custom_vjp
def segsum(x, e, et):
    return _dot2(_dot2(x, e), et)


def _segsum_fwd(x, e, et):
    return segsum(x, e, et), (e, et)


def _segsum_bwd(res, g):
    e, et = res
    return segsum(g, e, et), jnp.zeros_like(e), jnp.zeros_like(et)


segsum.defvjp(_segsum_fwd, _segsum_bwd)


def _sigmoid(x):
    return 1.0 / (1.0 + jnp.exp(-x))


def _softplus(x):
    return jnp.maximum(x, 0.0) + jnp.log(1.0 + jnp.exp(-jnp.abs(x)))


def _rms(x, g):
    return x * lax.rsqrt(jnp.mean(x * x, axis=-1, keepdims=True) + NORM_EPS) * g


_DIMS = {"nn": ((1,), (0,)), "nt": ((1,), (1,)), "tn": ((0,), (0,))}


def matmul(pairs, mode, *, name, out_dtype=F32, res=None, alpha=1.0, tm=1088, tn=1024, tk=2048):
    a0, b0 = pairs[0]
    if mode == "nn":
        (m, k), n = a0.shape, b0.shape[1]
    elif mode == "nt":
        (m, k), n = a0.shape, b0.shape[0]
    else:
        (k, m), n = a0.shape, b0.shape[1]
    tm = _tile(m, tm, 128 if mode == "tn" else 16)
    tn = _tile(n, 2048 if mode == "tn" else tn, 128)
    tk = _tile(k, min(tk, 1024), 16) if mode == "tn" else _tile(k, tk, 128)
    nk = k // tk
    npair = len(pairs)
    if mode == "tn":
        a_spec = pl.BlockSpec((tk, tm), lambda i, j, kk: (kk, i))
    else:
        a_spec = pl.BlockSpec((tm, tk), lambda i, j, kk: (i, kk))
    if mode == "nt":
        b_spec = pl.BlockSpec((tn, tk), lambda i, j, kk: (j, kk))
    else:
        b_spec = pl.BlockSpec((tk, tn), lambda i, j, kk: (kk, j))
    o_spec = pl.BlockSpec((tm, tn), lambda i, j, kk: (i, j))
    dims = _DIMS[mode]

    def body(*refs):
        ab = refs[:2 * npair]
        res_ref = refs[2 * npair] if res is not None else None
        o_ref, acc_ref = refs[-2], refs[-1]
        kk = pl.program_id(2)

        @pl.when(kk == 0)
        def _():
            acc_ref[...] = jnp.zeros_like(acc_ref)

        part = _mm(ab[0][...], ab[1][...], dims)
        for p in range(1, npair):
            part = part + _mm(ab[2 * p][...], ab[2 * p + 1][...], dims)
        acc_ref[...] += part

        @pl.when(kk == nk - 1)
        def _():
            out = acc_ref[...] * alpha if alpha != 1.0 else acc_ref[...]
            if res_ref is not None:
                out = res_ref[...].astype(F32) + out
            o_ref[...] = out.astype(o_ref.dtype)

    args, specs = [], []
    for a, b in pairs:
        args += [a, b]
        specs += [a_spec, b_spec]
    if res is not None:
        args.append(res)
        specs.append(o_spec)
    return pl.pallas_call(
        body, grid=(m // tm, n // tn, nk), in_specs=specs, out_specs=o_spec,
        out_shape=jax.ShapeDtypeStruct((m, n), out_dtype), scratch_shapes=[pltpu.VMEM((tm, tn), F32)],
        compiler_params=_params(("parallel", "parallel", "arbitrary")), name=name)(*args)


def tilek(fn, ins, outs, *, n_rows, tr, name):
    tr = _tile(n_rows, tr, 16)
    n_in = len(ins)
    in_specs = []
    for arr, kind in ins:
        if kind == "r":
            in_specs.append(pl.BlockSpec((tr, arr.shape[1]), lambda i: (i, 0)))
        else:
            in_specs.append(pl.BlockSpec(arr.shape, lambda i, nd=arr.ndim: (0,) * nd))
    out_specs, out_shape = [], []
    has_acc = False
    for o in outs:
        if o[0] == "r":
            out_specs.append(pl.BlockSpec((tr, o[1]), lambda i: (i, 0)))
            out_shape.append(jax.ShapeDtypeStruct((n_rows, o[1]), o[2]))
        else:
            has_acc = True
            out_specs.append(pl.BlockSpec(o[1], lambda i, nd=len(o[1]): (0,) * nd))
            out_shape.append(jax.ShapeDtypeStruct(o[1], F32))

    def body(*refs):
        i = pl.program_id(0)
        vals = fn(*[r[...] for r in refs[:n_in]])
        for o, r, v in zip(outs, refs[n_in:], vals):
            if o[0] == "r":
                r[...] = v.astype(r.dtype)
            else:
                @pl.when(i == 0)
                def _(r=r):
                    r[...] = jnp.zeros_like(r)

                r[...] += v

    return pl.pallas_call(
        body, grid=(n_rows // tr,), in_specs=in_specs, out_specs=out_specs, out_shape=out_shape,
        compiler_params=_params(("arbitrary",) if has_acc else ("parallel",)), name=name)(*[a for a, _ in ins])


def rms_fwd(x, g, name):
    n, d = x.shape
    return tilek(lambda xv, gv: (_rms(xv, gv),), [(x, "r"), (g, "f")], [("r", d, MMD)], n_rows=n, tr=256, name=name)[0]


def rms_bwd(x, g, dy, dres, name):
    n, d = x.shape

    def fn(xv, gv, dyv, drv):
        _, vjp = jax.vjp(_rms, xv, gv)
        dx, dg = vjp(dyv.astype(F32))
        return drv + dx, dg

    return tilek(fn, [(x, "r"), (g, "f"), (dy, "r"), (dres, "r")], [("r", d, F32), ("acc", (1, d))],
                 n_rows=n, tr=128, name=name)


def loss_head(h, tgt, mask, g, name):
    n, d = h.shape

    def fn(hv, tv, mv, gv):
        def lossf(hh, gg):
            e = (_rms(hh, gg) - tv) * mv
            s = jnp.sum(jnp.sum(e * e, axis=1, keepdims=True), axis=0, keepdims=True)
            return s * (0.5 / d)

        l, vjp = jax.vjp(lossf, hv, gv)
        dh, dg = vjp(jnp.ones((1, 1), F32))
        return dh, dg, jnp.broadcast_to(l, (1, LANES))

    return tilek(fn, [(h, "r"), (tgt, "r"), (mask, "r"), (g, "f")],
                 [("r", d, F32), ("acc", (1, d)), ("acc", (1, LANES))], n_rows=n, tr=128, name=name)


def ffn_up(hn, wg, wu, name):
    n, d = hn.shape
    f = wg.shape[0]
    tm, tn = _tile(n, 544, 16), _tile(f, 512, 128)

    def body(a_ref, g_ref, u_ref, og_ref, ou_ref, oa_ref):
        a = a_ref[...]
        g = _mm(a, g_ref[...], ((1,), (1,)))
        u = _mm(a, u_ref[...], ((1,), (1,)))
        og_ref[...] = g
        ou_ref[...] = u
        oa_ref[...] = (g * _sigmoid(g) * u).astype(oa_ref.dtype)

    o_spec = pl.BlockSpec((tm, tn), lambda i, j: (i, j))
    w_spec = pl.BlockSpec((tn, d), lambda i, j: (j, 0))
    return pl.pallas_call(
        body, grid=(n // tm, f // tn), in_specs=[pl.BlockSpec((tm, d), lambda i, j: (i, 0)), w_spec, w_spec],
        out_specs=[o_spec, o_spec, o_spec],
        out_shape=[jax.ShapeDtypeStruct((n, f), F32), jax.ShapeDtypeStruct((n, f), F32), jax.ShapeDtypeStruct((n, f), MMD)],
        compiler_params=_params(("parallel", "parallel")), name=name)(hn, wg, wu)


def ffn_down_bwd(dh, wd, gate, up, name):
    n, d = dh.shape
    f = wd.shape[0]
    tm, tn = _tile(n, 544, 16), _tile(f, 1024, 128)

    def body(dh_ref, w_ref, g_ref, u_ref, dg_ref, du_ref):
        da = 0.5 * _mm(dh_ref[...], w_ref[...], ((1,), (1,)))
        g, u = g_ref[...], u_ref[...]
        s = _sigmoid(g)
        dg_ref[...] = (da * u * (s * (1.0 + g * (1.0 - s)))).astype(dg_ref.dtype)
        du_ref[...] = (da * (g * s)).astype(du_ref.dtype)

    o_spec = pl.BlockSpec((tm, tn), lambda i, j: (i, j))
    return pl.pallas_call(
        body, grid=(n // tm, f // tn),
        in_specs=[pl.BlockSpec((tm, d), lambda i, j: (i, 0)), pl.BlockSpec((tn, d), lambda i, j: (j, 0)), o_spec, o_spec],
        out_specs=[o_spec, o_spec],
        out_shape=[jax.ShapeDtypeStruct((n, f), MMD), jax.ShapeDtypeStruct((n, f), MMD)],
        compiler_params=_params(("parallel", "parallel")), name=name)(dh, wd, gate, up)


def ffn_forward(h, g, wg, wu, wd, tag):
    hn = rms_fwd(h, g, f"{tag}_rms")
    gate, up, act = ffn_up(hn, wg, wu, f"{tag}_up")
    out = matmul([(act, wd)], "nn", res=h, alpha=0.5, name=f"{tag}_down")
    return out, (hn, gate, up, act)


def ffn_backward(dout, h, g, wg, wu, wd, saved, tag):
    hn, gate, up, act = saved
    dgate, dup = ffn_down_bwd(dout, wd, gate, up, f"{tag}_dact")
    dwd = matmul([(act, dout)], "tn", alpha=0.5, name=f"{tag}_dwd")
    dwg = matmul([(dgate, hn)], "tn", name=f"{tag}_dwg")
    dwu = matmul([(dup, hn)], "tn", name=f"{tag}_dwu")
    dhn = matmul([(dgate, wg), (dup, wu)], "nn", name=f"{tag}_dhn")
    dh, dg = rms_bwd(h, g, dhn, dout, f"{tag}_drms")
    return dh, dg, dwg, dwu, dwd


def lerp_fwd(p, mu, bl, t, name):
    n, w = p.shape
    cb = _tile(w, 256, 128)

    def body(p_ref, mu_ref, o_ref):
        x = p_ref[...]
        row = lax.broadcasted_iota(jnp.int32, x.shape, 0)
        prev = jnp.where(row == 0, 0.0, pltpu.roll(x, 1, 0))
        o_ref[...] = x + mu_ref[...] * (prev - x)

    spec = pl.BlockSpec((t, cb), lambda b, j: (b, j))
    return pl.pallas_call(
        body, grid=(bl, w // cb), in_specs=[spec, pl.BlockSpec((1, cb), lambda b, j: (0, j))], out_specs=spec,
        out_shape=jax.ShapeDtypeStruct((n, w), F32), compiler_params=_params(("parallel", "parallel")), name=name)(p, mu)


def lerp_bwd(p, mu, douts, bl, t, name):
    n, w = p.shape
    cb = _tile(w, 256, 128)
    nd = len(douts)

    def body(*refs):
        p_ref, mu_ref = refs[0], refs[1]
        dp_ref, dmu_ref = refs[2 + nd], refs[3 + nd]
        b = pl.program_id(1)
        x, m = p_ref[...], mu_ref[...]
        d = refs[2][...]
        for r in refs[3:2 + nd]:
            d = d + r[...]
        row = lax.broadcasted_iota(jnp.int32, x.shape, 0)
        prev = jnp.where(row == 0, 0.0, pltpu.roll(x, 1, 0))
        z = d * m
        nxt = jnp.where(row == t - 1, 0.0, pltpu.roll(z, t - 1, 0))
        dp_ref[...] = d - z + nxt

        @pl.when(b == 0)
        def _():
            dmu_ref[...] = jnp.zeros_like(dmu_ref)

        dmu_ref[...] += jnp.sum(d * (prev - x), axis=0, keepdims=True)

    spec = pl.BlockSpec((t, cb), lambda j, b: (b, j))
    cspec = pl.BlockSpec((1, cb), lambda j, b: (0, j))
    return pl.pallas_call(
        body, grid=(w // cb, bl), in_specs=[spec, cspec] + [spec] * nd, out_specs=[spec, cspec],
        out_shape=[jax.ShapeDtypeStruct((n, w), F32), jax.ShapeDtypeStruct((1, w), F32)],
        compiler_params=_params(("parallel", "arbitrary")), name=name)(p, mu, *douts)


def _prep(k, xw, xa, xg, w0, a0, k_k, k_a, w_up, a_up, g_up, e, et):
    w_pre = -_softplus(-(w0 + mmdot(jnp.tanh(xw), w_up))) - 0.5
    decay = jnp.exp(-jnp.exp(w_pre))
    a = _sigmoid(a0 + mmdot(xa, a_up))
    g = mmdot(_sigmoid(xg), g_up)
    kk = k * k_k
    kk = kk * lax.rsqrt(jnp.maximum(segsum(kk * kk, e, et), 1e-24))
    kmod = k * (1.0 + (a - 1.0) * k_a)
    return decay, kmod, -kk, kk * a, g


def _lora_parts(xl):
    return xl[:, :LANES], xl[:, LANES:2 * LANES], xl[:, 2 * LANES:]


def rwkv_prep_fwd(pk, pl_, prm, e, et, name):
    n, d = pk.shape
    small = [prm[k] for k in ("w0", "a0", "k_k", "k_a", "w_up", "a_up", "g_up")]
    ins = [(pk, "r"), (pl_, "r")] + [(s, "f") for s in small] + [(e, "f"), (et, "f")]
    return tilek(lambda k, xl, *rest: _prep(k, *_lora_parts(xl), *rest), ins, [("r", d, F32)] * 5, n_rows=n, tr=128, name=name)


def rwkv_prep_bwd(pk, pl_, prm, e, et, cts, name):
    n, d = pk.shape
    small = [prm[k] for k in ("w0", "a0", "k_k", "k_a", "w_up", "a_up", "g_up")]

    def fn(k, xl, w0, a0, k_k, k_a, w_up, a_up, g_up, ev, etv, dw, dkm1, dkm2, dkn, db, dg):
        _, vjp = jax.vjp(lambda *a: _prep(*a, ev, etv), k, *_lora_parts(xl), w0, a0, k_k, k_a, w_up, a_up, g_up)
        dk, dxw, dxa, dxg, *dsmall = vjp((dw, dkm1 + dkm2, dkn, db, dg))
        return (dk, jnp.concatenate([dxw, dxa, dxg], axis=1), *dsmall)

    ins = [(pk, "r"), (pl_, "r")] + [(s, "f") for s in small] + [(e, "f"), (et, "f")] + [(c, "r") for c in cts]
    outs = [("r", d, F32), ("r", pl_.shape[1], F32)] + [("acc", s.shape) for s in small]
    return tilek(fn, ins, outs, n_rows=n, tr=64, name=name)


def _post(y, r, km, v, g, pga, pgb, yb, gn_w, gn_b, r_k, e, et):
    inv = 1.0 / RWKV_HEAD
    yc = y - segsum(y, e, et) * inv
    var = segsum(yc * yc, e, et) * inv
    yn = yc * lax.rsqrt(var + GN_EPS) * gn_w + gn_b
    bonus = segsum(r * km * r_k, e, et) * v
    ya = (yn + bonus) * g
    return _sigmoid(pga) * ya + _sigmoid(pgb) * yb


def rwkv_post_fwd(acts, prm, e, et, name):
    n, d = acts[0].shape
    small = [prm[k] for k in ("gn_w", "gn_b", "r_k")]
    ins = [(a, "r") for a in acts] + [(s, "f") for s in small] + [(e, "f"), (et, "f")]
    return tilek(lambda *a: (_post(*a),), ins, [("r", d, MMD)], n_rows=n, tr=128, name=name)[0]


def rwkv_post_bwd(acts, prm, e, et, dm, name):
    n, d = acts[0].shape
    small = [prm[k] for k in ("gn_w", "gn_b", "r_k")]
    na = len(acts)

    def fn(*a):
        prim, ev, etv, dmv = a[:na + 3], a[na + 3], a[na + 4], a[na + 5]
        _, vjp = jax.vjp(lambda *z: _post(*z, ev, etv), *prim)
        return vjp(dmv.astype(F32))

    ins = [(x, "r") for x in acts] + [(s, "f") for s in small] + [(e, "f"), (et, "f"), (dm, "r")]
    outs = [("r", d, F32)] * na + [("acc", s.shape) for s in small]
    return tilek(fn, ins, outs, n_rows=n, tr=64, name=name)


def _head_sums(x, first_head):
    a = jnp.sum(jnp.where(first_head, x, 0.0), axis=1, keepdims=True)
    b = jnp.sum(jnp.where(first_head, 0.0, x), axis=1, keepdims=True)
    return jnp.where(first_head, a, b)


def _round1(x):
    return (x.astype(BF16), None) if MMD == BF16 else _split2(x)


def _split2(x):
    hi = x.astype(BF16)
    return hi, (x - hi.astype(F32)).astype(BF16)


def _spread(row, eye2):
    hi, lo = _split2(row)
    return eye2 * hi, eye2 * lo


def _ones_dot(tiles, ones_blk):
    dims = (((1,), (0,)), ((), ()))
    res = lax.dot_general(jnp.concatenate([t[0] for t in tiles], axis=0), ones_blk, dims, preferred_element_type=F32)
    out = [res[i * RWKV_HEAD:(i + 1) * RWKV_HEAD] for i in range(len(tiles))]
    two_term = [i for i, t in enumerate(tiles) if t[1] is not None]
    if two_term:
        low = lax.dot_general(jnp.concatenate([tiles[i][1] for i in two_term], axis=0), ones_blk, dims,
                              preferred_element_type=F32)
        for n, i in enumerate(two_term):
            out[i] = out[i] + low[n * RWKV_HEAD:(n + 1) * RWKV_HEAD]
    return out


def _scan_consts():
    lane = lax.broadcasted_iota(jnp.int32, (1, LANES), 1)
    rows = lax.broadcasted_iota(jnp.int32, (RWKV_HEAD, LANES), 0)
    cols = lax.broadcasted_iota(jnp.int32, (RWKV_HEAD, LANES), 1)
    eye2 = ((cols & (RWKV_HEAD - 1)) == rows).astype(BF16)
    r2 = lax.broadcasted_iota(jnp.int32, (LANES, LANES), 0)
    c2 = lax.broadcasted_iota(jnp.int32, (LANES, LANES), 1)
    ones_blk = ((r2 // RWKV_HEAD) == (c2 // RWKV_HEAD)).astype(BF16)
    return lane, lane < RWKV_HEAD, eye2, ones_blk


def _riding_exchange(src_ref, dst_ref, send_sems, recv_sems, local_sem, scatter, grid):
    def copies():
        _, me = _peer(0)
        out = [pltpu.make_async_copy(src_ref.at[me] if scatter else src_ref, dst_ref.at[me], local_sem)]
        for k in range(1, N_DEV):
            dev, idx = _peer(k)
            out.append(pltpu.make_async_remote_copy(src_ref=src_ref.at[idx] if scatter else src_ref, dst_ref=dst_ref.at[me],
                                                    send_sem=send_sems.at[k - 1], recv_sem=recv_sems.at[k - 1],
                                                    device_id=dev, device_id_type=MESH))
        return out

    ids = [pl.program_id(a) for a in range(len(grid))]
    first = functools.reduce(jnp.logical_and, [i == 0 for i in ids])
    last = functools.reduce(jnp.logical_and, [i == n - 1 for i, n in zip(ids, grid)])

    def start():
        @pl.when(first)
        def _():
            for cp in copies():
                cp.start()

    def finish():
        @pl.when(last)
        def _():
            for cp in copies():
                cp.wait()

    return start, finish


_RIDE_SCRATCH = [pltpu.SemaphoreType.DMA((N_DEV - 1,)), pltpu.SemaphoreType.DMA((N_DEV - 1,)), pltpu.SemaphoreType.DMA]


def scan_forward(r, w, k, kn, b, v, ride, bl, t, d, name, pg, hch):
    npair, nst = d // LANES, t // hch
    grid = (bl, npair // pg, nst)

    def body(r_ref, w_ref, k_ref, kn_ref, b_ref, v_ref, ride_ref, y_ref, hist_ref, land_ref, s_ref, vb_ref, *sems):
        start, finish = _riding_exchange(ride_ref, land_ref, *sems, False, grid)
        start()
        _, first_head, eye2, ones_blk = _scan_consts()
        eye2f = eye2.astype(F32)
        diag = lambda tile: jnp.sum(tile * eye2f, axis=0, keepdims=True)

        @pl.when(pl.program_id(2) == 0)
        def _():
            s_ref[...] = jnp.zeros_like(s_ref)

        pair_cols = [slice(p * LANES, (p + 1) * LANES) for p in range(pg)]
        for p, tile in enumerate(_ones_dot([_spread(v_ref[0, :, cols], eye2) for cols in pair_cols], ones_blk)):
            vb_ref[p] = tile

        def step(ts, carry):
            prev, nxt = jnp.maximum(ts - 1, 0), jnp.minimum(ts + 1, hch - 1)
            states, tiles = [], []
            for p in range(pg):
                cols = slice(p * LANES, (p + 1) * LANES)
                s = s_ref[p]
                hist_ref[0, p, pl.ds(ts, 1)] = s[None]
                states.append(s)
                tiles.append(_round1(s * r_ref[prev, :, cols]))
                tiles.append(_spread(v_ref[nxt, :, cols], eye2))
            res = _ones_dot(tiles, ones_blk)
            for p in range(pg):
                cols = slice(p * LANES, (p + 1) * LANES)
                s = states[p]
                sa = _head_sums(s * kn_ref[ts, :, cols], first_head)
                s_ref[p] = s * w_ref[ts, :, cols] + sa * b_ref[ts, :, cols] + vb_ref[p] * k_ref[ts, :, cols]
            for p in range(pg):
                cols = slice(p * LANES, (p + 1) * LANES)
                y_ref[prev, :, cols] = diag(res[2 * p])
                vb_ref[p] = res[2 * p + 1]
            return carry

        lax.fori_loop(0, hch, step, 0)
        last = _ones_dot([_round1(s_ref[p] * r_ref[hch - 1, :, cols]) for p, cols in enumerate(pair_cols)], ones_blk)
        for p, cols in enumerate(pair_cols):
            y_ref[hch - 1, :, cols] = diag(last[p])
        finish()

    row_spec = pl.BlockSpec((hch, 1, pg * LANES), lambda bb, g, c: (bb * nst + c, 0, g))
    hist_spec = pl.BlockSpec((1, pg, hch, RWKV_HEAD, LANES), lambda bb, g, c: (bb, g, c, 0, 0))
    hbm = pl.BlockSpec(memory_space=pl.ANY)
    rows3 = [a.reshape(bl * t, 1, d) for a in (r, w, k, kn, b, v)]
    y, hist, landed = pl.pallas_call(
        body, grid=grid, in_specs=[row_spec] * 6 + [hbm], out_specs=[row_spec, hist_spec, hbm],
        out_shape=[jax.ShapeDtypeStruct((bl * t, 1, d), F32), jax.ShapeDtypeStruct((bl, npair, t, RWKV_HEAD, LANES), F32),
                   jax.ShapeDtypeStruct((N_DEV,) + ride.shape, ride.dtype)],
        scratch_shapes=[pltpu.VMEM((pg, RWKV_HEAD, LANES), F32)] * 2 + _RIDE_SCRATCH,
        compiler_params=_params(("arbitrary", "arbitrary", "arbitrary")), name=name)(*rows3, ride)
    return y.reshape(bl * t, d), hist, landed


def scan_backward(r, w, k, kn, b, v, dy, hist, ride, bl, t, d, name, pg, hch):
    npair, nst = d // LANES, t // hch
    grid = (bl, npair // pg, nst)

    def body(r_ref, w_ref, k_ref, kn_ref, b_ref, v_ref, dy_ref, hist_ref, ride_ref,
             dr_ref, dw_ref, dk_ref, dkn_ref, db_ref, dv_ref, land_ref, ds_ref, cur_ref, *sems):
        start, finish = _riding_exchange(ride_ref, land_ref, *sems, True, grid)
        start()
        _, first_head, eye2, ones_blk = _scan_consts()
        eye2f = eye2.astype(F32)
        colsum = lambda x: jnp.sum(x, axis=0, keepdims=True)

        @pl.when(pl.program_id(2) == 0)
        def _():
            ds_ref[...] = jnp.zeros_like(ds_ref)

        tiles = []
        for p in range(pg):
            cols = slice(p * LANES, (p + 1) * LANES)
            tiles += [_spread(v_ref[hch - 1, :, cols], eye2), _spread(dy_ref[hch - 1, :, cols], eye2),
                      _split2(hist_ref[0, p, hch - 1] * kn_ref[hch - 1, :, cols])]
        first = _ones_dot(tiles, ones_blk)
        for p in range(pg):
            cols = slice(p * LANES, (p + 1) * LANES)
            row = lambda ref: ref[hch - 1, :, cols]
            s_prev = hist_ref[0, p, hch - 1]
            vb, dyb, sa = first[3 * p], first[3 * p + 1], first[3 * p + 2]
            cur_ref[0, p], cur_ref[1, p] = vb, sa
            dr_ref[hch - 1, :, cols] = colsum((s_prev * row(w_ref) + sa * row(b_ref) + vb * row(k_ref)) * dyb)
            ds_ref[p] += dyb * row(r_ref)

        def step(it, carry):
            ts = hch - 1 - it
            prev = jnp.maximum(ts - 1, 0)
            has_prev = ts > 0
            grads, tiles = [], []
            for p in range(pg):
                cols = slice(p * LANES, (p + 1) * LANES)
                ds = ds_ref[p]
                grads.append(ds)
                tiles.append(_spread(v_ref[prev, :, cols], eye2))
                tiles.append(_spread(dy_ref[prev, :, cols], eye2))
                tiles.append(_split2(hist_ref[0, p, pl.ds(prev, 1)][0] * kn_ref[prev, :, cols]))
                tiles.append(_round1(ds * k_ref[ts, :, cols]))
            res = _ones_dot(tiles, ones_blk)
            for p in range(pg):
                cols = slice(p * LANES, (p + 1) * LANES)
                row = lambda ref: ref[ts, :, cols]
                ds = grads[p]
                w_, kn_, b_ = row(w_ref), row(kn_ref), row(b_ref)
                dsa = _head_sums(ds * b_, first_head)
                s_prev = hist_ref[0, p, pl.ds(ts, 1)][0]
                vb, sa, dyb_prev = cur_ref[0, p], cur_ref[1, p], res[4 * p + 1]
                dk_ref[ts, :, cols] = colsum(ds * vb)
                db_ref[ts, :, cols] = colsum(ds * sa)
                dw_ref[ts, :, cols] = colsum(ds * s_prev)
                dkn_ref[ts, :, cols] = colsum(s_prev * dsa)
                dv_ref[ts, :, cols] = colsum(res[4 * p + 3] * eye2f)
                dr_ref[prev, :, cols] = jnp.where(has_prev, colsum(s_prev * dyb_prev), dr_ref[prev, :, cols])
                ds_ref[p] = ds * w_ + dsa * kn_ + jnp.where(has_prev, dyb_prev, 0.0) * r_ref[prev, :, cols]
            for p in range(pg):
                cur_ref[0, p] = res[4 * p]
                cur_ref[1, p] = res[4 * p + 2]
            return carry

        lax.fori_loop(0, hch, step, 0)
        finish()

    row_spec = pl.BlockSpec((hch, 1, pg * LANES), lambda bb, g, c: (bb * nst + nst - 1 - c, 0, g))
    hist_spec = pl.BlockSpec((1, pg, hch, RWKV_HEAD, LANES), lambda bb, g, c: (bb, g, nst - 1 - c, 0, 0))
    hbm = pl.BlockSpec(memory_space=pl.ANY)
    row_shape = jax.ShapeDtypeStruct((bl * t, 1, d), F32)
    rows3 = [a.reshape(bl * t, 1, d) for a in (r, w, k, kn, b, v, dy)]
    outs = pl.pallas_call(
        body, grid=grid, in_specs=[row_spec] * 7 + [hist_spec, hbm], out_specs=[row_spec] * 6 + [hbm],
        out_shape=[row_shape] * 6 + [jax.ShapeDtypeStruct(ride.shape, ride.dtype)],
        scratch_shapes=[pltpu.VMEM((pg, RWKV_HEAD, LANES), F32), pltpu.VMEM((2, pg, RWKV_HEAD, LANES), F32)] + _RIDE_SCRATCH,
        compiler_params=_params(("arbitrary", "arbitrary", "arbitrary")), name=name)(*rows3, hist, ride)
    return [o.reshape(bl * t, d) for o in outs[:6]] + [outs[6]]


def _mla_norms(pm, gq, gkv):
    ql = gq.shape[1]
    kvl = gkv.shape[1]
    return _rms(pm[:, :ql], gq), _rms(pm[:, ql:ql + kvl], gkv)


def mla_prep_fwd(pm, gq, gkv, name):
    n = pm.shape[0]
    return tilek(_mla_norms, [(pm, "r"), (gq, "f"), (gkv, "f")],
                 [("r", gq.shape[1], MMD), ("r", gkv.shape[1], MMD)], n_rows=n, tr=256, name=name)


def mla_prep_bwd(pm, gq, gkv, dcq, dckv, dkpe, name):
    n, wm = pm.shape
    ql, kvl = gq.shape[1], gkv.shape[1]

    def fn(pmv, gqv, gkvv, d1, d2, d3):
        _, vjp1 = jax.vjp(_rms, pmv[:, :ql], gqv)
        _, vjp2 = jax.vjp(_rms, pmv[:, ql:ql + kvl], gkvv)
        dcq_in, dgq = vjp1(d1)
        dckv_in, dgkv = vjp2(d2)
        return jnp.concatenate([dcq_in, dckv_in, d3], axis=1), dgq, dgkv

    return tilek(fn, [(pm, "r"), (gq, "f"), (gkv, "f"), (dcq, "r"), (dckv, "r"), (dkpe, "r")],
                 [("r", wm, F32), ("acc", gq.shape), ("acc", gkv.shape)], n_rows=n, tr=128, name=name)


def _rope(x, c, s, first):
    sw = jnp.where(first, pltpu.roll(x, LANES - ROPE_DIM // 2, 1), pltpu.roll(x, ROPE_DIM // 2, 1))
    return x * c + sw * s


def _unrope(d, c, s, first):
    z = d * s
    sw = jnp.where(first, pltpu.roll(z, LANES - ROPE_DIM // 2, 1), pltpu.roll(z, ROPE_DIM // 2, 1))
    return d * c + sw


def _causal_segments(n_tiles, parts=4):
    bounds = sorted({round(n_tiles * s / parts) for s in range(parts + 1)})
    return list(zip(bounds[:-1], bounds[1:]))


def attn_fwd(q, kv, pm, ct, st, bl, t, hm, name):
    n = q.shape[0]
    tq = LANES
    scale = QK_DIM ** -0.5
    kpe_blk = pm.shape[1] // LANES - 1

    def body(qn_ref, qpe_ref, kn_ref, v_ref, kpe_ref, ct_ref, st_ref, o_ref, lse_ref, kp_s, kn_s, v_s):
        h = pl.program_id(1)
        lane = lax.broadcasted_iota(jnp.int32, (1, LANES), 1)
        first = (lane & (ROPE_DIM - 1)) < ROPE_DIM // 2
        kp = _rope(kpe_ref[...], ct_ref[...], st_ref[...], first)
        kp_s[...] = jnp.where(h % 2 == 0, kp, pltpu.roll(kp, ROPE_DIM, 1)).astype(MMD)
        kn_s[...] = kn_ref[...].astype(MMD)
        v_s[...] = v_ref[...].astype(MMD)
        def segment(lo, hi):
            ext = hi * tq
            kpos = lax.broadcasted_iota(jnp.int32, (1, ext), 1)

            def qtile(i, carry):
                rows = pl.ds(pl.multiple_of(i * tq, tq), tq)
                q2 = _rope(qpe_ref[rows, :], ct_ref[rows, :], st_ref[rows, :], first)
                s = (_mm(qn_ref[rows, :], kn_s[:ext, :], ((1,), (1,))) + _mm(q2, kp_s[:ext, :], ((1,), (1,)))) * scale
                qpos = i * tq + lax.broadcasted_iota(jnp.int32, (tq, 1), 0)
                s = jnp.where(kpos <= qpos, s, -1e30)
                m = jnp.max(s, axis=1, keepdims=True)
                p = jnp.exp(s - m)
                l = jnp.sum(p, axis=1, keepdims=True)
                o_ref[rows, :] = _mm(p, v_s[:ext, :]) / l
                lse_ref[0, 0, rows, :] = m + jnp.log(l)
                return carry

            lax.fori_loop(lo, hi, qtile, 0)

        for lo, hi in _causal_segments(t // tq):
            segment(lo, hi)

    blk = lambda f: pl.BlockSpec((t, LANES), f)
    return pl.pallas_call(
        body, grid=(bl, hm),
        in_specs=[blk(lambda b, h: (b, h)), blk(lambda b, h: (b, hm + h // 2)), blk(lambda b, h: (b, h)),
                  blk(lambda b, h: (b, hm + h)), blk(lambda b, h: (b, kpe_blk)), blk(lambda b, h: (0, 0)), blk(lambda b, h: (0, 0))],
        out_specs=[blk(lambda b, h: (b, h)), pl.BlockSpec((1, 1, t, 1), lambda b, h: (b, h, 0, 0))],
        out_shape=[jax.ShapeDtypeStruct((n, hm * LANES), F32), jax.ShapeDtypeStruct((bl, hm, t, 1), F32)],
        scratch_shapes=[pltpu.VMEM((t, LANES), MMD)] * 3,
        compiler_params=_params(("parallel", "arbitrary")), name=name)(q, q, kv, kv, pm, ct, st)


def attn_bwd(q, kv, pm, o, do, lse, ct, st, bl, t, hm, name):
    n = q.shape[0]
    tq = LANES
    scale = QK_DIM ** -0.5
    kpe_blk = pm.shape[1] // LANES - 1

    def body(qn_ref, qpe_ref, kn_ref, v_ref, kpe_ref, o_ref, do_ref, lse_ref, ct_ref, st_ref,
             dqn_ref, dqpe_ref, dkn_ref, dv_ref, dkpe_ref, kp_s, kn_s, v_s, dkn_s, dkp_s, dv_s):
        h = pl.program_id(1)
        lane = lax.broadcasted_iota(jnp.int32, (1, LANES), 1)
        first = (lane & (ROPE_DIM - 1)) < ROPE_DIM // 2
        mine = (lane // ROPE_DIM) == (h % 2)
        kp = _rope(kpe_ref[...], ct_ref[...], st_ref[...], first)
        kp_s[...] = jnp.where(h % 2 == 0, kp, pltpu.roll(kp, ROPE_DIM, 1)).astype(MMD)
        kn_s[...] = kn_ref[...].astype(MMD)
        v_s[...] = v_ref[...].astype(MMD)
        dkn_s[...] = jnp.zeros_like(dkn_s)
        dkp_s[...] = jnp.zeros_like(dkp_s)
        dv_s[...] = jnp.zeros_like(dv_s)
        @pl.when(h % 2 == 0)
        def _():
            dqpe_ref[...] = jnp.zeros_like(dqpe_ref)

        @pl.when(h == 0)
        def _():
            dkpe_ref[...] = jnp.zeros_like(dkpe_ref)

        def segment(lo, hi):
            ext = hi * tq
            kpos = lax.broadcasted_iota(jnp.int32, (1, ext), 1)

            def qtile(i, carry):
                rows = pl.ds(pl.multiple_of(i * tq, tq), tq)
                c_i, s_i = ct_ref[rows, :], st_ref[rows, :]
                q1 = qn_ref[rows, :].astype(MMD)
                q2 = _rope(qpe_ref[rows, :], c_i, s_i, first).astype(MMD)
                s = (_mm(q1, kn_s[:ext, :], ((1,), (1,))) + _mm(q2, kp_s[:ext, :], ((1,), (1,)))) * scale
                qpos = i * tq + lax.broadcasted_iota(jnp.int32, (tq, 1), 0)
                p = jnp.where(kpos <= qpos, jnp.exp(s - lse_ref[0, 0, rows, :]), 0.0)
                do_i = do_ref[rows, :]
                delta = jnp.sum(do_i * o_ref[rows, :], axis=1, keepdims=True)
                dp = _mm(do_i, v_s[:ext, :], ((1,), (1,)))
                ds = (p * (dp - delta) * scale).astype(MMD)
                dqn_ref[rows, :] = _mm(ds, kn_s[:ext, :])
                dq2 = jnp.where(mine, _mm(ds, kp_s[:ext, :]), 0.0)
                dqpe_ref[rows, :] += _unrope(dq2, c_i, s_i, first)
                dkn_s[:ext, :] += _mm(ds, q1, ((0,), (0,)))
                dkp_s[:ext, :] += _mm(ds, q2, ((0,), (0,)))
                dv_s[:ext, :] += _mm(p, do_i, ((0,), (0,)))
                return carry

            lax.fori_loop(lo, hi, qtile, 0)

        for lo, hi in _causal_segments(t // tq):
            segment(lo, hi)
        dkn_ref[...] = dkn_s[...]
        dv_ref[...] = dv_s[...]
        dkp = jnp.where(mine, dkp_s[...], 0.0)
        dkp = jnp.where(h % 2 == 0, dkp, pltpu.roll(dkp, ROPE_DIM, 1))
        dkpe_ref[...] += _unrope(dkp, ct_ref[...], st_ref[...], first)

    blk = lambda f: pl.BlockSpec((t, LANES), f)
    hd = lambda b, h: (b, h)
    shp = lambda wd: jax.ShapeDtypeStruct((n, wd), F32)
    return pl.pallas_call(
        body, grid=(bl, hm),
        in_specs=[blk(hd), blk(lambda b, h: (b, hm + h // 2)), blk(hd), blk(lambda b, h: (b, hm + h)),
                  blk(lambda b, h: (b, kpe_blk)), blk(hd), blk(hd), pl.BlockSpec((1, 1, t, 1), lambda b, h: (b, h, 0, 0)),
                  blk(lambda b, h: (0, 0)), blk(lambda b, h: (0, 0))],
        out_specs=[blk(hd), blk(lambda b, h: (b, h // 2)), blk(hd), blk(hd), blk(lambda b, h: (b, 0))],
        out_shape=[shp(hm * LANES), shp(hm * ROPE_DIM), shp(hm * LANES), shp(hm * LANES), shp(LANES)],
        scratch_shapes=[pltpu.VMEM((t, LANES), MMD)] * 3 + [pltpu.VMEM((t, LANES), F32)] * 3,
        compiler_params=_params(("parallel", "arbitrary")), name=name)(q, q, kv, kv, pm, o, do, lse, ct, st)


def _peer(k):
    mx, my, mc = lax.axis_index("x"), lax.axis_index("y"), lax.axis_index("c")
    px = 1 - mx if k & 4 else mx
    py = 1 - my if k & 2 else my
    pc = 1 - mc if k & 1 else mc
    return (px, py, pc), 4 * px + 2 * py + pc


def all_gather(x, name):
    def body(x_ref, o_ref, send_sems, recv_sems, local_sem):
        _, me = _peer(0)
        local = pltpu.make_async_copy(x_ref, o_ref.at[me], local_sem)
        local.start()
        copies = []
        for k in range(1, N_DEV):
            dev, _ = _peer(k)
            cp = pltpu.make_async_remote_copy(src_ref=x_ref, dst_ref=o_ref.at[me], send_sem=send_sems.at[k - 1],
                                              recv_sem=recv_sems.at[k - 1], device_id=dev, device_id_type=MESH)
            cp.start()
            copies.append(cp)
        for cp in copies:
            cp.wait()
        local.wait()

    return pl.pallas_call(
        body, in_specs=[pl.BlockSpec(memory_space=pl.ANY)], out_specs=pl.BlockSpec(memory_space=pl.ANY),
        out_shape=jax.ShapeDtypeStruct((N_DEV,) + x.shape, x.dtype),
        scratch_shapes=[pltpu.SemaphoreType.DMA((N_DEV - 1,)), pltpu.SemaphoreType.DMA((N_DEV - 1,)), pltpu.SemaphoreType.DMA],
        name=name)(x)


def all_to_all(x, name):
    def body(x_ref, o_ref, send_sems, recv_sems, local_sem):
        _, me = _peer(0)
        local = pltpu.make_async_copy(x_ref.at[me], o_ref.at[me], local_sem)
        local.start()
        copies = []
        for k in range(1, N_DEV):
            dev, idx = _peer(k)
            cp = pltpu.make_async_remote_copy(src_ref=x_ref.at[idx], dst_ref=o_ref.at[me], send_sem=send_sems.at[k - 1],
                                              recv_sem=recv_sems.at[k - 1], device_id=dev, device_id_type=MESH)
            cp.start()
            copies.append(cp)
        for cp in copies:
            cp.wait()
        local.wait()

    return pl.pallas_call(
        body, in_specs=[pl.BlockSpec(memory_space=pl.ANY)], out_specs=pl.BlockSpec(memory_space=pl.ANY),
        out_shape=jax.ShapeDtypeStruct(x.shape, x.dtype),
        scratch_shapes=[pltpu.SemaphoreType.DMA((N_DEV - 1,)), pltpu.SemaphoreType.DMA((N_DEV - 1,)), pltpu.SemaphoreType.DMA],
        name=name)(x)


def _chips():
    mx, my, mc = lax.axis_index("x"), lax.axis_index("y"), lax.axis_index("c")
    return (mx, my, mc), (mx, my, 1 - mc), [(1 - mx, my), (mx, 1 - my), (1 - mx, 1 - my)]


def all_gather_two_level(x, name):
    def body(x_ref, o_ref, send_sems, recv_sems, local_sem):
        me, sibling, chips = _chips()
        blk = lambda px, py, pc: o_ref.at[4 * px + 2 * py + pc]

        def copy(k, block, to, src=None):
            return pltpu.make_async_remote_copy(src_ref=blk(*block) if src is None else src, dst_ref=blk(*block),
                                                send_sem=send_sems.at[k], recv_sem=recv_sems.at[k], device_id=to,
                                                device_id_type=MESH)

        mine = pltpu.make_async_copy(x_ref, blk(*me), local_sem)
        mine.start()
        first = [copy(0, me, sibling, src=x_ref)] + [copy(1 + j, me, (*chip, me[2]), src=x_ref) for j, chip in enumerate(chips)]
        for cp in first:
            cp.start()
        passed = [copy(4 + j, (*chip, me[2]), sibling) for j, chip in enumerate(chips)]
        for j, chip in enumerate(chips):
            copy(1 + j, (*chip, me[2]), me).wait_recv()
            passed[j].start()
        copy(0, sibling, me).wait_recv()
        for j, chip in enumerate(chips):
            copy(4 + j, (*chip, 1 - me[2]), me).wait_recv()
        for cp in first + passed:
            cp.wait_send()
        mine.wait()

    return pl.pallas_call(
        body, in_specs=[pl.BlockSpec(memory_space=pl.ANY)], out_specs=pl.BlockSpec(memory_space=pl.ANY),
        out_shape=jax.ShapeDtypeStruct((N_DEV,) + x.shape, x.dtype),
        scratch_shapes=[pltpu.SemaphoreType.DMA((N_DEV - 1,)), pltpu.SemaphoreType.DMA((N_DEV - 1,)), pltpu.SemaphoreType.DMA],
        name=name)(x)


def exchange_sibling(x, name):
    def body(x_ref, o_ref, send_sems, recv_sems):
        me, sibling, _ = _chips()
        copies = []
        for q in range(N_DEV // 2):
            cp = pltpu.make_async_remote_copy(src_ref=x_ref.at[2 * q + 1 - me[2]], dst_ref=o_ref.at[q], send_sem=send_sems.at[q],
                                              recv_sem=recv_sems.at[q], device_id=sibling, device_id_type=MESH)
            cp.start()
            copies.append(cp)
        for cp in copies:
            cp.wait()

    return pl.pallas_call(
        body, in_specs=[pl.BlockSpec(memory_space=pl.ANY)], out_specs=pl.BlockSpec(memory_space=pl.ANY),
        out_shape=jax.ShapeDtypeStruct((N_DEV // 2,) + x.shape[1:], x.dtype),
        scratch_shapes=[pltpu.SemaphoreType.DMA((N_DEV // 2,)), pltpu.SemaphoreType.DMA((N_DEV // 2,))], name=name)(x)


def exchange_chips(x, name):
    def body(x_ref, o_ref, send_sems, recv_sems, local_sem):
        me, _, chips = _chips()
        here = 2 * me[0] + me[1]
        local = pltpu.make_async_copy(x_ref.at[here], o_ref.at[here], local_sem)
        local.start()
        copies = []
        for j, (px, py) in enumerate(chips):
            cp = pltpu.make_async_remote_copy(src_ref=x_ref.at[2 * px + py], dst_ref=o_ref.at[here], send_sem=send_sems.at[j],
                                              recv_sem=recv_sems.at[j], device_id=(px, py, me[2]), device_id_type=MESH)
            cp.start()
            copies.append(cp)
        for cp in copies:
            cp.wait()
        local.wait()

    return pl.pallas_call(
        body, in_specs=[pl.BlockSpec(memory_space=pl.ANY)], out_specs=pl.BlockSpec(memory_space=pl.ANY),
        out_shape=jax.ShapeDtypeStruct(x.shape, x.dtype),
        scratch_shapes=[pltpu.SemaphoreType.DMA((3,)), pltpu.SemaphoreType.DMA((3,)), pltpu.SemaphoreType.DMA], name=name)(x)


def add_blocks(a, b, name):
    q, r, c = a.shape
    tr = _tile(r, max(16, (2 << 20) // (c * a.dtype.itemsize)), 16)
    spec = pl.BlockSpec((1, tr, c), lambda i, j: (i, j, 0))

    def body(a_ref, b_ref, o_ref):
        o_ref[...] = (a_ref[...].astype(F32) + b_ref[...].astype(F32)).astype(o_ref.dtype)

    return pl.pallas_call(
        body, grid=(q, r // tr), in_specs=[spec, spec], out_specs=spec, out_shape=jax.ShapeDtypeStruct(a.shape, a.dtype),
        compiler_params=_params(("parallel", "parallel")), name=name)(a, b)


def reduce_scatter_two_level(x, tag):
    q = N_DEV // 2
    from_sibling = exchange_sibling(x, f"{tag}_sibling")
    mine = lax.dynamic_index_in_dim(x.reshape((q, 2) + x.shape[1:]), lax.axis_index("c"), axis=1, keepdims=False)
    chip_sums = add_blocks(mine, from_sibling, f"{tag}_pair_sum")
    return sum_blocks(exchange_chips(chip_sums, f"{tag}_chips"), f"{tag}_sum")


def sum_blocks(x, name):
    nb, r, c = x.shape
    tr = _tile(r, max(16, (4 << 20) // (nb * c * x.dtype.itemsize)), 16)

    def body(x_ref, o_ref):
        acc = x_ref[0].astype(F32)
        for i in range(1, nb):
            acc = acc + x_ref[i].astype(F32)
        o_ref[...] = acc

    return pl.pallas_call(
        body, grid=(r // tr,), in_specs=[pl.BlockSpec((nb, tr, c), lambda i: (0, i, 0))],
        out_specs=pl.BlockSpec((tr, c), lambda i: (i, 0)), out_shape=jax.ShapeDtypeStruct((r, c), F32),
        compiler_params=_params(("parallel",)), name=name)(x)


def _adamw(w, g, m, v):
    m = ADAM_B1 * m + (1.0 - ADAM_B1) * g
    v = ADAM_B2 * v + (1.0 - ADAM_B2) * jnp.square(g)
    m_hat = m / (1.0 - ADAM_B1 ** ADAM_STEP)
    v_hat = v / (1.0 - ADAM_B2 ** ADAM_STEP)
    delta = -ADAM_LR * (m_hat / (jnp.sqrt(v_hat) + ADAM_EPS) + ADAM_WD * w)
    return delta, m, v


def adamw(w, g, m, v, name):
    r, c = w.shape
    tr = _tile(r, 256, 8)
    spec = pl.BlockSpec((tr, c), lambda i: (i, 0))

    def body(w_ref, g_ref, m_ref, v_ref, d_ref, nm_ref, nv_ref):
        d_ref[...], nm_ref[...], nv_ref[...] = _adamw(w_ref[...], g_ref[...], m_ref[...], v_ref[...])

    return pl.pallas_call(
        body, grid=(r // tr,), in_specs=[spec] * 4, out_specs=[spec] * 3,
        out_shape=[jax.ShapeDtypeStruct((r, c), F32)] * 3, compiler_params=_params(("parallel",)), name=name)(w, g, m, v)


def batch_sum_rows(dh, bl, t, rows, name):
    d = dh.shape[1]

    def body(x_ref, o_ref):
        @pl.when(pl.program_id(0) == 0)
        def _():
            o_ref[...] = jnp.zeros_like(o_ref)

        o_ref[...] += x_ref[...]

    return pl.pallas_call(
        body, grid=(bl,), in_specs=[pl.BlockSpec((rows, d), lambda b: (b * (t // rows), 0))],
        out_specs=pl.BlockSpec((rows, d), lambda b: (0, 0)), out_shape=jax.ShapeDtypeStruct((rows, d), F32),
        compiler_params=_params(("arbitrary",)), name=name)(dh)


class Dims:
    def __init__(self, x, meta_full_cols, w_up, g_up, q_norm, kv_norm, d_ff):
        self.bl, self.seq, self.d = x.shape
        self.n_meta = 16
        self.t_real = self.n_meta + self.seq
        self.t = -(-self.t_real // LANES) * LANES
        self.n = self.bl * self.t
        self.f = d_ff
        self.wl, self.gl = w_up.shape[-2], g_up.shape[-2]
        self.ql, self.kvl = q_norm.shape[-1], kv_norm.shape[-1]
        self.hm = self.d // V_DIM
        self.in_cols = 5 * self.d + 2 * self.wl + self.gl + self.ql + self.kvl + ROPE_DIM


def _pad_cols(a, width):
    return jnp.pad(a, ((0, 0), (0, width - a.shape[1])))


def _pad_rows(a, rows):
    return jnp.pad(a, ((0, rows - a.shape[0]), (0, 0)))


def split_in(a, dm, axis=1):
    d, wl, gl, ql, kvl = dm.d, dm.wl, dm.gl, dm.ql, dm.kvl
    size = a.shape[axis]
    cut = lambda lo, hi: lax.slice_in_dim(a, min(lo, size), min(hi, size), axis=axis)

    def pad(p, width):
        cfg = [(0, 0)] * a.ndim
        cfg[axis] = (0, width - p.shape[axis])
        return jnp.pad(p, cfg)

    o = 3 * d
    lora = jnp.concatenate([pad(cut(o, o + wl), LANES), pad(cut(o + wl, o + 2 * wl), LANES),
                            cut(o + 2 * wl, o + 2 * wl + gl)], axis=axis)
    o += 2 * wl + gl
    mla = pad(cut(o, o + ql + kvl + ROPE_DIM), ql + kvl + LANES)
    o += ql + kvl + ROPE_DIM
    return dict(r=cut(0, d), k=cut(d, 2 * d), v=cut(2 * d, 3 * d), l=lora, m=mla, ga=cut(o, o + d), gb=cut(o + d, o + 2 * d))


def merge_in(g, dm, axis=1):
    wl, gl, ql, kvl = dm.wl, dm.gl, dm.ql, dm.kvl
    cut = lambda p, lo, hi: lax.slice_in_dim(p, lo, hi, axis=axis)
    l, m = g["l"], g["m"]
    return jnp.concatenate([g["r"], g["k"], g["v"], cut(l, 0, wl), cut(l, LANES, LANES + wl), cut(l, 2 * LANES, 2 * LANES + gl),
                            cut(m, 0, ql + kvl + ROPE_DIM), g["ga"], g["gb"]], axis=axis)


def split_uq(w, dm):
    w3 = w.reshape(w.shape[0], dm.hm, QK_DIM)
    return jnp.concatenate([w3[:, :, :NOPE_DIM].reshape(w.shape[0], -1), w3[:, :, NOPE_DIM:].reshape(w.shape[0], -1)], axis=1)


def merge_uq(gn, gp, dm):
    r = gn.shape[0]
    return jnp.concatenate([gn.reshape(r, dm.hm, NOPE_DIM), gp.reshape(r, dm.hm, ROPE_DIM)], axis=2).reshape(r, -1)


def split_ukv(w, dm):
    w3 = w.reshape(w.shape[0], dm.hm, NOPE_DIM + V_DIM)
    return jnp.concatenate([w3[:, :, :NOPE_DIM].reshape(w.shape[0], -1), w3[:, :, NOPE_DIM:].reshape(w.shape[0], -1)], axis=1)


def merge_ukv(gk, gv, dm):
    r = gk.shape[0]
    return jnp.concatenate([gk.reshape(r, dm.hm, NOPE_DIM), gv.reshape(r, dm.hm, V_DIM)], axis=2).reshape(r, -1)


def head_matrices(d):
    heads = d // RWKV_HEAD
    e = (np.arange(d)[:, None] // RWKV_HEAD == np.arange(LANES)[None, :]) & (np.arange(LANES)[None, :] < heads)
    return jnp.asarray(e, BF16), jnp.asarray(e.T, BF16)


def rope_tables(t):
    pos = jnp.arange(t, dtype=F32)
    inv_freq = 1.0 / (ROPE_THETA ** (jnp.arange(0, ROPE_DIM, 2, dtype=F32) / ROPE_DIM))
    ang = pos[:, None] * inv_freq[None, :]
    cos, sin = jnp.cos(ang), jnp.sin(ang)
    return jnp.tile(jnp.concatenate([cos, cos], axis=1), (1, 2)), jnp.tile(jnp.concatenate([-sin, sin], axis=1), (1, 2))


def local_step(dm, x, loss_target, meta, wt, late_shards, late_rows, sp):
    bl, t, n, d, hm = dm.bl, dm.t, dm.n, dm.d, dm.hm
    e, et = head_matrices(d)
    ct, st = rope_tables(t)
    padz = jnp.zeros((bl, t - dm.t_real, d), F32)
    h0 = jnp.concatenate([jnp.broadcast_to(meta[None], (bl, dm.n_meta, d)), x, padz], axis=1).reshape(n, d)
    tgt = jnp.concatenate([jnp.zeros((bl, dm.n_meta, d), F32), loss_target, padz], axis=1).reshape(n, d)
    tpos = jnp.arange(t)
    mask = jnp.tile(((tpos >= dm.n_meta) & (tpos < dm.t_real)).astype(F32), bl).reshape(n, 1)

    win = split_in(wt["w_in"], dm, axis=0)
    mu = split_in(sp["tm_mu"], dm)
    wq, wkv = split_uq(wt["w_uq"], dm), split_ukv(wt["w_ukv"], dm)
    prm = dict(w0=sp["w0"], a0=sp["a0"], k_k=sp["k_k"], k_a=sp["k_a"], gn_w=sp["gn_w"], gn_b=sp["gn_b"], r_k=sp["r_k"],
               w_up=_pad_rows(wt["w_up"], LANES).astype(F32), a_up=_pad_rows(wt["a_up"], LANES).astype(F32),
               g_up=wt["g_up"].astype(F32))

    h1, ffn1 = ffn_forward(h0, sp["ffn1_norm"], wt["ffn1_w_gate"], wt["ffn1_w_up"], wt["ffn1_w_down"], "ffn1")
    u = rms_fwd(h1, sp["mix_norm"], "mix_rms")
    proj = {key: matmul([(u, win[key])], "nt", name=f"proj_{key}") for key in win}
    sh = {key: lerp_fwd(proj[key], mu[key], bl, t, f"shift_{key}") for key in ("r", "k", "v", "l")}
    decay, kmod, kneg, bvec, gate = rwkv_prep_fwd(sh["k"], sh["l"], prm, e, et, "rwkv_prep")
    pairs = min(SCAN_PAIRS, d // LANES)
    y, hist, late_all = scan_forward(sh["r"], decay, kmod, kneg, bvec, sh["v"], late_shards, bl, t, d, "wkv_scan",
                                     pairs, SCAN_FWD_STEPS)
    wt = dict(wt, **{key: late_all[:, lo:hi].reshape(-1, d) for key, lo, hi in zip(LATE, late_rows[:-1], late_rows[1:])})
    cqn, ckvn = mla_prep_fwd(proj["m"], sp["q_norm"], sp["kv_norm"], "mla_norms")
    q = matmul([(cqn, wq)], "nn", name="mla_q")
    kv = matmul([(ckvn, wkv)], "nn", name="mla_kv")
    o, lse = attn_fwd(q, kv, proj["m"], ct, st, bl, t, hm, "mla_attn")
    post_in = [y, sh["r"], kmod, sh["v"], gate, proj["ga"], proj["gb"], o]
    mix = rwkv_post_fwd(post_in, prm, e, et, "mix_gate")
    h2 = matmul([(mix, wt["w_out"])], "nn", res=h1, name="out_proj")
    h3, ffn2 = ffn_forward(h2, sp["ffn2_norm"], wt["ffn2_w_gate"], wt["ffn2_w_up"], wt["ffn2_w_down"], "ffn2")
    dh3, d_final, loss = loss_head(h3, tgt, mask, sp["final_norm"], "loss_head")

    gw, gs = {}, {"final_norm": d_final}
    dh2, gs["ffn2_norm"], gw["ffn2_w_gate"], gw["ffn2_w_up"], gw["ffn2_w_down"] = ffn_backward(
        dh3, h2, sp["ffn2_norm"], wt["ffn2_w_gate"], wt["ffn2_w_up"], wt["ffn2_w_down"], ffn2, "ffn2")
    dmix = matmul([(dh2, wt["w_out"])], "nt", name="out_proj_dx")
    gw["w_out"] = matmul([(mix, dh2)], "tn", name="out_proj_dw")
    late_grads = jnp.concatenate([gw.pop(key).reshape(N_DEV, hi - lo, d) for key, lo, hi in
                                  zip(LATE, late_rows[:-1], late_rows[1:])], axis=1).astype(MMD)
    (dy, dr_p, dkm_p, dv_p, dgate, dpga, dpgb, do, gs["gn_w"], gs["gn_b"], gs["r_k"]) = rwkv_post_bwd(
        post_in, prm, e, et, dmix, "mix_gate_bwd")
    dqn, dqpe, dkn, dv_att, dkpe = attn_bwd(q, kv, proj["m"], o, do, lse, ct, st, bl, t, hm, "mla_attn_bwd")
    nq = hm * NOPE_DIM
    dcqn = matmul([(dqn, wq[:, :nq])], "nt", name="mla_q_dx1")
    dcqn = matmul([(dqpe, wq[:, nq:])], "nt", res=dcqn, name="mla_q_dx2")
    gw["w_uq"] = merge_uq(matmul([(cqn, dqn)], "tn", name="mla_q_dw1"), matmul([(cqn, dqpe)], "tn", name="mla_q_dw2"), dm)
    dckvn = matmul([(dkn, wkv[:, :nq]), (dv_att, wkv[:, nq:])], "nt", name="mla_kv_dx", tk=1024)
    gw["w_ukv"] = merge_ukv(matmul([(ckvn, dkn)], "tn", name="mla_kv_dw1"), matmul([(ckvn, dv_att)], "tn", name="mla_kv_dw2"), dm)
    dproj = {"ga": dpga, "gb": dpgb}
    dproj["m"], gs["q_norm"], gs["kv_norm"] = mla_prep_bwd(proj["m"], sp["q_norm"], sp["kv_norm"], dcqn, dckvn, dkpe, "mla_norms_bwd")
    dr_s, ddecay, dk_s, dkneg, dbvec, dv_s, late_recv = scan_backward(
        sh["r"], decay, kmod, kneg, bvec, sh["v"], dy, hist, late_grads, bl, t, d, "wkv_scan_bwd", pairs, SCAN_BWD_STEPS)
    late_sum = sum_blocks(late_recv, "sum_late")
    (dsh_k, dsh_l, gs["w0"], gs["a0"], gs["k_k"], gs["k_a"], g_wup, g_aup, gw["g_up"]) = rwkv_prep_bwd(
        sh["k"], sh["l"], prm, e, et, [ddecay, dk_s, dkm_p, dkneg, dbvec, dgate], "rwkv_prep_bwd")
    gw["w_up"], gw["a_up"] = g_wup[:dm.wl], g_aup[:dm.wl]
    dmu = {}
    for key, cts in (("r", [dr_s, dr_p]), ("k", [dsh_k]), ("v", [dv_s, dv_p]), ("l", [dsh_l])):
        dproj[key], dmu[key] = lerp_bwd(proj[key], mu[key], cts, bl, t, f"shift_{key}_bwd")
    zero_m = jnp.zeros((1, proj["m"].shape[1]), F32)
    gs["tm_mu"] = merge_in(dict(dmu, m=zero_m, ga=zero_m[:, :0], gb=zero_m[:, :0]), dm)[:, :3 * d + 2 * dm.wl + dm.gl]
    wide = ("r", "k", "v", "ga", "gb")
    du = matmul([(dproj[key], win[key]) for key in wide], "nn", name="proj_dx", tn=512, tk=512)
    du = matmul([(dproj["l"], win["l"])], "nn", res=du, name="proj_dx_l")
    du = matmul([(dproj["m"], win["m"])], "nn", res=du, name="proj_dx_m")
    gw["w_in"] = merge_in({key: matmul([(dproj[key], u)], "tn", name=f"proj_dw_{key}") for key in win}, dm, axis=0)
    dh1, gs["mix_norm"] = rms_bwd(h1, sp["mix_norm"], du, dh2, "mix_rms_bwd")
    dh0, gs["ffn1_norm"], gw["ffn1_w_gate"], gw["ffn1_w_up"], gw["ffn1_w_down"] = ffn_backward(
        dh1, h0, sp["ffn1_norm"], wt["ffn1_w_gate"], wt["ffn1_w_up"], wt["ffn1_w_down"], ffn1, "ffn1")
    grad_x = dh0.reshape(bl, t, d)[:, dm.n_meta:dm.t_real]
    dmeta = batch_sum_rows(dh0, bl, t, dm.n_meta, "meta_grad")
    return loss, grad_x, dmeta, gw, late_sum, gs


COL_SHARDED = ("ffn1_w_gate", "ffn1_w_up", "w_in", "w_up", "a_up", "g_up", "w_uq", "w_ukv", "ffn2_w_gate", "ffn2_w_up")
ROW_SHARDED = ("ffn1_w_down", "w_out", "ffn2_w_down")
TRANSPOSED = ("ffn1_w_gate", "ffn1_w_up", "w_in", "ffn2_w_gate", "ffn2_w_up")
EARLY = ("ffn1_w_gate", "ffn1_w_up", "ffn1_w_down", "w_in")
LATE = ("w_out", "ffn2_w_gate", "ffn2_w_up", "ffn2_w_down")
NARROW = ("w_up", "a_up", "g_up", "w_uq", "w_ukv")
MATRICES = ("ffn1_w_gate", "ffn1_w_up", "ffn1_w_down", "w_in", "w_up", "a_up", "g_up", "w_uq", "w_ukv", "w_out",
            "ffn2_w_gate", "ffn2_w_up", "ffn2_w_down")
SMALL = ("ffn1_norm", "mix_norm", "tm_mu", "w0", "a0", "k_k", "k_a", "r_k", "gn_w", "gn_b", "q_norm", "kv_norm",
         "ffn2_norm", "final_norm")
WEIGHTS = ("meta_tokens", "ffn1_norm", "ffn1_w_gate", "ffn1_w_up", "ffn1_w_down", "mix_norm", "w_in", "tm_mu", "w0", "w_up",
           "a0", "a_up", "g_up", "k_k", "k_a", "r_k", "gn_w", "gn_b", "q_norm", "w_uq", "kv_norm", "w_ukv", "w_out",
           "ffn2_norm", "ffn2_w_gate", "ffn2_w_up", "ffn2_w_down", "final_norm")
PACK_COLS = 1024
PACK_ALIGN = 16 * PACK_COLS


def _pack(parts):
    offs, o = [], 0
    for p in parts:
        offs.append(o)
        o += p.shape[1]
    total = -(-o // PACK_ALIGN) * PACK_ALIGN
    flat = jnp.concatenate(list(parts) + [jnp.zeros((parts[0].shape[0], total - o), parts[0].dtype)], axis=1)
    return flat.reshape(parts[0].shape[0], total // PACK_COLS, PACK_COLS), offs


def kernel(x, meta_tokens, ffn1_norm, ffn1_w_gate, ffn1_w_up, ffn1_w_down, mix_norm, w_in, tm_mu, w0, w_up, a0, a_up, g_up, k_k, k_a, r_k, gn_w, gn_b, q_norm, w_uq, kv_norm, w_ukv, w_out, ffn2_norm, ffn2_w_gate, ffn2_w_up, ffn2_w_down, final_norm, loss_target, m_meta_tokens, m_ffn1_norm, m_ffn1_w_gate, m_ffn1_w_up, m_ffn1_w_down, m_mix_norm, m_w_in, m_tm_mu, m_w0, m_w_up, m_a0, m_a_up, m_g_up, m_k_k, m_k_a, m_r_k, m_gn_w, m_gn_b, m_q_norm, m_w_uq, m_kv_norm, m_w_ukv, m_w_out, m_ffn2_norm, m_ffn2_w_gate, m_ffn2_w_up, m_ffn2_w_down, m_final_norm, v_meta_tokens, v_ffn1_norm, v_ffn1_w_gate, v_ffn1_w_up, v_ffn1_w_down, v_mix_norm, v_w_in, v_tm_mu, v_w0, v_w_up, v_a0, v_a_up, v_g_up, v_k_k, v_k_a, v_r_k, v_gn_w, v_gn_b, v_q_norm, v_w_uq, v_kv_norm, v_w_ukv, v_w_out, v_ffn2_norm, v_ffn2_w_gate, v_ffn2_w_up, v_ffn2_w_down, v_final_norm):
    args = dict(locals())
    wts = {k: args[k] for k in WEIGHTS}
    ms = {k: args["m_" + k] for k in WEIGHTS}
    vs = {k: args["v_" + k] for k in WEIGHTS}
    dm = Dims(x, None, w_up, g_up, q_norm, kv_norm, ffn1_w_down.shape[1] * N_DEV)

    shard2d = {k: wts[k].reshape(wts[k].shape[-2], wts[k].shape[-1]) for k in MATRICES}
    sent = {k: shard2d[k].T if k in TRANSPOSED else shard2d[k] for k in MATRICES}
    early_rows = np.cumsum([0] + [sent[k].shape[0] for k in EARLY])
    late_rows = np.cumsum([0] + [sent[k].shape[0] for k in LATE])
    got_early = all_gather_two_level(jnp.concatenate([sent[k].astype(MMD) for k in EARLY], axis=0), "gather_early")
    full = {k: got_early[:, lo:hi].reshape(-1, dm.d) for k, lo, hi in zip(EARLY, early_rows[:-1], early_rows[1:])}
    late_shards = jnp.concatenate([sent[k].astype(MMD) for k in LATE], axis=0)
    send, offs = _pack([sent[k].astype(MMD).reshape(1, -1) for k in NARROW])
    got = all_gather(send[0], "gather_narrow").reshape(N_DEV, -1)
    for k, o in zip(NARROW, offs):
        r, c = sent[k].shape
        full[k] = got[:, o:o + r * c].reshape(N_DEV, r, c).transpose(1, 0, 2).reshape(r, N_DEV * c)
    mr, mc = meta_tokens.shape
    meta = all_gather(meta_tokens, "gather_meta").transpose(1, 0, 2).reshape(mr, N_DEV * mc)
    small = {k: wts[k].reshape(1, -1) for k in SMALL}

    loss, grad_x, dmeta, gw, gsum_late, gs = local_step(dm, x, loss_target, meta, full, late_shards, late_rows, small)

    gearly = jnp.concatenate([gw[k].reshape(N_DEV, sent[k].shape[0], dm.d) for k in EARLY], axis=1).astype(MMD)
    gsum_early = reduce_scatter_two_level(gearly, "scatter_early")
    grads = {}
    for names, rows, gsum_rows in ((EARLY, early_rows, gsum_early), (LATE, late_rows, gsum_late)):
        for k, lo, hi in zip(names, rows[:-1], rows[1:]):
            grads[k] = gsum_rows[lo:hi].T if k in TRANSPOSED else gsum_rows[lo:hi]

    def blocks(k, g):
        r, c = sent[k].shape
        return g.reshape(r, N_DEV, c).transpose(1, 0, 2).reshape(N_DEV, r * c)

    gsend, goffs = _pack([blocks(k, gw[k]).astype(MMD) for k in NARROW]
                         + [dmeta.reshape(mr, N_DEV, mc).transpose(1, 0, 2).reshape(N_DEV, mr * mc).astype(MMD)])
    gsum = sum_blocks(all_to_all(gsend, "scatter_narrow"), "sum_narrow").reshape(-1)
    for k, o in zip(NARROW, goffs):
        r, c = sent[k].shape
        grads[k] = gsum[o:o + r * c].reshape(r, c)
    grads["meta_tokens"] = gsum[goffs[-1]:goffs[-1] + mr * mc].reshape(mr, mc)

    ssend, soffs = _pack([gs[k].reshape(1, -1) for k in SMALL] + [loss])
    ssum = sum_blocks(all_gather(ssend[0], "gather_small"), "sum_small").reshape(-1)
    for k, o in zip(SMALL, soffs):
        grads[k] = ssum[o:o + small[k].shape[1]]
    loss_total = ssum[soffs[-1]]

    delta, new_m, new_v = {}, {}, {}
    for k in MATRICES + ("meta_tokens",):
        shp = wts[k].shape
        to2d = lambda a: a.reshape(shp[-2], shp[-1])
        dlt, nm, nv = adamw(to2d(wts[k]), grads[k], to2d(ms[k]), to2d(vs[k]), f"adamw_{k}")
        delta[k], new_m[k], new_v[k] = dlt.reshape(shp), nm.reshape(shp), nv.reshape(shp)
        grads[k] = grads[k].reshape(shp)
    pw, _ = _pack([wts[k].reshape(1, -1) for k in SMALL])
    pm_, _ = _pack([ms[k].reshape(1, -1) for k in SMALL])
    pv, _ = _pack([vs[k].reshape(1, -1) for k in SMALL])
    pg, poffs = _pack([grads[k].reshape(1, -1) for k in SMALL])
    dlt, nm, nv = adamw(pw[0], pg[0], pm_[0], pv[0], "adamw_small")
    for k, o in zip(SMALL, poffs):
        shp, sz = wts[k].shape, small[k].shape[1]
        cut = lambda a: a.reshape(-1)[o:o + sz].reshape(shp)
        delta[k], new_m[k], new_v[k] = cut(dlt), cut(nm), cut(nv)
        grads[k] = grads[k].reshape(shp)

    return (loss_total, grad_x, *[grads[k] for k in WEIGHTS], *[delta[k] for k in WEIGHTS],
            *[new_m[k] for k in WEIGHTS], *[new_v[k] for k in WEIGHTS])
```

```python
import functools

import numpy as np
import jax
import jax.numpy as jnp
from jax import lax
from jax.experimental import pallas as pl
from jax.experimental.pallas import tpu as pltpu

F32 = jnp.float32
BF16 = jnp.bfloat16
MMD = BF16

NORM_EPS = 1e-6
RWKV_HEAD = 64
GN_EPS = RWKV_HEAD * 1e-5
NOPE_DIM = 128
ROPE_DIM = 64
V_DIM = 128
QK_DIM = NOPE_DIM + ROPE_DIM
ROPE_THETA = 10000.0
ADAM_LR = 0.001
ADAM_B1 = 0.9
ADAM_B2 = 0.999
ADAM_EPS = 1e-08
ADAM_WD = 0.01
ADAM_STEP = 10

LANES = 128
SCAN_PAIRS = 8
SCAN_FWD_STEPS = 32
SCAN_BWD_STEPS = 16
N_DEV = 8
VMEM_LIMIT = 56 * 1024 * 1024
MESH = pl.DeviceIdType.MESH


def _tile(n, target, align):
    best = None
    for d in range(align, min(n, target) + 1, align):
        if n % d == 0:
            best = d
    return best if best is not None else n


def _params(sem=None):
    return pltpu.CompilerParams(dimension_semantics=sem, vmem_limit_bytes=VMEM_LIMIT)


def _mm(a, b, dims=((1,), (0,))):
    return lax.dot_general(a.astype(MMD), b.astype(MMD), (dims, ((), ())), preferred_element_type=F32)


@jax.custom_vjp
def mmdot(a, b):
    return _mm(a, b)


def _mmdot_fwd(a, b):
    return _mm(a, b), (a, b)


def _mmdot_bwd(res, g):
    a, b = res
    return _mm(g, b, ((1,), (1,))).astype(a.dtype), _mm(a, g, ((0,), (0,))).astype(b.dtype)


mmdot.defvjp(_mmdot_fwd, _mmdot_bwd)


def _dot2(x, m):
    hi = x.astype(BF16)
    lo = (x - hi.astype(F32)).astype(BF16)
    return (lax.dot_general(hi, m, (((1,), (0,)), ((), ())), preferred_element_type=F32)
            + lax.dot_general(lo, m, (((1,), (0,)), ((), ())), preferred_element_type=F32))


@jax.custom_vjp
def segsum(x, e, et):
    return _dot2(_dot2(x, e), et)


def _segsum_fwd(x, e, et):
    return segsum(x, e, et), (e, et)


def _segsum_bwd(res, g):
    e, et = res
    return segsum(g, e, et), jnp.zeros_like(e), jnp.zeros_like(et)


segsum.defvjp(_segsum_fwd, _segsum_bwd)


def _sigmoid(x):
    return 1.0 / (1.0 + jnp.exp(-x))


def _softplus(x):
    return jnp.maximum(x, 0.0) + jnp.log(1.0 + jnp.exp(-jnp.abs(x)))


def _rms(x, g):
    return x * lax.rsqrt(jnp.mean(x * x, axis=-1, keepdims=True) + NORM_EPS) * g


_DIMS = {"nn": ((1,), (0,)), "nt": ((1,), (1,)), "tn": ((0,), (0,))}


def matmul(pairs, mode, *, name, out_dtype=F32, res=None, alpha=1.0, tm=1088, tn=1024, tk=2048):
    a0, b0 = pairs[0]
    if mode == "nn":
        (m, k), n = a0.shape, b0.shape[1]
    elif mode == "nt":
        (m, k), n = a0.shape, b0.shape[0]
    else:
        (k, m), n = a0.shape, b0.shape[1]
    tm = _tile(m, tm, 128 if mode == "tn" else 16)
    tn = _tile(n, 2048 if mode == "tn" else tn, 128)
    tk = _tile(k, min(tk, 1024), 16) if mode == "tn" else _tile(k, tk, 128)
    nk = k // tk
    npair = len(pairs)
    if mode == "tn":
        a_spec = pl.BlockSpec((tk, tm), lambda i, j, kk: (kk, i))
    else:
        a_spec = pl.BlockSpec((tm, tk), lambda i, j, kk: (i, kk))
    if mode == "nt":
        b_spec = pl.BlockSpec((tn, tk), lambda i, j, kk: (j, kk))
    else:
        b_spec = pl.BlockSpec((tk, tn), lambda i, j, kk: (kk, j))
    o_spec = pl.BlockSpec((tm, tn), lambda i, j, kk: (i, j))
    dims = _DIMS[mode]

    def body(*refs):
        ab = refs[:2 * npair]
        res_ref = refs[2 * npair] if res is not None else None
        o_ref, acc_ref = refs[-2], refs[-1]
        kk = pl.program_id(2)

        @pl.when(kk == 0)
        def _():
            acc_ref[...] = jnp.zeros_like(acc_ref)

        part = _mm(ab[0][...], ab[1][...], dims)
        for p in range(1, npair):
            part = part + _mm(ab[2 * p][...], ab[2 * p + 1][...], dims)
        acc_ref[...] += part

        @pl.when(kk == nk - 1)
        def _():
            out = acc_ref[...] * alpha if alpha != 1.0 else acc_ref[...]
            if res_ref is not None:
                out = res_ref[...].astype(F32) + out
            o_ref[...] = out.astype(o_ref.dtype)

    args, specs = [], []
    for a, b in pairs:
        args += [a, b]
        specs += [a_spec, b_spec]
    if res is not None:
        args.append(res)
        specs.append(o_spec)
    return pl.pallas_call(
        body, grid=(m // tm, n // tn, nk), in_specs=specs, out_specs=o_spec,
        out_shape=jax.ShapeDtypeStruct((m, n), out_dtype), scratch_shapes=[pltpu.VMEM((tm, tn), F32)],
        compiler_params=_params(("parallel", "parallel", "arbitrary")), name=name)(*args)


def tilek(fn, ins, outs, *, n_rows, tr, name):
    tr = _tile(n_rows, tr, 16)
    n_in = len(ins)
    in_specs = []
    for arr, kind in ins:
        if kind == "r":
            in_specs.append(pl.BlockSpec((tr, arr.shape[1]), lambda i: (i, 0)))
        else:
            in_specs.append(pl.BlockSpec(arr.shape, lambda i, nd=arr.ndim: (0,) * nd))
    out_specs, out_shape = [], []
    has_acc = False
    for o in outs:
        if o[0] == "r":
            out_specs.append(pl.BlockSpec((tr, o[1]), lambda i: (i, 0)))
            out_shape.append(jax.ShapeDtypeStruct((n_rows, o[1]), o[2]))
        else:
            has_acc = True
            out_specs.append(pl.BlockSpec(o[1], lambda i, nd=len(o[1]): (0,) * nd))
            out_shape.append(jax.ShapeDtypeStruct(o[1], F32))

    def body(*refs):
        i = pl.program_id(0)
        vals = fn(*[r[...] for r in refs[:n_in]])
        for o, r, v in zip(outs, refs[n_in:], vals):
            if o[0] == "r":
                r[...] = v.astype(r.dtype)
            else:
                @pl.when(i == 0)
                def _(r=r):
                    r[...] = jnp.zeros_like(r)

                r[...] += v

    return pl.pallas_call(
        body, grid=(n_rows // tr,), in_specs=in_specs, out_specs=out_specs, out_shape=out_shape,
        compiler_params=_params(("arbitrary",) if has_acc else ("parallel",)), name=name)(*[a for a, _ in ins])


def rms_fwd(x, g, name):
    n, d = x.shape
    return tilek(lambda xv, gv: (_rms(xv, gv),), [(x, "r"), (g, "f")], [("r", d, MMD)], n_rows=n, tr=256, name=name)[0]


def rms_bwd(x, g, dy, dres, name):
    n, d = x.shape

    def fn(xv, gv, dyv, drv):
        _, vjp = jax.vjp(_rms, xv, gv)
        dx, dg = vjp(dyv.astype(F32))
        return drv + dx, dg

    return tilek(fn, [(x, "r"), (g, "f"), (dy, "r"), (dres, "r")], [("r", d, F32), ("acc", (1, d))],
                 n_rows=n, tr=128, name=name)


def loss_head(h, tgt, mask, g, name):
    n, d = h.shape

    def fn(hv, tv, mv, gv):
        def lossf(hh, gg):
            e = (_rms(hh, gg) - tv) * mv
            s = jnp.sum(jnp.sum(e * e, axis=1, keepdims=True), axis=0, keepdims=True)
            return s * (0.5 / d)

        l, vjp = jax.vjp(lossf, hv, gv)
        dh, dg = vjp(jnp.ones((1, 1), F32))
        return dh, dg, jnp.broadcast_to(l, (1, LANES))

    return tilek(fn, [(h, "r"), (tgt, "r"), (mask, "r"), (g, "f")],
                 [("r", d, F32), ("acc", (1, d)), ("acc", (1, LANES))], n_rows=n, tr=128, name=name)


def ffn_up(hn, wg, wu, name):
    n, d = hn.shape
    f = wg.shape[0]
    tm, tn = _tile(n, 544, 16), _tile(f, 512, 128)

    def body(a_ref, g_ref, u_ref, og_ref, ou_ref, oa_ref):
        a = a_ref[...]
        g = _mm(a, g_ref[...], ((1,), (1,)))
        u = _mm(a, u_ref[...], ((1,), (1,)))
        og_ref[...] = g
        ou_ref[...] = u
        oa_ref[...] = (g * _sigmoid(g) * u).astype(oa_ref.dtype)

    o_spec = pl.BlockSpec((tm, tn), lambda i, j: (i, j))
    w_spec = pl.BlockSpec((tn, d), lambda i, j: (j, 0))
    return pl.pallas_call(
        body, grid=(n // tm, f // tn), in_specs=[pl.BlockSpec((tm, d), lambda i, j: (i, 0)), w_spec, w_spec],
        out_specs=[o_spec, o_spec, o_spec],
        out_shape=[jax.ShapeDtypeStruct((n, f), F32), jax.ShapeDtypeStruct((n, f), F32), jax.ShapeDtypeStruct((n, f), MMD)],
        compiler_params=_params(("parallel", "parallel")), name=name)(hn, wg, wu)


def ffn_down_bwd(dh, wd, gate, up, name):
    n, d = dh.shape
    f = wd.shape[0]
    tm, tn = _tile(n, 544, 16), _tile(f, 1024, 128)

    def body(dh_ref, w_ref, g_ref, u_ref, dg_ref, du_ref):
        da = 0.5 * _mm(dh_ref[...], w_ref[...], ((1,), (1,)))
        g, u = g_ref[...], u_ref[...]
        s = _sigmoid(g)
        dg_ref[...] = (da * u * (s * (1.0 + g * (1.0 - s)))).astype(dg_ref.dtype)
        du_ref[...] = (da * (g * s)).astype(du_ref.dtype)

    o_spec = pl.BlockSpec((tm, tn), lambda i, j: (i, j))
    return pl.pallas_call(
        body, grid=(n // tm, f // tn),
        in_specs=[pl.BlockSpec((tm, d), lambda i, j: (i, 0)), pl.BlockSpec((tn, d), lambda i, j: (j, 0)), o_spec, o_spec],
        out_specs=[o_spec, o_spec],
        out_shape=[jax.ShapeDtypeStruct((n, f), MMD), jax.ShapeDtypeStruct((n, f), MMD)],
        compiler_params=_params(("parallel", "parallel")), name=name)(dh, wd, gate, up)


def ffn_forward(h, g, wg, wu, wd, tag):
    hn = rms_fwd(h, g, f"{tag}_rms")
    gate, up, act = ffn_up(hn, wg, wu, f"{tag}_up")
    out = matmul([(act, wd)], "nn", res=h, alpha=0.5, name=f"{tag}_down")
    return out, (hn, gate, up, act)


def ffn_backward(dout, h, g, wg, wu, wd, saved, tag):
    hn, gate, up, act = saved
    dgate, dup = ffn_down_bwd(dout, wd, gate, up, f"{tag}_dact")
    dwd = matmul([(act, dout)], "tn", alpha=0.5, out_dtype=MMD, name=f"{tag}_dwd")
    dwg = matmul([(dgate, hn)], "tn", out_dtype=MMD, name=f"{tag}_dwg")
    dwu = matmul([(dup, hn)], "tn", out_dtype=MMD, name=f"{tag}_dwu")
    dhn = matmul([(dgate, wg), (dup, wu)], "nn", name=f"{tag}_dhn")
    dh, dg = rms_bwd(h, g, dhn, dout, f"{tag}_drms")
    return dh, dg, dwg, dwu, dwd


def lerp_fwd(p, mu, bl, t, name):
    n, w = p.shape
    cb = _tile(w, 256, 128)

    def body(p_ref, mu_ref, o_ref):
        x = p_ref[...]
        row = lax.broadcasted_iota(jnp.int32, x.shape, 0)
        prev = jnp.where(row == 0, 0.0, pltpu.roll(x, 1, 0))
        o_ref[...] = x + mu_ref[...] * (prev - x)

    spec = pl.BlockSpec((t, cb), lambda b, j: (b, j))
    return pl.pallas_call(
        body, grid=(bl, w // cb), in_specs=[spec, pl.BlockSpec((1, cb), lambda b, j: (0, j))], out_specs=spec,
        out_shape=jax.ShapeDtypeStruct((n, w), F32), compiler_params=_params(("parallel", "parallel")), name=name)(p, mu)


def lerp_bwd(p, mu, douts, bl, t, name):
    n, w = p.shape
    cb = _tile(w, 256, 128)
    nd = len(douts)

    def body(*refs):
        p_ref, mu_ref = refs[0], refs[1]
        dp_ref, dmu_ref = refs[2 + nd], refs[3 + nd]
        b = pl.program_id(1)
        x, m = p_ref[...], mu_ref[...]
        d = refs[2][...]
        for r in refs[3:2 + nd]:
            d = d + r[...]
        row = lax.broadcasted_iota(jnp.int32, x.shape, 0)
        prev = jnp.where(row == 0, 0.0, pltpu.roll(x, 1, 0))
        z = d * m
        nxt = jnp.where(row == t - 1, 0.0, pltpu.roll(z, t - 1, 0))
        dp_ref[...] = d - z + nxt

        @pl.when(b == 0)
        def _():
            dmu_ref[...] = jnp.zeros_like(dmu_ref)

        dmu_ref[...] += jnp.sum(d * (prev - x), axis=0, keepdims=True)

    spec = pl.BlockSpec((t, cb), lambda j, b: (b, j))
    cspec = pl.BlockSpec((1, cb), lambda j, b: (0, j))
    return pl.pallas_call(
        body, grid=(w // cb, bl), in_specs=[spec, cspec] + [spec] * nd, out_specs=[spec, cspec],
        out_shape=[jax.ShapeDtypeStruct((n, w), F32), jax.ShapeDtypeStruct((1, w), F32)],
        compiler_params=_params(("parallel", "arbitrary")), name=name)(p, mu, *douts)


def _prep(k, xw, xa, xg, w0, a0, k_k, k_a, w_up, a_up, g_up, e, et):
    w_pre = -_softplus(-(w0 + mmdot(jnp.tanh(xw), w_up))) - 0.5
    decay = jnp.exp(-jnp.exp(w_pre))
    a = _sigmoid(a0 + mmdot(xa, a_up))
    g = mmdot(_sigmoid(xg), g_up)
    kk = k * k_k
    kk = kk * lax.rsqrt(jnp.maximum(segsum(kk * kk, e, et), 1e-24))
    kmod = k * (1.0 + (a - 1.0) * k_a)
    return decay, kmod, -kk, kk * a, g


def _lora_parts(xl):
    return xl[:, :LANES], xl[:, LANES:2 * LANES], xl[:, 2 * LANES:]


def rwkv_prep_fwd(pk, pl_, prm, e, et, name):
    n, d = pk.shape
    small = [prm[k] for k in ("w0", "a0", "k_k", "k_a", "w_up", "a_up", "g_up")]
    ins = [(pk, "r"), (pl_, "r")] + [(s, "f") for s in small] + [(e, "f"), (et, "f")]
    return tilek(lambda k, xl, *rest: _prep(k, *_lora_parts(xl), *rest), ins, [("r", d, F32)] * 5, n_rows=n, tr=128, name=name)


def rwkv_prep_bwd(pk, pl_, prm, e, et, cts, name):
    n, d = pk.shape
    small = [prm[k] for k in ("w0", "a0", "k_k", "k_a", "w_up", "a_up", "g_up")]

    def fn(k, xl, w0, a0, k_k, k_a, w_up, a_up, g_up, ev, etv, dw, dkm1, dkm2, dkn, db, dg):
        _, vjp = jax.vjp(lambda *a: _prep(*a, ev, etv), k, *_lora_parts(xl), w0, a0, k_k, k_a, w_up, a_up, g_up)
        dk, dxw, dxa, dxg, *dsmall = vjp((dw, dkm1 + dkm2, dkn, db, dg))
        return (dk, jnp.concatenate([dxw, dxa, dxg], axis=1), *dsmall)

    ins = [(pk, "r"), (pl_, "r")] + [(s, "f") for s in small] + [(e, "f"), (et, "f")] + [(c, "r") for c in cts]
    outs = [("r", d, F32), ("r", pl_.shape[1], F32)] + [("acc", s.shape) for s in small]
    return tilek(fn, ins, outs, n_rows=n, tr=64, name=name)


def _post(y, r, km, v, g, pga, pgb, yb, gn_w, gn_b, r_k, e, et):
    inv = 1.0 / RWKV_HEAD
    yc = y - segsum(y, e, et) * inv
    var = segsum(yc * yc, e, et) * inv
    yn = yc * lax.rsqrt(var + GN_EPS) * gn_w + gn_b
    bonus = segsum(r * km * r_k, e, et) * v
    ya = (yn + bonus) * g
    return _sigmoid(pga) * ya + _sigmoid(pgb) * yb


def rwkv_post_fwd(acts, prm, e, et, name):
    n, d = acts[0].shape
    small = [prm[k] for k in ("gn_w", "gn_b", "r_k")]
    ins = [(a, "r") for a in acts] + [(s, "f") for s in small] + [(e, "f"), (et, "f")]
    return tilek(lambda *a: (_post(*a),), ins, [("r", d, MMD)], n_rows=n, tr=128, name=name)[0]


def rwkv_post_bwd(acts, prm, e, et, dm, name):
    n, d = acts[0].shape
    small = [prm[k] for k in ("gn_w", "gn_b", "r_k")]
    na = len(acts)

    def fn(*a):
        prim, ev, etv, dmv = a[:na + 3], a[na + 3], a[na + 4], a[na + 5]
        _, vjp = jax.vjp(lambda *z: _post(*z, ev, etv), *prim)
        return vjp(dmv.astype(F32))

    ins = [(x, "r") for x in acts] + [(s, "f") for s in small] + [(e, "f"), (et, "f"), (dm, "r")]
    outs = [("r", d, F32)] * na + [("acc", s.shape) for s in small]
    return tilek(fn, ins, outs, n_rows=n, tr=64, name=name)


def _head_sums(x, first_head):
    a = jnp.sum(jnp.where(first_head, x, 0.0), axis=1, keepdims=True)
    b = jnp.sum(jnp.where(first_head, 0.0, x), axis=1, keepdims=True)
    return jnp.where(first_head, a, b)


def _round1(x):
    return (x.astype(BF16), None) if MMD == BF16 else _split2(x)


def _split2(x):
    hi = x.astype(BF16)
    return hi, (x - hi.astype(F32)).astype(BF16)


def _spread(row, eye2):
    hi, lo = _split2(row)
    return eye2 * hi, eye2 * lo


def _ones_dot(tiles, ones_blk):
    dims = (((1,), (0,)), ((), ()))
    res = lax.dot_general(jnp.concatenate([t[0] for t in tiles], axis=0), ones_blk, dims, preferred_element_type=F32)
    out = [res[i * RWKV_HEAD:(i + 1) * RWKV_HEAD] for i in range(len(tiles))]
    two_term = [i for i, t in enumerate(tiles) if t[1] is not None]
    if two_term:
        low = lax.dot_general(jnp.concatenate([tiles[i][1] for i in two_term], axis=0), ones_blk, dims,
                              preferred_element_type=F32)
        for n, i in enumerate(two_term):
            out[i] = out[i] + low[n * RWKV_HEAD:(n + 1) * RWKV_HEAD]
    return out


def _scan_consts():
    lane = lax.broadcasted_iota(jnp.int32, (1, LANES), 1)
    rows = lax.broadcasted_iota(jnp.int32, (RWKV_HEAD, LANES), 0)
    cols = lax.broadcasted_iota(jnp.int32, (RWKV_HEAD, LANES), 1)
    eye2 = ((cols & (RWKV_HEAD - 1)) == rows).astype(BF16)
    r2 = lax.broadcasted_iota(jnp.int32, (LANES, LANES), 0)
    c2 = lax.broadcasted_iota(jnp.int32, (LANES, LANES), 1)
    ones_blk = ((r2 // RWKV_HEAD) == (c2 // RWKV_HEAD)).astype(BF16)
    return lane, lane < RWKV_HEAD, eye2, ones_blk


def _riding_exchange(src_ref, dst_ref, send_sems, recv_sems, local_sem, scatter, grid):
    def copies():
        _, me = _peer(0)
        out = [pltpu.make_async_copy(src_ref.at[me] if scatter else src_ref, dst_ref.at[me], local_sem)]
        for k in range(1, N_DEV):
            dev, idx = _peer(k)
            out.append(pltpu.make_async_remote_copy(src_ref=src_ref.at[idx] if scatter else src_ref, dst_ref=dst_ref.at[me],
                                                    send_sem=send_sems.at[k - 1], recv_sem=recv_sems.at[k - 1],
                                                    device_id=dev, device_id_type=MESH))
        return out

    ids = [pl.program_id(a) for a in range(len(grid))]
    first = functools.reduce(jnp.logical_and, [i == 0 for i in ids])
    last = functools.reduce(jnp.logical_and, [i == n - 1 for i, n in zip(ids, grid)])

    def start():
        @pl.when(first)
        def _():
            for cp in copies():
                cp.start()

    def finish():
        @pl.when(last)
        def _():
            for cp in copies():
                cp.wait()

    return start, finish


_RIDE_SCRATCH = [pltpu.SemaphoreType.DMA((N_DEV - 1,)), pltpu.SemaphoreType.DMA((N_DEV - 1,)), pltpu.SemaphoreType.DMA]


def scan_forward(r, w, k, kn, b, v, ride, bl, t, t_real, d, name, pg, hch):
    npair, nst = d // LANES, t // hch
    grid = (bl, npair // pg, nst)

    def body(r_ref, w_ref, k_ref, kn_ref, b_ref, v_ref, ride_ref, y_ref, hist_ref, land_ref, s_ref, vb_ref, *sems):
        start, finish = _riding_exchange(ride_ref, land_ref, *sems, False, grid)
        start()
        _, first_head, eye2, ones_blk = _scan_consts()
        eye2f = eye2.astype(F32)
        diag = lambda tile: jnp.sum(tile * eye2f, axis=0, keepdims=True)

        @pl.when(pl.program_id(2) == 0)
        def _():
            s_ref[...] = jnp.zeros_like(s_ref)

        pair_cols = [slice(p * LANES, (p + 1) * LANES) for p in range(pg)]
        for p, tile in enumerate(_ones_dot([_spread(v_ref[0, :, cols], eye2) for cols in pair_cols], ones_blk)):
            vb_ref[p] = tile

        def step(ts, carry):
            prev, nxt = jnp.maximum(ts - 1, 0), jnp.minimum(ts + 1, hch - 1)
            states, tiles = [], []
            for p in range(pg):
                cols = slice(p * LANES, (p + 1) * LANES)
                s = s_ref[p]
                hist_ref[0, p, pl.ds(ts, 1)] = s[None]
                states.append(s)
                tiles.append(_round1(s * r_ref[prev, :, cols]))
                tiles.append(_spread(v_ref[nxt, :, cols], eye2))
            res = _ones_dot(tiles, ones_blk)
            for p in range(pg):
                cols = slice(p * LANES, (p + 1) * LANES)
                s = states[p]
                sa = _head_sums(s * kn_ref[ts, :, cols], first_head)
                s_ref[p] = s * w_ref[ts, :, cols] + sa * b_ref[ts, :, cols] + vb_ref[p] * k_ref[ts, :, cols]
            for p in range(pg):
                cols = slice(p * LANES, (p + 1) * LANES)
                y_ref[prev, :, cols] = diag(res[2 * p])
                vb_ref[p] = res[2 * p + 1]
            return carry

        real = pl.program_id(2) * hch < t_real
        lax.fori_loop(0, jnp.where(real, hch, 0), step, 0)
        last = _ones_dot([_round1(s_ref[p] * r_ref[hch - 1, :, cols]) for p, cols in enumerate(pair_cols)], ones_blk)
        for p, cols in enumerate(pair_cols):
            y_ref[hch - 1, :, cols] = diag(last[p])

        @pl.when(jnp.logical_not(real))
        def _():
            y_ref[...] = jnp.zeros_like(y_ref)
            hist_ref[...] = jnp.zeros_like(hist_ref)

        finish()

    row_spec = pl.BlockSpec((hch, 1, pg * LANES), lambda bb, g, c: (bb * nst + c, 0, g))
    hist_spec = pl.BlockSpec((1, pg, hch, RWKV_HEAD, LANES), lambda bb, g, c: (bb, g, c, 0, 0))
    hbm = pl.BlockSpec(memory_space=pl.ANY)
    rows3 = [a.reshape(bl * t, 1, d) for a in (r, w, k, kn, b, v)]
    y, hist, landed = pl.pallas_call(
        body, grid=grid, in_specs=[row_spec] * 6 + [hbm], out_specs=[row_spec, hist_spec, hbm],
        out_shape=[jax.ShapeDtypeStruct((bl * t, 1, d), F32), jax.ShapeDtypeStruct((bl, npair, t, RWKV_HEAD, LANES), F32),
                   jax.ShapeDtypeStruct((N_DEV,) + ride.shape, ride.dtype)],
        scratch_shapes=[pltpu.VMEM((pg, RWKV_HEAD, LANES), F32)] * 2 + _RIDE_SCRATCH,
        compiler_params=_params(("arbitrary", "arbitrary", "arbitrary")), name=name)(*rows3, ride)
    return y.reshape(bl * t, d), hist, landed


def scan_backward(r, w, k, kn, b, v, dy, hist, ride, bl, t, t_real, d, name, pg, hch):
    npair, nst = d // LANES, t // hch
    grid = (bl, npair // pg, nst)

    def body(r_ref, w_ref, k_ref, kn_ref, b_ref, v_ref, dy_ref, hist_ref, ride_ref,
             dr_ref, dw_ref, dk_ref, dkn_ref, db_ref, dv_ref, land_ref, ds_ref, cur_ref, *sems):
        start, finish = _riding_exchange(ride_ref, land_ref, *sems, True, grid)
        start()
        _, first_head, eye2, ones_blk = _scan_consts()
        eye2f = eye2.astype(F32)
        colsum = lambda x: jnp.sum(x, axis=0, keepdims=True)

        @pl.when(pl.program_id(2) == 0)
        def _():
            ds_ref[...] = jnp.zeros_like(ds_ref)

        tiles = []
        for p in range(pg):
            cols = slice(p * LANES, (p + 1) * LANES)
            tiles += [_spread(v_ref[hch - 1, :, cols], eye2), _spread(dy_ref[hch - 1, :, cols], eye2),
                      _split2(hist_ref[0, p, hch - 1] * kn_ref[hch - 1, :, cols])]
        first = _ones_dot(tiles, ones_blk)
        for p in range(pg):
            cols = slice(p * LANES, (p + 1) * LANES)
            row = lambda ref: ref[hch - 1, :, cols]
            s_prev = hist_ref[0, p, hch - 1]
            vb, dyb, sa = first[3 * p], first[3 * p + 1], first[3 * p + 2]
            cur_ref[0, p], cur_ref[1, p] = vb, sa
            dr_ref[hch - 1, :, cols] = colsum((s_prev * row(w_ref) + sa * row(b_ref) + vb * row(k_ref)) * dyb)
            ds_ref[p] += dyb * row(r_ref)

        def step(it, carry):
            ts = hch - 1 - it
            prev = jnp.maximum(ts - 1, 0)
            has_prev = ts > 0
            grads, tiles = [], []
            for p in range(pg):
                cols = slice(p * LANES, (p + 1) * LANES)
                ds = ds_ref[p]
                grads.append(ds)
                tiles.append(_spread(v_ref[prev, :, cols], eye2))
                tiles.append(_spread(dy_ref[prev, :, cols], eye2))
                tiles.append(_split2(hist_ref[0, p, pl.ds(prev, 1)][0] * kn_ref[prev, :, cols]))
                tiles.append(_round1(ds * k_ref[ts, :, cols]))
            res = _ones_dot(tiles, ones_blk)
            for p in range(pg):
                cols = slice(p * LANES, (p + 1) * LANES)
                row = lambda ref: ref[ts, :, cols]
                ds = grads[p]
                w_, kn_, b_ = row(w_ref), row(kn_ref), row(b_ref)
                dsa = _head_sums(ds * b_, first_head)
                s_prev = hist_ref[0, p, pl.ds(ts, 1)][0]
                vb, sa, dyb_prev = cur_ref[0, p], cur_ref[1, p], res[4 * p + 1]
                dk_ref[ts, :, cols] = colsum(ds * vb)
                db_ref[ts, :, cols] = colsum(ds * sa)
                dw_ref[ts, :, cols] = colsum(ds * s_prev)
                dkn_ref[ts, :, cols] = colsum(s_prev * dsa)
                dv_ref[ts, :, cols] = colsum(res[4 * p + 3] * eye2f)
                dr_ref[prev, :, cols] = jnp.where(has_prev, colsum(s_prev * dyb_prev), dr_ref[prev, :, cols])
                ds_ref[p] = ds * w_ + dsa * kn_ + jnp.where(has_prev, dyb_prev, 0.0) * r_ref[prev, :, cols]
            for p in range(pg):
                cur_ref[0, p] = res[4 * p]
                cur_ref[1, p] = res[4 * p + 2]
            return carry

        real = (nst - 1 - pl.program_id(2)) * hch < t_real
        lax.fori_loop(0, jnp.where(real, hch, 0), step, 0)

        @pl.when(jnp.logical_not(real))
        def _():
            for ref in (dr_ref, dw_ref, dk_ref, dkn_ref, db_ref, dv_ref):
                ref[...] = jnp.zeros_like(ref)

        finish()

    row_spec = pl.BlockSpec((hch, 1, pg * LANES), lambda bb, g, c: (bb * nst + nst - 1 - c, 0, g))
    hist_spec = pl.BlockSpec((1, pg, hch, RWKV_HEAD, LANES), lambda bb, g, c: (bb, g, nst - 1 - c, 0, 0))
    hbm = pl.BlockSpec(memory_space=pl.ANY)
    row_shape = jax.ShapeDtypeStruct((bl * t, 1, d), F32)
    rows3 = [a.reshape(bl * t, 1, d) for a in (r, w, k, kn, b, v, dy)]
    outs = pl.pallas_call(
        body, grid=grid, in_specs=[row_spec] * 7 + [hist_spec, hbm], out_specs=[row_spec] * 6 + [hbm],
        out_shape=[row_shape] * 6 + [jax.ShapeDtypeStruct(ride.shape, ride.dtype)],
        scratch_shapes=[pltpu.VMEM((pg, RWKV_HEAD, LANES), F32), pltpu.VMEM((2, pg, RWKV_HEAD, LANES), F32)] + _RIDE_SCRATCH,
        compiler_params=_params(("arbitrary", "arbitrary", "arbitrary")), name=name)(*rows3, hist, ride)
    return [o.reshape(bl * t, d) for o in outs[:6]] + [outs[6]]


def _mla_norms(pm, gq, gkv):
    ql = gq.shape[1]
    kvl = gkv.shape[1]
    return _rms(pm[:, :ql], gq), _rms(pm[:, ql:ql + kvl], gkv)


def mla_prep_fwd(pm, gq, gkv, name):
    n = pm.shape[0]
    return tilek(_mla_norms, [(pm, "r"), (gq, "f"), (gkv, "f")],
                 [("r", gq.shape[1], MMD), ("r", gkv.shape[1], MMD)], n_rows=n, tr=256, name=name)


def mla_prep_bwd(pm, gq, gkv, dcq, dckv, dkpe, name):
    n, wm = pm.shape
    ql, kvl = gq.shape[1], gkv.shape[1]

    def fn(pmv, gqv, gkvv, d1, d2, d3):
        _, vjp1 = jax.vjp(_rms, pmv[:, :ql], gqv)
        _, vjp2 = jax.vjp(_rms, pmv[:, ql:ql + kvl], gkvv)
        dcq_in, dgq = vjp1(d1)
        dckv_in, dgkv = vjp2(d2)
        return jnp.concatenate([dcq_in, dckv_in, d3], axis=1), dgq, dgkv

    return tilek(fn, [(pm, "r"), (gq, "f"), (gkv, "f"), (dcq, "r"), (dckv, "r"), (dkpe, "r")],
                 [("r", wm, F32), ("acc", gq.shape), ("acc", gkv.shape)], n_rows=n, tr=128, name=name)


def _rope(x, c, s, first):
    sw = jnp.where(first, pltpu.roll(x, LANES - ROPE_DIM // 2, 1), pltpu.roll(x, ROPE_DIM // 2, 1))
    return x * c + sw * s


def _unrope(d, c, s, first):
    z = d * s
    sw = jnp.where(first, pltpu.roll(z, LANES - ROPE_DIM // 2, 1), pltpu.roll(z, ROPE_DIM // 2, 1))
    return d * c + sw


def _causal_segments(n_tiles, parts=4):
    bounds = sorted({round(n_tiles * s / parts) for s in range(parts + 1)})
    return list(zip(bounds[:-1], bounds[1:]))


def attn_fwd(q, kv, pm, ct, st, bl, t, hm, name):
    n = q.shape[0]
    tq = LANES
    scale = QK_DIM ** -0.5
    kpe_blk = pm.shape[1] // LANES - 1

    def body(qn_ref, qpe_ref, kn_ref, v_ref, kpe_ref, ct_ref, st_ref, o_ref, lse_ref, kp_s, kn_s, v_s):
        h = pl.program_id(1)
        lane = lax.broadcasted_iota(jnp.int32, (1, LANES), 1)
        first = (lane & (ROPE_DIM - 1)) < ROPE_DIM // 2
        kp = _rope(kpe_ref[...], ct_ref[...], st_ref[...], first)
        kp_s[...] = jnp.where(h % 2 == 0, kp, pltpu.roll(kp, ROPE_DIM, 1)).astype(MMD)
        kn_s[...] = kn_ref[...].astype(MMD)
        v_s[...] = v_ref[...].astype(MMD)
        def segment(lo, hi):
            ext = hi * tq
            kpos = lax.broadcasted_iota(jnp.int32, (1, ext), 1)

            def qtile(i, carry):
                rows = pl.ds(pl.multiple_of(i * tq, tq), tq)
                q2 = _rope(qpe_ref[rows, :], ct_ref[rows, :], st_ref[rows, :], first)
                s = (_mm(qn_ref[rows, :], kn_s[:ext, :], ((1,), (1,))) + _mm(q2, kp_s[:ext, :], ((1,), (1,)))) * scale
                qpos = i * tq + lax.broadcasted_iota(jnp.int32, (tq, 1), 0)
                s = jnp.where(kpos <= qpos, s, -1e30)
                m = jnp.max(s, axis=1, keepdims=True)
                p = jnp.exp(s - m)
                l = jnp.sum(p, axis=1, keepdims=True)
                o_ref[rows, :] = _mm(p, v_s[:ext, :]) / l
                lse_ref[0, 0, rows, :] = m + jnp.log(l)
                return carry

            lax.fori_loop(lo, hi, qtile, 0)

        for lo, hi in _causal_segments(t // tq):
            segment(lo, hi)

    blk = lambda f: pl.BlockSpec((t, LANES), f)
    return pl.pallas_call(
        body, grid=(bl, hm),
        in_specs=[blk(lambda b, h: (b, h)), blk(lambda b, h: (b, hm + h // 2)), blk(lambda b, h: (b, h)),
                  blk(lambda b, h: (b, hm + h)), blk(lambda b, h: (b, kpe_blk)), blk(lambda b, h: (0, 0)), blk(lambda b, h: (0, 0))],
        out_specs=[blk(lambda b, h: (b, h)), pl.BlockSpec((1, 1, t, 1), lambda b, h: (b, h, 0, 0))],
        out_shape=[jax.ShapeDtypeStruct((n, hm * LANES), F32), jax.ShapeDtypeStruct((bl, hm, t, 1), F32)],
        scratch_shapes=[pltpu.VMEM((t, LANES), MMD)] * 3,
        compiler_params=_params(("parallel", "arbitrary")), name=name)(q, q, kv, kv, pm, ct, st)


def attn_bwd(q, kv, pm, o, do, lse, ct, st, bl, t, hm, name):
    n = q.shape[0]
    tq = LANES
    scale = QK_DIM ** -0.5
    kpe_blk = pm.shape[1] // LANES - 1

    def body(qn_ref, qpe_ref, kn_ref, v_ref, kpe_ref, o_ref, do_ref, lse_ref, ct_ref, st_ref,
             dqn_ref, dqpe_ref, dkn_ref, dv_ref, dkpe_ref, kp_s, kn_s, v_s, dkn_s, dkp_s, dv_s):
        h = pl.program_id(1)
        lane = lax.broadcasted_iota(jnp.int32, (1, LANES), 1)
        first = (lane & (ROPE_DIM - 1)) < ROPE_DIM // 2
        mine = (lane // ROPE_DIM) == (h % 2)
        kp = _rope(kpe_ref[...], ct_ref[...], st_ref[...], first)
        kp_s[...] = jnp.where(h % 2 == 0, kp, pltpu.roll(kp, ROPE_DIM, 1)).astype(MMD)
        kn_s[...] = kn_ref[...].astype(MMD)
        v_s[...] = v_ref[...].astype(MMD)
        dkn_s[...] = jnp.zeros_like(dkn_s)
        dkp_s[...] = jnp.zeros_like(dkp_s)
        dv_s[...] = jnp.zeros_like(dv_s)
        @pl.when(h % 2 == 0)
        def _():
            dqpe_ref[...] = jnp.zeros_like(dqpe_ref)

        @pl.when(h == 0)
        def _():
            dkpe_ref[...] = jnp.zeros_like(dkpe_ref)

        def segment(lo, hi):
            ext = hi * tq
            kpos = lax.broadcasted_iota(jnp.int32, (1, ext), 1)

            def qtile(i, carry):
                rows = pl.ds(pl.multiple_of(i * tq, tq), tq)
                c_i, s_i = ct_ref[rows, :], st_ref[rows, :]
                q1 = qn_ref[rows, :].astype(MMD)
                q2 = _rope(qpe_ref[rows, :], c_i, s_i, first).astype(MMD)
                s = (_mm(q1, kn_s[:ext, :], ((1,), (1,))) + _mm(q2, kp_s[:ext, :], ((1,), (1,)))) * scale
                qpos = i * tq + lax.broadcasted_iota(jnp.int32, (tq, 1), 0)
                p = jnp.where(kpos <= qpos, jnp.exp(s - lse_ref[0, 0, rows, :]), 0.0)
                do_i = do_ref[rows, :]
                delta = jnp.sum(do_i * o_ref[rows, :], axis=1, keepdims=True)
                dp = _mm(do_i, v_s[:ext, :], ((1,), (1,)))
                ds = (p * (dp - delta) * scale).astype(MMD)
                dqn_ref[rows, :] = _mm(ds, kn_s[:ext, :])
                dq2 = jnp.where(mine, _mm(ds, kp_s[:ext, :]), 0.0)
                dqpe_ref[rows, :] += _unrope(dq2, c_i, s_i, first)
                dkn_s[:ext, :] += _mm(ds, q1, ((0,), (0,)))
                dkp_s[:ext, :] += _mm(ds, q2, ((0,), (0,)))
                dv_s[:ext, :] += _mm(p, do_i, ((0,), (0,)))
                return carry

            lax.fori_loop(lo, hi, qtile, 0)

        for lo, hi in _causal_segments(t // tq):
            segment(lo, hi)
        dkn_ref[...] = dkn_s[...]
        dv_ref[...] = dv_s[...]
        dkp = jnp.where(mine, dkp_s[...], 0.0)
        dkp = jnp.where(h % 2 == 0, dkp, pltpu.roll(dkp, ROPE_DIM, 1))
        dkpe_ref[...] += _unrope(dkp, ct_ref[...], st_ref[...], first)

    blk = lambda f: pl.BlockSpec((t, LANES), f)
    hd = lambda b, h: (b, h)
    shp = lambda wd: jax.ShapeDtypeStruct((n, wd), F32)
    return pl.pallas_call(
        body, grid=(bl, hm),
        in_specs=[blk(hd), blk(lambda b, h: (b, hm + h // 2)), blk(hd), blk(lambda b, h: (b, hm + h)),
                  blk(lambda b, h: (b, kpe_blk)), blk(hd), blk(hd), pl.BlockSpec((1, 1, t, 1), lambda b, h: (b, h, 0, 0)),
                  blk(lambda b, h: (0, 0)), blk(lambda b, h: (0, 0))],
        out_specs=[blk(hd), blk(lambda b, h: (b, h // 2)), blk(hd), blk(hd), blk(lambda b, h: (b, 0))],
        out_shape=[shp(hm * LANES), shp(hm * ROPE_DIM), shp(hm * LANES), shp(hm * LANES), shp(LANES)],
        scratch_shapes=[pltpu.VMEM((t, LANES), MMD)] * 3 + [pltpu.VMEM((t, LANES), F32)] * 3,
        compiler_params=_params(("parallel", "arbitrary")), name=name)(q, q, kv, kv, pm, o, do, lse, ct, st)


def _peer(k):
    mx, my, mc = lax.axis_index("x"), lax.axis_index("y"), lax.axis_index("c")
    px = 1 - mx if k & 4 else mx
    py = 1 - my if k & 2 else my
    pc = 1 - mc if k & 1 else mc
    return (px, py, pc), 4 * px + 2 * py + pc


def all_gather(x, name):
    def body(x_ref, o_ref, send_sems, recv_sems, local_sem):
        _, me = _peer(0)
        local = pltpu.make_async_copy(x_ref, o_ref.at[me], local_sem)
        local.start()
        copies = []
        for k in range(1, N_DEV):
            dev, _ = _peer(k)
            cp = pltpu.make_async_remote_copy(src_ref=x_ref, dst_ref=o_ref.at[me], send_sem=send_sems.at[k - 1],
                                              recv_sem=recv_sems.at[k - 1], device_id=dev, device_id_type=MESH)
            cp.start()
            copies.append(cp)
        for cp in copies:
            cp.wait()
        local.wait()

    return pl.pallas_call(
        body, in_specs=[pl.BlockSpec(memory_space=pl.ANY)], out_specs=pl.BlockSpec(memory_space=pl.ANY),
        out_shape=jax.ShapeDtypeStruct((N_DEV,) + x.shape, x.dtype),
        scratch_shapes=[pltpu.SemaphoreType.DMA((N_DEV - 1,)), pltpu.SemaphoreType.DMA((N_DEV - 1,)), pltpu.SemaphoreType.DMA],
        name=name)(x)


def all_to_all(x, name):
    def body(x_ref, o_ref, send_sems, recv_sems, local_sem):
        _, me = _peer(0)
        local = pltpu.make_async_copy(x_ref.at[me], o_ref.at[me], local_sem)
        local.start()
        copies = []
        for k in range(1, N_DEV):
            dev, idx = _peer(k)
            cp = pltpu.make_async_remote_copy(src_ref=x_ref.at[idx], dst_ref=o_ref.at[me], send_sem=send_sems.at[k - 1],
                                              recv_sem=recv_sems.at[k - 1], device_id=dev, device_id_type=MESH)
            cp.start()
            copies.append(cp)
        for cp in copies:
            cp.wait()
        local.wait()

    return pl.pallas_call(
        body, in_specs=[pl.BlockSpec(memory_space=pl.ANY)], out_specs=pl.BlockSpec(memory_space=pl.ANY),
        out_shape=jax.ShapeDtypeStruct(x.shape, x.dtype),
        scratch_shapes=[pltpu.SemaphoreType.DMA((N_DEV - 1,)), pltpu.SemaphoreType.DMA((N_DEV - 1,)), pltpu.SemaphoreType.DMA],
        name=name)(x)


def _chips():
    mx, my, mc = lax.axis_index("x"), lax.axis_index("y"), lax.axis_index("c")
    return (mx, my, mc), (mx, my, 1 - mc), [(1 - mx, my), (mx, 1 - my), (1 - mx, 1 - my)]


def all_gather_two_level(x, name):
    def body(x_ref, o_ref, send_sems, recv_sems, local_sem):
        me, sibling, chips = _chips()
        blk = lambda px, py, pc: o_ref.at[4 * px + 2 * py + pc]

        def copy(k, block, to, src=None):
            return pltpu.make_async_remote_copy(src_ref=blk(*block) if src is None else src, dst_ref=blk(*block),
                                                send_sem=send_sems.at[k], recv_sem=recv_sems.at[k], device_id=to,
                                                device_id_type=MESH)

        mine = pltpu.make_async_copy(x_ref, blk(*me), local_sem)
        mine.start()
        first = [copy(0, me, sibling, src=x_ref)] + [copy(1 + j, me, (*chip, me[2]), src=x_ref) for j, chip in enumerate(chips)]
        for cp in first:
            cp.start()
        passed = [copy(4 + j, (*chip, me[2]), sibling) for j, chip in enumerate(chips)]
        for j, chip in enumerate(chips):
            copy(1 + j, (*chip, me[2]), me).wait_recv()
            passed[j].start()
        copy(0, sibling, me).wait_recv()
        for j, chip in enumerate(chips):
            copy(4 + j, (*chip, 1 - me[2]), me).wait_recv()
        for cp in first + passed:
            cp.wait_send()
        mine.wait()

    return pl.pallas_call(
        body, in_specs=[pl.BlockSpec(memory_space=pl.ANY)], out_specs=pl.BlockSpec(memory_space=pl.ANY),
        out_shape=jax.ShapeDtypeStruct((N_DEV,) + x.shape, x.dtype),
        scratch_shapes=[pltpu.SemaphoreType.DMA((N_DEV - 1,)), pltpu.SemaphoreType.DMA((N_DEV - 1,)), pltpu.SemaphoreType.DMA],
        name=name)(x)


def exchange_sibling(x, name):
    def body(x_ref, o_ref, send_sems, recv_sems):
        me, sibling, _ = _chips()
        copies = []
        for q in range(N_DEV // 2):
            cp = pltpu.make_async_remote_copy(src_ref=x_ref.at[2 * q + 1 - me[2]], dst_ref=o_ref.at[q], send_sem=send_sems.at[q],
                                              recv_sem=recv_sems.at[q], device_id=sibling, device_id_type=MESH)
            cp.start()
            copies.append(cp)
        for cp in copies:
            cp.wait()

    return pl.pallas_call(
        body, in_specs=[pl.BlockSpec(memory_space=pl.ANY)], out_specs=pl.BlockSpec(memory_space=pl.ANY),
        out_shape=jax.ShapeDtypeStruct((N_DEV // 2,) + x.shape[1:], x.dtype),
        scratch_shapes=[pltpu.SemaphoreType.DMA((N_DEV // 2,)), pltpu.SemaphoreType.DMA((N_DEV // 2,))], name=name)(x)


def exchange_chips(x, name):
    def body(x_ref, o_ref, send_sems, recv_sems, local_sem):
        me, _, chips = _chips()
        here = 2 * me[0] + me[1]
        local = pltpu.make_async_copy(x_ref.at[here], o_ref.at[here], local_sem)
        local.start()
        copies = []
        for j, (px, py) in enumerate(chips):
            cp = pltpu.make_async_remote_copy(src_ref=x_ref.at[2 * px + py], dst_ref=o_ref.at[here], send_sem=send_sems.at[j],
                                              recv_sem=recv_sems.at[j], device_id=(px, py, me[2]), device_id_type=MESH)
            cp.start()
            copies.append(cp)
        for cp in copies:
            cp.wait()
        local.wait()

    return pl.pallas_call(
        body, in_specs=[pl.BlockSpec(memory_space=pl.ANY)], out_specs=pl.BlockSpec(memory_space=pl.ANY),
        out_shape=jax.ShapeDtypeStruct(x.shape, x.dtype),
        scratch_shapes=[pltpu.SemaphoreType.DMA((3,)), pltpu.SemaphoreType.DMA((3,)), pltpu.SemaphoreType.DMA], name=name)(x)


def add_blocks(a, b, name):
    q, r, c = a.shape
    tr = _tile(r, max(16, (2 << 20) // (c * a.dtype.itemsize)), 16)
    spec = pl.BlockSpec((1, tr, c), lambda i, j: (i, j, 0))

    def body(a_ref, b_ref, o_ref):
        o_ref[...] = (a_ref[...].astype(F32) + b_ref[...].astype(F32)).astype(o_ref.dtype)

    return pl.pallas_call(
        body, grid=(q, r // tr), in_specs=[spec, spec], out_specs=spec, out_shape=jax.ShapeDtypeStruct(a.shape, a.dtype),
        compiler_params=_params(("parallel", "parallel")), name=name)(a, b)


def reduce_scatter_two_level(x, tag):
    q = N_DEV // 2
    from_sibling = exchange_sibling(x, f"{tag}_sibling")
    mine = lax.dynamic_index_in_dim(x.reshape((q, 2) + x.shape[1:]), lax.axis_index("c"), axis=1, keepdims=False)
    chip_sums = add_blocks(mine, from_sibling, f"{tag}_pair_sum")
    return sum_blocks(exchange_chips(chip_sums, f"{tag}_chips"), f"{tag}_sum")


def sum_blocks(x, name):
    nb, r, c = x.shape
    tr = _tile(r, max(16, (4 << 20) // (nb * c * x.dtype.itemsize)), 16)

    def body(x_ref, o_ref):
        acc = x_ref[0].astype(F32)
        for i in range(1, nb):
            acc = acc + x_ref[i].astype(F32)
        o_ref[...] = acc

    return pl.pallas_call(
        body, grid=(r // tr,), in_specs=[pl.BlockSpec((nb, tr, c), lambda i: (0, i, 0))],
        out_specs=pl.BlockSpec((tr, c), lambda i: (i, 0)), out_shape=jax.ShapeDtypeStruct((r, c), F32),
        compiler_params=_params(("parallel",)), name=name)(x)


def _adamw(w, g, m, v):
    m = ADAM_B1 * m + (1.0 - ADAM_B1) * g
    v = ADAM_B2 * v + (1.0 - ADAM_B2) * jnp.square(g)
    m_hat = m / (1.0 - ADAM_B1 ** ADAM_STEP)
    v_hat = v / (1.0 - ADAM_B2 ** ADAM_STEP)
    delta = -ADAM_LR * (m_hat / (jnp.sqrt(v_hat) + ADAM_EPS) + ADAM_WD * w)
    return delta, m, v


def adamw(w, g, m, v, name):
    r, c = w.shape
    tr = _tile(r, 256, 8)
    spec = pl.BlockSpec((tr, c), lambda i: (i, 0))

    def body(w_ref, g_ref, m_ref, v_ref, d_ref, nm_ref, nv_ref):
        d_ref[...], nm_ref[...], nv_ref[...] = _adamw(w_ref[...], g_ref[...], m_ref[...], v_ref[...])

    return pl.pallas_call(
        body, grid=(r // tr,), in_specs=[spec] * 4, out_specs=[spec] * 3,
        out_shape=[jax.ShapeDtypeStruct((r, c), F32)] * 3, compiler_params=_params(("parallel",)), name=name)(w, g, m, v)


def batch_sum_rows(dh, bl, t, rows, name):
    d = dh.shape[1]

    def body(x_ref, o_ref):
        @pl.when(pl.program_id(0) == 0)
        def _():
            o_ref[...] = jnp.zeros_like(o_ref)

        o_ref[...] += x_ref[...]

    return pl.pallas_call(
        body, grid=(bl,), in_specs=[pl.BlockSpec((rows, d), lambda b: (b * (t // rows), 0))],
        out_specs=pl.BlockSpec((rows, d), lambda b: (0, 0)), out_shape=jax.ShapeDtypeStruct((rows, d), F32),
        compiler_params=_params(("arbitrary",)), name=name)(dh)


class Dims:
    def __init__(self, x, meta_full_cols, w_up, g_up, q_norm, kv_norm, d_ff):
        self.bl, self.seq, self.d = x.shape
        self.n_meta = 16
        self.t_real = self.n_meta + self.seq
        self.t = -(-self.t_real // LANES) * LANES
        self.n = self.bl * self.t
        self.f = d_ff
        self.wl, self.gl = w_up.shape[-2], g_up.shape[-2]
        self.ql, self.kvl = q_norm.shape[-1], kv_norm.shape[-1]
        self.hm = self.d // V_DIM
        self.in_cols = 5 * self.d + 2 * self.wl + self.gl + self.ql + self.kvl + ROPE_DIM


def _pad_cols(a, width):
    return jnp.pad(a, ((0, 0), (0, width - a.shape[1])))


def _pad_rows(a, rows):
    return jnp.pad(a, ((0, rows - a.shape[0]), (0, 0)))


def split_in(a, dm, axis=1):
    d, wl, gl, ql, kvl = dm.d, dm.wl, dm.gl, dm.ql, dm.kvl
    size = a.shape[axis]
    cut = lambda lo, hi: lax.slice_in_dim(a, min(lo, size), min(hi, size), axis=axis)

    def pad(p, width):
        cfg = [(0, 0)] * a.ndim
        cfg[axis] = (0, width - p.shape[axis])
        return jnp.pad(p, cfg)

    o = 3 * d
    lora = jnp.concatenate([pad(cut(o, o + wl), LANES), pad(cut(o + wl, o + 2 * wl), LANES),
                            cut(o + 2 * wl, o + 2 * wl + gl)], axis=axis)
    o += 2 * wl + gl
    mla = pad(cut(o, o + ql + kvl + ROPE_DIM), ql + kvl + LANES)
    o += ql + kvl + ROPE_DIM
    return dict(r=cut(0, d), k=cut(d, 2 * d), v=cut(2 * d, 3 * d), l=lora, m=mla, ga=cut(o, o + d), gb=cut(o + d, o + 2 * d))


def merge_in(g, dm, axis=1):
    wl, gl, ql, kvl = dm.wl, dm.gl, dm.ql, dm.kvl
    cut = lambda p, lo, hi: lax.slice_in_dim(p, lo, hi, axis=axis)
    l, m = g["l"], g["m"]
    return jnp.concatenate([g["r"], g["k"], g["v"], cut(l, 0, wl), cut(l, LANES, LANES + wl), cut(l, 2 * LANES, 2 * LANES + gl),
                            cut(m, 0, ql + kvl + ROPE_DIM), g["ga"], g["gb"]], axis=axis)


def split_uq(w, dm):
    w3 = w.reshape(w.shape[0], dm.hm, QK_DIM)
    return jnp.concatenate([w3[:, :, :NOPE_DIM].reshape(w.shape[0], -1), w3[:, :, NOPE_DIM:].reshape(w.shape[0], -1)], axis=1)


def merge_uq(gn, gp, dm):
    r = gn.shape[0]
    return jnp.concatenate([gn.reshape(r, dm.hm, NOPE_DIM), gp.reshape(r, dm.hm, ROPE_DIM)], axis=2).reshape(r, -1)


def split_ukv(w, dm):
    w3 = w.reshape(w.shape[0], dm.hm, NOPE_DIM + V_DIM)
    return jnp.concatenate([w3[:, :, :NOPE_DIM].reshape(w.shape[0], -1), w3[:, :, NOPE_DIM:].reshape(w.shape[0], -1)], axis=1)


def merge_ukv(gk, gv, dm):
    r = gk.shape[0]
    return jnp.concatenate([gk.reshape(r, dm.hm, NOPE_DIM), gv.reshape(r, dm.hm, V_DIM)], axis=2).reshape(r, -1)


def head_matrices(d):
    heads = d // RWKV_HEAD
    e = (np.arange(d)[:, None] // RWKV_HEAD == np.arange(LANES)[None, :]) & (np.arange(LANES)[None, :] < heads)
    return jnp.asarray(e, BF16), jnp.asarray(e.T, BF16)


def rope_tables(t):
    pos = jnp.arange(t, dtype=F32)
    inv_freq = 1.0 / (ROPE_THETA ** (jnp.arange(0, ROPE_DIM, 2, dtype=F32) / ROPE_DIM))
    ang = pos[:, None] * inv_freq[None, :]
    cos, sin = jnp.cos(ang), jnp.sin(ang)
    return jnp.tile(jnp.concatenate([cos, cos], axis=1), (1, 2)), jnp.tile(jnp.concatenate([-sin, sin], axis=1), (1, 2))


def local_step(dm, x, loss_target, meta, wt, late_shards, late_rows, sp):
    bl, t, n, d, hm = dm.bl, dm.t, dm.n, dm.d, dm.hm
    e, et = head_matrices(d)
    ct, st = rope_tables(t)
    padz = jnp.zeros((bl, t - dm.t_real, d), F32)
    h0 = jnp.concatenate([jnp.broadcast_to(meta[None], (bl, dm.n_meta, d)), x, padz], axis=1).reshape(n, d)
    tgt = jnp.concatenate([jnp.zeros((bl, dm.n_meta, d), F32), loss_target, padz], axis=1).reshape(n, d)
    tpos = jnp.arange(t)
    mask = jnp.tile(((tpos >= dm.n_meta) & (tpos < dm.t_real)).astype(F32), bl).reshape(n, 1)

    win = split_in(wt["w_in"], dm, axis=0)
    mu = split_in(sp["tm_mu"], dm)
    wq, wkv = split_uq(wt["w_uq"], dm), split_ukv(wt["w_ukv"], dm)
    prm = dict(w0=sp["w0"], a0=sp["a0"], k_k=sp["k_k"], k_a=sp["k_a"], gn_w=sp["gn_w"], gn_b=sp["gn_b"], r_k=sp["r_k"],
               w_up=_pad_rows(wt["w_up"], LANES).astype(F32), a_up=_pad_rows(wt["a_up"], LANES).astype(F32),
               g_up=wt["g_up"].astype(F32))

    h1, ffn1 = ffn_forward(h0, sp["ffn1_norm"], wt["ffn1_w_gate"], wt["ffn1_w_up"], wt["ffn1_w_down"], "ffn1")
    u = rms_fwd(h1, sp["mix_norm"], "mix_rms")
    proj = {key: matmul([(u, win[key])], "nt", name=f"proj_{key}") for key in win}
    sh = {key: lerp_fwd(proj[key], mu[key], bl, t, f"shift_{key}") for key in ("r", "k", "v", "l")}
    decay, kmod, kneg, bvec, gate = rwkv_prep_fwd(sh["k"], sh["l"], prm, e, et, "rwkv_prep")
    pairs = min(SCAN_PAIRS, d // LANES)
    y, hist, late_all = scan_forward(sh["r"], decay, kmod, kneg, bvec, sh["v"], late_shards, bl, t, dm.t_real, d, "wkv_scan",
                                     pairs, SCAN_FWD_STEPS)
    wt = dict(wt, **{key: late_all[:, lo:hi].reshape(-1, d) for key, lo, hi in zip(LATE, late_rows[:-1], late_rows[1:])})
    cqn, ckvn = mla_prep_fwd(proj["m"], sp["q_norm"], sp["kv_norm"], "mla_norms")
    q = matmul([(cqn, wq)], "nn", name="mla_q")
    kv = matmul([(ckvn, wkv)], "nn", name="mla_kv")
    o, lse = attn_fwd(q, kv, proj["m"], ct, st, bl, t, hm, "mla_attn")
    post_in = [y, sh["r"], kmod, sh["v"], gate, proj["ga"], proj["gb"], o]
    mix = rwkv_post_fwd(post_in, prm, e, et, "mix_gate")
    h2 = matmul([(mix, wt["w_out"])], "nn", res=h1, name="out_proj")
    h3, ffn2 = ffn_forward(h2, sp["ffn2_norm"], wt["ffn2_w_gate"], wt["ffn2_w_up"], wt["ffn2_w_down"], "ffn2")
    dh3, d_final, loss = loss_head(h3, tgt, mask, sp["final_norm"], "loss_head")

    gw, gs = {}, {"final_norm": d_final}
    dh2, gs["ffn2_norm"], gw["ffn2_w_gate"], gw["ffn2_w_up"], gw["ffn2_w_down"] = ffn_backward(
        dh3, h2, sp["ffn2_norm"], wt["ffn2_w_gate"], wt["ffn2_w_up"], wt["ffn2_w_down"], ffn2, "ffn2")
    dmix = matmul([(dh2, wt["w_out"])], "nt", name="out_proj_dx")
    gw["w_out"] = matmul([(mix, dh2)], "tn", out_dtype=MMD, name="out_proj_dw")
    late_grads = jnp.concatenate([gw.pop(key).reshape(N_DEV, hi - lo, d) for key, lo, hi in
                                  zip(LATE, late_rows[:-1], late_rows[1:])], axis=1).astype(MMD)
    (dy, dr_p, dkm_p, dv_p, dgate, dpga, dpgb, do, gs["gn_w"], gs["gn_b"], gs["r_k"]) = rwkv_post_bwd(
        post_in, prm, e, et, dmix, "mix_gate_bwd")
    dqn, dqpe, dkn, dv_att, dkpe = attn_bwd(q, kv, proj["m"], o, do, lse, ct, st, bl, t, hm, "mla_attn_bwd")
    nq = hm * NOPE_DIM
    dcqn = matmul([(dqn, wq[:, :nq])], "nt", name="mla_q_dx1")
    dcqn = matmul([(dqpe, wq[:, nq:])], "nt", res=dcqn, name="mla_q_dx2")
    gw["w_uq"] = merge_uq(matmul([(cqn, dqn)], "tn", name="mla_q_dw1"), matmul([(cqn, dqpe)], "tn", name="mla_q_dw2"), dm)
    dckvn = matmul([(dkn, wkv[:, :nq]), (dv_att, wkv[:, nq:])], "nt", name="mla_kv_dx", tk=1024)
    gw["w_ukv"] = merge_ukv(matmul([(ckvn, dkn)], "tn", name="mla_kv_dw1"), matmul([(ckvn, dv_att)], "tn", name="mla_kv_dw2"), dm)
    dproj = {"ga": dpga, "gb": dpgb}
    dproj["m"], gs["q_norm"], gs["kv_norm"] = mla_prep_bwd(proj["m"], sp["q_norm"], sp["kv_norm"], dcqn, dckvn, dkpe, "mla_norms_bwd")
    dr_s, ddecay, dk_s, dkneg, dbvec, dv_s, late_recv = scan_backward(
        sh["r"], decay, kmod, kneg, bvec, sh["v"], dy, hist, late_grads, bl, t, dm.t_real, d, "wkv_scan_bwd", pairs,
        SCAN_BWD_STEPS)
    late_sum = sum_blocks(late_recv, "sum_late")
    (dsh_k, dsh_l, gs["w0"], gs["a0"], gs["k_k"], gs["k_a"], g_wup, g_aup, gw["g_up"]) = rwkv_prep_bwd(
        sh["k"], sh["l"], prm, e, et, [ddecay, dk_s, dkm_p, dkneg, dbvec, dgate], "rwkv_prep_bwd")
    gw["w_up"], gw["a_up"] = g_wup[:dm.wl], g_aup[:dm.wl]
    dmu = {}
    for key, cts in (("r", [dr_s, dr_p]), ("k", [dsh_k]), ("v", [dv_s, dv_p]), ("l", [dsh_l])):
        dproj[key], dmu[key] = lerp_bwd(proj[key], mu[key], cts, bl, t, f"shift_{key}_bwd")
    zero_m = jnp.zeros((1, proj["m"].shape[1]), F32)
    gs["tm_mu"] = merge_in(dict(dmu, m=zero_m, ga=zero_m[:, :0], gb=zero_m[:, :0]), dm)[:, :3 * d + 2 * dm.wl + dm.gl]
    wide = ("r", "k", "v", "ga", "gb")
    du = matmul([(dproj[key], win[key]) for key in wide], "nn", name="proj_dx", tn=512, tk=512)
    du = matmul([(dproj["l"], win["l"])], "nn", res=du, name="proj_dx_l")
    du = matmul([(dproj["m"], win["m"])], "nn", res=du, name="proj_dx_m")
    gw["w_in"] = merge_in({key: matmul([(dproj[key], u)], "tn", out_dtype=MMD, name=f"proj_dw_{key}") for key in win},
                          dm, axis=0)
    dh1, gs["mix_norm"] = rms_bwd(h1, sp["mix_norm"], du, dh2, "mix_rms_bwd")
    dh0, gs["ffn1_norm"], gw["ffn1_w_gate"], gw["ffn1_w_up"], gw["ffn1_w_down"] = ffn_backward(
        dh1, h0, sp["ffn1_norm"], wt["ffn1_w_gate"], wt["ffn1_w_up"], wt["ffn1_w_down"], ffn1, "ffn1")
    grad_x = dh0.reshape(bl, t, d)[:, dm.n_meta:dm.t_real]
    dmeta = batch_sum_rows(dh0, bl, t, dm.n_meta, "meta_grad")
    return loss, grad_x, dmeta, gw, late_sum, gs


COL_SHARDED = ("ffn1_w_gate", "ffn1_w_up", "w_in", "w_up", "a_up", "g_up", "w_uq", "w_ukv", "ffn2_w_gate", "ffn2_w_up")
ROW_SHARDED = ("ffn1_w_down", "w_out", "ffn2_w_down")
TRANSPOSED = ("ffn1_w_gate", "ffn1_w_up", "w_in", "ffn2_w_gate", "ffn2_w_up")
EARLY = ("ffn1_w_gate", "ffn1_w_up", "ffn1_w_down", "w_in")
LATE = ("w_out", "ffn2_w_gate", "ffn2_w_up", "ffn2_w_down")
NARROW = ("w_up", "a_up", "g_up", "w_uq", "w_ukv")
MATRICES = ("ffn1_w_gate", "ffn1_w_up", "ffn1_w_down", "w_in", "w_up", "a_up", "g_up", "w_uq", "w_ukv", "w_out",
            "ffn2_w_gate", "ffn2_w_up", "ffn2_w_down")
SMALL = ("ffn1_norm", "mix_norm", "tm_mu", "w0", "a0", "k_k", "k_a", "r_k", "gn_w", "gn_b", "q_norm", "kv_norm",
         "ffn2_norm", "final_norm")
WEIGHTS = ("meta_tokens", "ffn1_norm", "ffn1_w_gate", "ffn1_w_up", "ffn1_w_down", "mix_norm", "w_in", "tm_mu", "w0", "w_up",
           "a0", "a_up", "g_up", "k_k", "k_a", "r_k", "gn_w", "gn_b", "q_norm", "w_uq", "kv_norm", "w_ukv", "w_out",
           "ffn2_norm", "ffn2_w_gate", "ffn2_w_up", "ffn2_w_down", "final_norm")
PACK_COLS = 1024
PACK_ALIGN = 16 * PACK_COLS


def _pack(parts):
    offs, o = [], 0
    for p in parts:
        offs.append(o)
        o += p.shape[1]
    total = -(-o // PACK_ALIGN) * PACK_ALIGN
    flat = jnp.concatenate(list(parts) + [jnp.zeros((parts[0].shape[0], total - o), parts[0].dtype)], axis=1)
    return flat.reshape(parts[0].shape[0], total // PACK_COLS, PACK_COLS), offs


def kernel(x, meta_tokens, ffn1_norm, ffn1_w_gate, ffn1_w_up, ffn1_w_down, mix_norm, w_in, tm_mu, w0, w_up, a0, a_up, g_up, k_k, k_a, r_k, gn_w, gn_b, q_norm, w_uq, kv_norm, w_ukv, w_out, ffn2_norm, ffn2_w_gate, ffn2_w_up, ffn2_w_down, final_norm, loss_target, m_meta_tokens, m_ffn1_norm, m_ffn1_w_gate, m_ffn1_w_up, m_ffn1_w_down, m_mix_norm, m_w_in, m_tm_mu, m_w0, m_w_up, m_a0, m_a_up, m_g_up, m_k_k, m_k_a, m_r_k, m_gn_w, m_gn_b, m_q_norm, m_w_uq, m_kv_norm, m_w_ukv, m_w_out, m_ffn2_norm, m_ffn2_w_gate, m_ffn2_w_up, m_ffn2_w_down, m_final_norm, v_meta_tokens, v_ffn1_norm, v_ffn1_w_gate, v_ffn1_w_up, v_ffn1_w_down, v_mix_norm, v_w_in, v_tm_mu, v_w0, v_w_up, v_a0, v_a_up, v_g_up, v_k_k, v_k_a, v_r_k, v_gn_w, v_gn_b, v_q_norm, v_w_uq, v_kv_norm, v_w_ukv, v_w_out, v_ffn2_norm, v_ffn2_w_gate, v_ffn2_w_up, v_ffn2_w_down, v_final_norm):
    args = dict(locals())
    wts = {k: args[k] for k in WEIGHTS}
    ms = {k: args["m_" + k] for k in WEIGHTS}
    vs = {k: args["v_" + k] for k in WEIGHTS}
    dm = Dims(x, None, w_up, g_up, q_norm, kv_norm, ffn1_w_down.shape[1] * N_DEV)

    shard2d = {k: wts[k].reshape(wts[k].shape[-2], wts[k].shape[-1]) for k in MATRICES}
    sent = {k: shard2d[k].astype(MMD).T if k in TRANSPOSED else shard2d[k] for k in MATRICES}
    early_rows = np.cumsum([0] + [sent[k].shape[0] for k in EARLY])
    late_rows = np.cumsum([0] + [sent[k].shape[0] for k in LATE])
    got_early = all_gather_two_level(jnp.concatenate([sent[k].astype(MMD) for k in EARLY], axis=0), "gather_early")
    full = {k: got_early[:, lo:hi].reshape(-1, dm.d) for k, lo, hi in zip(EARLY, early_rows[:-1], early_rows[1:])}
    late_shards = jnp.concatenate([sent[k].astype(MMD) for k in LATE], axis=0)
    send, offs = _pack([sent[k].astype(MMD).reshape(1, -1) for k in NARROW])
    got = all_gather(send[0], "gather_narrow").reshape(N_DEV, -1)
    for k, o in zip(NARROW, offs):
        r, c = sent[k].shape
        full[k] = got[:, o:o + r * c].reshape(N_DEV, r, c).transpose(1, 0, 2).reshape(r, N_DEV * c)
    mr, mc = meta_tokens.shape
    meta = all_gather(meta_tokens, "gather_meta").transpose(1, 0, 2).reshape(mr, N_DEV * mc)
    small = {k: wts[k].reshape(1, -1) for k in SMALL}

    loss, grad_x, dmeta, gw, gsum_late, gs = local_step(dm, x, loss_target, meta, full, late_shards, late_rows, small)

    gearly = jnp.concatenate([gw[k].reshape(N_DEV, sent[k].shape[0], dm.d) for k in EARLY], axis=1).astype(MMD)
    gsum_early = reduce_scatter_two_level(gearly, "scatter_early")
    grads = {}
    for names, rows, gsum_rows in ((EARLY, early_rows, gsum_early), (LATE, late_rows, gsum_late)):
        for k, lo, hi in zip(names, rows[:-1], rows[1:]):
            grads[k] = gsum_rows[lo:hi].T if k in TRANSPOSED else gsum_rows[lo:hi]

    def blocks(k, g):
        r, c = sent[k].shape
        return g.reshape(r, N_DEV, c).transpose(1, 0, 2).reshape(N_DEV, r * c)

    gsend, goffs = _pack([blocks(k, gw[k]).astype(MMD) for k in NARROW]
                         + [dmeta.reshape(mr, N_DEV, mc).transpose(1, 0, 2).reshape(N_DEV, mr * mc).astype(MMD)])
    gsum = sum_blocks(all_to_all(gsend, "scatter_narrow"), "sum_narrow").reshape(-1)
    for k, o in zip(NARROW, goffs):
        r, c = sent[k].shape
        grads[k] = gsum[o:o + r * c].reshape(r, c)
    grads["meta_tokens"] = gsum[goffs[-1]:goffs[-1] + mr * mc].reshape(mr, mc)

    ssend, soffs = _pack([gs[k].reshape(1, -1) for k in SMALL] + [loss])
    ssum = sum_blocks(all_gather(ssend[0], "gather_small"), "sum_small").reshape(-1)
    for k, o in zip(SMALL, soffs):
        grads[k] = ssum[o:o + small[k].shape[1]]
    loss_total = ssum[soffs[-1]]

    delta, new_m, new_v = {}, {}, {}
    for k in MATRICES + ("meta_tokens",):
        shp = wts[k].shape
        to2d = lambda a: a.reshape(shp[-2], shp[-1])
        dlt, nm, nv = adamw(to2d(wts[k]), grads[k], to2d(ms[k]), to2d(vs[k]), f"adamw_{k}")
        delta[k], new_m[k], new_v[k] = dlt.reshape(shp), nm.reshape(shp), nv.reshape(shp)
        grads[k] = grads[k].reshape(shp)
    pw, _ = _pack([wts[k].reshape(1, -1) for k in SMALL])
    pm_, _ = _pack([ms[k].reshape(1, -1) for k in SMALL])
    pv, _ = _pack([vs[k].reshape(1, -1) for k in SMALL])
    pg, poffs = _pack([grads[k].reshape(1, -1) for k in SMALL])
    dlt, nm, nv = adamw(pw[0], pg[0], pm_[0], pv[0], "adamw_small")
    for k, o in zip(SMALL, poffs):
        shp, sz = wts[k].shape, small[k].shape[1]
        cut = lambda a: a.reshape(-1)[o:o + sz].reshape(shp)
        delta[k], new_m[k], new_v[k] = cut(dlt), cut(nm), cut(nv)
        grads[k] = grads[k].reshape(shp)

    return (loss_total, grad_x, *[grads[k] for k in WEIGHTS], *[delta[k] for k in WEIGHTS],
            *[new_m[k] for k in WEIGHTS], *[new_v[k] for k in WEIGHTS])
```

```python
import functools

import numpy as np
import jax
import jax.numpy as jnp
from jax import lax
from jax.experimental import pallas as pl
from jax.experimental.pallas import tpu as pltpu

F32 = jnp.float32
BF16 = jnp.bfloat16
MMD = BF16

NORM_EPS = 1e-6
RWKV_HEAD = 64
GN_EPS = RWKV_HEAD * 1e-5
NOPE_DIM = 128
ROPE_DIM = 64
V_DIM = 128
QK_DIM = NOPE_DIM + ROPE_DIM
ROPE_THETA = 10000.0
ADAM_LR = 0.001
ADAM_B1 = 0.9
ADAM_B2 = 0.999
ADAM_EPS = 1e-08
ADAM_WD = 0.01
ADAM_STEP = 10

LANES = 128
SCAN_PAIRS = 8
SCAN_FWD_STEPS = 32
SCAN_BWD_STEPS = 16
N_DEV = 8
VMEM_LIMIT = 56 * 1024 * 1024
MESH = pl.DeviceIdType.MESH


def _tile(n, target, align):
    best = None
    for d in range(align, min(n, target) + 1, align):
        if n % d == 0:
            best = d
    return best if best is not None else n


def _params(sem=None):
    return pltpu.CompilerParams(dimension_semantics=sem, vmem_limit_bytes=VMEM_LIMIT)


def _mm(a, b, dims=((1,), (0,))):
    return lax.dot_general(a.astype(MMD), b.astype(MMD), (dims, ((), ())), preferred_element_type=F32)


@jax.custom_vjp
def mmdot(a, b):
    return _mm(a, b)


def _mmdot_fwd(a, b):
    return _mm(a, b), (a, b)


def _mmdot_bwd(res, g):
    a, b = res
    return _mm(g, b, ((1,), (1,))).astype(a.dtype), _mm(a, g, ((0,), (0,))).astype(b.dtype)


mmdot.defvjp(_mmdot_fwd, _mmdot_bwd)


def _dot2(x, m):
    hi = x.astype(BF16)
    lo = (x - hi.astype(F32)).astype(BF16)
    return (lax.dot_general(hi, m, (((1,), (0,)), ((), ())), preferred_element_type=F32)
            + lax.dot_general(lo, m, (((1,), (0,)), ((), ())), preferred_element_type=F32))


@jax.custom_vjp
def segsum(x, e, et):
    return _dot2(_dot2(x, e), et)


def _segsum_fwd(x, e, et):
    return segsum(x, e, et), (e, et)


def _segsum_bwd(res, g):
    e, et = res
    return segsum(g, e, et), jnp.zeros_like(e), jnp.zeros_like(et)


segsum.defvjp(_segsum_fwd, _segsum_bwd)


def _sigmoid(x):
    return 1.0 / (1.0 + jnp.exp(-x))


def _softplus(x):
    return jnp.maximum(x, 0.0) + jnp.log(1.0 + jnp.exp(-jnp.abs(x)))


def _rms(x, g):
    return x * lax.rsqrt(jnp.mean(x * x, axis=-1, keepdims=True) + NORM_EPS) * g


_DIMS = {"nn": ((1,), (0,)), "nt": ((1,), (1,)), "tn": ((0,), (0,))}


def matmul(pairs, mode, *, name, out_dtype=F32, res=None, alpha=1.0, tm=1088, tn=1024, tk=2048):
    a0, b0 = pairs[0]
    if mode == "nn":
        (m, k), n = a0.shape, b0.shape[1]
    elif mode == "nt":
        (m, k), n = a0.shape, b0.shape[0]
    else:
        (k, m), n = a0.shape, b0.shape[1]
    tm = _tile(m, 1408, 128) if mode == "tn" else _tile(m, tm, 16)
    tn = _tile(n, 2048 if mode == "tn" else tn, 128)
    tk = _tile(k, min(tk, 1024), 16) if mode == "tn" else _tile(k, tk, 128)
    nk = k // tk
    npair = len(pairs)
    if mode == "tn":
        a_spec = pl.BlockSpec((tk, tm), lambda i, j, kk: (kk, i))
    else:
        a_spec = pl.BlockSpec((tm, tk), lambda i, j, kk: (i, kk))
    if mode == "nt":
        b_spec = pl.BlockSpec((tn, tk), lambda i, j, kk: (j, kk))
    else:
        b_spec = pl.BlockSpec((tk, tn), lambda i, j, kk: (kk, j))
    o_spec = pl.BlockSpec((tm, tn), lambda i, j, kk: (i, j))
    dims = _DIMS[mode]

    def body(*refs):
        ab = refs[:2 * npair]
        res_ref = refs[2 * npair] if res is not None else None
        o_ref, acc_ref = refs[-2], refs[-1]
        kk = pl.program_id(2)

        @pl.when(kk == 0)
        def _():
            acc_ref[...] = jnp.zeros_like(acc_ref)

        part = _mm(ab[0][...], ab[1][...], dims)
        for p in range(1, npair):
            part = part + _mm(ab[2 * p][...], ab[2 * p + 1][...], dims)
        acc_ref[...] += part

        @pl.when(kk == nk - 1)
        def _():
            out = acc_ref[...] * alpha if alpha != 1.0 else acc_ref[...]
            if res_ref is not None:
                out = res_ref[...].astype(F32) + out
            o_ref[...] = out.astype(o_ref.dtype)

    args, specs = [], []
    for a, b in pairs:
        args += [a, b]
        specs += [a_spec, b_spec]
    if res is not None:
        args.append(res)
        specs.append(o_spec)
    return pl.pallas_call(
        body, grid=(m // tm, n // tn, nk), in_specs=specs, out_specs=o_spec,
        out_shape=jax.ShapeDtypeStruct((m, n), out_dtype), scratch_shapes=[pltpu.VMEM((tm, tn), F32)],
        compiler_params=_params(("parallel", "parallel", "arbitrary")), name=name)(*args)


def tilek(fn, ins, outs, *, n_rows, tr, name):
    tr = _tile(n_rows, tr, 16)
    n_in = len(ins)
    in_specs = []
    for arr, kind in ins:
        if kind == "r":
            in_specs.append(pl.BlockSpec((tr, arr.shape[1]), lambda i: (i, 0)))
        else:
            in_specs.append(pl.BlockSpec(arr.shape, lambda i, nd=arr.ndim: (0,) * nd))
    out_specs, out_shape = [], []
    has_acc = False
    for o in outs:
        if o[0] == "r":
            out_specs.append(pl.BlockSpec((tr, o[1]), lambda i: (i, 0)))
            out_shape.append(jax.ShapeDtypeStruct((n_rows, o[1]), o[2]))
        else:
            has_acc = True
            out_specs.append(pl.BlockSpec(o[1], lambda i, nd=len(o[1]): (0,) * nd))
            out_shape.append(jax.ShapeDtypeStruct(o[1], F32))

    def body(*refs):
        i = pl.program_id(0)
        vals = fn(*[r[...] for r in refs[:n_in]])
        for o, r, v in zip(outs, refs[n_in:], vals):
            if o[0] == "r":
                r[...] = v.astype(r.dtype)
            else:
                @pl.when(i == 0)
                def _(r=r):
                    r[...] = jnp.zeros_like(r)

                r[...] += v

    return pl.pallas_call(
        body, grid=(n_rows // tr,), in_specs=in_specs, out_specs=out_specs, out_shape=out_shape,
        compiler_params=_params(("arbitrary",) if has_acc else ("parallel",)), name=name)(*[a for a, _ in ins])


def rms_fwd(x, g, name):
    n, d = x.shape
    return tilek(lambda xv, gv: (_rms(xv, gv),), [(x, "r"), (g, "f")], [("r", d, MMD)], n_rows=n, tr=256, name=name)[0]


def rms_bwd(x, g, dy, dres, name):
    n, d = x.shape

    def fn(xv, gv, dyv, drv):
        _, vjp = jax.vjp(_rms, xv, gv)
        dx, dg = vjp(dyv.astype(F32))
        return drv + dx, dg

    return tilek(fn, [(x, "r"), (g, "f"), (dy, "r"), (dres, "r")], [("r", d, F32), ("acc", (1, d))],
                 n_rows=n, tr=128, name=name)


def loss_head(h, tgt, mask, g, name):
    n, d = h.shape

    def fn(hv, tv, mv, gv):
        def lossf(hh, gg):
            e = (_rms(hh, gg) - tv) * mv
            s = jnp.sum(jnp.sum(e * e, axis=1, keepdims=True), axis=0, keepdims=True)
            return s * (0.5 / d)

        l, vjp = jax.vjp(lossf, hv, gv)
        dh, dg = vjp(jnp.ones((1, 1), F32))
        return dh, dg, jnp.broadcast_to(l, (1, LANES))

    return tilek(fn, [(h, "r"), (tgt, "r"), (mask, "r"), (g, "f")],
                 [("r", d, F32), ("acc", (1, d)), ("acc", (1, LANES))], n_rows=n, tr=128, name=name)


def ffn_up(hn, wg, wu, name):
    n, d = hn.shape
    f = wg.shape[0]
    tm, tn = _tile(n, 544, 16), _tile(f, 1408, 128)

    def body(a_ref, g_ref, u_ref, og_ref, ou_ref, oa_ref):
        a = a_ref[...]
        g = _mm(a, g_ref[...], ((1,), (1,)))
        u = _mm(a, u_ref[...], ((1,), (1,)))
        og_ref[...] = g.astype(og_ref.dtype)
        ou_ref[...] = u.astype(ou_ref.dtype)
        oa_ref[...] = (g * _sigmoid(g) * u).astype(oa_ref.dtype)

    o_spec = pl.BlockSpec((tm, tn), lambda i, j: (i, j))
    w_spec = pl.BlockSpec((tn, d), lambda i, j: (j, 0))
    return pl.pallas_call(
        body, grid=(n // tm, f // tn), in_specs=[pl.BlockSpec((tm, d), lambda i, j: (i, 0)), w_spec, w_spec],
        out_specs=[o_spec, o_spec, o_spec],
        out_shape=[jax.ShapeDtypeStruct((n, f), MMD)] * 3,
        compiler_params=_params(("parallel", "parallel")), name=name)(hn, wg, wu)


def ffn_down_bwd(dh, wd, gate, up, name):
    n, d = dh.shape
    f = wd.shape[0]
    tm, tn = _tile(n, 544, 16), _tile(f, 1408, 128)

    def body(dh_ref, w_ref, g_ref, u_ref, dg_ref, du_ref):
        da = 0.5 * _mm(dh_ref[...], w_ref[...], ((1,), (1,)))
        g, u = g_ref[...].astype(F32), u_ref[...].astype(F32)
        s = _sigmoid(g)
        dg_ref[...] = (da * u * (s * (1.0 + g * (1.0 - s)))).astype(dg_ref.dtype)
        du_ref[...] = (da * (g * s)).astype(du_ref.dtype)

    o_spec = pl.BlockSpec((tm, tn), lambda i, j: (i, j))
    return pl.pallas_call(
        body, grid=(n // tm, f // tn),
        in_specs=[pl.BlockSpec((tm, d), lambda i, j: (i, 0)), pl.BlockSpec((tn, d), lambda i, j: (j, 0)), o_spec, o_spec],
        out_specs=[o_spec, o_spec],
        out_shape=[jax.ShapeDtypeStruct((n, f), MMD), jax.ShapeDtypeStruct((n, f), MMD)],
        compiler_params=_params(("parallel", "parallel")), name=name)(dh, wd, gate, up)


def ffn_forward(h, g, wg, wu, wd, tag):
    hn = rms_fwd(h, g, f"{tag}_rms")
    gate, up, act = ffn_up(hn, wg, wu, f"{tag}_up")
    out = matmul([(act, wd)], "nn", res=h, alpha=0.5, name=f"{tag}_down")
    return out, (hn, gate, up, act)


def ffn_backward(dout, h, g, wg, wu, wd, saved, tag):
    hn, gate, up, act = saved
    dgate, dup = ffn_down_bwd(dout, wd, gate, up, f"{tag}_dact")
    dwd = matmul([(act, dout)], "tn", alpha=0.5, out_dtype=MMD, name=f"{tag}_dwd")
    dwg = matmul([(dgate, hn)], "tn", out_dtype=MMD, name=f"{tag}_dwg")
    dwu = matmul([(dup, hn)], "tn", out_dtype=MMD, name=f"{tag}_dwu")
    dhn = matmul([(dgate, wg), (dup, wu)], "nn", name=f"{tag}_dhn")
    dh, dg = rms_bwd(h, g, dhn, dout, f"{tag}_drms")
    return dh, dg, dwg, dwu, dwd


def lerp_fwd(p, mu, bl, t, name):
    n, w = p.shape
    cb = _tile(w, 256, 128)

    def body(p_ref, mu_ref, o_ref):
        x = p_ref[...]
        row = lax.broadcasted_iota(jnp.int32, x.shape, 0)
        prev = jnp.where(row == 0, 0.0, pltpu.roll(x, 1, 0))
        o_ref[...] = x + mu_ref[...] * (prev - x)

    spec = pl.BlockSpec((t, cb), lambda b, j: (b, j))
    return pl.pallas_call(
        body, grid=(bl, w // cb), in_specs=[spec, pl.BlockSpec((1, cb), lambda b, j: (0, j))], out_specs=spec,
        out_shape=jax.ShapeDtypeStruct((n, w), F32), compiler_params=_params(("parallel", "parallel")), name=name)(p, mu)


def lerp_bwd(p, mu, douts, bl, t, name):
    n, w = p.shape
    cb = _tile(w, 256, 128)
    nd = len(douts)

    def body(*refs):
        p_ref, mu_ref = refs[0], refs[1]
        dp_ref, dmu_ref = refs[2 + nd], refs[3 + nd]
        b = pl.program_id(1)
        x, m = p_ref[...], mu_ref[...]
        d = refs[2][...]
        for r in refs[3:2 + nd]:
            d = d + r[...]
        row = lax.broadcasted_iota(jnp.int32, x.shape, 0)
        prev = jnp.where(row == 0, 0.0, pltpu.roll(x, 1, 0))
        z = d * m
        nxt = jnp.where(row == t - 1, 0.0, pltpu.roll(z, t - 1, 0))
        dp_ref[...] = d - z + nxt

        @pl.when(b == 0)
        def _():
            dmu_ref[...] = jnp.zeros_like(dmu_ref)

        dmu_ref[...] += jnp.sum(d * (prev - x), axis=0, keepdims=True)

    spec = pl.BlockSpec((t, cb), lambda j, b: (b, j))
    cspec = pl.BlockSpec((1, cb), lambda j, b: (0, j))
    return pl.pallas_call(
        body, grid=(w // cb, bl), in_specs=[spec, cspec] + [spec] * nd, out_specs=[spec, cspec],
        out_shape=[jax.ShapeDtypeStruct((n, w), F32), jax.ShapeDtypeStruct((1, w), F32)],
        compiler_params=_params(("parallel", "arbitrary")), name=name)(p, mu, *douts)


def _prep(k, xw, xa, xg, w0, a0, k_k, k_a, w_up, a_up, g_up, e, et):
    w_pre = -_softplus(-(w0 + mmdot(jnp.tanh(xw), w_up))) - 0.5
    decay = jnp.exp(-jnp.exp(w_pre))
    a = _sigmoid(a0 + mmdot(xa, a_up))
    g = mmdot(_sigmoid(xg), g_up)
    kk = k * k_k
    kk = kk * lax.rsqrt(jnp.maximum(segsum(kk * kk, e, et), 1e-24))
    kmod = k * (1.0 + (a - 1.0) * k_a)
    return decay, kmod, -kk, kk * a, g


def _lora_parts(xl):
    return xl[:, :LANES], xl[:, LANES:2 * LANES], xl[:, 2 * LANES:]


def rwkv_prep_fwd(pk, pl_, prm, e, et, name):
    n, d = pk.shape
    small = [prm[k] for k in ("w0", "a0", "k_k", "k_a", "w_up", "a_up", "g_up")]
    ins = [(pk, "r"), (pl_, "r")] + [(s, "f") for s in small] + [(e, "f"), (et, "f")]
    return tilek(lambda k, xl, *rest: _prep(k, *_lora_parts(xl), *rest), ins, [("r", d, F32)] * 5, n_rows=n, tr=128, name=name)


def rwkv_prep_bwd(pk, pl_, prm, e, et, cts, name):
    n, d = pk.shape
    small = [prm[k] for k in ("w0", "a0", "k_k", "k_a", "w_up", "a_up", "g_up")]

    def fn(k, xl, w0, a0, k_k, k_a, w_up, a_up, g_up, ev, etv, dw, dkm1, dkm2, dkn, db, dg):
        _, vjp = jax.vjp(lambda *a: _prep(*a, ev, etv), k, *_lora_parts(xl), w0, a0, k_k, k_a, w_up, a_up, g_up)
        dk, dxw, dxa, dxg, *dsmall = vjp((dw, dkm1 + dkm2, dkn, db, dg))
        return (dk, jnp.concatenate([dxw, dxa, dxg], axis=1), *dsmall)

    ins = [(pk, "r"), (pl_, "r")] + [(s, "f") for s in small] + [(e, "f"), (et, "f")] + [(c, "r") for c in cts]
    outs = [("r", d, F32), ("r", pl_.shape[1], F32)] + [("acc", s.shape) for s in small]
    return tilek(fn, ins, outs, n_rows=n, tr=64, name=name)


def _post(y, r, km, v, g, pga, pgb, yb, gn_w, gn_b, r_k, e, et):
    inv = 1.0 / RWKV_HEAD
    yc = y - segsum(y, e, et) * inv
    var = segsum(yc * yc, e, et) * inv
    yn = yc * lax.rsqrt(var + GN_EPS) * gn_w + gn_b
    bonus = segsum(r * km * r_k, e, et) * v
    ya = (yn + bonus) * g
    return _sigmoid(pga) * ya + _sigmoid(pgb) * yb


def rwkv_post_fwd(acts, prm, e, et, name):
    n, d = acts[0].shape
    small = [prm[k] for k in ("gn_w", "gn_b", "r_k")]
    ins = [(a, "r") for a in acts] + [(s, "f") for s in small] + [(e, "f"), (et, "f")]
    return tilek(lambda *a: (_post(*a),), ins, [("r", d, MMD)], n_rows=n, tr=128, name=name)[0]


def rwkv_post_bwd(acts, prm, e, et, dm, name):
    n, d = acts[0].shape
    small = [prm[k] for k in ("gn_w", "gn_b", "r_k")]
    na = len(acts)

    def fn(*a):
        prim, ev, etv, dmv = a[:na + 3], a[na + 3], a[na + 4], a[na + 5]
        _, vjp = jax.vjp(lambda *z: _post(*z, ev, etv), *prim)
        return vjp(dmv.astype(F32))

    ins = [(x, "r") for x in acts] + [(s, "f") for s in small] + [(e, "f"), (et, "f"), (dm, "r")]
    outs = [("r", d, F32)] * na + [("acc", s.shape) for s in small]
    return tilek(fn, ins, outs, n_rows=n, tr=64, name=name)


def _head_sums(x, first_head):
    a = jnp.sum(jnp.where(first_head, x, 0.0), axis=1, keepdims=True)
    b = jnp.sum(jnp.where(first_head, 0.0, x), axis=1, keepdims=True)
    return jnp.where(first_head, a, b)


def _round1(x):
    return (x.astype(BF16), None) if MMD == BF16 else _split2(x)


def _split2(x):
    hi = x.astype(BF16)
    return hi, (x - hi.astype(F32)).astype(BF16)


def _spread(row, eye2):
    hi, lo = _split2(row)
    return eye2 * hi, eye2 * lo


def _ones_dot(tiles, ones_blk):
    dims = (((1,), (0,)), ((), ()))
    res = lax.dot_general(jnp.concatenate([t[0] for t in tiles], axis=0), ones_blk, dims, preferred_element_type=F32)
    out = [res[i * RWKV_HEAD:(i + 1) * RWKV_HEAD] for i in range(len(tiles))]
    two_term = [i for i, t in enumerate(tiles) if t[1] is not None]
    if two_term:
        low = lax.dot_general(jnp.concatenate([tiles[i][1] for i in two_term], axis=0), ones_blk, dims,
                              preferred_element_type=F32)
        for n, i in enumerate(two_term):
            out[i] = out[i] + low[n * RWKV_HEAD:(n + 1) * RWKV_HEAD]
    return out


def _scan_consts():
    lane = lax.broadcasted_iota(jnp.int32, (1, LANES), 1)
    rows = lax.broadcasted_iota(jnp.int32, (RWKV_HEAD, LANES), 0)
    cols = lax.broadcasted_iota(jnp.int32, (RWKV_HEAD, LANES), 1)
    eye2 = ((cols & (RWKV_HEAD - 1)) == rows).astype(BF16)
    r2 = lax.broadcasted_iota(jnp.int32, (LANES, LANES), 0)
    c2 = lax.broadcasted_iota(jnp.int32, (LANES, LANES), 1)
    ones_blk = ((r2 // RWKV_HEAD) == (c2 // RWKV_HEAD)).astype(BF16)
    return lane, lane < RWKV_HEAD, eye2, ones_blk


def _riding_exchange(src_ref, dst_ref, send_sems, recv_sems, local_sem, scatter, grid):
    def copies():
        _, me = _peer(0)
        out = [pltpu.make_async_copy(src_ref.at[me] if scatter else src_ref, dst_ref.at[me], local_sem)]
        for k in range(1, N_DEV):
            dev, idx = _peer(k)
            out.append(pltpu.make_async_remote_copy(src_ref=src_ref.at[idx] if scatter else src_ref, dst_ref=dst_ref.at[me],
                                                    send_sem=send_sems.at[k - 1], recv_sem=recv_sems.at[k - 1],
                                                    device_id=dev, device_id_type=MESH))
        return out

    ids = [pl.program_id(a) for a in range(len(grid))]
    first = functools.reduce(jnp.logical_and, [i == 0 for i in ids])
    last = functools.reduce(jnp.logical_and, [i == n - 1 for i, n in zip(ids, grid)])

    def start():
        @pl.when(first)
        def _():
            for cp in copies():
                cp.start()

    def finish():
        @pl.when(last)
        def _():
            for cp in copies():
                cp.wait()

    return start, finish


_RIDE_SCRATCH = [pltpu.SemaphoreType.DMA((N_DEV - 1,)), pltpu.SemaphoreType.DMA((N_DEV - 1,)), pltpu.SemaphoreType.DMA]


def scan_forward(r, w, k, kn, b, v, ride, bl, t, t_real, d, name, pg, hch):
    npair, nst = d // LANES, t // hch
    grid = (bl, npair // pg, nst)

    def body(r_ref, w_ref, k_ref, kn_ref, b_ref, v_ref, ride_ref, y_ref, hist_ref, land_ref, s_ref, vb_ref, *sems):
        start, finish = _riding_exchange(ride_ref, land_ref, *sems, False, grid)
        start()
        _, first_head, eye2, ones_blk = _scan_consts()
        eye2f = eye2.astype(F32)
        diag = lambda tile: jnp.sum(tile * eye2f, axis=0, keepdims=True)

        @pl.when(pl.program_id(2) == 0)
        def _():
            s_ref[...] = jnp.zeros_like(s_ref)

        pair_cols = [slice(p * LANES, (p + 1) * LANES) for p in range(pg)]
        for p, tile in enumerate(_ones_dot([_spread(v_ref[0, :, cols], eye2) for cols in pair_cols], ones_blk)):
            vb_ref[p] = tile

        def step(ts, carry):
            prev, nxt = jnp.maximum(ts - 1, 0), jnp.minimum(ts + 1, hch - 1)
            states, tiles = [], []
            for p in range(pg):
                cols = slice(p * LANES, (p + 1) * LANES)
                s = s_ref[p]
                hist_ref[0, p, pl.ds(ts, 1)] = s[None]
                states.append(s)
                tiles.append(_round1(s * r_ref[prev, :, cols]))
                tiles.append(_spread(v_ref[nxt, :, cols], eye2))
            res = _ones_dot(tiles, ones_blk)
            for p in range(pg):
                cols = slice(p * LANES, (p + 1) * LANES)
                s = states[p]
                sa = _head_sums(s * kn_ref[ts, :, cols], first_head)
                s_ref[p] = s * w_ref[ts, :, cols] + sa * b_ref[ts, :, cols] + vb_ref[p] * k_ref[ts, :, cols]
            for p in range(pg):
                cols = slice(p * LANES, (p + 1) * LANES)
                y_ref[prev, :, cols] = diag(res[2 * p])
                vb_ref[p] = res[2 * p + 1]
            return carry

        real = pl.program_id(2) * hch < t_real
        lax.fori_loop(0, jnp.where(real, hch, 0), step, 0)
        last = _ones_dot([_round1(s_ref[p] * r_ref[hch - 1, :, cols]) for p, cols in enumerate(pair_cols)], ones_blk)
        for p, cols in enumerate(pair_cols):
            y_ref[hch - 1, :, cols] = diag(last[p])

        @pl.when(jnp.logical_not(real))
        def _():
            y_ref[...] = jnp.zeros_like(y_ref)
            hist_ref[...] = jnp.zeros_like(hist_ref)

        finish()

    row_spec = pl.BlockSpec((hch, 1, pg * LANES), lambda bb, g, c: (bb * nst + c, 0, g))
    hist_spec = pl.BlockSpec((1, pg, hch, RWKV_HEAD, LANES), lambda bb, g, c: (bb, g, c, 0, 0))
    hbm = pl.BlockSpec(memory_space=pl.ANY)
    rows3 = [a.reshape(bl * t, 1, d) for a in (r, w, k, kn, b, v)]
    y, hist, landed = pl.pallas_call(
        body, grid=grid, in_specs=[row_spec] * 6 + [hbm], out_specs=[row_spec, hist_spec, hbm],
        out_shape=[jax.ShapeDtypeStruct((bl * t, 1, d), F32), jax.ShapeDtypeStruct((bl, npair, t, RWKV_HEAD, LANES), F32),
                   jax.ShapeDtypeStruct((N_DEV,) + ride.shape, ride.dtype)],
        scratch_shapes=[pltpu.VMEM((pg, RWKV_HEAD, LANES), F32)] * 2 + _RIDE_SCRATCH,
        compiler_params=_params(("arbitrary", "arbitrary", "arbitrary")), name=name)(*rows3, ride)
    return y.reshape(bl * t, d), hist, landed


def scan_backward(r, w, k, kn, b, v, dy, hist, ride, bl, t, t_real, d, name, pg, hch):
    npair, nst = d // LANES, t // hch
    grid = (bl, npair // pg, nst)

    def body(r_ref, w_ref, k_ref, kn_ref, b_ref, v_ref, dy_ref, hist_ref, ride_ref,
             dr_ref, dw_ref, dk_ref, dkn_ref, db_ref, dv_ref, land_ref, ds_ref, cur_ref, *sems):
        start, finish = _riding_exchange(ride_ref, land_ref, *sems, True, grid)
        start()
        _, first_head, eye2, ones_blk = _scan_consts()
        eye2f = eye2.astype(F32)
        colsum = lambda x: jnp.sum(x, axis=0, keepdims=True)

        @pl.when(pl.program_id(2) == 0)
        def _():
            ds_ref[...] = jnp.zeros_like(ds_ref)

        tiles = []
        for p in range(pg):
            cols = slice(p * LANES, (p + 1) * LANES)
            tiles += [_spread(v_ref[hch - 1, :, cols], eye2), _spread(dy_ref[hch - 1, :, cols], eye2),
                      _split2(hist_ref[0, p, hch - 1] * kn_ref[hch - 1, :, cols])]
        first = _ones_dot(tiles, ones_blk)
        for p in range(pg):
            cols = slice(p * LANES, (p + 1) * LANES)
            row = lambda ref: ref[hch - 1, :, cols]
            s_prev = hist_ref[0, p, hch - 1]
            vb, dyb, sa = first[3 * p], first[3 * p + 1], first[3 * p + 2]
            cur_ref[0, p], cur_ref[1, p] = vb, sa
            dr_ref[hch - 1, :, cols] = colsum((s_prev * row(w_ref) + sa * row(b_ref) + vb * row(k_ref)) * dyb)
            ds_ref[p] += dyb * row(r_ref)

        def step(it, carry):
            ts = hch - 1 - it
            prev = jnp.maximum(ts - 1, 0)
            has_prev = ts > 0
            grads, tiles = [], []
            for p in range(pg):
                cols = slice(p * LANES, (p + 1) * LANES)
                ds = ds_ref[p]
                grads.append(ds)
                tiles.append(_spread(v_ref[prev, :, cols], eye2))
                tiles.append(_spread(dy_ref[prev, :, cols], eye2))
                tiles.append(_split2(hist_ref[0, p, pl.ds(prev, 1)][0] * kn_ref[prev, :, cols]))
                tiles.append(_round1(ds * k_ref[ts, :, cols]))
            res = _ones_dot(tiles, ones_blk)
            for p in range(pg):
                cols = slice(p * LANES, (p + 1) * LANES)
                row = lambda ref: ref[ts, :, cols]
                ds = grads[p]
                w_, kn_, b_ = row(w_ref), row(kn_ref), row(b_ref)
                dsa = _head_sums(ds * b_, first_head)
                s_prev = hist_ref[0, p, pl.ds(ts, 1)][0]
                vb, sa, dyb_prev = cur_ref[0, p], cur_ref[1, p], res[4 * p + 1]
                dk_ref[ts, :, cols] = colsum(ds * vb)
                db_ref[ts, :, cols] = colsum(ds * sa)
                dw_ref[ts, :, cols] = colsum(ds * s_prev)
                dkn_ref[ts, :, cols] = colsum(s_prev * dsa)
                dv_ref[ts, :, cols] = colsum(res[4 * p + 3] * eye2f)
                dr_ref[prev, :, cols] = jnp.where(has_prev, colsum(s_prev * dyb_prev), dr_ref[prev, :, cols])
                ds_ref[p] = ds * w_ + dsa * kn_ + jnp.where(has_prev, dyb_prev, 0.0) * r_ref[prev, :, cols]
            for p in range(pg):
                cur_ref[0, p] = res[4 * p]
                cur_ref[1, p] = res[4 * p + 2]
            return carry

        real = (nst - 1 - pl.program_id(2)) * hch < t_real
        lax.fori_loop(0, jnp.where(real, hch, 0), step, 0)

        @pl.when(jnp.logical_not(real))
        def _():
            for ref in (dr_ref, dw_ref, dk_ref, dkn_ref, db_ref, dv_ref):
                ref[...] = jnp.zeros_like(ref)

        finish()

    row_spec = pl.BlockSpec((hch, 1, pg * LANES), lambda bb, g, c: (bb * nst + nst - 1 - c, 0, g))
    hist_spec = pl.BlockSpec((1, pg, hch, RWKV_HEAD, LANES), lambda bb, g, c: (bb, g, nst - 1 - c, 0, 0))
    hbm = pl.BlockSpec(memory_space=pl.ANY)
    row_shape = jax.ShapeDtypeStruct((bl * t, 1, d), F32)
    rows3 = [a.reshape(bl * t, 1, d) for a in (r, w, k, kn, b, v, dy)]
    outs = pl.pallas_call(
        body, grid=grid, in_specs=[row_spec] * 7 + [hist_spec, hbm], out_specs=[row_spec] * 6 + [hbm],
        out_shape=[row_shape] * 6 + [jax.ShapeDtypeStruct(ride.shape, ride.dtype)],
        scratch_shapes=[pltpu.VMEM((pg, RWKV_HEAD, LANES), F32), pltpu.VMEM((2, pg, RWKV_HEAD, LANES), F32)] + _RIDE_SCRATCH,
        compiler_params=_params(("arbitrary", "arbitrary", "arbitrary")), name=name)(*rows3, hist, ride)
    return [o.reshape(bl * t, d) for o in outs[:6]] + [outs[6]]


def _mla_norms(pm, gq, gkv):
    ql = gq.shape[1]
    kvl = gkv.shape[1]
    return _rms(pm[:, :ql], gq), _rms(pm[:, ql:ql + kvl], gkv)


def mla_prep_fwd(pm, gq, gkv, name):
    n = pm.shape[0]
    return tilek(_mla_norms, [(pm, "r"), (gq, "f"), (gkv, "f")],
                 [("r", gq.shape[1], MMD), ("r", gkv.shape[1], MMD)], n_rows=n, tr=256, name=name)


def mla_prep_bwd(pm, gq, gkv, dcq, dckv, dkpe, name):
    n, wm = pm.shape
    ql, kvl = gq.shape[1], gkv.shape[1]

    def fn(pmv, gqv, gkvv, d1, d2, d3):
        _, vjp1 = jax.vjp(_rms, pmv[:, :ql], gqv)
        _, vjp2 = jax.vjp(_rms, pmv[:, ql:ql + kvl], gkvv)
        dcq_in, dgq = vjp1(d1)
        dckv_in, dgkv = vjp2(d2)
        return jnp.concatenate([dcq_in, dckv_in, d3], axis=1), dgq, dgkv

    return tilek(fn, [(pm, "r"), (gq, "f"), (gkv, "f"), (dcq, "r"), (dckv, "r"), (dkpe, "r")],
                 [("r", wm, F32), ("acc", gq.shape), ("acc", gkv.shape)], n_rows=n, tr=128, name=name)


def _rope(x, c, s, first):
    sw = jnp.where(first, pltpu.roll(x, LANES - ROPE_DIM // 2, 1), pltpu.roll(x, ROPE_DIM // 2, 1))
    return x * c + sw * s


def _unrope(d, c, s, first):
    z = d * s
    sw = jnp.where(first, pltpu.roll(z, LANES - ROPE_DIM // 2, 1), pltpu.roll(z, ROPE_DIM // 2, 1))
    return d * c + sw


def _causal_segments(n_tiles, parts=4):
    bounds = sorted({round(n_tiles * s / parts) for s in range(parts + 1)})
    return list(zip(bounds[:-1], bounds[1:]))


def attn_fwd(q, kv, pm, ct, st, bl, t, hm, name):
    n = q.shape[0]
    tq = LANES
    scale = QK_DIM ** -0.5
    kpe_blk = pm.shape[1] // LANES - 1

    def body(qn_ref, qpe_ref, kn_ref, v_ref, kpe_ref, ct_ref, st_ref, o_ref, lse_ref, kp_s, kn_s, v_s):
        h = pl.program_id(1)
        lane = lax.broadcasted_iota(jnp.int32, (1, LANES), 1)
        first = (lane & (ROPE_DIM - 1)) < ROPE_DIM // 2
        kp = _rope(kpe_ref[...], ct_ref[...], st_ref[...], first)
        kp_s[...] = jnp.where(h % 2 == 0, kp, pltpu.roll(kp, ROPE_DIM, 1)).astype(MMD)
        kn_s[...] = kn_ref[...].astype(MMD)
        v_s[...] = v_ref[...].astype(MMD)
        def segment(lo, hi):
            ext = hi * tq
            kpos = lax.broadcasted_iota(jnp.int32, (1, ext), 1)

            def qtile(i, carry):
                rows = pl.ds(pl.multiple_of(i * tq, tq), tq)
                q2 = _rope(qpe_ref[rows, :], ct_ref[rows, :], st_ref[rows, :], first)
                s = (_mm(qn_ref[rows, :], kn_s[:ext, :], ((1,), (1,))) + _mm(q2, kp_s[:ext, :], ((1,), (1,)))) * scale
                qpos = i * tq + lax.broadcasted_iota(jnp.int32, (tq, 1), 0)
                s = jnp.where(kpos <= qpos, s, -1e30)
                m = jnp.max(s, axis=1, keepdims=True)
                p = jnp.exp(s - m)
                l = jnp.sum(p, axis=1, keepdims=True)
                o_ref[rows, :] = _mm(p, v_s[:ext, :]) / l
                lse_ref[0, 0, rows, :] = m + jnp.log(l)
                return carry

            lax.fori_loop(lo, hi, qtile, 0)

        for lo, hi in _causal_segments(t // tq):
            segment(lo, hi)

    blk = lambda f: pl.BlockSpec((t, LANES), f)
    return pl.pallas_call(
        body, grid=(bl, hm),
        in_specs=[blk(lambda b, h: (b, h)), blk(lambda b, h: (b, hm + h // 2)), blk(lambda b, h: (b, h)),
                  blk(lambda b, h: (b, hm + h)), blk(lambda b, h: (b, kpe_blk)), blk(lambda b, h: (0, 0)), blk(lambda b, h: (0, 0))],
        out_specs=[blk(lambda b, h: (b, h)), pl.BlockSpec((1, 1, t, 1), lambda b, h: (b, h, 0, 0))],
        out_shape=[jax.ShapeDtypeStruct((n, hm * LANES), F32), jax.ShapeDtypeStruct((bl, hm, t, 1), F32)],
        scratch_shapes=[pltpu.VMEM((t, LANES), MMD)] * 3,
        compiler_params=_params(("parallel", "arbitrary")), name=name)(q, q, kv, kv, pm, ct, st)


def attn_bwd(q, kv, pm, o, do, lse, ct, st, bl, t, hm, name):
    n = q.shape[0]
    tq = LANES
    scale = QK_DIM ** -0.5
    kpe_blk = pm.shape[1] // LANES - 1

    def body(qn_ref, qpe_ref, kn_ref, v_ref, kpe_ref, o_ref, do_ref, lse_ref, ct_ref, st_ref,
             dqn_ref, dqpe_ref, dkn_ref, dv_ref, dkpe_ref, kp_s, kn_s, v_s, dkn_s, dkp_s, dv_s):
        h = pl.program_id(1)
        lane = lax.broadcasted_iota(jnp.int32, (1, LANES), 1)
        first = (lane & (ROPE_DIM - 1)) < ROPE_DIM // 2
        mine = (lane // ROPE_DIM) == (h % 2)
        kp = _rope(kpe_ref[...], ct_ref[...], st_ref[...], first)
        kp_s[...] = jnp.where(h % 2 == 0, kp, pltpu.roll(kp, ROPE_DIM, 1)).astype(MMD)
        kn_s[...] = kn_ref[...].astype(MMD)
        v_s[...] = v_ref[...].astype(MMD)
        dkn_s[...] = jnp.zeros_like(dkn_s)
        dkp_s[...] = jnp.zeros_like(dkp_s)
        dv_s[...] = jnp.zeros_like(dv_s)
        @pl.when(h % 2 == 0)
        def _():
            dqpe_ref[...] = jnp.zeros_like(dqpe_ref)

        @pl.when(h == 0)
        def _():
            dkpe_ref[...] = jnp.zeros_like(dkpe_ref)

        def segment(lo, hi):
            ext = hi * tq
            kpos = lax.broadcasted_iota(jnp.int32, (1, ext), 1)

            def qtile(i, carry):
                rows = pl.ds(pl.multiple_of(i * tq, tq), tq)
                c_i, s_i = ct_ref[rows, :], st_ref[rows, :]
                q1 = qn_ref[rows, :].astype(MMD)
                q2 = _rope(qpe_ref[rows, :], c_i, s_i, first).astype(MMD)
                s = (_mm(q1, kn_s[:ext, :], ((1,), (1,))) + _mm(q2, kp_s[:ext, :], ((1,), (1,)))) * scale
                qpos = i * tq + lax.broadcasted_iota(jnp.int32, (tq, 1), 0)
                p = jnp.where(kpos <= qpos, jnp.exp(s - lse_ref[0, 0, rows, :]), 0.0)
                do_i = do_ref[rows, :]
                delta = jnp.sum(do_i * o_ref[rows, :], axis=1, keepdims=True)
                dp = _mm(do_i, v_s[:ext, :], ((1,), (1,)))
                ds = (p * (dp - delta) * scale).astype(MMD)
                dqn_ref[rows, :] = _mm(ds, kn_s[:ext, :])
                dq2 = jnp.where(mine, _mm(ds, kp_s[:ext, :]), 0.0)
                dqpe_ref[rows, :] += _unrope(dq2, c_i, s_i, first)
                dkn_s[:ext, :] += _mm(ds, q1, ((0,), (0,)))
                dkp_s[:ext, :] += _mm(ds, q2, ((0,), (0,)))
                dv_s[:ext, :] += _mm(p, do_i, ((0,), (0,)))
                return carry

            lax.fori_loop(lo, hi, qtile, 0)

        for lo, hi in _causal_segments(t // tq):
            segment(lo, hi)
        dkn_ref[...] = dkn_s[...]
        dv_ref[...] = dv_s[...]
        dkp = jnp.where(mine, dkp_s[...], 0.0)
        dkp = jnp.where(h % 2 == 0, dkp, pltpu.roll(dkp, ROPE_DIM, 1))
        dkpe_ref[...] += _unrope(dkp, ct_ref[...], st_ref[...], first)

    blk = lambda f: pl.BlockSpec((t, LANES), f)
    hd = lambda b, h: (b, h)
    shp = lambda wd: jax.ShapeDtypeStruct((n, wd), F32)
    return pl.pallas_call(
        body, grid=(bl, hm),
        in_specs=[blk(hd), blk(lambda b, h: (b, hm + h // 2)), blk(hd), blk(lambda b, h: (b, hm + h)),
                  blk(lambda b, h: (b, kpe_blk)), blk(hd), blk(hd), pl.BlockSpec((1, 1, t, 1), lambda b, h: (b, h, 0, 0)),
                  blk(lambda b, h: (0, 0)), blk(lambda b, h: (0, 0))],
        out_specs=[blk(hd), blk(lambda b, h: (b, h // 2)), blk(hd), blk(hd), blk(lambda b, h: (b, 0))],
        out_shape=[shp(hm * LANES), shp(hm * ROPE_DIM), shp(hm * LANES), shp(hm * LANES), shp(LANES)],
        scratch_shapes=[pltpu.VMEM((t, LANES), MMD)] * 3 + [pltpu.VMEM((t, LANES), F32)] * 3,
        compiler_params=_params(("parallel", "arbitrary")), name=name)(q, q, kv, kv, pm, o, do, lse, ct, st)


def _peer(k):
    mx, my, mc = lax.axis_index("x"), lax.axis_index("y"), lax.axis_index("c")
    px = 1 - mx if k & 4 else mx
    py = 1 - my if k & 2 else my
    pc = 1 - mc if k & 1 else mc
    return (px, py, pc), 4 * px + 2 * py + pc


def all_gather(x, name):
    def body(x_ref, o_ref, send_sems, recv_sems, local_sem):
        _, me = _peer(0)
        local = pltpu.make_async_copy(x_ref, o_ref.at[me], local_sem)
        local.start()
        copies = []
        for k in range(1, N_DEV):
            dev, _ = _peer(k)
            cp = pltpu.make_async_remote_copy(src_ref=x_ref, dst_ref=o_ref.at[me], send_sem=send_sems.at[k - 1],
                                              recv_sem=recv_sems.at[k - 1], device_id=dev, device_id_type=MESH)
            cp.start()
            copies.append(cp)
        for cp in copies:
            cp.wait()
        local.wait()

    return pl.pallas_call(
        body, in_specs=[pl.BlockSpec(memory_space=pl.ANY)], out_specs=pl.BlockSpec(memory_space=pl.ANY),
        out_shape=jax.ShapeDtypeStruct((N_DEV,) + x.shape, x.dtype),
        scratch_shapes=[pltpu.SemaphoreType.DMA((N_DEV - 1,)), pltpu.SemaphoreType.DMA((N_DEV - 1,)), pltpu.SemaphoreType.DMA],
        name=name)(x)


def all_to_all(x, name):
    def body(x_ref, o_ref, send_sems, recv_sems, local_sem):
        _, me = _peer(0)
        local = pltpu.make_async_copy(x_ref.at[me], o_ref.at[me], local_sem)
        local.start()
        copies = []
        for k in range(1, N_DEV):
            dev, idx = _peer(k)
            cp = pltpu.make_async_remote_copy(src_ref=x_ref.at[idx], dst_ref=o_ref.at[me], send_sem=send_sems.at[k - 1],
                                              recv_sem=recv_sems.at[k - 1], device_id=dev, device_id_type=MESH)
            cp.start()
            copies.append(cp)
        for cp in copies:
            cp.wait()
        local.wait()

    return pl.pallas_call(
        body, in_specs=[pl.BlockSpec(memory_space=pl.ANY)], out_specs=pl.BlockSpec(memory_space=pl.ANY),
        out_shape=jax.ShapeDtypeStruct(x.shape, x.dtype),
        scratch_shapes=[pltpu.SemaphoreType.DMA((N_DEV - 1,)), pltpu.SemaphoreType.DMA((N_DEV - 1,)), pltpu.SemaphoreType.DMA],
        name=name)(x)


def _chips():
    mx, my, mc = lax.axis_index("x"), lax.axis_index("y"), lax.axis_index("c")
    return (mx, my, mc), (mx, my, 1 - mc), [(1 - mx, my), (mx, 1 - my), (1 - mx, 1 - my)]


def all_gather_two_level(x, name):
    def body(x_ref, o_ref, send_sems, recv_sems, local_sem):
        me, sibling, chips = _chips()
        blk = lambda px, py, pc: o_ref.at[4 * px + 2 * py + pc]

        def copy(k, block, to, src=None):
            return pltpu.make_async_remote_copy(src_ref=blk(*block) if src is None else src, dst_ref=blk(*block),
                                                send_sem=send_sems.at[k], recv_sem=recv_sems.at[k], device_id=to,
                                                device_id_type=MESH)

        mine = pltpu.make_async_copy(x_ref, blk(*me), local_sem)
        mine.start()
        first = [copy(0, me, sibling, src=x_ref)] + [copy(1 + j, me, (*chip, me[2]), src=x_ref) for j, chip in enumerate(chips)]
        for cp in first:
            cp.start()
        passed = [copy(4 + j, (*chip, me[2]), sibling) for j, chip in enumerate(chips)]
        for j, chip in enumerate(chips):
            copy(1 + j, (*chip, me[2]), me).wait_recv()
            passed[j].start()
        copy(0, sibling, me).wait_recv()
        for j, chip in enumerate(chips):
            copy(4 + j, (*chip, 1 - me[2]), me).wait_recv()
        for cp in first + passed:
            cp.wait_send()
        mine.wait()

    return pl.pallas_call(
        body, in_specs=[pl.BlockSpec(memory_space=pl.ANY)], out_specs=pl.BlockSpec(memory_space=pl.ANY),
        out_shape=jax.ShapeDtypeStruct((N_DEV,) + x.shape, x.dtype),
        scratch_shapes=[pltpu.SemaphoreType.DMA((N_DEV - 1,)), pltpu.SemaphoreType.DMA((N_DEV - 1,)), pltpu.SemaphoreType.DMA],
        name=name)(x)


def exchange_sibling(x, name):
    def body(x_ref, o_ref, send_sems, recv_sems):
        me, sibling, _ = _chips()
        copies = []
        for q in range(N_DEV // 2):
            cp = pltpu.make_async_remote_copy(src_ref=x_ref.at[2 * q + 1 - me[2]], dst_ref=o_ref.at[q], send_sem=send_sems.at[q],
                                              recv_sem=recv_sems.at[q], device_id=sibling, device_id_type=MESH)
            cp.start()
            copies.append(cp)
        for cp in copies:
            cp.wait()

    return pl.pallas_call(
        body, in_specs=[pl.BlockSpec(memory_space=pl.ANY)], out_specs=pl.BlockSpec(memory_space=pl.ANY),
        out_shape=jax.ShapeDtypeStruct((N_DEV // 2,) + x.shape[1:], x.dtype),
        scratch_shapes=[pltpu.SemaphoreType.DMA((N_DEV // 2,)), pltpu.SemaphoreType.DMA((N_DEV // 2,))], name=name)(x)


def exchange_chips(x, name):
    def body(x_ref, o_ref, send_sems, recv_sems, local_sem):
        me, _, chips = _chips()
        here = 2 * me[0] + me[1]
        local = pltpu.make_async_copy(x_ref.at[here], o_ref.at[here], local_sem)
        local.start()
        copies = []
        for j, (px, py) in enumerate(chips):
            cp = pltpu.make_async_remote_copy(src_ref=x_ref.at[2 * px + py], dst_ref=o_ref.at[here], send_sem=send_sems.at[j],
                                              recv_sem=recv_sems.at[j], device_id=(px, py, me[2]), device_id_type=MESH)
            cp.start()
            copies.append(cp)
        for cp in copies:
            cp.wait()
        local.wait()

    return pl.pallas_call(
        body, in_specs=[pl.BlockSpec(memory_space=pl.ANY)], out_specs=pl.BlockSpec(memory_space=pl.ANY),
        out_shape=jax.ShapeDtypeStruct(x.shape, x.dtype),
        scratch_shapes=[pltpu.SemaphoreType.DMA((3,)), pltpu.SemaphoreType.DMA((3,)), pltpu.SemaphoreType.DMA], name=name)(x)


def add_blocks(a, b, name):
    q, r, c = a.shape
    tr = _tile(r, max(16, (2 << 20) // (c * a.dtype.itemsize)), 16)
    spec = pl.BlockSpec((1, tr, c), lambda i, j: (i, j, 0))

    def body(a_ref, b_ref, o_ref):
        o_ref[...] = (a_ref[...].astype(F32) + b_ref[...].astype(F32)).astype(o_ref.dtype)

    return pl.pallas_call(
        body, grid=(q, r // tr), in_specs=[spec, spec], out_specs=spec, out_shape=jax.ShapeDtypeStruct(a.shape, a.dtype),
        compiler_params=_params(("parallel", "parallel")), name=name)(a, b)


def reduce_scatter_two_level(x, tag):
    q = N_DEV // 2
    from_sibling = exchange_sibling(x, f"{tag}_sibling")
    mine = lax.dynamic_index_in_dim(x.reshape((q, 2) + x.shape[1:]), lax.axis_index("c"), axis=1, keepdims=False)
    chip_sums = add_blocks(mine, from_sibling, f"{tag}_pair_sum")
    return sum_blocks(exchange_chips(chip_sums, f"{tag}_chips"), f"{tag}_sum")


def sum_blocks(x, name):
    nb, r, c = x.shape
    tr = _tile(r, max(16, (4 << 20) // (nb * c * x.dtype.itemsize)), 16)

    def body(x_ref, o_ref):
        acc = x_ref[0].astype(F32)
        for i in range(1, nb):
            acc = acc + x_ref[i].astype(F32)
        o_ref[...] = acc

    return pl.pallas_call(
        body, grid=(r // tr,), in_specs=[pl.BlockSpec((nb, tr, c), lambda i: (0, i, 0))],
        out_specs=pl.BlockSpec((tr, c), lambda i: (i, 0)), out_shape=jax.ShapeDtypeStruct((r, c), F32),
        compiler_params=_params(("parallel",)), name=name)(x)


def _adamw(w, g, m, v):
    m = ADAM_B1 * m + (1.0 - ADAM_B1) * g
    v = ADAM_B2 * v + (1.0 - ADAM_B2) * jnp.square(g)
    m_hat = m / (1.0 - ADAM_B1 ** ADAM_STEP)
    v_hat = v / (1.0 - ADAM_B2 ** ADAM_STEP)
    delta = -ADAM_LR * (m_hat / (jnp.sqrt(v_hat) + ADAM_EPS) + ADAM_WD * w)
    return delta, m, v


def adamw(w, g, m, v, name):
    r, c = w.shape
    tr = _tile(r, 256, 8)
    spec = pl.BlockSpec((tr, c), lambda i: (i, 0))

    def body(w_ref, g_ref, m_ref, v_ref, d_ref, nm_ref, nv_ref):
        d_ref[...], nm_ref[...], nv_ref[...] = _adamw(w_ref[...], g_ref[...], m_ref[...], v_ref[...])

    return pl.pallas_call(
        body, grid=(r // tr,), in_specs=[spec] * 4, out_specs=[spec] * 3,
        out_shape=[jax.ShapeDtypeStruct((r, c), F32)] * 3, compiler_params=_params(("parallel",)), name=name)(w, g, m, v)


def batch_sum_rows(dh, bl, t, rows, name):
    d = dh.shape[1]

    def body(x_ref, o_ref):
        @pl.when(pl.program_id(0) == 0)
        def _():
            o_ref[...] = jnp.zeros_like(o_ref)

        o_ref[...] += x_ref[...]

    return pl.pallas_call(
        body, grid=(bl,), in_specs=[pl.BlockSpec((rows, d), lambda b: (b * (t // rows), 0))],
        out_specs=pl.BlockSpec((rows, d), lambda b: (0, 0)), out_shape=jax.ShapeDtypeStruct((rows, d), F32),
        compiler_params=_params(("arbitrary",)), name=name)(dh)


class Dims:
    def __init__(self, x, meta_full_cols, w_up, g_up, q_norm, kv_norm, d_ff):
        self.bl, self.seq, self.d = x.shape
        self.n_meta = 16
        self.t_real = self.n_meta + self.seq
        self.t = -(-self.t_real // LANES) * LANES
        self.n = self.bl * self.t
        self.f = d_ff
        self.wl, self.gl = w_up.shape[-2], g_up.shape[-2]
        self.ql, self.kvl = q_norm.shape[-1], kv_norm.shape[-1]
        self.hm = self.d // V_DIM
        self.in_cols = 5 * self.d + 2 * self.wl + self.gl + self.ql + self.kvl + ROPE_DIM


def _pad_cols(a, width):
    return jnp.pad(a, ((0, 0), (0, width - a.shape[1])))


def _pad_rows(a, rows):
    return jnp.pad(a, ((0, rows - a.shape[0]), (0, 0)))


def split_in(a, dm, axis=1):
    d, wl, gl, ql, kvl = dm.d, dm.wl, dm.gl, dm.ql, dm.kvl
    size = a.shape[axis]
    cut = lambda lo, hi: lax.slice_in_dim(a, min(lo, size), min(hi, size), axis=axis)

    def pad(p, width):
        cfg = [(0, 0)] * a.ndim
        cfg[axis] = (0, width - p.shape[axis])
        return jnp.pad(p, cfg)

    o = 3 * d
    lora = jnp.concatenate([pad(cut(o, o + wl), LANES), pad(cut(o + wl, o + 2 * wl), LANES),
                            cut(o + 2 * wl, o + 2 * wl + gl)], axis=axis)
    o += 2 * wl + gl
    mla = pad(cut(o, o + ql + kvl + ROPE_DIM), ql + kvl + LANES)
    o += ql + kvl + ROPE_DIM
    return dict(r=cut(0, d), k=cut(d, 2 * d), v=cut(2 * d, 3 * d), l=lora, m=mla, ga=cut(o, o + d), gb=cut(o + d, o + 2 * d))


def merge_in(g, dm, axis=1):
    wl, gl, ql, kvl = dm.wl, dm.gl, dm.ql, dm.kvl
    cut = lambda p, lo, hi: lax.slice_in_dim(p, lo, hi, axis=axis)
    l, m = g["l"], g["m"]
    return jnp.concatenate([g["r"], g["k"], g["v"], cut(l, 0, wl), cut(l, LANES, LANES + wl), cut(l, 2 * LANES, 2 * LANES + gl),
                            cut(m, 0, ql + kvl + ROPE_DIM), g["ga"], g["gb"]], axis=axis)


def split_uq(w, dm):
    w3 = w.reshape(w.shape[0], dm.hm, QK_DIM)
    return jnp.concatenate([w3[:, :, :NOPE_DIM].reshape(w.shape[0], -1), w3[:, :, NOPE_DIM:].reshape(w.shape[0], -1)], axis=1)


def merge_uq(gn, gp, dm):
    r = gn.shape[0]
    return jnp.concatenate([gn.reshape(r, dm.hm, NOPE_DIM), gp.reshape(r, dm.hm, ROPE_DIM)], axis=2).reshape(r, -1)


def split_ukv(w, dm):
    w3 = w.reshape(w.shape[0], dm.hm, NOPE_DIM + V_DIM)
    return jnp.concatenate([w3[:, :, :NOPE_DIM].reshape(w.shape[0], -1), w3[:, :, NOPE_DIM:].reshape(w.shape[0], -1)], axis=1)


def merge_ukv(gk, gv, dm):
    r = gk.shape[0]
    return jnp.concatenate([gk.reshape(r, dm.hm, NOPE_DIM), gv.reshape(r, dm.hm, V_DIM)], axis=2).reshape(r, -1)


def head_matrices(d):
    heads = d // RWKV_HEAD
    e = (np.arange(d)[:, None] // RWKV_HEAD == np.arange(LANES)[None, :]) & (np.arange(LANES)[None, :] < heads)
    return jnp.asarray(e, BF16), jnp.asarray(e.T, BF16)


def rope_tables(t):
    pos = jnp.arange(t, dtype=F32)
    inv_freq = 1.0 / (ROPE_THETA ** (jnp.arange(0, ROPE_DIM, 2, dtype=F32) / ROPE_DIM))
    ang = pos[:, None] * inv_freq[None, :]
    cos, sin = jnp.cos(ang), jnp.sin(ang)
    return jnp.tile(jnp.concatenate([cos, cos], axis=1), (1, 2)), jnp.tile(jnp.concatenate([-sin, sin], axis=1), (1, 2))


def local_step(dm, x, loss_target, meta, wt, late_shards, late_rows, sp):
    bl, t, n, d, hm = dm.bl, dm.t, dm.n, dm.d, dm.hm
    e, et = head_matrices(d)
    ct, st = rope_tables(t)
    padz = jnp.zeros((bl, t - dm.t_real, d), F32)
    h0 = jnp.concatenate([jnp.broadcast_to(meta[None], (bl, dm.n_meta, d)), x, padz], axis=1).reshape(n, d)
    tgt = jnp.concatenate([jnp.zeros((bl, dm.n_meta, d), F32), loss_target, padz], axis=1).reshape(n, d)
    tpos = jnp.arange(t)
    mask = jnp.tile(((tpos >= dm.n_meta) & (tpos < dm.t_real)).astype(F32), bl).reshape(n, 1)

    win = split_in(wt["w_in"], dm, axis=0)
    mu = split_in(sp["tm_mu"], dm)
    wq, wkv = split_uq(wt["w_uq"], dm), split_ukv(wt["w_ukv"], dm)
    prm = dict(w0=sp["w0"], a0=sp["a0"], k_k=sp["k_k"], k_a=sp["k_a"], gn_w=sp["gn_w"], gn_b=sp["gn_b"], r_k=sp["r_k"],
               w_up=_pad_rows(wt["w_up"], LANES).astype(F32), a_up=_pad_rows(wt["a_up"], LANES).astype(F32),
               g_up=wt["g_up"].astype(F32))

    h1, ffn1 = ffn_forward(h0, sp["ffn1_norm"], wt["ffn1_w_gate"], wt["ffn1_w_up"], wt["ffn1_w_down"], "ffn1")
    u = rms_fwd(h1, sp["mix_norm"], "mix_rms")
    proj = {key: matmul([(u, win[key])], "nt", name=f"proj_{key}") for key in win}
    sh = {key: lerp_fwd(proj[key], mu[key], bl, t, f"shift_{key}") for key in ("r", "k", "v", "l")}
    decay, kmod, kneg, bvec, gate = rwkv_prep_fwd(sh["k"], sh["l"], prm, e, et, "rwkv_prep")
    pairs = min(SCAN_PAIRS, d // LANES)
    y, hist, late_all = scan_forward(sh["r"], decay, kmod, kneg, bvec, sh["v"], late_shards, bl, t, dm.t_real, d, "wkv_scan",
                                     pairs, SCAN_FWD_STEPS)
    wt = dict(wt, **{key: late_all[:, lo:hi].reshape(-1, d) for key, lo, hi in zip(LATE, late_rows[:-1], late_rows[1:])})
    cqn, ckvn = mla_prep_fwd(proj["m"], sp["q_norm"], sp["kv_norm"], "mla_norms")
    q = matmul([(cqn, wq)], "nn", name="mla_q")
    kv = matmul([(ckvn, wkv)], "nn", name="mla_kv")
    o, lse = attn_fwd(q, kv, proj["m"], ct, st, bl, t, hm, "mla_attn")
    post_in = [y, sh["r"], kmod, sh["v"], gate, proj["ga"], proj["gb"], o]
    mix = rwkv_post_fwd(post_in, prm, e, et, "mix_gate")
    h2 = matmul([(mix, wt["w_out"])], "nn", res=h1, name="out_proj")
    h3, ffn2 = ffn_forward(h2, sp["ffn2_norm"], wt["ffn2_w_gate"], wt["ffn2_w_up"], wt["ffn2_w_down"], "ffn2")
    dh3, d_final, loss = loss_head(h3, tgt, mask, sp["final_norm"], "loss_head")

    gw, gs = {}, {"final_norm": d_final}
    dh2, gs["ffn2_norm"], gw["ffn2_w_gate"], gw["ffn2_w_up"], gw["ffn2_w_down"] = ffn_backward(
        dh3, h2, sp["ffn2_norm"], wt["ffn2_w_gate"], wt["ffn2_w_up"], wt["ffn2_w_down"], ffn2, "ffn2")
    dmix = matmul([(dh2, wt["w_out"])], "nt", name="out_proj_dx")
    gw["w_out"] = matmul([(mix, dh2)], "tn", out_dtype=MMD, name="out_proj_dw")
    late_grads = jnp.concatenate([gw.pop(key).reshape(N_DEV, hi - lo, d) for key, lo, hi in
                                  zip(LATE, late_rows[:-1], late_rows[1:])], axis=1).astype(MMD)
    (dy, dr_p, dkm_p, dv_p, dgate, dpga, dpgb, do, gs["gn_w"], gs["gn_b"], gs["r_k"]) = rwkv_post_bwd(
        post_in, prm, e, et, dmix, "mix_gate_bwd")
    dqn, dqpe, dkn, dv_att, dkpe = attn_bwd(q, kv, proj["m"], o, do, lse, ct, st, bl, t, hm, "mla_attn_bwd")
    nq = hm * NOPE_DIM
    dcqn = matmul([(dqn, wq[:, :nq])], "nt", name="mla_q_dx1")
    dcqn = matmul([(dqpe, wq[:, nq:])], "nt", res=dcqn, name="mla_q_dx2")
    gw["w_uq"] = merge_uq(matmul([(cqn, dqn)], "tn", name="mla_q_dw1"), matmul([(cqn, dqpe)], "tn", name="mla_q_dw2"), dm)
    dckvn = matmul([(dkn, wkv[:, :nq]), (dv_att, wkv[:, nq:])], "nt", name="mla_kv_dx", tk=1024)
    gw["w_ukv"] = merge_ukv(matmul([(ckvn, dkn)], "tn", name="mla_kv_dw1"), matmul([(ckvn, dv_att)], "tn", name="mla_kv_dw2"), dm)
    dproj = {"ga": dpga, "gb": dpgb}
    dproj["m"], gs["q_norm"], gs["kv_norm"] = mla_prep_bwd(proj["m"], sp["q_norm"], sp["kv_norm"], dcqn, dckvn, dkpe, "mla_norms_bwd")
    dr_s, ddecay, dk_s, dkneg, dbvec, dv_s, late_recv = scan_backward(
        sh["r"], decay, kmod, kneg, bvec, sh["v"], dy, hist, late_grads, bl, t, dm.t_real, d, "wkv_scan_bwd", pairs,
        SCAN_BWD_STEPS)
    late_sum = sum_blocks(late_recv, "sum_late")
    (dsh_k, dsh_l, gs["w0"], gs["a0"], gs["k_k"], gs["k_a"], g_wup, g_aup, gw["g_up"]) = rwkv_prep_bwd(
        sh["k"], sh["l"], prm, e, et, [ddecay, dk_s, dkm_p, dkneg, dbvec, dgate], "rwkv_prep_bwd")
    gw["w_up"], gw["a_up"] = g_wup[:dm.wl], g_aup[:dm.wl]
    dmu = {}
    for key, cts in (("r", [dr_s, dr_p]), ("k", [dsh_k]), ("v", [dv_s, dv_p]), ("l", [dsh_l])):
        dproj[key], dmu[key] = lerp_bwd(proj[key], mu[key], cts, bl, t, f"shift_{key}_bwd")
    zero_m = jnp.zeros((1, proj["m"].shape[1]), F32)
    gs["tm_mu"] = merge_in(dict(dmu, m=zero_m, ga=zero_m[:, :0], gb=zero_m[:, :0]), dm)[:, :3 * d + 2 * dm.wl + dm.gl]
    wide = ("r", "k", "v", "ga", "gb")
    du = matmul([(dproj[key], win[key]) for key in wide], "nn", name="proj_dx", tn=512, tk=512)
    du = matmul([(dproj["l"], win["l"])], "nn", res=du, name="proj_dx_l")
    du = matmul([(dproj["m"], win["m"])], "nn", res=du, name="proj_dx_m")
    gw["w_in"] = merge_in({key: matmul([(dproj[key], u)], "tn", out_dtype=MMD, name=f"proj_dw_{key}") for key in win},
                          dm, axis=0)
    dh1, gs["mix_norm"] = rms_bwd(h1, sp["mix_norm"], du, dh2, "mix_rms_bwd")
    dh0, gs["ffn1_norm"], gw["ffn1_w_gate"], gw["ffn1_w_up"], gw["ffn1_w_down"] = ffn_backward(
        dh1, h0, sp["ffn1_norm"], wt["ffn1_w_gate"], wt["ffn1_w_up"], wt["ffn1_w_down"], ffn1, "ffn1")
    grad_x = dh0.reshape(bl, t, d)[:, dm.n_meta:dm.t_real]
    dmeta = batch_sum_rows(dh0, bl, t, dm.n_meta, "meta_grad")
    return loss, grad_x, dmeta, gw, late_sum, gs


TRANSPOSED = ("ffn1_w_gate", "ffn1_w_up", "w_in", "ffn2_w_gate", "ffn2_w_up")
EARLY = ("ffn1_w_gate", "ffn1_w_up", "ffn1_w_down", "w_in")
LATE = ("w_out", "ffn2_w_gate", "ffn2_w_up", "ffn2_w_down")
NARROW = ("w_up", "a_up", "g_up", "w_uq", "w_ukv")
MATRICES = ("ffn1_w_gate", "ffn1_w_up", "ffn1_w_down", "w_in", "w_up", "a_up", "g_up", "w_uq", "w_ukv", "w_out",
            "ffn2_w_gate", "ffn2_w_up", "ffn2_w_down")
SMALL = ("ffn1_norm", "mix_norm", "tm_mu", "w0", "a0", "k_k", "k_a", "r_k", "gn_w", "gn_b", "q_norm", "kv_norm",
         "ffn2_norm", "final_norm")
WEIGHTS = ("meta_tokens", "ffn1_norm", "ffn1_w_gate", "ffn1_w_up", "ffn1_w_down", "mix_norm", "w_in", "tm_mu", "w0", "w_up",
           "a0", "a_up", "g_up", "k_k", "k_a", "r_k", "gn_w", "gn_b", "q_norm", "w_uq", "kv_norm", "w_ukv", "w_out",
           "ffn2_norm", "ffn2_w_gate", "ffn2_w_up", "ffn2_w_down", "final_norm")
PACK_COLS = 1024
PACK_ALIGN = 16 * PACK_COLS


def _pack(parts):
    offs, o = [], 0
    for p in parts:
        offs.append(o)
        o += p.shape[1]
    total = -(-o // PACK_ALIGN) * PACK_ALIGN
    flat = jnp.concatenate(list(parts) + [jnp.zeros((parts[0].shape[0], total - o), parts[0].dtype)], axis=1)
    return flat.reshape(parts[0].shape[0], total // PACK_COLS, PACK_COLS), offs


def kernel(x, meta_tokens, ffn1_norm, ffn1_w_gate, ffn1_w_up, ffn1_w_down, mix_norm, w_in, tm_mu, w0, w_up, a0, a_up, g_up, k_k, k_a, r_k, gn_w, gn_b, q_norm, w_uq, kv_norm, w_ukv, w_out, ffn2_norm, ffn2_w_gate, ffn2_w_up, ffn2_w_down, final_norm, loss_target, m_meta_tokens, m_ffn1_norm, m_ffn1_w_gate, m_ffn1_w_up, m_ffn1_w_down, m_mix_norm, m_w_in, m_tm_mu, m_w0, m_w_up, m_a0, m_a_up, m_g_up, m_k_k, m_k_a, m_r_k, m_gn_w, m_gn_b, m_q_norm, m_w_uq, m_kv_norm, m_w_ukv, m_w_out, m_ffn2_norm, m_ffn2_w_gate, m_ffn2_w_up, m_ffn2_w_down, m_final_norm, v_meta_tokens, v_ffn1_norm, v_ffn1_w_gate, v_ffn1_w_up, v_ffn1_w_down, v_mix_norm, v_w_in, v_tm_mu, v_w0, v_w_up, v_a0, v_a_up, v_g_up, v_k_k, v_k_a, v_r_k, v_gn_w, v_gn_b, v_q_norm, v_w_uq, v_kv_norm, v_w_ukv, v_w_out, v_ffn2_norm, v_ffn2_w_gate, v_ffn2_w_up, v_ffn2_w_down, v_final_norm):
    args = dict(locals())
    wts = {k: args[k] for k in WEIGHTS}
    ms = {k: args["m_" + k] for k in WEIGHTS}
    vs = {k: args["v_" + k] for k in WEIGHTS}
    dm = Dims(x, None, w_up, g_up, q_norm, kv_norm, ffn1_w_down.shape[1] * N_DEV)

    shard2d = {k: wts[k].reshape(wts[k].shape[-2], wts[k].shape[-1]) for k in MATRICES}
    sent = {k: shard2d[k].astype(MMD).T if k in TRANSPOSED else shard2d[k] for k in MATRICES}
    early_rows = np.cumsum([0] + [sent[k].shape[0] for k in EARLY])
    late_rows = np.cumsum([0] + [sent[k].shape[0] for k in LATE])
    got_early = all_gather_two_level(jnp.concatenate([sent[k].astype(MMD) for k in EARLY], axis=0), "gather_early")
    full = {k: got_early[:, lo:hi].reshape(-1, dm.d) for k, lo, hi in zip(EARLY, early_rows[:-1], early_rows[1:])}
    late_shards = jnp.concatenate([sent[k].astype(MMD) for k in LATE], axis=0)
    send, offs = _pack([sent[k].astype(MMD).reshape(1, -1) for k in NARROW])
    got = all_gather(send[0], "gather_narrow").reshape(N_DEV, -1)
    for k, o in zip(NARROW, offs):
        r, c = sent[k].shape
        full[k] = got[:, o:o + r * c].reshape(N_DEV, r, c).transpose(1, 0, 2).reshape(r, N_DEV * c)
    mr, mc = meta_tokens.shape
    meta = all_gather(meta_tokens, "gather_meta").transpose(1, 0, 2).reshape(mr, N_DEV * mc)
    small = {k: wts[k].reshape(1, -1) for k in SMALL}

    loss, grad_x, dmeta, gw, gsum_late, gs = local_step(dm, x, loss_target, meta, full, late_shards, late_rows, small)

    gearly = jnp.concatenate([gw[k].reshape(N_DEV, sent[k].shape[0], dm.d) for k in EARLY], axis=1).astype(MMD)
    gsum_early = reduce_scatter_two_level(gearly, "scatter_early")
    grads = {}
    for names, rows, gsum_rows in ((EARLY, early_rows, gsum_early), (LATE, late_rows, gsum_late)):
        for k, lo, hi in zip(names, rows[:-1], rows[1:]):
            grads[k] = gsum_rows[lo:hi].T if k in TRANSPOSED else gsum_rows[lo:hi]

    def blocks(k, g):
        r, c = sent[k].shape
        return g.reshape(r, N_DEV, c).transpose(1, 0, 2).reshape(N_DEV, r * c)

    gsend, goffs = _pack([blocks(k, gw[k]).astype(MMD) for k in NARROW]
                         + [dmeta.reshape(mr, N_DEV, mc).transpose(1, 0, 2).reshape(N_DEV, mr * mc).astype(MMD)])
    gsum = sum_blocks(all_to_all(gsend, "scatter_narrow"), "sum_narrow").reshape(-1)
    for k, o in zip(NARROW, goffs):
        r, c = sent[k].shape
        grads[k] = gsum[o:o + r * c].reshape(r, c)
    grads["meta_tokens"] = gsum[goffs[-1]:goffs[-1] + mr * mc].reshape(mr, mc)

    ssend, soffs = _pack([gs[k].reshape(1, -1) for k in SMALL] + [loss])
    ssum = sum_blocks(all_gather(ssend[0], "gather_small"), "sum_small").reshape(-1)
    for k, o in zip(SMALL, soffs):
        grads[k] = ssum[o:o + small[k].shape[1]]
    loss_total = ssum[soffs[-1]]

    delta, new_m, new_v = {}, {}, {}
    for k in MATRICES + ("meta_tokens",):
        shp = wts[k].shape
        to2d = lambda a: a.reshape(shp[-2], shp[-1])
        dlt, nm, nv = adamw(to2d(wts[k]), grads[k], to2d(ms[k]), to2d(vs[k]), f"adamw_{k}")
        delta[k], new_m[k], new_v[k] = dlt.reshape(shp), nm.reshape(shp), nv.reshape(shp)
        grads[k] = grads[k].reshape(shp)
    pw, _ = _pack([wts[k].reshape(1, -1) for k in SMALL])
    pm_, _ = _pack([ms[k].reshape(1, -1) for k in SMALL])
    pv, _ = _pack([vs[k].reshape(1, -1) for k in SMALL])
    pg, poffs = _pack([grads[k].reshape(1, -1) for k in SMALL])
    dlt, nm, nv = adamw(pw[0], pg[0], pm_[0], pv[0], "adamw_small")
    for k, o in zip(SMALL, poffs):
        shp, sz = wts[k].shape, small[k].shape[1]
        cut = lambda a: a.reshape(-1)[o:o + sz].reshape(shp)
        delta[k], new_m[k], new_v[k] = cut(dlt), cut(nm), cut(nv)
        grads[k] = grads[k].reshape(shp)

    return (loss_total, grad_x, *[grads[k] for k in WEIGHTS], *[delta[k] for k in WEIGHTS],
            *[new_m[k] for k in WEIGHTS], *[new_v[k] for k in WEIGHTS])
```

```python
import functools

import numpy as np
import jax
import jax.numpy as jnp
from jax import lax
from jax.experimental import pallas as pl
from jax.experimental.pallas import tpu as pltpu

F32 = jnp.float32
BF16 = jnp.bfloat16
MMD = BF16

NORM_EPS = 1e-6
RWKV_HEAD = 64
GN_EPS = RWKV_HEAD * 1e-5
NOPE_DIM = 128
ROPE_DIM = 64
V_DIM = 128
QK_DIM = NOPE_DIM + ROPE_DIM
ROPE_THETA = 10000.0
ADAM_LR = 0.001
ADAM_B1 = 0.9
ADAM_B2 = 0.999
ADAM_EPS = 1e-08
ADAM_WD = 0.01
ADAM_STEP = 10

LANES = 128
SCAN_PAIRS = 8
SCAN_FWD_STEPS = 32
SCAN_BWD_STEPS = 16
N_DEV = 8
VMEM_LIMIT = 56 * 1024 * 1024
MESH = pl.DeviceIdType.MESH


def _tile(n, target, align):
    best = None
    for d in range(align, min(n, target) + 1, align):
        if n % d == 0:
            best = d
    return best if best is not None else n


def _params(sem=None):
    return pltpu.CompilerParams(dimension_semantics=sem, vmem_limit_bytes=VMEM_LIMIT)


def _mm(a, b, dims=((1,), (0,))):
    return lax.dot_general(a.astype(MMD), b.astype(MMD), (dims, ((), ())), preferred_element_type=F32)


@jax.custom_vjp
def mmdot(a, b):
    return _mm(a, b)


def _mmdot_fwd(a, b):
    return _mm(a, b), (a, b)


def _mmdot_bwd(res, g):
    a, b = res
    return _mm(g, b, ((1,), (1,))).astype(a.dtype), _mm(a, g, ((0,), (0,))).astype(b.dtype)


mmdot.defvjp(_mmdot_fwd, _mmdot_bwd)


def _dot2(x, m):
    hi = x.astype(BF16)
    lo = (x - hi.astype(F32)).astype(BF16)
    return (lax.dot_general(hi, m, (((1,), (0,)), ((), ())), preferred_element_type=F32)
            + lax.dot_general(lo, m, (((1,), (0,)), ((), ())), preferred_element_type=F32))


@jax.custom_vjp
def segsum(x, e, et):
    return _dot2(_dot2(x, e), et)


def _segsum_fwd(x, e, et):
    return segsum(x, e, et), (e, et)


def _segsum_bwd(res, g):
    e, et = res
    return segsum(g, e, et), jnp.zeros_like(e), jnp.zeros_like(et)


segsum.defvjp(_segsum_fwd, _segsum_bwd)


def _sigmoid(x):
    return 1.0 / (1.0 + jnp.exp(-x))


def _softplus(x):
    return jnp.maximum(x, 0.0) + jnp.log(1.0 + jnp.exp(-jnp.abs(x)))


def _rms(x, g):
    return x * lax.rsqrt(jnp.mean(x * x, axis=-1, keepdims=True) + NORM_EPS) * g


_DIMS = {"nn": ((1,), (0,)), "nt": ((1,), (1,)), "tn": ((0,), (0,))}


def matmul(pairs, mode, *, name, out_dtype=F32, res=None, alpha=1.0, tm=1088, tn=1024, tk=2048):
    a0, b0 = pairs[0]
    if mode == "nn":
        (m, k), n = a0.shape, b0.shape[1]
    elif mode == "nt":
        (m, k), n = a0.shape, b0.shape[0]
    else:
        (k, m), n = a0.shape, b0.shape[1]
    tm = _tile(m, 1408, 128) if mode == "tn" else _tile(m, tm, 16)
    tn = _tile(n, 2048 if mode == "tn" else tn, 128)
    tk = _tile(k, min(tk, 1024), 16) if mode == "tn" else _tile(k, tk, 128)
    nk = k // tk
    npair = len(pairs)
    if mode == "tn":
        a_spec = pl.BlockSpec((tk, tm), lambda i, j, kk: (kk, i))
    else:
        a_spec = pl.BlockSpec((tm, tk), lambda i, j, kk: (i, kk))
    if mode == "nt":
        b_spec = pl.BlockSpec((tn, tk), lambda i, j, kk: (j, kk))
    else:
        b_spec = pl.BlockSpec((tk, tn), lambda i, j, kk: (kk, j))
    o_spec = pl.BlockSpec((tm, tn), lambda i, j, kk: (i, j))
    dims = _DIMS[mode]

    def body(*refs):
        ab = refs[:2 * npair]
        res_ref = refs[2 * npair] if res is not None else None
        o_ref, acc_ref = refs[-2], refs[-1]
        kk = pl.program_id(2)

        @pl.when(kk == 0)
        def _():
            acc_ref[...] = jnp.zeros_like(acc_ref)

        part = _mm(ab[0][...], ab[1][...], dims)
        for p in range(1, npair):
            part = part + _mm(ab[2 * p][...], ab[2 * p + 1][...], dims)
        acc_ref[...] += part

        @pl.when(kk == nk - 1)
        def _():
            out = acc_ref[...] * alpha if alpha != 1.0 else acc_ref[...]
            if res_ref is not None:
                out = res_ref[...].astype(F32) + out
            o_ref[...] = out.astype(o_ref.dtype)

    args, specs = [], []
    for a, b in pairs:
        args += [a, b]
        specs += [a_spec, b_spec]
    if res is not None:
        args.append(res)
        specs.append(o_spec)
    return pl.pallas_call(
        body, grid=(m // tm, n // tn, nk), in_specs=specs, out_specs=o_spec,
        out_shape=jax.ShapeDtypeStruct((m, n), out_dtype), scratch_shapes=[pltpu.VMEM((tm, tn), F32)],
        compiler_params=_params(("parallel", "parallel", "arbitrary")), name=name)(*args)


def tilek(fn, ins, outs, *, n_rows, tr, name):
    tr = _tile(n_rows, tr, 16)
    n_in = len(ins)
    in_specs = []
    for arr, kind in ins:
        if kind == "r":
            in_specs.append(pl.BlockSpec((tr, arr.shape[1]), lambda i: (i, 0)))
        else:
            in_specs.append(pl.BlockSpec(arr.shape, lambda i, nd=arr.ndim: (0,) * nd))
    out_specs, out_shape = [], []
    has_acc = False
    for o in outs:
        if o[0] == "r":
            out_specs.append(pl.BlockSpec((tr, o[1]), lambda i: (i, 0)))
            out_shape.append(jax.ShapeDtypeStruct((n_rows, o[1]), o[2]))
        else:
            has_acc = True
            out_specs.append(pl.BlockSpec(o[1], lambda i, nd=len(o[1]): (0,) * nd))
            out_shape.append(jax.ShapeDtypeStruct(o[1], F32))

    def body(*refs):
        i = pl.program_id(0)
        vals = fn(*[r[...] for r in refs[:n_in]])
        for o, r, v in zip(outs, refs[n_in:], vals):
            if o[0] == "r":
                r[...] = v.astype(r.dtype)
            else:
                @pl.when(i == 0)
                def _(r=r):
                    r[...] = jnp.zeros_like(r)

                r[...] += v

    return pl.pallas_call(
        body, grid=(n_rows // tr,), in_specs=in_specs, out_specs=out_specs, out_shape=out_shape,
        compiler_params=_params(("arbitrary",) if has_acc else ("parallel",)), name=name)(*[a for a, _ in ins])


def rms_fwd(x, g, name):
    n, d = x.shape
    return tilek(lambda xv, gv: (_rms(xv, gv),), [(x, "r"), (g, "f")], [("r", d, MMD)], n_rows=n, tr=256, name=name)[0]


def rms_bwd(x, g, dy, dres, name):
    n, d = x.shape

    def fn(xv, gv, dyv, drv):
        _, vjp = jax.vjp(_rms, xv, gv)
        dx, dg = vjp(dyv.astype(F32))
        return drv + dx, dg

    return tilek(fn, [(x, "r"), (g, "f"), (dy, "r"), (dres, "r")], [("r", d, F32), ("acc", (1, d))],
                 n_rows=n, tr=128, name=name)


def loss_head(h, tgt, mask, g, name):
    n, d = h.shape

    def fn(hv, tv, mv, gv):
        def lossf(hh, gg):
            e = (_rms(hh, gg) - tv) * mv
            s = jnp.sum(jnp.sum(e * e, axis=1, keepdims=True), axis=0, keepdims=True)
            return s * (0.5 / d)

        l, vjp = jax.vjp(lossf, hv, gv)
        dh, dg = vjp(jnp.ones((1, 1), F32))
        return dh, dg, jnp.broadcast_to(l, (1, LANES))

    return tilek(fn, [(h, "r"), (tgt, "r"), (mask, "r"), (g, "f")],
                 [("r", d, F32), ("acc", (1, d)), ("acc", (1, LANES))], n_rows=n, tr=128, name=name)


def ffn_up(hn, wg, wu, name):
    n, d = hn.shape
    f = wg.shape[0]
    tm, tn = _tile(n, 544, 16), _tile(f, 1408, 128)

    def body(a_ref, g_ref, u_ref, og_ref, ou_ref, oa_ref):
        a = a_ref[...]
        g = _mm(a, g_ref[...], ((1,), (1,)))
        u = _mm(a, u_ref[...], ((1,), (1,)))
        og_ref[...] = g.astype(og_ref.dtype)
        ou_ref[...] = u.astype(ou_ref.dtype)
        oa_ref[...] = (g * _sigmoid(g) * u).astype(oa_ref.dtype)

    o_spec = pl.BlockSpec((tm, tn), lambda i, j: (i, j))
    w_spec = pl.BlockSpec((tn, d), lambda i, j: (j, 0))
    return pl.pallas_call(
        body, grid=(n // tm, f // tn), in_specs=[pl.BlockSpec((tm, d), lambda i, j: (i, 0)), w_spec, w_spec],
        out_specs=[o_spec, o_spec, o_spec],
        out_shape=[jax.ShapeDtypeStruct((n, f), MMD)] * 3,
        compiler_params=_params(("parallel", "parallel")), name=name)(hn, wg, wu)


def ffn_down_bwd(dh, wd, gate, up, name):
    n, d = dh.shape
    f = wd.shape[0]
    tm, tn = _tile(n, 544, 16), _tile(f, 1408, 128)

    def body(dh_ref, w_ref, g_ref, u_ref, dg_ref, du_ref):
        da = 0.5 * _mm(dh_ref[...], w_ref[...], ((1,), (1,)))
        g, u = g_ref[...].astype(F32), u_ref[...].astype(F32)
        s = _sigmoid(g)
        dg_ref[...] = (da * u * (s * (1.0 + g * (1.0 - s)))).astype(dg_ref.dtype)
        du_ref[...] = (da * (g * s)).astype(du_ref.dtype)

    o_spec = pl.BlockSpec((tm, tn), lambda i, j: (i, j))
    return pl.pallas_call(
        body, grid=(n // tm, f // tn),
        in_specs=[pl.BlockSpec((tm, d), lambda i, j: (i, 0)), pl.BlockSpec((tn, d), lambda i, j: (j, 0)), o_spec, o_spec],
        out_specs=[o_spec, o_spec],
        out_shape=[jax.ShapeDtypeStruct((n, f), MMD), jax.ShapeDtypeStruct((n, f), MMD)],
        compiler_params=_params(("parallel", "parallel")), name=name)(dh, wd, gate, up)


def ffn_forward(h, g, wg, wu, wd, tag):
    hn = rms_fwd(h, g, f"{tag}_rms")
    gate, up, act = ffn_up(hn, wg, wu, f"{tag}_up")
    out = matmul([(act, wd)], "nn", res=h, alpha=0.5, name=f"{tag}_down")
    return out, (hn, gate, up, act)


def ffn_backward(dout, h, g, wg, wu, wd, saved, tag):
    hn, gate, up, act = saved
    dgate, dup = ffn_down_bwd(dout, wd, gate, up, f"{tag}_dact")
    dwd = matmul([(act, dout)], "tn", alpha=0.5, out_dtype=MMD, name=f"{tag}_dwd")
    dwg = matmul([(dgate, hn)], "tn", out_dtype=MMD, name=f"{tag}_dwg")
    dwu = matmul([(dup, hn)], "tn", out_dtype=MMD, name=f"{tag}_dwu")
    dhn = matmul([(dgate, wg), (dup, wu)], "nn", name=f"{tag}_dhn")
    dh, dg = rms_bwd(h, g, dhn, dout, f"{tag}_drms")
    return dh, dg, dwg, dwu, dwd


def lerp_fwd(p, mu, bl, t, name):
    n, w = p.shape
    cb = _tile(w, 256, 128)

    def body(p_ref, mu_ref, o_ref):
        x = p_ref[...]
        row = lax.broadcasted_iota(jnp.int32, x.shape, 0)
        prev = jnp.where(row == 0, 0.0, pltpu.roll(x, 1, 0))
        o_ref[...] = x + mu_ref[...] * (prev - x)

    spec = pl.BlockSpec((t, cb), lambda b, j: (b, j))
    return pl.pallas_call(
        body, grid=(bl, w // cb), in_specs=[spec, pl.BlockSpec((1, cb), lambda b, j: (0, j))], out_specs=spec,
        out_shape=jax.ShapeDtypeStruct((n, w), F32), compiler_params=_params(("parallel", "parallel")), name=name)(p, mu)


def lerp_bwd(p, mu, douts, bl, t, name):
    n, w = p.shape
    cb = _tile(w, 256, 128)
    nd = len(douts)

    def body(*refs):
        p_ref, mu_ref = refs[0], refs[1]
        dp_ref, dmu_ref = refs[2 + nd], refs[3 + nd]
        b = pl.program_id(1)
        x, m = p_ref[...], mu_ref[...]
        d = refs[2][...]
        for r in refs[3:2 + nd]:
            d = d + r[...]
        row = lax.broadcasted_iota(jnp.int32, x.shape, 0)
        prev = jnp.where(row == 0, 0.0, pltpu.roll(x, 1, 0))
        z = d * m
        nxt = jnp.where(row == t - 1, 0.0, pltpu.roll(z, t - 1, 0))
        dp_ref[...] = d - z + nxt

        @pl.when(b == 0)
        def _():
            dmu_ref[...] = jnp.zeros_like(dmu_ref)

        dmu_ref[...] += jnp.sum(d * (prev - x), axis=0, keepdims=True)

    spec = pl.BlockSpec((t, cb), lambda j, b: (b, j))
    cspec = pl.BlockSpec((1, cb), lambda j, b: (0, j))
    return pl.pallas_call(
        body, grid=(w // cb, bl), in_specs=[spec, cspec] + [spec] * nd, out_specs=[spec, cspec],
        out_shape=[jax.ShapeDtypeStruct((n, w), F32), jax.ShapeDtypeStruct((1, w), F32)],
        compiler_params=_params(("parallel", "arbitrary")), name=name)(p, mu, *douts)


def _prep(k, xw, xa, xg, w0, a0, k_k, k_a, w_up, a_up, g_up, e, et):
    w_pre = -_softplus(-(w0 + mmdot(jnp.tanh(xw), w_up))) - 0.5
    decay = jnp.exp(-jnp.exp(w_pre))
    a = _sigmoid(a0 + mmdot(xa, a_up))
    g = mmdot(_sigmoid(xg), g_up)
    kk = k * k_k
    kk = kk * lax.rsqrt(jnp.maximum(segsum(kk * kk, e, et), 1e-24))
    kmod = k * (1.0 + (a - 1.0) * k_a)
    return decay, kmod, -kk, kk * a, g


def _lora_parts(xl):
    return xl[:, :LANES], xl[:, LANES:2 * LANES], xl[:, 2 * LANES:]


def rwkv_prep_fwd(pk, pl_, prm, e, et, name):
    n, d = pk.shape
    small = [prm[k] for k in ("w0", "a0", "k_k", "k_a", "w_up", "a_up", "g_up")]
    ins = [(pk, "r"), (pl_, "r")] + [(s, "f") for s in small] + [(e, "f"), (et, "f")]
    return tilek(lambda k, xl, *rest: _prep(k, *_lora_parts(xl), *rest), ins, [("r", d, F32)] * 5, n_rows=n, tr=128, name=name)


def rwkv_prep_bwd(pk, pl_, prm, e, et, cts, name):
    n, d = pk.shape
    small = [prm[k] for k in ("w0", "a0", "k_k", "k_a", "w_up", "a_up", "g_up")]

    def fn(k, xl, w0, a0, k_k, k_a, w_up, a_up, g_up, ev, etv, dw, dkm1, dkm2, dkn, db, dg):
        _, vjp = jax.vjp(lambda *a: _prep(*a, ev, etv), k, *_lora_parts(xl), w0, a0, k_k, k_a, w_up, a_up, g_up)
        dk, dxw, dxa, dxg, *dsmall = vjp((dw, dkm1 + dkm2, dkn, db, dg))
        return (dk, jnp.concatenate([dxw, dxa, dxg], axis=1), *dsmall)

    ins = [(pk, "r"), (pl_, "r")] + [(s, "f") for s in small] + [(e, "f"), (et, "f")] + [(c, "r") for c in cts]
    outs = [("r", d, F32), ("r", pl_.shape[1], F32)] + [("acc", s.shape) for s in small]
    return tilek(fn, ins, outs, n_rows=n, tr=64, name=name)


def _post(y, r, km, v, g, pga, pgb, yb, gn_w, gn_b, r_k, e, et):
    inv = 1.0 / RWKV_HEAD
    yc = y - segsum(y, e, et) * inv
    var = segsum(yc * yc, e, et) * inv
    yn = yc * lax.rsqrt(var + GN_EPS) * gn_w + gn_b
    bonus = segsum(r * km * r_k, e, et) * v
    ya = (yn + bonus) * g
    return _sigmoid(pga) * ya + _sigmoid(pgb) * yb


def rwkv_post_fwd(acts, prm, e, et, name):
    n, d = acts[0].shape
    small = [prm[k] for k in ("gn_w", "gn_b", "r_k")]
    ins = [(a, "r") for a in acts] + [(s, "f") for s in small] + [(e, "f"), (et, "f")]
    return tilek(lambda *a: (_post(*a),), ins, [("r", d, MMD)], n_rows=n, tr=128, name=name)[0]


def rwkv_post_bwd(acts, prm, e, et, dm, name):
    n, d = acts[0].shape
    small = [prm[k] for k in ("gn_w", "gn_b", "r_k")]
    na = len(acts)

    def fn(*a):
        prim, ev, etv, dmv = a[:na + 3], a[na + 3], a[na + 4], a[na + 5]
        _, vjp = jax.vjp(lambda *z: _post(*z, ev, etv), *prim)
        return vjp(dmv.astype(F32))

    ins = [(x, "r") for x in acts] + [(s, "f") for s in small] + [(e, "f"), (et, "f"), (dm, "r")]
    outs = [("r", d, F32)] * na + [("acc", s.shape) for s in small]
    return tilek(fn, ins, outs, n_rows=n, tr=64, name=name)


def _head_sums(x, first_head):
    a = jnp.sum(jnp.where(first_head, x, 0.0), axis=1, keepdims=True)
    b = jnp.sum(jnp.where(first_head, 0.0, x), axis=1, keepdims=True)
    return jnp.where(first_head, a, b)


def _round1(x):
    return (x.astype(BF16), None) if MMD == BF16 else _split2(x)


def _split2(x):
    hi = x.astype(BF16)
    return hi, (x - hi.astype(F32)).astype(BF16)


def _spread(row, eye2):
    hi, lo = _split2(row)
    return eye2 * hi, eye2 * lo


def _ones_dot(tiles, ones_blk):
    dims = (((1,), (0,)), ((), ()))
    res = lax.dot_general(jnp.concatenate([t[0] for t in tiles], axis=0), ones_blk, dims, preferred_element_type=F32)
    out = [res[i * RWKV_HEAD:(i + 1) * RWKV_HEAD] for i in range(len(tiles))]
    two_term = [i for i, t in enumerate(tiles) if t[1] is not None]
    if two_term:
        low = lax.dot_general(jnp.concatenate([tiles[i][1] for i in two_term], axis=0), ones_blk, dims,
                              preferred_element_type=F32)
        for n, i in enumerate(two_term):
            out[i] = out[i] + low[n * RWKV_HEAD:(n + 1) * RWKV_HEAD]
    return out


def _scan_consts():
    lane = lax.broadcasted_iota(jnp.int32, (1, LANES), 1)
    rows = lax.broadcasted_iota(jnp.int32, (RWKV_HEAD, LANES), 0)
    cols = lax.broadcasted_iota(jnp.int32, (RWKV_HEAD, LANES), 1)
    eye2 = ((cols & (RWKV_HEAD - 1)) == rows).astype(BF16)
    r2 = lax.broadcasted_iota(jnp.int32, (LANES, LANES), 0)
    c2 = lax.broadcasted_iota(jnp.int32, (LANES, LANES), 1)
    ones_blk = ((r2 // RWKV_HEAD) == (c2 // RWKV_HEAD)).astype(BF16)
    return lane, lane < RWKV_HEAD, eye2, ones_blk


def _direct_copies(src_ref, dst_ref, send_sems, recv_sems, local_sem, scatter):
    _, me = _peer(0)
    out = [pltpu.make_async_copy(src_ref.at[me] if scatter else src_ref, dst_ref.at[me], local_sem)]
    for k in range(1, N_DEV):
        dev, idx = _peer(k)
        out.append(pltpu.make_async_remote_copy(src_ref=src_ref.at[idx] if scatter else src_ref, dst_ref=dst_ref.at[me],
                                                send_sem=send_sems.at[k - 1], recv_sem=recv_sems.at[k - 1],
                                                device_id=dev, device_id_type=MESH))
    return out


def _riding_exchange(src_ref, dst_ref, send_sems, recv_sems, local_sem, scatter, grid):
    copies = lambda: _direct_copies(src_ref, dst_ref, send_sems, recv_sems, local_sem, scatter)
    ids = [pl.program_id(a) for a in range(len(grid))]
    first = functools.reduce(jnp.logical_and, [i == 0 for i in ids])
    last = functools.reduce(jnp.logical_and, [i == n - 1 for i, n in zip(ids, grid)])

    def start():
        @pl.when(first)
        def _():
            for cp in copies():
                cp.start()

    def finish():
        @pl.when(last)
        def _():
            for cp in copies():
                cp.wait()

    return start, finish


_RIDE_SCRATCH = [pltpu.SemaphoreType.DMA((N_DEV - 1,)), pltpu.SemaphoreType.DMA((N_DEV - 1,)), pltpu.SemaphoreType.DMA]


def scan_forward(r, w, k, kn, b, v, ride, bl, t, t_real, d, name, pg, hch):
    npair, nst = d // LANES, t // hch
    grid = (bl, npair // pg, nst)

    def body(r_ref, w_ref, k_ref, kn_ref, b_ref, v_ref, ride_ref, y_ref, hist_ref, land_ref, s_ref, vb_ref, *sems):
        start, finish = _riding_exchange(ride_ref, land_ref, *sems, False, grid)
        start()
        _, first_head, eye2, ones_blk = _scan_consts()
        eye2f = eye2.astype(F32)
        diag = lambda tile: jnp.sum(tile * eye2f, axis=0, keepdims=True)

        @pl.when(pl.program_id(2) == 0)
        def _():
            s_ref[...] = jnp.zeros_like(s_ref)

        pair_cols = [slice(p * LANES, (p + 1) * LANES) for p in range(pg)]
        for p, tile in enumerate(_ones_dot([_spread(v_ref[0, :, cols], eye2) for cols in pair_cols], ones_blk)):
            vb_ref[p] = tile

        def step(ts, carry):
            prev, nxt = jnp.maximum(ts - 1, 0), jnp.minimum(ts + 1, hch - 1)
            states, tiles = [], []
            for p in range(pg):
                cols = slice(p * LANES, (p + 1) * LANES)
                s = s_ref[p]
                hist_ref[0, p, pl.ds(ts, 1)] = s[None]
                states.append(s)
                tiles.append(_round1(s * r_ref[prev, :, cols]))
                tiles.append(_spread(v_ref[nxt, :, cols], eye2))
            res = _ones_dot(tiles, ones_blk)
            for p in range(pg):
                cols = slice(p * LANES, (p + 1) * LANES)
                s = states[p]
                sa = _head_sums(s * kn_ref[ts, :, cols], first_head)
                s_ref[p] = s * w_ref[ts, :, cols] + sa * b_ref[ts, :, cols] + vb_ref[p] * k_ref[ts, :, cols]
            for p in range(pg):
                cols = slice(p * LANES, (p + 1) * LANES)
                y_ref[prev, :, cols] = diag(res[2 * p])
                vb_ref[p] = res[2 * p + 1]
            return carry

        real = pl.program_id(2) * hch < t_real
        lax.fori_loop(0, jnp.where(real, hch, 0), step, 0)
        last = _ones_dot([_round1(s_ref[p] * r_ref[hch - 1, :, cols]) for p, cols in enumerate(pair_cols)], ones_blk)
        for p, cols in enumerate(pair_cols):
            y_ref[hch - 1, :, cols] = diag(last[p])

        @pl.when(jnp.logical_not(real))
        def _():
            y_ref[...] = jnp.zeros_like(y_ref)
            hist_ref[...] = jnp.zeros_like(hist_ref)

        finish()

    row_spec = pl.BlockSpec((hch, 1, pg * LANES), lambda bb, g, c: (bb * nst + c, 0, g))
    hist_spec = pl.BlockSpec((1, pg, hch, RWKV_HEAD, LANES), lambda bb, g, c: (bb, g, c, 0, 0))
    hbm = pl.BlockSpec(memory_space=pl.ANY)
    rows3 = [a.reshape(bl * t, 1, d) for a in (r, w, k, kn, b, v)]
    y, hist, landed = pl.pallas_call(
        body, grid=grid, in_specs=[row_spec] * 6 + [hbm], out_specs=[row_spec, hist_spec, hbm],
        out_shape=[jax.ShapeDtypeStruct((bl * t, 1, d), F32), jax.ShapeDtypeStruct((bl, npair, t, RWKV_HEAD, LANES), F32),
                   jax.ShapeDtypeStruct((N_DEV,) + ride.shape, ride.dtype)],
        scratch_shapes=[pltpu.VMEM((pg, RWKV_HEAD, LANES), F32)] * 2 + _RIDE_SCRATCH,
        compiler_params=_params(("arbitrary", "arbitrary", "arbitrary")), name=name)(*rows3, ride)
    return y.reshape(bl * t, d), hist, landed


def scan_backward(r, w, k, kn, b, v, dy, hist, ride, bl, t, t_real, d, name, pg, hch):
    npair, nst = d // LANES, t // hch
    grid = (bl, npair // pg, nst)

    def body(r_ref, w_ref, k_ref, kn_ref, b_ref, v_ref, dy_ref, hist_ref, ride_ref,
             dr_ref, dw_ref, dk_ref, dkn_ref, db_ref, dv_ref, land_ref, ds_ref, cur_ref, *sems):
        start, finish = _riding_exchange(ride_ref, land_ref, *sems, True, grid)
        start()
        _, first_head, eye2, ones_blk = _scan_consts()
        eye2f = eye2.astype(F32)
        colsum = lambda x: jnp.sum(x, axis=0, keepdims=True)

        @pl.when(pl.program_id(2) == 0)
        def _():
            ds_ref[...] = jnp.zeros_like(ds_ref)

        tiles = []
        for p in range(pg):
            cols = slice(p * LANES, (p + 1) * LANES)
            tiles += [_spread(v_ref[hch - 1, :, cols], eye2), _spread(dy_ref[hch - 1, :, cols], eye2),
                      _split2(hist_ref[0, p, hch - 1] * kn_ref[hch - 1, :, cols])]
        first = _ones_dot(tiles, ones_blk)
        for p in range(pg):
            cols = slice(p * LANES, (p + 1) * LANES)
            row = lambda ref: ref[hch - 1, :, cols]
            s_prev = hist_ref[0, p, hch - 1]
            vb, dyb, sa = first[3 * p], first[3 * p + 1], first[3 * p + 2]
            cur_ref[0, p], cur_ref[1, p] = vb, sa
            dr_ref[hch - 1, :, cols] = colsum((s_prev * row(w_ref) + sa * row(b_ref) + vb * row(k_ref)) * dyb)
            ds_ref[p] += dyb * row(r_ref)

        def step(it, carry):
            ts = hch - 1 - it
            prev = jnp.maximum(ts - 1, 0)
            has_prev = ts > 0
            grads, tiles = [], []
            for p in range(pg):
                cols = slice(p * LANES, (p + 1) * LANES)
                ds = ds_ref[p]
                grads.append(ds)
                tiles.append(_spread(v_ref[prev, :, cols], eye2))
                tiles.append(_spread(dy_ref[prev, :, cols], eye2))
                tiles.append(_split2(hist_ref[0, p, pl.ds(prev, 1)][0] * kn_ref[prev, :, cols]))
                tiles.append(_round1(ds * k_ref[ts, :, cols]))
            res = _ones_dot(tiles, ones_blk)
            for p in range(pg):
                cols = slice(p * LANES, (p + 1) * LANES)
                row = lambda ref: ref[ts, :, cols]
                ds = grads[p]
                w_, kn_, b_ = row(w_ref), row(kn_ref), row(b_ref)
                dsa = _head_sums(ds * b_, first_head)
                s_prev = hist_ref[0, p, pl.ds(ts, 1)][0]
                vb, sa, dyb_prev = cur_ref[0, p], cur_ref[1, p], res[4 * p + 1]
                dk_ref[ts, :, cols] = colsum(ds * vb)
                db_ref[ts, :, cols] = colsum(ds * sa)
                dw_ref[ts, :, cols] = colsum(ds * s_prev)
                dkn_ref[ts, :, cols] = colsum(s_prev * dsa)
                dv_ref[ts, :, cols] = colsum(res[4 * p + 3] * eye2f)
                dr_ref[prev, :, cols] = jnp.where(has_prev, colsum(s_prev * dyb_prev), dr_ref[prev, :, cols])
                ds_ref[p] = ds * w_ + dsa * kn_ + jnp.where(has_prev, dyb_prev, 0.0) * r_ref[prev, :, cols]
            for p in range(pg):
                cur_ref[0, p] = res[4 * p]
                cur_ref[1, p] = res[4 * p + 2]
            return carry

        real = (nst - 1 - pl.program_id(2)) * hch < t_real
        lax.fori_loop(0, jnp.where(real, hch, 0), step, 0)

        @pl.when(jnp.logical_not(real))
        def _():
            for ref in (dr_ref, dw_ref, dk_ref, dkn_ref, db_ref, dv_ref):
                ref[...] = jnp.zeros_like(ref)

        finish()

    row_spec = pl.BlockSpec((hch, 1, pg * LANES), lambda bb, g, c: (bb * nst + nst - 1 - c, 0, g))
    hist_spec = pl.BlockSpec((1, pg, hch, RWKV_HEAD, LANES), lambda bb, g, c: (bb, g, nst - 1 - c, 0, 0))
    hbm = pl.BlockSpec(memory_space=pl.ANY)
    row_shape = jax.ShapeDtypeStruct((bl * t, 1, d), F32)
    rows3 = [a.reshape(bl * t, 1, d) for a in (r, w, k, kn, b, v, dy)]
    outs = pl.pallas_call(
        body, grid=grid, in_specs=[row_spec] * 7 + [hist_spec, hbm], out_specs=[row_spec] * 6 + [hbm],
        out_shape=[row_shape] * 6 + [jax.ShapeDtypeStruct(ride.shape, ride.dtype)],
        scratch_shapes=[pltpu.VMEM((pg, RWKV_HEAD, LANES), F32), pltpu.VMEM((2, pg, RWKV_HEAD, LANES), F32)] + _RIDE_SCRATCH,
        compiler_params=_params(("arbitrary", "arbitrary", "arbitrary")), name=name)(*rows3, hist, ride)
    return [o.reshape(bl * t, d) for o in outs[:6]] + [outs[6]]


def _mla_norms(pm, gq, gkv):
    ql = gq.shape[1]
    kvl = gkv.shape[1]
    return _rms(pm[:, :ql], gq), _rms(pm[:, ql:ql + kvl], gkv)


def mla_prep_fwd(pm, gq, gkv, name):
    n = pm.shape[0]
    return tilek(_mla_norms, [(pm, "r"), (gq, "f"), (gkv, "f")],
                 [("r", gq.shape[1], MMD), ("r", gkv.shape[1], MMD)], n_rows=n, tr=256, name=name)


def mla_prep_bwd(pm, gq, gkv, dcq, dckv, dkpe, name):
    n, wm = pm.shape
    ql, kvl = gq.shape[1], gkv.shape[1]

    def fn(pmv, gqv, gkvv, d1, d2, d3):
        _, vjp1 = jax.vjp(_rms, pmv[:, :ql], gqv)
        _, vjp2 = jax.vjp(_rms, pmv[:, ql:ql + kvl], gkvv)
        dcq_in, dgq = vjp1(d1)
        dckv_in, dgkv = vjp2(d2)
        return jnp.concatenate([dcq_in, dckv_in, d3], axis=1), dgq, dgkv

    return tilek(fn, [(pm, "r"), (gq, "f"), (gkv, "f"), (dcq, "r"), (dckv, "r"), (dkpe, "r")],
                 [("r", wm, F32), ("acc", gq.shape), ("acc", gkv.shape)], n_rows=n, tr=128, name=name)


def _rope(x, c, s, first):
    sw = jnp.where(first, pltpu.roll(x, LANES - ROPE_DIM // 2, 1), pltpu.roll(x, ROPE_DIM // 2, 1))
    return x * c + sw * s


def _unrope(d, c, s, first):
    z = d * s
    sw = jnp.where(first, pltpu.roll(z, LANES - ROPE_DIM // 2, 1), pltpu.roll(z, ROPE_DIM // 2, 1))
    return d * c + sw


def _causal_segments(n_tiles, parts=4):
    bounds = sorted({round(n_tiles * s / parts) for s in range(parts + 1)})
    return list(zip(bounds[:-1], bounds[1:]))


def attn_fwd(q, kv, pm, ct, st, bl, t, hm, name):
    n = q.shape[0]
    tq = LANES
    scale = QK_DIM ** -0.5
    kpe_blk = pm.shape[1] // LANES - 1

    def body(qn_ref, qpe_ref, kn_ref, v_ref, kpe_ref, ct_ref, st_ref, o_ref, lse_ref, kp_s, kn_s, v_s):
        h = pl.program_id(1)
        lane = lax.broadcasted_iota(jnp.int32, (1, LANES), 1)
        first = (lane & (ROPE_DIM - 1)) < ROPE_DIM // 2
        kp = _rope(kpe_ref[...], ct_ref[...], st_ref[...], first)
        kp_s[...] = jnp.where(h % 2 == 0, kp, pltpu.roll(kp, ROPE_DIM, 1)).astype(MMD)
        kn_s[...] = kn_ref[...].astype(MMD)
        v_s[...] = v_ref[...].astype(MMD)
        def segment(lo, hi):
            ext = hi * tq
            kpos = lax.broadcasted_iota(jnp.int32, (1, ext), 1)

            def qtile(i, carry):
                rows = pl.ds(pl.multiple_of(i * tq, tq), tq)
                q2 = _rope(qpe_ref[rows, :], ct_ref[rows, :], st_ref[rows, :], first)
                s = (_mm(qn_ref[rows, :], kn_s[:ext, :], ((1,), (1,))) + _mm(q2, kp_s[:ext, :], ((1,), (1,)))) * scale
                qpos = i * tq + lax.broadcasted_iota(jnp.int32, (tq, 1), 0)
                s = jnp.where(kpos <= qpos, s, -1e30)
                m = jnp.max(s, axis=1, keepdims=True)
                p = jnp.exp(s - m)
                l = jnp.sum(p, axis=1, keepdims=True)
                o_ref[rows, :] = _mm(p, v_s[:ext, :]) / l
                lse_ref[0, 0, rows, :] = m + jnp.log(l)
                return carry

            lax.fori_loop(lo, hi, qtile, 0)

        for lo, hi in _causal_segments(t // tq):
            segment(lo, hi)

    blk = lambda f: pl.BlockSpec((t, LANES), f)
    return pl.pallas_call(
        body, grid=(bl, hm),
        in_specs=[blk(lambda b, h: (b, h)), blk(lambda b, h: (b, hm + h // 2)), blk(lambda b, h: (b, h)),
                  blk(lambda b, h: (b, hm + h)), blk(lambda b, h: (b, kpe_blk)), blk(lambda b, h: (0, 0)), blk(lambda b, h: (0, 0))],
        out_specs=[blk(lambda b, h: (b, h)), pl.BlockSpec((1, 1, t, 1), lambda b, h: (b, h, 0, 0))],
        out_shape=[jax.ShapeDtypeStruct((n, hm * LANES), F32), jax.ShapeDtypeStruct((bl, hm, t, 1), F32)],
        scratch_shapes=[pltpu.VMEM((t, LANES), MMD)] * 3,
        compiler_params=_params(("parallel", "arbitrary")), name=name)(q, q, kv, kv, pm, ct, st)


def attn_bwd(q, kv, pm, o, do, lse, ct, st, bl, t, hm, name):
    n = q.shape[0]
    tq = LANES
    scale = QK_DIM ** -0.5
    kpe_blk = pm.shape[1] // LANES - 1

    def body(qn_ref, qpe_ref, kn_ref, v_ref, kpe_ref, o_ref, do_ref, lse_ref, ct_ref, st_ref,
             dqn_ref, dqpe_ref, dkn_ref, dv_ref, dkpe_ref, kp_s, kn_s, v_s, dkn_s, dkp_s, dv_s):
        h = pl.program_id(1)
        lane = lax.broadcasted_iota(jnp.int32, (1, LANES), 1)
        first = (lane & (ROPE_DIM - 1)) < ROPE_DIM // 2
        mine = (lane // ROPE_DIM) == (h % 2)
        kp = _rope(kpe_ref[...], ct_ref[...], st_ref[...], first)
        kp_s[...] = jnp.where(h % 2 == 0, kp, pltpu.roll(kp, ROPE_DIM, 1)).astype(MMD)
        kn_s[...] = kn_ref[...].astype(MMD)
        v_s[...] = v_ref[...].astype(MMD)
        dkn_s[...] = jnp.zeros_like(dkn_s)
        dkp_s[...] = jnp.zeros_like(dkp_s)
        dv_s[...] = jnp.zeros_like(dv_s)
        @pl.when(h % 2 == 0)
        def _():
            dqpe_ref[...] = jnp.zeros_like(dqpe_ref)

        @pl.when(h == 0)
        def _():
            dkpe_ref[...] = jnp.zeros_like(dkpe_ref)

        def segment(lo, hi):
            ext = hi * tq
            kpos = lax.broadcasted_iota(jnp.int32, (1, ext), 1)

            def qtile(i, carry):
                rows = pl.ds(pl.multiple_of(i * tq, tq), tq)
                c_i, s_i = ct_ref[rows, :], st_ref[rows, :]
                q1 = qn_ref[rows, :].astype(MMD)
                q2 = _rope(qpe_ref[rows, :], c_i, s_i, first).astype(MMD)
                s = (_mm(q1, kn_s[:ext, :], ((1,), (1,))) + _mm(q2, kp_s[:ext, :], ((1,), (1,)))) * scale
                qpos = i * tq + lax.broadcasted_iota(jnp.int32, (tq, 1), 0)
                p = jnp.where(kpos <= qpos, jnp.exp(s - lse_ref[0, 0, rows, :]), 0.0)
                do_i = do_ref[rows, :]
                delta = jnp.sum(do_i * o_ref[rows, :], axis=1, keepdims=True)
                dp = _mm(do_i, v_s[:ext, :], ((1,), (1,)))
                ds = (p * (dp - delta) * scale).astype(MMD)
                dqn_ref[rows, :] = _mm(ds, kn_s[:ext, :])
                dq2 = jnp.where(mine, _mm(ds, kp_s[:ext, :]), 0.0)
                dqpe_ref[rows, :] += _unrope(dq2, c_i, s_i, first)
                dkn_s[:ext, :] += _mm(ds, q1, ((0,), (0,)))
                dkp_s[:ext, :] += _mm(ds, q2, ((0,), (0,)))
                dv_s[:ext, :] += _mm(p, do_i, ((0,), (0,)))
                return carry

            lax.fori_loop(lo, hi, qtile, 0)

        for lo, hi in _causal_segments(t // tq):
            segment(lo, hi)
        dkn_ref[...] = dkn_s[...]
        dv_ref[...] = dv_s[...]
        dkp = jnp.where(mine, dkp_s[...], 0.0)
        dkp = jnp.where(h % 2 == 0, dkp, pltpu.roll(dkp, ROPE_DIM, 1))
        dkpe_ref[...] += _unrope(dkp, ct_ref[...], st_ref[...], first)

    blk = lambda f: pl.BlockSpec((t, LANES), f)
    hd = lambda b, h: (b, h)
    shp = lambda wd: jax.ShapeDtypeStruct((n, wd), F32)
    return pl.pallas_call(
        body, grid=(bl, hm),
        in_specs=[blk(hd), blk(lambda b, h: (b, hm + h // 2)), blk(hd), blk(lambda b, h: (b, hm + h)),
                  blk(lambda b, h: (b, kpe_blk)), blk(hd), blk(hd), pl.BlockSpec((1, 1, t, 1), lambda b, h: (b, h, 0, 0)),
                  blk(lambda b, h: (0, 0)), blk(lambda b, h: (0, 0))],
        out_specs=[blk(hd), blk(lambda b, h: (b, h // 2)), blk(hd), blk(hd), blk(lambda b, h: (b, 0))],
        out_shape=[shp(hm * LANES), shp(hm * ROPE_DIM), shp(hm * LANES), shp(hm * LANES), shp(LANES)],
        scratch_shapes=[pltpu.VMEM((t, LANES), MMD)] * 3 + [pltpu.VMEM((t, LANES), F32)] * 3,
        compiler_params=_params(("parallel", "arbitrary")), name=name)(q, q, kv, kv, pm, o, do, lse, ct, st)


def _peer(k):
    mx, my, mc = lax.axis_index("x"), lax.axis_index("y"), lax.axis_index("c")
    px = 1 - mx if k & 4 else mx
    py = 1 - my if k & 2 else my
    pc = 1 - mc if k & 1 else mc
    return (px, py, pc), 4 * px + 2 * py + pc


def _chips():
    mx, my, mc = lax.axis_index("x"), lax.axis_index("y"), lax.axis_index("c")
    return (mx, my, mc), (mx, my, 1 - mc), [(1 - mx, my), (mx, 1 - my), (1 - mx, 1 - my)]


def _riders(refs, gathers, scatters):
    n = gathers + scatters
    ins, outs, sems = refs[:n], refs[n:2 * n], refs[2 * n:]
    copies = []
    for i in range(n):
        copies += _direct_copies(ins[i], outs[i], *sems[3 * i:3 * i + 3], i >= gathers)
    return copies


def _rider_shapes(gathers, scatters):
    shapes = [jax.ShapeDtypeStruct((N_DEV,) + a.shape, a.dtype) for a in gathers]
    return shapes + [jax.ShapeDtypeStruct(a.shape, a.dtype) for a in scatters]


def all_gather_two_level(x, name, also=()):
    na = len(also)

    def body(*refs):
        x_ref, o_ref = refs[0], refs[1 + na]
        send_sems, recv_sems, local_sem = refs[2 + 2 * na:5 + 2 * na]
        riders = _riders(refs[1:1 + na] + refs[2 + na:2 + 2 * na] + refs[5 + 2 * na:], na, 0)
        for cp in riders:
            cp.start()
        me, sibling, chips = _chips()
        blk = lambda px, py, pc: o_ref.at[4 * px + 2 * py + pc]

        def copy(k, block, to, src=None):
            return pltpu.make_async_remote_copy(src_ref=blk(*block) if src is None else src, dst_ref=blk(*block),
                                                send_sem=send_sems.at[k], recv_sem=recv_sems.at[k], device_id=to,
                                                device_id_type=MESH)

        mine = pltpu.make_async_copy(x_ref, blk(*me), local_sem)
        mine.start()
        first = [copy(0, me, sibling, src=x_ref)] + [copy(1 + j, me, (*chip, me[2]), src=x_ref) for j, chip in enumerate(chips)]
        for cp in first:
            cp.start()
        passed = [copy(4 + j, (*chip, me[2]), sibling) for j, chip in enumerate(chips)]
        for j, chip in enumerate(chips):
            copy(1 + j, (*chip, me[2]), me).wait_recv()
            passed[j].start()
        copy(0, sibling, me).wait_recv()
        for j, chip in enumerate(chips):
            copy(4 + j, (*chip, 1 - me[2]), me).wait_recv()
        for cp in first + passed:
            cp.wait_send()
        mine.wait()
        for cp in riders:
            cp.wait()

    hbm = pl.BlockSpec(memory_space=pl.ANY)
    return pl.pallas_call(
        body, in_specs=[hbm] * (1 + na), out_specs=[hbm] * (1 + na),
        out_shape=[jax.ShapeDtypeStruct((N_DEV,) + x.shape, x.dtype)] + _rider_shapes(also, ()),
        scratch_shapes=_RIDE_SCRATCH * (1 + na), name=name)(x, *also)


def exchange_sibling(x, name):
    def body(x_ref, o_ref, send_sems, recv_sems):
        me, sibling, _ = _chips()
        copies = []
        for q in range(N_DEV // 2):
            cp = pltpu.make_async_remote_copy(src_ref=x_ref.at[2 * q + 1 - me[2]], dst_ref=o_ref.at[q], send_sem=send_sems.at[q],
                                              recv_sem=recv_sems.at[q], device_id=sibling, device_id_type=MESH)
            cp.start()
            copies.append(cp)
        for cp in copies:
            cp.wait()

    return pl.pallas_call(
        body, in_specs=[pl.BlockSpec(memory_space=pl.ANY)], out_specs=pl.BlockSpec(memory_space=pl.ANY),
        out_shape=jax.ShapeDtypeStruct((N_DEV // 2,) + x.shape[1:], x.dtype),
        scratch_shapes=[pltpu.SemaphoreType.DMA((N_DEV // 2,)), pltpu.SemaphoreType.DMA((N_DEV // 2,))], name=name)(x)


def exchange_chips(x, name, gathers=(), scatters=()):
    na = len(gathers) + len(scatters)

    def body(*refs):
        x_ref, o_ref = refs[0], refs[1 + na]
        send_sems, recv_sems, local_sem = refs[2 + 2 * na:5 + 2 * na]
        riders = _riders(refs[1:1 + na] + refs[2 + na:2 + 2 * na] + refs[5 + 2 * na:], len(gathers), len(scatters))
        for cp in riders:
            cp.start()
        me, _, chips = _chips()
        here = 2 * me[0] + me[1]
        local = pltpu.make_async_copy(x_ref.at[here], o_ref.at[here], local_sem)
        local.start()
        copies = []
        for j, (px, py) in enumerate(chips):
            cp = pltpu.make_async_remote_copy(src_ref=x_ref.at[2 * px + py], dst_ref=o_ref.at[here], send_sem=send_sems.at[j],
                                              recv_sem=recv_sems.at[j], device_id=(px, py, me[2]), device_id_type=MESH)
            cp.start()
            copies.append(cp)
        for cp in copies:
            cp.wait()
        local.wait()
        for cp in riders:
            cp.wait()

    hbm = pl.BlockSpec(memory_space=pl.ANY)
    return pl.pallas_call(
        body, in_specs=[hbm] * (1 + na), out_specs=[hbm] * (1 + na),
        out_shape=[jax.ShapeDtypeStruct(x.shape, x.dtype)] + _rider_shapes(gathers, scatters),
        scratch_shapes=[pltpu.SemaphoreType.DMA((3,)), pltpu.SemaphoreType.DMA((3,)), pltpu.SemaphoreType.DMA] + _RIDE_SCRATCH * na,
        name=name)(x, *gathers, *scatters)


def add_blocks(a, b, name):
    q, r, c = a.shape
    tr = _tile(r, max(16, (2 << 20) // (c * a.dtype.itemsize)), 16)
    spec = pl.BlockSpec((1, tr, c), lambda i, j: (i, j, 0))

    def body(a_ref, b_ref, o_ref):
        o_ref[...] = (a_ref[...].astype(F32) + b_ref[...].astype(F32)).astype(o_ref.dtype)

    return pl.pallas_call(
        body, grid=(q, r // tr), in_specs=[spec, spec], out_specs=spec, out_shape=jax.ShapeDtypeStruct(a.shape, a.dtype),
        compiler_params=_params(("parallel", "parallel")), name=name)(a, b)


def reduce_scatter_two_level(x, tag, gathers=(), scatters=()):
    q = N_DEV // 2
    from_sibling = exchange_sibling(x, f"{tag}_sibling")
    mine = lax.dynamic_index_in_dim(x.reshape((q, 2) + x.shape[1:]), lax.axis_index("c"), axis=1, keepdims=False)
    chip_sums = add_blocks(mine, from_sibling, f"{tag}_pair_sum")
    from_chips, *small = exchange_chips(chip_sums, f"{tag}_chips", gathers, scatters)
    return (sum_blocks(from_chips, f"{tag}_sum"), *small)


def sum_blocks(x, name):
    nb, r, c = x.shape
    tr = _tile(r, max(16, (4 << 20) // (nb * c * x.dtype.itemsize)), 16)

    def body(x_ref, o_ref):
        acc = x_ref[0].astype(F32)
        for i in range(1, nb):
            acc = acc + x_ref[i].astype(F32)
        o_ref[...] = acc

    return pl.pallas_call(
        body, grid=(r // tr,), in_specs=[pl.BlockSpec((nb, tr, c), lambda i: (0, i, 0))],
        out_specs=pl.BlockSpec((tr, c), lambda i: (i, 0)), out_shape=jax.ShapeDtypeStruct((r, c), F32),
        compiler_params=_params(("parallel",)), name=name)(x)


def _adamw(w, g, m, v):
    m = ADAM_B1 * m + (1.0 - ADAM_B1) * g
    v = ADAM_B2 * v + (1.0 - ADAM_B2) * jnp.square(g)
    m_hat = m / (1.0 - ADAM_B1 ** ADAM_STEP)
    v_hat = v / (1.0 - ADAM_B2 ** ADAM_STEP)
    delta = -ADAM_LR * (m_hat / (jnp.sqrt(v_hat) + ADAM_EPS) + ADAM_WD * w)
    return delta, m, v


def adamw(w, g, m, v, name):
    r, c = w.shape
    tr = _tile(r, 256, 8)
    spec = pl.BlockSpec((tr, c), lambda i: (i, 0))

    def body(w_ref, g_ref, m_ref, v_ref, d_ref, nm_ref, nv_ref):
        d_ref[...], nm_ref[...], nv_ref[...] = _adamw(w_ref[...], g_ref[...], m_ref[...], v_ref[...])

    return pl.pallas_call(
        body, grid=(r // tr,), in_specs=[spec] * 4, out_specs=[spec] * 3,
        out_shape=[jax.ShapeDtypeStruct((r, c), F32)] * 3, compiler_params=_params(("parallel",)), name=name)(w, g, m, v)


def batch_sum_rows(dh, bl, t, rows, name):
    d = dh.shape[1]

    def body(x_ref, o_ref):
        @pl.when(pl.program_id(0) == 0)
        def _():
            o_ref[...] = jnp.zeros_like(o_ref)

        o_ref[...] += x_ref[...]

    return pl.pallas_call(
        body, grid=(bl,), in_specs=[pl.BlockSpec((rows, d), lambda b: (b * (t // rows), 0))],
        out_specs=pl.BlockSpec((rows, d), lambda b: (0, 0)), out_shape=jax.ShapeDtypeStruct((rows, d), F32),
        compiler_params=_params(("arbitrary",)), name=name)(dh)


class Dims:
    def __init__(self, x, w_up, g_up, q_norm, kv_norm, d_ff):
        self.bl, self.seq, self.d = x.shape
        self.n_meta = 16
        self.t_real = self.n_meta + self.seq
        self.t = -(-self.t_real // LANES) * LANES
        self.n = self.bl * self.t
        self.f = d_ff
        self.wl, self.gl = w_up.shape[-2], g_up.shape[-2]
        self.ql, self.kvl = q_norm.shape[-1], kv_norm.shape[-1]
        self.hm = self.d // V_DIM
        self.in_cols = 5 * self.d + 2 * self.wl + self.gl + self.ql + self.kvl + ROPE_DIM


def _pad_cols(a, width):
    return jnp.pad(a, ((0, 0), (0, width - a.shape[1])))


def _pad_rows(a, rows):
    return jnp.pad(a, ((0, rows - a.shape[0]), (0, 0)))


def split_in(a, dm, axis=1):
    d, wl, gl, ql, kvl = dm.d, dm.wl, dm.gl, dm.ql, dm.kvl
    size = a.shape[axis]
    cut = lambda lo, hi: lax.slice_in_dim(a, min(lo, size), min(hi, size), axis=axis)

    def pad(p, width):
        cfg = [(0, 0)] * a.ndim
        cfg[axis] = (0, width - p.shape[axis])
        return jnp.pad(p, cfg)

    o = 3 * d
    lora = jnp.concatenate([pad(cut(o, o + wl), LANES), pad(cut(o + wl, o + 2 * wl), LANES),
                            cut(o + 2 * wl, o + 2 * wl + gl)], axis=axis)
    o += 2 * wl + gl
    mla = pad(cut(o, o + ql + kvl + ROPE_DIM), ql + kvl + LANES)
    o += ql + kvl + ROPE_DIM
    return dict(r=cut(0, d), k=cut(d, 2 * d), v=cut(2 * d, 3 * d), l=lora, m=mla, ga=cut(o, o + d), gb=cut(o + d, o + 2 * d))


def merge_in(g, dm, axis=1):
    wl, gl, ql, kvl = dm.wl, dm.gl, dm.ql, dm.kvl
    cut = lambda p, lo, hi: lax.slice_in_dim(p, lo, hi, axis=axis)
    l, m = g["l"], g["m"]
    return jnp.concatenate([g["r"], g["k"], g["v"], cut(l, 0, wl), cut(l, LANES, LANES + wl), cut(l, 2 * LANES, 2 * LANES + gl),
                            cut(m, 0, ql + kvl + ROPE_DIM), g["ga"], g["gb"]], axis=axis)


def split_uq(w, dm):
    w3 = w.reshape(w.shape[0], dm.hm, QK_DIM)
    return jnp.concatenate([w3[:, :, :NOPE_DIM].reshape(w.shape[0], -1), w3[:, :, NOPE_DIM:].reshape(w.shape[0], -1)], axis=1)


def merge_uq(gn, gp, dm):
    r = gn.shape[0]
    return jnp.concatenate([gn.reshape(r, dm.hm, NOPE_DIM), gp.reshape(r, dm.hm, ROPE_DIM)], axis=2).reshape(r, -1)


def split_ukv(w, dm):
    w3 = w.reshape(w.shape[0], dm.hm, NOPE_DIM + V_DIM)
    return jnp.concatenate([w3[:, :, :NOPE_DIM].reshape(w.shape[0], -1), w3[:, :, NOPE_DIM:].reshape(w.shape[0], -1)], axis=1)


def merge_ukv(gk, gv, dm):
    r = gk.shape[0]
    return jnp.concatenate([gk.reshape(r, dm.hm, NOPE_DIM), gv.reshape(r, dm.hm, V_DIM)], axis=2).reshape(r, -1)


def head_matrices(d):
    heads = d // RWKV_HEAD
    e = (np.arange(d)[:, None] // RWKV_HEAD == np.arange(LANES)[None, :]) & (np.arange(LANES)[None, :] < heads)
    return jnp.asarray(e, BF16), jnp.asarray(e.T, BF16)


def rope_tables(t):
    pos = jnp.arange(t, dtype=F32)
    inv_freq = 1.0 / (ROPE_THETA ** (jnp.arange(0, ROPE_DIM, 2, dtype=F32) / ROPE_DIM))
    ang = pos[:, None] * inv_freq[None, :]
    cos, sin = jnp.cos(ang), jnp.sin(ang)
    return jnp.tile(jnp.concatenate([cos, cos], axis=1), (1, 2)), jnp.tile(jnp.concatenate([-sin, sin], axis=1), (1, 2))


def local_step(dm, x, loss_target, meta, wt, late_shards, late_rows, sp):
    bl, t, n, d, hm = dm.bl, dm.t, dm.n, dm.d, dm.hm
    e, et = head_matrices(d)
    ct, st = rope_tables(t)
    padz = jnp.zeros((bl, t - dm.t_real, d), F32)
    h0 = jnp.concatenate([jnp.broadcast_to(meta[None], (bl, dm.n_meta, d)), x, padz], axis=1).reshape(n, d)
    tgt = jnp.concatenate([jnp.zeros((bl, dm.n_meta, d), F32), loss_target, padz], axis=1).reshape(n, d)
    tpos = jnp.arange(t)
    mask = jnp.tile(((tpos >= dm.n_meta) & (tpos < dm.t_real)).astype(F32), bl).reshape(n, 1)

    win = split_in(wt["w_in"], dm, axis=0)
    mu = split_in(sp["tm_mu"], dm)
    wq, wkv = split_uq(wt["w_uq"], dm), split_ukv(wt["w_ukv"], dm)
    prm = dict(w0=sp["w0"], a0=sp["a0"], k_k=sp["k_k"], k_a=sp["k_a"], gn_w=sp["gn_w"], gn_b=sp["gn_b"], r_k=sp["r_k"],
               w_up=_pad_rows(wt["w_up"], LANES).astype(F32), a_up=_pad_rows(wt["a_up"], LANES).astype(F32),
               g_up=wt["g_up"].astype(F32))

    h1, ffn1 = ffn_forward(h0, sp["ffn1_norm"], wt["ffn1_w_gate"], wt["ffn1_w_up"], wt["ffn1_w_down"], "ffn1")
    u = rms_fwd(h1, sp["mix_norm"], "mix_rms")
    proj = {key: matmul([(u, win[key])], "nt", name=f"proj_{key}") for key in win}
    sh = {key: lerp_fwd(proj[key], mu[key], bl, t, f"shift_{key}") for key in ("r", "k", "v", "l")}
    decay, kmod, kneg, bvec, gate = rwkv_prep_fwd(sh["k"], sh["l"], prm, e, et, "rwkv_prep")
    pairs = min(SCAN_PAIRS, d // LANES)
    y, hist, late_all = scan_forward(sh["r"], decay, kmod, kneg, bvec, sh["v"], late_shards, bl, t, dm.t_real, d, "wkv_scan",
                                     pairs, SCAN_FWD_STEPS)
    wt = dict(wt, **{key: late_all[:, lo:hi].reshape(-1, d) for key, lo, hi in zip(LATE, late_rows[:-1], late_rows[1:])})
    cqn, ckvn = mla_prep_fwd(proj["m"], sp["q_norm"], sp["kv_norm"], "mla_norms")
    q = matmul([(cqn, wq)], "nn", name="mla_q")
    kv = matmul([(ckvn, wkv)], "nn", name="mla_kv")
    o, lse = attn_fwd(q, kv, proj["m"], ct, st, bl, t, hm, "mla_attn")
    post_in = [y, sh["r"], kmod, sh["v"], gate, proj["ga"], proj["gb"], o]
    mix = rwkv_post_fwd(post_in, prm, e, et, "mix_gate")
    h2 = matmul([(mix, wt["w_out"])], "nn", res=h1, name="out_proj")
    h3, ffn2 = ffn_forward(h2, sp["ffn2_norm"], wt["ffn2_w_gate"], wt["ffn2_w_up"], wt["ffn2_w_down"], "ffn2")
    dh3, d_final, loss = loss_head(h3, tgt, mask, sp["final_norm"], "loss_head")

    gw, gs = {}, {"final_norm": d_final}
    dh2, gs["ffn2_norm"], gw["ffn2_w_gate"], gw["ffn2_w_up"], gw["ffn2_w_down"] = ffn_backward(
        dh3, h2, sp["ffn2_norm"], wt["ffn2_w_gate"], wt["ffn2_w_up"], wt["ffn2_w_down"], ffn2, "ffn2")
    dmix = matmul([(dh2, wt["w_out"])], "nt", name="out_proj_dx")
    gw["w_out"] = matmul([(mix, dh2)], "tn", out_dtype=MMD, name="out_proj_dw")
    late_grads = jnp.concatenate([gw.pop(key).reshape(N_DEV, hi - lo, d) for key, lo, hi in
                                  zip(LATE, late_rows[:-1], late_rows[1:])], axis=1).astype(MMD)
    (dy, dr_p, dkm_p, dv_p, dgate, dpga, dpgb, do, gs["gn_w"], gs["gn_b"], gs["r_k"]) = rwkv_post_bwd(
        post_in, prm, e, et, dmix, "mix_gate_bwd")
    dqn, dqpe, dkn, dv_att, dkpe = attn_bwd(q, kv, proj["m"], o, do, lse, ct, st, bl, t, hm, "mla_attn_bwd")
    nq = hm * NOPE_DIM
    dcqn = matmul([(dqn, wq[:, :nq])], "nt", name="mla_q_dx1")
    dcqn = matmul([(dqpe, wq[:, nq:])], "nt", res=dcqn, name="mla_q_dx2")
    gw["w_uq"] = merge_uq(matmul([(cqn, dqn)], "tn", name="mla_q_dw1"), matmul([(cqn, dqpe)], "tn", name="mla_q_dw2"), dm)
    dckvn = matmul([(dkn, wkv[:, :nq]), (dv_att, wkv[:, nq:])], "nt", name="mla_kv_dx", tk=1024)
    gw["w_ukv"] = merge_ukv(matmul([(ckvn, dkn)], "tn", name="mla_kv_dw1"), matmul([(ckvn, dv_att)], "tn", name="mla_kv_dw2"), dm)
    dproj = {"ga": dpga, "gb": dpgb}
    dproj["m"], gs["q_norm"], gs["kv_norm"] = mla_prep_bwd(proj["m"], sp["q_norm"], sp["kv_norm"], dcqn, dckvn, dkpe, "mla_norms_bwd")
    dr_s, ddecay, dk_s, dkneg, dbvec, dv_s, late_recv = scan_backward(
        sh["r"], decay, kmod, kneg, bvec, sh["v"], dy, hist, late_grads, bl, t, dm.t_real, d, "wkv_scan_bwd", pairs,
        SCAN_BWD_STEPS)
    late_sum = sum_blocks(late_recv, "sum_late")
    (dsh_k, dsh_l, gs["w0"], gs["a0"], gs["k_k"], gs["k_a"], g_wup, g_aup, gw["g_up"]) = rwkv_prep_bwd(
        sh["k"], sh["l"], prm, e, et, [ddecay, dk_s, dkm_p, dkneg, dbvec, dgate], "rwkv_prep_bwd")
    gw["w_up"], gw["a_up"] = g_wup[:dm.wl], g_aup[:dm.wl]
    dmu = {}
    for key, cts in (("r", [dr_s, dr_p]), ("k", [dsh_k]), ("v", [dv_s, dv_p]), ("l", [dsh_l])):
        dproj[key], dmu[key] = lerp_bwd(proj[key], mu[key], cts, bl, t, f"shift_{key}_bwd")
    zero_m = jnp.zeros((1, proj["m"].shape[1]), F32)
    gs["tm_mu"] = merge_in(dict(dmu, m=zero_m, ga=zero_m[:, :0], gb=zero_m[:, :0]), dm)[:, :3 * d + 2 * dm.wl + dm.gl]
    wide = ("r", "k", "v", "ga", "gb")
    du = matmul([(dproj[key], win[key]) for key in wide], "nn", name="proj_dx", tn=512, tk=512)
    du = matmul([(dproj["l"], win["l"])], "nn", res=du, name="proj_dx_l")
    du = matmul([(dproj["m"], win["m"])], "nn", res=du, name="proj_dx_m")
    gw["w_in"] = merge_in({key: matmul([(dproj[key], u)], "tn", out_dtype=MMD, name=f"proj_dw_{key}") for key in win},
                          dm, axis=0)
    dh1, gs["mix_norm"] = rms_bwd(h1, sp["mix_norm"], du, dh2, "mix_rms_bwd")
    dh0, gs["ffn1_norm"], gw["ffn1_w_gate"], gw["ffn1_w_up"], gw["ffn1_w_down"] = ffn_backward(
        dh1, h0, sp["ffn1_norm"], wt["ffn1_w_gate"], wt["ffn1_w_up"], wt["ffn1_w_down"], ffn1, "ffn1")
    grad_x = dh0.reshape(bl, t, d)[:, dm.n_meta:dm.t_real]
    dmeta = batch_sum_rows(dh0, bl, t, dm.n_meta, "meta_grad")
    return loss, grad_x, dmeta, gw, late_sum, gs


TRANSPOSED = ("ffn1_w_gate", "ffn1_w_up", "w_in", "ffn2_w_gate", "ffn2_w_up")
EARLY = ("ffn1_w_gate", "ffn1_w_up", "ffn1_w_down", "w_in")
LATE = ("w_out", "ffn2_w_gate", "ffn2_w_up", "ffn2_w_down")
NARROW = ("w_up", "a_up", "g_up", "w_uq", "w_ukv")
MATRICES = ("ffn1_w_gate", "ffn1_w_up", "ffn1_w_down", "w_in", "w_up", "a_up", "g_up", "w_uq", "w_ukv", "w_out",
            "ffn2_w_gate", "ffn2_w_up", "ffn2_w_down")
SMALL = ("ffn1_norm", "mix_norm", "tm_mu", "w0", "a0", "k_k", "k_a", "r_k", "gn_w", "gn_b", "q_norm", "kv_norm",
         "ffn2_norm", "final_norm")
WEIGHTS = ("meta_tokens", "ffn1_norm", "ffn1_w_gate", "ffn1_w_up", "ffn1_w_down", "mix_norm", "w_in", "tm_mu", "w0", "w_up",
           "a0", "a_up", "g_up", "k_k", "k_a", "r_k", "gn_w", "gn_b", "q_norm", "w_uq", "kv_norm", "w_ukv", "w_out",
           "ffn2_norm", "ffn2_w_gate", "ffn2_w_up", "ffn2_w_down", "final_norm")
PACK_COLS = 1024
PACK_ALIGN = 16 * PACK_COLS


def _pack(parts):
    offs, o = [], 0
    for p in parts:
        offs.append(o)
        o += p.shape[1]
    total = -(-o // PACK_ALIGN) * PACK_ALIGN
    flat = jnp.concatenate(list(parts) + [jnp.zeros((parts[0].shape[0], total - o), parts[0].dtype)], axis=1)
    return flat.reshape(parts[0].shape[0], total // PACK_COLS, PACK_COLS), offs


def kernel(x, meta_tokens, ffn1_norm, ffn1_w_gate, ffn1_w_up, ffn1_w_down, mix_norm, w_in, tm_mu, w0, w_up, a0, a_up, g_up, k_k, k_a, r_k, gn_w, gn_b, q_norm, w_uq, kv_norm, w_ukv, w_out, ffn2_norm, ffn2_w_gate, ffn2_w_up, ffn2_w_down, final_norm, loss_target, m_meta_tokens, m_ffn1_norm, m_ffn1_w_gate, m_ffn1_w_up, m_ffn1_w_down, m_mix_norm, m_w_in, m_tm_mu, m_w0, m_w_up, m_a0, m_a_up, m_g_up, m_k_k, m_k_a, m_r_k, m_gn_w, m_gn_b, m_q_norm, m_w_uq, m_kv_norm, m_w_ukv, m_w_out, m_ffn2_norm, m_ffn2_w_gate, m_ffn2_w_up, m_ffn2_w_down, m_final_norm, v_meta_tokens, v_ffn1_norm, v_ffn1_w_gate, v_ffn1_w_up, v_ffn1_w_down, v_mix_norm, v_w_in, v_tm_mu, v_w0, v_w_up, v_a0, v_a_up, v_g_up, v_k_k, v_k_a, v_r_k, v_gn_w, v_gn_b, v_q_norm, v_w_uq, v_kv_norm, v_w_ukv, v_w_out, v_ffn2_norm, v_ffn2_w_gate, v_ffn2_w_up, v_ffn2_w_down, v_final_norm):
    args = dict(locals())
    wts = {k: args[k] for k in WEIGHTS}
    ms = {k: args["m_" + k] for k in WEIGHTS}
    vs = {k: args["v_" + k] for k in WEIGHTS}
    dm = Dims(x, w_up, g_up, q_norm, kv_norm, ffn1_w_down.shape[1] * N_DEV)

    shard2d = {k: wts[k].reshape(wts[k].shape[-2], wts[k].shape[-1]) for k in MATRICES}
    sent = {k: shard2d[k].astype(MMD).T if k in TRANSPOSED else shard2d[k] for k in MATRICES}
    early_rows = np.cumsum([0] + [sent[k].shape[0] for k in EARLY])
    late_rows = np.cumsum([0] + [sent[k].shape[0] for k in LATE])
    send, offs = _pack([sent[k].astype(MMD).reshape(1, -1) for k in NARROW])
    got_early, got, got_meta = all_gather_two_level(jnp.concatenate([sent[k].astype(MMD) for k in EARLY], axis=0),
                                                    "gather_early", also=(send[0], meta_tokens))
    full = {k: got_early[:, lo:hi].reshape(-1, dm.d) for k, lo, hi in zip(EARLY, early_rows[:-1], early_rows[1:])}
    late_shards = jnp.concatenate([sent[k].astype(MMD) for k in LATE], axis=0)
    got = got.reshape(N_DEV, -1)
    for k, o in zip(NARROW, offs):
        r, c = sent[k].shape
        full[k] = got[:, o:o + r * c].reshape(N_DEV, r, c).transpose(1, 0, 2).reshape(r, N_DEV * c)
    mr, mc = meta_tokens.shape
    meta = got_meta.transpose(1, 0, 2).reshape(mr, N_DEV * mc)
    small = {k: wts[k].reshape(1, -1) for k in SMALL}

    loss, grad_x, dmeta, gw, gsum_late, gs = local_step(dm, x, loss_target, meta, full, late_shards, late_rows, small)

    def blocks(k, g):
        r, c = sent[k].shape
        return g.reshape(r, N_DEV, c).transpose(1, 0, 2).reshape(N_DEV, r * c)

    gsend, goffs = _pack([blocks(k, gw[k]).astype(MMD) for k in NARROW]
                         + [dmeta.reshape(mr, N_DEV, mc).transpose(1, 0, 2).reshape(N_DEV, mr * mc).astype(MMD)])
    ssend, soffs = _pack([gs[k].reshape(1, -1) for k in SMALL] + [loss])
    gearly = jnp.concatenate([gw[k].reshape(N_DEV, sent[k].shape[0], dm.d) for k in EARLY], axis=1).astype(MMD)
    gsum_early, small_parts, narrow_parts = reduce_scatter_two_level(gearly, "scatter_early", gathers=(ssend[0],),
                                                                     scatters=(gsend,))
    grads = {}
    for names, rows, gsum_rows in ((EARLY, early_rows, gsum_early), (LATE, late_rows, gsum_late)):
        for k, lo, hi in zip(names, rows[:-1], rows[1:]):
            grads[k] = gsum_rows[lo:hi].T if k in TRANSPOSED else gsum_rows[lo:hi]
    gsum = sum_blocks(narrow_parts, "sum_narrow").reshape(-1)
    for k, o in zip(NARROW, goffs):
        r, c = sent[k].shape
        grads[k] = gsum[o:o + r * c].reshape(r, c)
    grads["meta_tokens"] = gsum[goffs[-1]:goffs[-1] + mr * mc].reshape(mr, mc)
    ssum = sum_blocks(small_parts, "sum_small").reshape(-1)
    for k, o in zip(SMALL, soffs):
        grads[k] = ssum[o:o + small[k].shape[1]]
    loss_total = ssum[soffs[-1]]

    delta, new_m, new_v = {}, {}, {}
    for k in MATRICES + ("meta_tokens",):
        shp = wts[k].shape
        to2d = lambda a: a.reshape(shp[-2], shp[-1])
        dlt, nm, nv = adamw(to2d(wts[k]), grads[k], to2d(ms[k]), to2d(vs[k]), f"adamw_{k}")
        delta[k], new_m[k], new_v[k] = dlt.reshape(shp), nm.reshape(shp), nv.reshape(shp)
        grads[k] = grads[k].reshape(shp)
    pw, _ = _pack([wts[k].reshape(1, -1) for k in SMALL])
    pm_, _ = _pack([ms[k].reshape(1, -1) for k in SMALL])
    pv, _ = _pack([vs[k].reshape(1, -1) for k in SMALL])
    pg, poffs = _pack([grads[k].reshape(1, -1) for k in SMALL])
    dlt, nm, nv = adamw(pw[0], pg[0], pm_[0], pv[0], "adamw_small")
    for k, o in zip(SMALL, poffs):
        shp, sz = wts[k].shape, small[k].shape[1]
        cut = lambda a: a.reshape(-1)[o:o + sz].reshape(shp)
        delta[k], new_m[k], new_v[k] = cut(dlt), cut(nm), cut(nv)
        grads[k] = grads[k].reshape(shp)

    return (loss_total, grad_x, *[grads[k] for k in WEIGHTS], *[delta[k] for k in WEIGHTS],
            *[new_m[k] for k in WEIGHTS], *[new_v[k] for k in WEIGHTS])
```

```python
import functools

import numpy as np
import jax
import jax.numpy as jnp
from jax import lax
from jax.experimental import pallas as pl
from jax.experimental.pallas import tpu as pltpu

F32 = jnp.float32
BF16 = jnp.bfloat16
MMD = BF16

NORM_EPS = 1e-6
RWKV_HEAD = 64
GN_EPS = RWKV_HEAD * 1e-5
NOPE_DIM = 128
ROPE_DIM = 64
V_DIM = 128
QK_DIM = NOPE_DIM + ROPE_DIM
ROPE_THETA = 10000.0
ADAM_LR = 0.001
ADAM_B1 = 0.9
ADAM_B2 = 0.999
ADAM_EPS = 1e-08
ADAM_WD = 0.01
ADAM_STEP = 10

LANES = 128
SCAN_PAIRS = 8
SCAN_FWD_STEPS = 32
SCAN_BWD_STEPS = 16
N_DEV = 8
VMEM_LIMIT = 56 * 1024 * 1024
MESH = pl.DeviceIdType.MESH


def _tile(n, target, align):
    best = None
    for d in range(align, min(n, target) + 1, align):
        if n % d == 0:
            best = d
    return best if best is not None else n


def _params(sem=None):
    return pltpu.CompilerParams(dimension_semantics=sem, vmem_limit_bytes=VMEM_LIMIT)


def _mm(a, b, dims=((1,), (0,))):
    return lax.dot_general(a.astype(MMD), b.astype(MMD), (dims, ((), ())), preferred_element_type=F32)


@jax.custom_vjp
def mmdot(a, b):
    return _mm(a, b)


def _mmdot_fwd(a, b):
    return _mm(a, b), (a, b)


def _mmdot_bwd(res, g):
    a, b = res
    return _mm(g, b, ((1,), (1,))).astype(a.dtype), _mm(a, g, ((0,), (0,))).astype(b.dtype)


mmdot.defvjp(_mmdot_fwd, _mmdot_bwd)


def _dot2(x, m):
    hi = x.astype(BF16)
    lo = (x - hi.astype(F32)).astype(BF16)
    return (lax.dot_general(hi, m, (((1,), (0,)), ((), ())), preferred_element_type=F32)
            + lax.dot_general(lo, m, (((1,), (0,)), ((), ())), preferred_element_type=F32))


@jax.custom_vjp
def segsum(x, e, et):
    return _dot2(_dot2(x, e), et)


def _segsum_fwd(x, e, et):
    return segsum(x, e, et), (e, et)


def _segsum_bwd(res, g):
    e, et = res
    return segsum(g, e, et), jnp.zeros_like(e), jnp.zeros_like(et)


segsum.defvjp(_segsum_fwd, _segsum_bwd)


def _sigmoid(x):
    return 1.0 / (1.0 + jnp.exp(-x))


def _softplus(x):
    return jnp.maximum(x, 0.0) + jnp.log(1.0 + jnp.exp(-jnp.abs(x)))


def _rms(x, g):
    return x * lax.rsqrt(jnp.mean(x * x, axis=-1, keepdims=True) + NORM_EPS) * g


_DIMS = {"nn": ((1,), (0,)), "nt": ((1,), (1,)), "tn": ((0,), (0,))}


def matmul(pairs, mode, *, name, out_dtype=F32, res=None, alpha=1.0, tm=1088, tn=1024, tk=2048):
    a0, b0 = pairs[0]
    if mode == "nn":
        (m, k), n = a0.shape, b0.shape[1]
    elif mode == "nt":
        (m, k), n = a0.shape, b0.shape[0]
    else:
        (k, m), n = a0.shape, b0.shape[1]
    tm = _tile(m, 1408, 128) if mode == "tn" else _tile(m, tm, 16)
    tn = _tile(n, 2048 if mode == "tn" else tn, 128)
    tk = _tile(k, min(tk, 1024), 16) if mode == "tn" else _tile(k, tk, 128)
    nk = k // tk
    npair = len(pairs)
    if mode == "tn":
        a_spec = pl.BlockSpec((tk, tm), lambda i, j, kk: (kk, i))
    else:
        a_spec = pl.BlockSpec((tm, tk), lambda i, j, kk: (i, kk))
    if mode == "nt":
        b_spec = pl.BlockSpec((tn, tk), lambda i, j, kk: (j, kk))
    else:
        b_spec = pl.BlockSpec((tk, tn), lambda i, j, kk: (kk, j))
    o_spec = pl.BlockSpec((tm, tn), lambda i, j, kk: (i, j))
    dims = _DIMS[mode]

    def body(*refs):
        ab = refs[:2 * npair]
        res_ref = refs[2 * npair] if res is not None else None
        o_ref, acc_ref = refs[-2], refs[-1]
        kk = pl.program_id(2)

        @pl.when(kk == 0)
        def _():
            acc_ref[...] = jnp.zeros_like(acc_ref)

        part = _mm(ab[0][...], ab[1][...], dims)
        for p in range(1, npair):
            part = part + _mm(ab[2 * p][...], ab[2 * p + 1][...], dims)
        acc_ref[...] += part

        @pl.when(kk == nk - 1)
        def _():
            out = acc_ref[...] * alpha if alpha != 1.0 else acc_ref[...]
            if res_ref is not None:
                out = res_ref[...].astype(F32) + out
            o_ref[...] = out.astype(o_ref.dtype)

    args, specs = [], []
    for a, b in pairs:
        args += [a, b]
        specs += [a_spec, b_spec]
    if res is not None:
        args.append(res)
        specs.append(o_spec)
    return pl.pallas_call(
        body, grid=(m // tm, n // tn, nk), in_specs=specs, out_specs=o_spec,
        out_shape=jax.ShapeDtypeStruct((m, n), out_dtype), scratch_shapes=[pltpu.VMEM((tm, tn), F32)],
        compiler_params=_params(("parallel", "parallel", "arbitrary")), name=name)(*args)


def tilek(fn, ins, outs, *, n_rows, tr, name):
    tr = _tile(n_rows, tr, 16)
    n_in = len(ins)
    in_specs = []
    for arr, kind in ins:
        if kind == "r":
            in_specs.append(pl.BlockSpec((tr, arr.shape[1]), lambda i: (i, 0)))
        else:
            in_specs.append(pl.BlockSpec(arr.shape, lambda i, nd=arr.ndim: (0,) * nd))
    out_specs, out_shape = [], []
    has_acc = False
    for o in outs:
        if o[0] == "r":
            out_specs.append(pl.BlockSpec((tr, o[1]), lambda i: (i, 0)))
            out_shape.append(jax.ShapeDtypeStruct((n_rows, o[1]), o[2]))
        else:
            has_acc = True
            out_specs.append(pl.BlockSpec(o[1], lambda i, nd=len(o[1]): (0,) * nd))
            out_shape.append(jax.ShapeDtypeStruct(o[1], F32))

    def body(*refs):
        i = pl.program_id(0)
        vals = fn(*[r[...] for r in refs[:n_in]])
        for o, r, v in zip(outs, refs[n_in:], vals):
            if o[0] == "r":
                r[...] = v.astype(r.dtype)
            else:
                @pl.when(i == 0)
                def _(r=r):
                    r[...] = jnp.zeros_like(r)

                r[...] += v

    return pl.pallas_call(
        body, grid=(n_rows // tr,), in_specs=in_specs, out_specs=out_specs, out_shape=out_shape,
        compiler_params=_params(("arbitrary",) if has_acc else ("parallel",)), name=name)(*[a for a, _ in ins])


def rms_fwd(x, g, name):
    n, d = x.shape
    return tilek(lambda xv, gv: (_rms(xv, gv),), [(x, "r"), (g, "f")], [("r", d, MMD)], n_rows=n, tr=256, name=name)[0]


def rms_bwd(x, g, dy, dres, name):
    n, d = x.shape

    def fn(xv, gv, dyv, drv):
        _, vjp = jax.vjp(_rms, xv, gv)
        dx, dg = vjp(dyv.astype(F32))
        return drv + dx, dg

    return tilek(fn, [(x, "r"), (g, "f"), (dy, "r"), (dres, "r")], [("r", d, F32), ("acc", (1, d))],
                 n_rows=n, tr=128, name=name)


def loss_head(h, tgt, mask, g, name):
    n, d = h.shape

    def fn(hv, tv, mv, gv):
        def lossf(hh, gg):
            e = (_rms(hh, gg) - tv) * mv
            s = jnp.sum(jnp.sum(e * e, axis=1, keepdims=True), axis=0, keepdims=True)
            return s * (0.5 / d)

        l, vjp = jax.vjp(lossf, hv, gv)
        dh, dg = vjp(jnp.ones((1, 1), F32))
        return dh, dg, jnp.broadcast_to(l, (1, LANES))

    return tilek(fn, [(h, "r"), (tgt, "r"), (mask, "r"), (g, "f")],
                 [("r", d, F32), ("acc", (1, d)), ("acc", (1, LANES))], n_rows=n, tr=128, name=name)


def ffn_up(hn, wg, wu, name):
    n, d = hn.shape
    f = wg.shape[0]
    tm, tn = _tile(n, 544, 16), _tile(f, 1408, 128)

    def body(a_ref, g_ref, u_ref, og_ref, ou_ref, oa_ref):
        a = a_ref[...]
        g = _mm(a, g_ref[...], ((1,), (1,)))
        u = _mm(a, u_ref[...], ((1,), (1,)))
        og_ref[...] = g.astype(og_ref.dtype)
        ou_ref[...] = u.astype(ou_ref.dtype)
        oa_ref[...] = (g * _sigmoid(g) * u).astype(oa_ref.dtype)

    o_spec = pl.BlockSpec((tm, tn), lambda i, j: (i, j))
    w_spec = pl.BlockSpec((tn, d), lambda i, j: (j, 0))
    return pl.pallas_call(
        body, grid=(n // tm, f // tn), in_specs=[pl.BlockSpec((tm, d), lambda i, j: (i, 0)), w_spec, w_spec],
        out_specs=[o_spec, o_spec, o_spec],
        out_shape=[jax.ShapeDtypeStruct((n, f), MMD)] * 3,
        compiler_params=_params(("parallel", "parallel")), name=name)(hn, wg, wu)


def ffn_down_bwd(dh, wd, gate, up, name):
    n, d = dh.shape
    f = wd.shape[0]
    tm, tn = _tile(n, 544, 16), _tile(f, 1408, 128)

    def body(dh_ref, w_ref, g_ref, u_ref, dg_ref, du_ref):
        da = 0.5 * _mm(dh_ref[...], w_ref[...], ((1,), (1,)))
        g, u = g_ref[...].astype(F32), u_ref[...].astype(F32)
        s = _sigmoid(g)
        dg_ref[...] = (da * u * (s * (1.0 + g * (1.0 - s)))).astype(dg_ref.dtype)
        du_ref[...] = (da * (g * s)).astype(du_ref.dtype)

    o_spec = pl.BlockSpec((tm, tn), lambda i, j: (i, j))
    return pl.pallas_call(
        body, grid=(n // tm, f // tn),
        in_specs=[pl.BlockSpec((tm, d), lambda i, j: (i, 0)), pl.BlockSpec((tn, d), lambda i, j: (j, 0)), o_spec, o_spec],
        out_specs=[o_spec, o_spec],
        out_shape=[jax.ShapeDtypeStruct((n, f), MMD), jax.ShapeDtypeStruct((n, f), MMD)],
        compiler_params=_params(("parallel", "parallel")), name=name)(dh, wd, gate, up)


def ffn_forward(h, g, wg, wu, wd, tag):
    hn = rms_fwd(h, g, f"{tag}_rms")
    gate, up, act = ffn_up(hn, wg, wu, f"{tag}_up")
    out = matmul([(act, wd)], "nn", res=h, alpha=0.5, name=f"{tag}_down")
    return out, (hn, gate, up, act)


def ffn_backward(dout, h, g, wg, wu, wd, saved, tag):
    hn, gate, up, act = saved
    dgate, dup = ffn_down_bwd(dout, wd, gate, up, f"{tag}_dact")
    dwd = matmul([(act, dout)], "tn", alpha=0.5, out_dtype=MMD, name=f"{tag}_dwd")
    dwg = matmul([(dgate, hn)], "tn", out_dtype=MMD, name=f"{tag}_dwg")
    dwu = matmul([(dup, hn)], "tn", out_dtype=MMD, name=f"{tag}_dwu")
    dhn = matmul([(dgate, wg), (dup, wu)], "nn", name=f"{tag}_dhn")
    dh, dg = rms_bwd(h, g, dhn, dout, f"{tag}_drms")
    return dh, dg, dwg, dwu, dwd


def lerp_fwd(p, mu, bl, t, name):
    n, w = p.shape
    cb = _tile(w, 256, 128)

    def body(p_ref, mu_ref, o_ref):
        x = p_ref[...]
        row = lax.broadcasted_iota(jnp.int32, x.shape, 0)
        prev = jnp.where(row == 0, 0.0, pltpu.roll(x, 1, 0))
        o_ref[...] = x + mu_ref[...] * (prev - x)

    spec = pl.BlockSpec((t, cb), lambda b, j: (b, j))
    return pl.pallas_call(
        body, grid=(bl, w // cb), in_specs=[spec, pl.BlockSpec((1, cb), lambda b, j: (0, j))], out_specs=spec,
        out_shape=jax.ShapeDtypeStruct((n, w), F32), compiler_params=_params(("parallel", "parallel")), name=name)(p, mu)


def lerp_bwd(p, mu, douts, bl, t, name):
    n, w = p.shape
    cb = _tile(w, 256, 128)
    nd = len(douts)

    def body(*refs):
        p_ref, mu_ref = refs[0], refs[1]
        dp_ref, dmu_ref = refs[2 + nd], refs[3 + nd]
        b = pl.program_id(1)
        x, m = p_ref[...], mu_ref[...]
        d = refs[2][...]
        for r in refs[3:2 + nd]:
            d = d + r[...]
        row = lax.broadcasted_iota(jnp.int32, x.shape, 0)
        prev = jnp.where(row == 0, 0.0, pltpu.roll(x, 1, 0))
        z = d * m
        nxt = jnp.where(row == t - 1, 0.0, pltpu.roll(z, t - 1, 0))
        dp_ref[...] = d - z + nxt

        @pl.when(b == 0)
        def _():
            dmu_ref[...] = jnp.zeros_like(dmu_ref)

        dmu_ref[...] += jnp.sum(d * (prev - x), axis=0, keepdims=True)

    spec = pl.BlockSpec((t, cb), lambda j, b: (b, j))
    cspec = pl.BlockSpec((1, cb), lambda j, b: (0, j))
    return pl.pallas_call(
        body, grid=(w // cb, bl), in_specs=[spec, cspec] + [spec] * nd, out_specs=[spec, cspec],
        out_shape=[jax.ShapeDtypeStruct((n, w), F32), jax.ShapeDtypeStruct((1, w), F32)],
        compiler_params=_params(("parallel", "arbitrary")), name=name)(p, mu, *douts)


def _prep(k, xw, xa, xg, w0, a0, k_k, k_a, w_up, a_up, g_up, e, et):
    w_pre = -_softplus(-(w0 + mmdot(jnp.tanh(xw), w_up))) - 0.5
    decay = jnp.exp(-jnp.exp(w_pre))
    a = _sigmoid(a0 + mmdot(xa, a_up))
    g = mmdot(_sigmoid(xg), g_up)
    kk = k * k_k
    kk = kk * lax.rsqrt(jnp.maximum(segsum(kk * kk, e, et), 1e-24))
    kmod = k * (1.0 + (a - 1.0) * k_a)
    return decay, kmod, -kk, kk * a, g


def _lora_parts(xl):
    return xl[:, :LANES], xl[:, LANES:2 * LANES], xl[:, 2 * LANES:]


def rwkv_prep_fwd(pk, pl_, prm, e, et, name):
    n, d = pk.shape
    small = [prm[k] for k in ("w0", "a0", "k_k", "k_a", "w_up", "a_up", "g_up")]
    ins = [(pk, "r"), (pl_, "r")] + [(s, "f") for s in small] + [(e, "f"), (et, "f")]
    return tilek(lambda k, xl, *rest: _prep(k, *_lora_parts(xl), *rest), ins, [("r", d, F32)] * 5, n_rows=n, tr=128, name=name)


def rwkv_prep_bwd(pk, pl_, prm, e, et, cts, name):
    n, d = pk.shape
    small = [prm[k] for k in ("w0", "a0", "k_k", "k_a", "w_up", "a_up", "g_up")]

    def fn(k, xl, w0, a0, k_k, k_a, w_up, a_up, g_up, ev, etv, dw, dkm1, dkm2, dkn, db, dg):
        _, vjp = jax.vjp(lambda *a: _prep(*a, ev, etv), k, *_lora_parts(xl), w0, a0, k_k, k_a, w_up, a_up, g_up)
        dk, dxw, dxa, dxg, *dsmall = vjp((dw, dkm1 + dkm2, dkn, db, dg))
        return (dk, jnp.concatenate([dxw, dxa, dxg], axis=1), *dsmall)

    ins = [(pk, "r"), (pl_, "r")] + [(s, "f") for s in small] + [(e, "f"), (et, "f")] + [(c, "r") for c in cts]
    outs = [("r", d, F32), ("r", pl_.shape[1], F32)] + [("acc", s.shape) for s in small]
    return tilek(fn, ins, outs, n_rows=n, tr=64, name=name)


def _post(y, r, km, v, g, pga, pgb, yb, gn_w, gn_b, r_k, e, et):
    inv = 1.0 / RWKV_HEAD
    yc = y - segsum(y, e, et) * inv
    var = segsum(yc * yc, e, et) * inv
    yn = yc * lax.rsqrt(var + GN_EPS) * gn_w + gn_b
    bonus = segsum(r * km * r_k, e, et) * v
    ya = (yn + bonus) * g
    return _sigmoid(pga) * ya + _sigmoid(pgb) * yb


def rwkv_post_fwd(acts, prm, e, et, name):
    n, d = acts[0].shape
    small = [prm[k] for k in ("gn_w", "gn_b", "r_k")]
    ins = [(a, "r") for a in acts] + [(s, "f") for s in small] + [(e, "f"), (et, "f")]
    return tilek(lambda *a: (_post(*a),), ins, [("r", d, MMD)], n_rows=n, tr=128, name=name)[0]


def rwkv_post_bwd(acts, prm, e, et, dm, name):
    n, d = acts[0].shape
    small = [prm[k] for k in ("gn_w", "gn_b", "r_k")]
    na = len(acts)

    def fn(*a):
        prim, ev, etv, dmv = a[:na + 3], a[na + 3], a[na + 4], a[na + 5]
        _, vjp = jax.vjp(lambda *z: _post(*z, ev, etv), *prim)
        return vjp(dmv.astype(F32))

    ins = [(x, "r") for x in acts] + [(s, "f") for s in small] + [(e, "f"), (et, "f"), (dm, "r")]
    outs = [("r", d, F32)] * na + [("acc", s.shape) for s in small]
    return tilek(fn, ins, outs, n_rows=n, tr=64, name=name)


def _head_sums(x, first_head):
    a = jnp.sum(jnp.where(first_head, x, 0.0), axis=1, keepdims=True)
    b = jnp.sum(jnp.where(first_head, 0.0, x), axis=1, keepdims=True)
    return jnp.where(first_head, a, b)


def _round1(x):
    return (x.astype(BF16), None) if MMD == BF16 else _split2(x)


def _split2(x):
    hi = x.astype(BF16)
    return hi, (x - hi.astype(F32)).astype(BF16)


def _spread(row, eye2):
    hi, lo = _split2(row)
    return eye2 * hi, eye2 * lo


def _ones_dot(tiles, ones_blk):
    dims = (((1,), (0,)), ((), ()))
    res = lax.dot_general(jnp.concatenate([t[0] for t in tiles], axis=0), ones_blk, dims, preferred_element_type=F32)
    out = [res[i * RWKV_HEAD:(i + 1) * RWKV_HEAD] for i in range(len(tiles))]
    two_term = [i for i, t in enumerate(tiles) if t[1] is not None]
    if two_term:
        low = lax.dot_general(jnp.concatenate([tiles[i][1] for i in two_term], axis=0), ones_blk, dims,
                              preferred_element_type=F32)
        for n, i in enumerate(two_term):
            out[i] = out[i] + low[n * RWKV_HEAD:(n + 1) * RWKV_HEAD]
    return out


def _scan_consts():
    lane = lax.broadcasted_iota(jnp.int32, (1, LANES), 1)
    rows = lax.broadcasted_iota(jnp.int32, (RWKV_HEAD, LANES), 0)
    cols = lax.broadcasted_iota(jnp.int32, (RWKV_HEAD, LANES), 1)
    eye2 = ((cols & (RWKV_HEAD - 1)) == rows).astype(BF16)
    r2 = lax.broadcasted_iota(jnp.int32, (LANES, LANES), 0)
    c2 = lax.broadcasted_iota(jnp.int32, (LANES, LANES), 1)
    ones_blk = ((r2 // RWKV_HEAD) == (c2 // RWKV_HEAD)).astype(BF16)
    return lane, lane < RWKV_HEAD, eye2, ones_blk


def _direct_copies(src_ref, dst_ref, send_sems, recv_sems, local_sem, scatter):
    _, me = _peer(0)
    out = [pltpu.make_async_copy(src_ref.at[me] if scatter else src_ref, dst_ref.at[me], local_sem)]
    for k in range(1, N_DEV):
        dev, idx = _peer(k)
        out.append(pltpu.make_async_remote_copy(src_ref=src_ref.at[idx] if scatter else src_ref, dst_ref=dst_ref.at[me],
                                                send_sem=send_sems.at[k - 1], recv_sem=recv_sems.at[k - 1],
                                                device_id=dev, device_id_type=MESH))
    return out


def _riding_exchange(src_ref, dst_ref, send_sems, recv_sems, local_sem, scatter, grid):
    copies = lambda: _direct_copies(src_ref, dst_ref, send_sems, recv_sems, local_sem, scatter)
    ids = [pl.program_id(a) for a in range(len(grid))]
    first = functools.reduce(jnp.logical_and, [i == 0 for i in ids])
    last = functools.reduce(jnp.logical_and, [i == n - 1 for i, n in zip(ids, grid)])

    def start():
        @pl.when(first)
        def _():
            for cp in copies():
                cp.start()

    def finish():
        @pl.when(last)
        def _():
            for cp in copies():
                cp.wait()

    return start, finish


_RIDE_SCRATCH = [pltpu.SemaphoreType.DMA((N_DEV - 1,)), pltpu.SemaphoreType.DMA((N_DEV - 1,)), pltpu.SemaphoreType.DMA]


def scan_forward(r, w, k, kn, b, v, ride, bl, t, t_real, d, name, pg, hch):
    npair, nst = d // LANES, t // hch
    grid = (bl, npair // pg, nst)

    def body(r_ref, w_ref, k_ref, kn_ref, b_ref, v_ref, ride_ref, y_ref, hist_ref, land_ref, s_ref, vb_ref, *sems):
        start, finish = _riding_exchange(ride_ref, land_ref, *sems, False, grid)
        start()
        _, first_head, eye2, ones_blk = _scan_consts()
        eye2f = eye2.astype(F32)
        diag = lambda tile: jnp.sum(tile * eye2f, axis=0, keepdims=True)

        @pl.when(pl.program_id(2) == 0)
        def _():
            s_ref[...] = jnp.zeros_like(s_ref)

        pair_cols = [slice(p * LANES, (p + 1) * LANES) for p in range(pg)]
        for p, tile in enumerate(_ones_dot([_spread(v_ref[0, :, cols], eye2) for cols in pair_cols], ones_blk)):
            vb_ref[p] = tile

        def step(ts, carry):
            prev, nxt = jnp.maximum(ts - 1, 0), jnp.minimum(ts + 1, hch - 1)
            states, tiles = [], []
            for p in range(pg):
                cols = slice(p * LANES, (p + 1) * LANES)
                s = s_ref[p]
                hist_ref[0, p, pl.ds(ts, 1)] = s[None]
                states.append(s)
                tiles.append(_round1(s * r_ref[prev, :, cols]))
                tiles.append(_spread(v_ref[nxt, :, cols], eye2))
            res = _ones_dot(tiles, ones_blk)
            for p in range(pg):
                cols = slice(p * LANES, (p + 1) * LANES)
                s = states[p]
                sa = _head_sums(s * kn_ref[ts, :, cols], first_head)
                s_ref[p] = s * w_ref[ts, :, cols] + sa * b_ref[ts, :, cols] + vb_ref[p] * k_ref[ts, :, cols]
            for p in range(pg):
                cols = slice(p * LANES, (p + 1) * LANES)
                y_ref[prev, :, cols] = diag(res[2 * p])
                vb_ref[p] = res[2 * p + 1]
            return carry

        real = pl.program_id(2) * hch < t_real
        lax.fori_loop(0, jnp.where(real, hch, 0), step, 0)
        last = _ones_dot([_round1(s_ref[p] * r_ref[hch - 1, :, cols]) for p, cols in enumerate(pair_cols)], ones_blk)
        for p, cols in enumerate(pair_cols):
            y_ref[hch - 1, :, cols] = diag(last[p])

        @pl.when(jnp.logical_not(real))
        def _():
            y_ref[...] = jnp.zeros_like(y_ref)
            hist_ref[...] = jnp.zeros_like(hist_ref)

        finish()

    row_spec = pl.BlockSpec((hch, 1, pg * LANES), lambda bb, g, c: (bb * nst + c, 0, g))
    hist_spec = pl.BlockSpec((1, pg, hch, RWKV_HEAD, LANES), lambda bb, g, c: (bb, g, c, 0, 0))
    hbm = pl.BlockSpec(memory_space=pl.ANY)
    rows3 = [a.reshape(bl * t, 1, d) for a in (r, w, k, kn, b, v)]
    y, hist, landed = pl.pallas_call(
        body, grid=grid, in_specs=[row_spec] * 6 + [hbm], out_specs=[row_spec, hist_spec, hbm],
        out_shape=[jax.ShapeDtypeStruct((bl * t, 1, d), F32), jax.ShapeDtypeStruct((bl, npair, t, RWKV_HEAD, LANES), F32),
                   jax.ShapeDtypeStruct((N_DEV,) + ride.shape, ride.dtype)],
        scratch_shapes=[pltpu.VMEM((pg, RWKV_HEAD, LANES), F32)] * 2 + _RIDE_SCRATCH,
        compiler_params=_params(("arbitrary", "arbitrary", "arbitrary")), name=name)(*rows3, ride)
    return y.reshape(bl * t, d), hist, landed


def scan_backward(r, w, k, kn, b, v, dy, hist, ride, bl, t, t_real, d, name, pg, hch):
    npair, nst = d // LANES, t // hch
    grid = (bl, npair // pg, nst)

    def body(r_ref, w_ref, k_ref, kn_ref, b_ref, v_ref, dy_ref, hist_ref, ride_ref,
             dr_ref, dw_ref, dk_ref, dkn_ref, db_ref, dv_ref, land_ref, ds_ref, cur_ref, *sems):
        start, finish = _riding_exchange(ride_ref, land_ref, *sems, True, grid)
        start()
        _, first_head, eye2, ones_blk = _scan_consts()
        eye2f = eye2.astype(F32)
        colsum = lambda x: jnp.sum(x, axis=0, keepdims=True)

        @pl.when(pl.program_id(2) == 0)
        def _():
            ds_ref[...] = jnp.zeros_like(ds_ref)

        tiles = []
        for p in range(pg):
            cols = slice(p * LANES, (p + 1) * LANES)
            tiles += [_spread(v_ref[hch - 1, :, cols], eye2), _spread(dy_ref[hch - 1, :, cols], eye2),
                      _split2(hist_ref[0, p, hch - 1] * kn_ref[hch - 1, :, cols])]
        first = _ones_dot(tiles, ones_blk)
        for p in range(pg):
            cols = slice(p * LANES, (p + 1) * LANES)
            row = lambda ref: ref[hch - 1, :, cols]
            s_prev = hist_ref[0, p, hch - 1]
            vb, dyb, sa = first[3 * p], first[3 * p + 1], first[3 * p + 2]
            cur_ref[0, p], cur_ref[1, p] = vb, sa
            dr_ref[hch - 1, :, cols] = colsum((s_prev * row(w_ref) + sa * row(b_ref) + vb * row(k_ref)) * dyb)
            ds_ref[p] += dyb * row(r_ref)

        def step(it, carry):
            ts = hch - 1 - it
            prev = jnp.maximum(ts - 1, 0)
            has_prev = ts > 0
            grads, tiles = [], []
            for p in range(pg):
                cols = slice(p * LANES, (p + 1) * LANES)
                ds = ds_ref[p]
                grads.append(ds)
                tiles.append(_spread(v_ref[prev, :, cols], eye2))
                tiles.append(_spread(dy_ref[prev, :, cols], eye2))
                tiles.append(_split2(hist_ref[0, p, pl.ds(prev, 1)][0] * kn_ref[prev, :, cols]))
                tiles.append(_round1(ds * k_ref[ts, :, cols]))
            res = _ones_dot(tiles, ones_blk)
            for p in range(pg):
                cols = slice(p * LANES, (p + 1) * LANES)
                row = lambda ref: ref[ts, :, cols]
                ds = grads[p]
                w_, kn_, b_ = row(w_ref), row(kn_ref), row(b_ref)
                dsa = _head_sums(ds * b_, first_head)
                s_prev = hist_ref[0, p, pl.ds(ts, 1)][0]
                vb, sa, dyb_prev = cur_ref[0, p], cur_ref[1, p], res[4 * p + 1]
                dk_ref[ts, :, cols] = colsum(ds * vb)
                db_ref[ts, :, cols] = colsum(ds * sa)
                dw_ref[ts, :, cols] = colsum(ds * s_prev)
                dkn_ref[ts, :, cols] = colsum(s_prev * dsa)
                dv_ref[ts, :, cols] = colsum(res[4 * p + 3] * eye2f)
                dr_ref[prev, :, cols] = jnp.where(has_prev, colsum(s_prev * dyb_prev), dr_ref[prev, :, cols])
                ds_ref[p] = ds * w_ + dsa * kn_ + jnp.where(has_prev, dyb_prev, 0.0) * r_ref[prev, :, cols]
            for p in range(pg):
                cur_ref[0, p] = res[4 * p]
                cur_ref[1, p] = res[4 * p + 2]
            return carry

        real = (nst - 1 - pl.program_id(2)) * hch < t_real
        lax.fori_loop(0, jnp.where(real, hch, 0), step, 0)

        @pl.when(jnp.logical_not(real))
        def _():
            for ref in (dr_ref, dw_ref, dk_ref, dkn_ref, db_ref, dv_ref):
                ref[...] = jnp.zeros_like(ref)

        finish()

    row_spec = pl.BlockSpec((hch, 1, pg * LANES), lambda bb, g, c: (bb * nst + nst - 1 - c, 0, g))
    hist_spec = pl.BlockSpec((1, pg, hch, RWKV_HEAD, LANES), lambda bb, g, c: (bb, g, nst - 1 - c, 0, 0))
    hbm = pl.BlockSpec(memory_space=pl.ANY)
    row_shape = jax.ShapeDtypeStruct((bl * t, 1, d), F32)
    rows3 = [a.reshape(bl * t, 1, d) for a in (r, w, k, kn, b, v, dy)]
    outs = pl.pallas_call(
        body, grid=grid, in_specs=[row_spec] * 7 + [hist_spec, hbm], out_specs=[row_spec] * 6 + [hbm],
        out_shape=[row_shape] * 6 + [jax.ShapeDtypeStruct(ride.shape, ride.dtype)],
        scratch_shapes=[pltpu.VMEM((pg, RWKV_HEAD, LANES), F32), pltpu.VMEM((2, pg, RWKV_HEAD, LANES), F32)] + _RIDE_SCRATCH,
        compiler_params=_params(("arbitrary", "arbitrary", "arbitrary")), name=name)(*rows3, hist, ride)
    return [o.reshape(bl * t, d) for o in outs[:6]] + [outs[6]]


def _mla_norms(pm, gq, gkv):
    ql = gq.shape[1]
    kvl = gkv.shape[1]
    return _rms(pm[:, :ql], gq), _rms(pm[:, ql:ql + kvl], gkv)


def mla_prep_fwd(pm, gq, gkv, name):
    n = pm.shape[0]
    return tilek(_mla_norms, [(pm, "r"), (gq, "f"), (gkv, "f")],
                 [("r", gq.shape[1], MMD), ("r", gkv.shape[1], MMD)], n_rows=n, tr=256, name=name)


def mla_prep_bwd(pm, gq, gkv, dcq, dckv, dkpe, name):
    n, wm = pm.shape
    ql, kvl = gq.shape[1], gkv.shape[1]

    def fn(pmv, gqv, gkvv, d1, d2, d3):
        _, vjp1 = jax.vjp(_rms, pmv[:, :ql], gqv)
        _, vjp2 = jax.vjp(_rms, pmv[:, ql:ql + kvl], gkvv)
        dcq_in, dgq = vjp1(d1)
        dckv_in, dgkv = vjp2(d2)
        return jnp.concatenate([dcq_in, dckv_in, d3], axis=1), dgq, dgkv

    return tilek(fn, [(pm, "r"), (gq, "f"), (gkv, "f"), (dcq, "r"), (dckv, "r"), (dkpe, "r")],
                 [("r", wm, F32), ("acc", gq.shape), ("acc", gkv.shape)], n_rows=n, tr=128, name=name)


def _rope(x, c, s, first):
    sw = jnp.where(first, pltpu.roll(x, LANES - ROPE_DIM // 2, 1), pltpu.roll(x, ROPE_DIM // 2, 1))
    return x * c + sw * s


def _unrope(d, c, s, first):
    z = d * s
    sw = jnp.where(first, pltpu.roll(z, LANES - ROPE_DIM // 2, 1), pltpu.roll(z, ROPE_DIM // 2, 1))
    return d * c + sw


def _causal_segments(n_tiles, parts=17):
    bounds = sorted({round(n_tiles * s / parts) for s in range(parts + 1)})
    return list(zip(bounds[:-1], bounds[1:]))


def attn_fwd(q, kv, pm, ct, st, bl, t, hm, name):
    n = q.shape[0]
    tq = LANES
    scale = QK_DIM ** -0.5
    kpe_blk = pm.shape[1] // LANES - 1

    def body(qn_ref, qpe_ref, kn_ref, v_ref, kpe_ref, ct_ref, st_ref, o_ref, lse_ref, kp_s, kn_s, v_s):
        h = pl.program_id(1)
        lane = lax.broadcasted_iota(jnp.int32, (1, LANES), 1)
        first = (lane & (ROPE_DIM - 1)) < ROPE_DIM // 2
        kp = _rope(kpe_ref[...], ct_ref[...], st_ref[...], first)
        kp_s[...] = jnp.where(h % 2 == 0, kp, pltpu.roll(kp, ROPE_DIM, 1)).astype(MMD)
        kn_s[...] = kn_ref[...].astype(MMD)
        v_s[...] = v_ref[...].astype(MMD)
        def segment(lo, hi):
            ext = hi * tq
            kpos = lax.broadcasted_iota(jnp.int32, (1, ext), 1)

            def qtile(i, carry):
                rows = pl.ds(pl.multiple_of(i * tq, tq), tq)
                q2 = _rope(qpe_ref[rows, :], ct_ref[rows, :], st_ref[rows, :], first)
                s = (_mm(qn_ref[rows, :], kn_s[:ext, :], ((1,), (1,))) + _mm(q2, kp_s[:ext, :], ((1,), (1,)))) * scale
                qpos = i * tq + lax.broadcasted_iota(jnp.int32, (tq, 1), 0)
                s = jnp.where(kpos <= qpos, s, -1e30)
                m = jnp.max(s, axis=1, keepdims=True)
                p = jnp.exp(s - m)
                l = jnp.sum(p, axis=1, keepdims=True)
                o_ref[rows, :] = _mm(p, v_s[:ext, :]) / l
                lse_ref[0, 0, rows, :] = m + jnp.log(l)
                return carry

            lax.fori_loop(lo, hi, qtile, 0)

        for lo, hi in _causal_segments(t // tq):
            segment(lo, hi)

    blk = lambda f: pl.BlockSpec((t, LANES), f)
    return pl.pallas_call(
        body, grid=(bl, hm),
        in_specs=[blk(lambda b, h: (b, h)), blk(lambda b, h: (b, hm + h // 2)), blk(lambda b, h: (b, h)),
                  blk(lambda b, h: (b, hm + h)), blk(lambda b, h: (b, kpe_blk)), blk(lambda b, h: (0, 0)), blk(lambda b, h: (0, 0))],
        out_specs=[blk(lambda b, h: (b, h)), pl.BlockSpec((1, 1, t, 1), lambda b, h: (b, h, 0, 0))],
        out_shape=[jax.ShapeDtypeStruct((n, hm * LANES), F32), jax.ShapeDtypeStruct((bl, hm, t, 1), F32)],
        scratch_shapes=[pltpu.VMEM((t, LANES), MMD)] * 3,
        compiler_params=_params(("parallel", "arbitrary")), name=name)(q, q, kv, kv, pm, ct, st)


def attn_bwd(q, kv, pm, o, do, lse, ct, st, bl, t, hm, name):
    n = q.shape[0]
    tq = LANES
    scale = QK_DIM ** -0.5
    kpe_blk = pm.shape[1] // LANES - 1

    def body(qn_ref, qpe_ref, kn_ref, v_ref, kpe_ref, o_ref, do_ref, lse_ref, ct_ref, st_ref,
             dqn_ref, dqpe_ref, dkn_ref, dv_ref, dkpe_ref, kp_s, kn_s, v_s, dkn_s, dkp_s, dv_s):
        h = pl.program_id(1)
        lane = lax.broadcasted_iota(jnp.int32, (1, LANES), 1)
        first = (lane & (ROPE_DIM - 1)) < ROPE_DIM // 2
        mine = (lane // ROPE_DIM) == (h % 2)
        kp = _rope(kpe_ref[...], ct_ref[...], st_ref[...], first)
        kp_s[...] = jnp.where(h % 2 == 0, kp, pltpu.roll(kp, ROPE_DIM, 1)).astype(MMD)
        kn_s[...] = kn_ref[...].astype(MMD)
        v_s[...] = v_ref[...].astype(MMD)
        dkn_s[...] = jnp.zeros_like(dkn_s)
        dkp_s[...] = jnp.zeros_like(dkp_s)
        dv_s[...] = jnp.zeros_like(dv_s)
        @pl.when(h % 2 == 0)
        def _():
            dqpe_ref[...] = jnp.zeros_like(dqpe_ref)

        @pl.when(h == 0)
        def _():
            dkpe_ref[...] = jnp.zeros_like(dkpe_ref)

        def segment(lo, hi):
            ext = hi * tq
            kpos = lax.broadcasted_iota(jnp.int32, (1, ext), 1)

            def qtile(i, carry):
                rows = pl.ds(pl.multiple_of(i * tq, tq), tq)
                c_i, s_i = ct_ref[rows, :], st_ref[rows, :]
                q1 = qn_ref[rows, :].astype(MMD)
                q2 = _rope(qpe_ref[rows, :], c_i, s_i, first).astype(MMD)
                s = (_mm(q1, kn_s[:ext, :], ((1,), (1,))) + _mm(q2, kp_s[:ext, :], ((1,), (1,)))) * scale
                qpos = i * tq + lax.broadcasted_iota(jnp.int32, (tq, 1), 0)
                p = jnp.where(kpos <= qpos, jnp.exp(s - lse_ref[0, 0, rows, :]), 0.0)
                do_i = do_ref[rows, :]
                delta = jnp.sum(do_i * o_ref[rows, :], axis=1, keepdims=True)
                dp = _mm(do_i, v_s[:ext, :], ((1,), (1,)))
                ds = (p * (dp - delta) * scale).astype(MMD)
                dqn_ref[rows, :] = _mm(ds, kn_s[:ext, :])
                dq2 = jnp.where(mine, _mm(ds, kp_s[:ext, :]), 0.0)
                dqpe_ref[rows, :] += _unrope(dq2, c_i, s_i, first)
                dkn_s[:ext, :] += _mm(ds, q1, ((0,), (0,)))
                dkp_s[:ext, :] += _mm(ds, q2, ((0,), (0,)))
                dv_s[:ext, :] += _mm(p, do_i, ((0,), (0,)))
                return carry

            lax.fori_loop(lo, hi, qtile, 0)

        for lo, hi in _causal_segments(t // tq):
            segment(lo, hi)
        dkn_ref[...] = dkn_s[...]
        dv_ref[...] = dv_s[...]
        dkp = jnp.where(mine, dkp_s[...], 0.0)
        dkp = jnp.where(h % 2 == 0, dkp, pltpu.roll(dkp, ROPE_DIM, 1))
        dkpe_ref[...] += _unrope(dkp, ct_ref[...], st_ref[...], first)

    blk = lambda f: pl.BlockSpec((t, LANES), f)
    hd = lambda b, h: (b, h)
    shp = lambda wd: jax.ShapeDtypeStruct((n, wd), F32)
    return pl.pallas_call(
        body, grid=(bl, hm),
        in_specs=[blk(hd), blk(lambda b, h: (b, hm + h // 2)), blk(hd), blk(lambda b, h: (b, hm + h)),
                  blk(lambda b, h: (b, kpe_blk)), blk(hd), blk(hd), pl.BlockSpec((1, 1, t, 1), lambda b, h: (b, h, 0, 0)),
                  blk(lambda b, h: (0, 0)), blk(lambda b, h: (0, 0))],
        out_specs=[blk(hd), blk(lambda b, h: (b, h // 2)), blk(hd), blk(hd), blk(lambda b, h: (b, 0))],
        out_shape=[shp(hm * LANES), shp(hm * ROPE_DIM), shp(hm * LANES), shp(hm * LANES), shp(LANES)],
        scratch_shapes=[pltpu.VMEM((t, LANES), MMD)] * 3 + [pltpu.VMEM((t, LANES), F32)] * 3,
        compiler_params=_params(("parallel", "arbitrary")), name=name)(q, q, kv, kv, pm, o, do, lse, ct, st)


def _peer(k):
    mx, my, mc = lax.axis_index("x"), lax.axis_index("y"), lax.axis_index("c")
    px = 1 - mx if k & 4 else mx
    py = 1 - my if k & 2 else my
    pc = 1 - mc if k & 1 else mc
    return (px, py, pc), 4 * px + 2 * py + pc


def _chips():
    mx, my, mc = lax.axis_index("x"), lax.axis_index("y"), lax.axis_index("c")
    return (mx, my, mc), (mx, my, 1 - mc), [(1 - mx, my), (mx, 1 - my), (1 - mx, 1 - my)]


def _riders(refs, gathers, scatters):
    n = gathers + scatters
    ins, outs, sems = refs[:n], refs[n:2 * n], refs[2 * n:]
    copies = []
    for i in range(n):
        copies += _direct_copies(ins[i], outs[i], *sems[3 * i:3 * i + 3], i >= gathers)
    return copies


def _rider_shapes(gathers, scatters):
    shapes = [jax.ShapeDtypeStruct((N_DEV,) + a.shape, a.dtype) for a in gathers]
    return shapes + [jax.ShapeDtypeStruct(a.shape, a.dtype) for a in scatters]


def all_gather_two_level(x, name, also=()):
    na = len(also)

    def body(*refs):
        x_ref, o_ref = refs[0], refs[1 + na]
        send_sems, recv_sems, local_sem = refs[2 + 2 * na:5 + 2 * na]
        riders = _riders(refs[1:1 + na] + refs[2 + na:2 + 2 * na] + refs[5 + 2 * na:], na, 0)
        for cp in riders:
            cp.start()
        me, sibling, chips = _chips()
        blk = lambda px, py, pc: o_ref.at[4 * px + 2 * py + pc]

        def copy(k, block, to, src=None):
            return pltpu.make_async_remote_copy(src_ref=blk(*block) if src is None else src, dst_ref=blk(*block),
                                                send_sem=send_sems.at[k], recv_sem=recv_sems.at[k], device_id=to,
                                                device_id_type=MESH)

        mine = pltpu.make_async_copy(x_ref, blk(*me), local_sem)
        mine.start()
        first = [copy(0, me, sibling, src=x_ref)] + [copy(1 + j, me, (*chip, me[2]), src=x_ref) for j, chip in enumerate(chips)]
        for cp in first:
            cp.start()
        passed = [copy(4 + j, (*chip, me[2]), sibling) for j, chip in enumerate(chips)]
        for j, chip in enumerate(chips):
            copy(1 + j, (*chip, me[2]), me).wait_recv()
            passed[j].start()
        copy(0, sibling, me).wait_recv()
        for j, chip in enumerate(chips):
            copy(4 + j, (*chip, 1 - me[2]), me).wait_recv()
        for cp in first + passed:
            cp.wait_send()
        mine.wait()
        for cp in riders:
            cp.wait()

    hbm = pl.BlockSpec(memory_space=pl.ANY)
    return pl.pallas_call(
        body, in_specs=[hbm] * (1 + na), out_specs=[hbm] * (1 + na),
        out_shape=[jax.ShapeDtypeStruct((N_DEV,) + x.shape, x.dtype)] + _rider_shapes(also, ()),
        scratch_shapes=_RIDE_SCRATCH * (1 + na), name=name)(x, *also)


def exchange_sibling(x, name):
    def body(x_ref, o_ref, send_sems, recv_sems):
        me, sibling, _ = _chips()
        copies = []
        for q in range(N_DEV // 2):
            cp = pltpu.make_async_remote_copy(src_ref=x_ref.at[2 * q + 1 - me[2]], dst_ref=o_ref.at[q], send_sem=send_sems.at[q],
                                              recv_sem=recv_sems.at[q], device_id=sibling, device_id_type=MESH)
            cp.start()
            copies.append(cp)
        for cp in copies:
            cp.wait()

    return pl.pallas_call(
        body, in_specs=[pl.BlockSpec(memory_space=pl.ANY)], out_specs=pl.BlockSpec(memory_space=pl.ANY),
        out_shape=jax.ShapeDtypeStruct((N_DEV // 2,) + x.shape[1:], x.dtype),
        scratch_shapes=[pltpu.SemaphoreType.DMA((N_DEV // 2,)), pltpu.SemaphoreType.DMA((N_DEV // 2,))], name=name)(x)


def exchange_chips(x, name, gathers=(), scatters=()):
    na = len(gathers) + len(scatters)

    def body(*refs):
        x_ref, o_ref = refs[0], refs[1 + na]
        send_sems, recv_sems, local_sem = refs[2 + 2 * na:5 + 2 * na]
        riders = _riders(refs[1:1 + na] + refs[2 + na:2 + 2 * na] + refs[5 + 2 * na:], len(gathers), len(scatters))
        for cp in riders:
            cp.start()
        me, _, chips = _chips()
        here = 2 * me[0] + me[1]
        local = pltpu.make_async_copy(x_ref.at[here], o_ref.at[here], local_sem)
        local.start()
        copies = []
        for j, (px, py) in enumerate(chips):
            cp = pltpu.make_async_remote_copy(src_ref=x_ref.at[2 * px + py], dst_ref=o_ref.at[here], send_sem=send_sems.at[j],
                                              recv_sem=recv_sems.at[j], device_id=(px, py, me[2]), device_id_type=MESH)
            cp.start()
            copies.append(cp)
        for cp in copies:
            cp.wait()
        local.wait()
        for cp in riders:
            cp.wait()

    hbm = pl.BlockSpec(memory_space=pl.ANY)
    return pl.pallas_call(
        body, in_specs=[hbm] * (1 + na), out_specs=[hbm] * (1 + na),
        out_shape=[jax.ShapeDtypeStruct(x.shape, x.dtype)] + _rider_shapes(gathers, scatters),
        scratch_shapes=[pltpu.SemaphoreType.DMA((3,)), pltpu.SemaphoreType.DMA((3,)), pltpu.SemaphoreType.DMA] + _RIDE_SCRATCH * na,
        name=name)(x, *gathers, *scatters)


def add_blocks(a, b, name):
    q, r, c = a.shape
    tr = _tile(r, max(16, (2 << 20) // (c * a.dtype.itemsize)), 16)
    spec = pl.BlockSpec((1, tr, c), lambda i, j: (i, j, 0))

    def body(a_ref, b_ref, o_ref):
        o_ref[...] = (a_ref[...].astype(F32) + b_ref[...].astype(F32)).astype(o_ref.dtype)

    return pl.pallas_call(
        body, grid=(q, r // tr), in_specs=[spec, spec], out_specs=spec, out_shape=jax.ShapeDtypeStruct(a.shape, a.dtype),
        compiler_params=_params(("parallel", "parallel")), name=name)(a, b)


def reduce_scatter_two_level(x, tag, gathers=(), scatters=()):
    q = N_DEV // 2
    from_sibling = exchange_sibling(x, f"{tag}_sibling")
    mine = lax.dynamic_index_in_dim(x.reshape((q, 2) + x.shape[1:]), lax.axis_index("c"), axis=1, keepdims=False)
    chip_sums = add_blocks(mine, from_sibling, f"{tag}_pair_sum")
    from_chips, *small = exchange_chips(chip_sums, f"{tag}_chips", gathers, scatters)
    return (sum_blocks(from_chips, f"{tag}_sum"), *small)


def sum_blocks(x, name):
    nb, r, c = x.shape
    tr = _tile(r, max(16, (4 << 20) // (nb * c * x.dtype.itemsize)), 16)

    def body(x_ref, o_ref):
        acc = x_ref[0].astype(F32)
        for i in range(1, nb):
            acc = acc + x_ref[i].astype(F32)
        o_ref[...] = acc

    return pl.pallas_call(
        body, grid=(r // tr,), in_specs=[pl.BlockSpec((nb, tr, c), lambda i: (0, i, 0))],
        out_specs=pl.BlockSpec((tr, c), lambda i: (i, 0)), out_shape=jax.ShapeDtypeStruct((r, c), F32),
        compiler_params=_params(("parallel",)), name=name)(x)


def _adamw(w, g, m, v):
    m = ADAM_B1 * m + (1.0 - ADAM_B1) * g
    v = ADAM_B2 * v + (1.0 - ADAM_B2) * jnp.square(g)
    m_hat = m / (1.0 - ADAM_B1 ** ADAM_STEP)
    v_hat = v / (1.0 - ADAM_B2 ** ADAM_STEP)
    delta = -ADAM_LR * (m_hat / (jnp.sqrt(v_hat) + ADAM_EPS) + ADAM_WD * w)
    return delta, m, v


def adamw(w, g, m, v, name):
    r, c = w.shape
    tr = _tile(r, 256, 8)
    spec = pl.BlockSpec((tr, c), lambda i: (i, 0))

    def body(w_ref, g_ref, m_ref, v_ref, d_ref, nm_ref, nv_ref):
        d_ref[...], nm_ref[...], nv_ref[...] = _adamw(w_ref[...], g_ref[...], m_ref[...], v_ref[...])

    return pl.pallas_call(
        body, grid=(r // tr,), in_specs=[spec] * 4, out_specs=[spec] * 3,
        out_shape=[jax.ShapeDtypeStruct((r, c), F32)] * 3, compiler_params=_params(("parallel",)), name=name)(w, g, m, v)


def batch_sum_rows(dh, bl, t, rows, name):
    d = dh.shape[1]

    def body(x_ref, o_ref):
        @pl.when(pl.program_id(0) == 0)
        def _():
            o_ref[...] = jnp.zeros_like(o_ref)

        o_ref[...] += x_ref[...]

    return pl.pallas_call(
        body, grid=(bl,), in_specs=[pl.BlockSpec((rows, d), lambda b: (b * (t // rows), 0))],
        out_specs=pl.BlockSpec((rows, d), lambda b: (0, 0)), out_shape=jax.ShapeDtypeStruct((rows, d), F32),
        compiler_params=_params(("arbitrary",)), name=name)(dh)


class Dims:
    def __init__(self, x, w_up, g_up, q_norm, kv_norm, d_ff):
        self.bl, self.seq, self.d = x.shape
        self.n_meta = 16
        self.t_real = self.n_meta + self.seq
        self.t = -(-self.t_real // LANES) * LANES
        self.n = self.bl * self.t
        self.f = d_ff
        self.wl, self.gl = w_up.shape[-2], g_up.shape[-2]
        self.ql, self.kvl = q_norm.shape[-1], kv_norm.shape[-1]
        self.hm = self.d // V_DIM
        self.in_cols = 5 * self.d + 2 * self.wl + self.gl + self.ql + self.kvl + ROPE_DIM


def _pad_cols(a, width):
    return jnp.pad(a, ((0, 0), (0, width - a.shape[1])))


def _pad_rows(a, rows):
    return jnp.pad(a, ((0, rows - a.shape[0]), (0, 0)))


def split_in(a, dm, axis=1):
    d, wl, gl, ql, kvl = dm.d, dm.wl, dm.gl, dm.ql, dm.kvl
    size = a.shape[axis]
    cut = lambda lo, hi: lax.slice_in_dim(a, min(lo, size), min(hi, size), axis=axis)

    def pad(p, width):
        cfg = [(0, 0)] * a.ndim
        cfg[axis] = (0, width - p.shape[axis])
        return jnp.pad(p, cfg)

    o = 3 * d
    lora = jnp.concatenate([pad(cut(o, o + wl), LANES), pad(cut(o + wl, o + 2 * wl), LANES),
                            cut(o + 2 * wl, o + 2 * wl + gl)], axis=axis)
    o += 2 * wl + gl
    mla = pad(cut(o, o + ql + kvl + ROPE_DIM), ql + kvl + LANES)
    o += ql + kvl + ROPE_DIM
    return dict(r=cut(0, d), k=cut(d, 2 * d), v=cut(2 * d, 3 * d), l=lora, m=mla, ga=cut(o, o + d), gb=cut(o + d, o + 2 * d))


def merge_in(g, dm, axis=1):
    wl, gl, ql, kvl = dm.wl, dm.gl, dm.ql, dm.kvl
    cut = lambda p, lo, hi: lax.slice_in_dim(p, lo, hi, axis=axis)
    l, m = g["l"], g["m"]
    return jnp.concatenate([g["r"], g["k"], g["v"], cut(l, 0, wl), cut(l, LANES, LANES + wl), cut(l, 2 * LANES, 2 * LANES + gl),
                            cut(m, 0, ql + kvl + ROPE_DIM), g["ga"], g["gb"]], axis=axis)


def split_uq(w, dm):
    w3 = w.reshape(w.shape[0], dm.hm, QK_DIM)
    return jnp.concatenate([w3[:, :, :NOPE_DIM].reshape(w.shape[0], -1), w3[:, :, NOPE_DIM:].reshape(w.shape[0], -1)], axis=1)


def merge_uq(gn, gp, dm):
    r = gn.shape[0]
    return jnp.concatenate([gn.reshape(r, dm.hm, NOPE_DIM), gp.reshape(r, dm.hm, ROPE_DIM)], axis=2).reshape(r, -1)


def split_ukv(w, dm):
    w3 = w.reshape(w.shape[0], dm.hm, NOPE_DIM + V_DIM)
    return jnp.concatenate([w3[:, :, :NOPE_DIM].reshape(w.shape[0], -1), w3[:, :, NOPE_DIM:].reshape(w.shape[0], -1)], axis=1)


def merge_ukv(gk, gv, dm):
    r = gk.shape[0]
    return jnp.concatenate([gk.reshape(r, dm.hm, NOPE_DIM), gv.reshape(r, dm.hm, V_DIM)], axis=2).reshape(r, -1)


def head_matrices(d):
    heads = d // RWKV_HEAD
    e = (np.arange(d)[:, None] // RWKV_HEAD == np.arange(LANES)[None, :]) & (np.arange(LANES)[None, :] < heads)
    return jnp.asarray(e, BF16), jnp.asarray(e.T, BF16)


def rope_tables(t):
    pos = jnp.arange(t, dtype=F32)
    inv_freq = 1.0 / (ROPE_THETA ** (jnp.arange(0, ROPE_DIM, 2, dtype=F32) / ROPE_DIM))
    ang = pos[:, None] * inv_freq[None, :]
    cos, sin = jnp.cos(ang), jnp.sin(ang)
    return jnp.tile(jnp.concatenate([cos, cos], axis=1), (1, 2)), jnp.tile(jnp.concatenate([-sin, sin], axis=1), (1, 2))


def local_step(dm, x, loss_target, meta, wt, late_shards, late_rows, sp):
    bl, t, n, d, hm = dm.bl, dm.t, dm.n, dm.d, dm.hm
    e, et = head_matrices(d)
    ct, st = rope_tables(t)
    padz = jnp.zeros((bl, t - dm.t_real, d), F32)
    h0 = jnp.concatenate([jnp.broadcast_to(meta[None], (bl, dm.n_meta, d)), x, padz], axis=1).reshape(n, d)
    tgt = jnp.concatenate([jnp.zeros((bl, dm.n_meta, d), F32), loss_target, padz], axis=1).reshape(n, d)
    tpos = jnp.arange(t)
    mask = jnp.tile(((tpos >= dm.n_meta) & (tpos < dm.t_real)).astype(F32), bl).reshape(n, 1)

    win = split_in(wt["w_in"], dm, axis=0)
    mu = split_in(sp["tm_mu"], dm)
    wq, wkv = split_uq(wt["w_uq"], dm), split_ukv(wt["w_ukv"], dm)
    prm = dict(w0=sp["w0"], a0=sp["a0"], k_k=sp["k_k"], k_a=sp["k_a"], gn_w=sp["gn_w"], gn_b=sp["gn_b"], r_k=sp["r_k"],
               w_up=_pad_rows(wt["w_up"], LANES).astype(F32), a_up=_pad_rows(wt["a_up"], LANES).astype(F32),
               g_up=wt["g_up"].astype(F32))

    h1, ffn1 = ffn_forward(h0, sp["ffn1_norm"], wt["ffn1_w_gate"], wt["ffn1_w_up"], wt["ffn1_w_down"], "ffn1")
    u = rms_fwd(h1, sp["mix_norm"], "mix_rms")
    proj = {key: matmul([(u, win[key])], "nt", name=f"proj_{key}") for key in win}
    sh = {key: lerp_fwd(proj[key], mu[key], bl, t, f"shift_{key}") for key in ("r", "k", "v", "l")}
    decay, kmod, kneg, bvec, gate = rwkv_prep_fwd(sh["k"], sh["l"], prm, e, et, "rwkv_prep")
    pairs = min(SCAN_PAIRS, d // LANES)
    y, hist, late_all = scan_forward(sh["r"], decay, kmod, kneg, bvec, sh["v"], late_shards, bl, t, dm.t_real, d, "wkv_scan",
                                     pairs, SCAN_FWD_STEPS)
    wt = dict(wt, **{key: late_all[:, lo:hi].reshape(-1, d) for key, lo, hi in zip(LATE, late_rows[:-1], late_rows[1:])})
    cqn, ckvn = mla_prep_fwd(proj["m"], sp["q_norm"], sp["kv_norm"], "mla_norms")
    q = matmul([(cqn, wq)], "nn", name="mla_q")
    kv = matmul([(ckvn, wkv)], "nn", name="mla_kv")
    o, lse = attn_fwd(q, kv, proj["m"], ct, st, bl, t, hm, "mla_attn")
    post_in = [y, sh["r"], kmod, sh["v"], gate, proj["ga"], proj["gb"], o]
    mix = rwkv_post_fwd(post_in, prm, e, et, "mix_gate")
    h2 = matmul([(mix, wt["w_out"])], "nn", res=h1, name="out_proj")
    h3, ffn2 = ffn_forward(h2, sp["ffn2_norm"], wt["ffn2_w_gate"], wt["ffn2_w_up"], wt["ffn2_w_down"], "ffn2")
    dh3, d_final, loss = loss_head(h3, tgt, mask, sp["final_norm"], "loss_head")

    gw, gs = {}, {"final_norm": d_final}
    dh2, gs["ffn2_norm"], gw["ffn2_w_gate"], gw["ffn2_w_up"], gw["ffn2_w_down"] = ffn_backward(
        dh3, h2, sp["ffn2_norm"], wt["ffn2_w_gate"], wt["ffn2_w_up"], wt["ffn2_w_down"], ffn2, "ffn2")
    dmix = matmul([(dh2, wt["w_out"])], "nt", name="out_proj_dx")
    gw["w_out"] = matmul([(mix, dh2)], "tn", out_dtype=MMD, name="out_proj_dw")
    late_grads = jnp.concatenate([gw.pop(key).reshape(N_DEV, hi - lo, d) for key, lo, hi in
                                  zip(LATE, late_rows[:-1], late_rows[1:])], axis=1).astype(MMD)
    (dy, dr_p, dkm_p, dv_p, dgate, dpga, dpgb, do, gs["gn_w"], gs["gn_b"], gs["r_k"]) = rwkv_post_bwd(
        post_in, prm, e, et, dmix, "mix_gate_bwd")
    dqn, dqpe, dkn, dv_att, dkpe = attn_bwd(q, kv, proj["m"], o, do, lse, ct, st, bl, t, hm, "mla_attn_bwd")
    nq = hm * NOPE_DIM
    dcqn = matmul([(dqn, wq[:, :nq])], "nt", name="mla_q_dx1")
    dcqn = matmul([(dqpe, wq[:, nq:])], "nt", res=dcqn, name="mla_q_dx2")
    gw["w_uq"] = merge_uq(matmul([(cqn, dqn)], "tn", name="mla_q_dw1"), matmul([(cqn, dqpe)], "tn", name="mla_q_dw2"), dm)
    dckvn = matmul([(dkn, wkv[:, :nq]), (dv_att, wkv[:, nq:])], "nt", name="mla_kv_dx", tk=1024)
    gw["w_ukv"] = merge_ukv(matmul([(ckvn, dkn)], "tn", name="mla_kv_dw1"), matmul([(ckvn, dv_att)], "tn", name="mla_kv_dw2"), dm)
    dproj = {"ga": dpga, "gb": dpgb}
    dproj["m"], gs["q_norm"], gs["kv_norm"] = mla_prep_bwd(proj["m"], sp["q_norm"], sp["kv_norm"], dcqn, dckvn, dkpe, "mla_norms_bwd")
    dr_s, ddecay, dk_s, dkneg, dbvec, dv_s, late_recv = scan_backward(
        sh["r"], decay, kmod, kneg, bvec, sh["v"], dy, hist, late_grads, bl, t, dm.t_real, d, "wkv_scan_bwd", pairs,
        SCAN_BWD_STEPS)
    late_sum = sum_blocks(late_recv, "sum_late")
    (dsh_k, dsh_l, gs["w0"], gs["a0"], gs["k_k"], gs["k_a"], g_wup, g_aup, gw["g_up"]) = rwkv_prep_bwd(
        sh["k"], sh["l"], prm, e, et, [ddecay, dk_s, dkm_p, dkneg, dbvec, dgate], "rwkv_prep_bwd")
    gw["w_up"], gw["a_up"] = g_wup[:dm.wl], g_aup[:dm.wl]
    dmu = {}
    for key, cts in (("r", [dr_s, dr_p]), ("k", [dsh_k]), ("v", [dv_s, dv_p]), ("l", [dsh_l])):
        dproj[key], dmu[key] = lerp_bwd(proj[key], mu[key], cts, bl, t, f"shift_{key}_bwd")
    zero_m = jnp.zeros((1, proj["m"].shape[1]), F32)
    gs["tm_mu"] = merge_in(dict(dmu, m=zero_m, ga=zero_m[:, :0], gb=zero_m[:, :0]), dm)[:, :3 * d + 2 * dm.wl + dm.gl]
    wide = ("r", "k", "v", "ga", "gb")
    du = matmul([(dproj[key], win[key]) for key in wide], "nn", name="proj_dx", tn=512, tk=512)
    du = matmul([(dproj["l"], win["l"])], "nn", res=du, name="proj_dx_l")
    du = matmul([(dproj["m"], win["m"])], "nn", res=du, name="proj_dx_m")
    gw["w_in"] = merge_in({key: matmul([(dproj[key], u)], "tn", out_dtype=MMD, name=f"proj_dw_{key}") for key in win},
                          dm, axis=0)
    dh1, gs["mix_norm"] = rms_bwd(h1, sp["mix_norm"], du, dh2, "mix_rms_bwd")
    dh0, gs["ffn1_norm"], gw["ffn1_w_gate"], gw["ffn1_w_up"], gw["ffn1_w_down"] = ffn_backward(
        dh1, h0, sp["ffn1_norm"], wt["ffn1_w_gate"], wt["ffn1_w_up"], wt["ffn1_w_down"], ffn1, "ffn1")
    grad_x = dh0.reshape(bl, t, d)[:, dm.n_meta:dm.t_real]
    dmeta = batch_sum_rows(dh0, bl, t, dm.n_meta, "meta_grad")
    return loss, grad_x, dmeta, gw, late_sum, gs


TRANSPOSED = ("ffn1_w_gate", "ffn1_w_up", "w_in", "ffn2_w_gate", "ffn2_w_up")
EARLY = ("ffn1_w_gate", "ffn1_w_up", "ffn1_w_down", "w_in")
LATE = ("w_out", "ffn2_w_gate", "ffn2_w_up", "ffn2_w_down")
NARROW = ("w_up", "a_up", "g_up", "w_uq", "w_ukv")
MATRICES = ("ffn1_w_gate", "ffn1_w_up", "ffn1_w_down", "w_in", "w_up", "a_up", "g_up", "w_uq", "w_ukv", "w_out",
            "ffn2_w_gate", "ffn2_w_up", "ffn2_w_down")
SMALL = ("ffn1_norm", "mix_norm", "tm_mu", "w0", "a0", "k_k", "k_a", "r_k", "gn_w", "gn_b", "q_norm", "kv_norm",
         "ffn2_norm", "final_norm")
WEIGHTS = ("meta_tokens", "ffn1_norm", "ffn1_w_gate", "ffn1_w_up", "ffn1_w_down", "mix_norm", "w_in", "tm_mu", "w0", "w_up",
           "a0", "a_up", "g_up", "k_k", "k_a", "r_k", "gn_w", "gn_b", "q_norm", "w_uq", "kv_norm", "w_ukv", "w_out",
           "ffn2_norm", "ffn2_w_gate", "ffn2_w_up", "ffn2_w_down", "final_norm")
PACK_COLS = 1024
PACK_ALIGN = 16 * PACK_COLS


def _pack(parts):
    offs, o = [], 0
    for p in parts:
        offs.append(o)
        o += p.shape[1]
    total = -(-o // PACK_ALIGN) * PACK_ALIGN
    flat = jnp.concatenate(list(parts) + [jnp.zeros((parts[0].shape[0], total - o), parts[0].dtype)], axis=1)
    return flat.reshape(parts[0].shape[0], total // PACK_COLS, PACK_COLS), offs


def kernel(x, meta_tokens, ffn1_norm, ffn1_w_gate, ffn1_w_up, ffn1_w_down, mix_norm, w_in, tm_mu, w0, w_up, a0, a_up, g_up, k_k, k_a, r_k, gn_w, gn_b, q_norm, w_uq, kv_norm, w_ukv, w_out, ffn2_norm, ffn2_w_gate, ffn2_w_up, ffn2_w_down, final_norm, loss_target, m_meta_tokens, m_ffn1_norm, m_ffn1_w_gate, m_ffn1_w_up, m_ffn1_w_down, m_mix_norm, m_w_in, m_tm_mu, m_w0, m_w_up, m_a0, m_a_up, m_g_up, m_k_k, m_k_a, m_r_k, m_gn_w, m_gn_b, m_q_norm, m_w_uq, m_kv_norm, m_w_ukv, m_w_out, m_ffn2_norm, m_ffn2_w_gate, m_ffn2_w_up, m_ffn2_w_down, m_final_norm, v_meta_tokens, v_ffn1_norm, v_ffn1_w_gate, v_ffn1_w_up, v_ffn1_w_down, v_mix_norm, v_w_in, v_tm_mu, v_w0, v_w_up, v_a0, v_a_up, v_g_up, v_k_k, v_k_a, v_r_k, v_gn_w, v_gn_b, v_q_norm, v_w_uq, v_kv_norm, v_w_ukv, v_w_out, v_ffn2_norm, v_ffn2_w_gate, v_ffn2_w_up, v_ffn2_w_down, v_final_norm):
    args = dict(locals())
    wts = {k: args[k] for k in WEIGHTS}
    ms = {k: args["m_" + k] for k in WEIGHTS}
    vs = {k: args["v_" + k] for k in WEIGHTS}
    dm = Dims(x, w_up, g_up, q_norm, kv_norm, ffn1_w_down.shape[1] * N_DEV)

    shard2d = {k: wts[k].reshape(wts[k].shape[-2], wts[k].shape[-1]) for k in MATRICES}
    sent = {k: shard2d[k].astype(MMD).T if k in TRANSPOSED else shard2d[k] for k in MATRICES}
    early_rows = np.cumsum([0] + [sent[k].shape[0] for k in EARLY])
    late_rows = np.cumsum([0] + [sent[k].shape[0] for k in LATE])
    send, offs = _pack([sent[k].astype(MMD).reshape(1, -1) for k in NARROW])
    got_early, got, got_meta = all_gather_two_level(jnp.concatenate([sent[k].astype(MMD) for k in EARLY], axis=0),
                                                    "gather_early", also=(send[0], meta_tokens))
    full = {k: got_early[:, lo:hi].reshape(-1, dm.d) for k, lo, hi in zip(EARLY, early_rows[:-1], early_rows[1:])}
    late_shards = jnp.concatenate([sent[k].astype(MMD) for k in LATE], axis=0)
    got = got.reshape(N_DEV, -1)
    for k, o in zip(NARROW, offs):
        r, c = sent[k].shape
        full[k] = got[:, o:o + r * c].reshape(N_DEV, r, c).transpose(1, 0, 2).reshape(r, N_DEV * c)
    mr, mc = meta_tokens.shape
    meta = got_meta.transpose(1, 0, 2).reshape(mr, N_DEV * mc)
    small = {k: wts[k].reshape(1, -1) for k in SMALL}

    loss, grad_x, dmeta, gw, gsum_late, gs = local_step(dm, x, loss_target, meta, full, late_shards, late_rows, small)

    def blocks(k, g):
        r, c = sent[k].shape
        return g.reshape(r, N_DEV, c).transpose(1, 0, 2).reshape(N_DEV, r * c)

    gsend, goffs = _pack([blocks(k, gw[k]).astype(MMD) for k in NARROW]
                         + [dmeta.reshape(mr, N_DEV, mc).transpose(1, 0, 2).reshape(N_DEV, mr * mc).astype(MMD)])
    ssend, soffs = _pack([gs[k].reshape(1, -1) for k in SMALL] + [loss])
    gearly = jnp.concatenate([gw[k].reshape(N_DEV, sent[k].shape[0], dm.d) for k in EARLY], axis=1).astype(MMD)
    gsum_early, small_parts, narrow_parts = reduce_scatter_two_level(gearly, "scatter_early", gathers=(ssend[0],),
                                                                     scatters=(gsend,))
    grads = {}
    for names, rows, gsum_rows in ((EARLY, early_rows, gsum_early), (LATE, late_rows, gsum_late)):
        for k, lo, hi in zip(names, rows[:-1], rows[1:]):
            grads[k] = gsum_rows[lo:hi].T if k in TRANSPOSED else gsum_rows[lo:hi]
    gsum = sum_blocks(narrow_parts, "sum_narrow").reshape(-1)
    for k, o in zip(NARROW, goffs):
        r, c = sent[k].shape
        grads[k] = gsum[o:o + r * c].reshape(r, c)
    grads["meta_tokens"] = gsum[goffs[-1]:goffs[-1] + mr * mc].reshape(mr, mc)
    ssum = sum_blocks(small_parts, "sum_small").reshape(-1)
    for k, o in zip(SMALL, soffs):
        grads[k] = ssum[o:o + small[k].shape[1]]
    loss_total = ssum[soffs[-1]]

    delta, new_m, new_v = {}, {}, {}
    for k in MATRICES + ("meta_tokens",):
        shp = wts[k].shape
        to2d = lambda a: a.reshape(shp[-2], shp[-1])
        dlt, nm, nv = adamw(to2d(wts[k]), grads[k], to2d(ms[k]), to2d(vs[k]), f"adamw_{k}")
        delta[k], new_m[k], new_v[k] = dlt.reshape(shp), nm.reshape(shp), nv.reshape(shp)
        grads[k] = grads[k].reshape(shp)
    pw, _ = _pack([wts[k].reshape(1, -1) for k in SMALL])
    pm_, _ = _pack([ms[k].reshape(1, -1) for k in SMALL])
    pv, _ = _pack([vs[k].reshape(1, -1) for k in SMALL])
    pg, poffs = _pack([grads[k].reshape(1, -1) for k in SMALL])
    dlt, nm, nv = adamw(pw[0], pg[0], pm_[0], pv[0], "adamw_small")
    for k, o in zip(SMALL, poffs):
        shp, sz = wts[k].shape, small[k].shape[1]
        cut = lambda a: a.reshape(-1)[o:o + sz].reshape(shp)
        delta[k], new_m[k], new_v[k] = cut(dlt), cut(nm), cut(nv)
        grads[k] = grads[k].reshape(shp)

    return (loss_total, grad_x, *[grads[k] for k in WEIGHTS], *[delta[k] for k in WEIGHTS],
            *[new_m[k] for k in WEIGHTS], *[new_v[k] for k in WEIGHTS])
```

```python
import functools

import numpy as np
import jax
import jax.numpy as jnp
from jax import lax
from jax.experimental import pallas as pl
from jax.experimental.pallas import tpu as pltpu

F32 = jnp.float32
BF16 = jnp.bfloat16
MMD = BF16

NORM_EPS = 1e-6
RWKV_HEAD = 64
GN_EPS = RWKV_HEAD * 1e-5
NOPE_DIM = 128
ROPE_DIM = 64
V_DIM = 128
QK_DIM = NOPE_DIM + ROPE_DIM
ROPE_THETA = 10000.0
ADAM_LR = 0.001
ADAM_B1 = 0.9
ADAM_B2 = 0.999
ADAM_EPS = 1e-08
ADAM_WD = 0.01
ADAM_STEP = 10

LANES = 128
SCAN_FWD_PAIRS = 16
SCAN_FWD_STEPS = 16
SCAN_PAIRS = 8
SCAN_BWD_STEPS = 16
N_DEV = 8
VMEM_LIMIT = 56 * 1024 * 1024
MESH = pl.DeviceIdType.MESH


def _tile(n, target, align):
    best = None
    for d in range(align, min(n, target) + 1, align):
        if n % d == 0:
            best = d
    return best if best is not None else n


def _params(sem=None):
    return pltpu.CompilerParams(dimension_semantics=sem, vmem_limit_bytes=VMEM_LIMIT)


def _mm(a, b, dims=((1,), (0,))):
    return lax.dot_general(a.astype(MMD), b.astype(MMD), (dims, ((), ())), preferred_element_type=F32)


@jax.custom_vjp
def mmdot(a, b):
    return _mm(a, b)


def _mmdot_fwd(a, b):
    return _mm(a, b), (a, b)


def _mmdot_bwd(res, g):
    a, b = res
    return _mm(g, b, ((1,), (1,))).astype(a.dtype), _mm(a, g, ((0,), (0,))).astype(b.dtype)


mmdot.defvjp(_mmdot_fwd, _mmdot_bwd)


def _dot2(x, m):
    hi = x.astype(BF16)
    lo = (x - hi.astype(F32)).astype(BF16)
    return (lax.dot_general(hi, m, (((1,), (0,)), ((), ())), preferred_element_type=F32)
            + lax.dot_general(lo, m, (((1,), (0,)), ((), ())), preferred_element_type=F32))


@jax.custom_vjp
def segsum(x, e, et):
    return _dot2(_dot2(x, e), et)


def _segsum_fwd(x, e, et):
    return segsum(x, e, et), (e, et)


def _segsum_bwd(res, g):
    e, et = res
    return segsum(g, e, et), jnp.zeros_like(e), jnp.zeros_like(et)


segsum.defvjp(_segsum_fwd, _segsum_bwd)


def _sigmoid(x):
    return 1.0 / (1.0 + jnp.exp(-x))


def _softplus(x):
    return jnp.maximum(x, 0.0) + jnp.log(1.0 + jnp.exp(-jnp.abs(x)))


def _rms(x, g):
    return x * lax.rsqrt(jnp.mean(x * x, axis=-1, keepdims=True) + NORM_EPS) * g


_DIMS = {"nn": ((1,), (0,)), "nt": ((1,), (1,)), "tn": ((0,), (0,))}


def matmul(pairs, mode, *, name, out_dtype=F32, res=None, alpha=1.0, tm=1088, tn=1024, tk=2048):
    a0, b0 = pairs[0]
    if mode == "nn":
        (m, k), n = a0.shape, b0.shape[1]
    elif mode == "nt":
        (m, k), n = a0.shape, b0.shape[0]
    else:
        (k, m), n = a0.shape, b0.shape[1]
    tm = _tile(m, 1408, 128) if mode == "tn" else _tile(m, tm, 16)
    tn = _tile(n, 2048 if mode == "tn" else tn, 128)
    tk = _tile(k, min(tk, 1024), 16) if mode == "tn" else _tile(k, tk, 128)
    nk = k // tk
    npair = len(pairs)
    if mode == "tn":
        a_spec = pl.BlockSpec((tk, tm), lambda i, j, kk: (kk, i))
    else:
        a_spec = pl.BlockSpec((tm, tk), lambda i, j, kk: (i, kk))
    if mode == "nt":
        b_spec = pl.BlockSpec((tn, tk), lambda i, j, kk: (j, kk))
    else:
        b_spec = pl.BlockSpec((tk, tn), lambda i, j, kk: (kk, j))
    o_spec = pl.BlockSpec((tm, tn), lambda i, j, kk: (i, j))
    dims = _DIMS[mode]

    def body(*refs):
        ab = refs[:2 * npair]
        res_ref = refs[2 * npair] if res is not None else None
        o_ref, acc_ref = refs[-2], refs[-1]
        kk = pl.program_id(2)

        @pl.when(kk == 0)
        def _():
            acc_ref[...] = jnp.zeros_like(acc_ref)

        part = _mm(ab[0][...], ab[1][...], dims)
        for p in range(1, npair):
            part = part + _mm(ab[2 * p][...], ab[2 * p + 1][...], dims)
        acc_ref[...] += part

        @pl.when(kk == nk - 1)
        def _():
            out = acc_ref[...] * alpha if alpha != 1.0 else acc_ref[...]
            if res_ref is not None:
                out = res_ref[...].astype(F32) + out
            o_ref[...] = out.astype(o_ref.dtype)

    args, specs = [], []
    for a, b in pairs:
        args += [a, b]
        specs += [a_spec, b_spec]
    if res is not None:
        args.append(res)
        specs.append(o_spec)
    return pl.pallas_call(
        body, grid=(m // tm, n // tn, nk), in_specs=specs, out_specs=o_spec,
        out_shape=jax.ShapeDtypeStruct((m, n), out_dtype), scratch_shapes=[pltpu.VMEM((tm, tn), F32)],
        compiler_params=_params(("parallel", "parallel", "arbitrary")), name=name)(*args)


def tilek(fn, ins, outs, *, n_rows, tr, name):
    tr = _tile(n_rows, tr, 16)
    n_in = len(ins)
    in_specs = []
    for arr, kind in ins:
        if kind == "r":
            in_specs.append(pl.BlockSpec((tr, arr.shape[1]), lambda i: (i, 0)))
        else:
            in_specs.append(pl.BlockSpec(arr.shape, lambda i, nd=arr.ndim: (0,) * nd))
    out_specs, out_shape = [], []
    has_acc = False
    for o in outs:
        if o[0] == "r":
            out_specs.append(pl.BlockSpec((tr, o[1]), lambda i: (i, 0)))
            out_shape.append(jax.ShapeDtypeStruct((n_rows, o[1]), o[2]))
        else:
            has_acc = True
            out_specs.append(pl.BlockSpec(o[1], lambda i, nd=len(o[1]): (0,) * nd))
            out_shape.append(jax.ShapeDtypeStruct(o[1], F32))

    def body(*refs):
        i = pl.program_id(0)
        vals = fn(*[r[...] for r in refs[:n_in]])
        for o, r, v in zip(outs, refs[n_in:], vals):
            if o[0] == "r":
                r[...] = v.astype(r.dtype)
            else:
                @pl.when(i == 0)
                def _(r=r):
                    r[...] = jnp.zeros_like(r)

                r[...] += v

    return pl.pallas_call(
        body, grid=(n_rows // tr,), in_specs=in_specs, out_specs=out_specs, out_shape=out_shape,
        compiler_params=_params(("arbitrary",) if has_acc else ("parallel",)), name=name)(*[a for a, _ in ins])


def rms_fwd(x, g, name):
    n, d = x.shape
    return tilek(lambda xv, gv: (_rms(xv, gv),), [(x, "r"), (g, "f")], [("r", d, MMD)], n_rows=n, tr=256, name=name)[0]


def rms_bwd(x, g, dy, dres, name):
    n, d = x.shape

    def fn(xv, gv, dyv, drv):
        _, vjp = jax.vjp(_rms, xv, gv)
        dx, dg = vjp(dyv.astype(F32))
        return drv + dx, dg

    return tilek(fn, [(x, "r"), (g, "f"), (dy, "r"), (dres, "r")], [("r", d, F32), ("acc", (1, d))],
                 n_rows=n, tr=128, name=name)


def loss_head(h, tgt, mask, g, name):
    n, d = h.shape

    def fn(hv, tv, mv, gv):
        def lossf(hh, gg):
            e = (_rms(hh, gg) - tv) * mv
            s = jnp.sum(jnp.sum(e * e, axis=1, keepdims=True), axis=0, keepdims=True)
            return s * (0.5 / d)

        l, vjp = jax.vjp(lossf, hv, gv)
        dh, dg = vjp(jnp.ones((1, 1), F32))
        return dh, dg, jnp.broadcast_to(l, (1, LANES))

    return tilek(fn, [(h, "r"), (tgt, "r"), (mask, "r"), (g, "f")],
                 [("r", d, F32), ("acc", (1, d)), ("acc", (1, LANES))], n_rows=n, tr=128, name=name)


def ffn_up(hn, wg, wu, name):
    n, d = hn.shape
    f = wg.shape[0]
    tm, tn = _tile(n, 544, 16), _tile(f, 1408, 128)

    def body(a_ref, g_ref, u_ref, og_ref, ou_ref, oa_ref):
        a = a_ref[...]
        g = _mm(a, g_ref[...], ((1,), (1,)))
        u = _mm(a, u_ref[...], ((1,), (1,)))
        og_ref[...] = g.astype(og_ref.dtype)
        ou_ref[...] = u.astype(ou_ref.dtype)
        oa_ref[...] = (g * _sigmoid(g) * u).astype(oa_ref.dtype)

    o_spec = pl.BlockSpec((tm, tn), lambda i, j: (i, j))
    w_spec = pl.BlockSpec((tn, d), lambda i, j: (j, 0))
    return pl.pallas_call(
        body, grid=(n // tm, f // tn), in_specs=[pl.BlockSpec((tm, d), lambda i, j: (i, 0)), w_spec, w_spec],
        out_specs=[o_spec, o_spec, o_spec],
        out_shape=[jax.ShapeDtypeStruct((n, f), MMD)] * 3,
        compiler_params=_params(("parallel", "parallel")), name=name)(hn, wg, wu)


def ffn_down_bwd(dh, wd, gate, up, name):
    n, d = dh.shape
    f = wd.shape[0]
    tm, tn = _tile(n, 544, 16), _tile(f, 1408, 128)

    def body(dh_ref, w_ref, g_ref, u_ref, dg_ref, du_ref):
        da = 0.5 * _mm(dh_ref[...], w_ref[...], ((1,), (1,)))
        g, u = g_ref[...].astype(F32), u_ref[...].astype(F32)
        s = _sigmoid(g)
        dg_ref[...] = (da * u * (s * (1.0 + g * (1.0 - s)))).astype(dg_ref.dtype)
        du_ref[...] = (da * (g * s)).astype(du_ref.dtype)

    o_spec = pl.BlockSpec((tm, tn), lambda i, j: (i, j))
    return pl.pallas_call(
        body, grid=(n // tm, f // tn),
        in_specs=[pl.BlockSpec((tm, d), lambda i, j: (i, 0)), pl.BlockSpec((tn, d), lambda i, j: (j, 0)), o_spec, o_spec],
        out_specs=[o_spec, o_spec],
        out_shape=[jax.ShapeDtypeStruct((n, f), MMD), jax.ShapeDtypeStruct((n, f), MMD)],
        compiler_params=_params(("parallel", "parallel")), name=name)(dh, wd, gate, up)


def ffn_forward(h, g, wg, wu, wd, tag):
    hn = rms_fwd(h, g, f"{tag}_rms")
    gate, up, act = ffn_up(hn, wg, wu, f"{tag}_up")
    out = matmul([(act, wd)], "nn", res=h, alpha=0.5, name=f"{tag}_down")
    return out, (hn, gate, up, act)


def ffn_backward(dout, h, g, wg, wu, wd, saved, tag):
    hn, gate, up, act = saved
    dgate, dup = ffn_down_bwd(dout, wd, gate, up, f"{tag}_dact")
    dwd = matmul([(act, dout)], "tn", alpha=0.5, out_dtype=MMD, name=f"{tag}_dwd")
    dwg = matmul([(dgate, hn)], "tn", out_dtype=MMD, name=f"{tag}_dwg")
    dwu = matmul([(dup, hn)], "tn", out_dtype=MMD, name=f"{tag}_dwu")
    dhn = matmul([(dgate, wg), (dup, wu)], "nn", name=f"{tag}_dhn")
    dh, dg = rms_bwd(h, g, dhn, dout, f"{tag}_drms")
    return dh, dg, dwg, dwu, dwd


def lerp_fwd(p, mu, bl, t, name):
    n, w = p.shape
    cb = _tile(w, 256, 128)

    def body(p_ref, mu_ref, o_ref):
        x = p_ref[...]
        row = lax.broadcasted_iota(jnp.int32, x.shape, 0)
        prev = jnp.where(row == 0, 0.0, pltpu.roll(x, 1, 0))
        o_ref[...] = x + mu_ref[...] * (prev - x)

    spec = pl.BlockSpec((t, cb), lambda b, j: (b, j))
    return pl.pallas_call(
        body, grid=(bl, w // cb), in_specs=[spec, pl.BlockSpec((1, cb), lambda b, j: (0, j))], out_specs=spec,
        out_shape=jax.ShapeDtypeStruct((n, w), F32), compiler_params=_params(("parallel", "parallel")), name=name)(p, mu)


def lerp_bwd(p, mu, douts, bl, t, name):
    n, w = p.shape
    cb = _tile(w, 256, 128)
    nd = len(douts)

    def body(*refs):
        p_ref, mu_ref = refs[0], refs[1]
        dp_ref, dmu_ref = refs[2 + nd], refs[3 + nd]
        b = pl.program_id(1)
        x, m = p_ref[...], mu_ref[...]
        d = refs[2][...]
        for r in refs[3:2 + nd]:
            d = d + r[...]
        row = lax.broadcasted_iota(jnp.int32, x.shape, 0)
        prev = jnp.where(row == 0, 0.0, pltpu.roll(x, 1, 0))
        z = d * m
        nxt = jnp.where(row == t - 1, 0.0, pltpu.roll(z, t - 1, 0))
        dp_ref[...] = d - z + nxt

        @pl.when(b == 0)
        def _():
            dmu_ref[...] = jnp.zeros_like(dmu_ref)

        dmu_ref[...] += jnp.sum(d * (prev - x), axis=0, keepdims=True)

    spec = pl.BlockSpec((t, cb), lambda j, b: (b, j))
    cspec = pl.BlockSpec((1, cb), lambda j, b: (0, j))
    return pl.pallas_call(
        body, grid=(w // cb, bl), in_specs=[spec, cspec] + [spec] * nd, out_specs=[spec, cspec],
        out_shape=[jax.ShapeDtypeStruct((n, w), F32), jax.ShapeDtypeStruct((1, w), F32)],
        compiler_params=_params(("parallel", "arbitrary")), name=name)(p, mu, *douts)


def _prep(k, xw, xa, xg, w0, a0, k_k, k_a, w_up, a_up, g_up, e, et):
    w_pre = -_softplus(-(w0 + mmdot(jnp.tanh(xw), w_up))) - 0.5
    decay = jnp.exp(-jnp.exp(w_pre))
    a = _sigmoid(a0 + mmdot(xa, a_up))
    g = mmdot(_sigmoid(xg), g_up)
    kk = k * k_k
    kk = kk * lax.rsqrt(jnp.maximum(segsum(kk * kk, e, et), 1e-24))
    kmod = k * (1.0 + (a - 1.0) * k_a)
    return decay, kmod, -kk, kk * a, g


def _lora_parts(xl):
    return xl[:, :LANES], xl[:, LANES:2 * LANES], xl[:, 2 * LANES:]


def rwkv_prep_fwd(pk, pl_, prm, e, et, name):
    n, d = pk.shape
    small = [prm[k] for k in ("w0", "a0", "k_k", "k_a", "w_up", "a_up", "g_up")]
    ins = [(pk, "r"), (pl_, "r")] + [(s, "f") for s in small] + [(e, "f"), (et, "f")]
    return tilek(lambda k, xl, *rest: _prep(k, *_lora_parts(xl), *rest), ins, [("r", d, F32)] * 5, n_rows=n, tr=128, name=name)


def rwkv_prep_bwd(pk, pl_, prm, e, et, cts, name):
    n, d = pk.shape
    small = [prm[k] for k in ("w0", "a0", "k_k", "k_a", "w_up", "a_up", "g_up")]

    def fn(k, xl, w0, a0, k_k, k_a, w_up, a_up, g_up, ev, etv, dw, dkm1, dkm2, dkn, db, dg):
        _, vjp = jax.vjp(lambda *a: _prep(*a, ev, etv), k, *_lora_parts(xl), w0, a0, k_k, k_a, w_up, a_up, g_up)
        dk, dxw, dxa, dxg, *dsmall = vjp((dw, dkm1 + dkm2, dkn, db, dg))
        return (dk, jnp.concatenate([dxw, dxa, dxg], axis=1), *dsmall)

    ins = [(pk, "r"), (pl_, "r")] + [(s, "f") for s in small] + [(e, "f"), (et, "f")] + [(c, "r") for c in cts]
    outs = [("r", d, F32), ("r", pl_.shape[1], F32)] + [("acc", s.shape) for s in small]
    return tilek(fn, ins, outs, n_rows=n, tr=64, name=name)


def _post(y, r, km, v, g, pga, pgb, yb, gn_w, gn_b, r_k, e, et):
    inv = 1.0 / RWKV_HEAD
    yc = y - segsum(y, e, et) * inv
    var = segsum(yc * yc, e, et) * inv
    yn = yc * lax.rsqrt(var + GN_EPS) * gn_w + gn_b
    bonus = segsum(r * km * r_k, e, et) * v
    ya = (yn + bonus) * g
    return _sigmoid(pga) * ya + _sigmoid(pgb) * yb


def rwkv_post_fwd(acts, prm, e, et, name):
    n, d = acts[0].shape
    small = [prm[k] for k in ("gn_w", "gn_b", "r_k")]
    ins = [(a, "r") for a in acts] + [(s, "f") for s in small] + [(e, "f"), (et, "f")]
    return tilek(lambda *a: (_post(*a),), ins, [("r", d, MMD)], n_rows=n, tr=128, name=name)[0]


def rwkv_post_bwd(acts, prm, e, et, dm, name):
    n, d = acts[0].shape
    small = [prm[k] for k in ("gn_w", "gn_b", "r_k")]
    na = len(acts)

    def fn(*a):
        prim, ev, etv, dmv = a[:na + 3], a[na + 3], a[na + 4], a[na + 5]
        _, vjp = jax.vjp(lambda *z: _post(*z, ev, etv), *prim)
        return vjp(dmv.astype(F32))

    ins = [(x, "r") for x in acts] + [(s, "f") for s in small] + [(e, "f"), (et, "f"), (dm, "r")]
    outs = [("r", d, F32)] * na + [("acc", s.shape) for s in small]
    return tilek(fn, ins, outs, n_rows=n, tr=64, name=name)


def _head_sums(x, first_head):
    a = jnp.sum(jnp.where(first_head, x, 0.0), axis=1, keepdims=True)
    b = jnp.sum(jnp.where(first_head, 0.0, x), axis=1, keepdims=True)
    return jnp.where(first_head, a, b)


def _round1(x):
    return (x.astype(BF16), None) if MMD == BF16 else _split2(x)


def _split2(x):
    hi = x.astype(BF16)
    return hi, (x - hi.astype(F32)).astype(BF16)


def _spread(row, eye2):
    hi, lo = _split2(row)
    return eye2 * hi, eye2 * lo


def _ones_dot(tiles, ones_blk):
    dims = (((1,), (0,)), ((), ()))
    res = lax.dot_general(jnp.concatenate([t[0] for t in tiles], axis=0), ones_blk, dims, preferred_element_type=F32)
    out = [res[i * RWKV_HEAD:(i + 1) * RWKV_HEAD] for i in range(len(tiles))]
    two_term = [i for i, t in enumerate(tiles) if t[1] is not None]
    if two_term:
        low = lax.dot_general(jnp.concatenate([tiles[i][1] for i in two_term], axis=0), ones_blk, dims,
                              preferred_element_type=F32)
        for n, i in enumerate(two_term):
            out[i] = out[i] + low[n * RWKV_HEAD:(n + 1) * RWKV_HEAD]
    return out


def _scan_consts():
    lane = lax.broadcasted_iota(jnp.int32, (1, LANES), 1)
    rows = lax.broadcasted_iota(jnp.int32, (RWKV_HEAD, LANES), 0)
    cols = lax.broadcasted_iota(jnp.int32, (RWKV_HEAD, LANES), 1)
    eye2 = ((cols & (RWKV_HEAD - 1)) == rows).astype(BF16)
    r2 = lax.broadcasted_iota(jnp.int32, (LANES, LANES), 0)
    c2 = lax.broadcasted_iota(jnp.int32, (LANES, LANES), 1)
    ones_blk = ((r2 // RWKV_HEAD) == (c2 // RWKV_HEAD)).astype(BF16)
    return lane, lane < RWKV_HEAD, eye2, ones_blk


def _direct_copies(src_ref, dst_ref, send_sems, recv_sems, local_sem, scatter):
    _, me = _peer(0)
    out = [pltpu.make_async_copy(src_ref.at[me] if scatter else src_ref, dst_ref.at[me], local_sem)]
    for k in range(1, N_DEV):
        dev, idx = _peer(k)
        out.append(pltpu.make_async_remote_copy(src_ref=src_ref.at[idx] if scatter else src_ref, dst_ref=dst_ref.at[me],
                                                send_sem=send_sems.at[k - 1], recv_sem=recv_sems.at[k - 1],
                                                device_id=dev, device_id_type=MESH))
    return out


def _riding_exchange(src_ref, dst_ref, send_sems, recv_sems, local_sem, scatter, grid):
    copies = lambda: _direct_copies(src_ref, dst_ref, send_sems, recv_sems, local_sem, scatter)
    ids = [pl.program_id(a) for a in range(len(grid))]
    first = functools.reduce(jnp.logical_and, [i == 0 for i in ids])
    last = functools.reduce(jnp.logical_and, [i == n - 1 for i, n in zip(ids, grid)])

    def start():
        @pl.when(first)
        def _():
            for cp in copies():
                cp.start()

    def finish():
        @pl.when(last)
        def _():
            for cp in copies():
                cp.wait()

    return start, finish


_RIDE_SCRATCH = [pltpu.SemaphoreType.DMA((N_DEV - 1,)), pltpu.SemaphoreType.DMA((N_DEV - 1,)), pltpu.SemaphoreType.DMA]


def scan_forward(r, w, k, kn, b, v, ride, bl, t, t_real, d, name, pg, hch):
    npair, nst = d // LANES, t // hch
    grid = (bl, npair // pg, nst)

    def body(r_ref, w_ref, k_ref, kn_ref, b_ref, v_ref, ride_ref, y_ref, hist_ref, land_ref, s_ref, vb_ref, *sems):
        start, finish = _riding_exchange(ride_ref, land_ref, *sems, False, grid)
        start()
        _, first_head, eye2, ones_blk = _scan_consts()
        eye2f = eye2.astype(F32)
        diag = lambda tile: jnp.sum(tile * eye2f, axis=0, keepdims=True)

        @pl.when(pl.program_id(2) == 0)
        def _():
            s_ref[...] = jnp.zeros_like(s_ref)

        pair_cols = [slice(p * LANES, (p + 1) * LANES) for p in range(pg)]
        for p, tile in enumerate(_ones_dot([_spread(v_ref[0, :, cols], eye2) for cols in pair_cols], ones_blk)):
            vb_ref[p] = tile

        def step(ts, carry):
            prev, nxt = jnp.maximum(ts - 1, 0), jnp.minimum(ts + 1, hch - 1)
            states, tiles = [], []
            for p in range(pg):
                cols = slice(p * LANES, (p + 1) * LANES)
                s = s_ref[p]
                hist_ref[0, p, pl.ds(ts, 1)] = s[None]
                states.append(s)
                tiles.append(_round1(s * r_ref[prev, :, cols]))
                tiles.append(_spread(v_ref[nxt, :, cols], eye2))
            res = _ones_dot(tiles, ones_blk)
            for p in range(pg):
                cols = slice(p * LANES, (p + 1) * LANES)
                s = states[p]
                sa = _head_sums(s * kn_ref[ts, :, cols], first_head)
                s_ref[p] = s * w_ref[ts, :, cols] + sa * b_ref[ts, :, cols] + vb_ref[p] * k_ref[ts, :, cols]
            for p in range(pg):
                cols = slice(p * LANES, (p + 1) * LANES)
                y_ref[prev, :, cols] = diag(res[2 * p])
                vb_ref[p] = res[2 * p + 1]
            return carry

        real = pl.program_id(2) * hch < t_real
        lax.fori_loop(0, jnp.where(real, hch, 0), step, 0)
        last = _ones_dot([_round1(s_ref[p] * r_ref[hch - 1, :, cols]) for p, cols in enumerate(pair_cols)], ones_blk)
        for p, cols in enumerate(pair_cols):
            y_ref[hch - 1, :, cols] = diag(last[p])

        @pl.when(jnp.logical_not(real))
        def _():
            y_ref[...] = jnp.zeros_like(y_ref)
            hist_ref[...] = jnp.zeros_like(hist_ref)

        finish()

    row_spec = pl.BlockSpec((hch, 1, pg * LANES), lambda bb, g, c: (bb * nst + c, 0, g))
    hist_spec = pl.BlockSpec((1, pg, hch, RWKV_HEAD, LANES), lambda bb, g, c: (bb, g, c, 0, 0))
    hbm = pl.BlockSpec(memory_space=pl.ANY)
    rows3 = [a.reshape(bl * t, 1, d) for a in (r, w, k, kn, b, v)]
    y, hist, landed = pl.pallas_call(
        body, grid=grid, in_specs=[row_spec] * 6 + [hbm], out_specs=[row_spec, hist_spec, hbm],
        out_shape=[jax.ShapeDtypeStruct((bl * t, 1, d), F32), jax.ShapeDtypeStruct((bl, npair, t, RWKV_HEAD, LANES), F32),
                   jax.ShapeDtypeStruct((N_DEV,) + ride.shape, ride.dtype)],
        scratch_shapes=[pltpu.VMEM((pg, RWKV_HEAD, LANES), F32)] * 2 + _RIDE_SCRATCH,
        compiler_params=_params(("arbitrary", "arbitrary", "arbitrary")), name=name)(*rows3, ride)
    return y.reshape(bl * t, d), hist, landed


def scan_backward(r, w, k, kn, b, v, dy, hist, ride, bl, t, t_real, d, name, pg, hch):
    npair, nst = d // LANES, t // hch
    grid = (bl, npair // pg, nst)

    def body(r_ref, w_ref, k_ref, kn_ref, b_ref, v_ref, dy_ref, hist_ref, ride_ref,
             dr_ref, dw_ref, dk_ref, dkn_ref, db_ref, dv_ref, land_ref, ds_ref, cur_ref, *sems):
        start, finish = _riding_exchange(ride_ref, land_ref, *sems, True, grid)
        start()
        _, first_head, eye2, ones_blk = _scan_consts()
        eye2f = eye2.astype(F32)
        colsum = lambda x: jnp.sum(x, axis=0, keepdims=True)

        @pl.when(pl.program_id(2) == 0)
        def _():
            ds_ref[...] = jnp.zeros_like(ds_ref)

        tiles = []
        for p in range(pg):
            cols = slice(p * LANES, (p + 1) * LANES)
            tiles += [_spread(v_ref[hch - 1, :, cols], eye2), _spread(dy_ref[hch - 1, :, cols], eye2),
                      _split2(hist_ref[0, p, hch - 1] * kn_ref[hch - 1, :, cols])]
        first = _ones_dot(tiles, ones_blk)
        for p in range(pg):
            cols = slice(p * LANES, (p + 1) * LANES)
            row = lambda ref: ref[hch - 1, :, cols]
            s_prev = hist_ref[0, p, hch - 1]
            vb, dyb, sa = first[3 * p], first[3 * p + 1], first[3 * p + 2]
            cur_ref[0, p], cur_ref[1, p] = vb, sa
            dr_ref[hch - 1, :, cols] = colsum((s_prev * row(w_ref) + sa * row(b_ref) + vb * row(k_ref)) * dyb)
            ds_ref[p] += dyb * row(r_ref)

        def step(it, carry):
            ts = hch - 1 - it
            prev = jnp.maximum(ts - 1, 0)
            has_prev = ts > 0
            grads, tiles = [], []
            for p in range(pg):
                cols = slice(p * LANES, (p + 1) * LANES)
                ds = ds_ref[p]
                grads.append(ds)
                tiles.append(_spread(v_ref[prev, :, cols], eye2))
                tiles.append(_spread(dy_ref[prev, :, cols], eye2))
                tiles.append(_split2(hist_ref[0, p, pl.ds(prev, 1)][0] * kn_ref[prev, :, cols]))
                tiles.append(_round1(ds * k_ref[ts, :, cols]))
            res = _ones_dot(tiles, ones_blk)
            for p in range(pg):
                cols = slice(p * LANES, (p + 1) * LANES)
                row = lambda ref: ref[ts, :, cols]
                ds = grads[p]
                w_, kn_, b_ = row(w_ref), row(kn_ref), row(b_ref)
                dsa = _head_sums(ds * b_, first_head)
                s_prev = hist_ref[0, p, pl.ds(ts, 1)][0]
                vb, sa, dyb_prev = cur_ref[0, p], cur_ref[1, p], res[4 * p + 1]
                dk_ref[ts, :, cols] = colsum(ds * vb)
                db_ref[ts, :, cols] = colsum(ds * sa)
                dw_ref[ts, :, cols] = colsum(ds * s_prev)
                dkn_ref[ts, :, cols] = colsum(s_prev * dsa)
                dv_ref[ts, :, cols] = colsum(res[4 * p + 3] * eye2f)
                dr_ref[prev, :, cols] = jnp.where(has_prev, colsum(s_prev * dyb_prev), dr_ref[prev, :, cols])
                ds_ref[p] = ds * w_ + dsa * kn_ + jnp.where(has_prev, dyb_prev, 0.0) * r_ref[prev, :, cols]
            for p in range(pg):
                cur_ref[0, p] = res[4 * p]
                cur_ref[1, p] = res[4 * p + 2]
            return carry

        real = (nst - 1 - pl.program_id(2)) * hch < t_real
        lax.fori_loop(0, jnp.where(real, hch, 0), step, 0)

        @pl.when(jnp.logical_not(real))
        def _():
            for ref in (dr_ref, dw_ref, dk_ref, dkn_ref, db_ref, dv_ref):
                ref[...] = jnp.zeros_like(ref)

        finish()

    row_spec = pl.BlockSpec((hch, 1, pg * LANES), lambda bb, g, c: (bb * nst + nst - 1 - c, 0, g))
    hist_spec = pl.BlockSpec((1, pg, hch, RWKV_HEAD, LANES), lambda bb, g, c: (bb, g, nst - 1 - c, 0, 0))
    hbm = pl.BlockSpec(memory_space=pl.ANY)
    row_shape = jax.ShapeDtypeStruct((bl * t, 1, d), F32)
    rows3 = [a.reshape(bl * t, 1, d) for a in (r, w, k, kn, b, v, dy)]
    outs = pl.pallas_call(
        body, grid=grid, in_specs=[row_spec] * 7 + [hist_spec, hbm], out_specs=[row_spec] * 6 + [hbm],
        out_shape=[row_shape] * 6 + [jax.ShapeDtypeStruct(ride.shape, ride.dtype)],
        scratch_shapes=[pltpu.VMEM((pg, RWKV_HEAD, LANES), F32), pltpu.VMEM((2, pg, RWKV_HEAD, LANES), F32)] + _RIDE_SCRATCH,
        compiler_params=_params(("arbitrary", "arbitrary", "arbitrary")), name=name)(*rows3, hist, ride)
    return [o.reshape(bl * t, d) for o in outs[:6]] + [outs[6]]


def _mla_norms(pm, gq, gkv):
    ql = gq.shape[1]
    kvl = gkv.shape[1]
    return _rms(pm[:, :ql], gq), _rms(pm[:, ql:ql + kvl], gkv)


def mla_prep_fwd(pm, gq, gkv, name):
    n = pm.shape[0]
    return tilek(_mla_norms, [(pm, "r"), (gq, "f"), (gkv, "f")],
                 [("r", gq.shape[1], MMD), ("r", gkv.shape[1], MMD)], n_rows=n, tr=256, name=name)


def mla_prep_bwd(pm, gq, gkv, dcq, dckv, dkpe, name):
    n, wm = pm.shape
    ql, kvl = gq.shape[1], gkv.shape[1]

    def fn(pmv, gqv, gkvv, d1, d2, d3):
        _, vjp1 = jax.vjp(_rms, pmv[:, :ql], gqv)
        _, vjp2 = jax.vjp(_rms, pmv[:, ql:ql + kvl], gkvv)
        dcq_in, dgq = vjp1(d1)
        dckv_in, dgkv = vjp2(d2)
        return jnp.concatenate([dcq_in, dckv_in, d3], axis=1), dgq, dgkv

    return tilek(fn, [(pm, "r"), (gq, "f"), (gkv, "f"), (dcq, "r"), (dckv, "r"), (dkpe, "r")],
                 [("r", wm, F32), ("acc", gq.shape), ("acc", gkv.shape)], n_rows=n, tr=128, name=name)


def _rope(x, c, s, first):
    sw = jnp.where(first, pltpu.roll(x, LANES - ROPE_DIM // 2, 1), pltpu.roll(x, ROPE_DIM // 2, 1))
    return x * c + sw * s


def _unrope(d, c, s, first):
    z = d * s
    sw = jnp.where(first, pltpu.roll(z, LANES - ROPE_DIM // 2, 1), pltpu.roll(z, ROPE_DIM // 2, 1))
    return d * c + sw


def _causal_segments(n_tiles, parts=17):
    bounds = sorted({round(n_tiles * s / parts) for s in range(parts + 1)})
    return list(zip(bounds[:-1], bounds[1:]))


def attn_fwd(q, kv, pm, ct, st, bl, t, hm, name):
    n = q.shape[0]
    tq = LANES
    scale = QK_DIM ** -0.5
    kpe_blk = pm.shape[1] // LANES - 1

    def body(qn_ref, qpe_ref, kn_ref, v_ref, kpe_ref, ct_ref, st_ref, o_ref, lse_ref, kp_s, kn_s, v_s):
        h = pl.program_id(1)
        lane = lax.broadcasted_iota(jnp.int32, (1, LANES), 1)
        first = (lane & (ROPE_DIM - 1)) < ROPE_DIM // 2
        kp = _rope(kpe_ref[...], ct_ref[...], st_ref[...], first)
        kp_s[...] = jnp.where(h % 2 == 0, kp, pltpu.roll(kp, ROPE_DIM, 1)).astype(MMD)
        kn_s[...] = kn_ref[...].astype(MMD)
        v_s[...] = v_ref[...].astype(MMD)
        def segment(lo, hi):
            ext = hi * tq
            kpos = lax.broadcasted_iota(jnp.int32, (1, ext), 1)

            def qtile(i, carry):
                rows = pl.ds(pl.multiple_of(i * tq, tq), tq)
                q2 = _rope(qpe_ref[rows, :], ct_ref[rows, :], st_ref[rows, :], first)
                s = (_mm(qn_ref[rows, :], kn_s[:ext, :], ((1,), (1,))) + _mm(q2, kp_s[:ext, :], ((1,), (1,)))) * scale
                qpos = i * tq + lax.broadcasted_iota(jnp.int32, (tq, 1), 0)
                s = jnp.where(kpos <= qpos, s, -1e30)
                m = jnp.max(s, axis=1, keepdims=True)
                p = jnp.exp(s - m)
                l = jnp.sum(p, axis=1, keepdims=True)
                o_ref[rows, :] = _mm(p, v_s[:ext, :]) / l
                lse_ref[0, 0, rows, :] = m + jnp.log(l)
                return carry

            lax.fori_loop(lo, hi, qtile, 0)

        for lo, hi in _causal_segments(t // tq):
            segment(lo, hi)

    blk = lambda f: pl.BlockSpec((t, LANES), f)
    return pl.pallas_call(
        body, grid=(bl, hm),
        in_specs=[blk(lambda b, h: (b, h)), blk(lambda b, h: (b, hm + h // 2)), blk(lambda b, h: (b, h)),
                  blk(lambda b, h: (b, hm + h)), blk(lambda b, h: (b, kpe_blk)), blk(lambda b, h: (0, 0)), blk(lambda b, h: (0, 0))],
        out_specs=[blk(lambda b, h: (b, h)), pl.BlockSpec((1, 1, t, 1), lambda b, h: (b, h, 0, 0))],
        out_shape=[jax.ShapeDtypeStruct((n, hm * LANES), F32), jax.ShapeDtypeStruct((bl, hm, t, 1), F32)],
        scratch_shapes=[pltpu.VMEM((t, LANES), MMD)] * 3,
        compiler_params=_params(("parallel", "arbitrary")), name=name)(q, q, kv, kv, pm, ct, st)


def attn_bwd(q, kv, pm, o, do, lse, ct, st, bl, t, hm, name):
    n = q.shape[0]
    tq = LANES
    scale = QK_DIM ** -0.5
    kpe_blk = pm.shape[1] // LANES - 1

    def body(qn_ref, qpe_ref, kn_ref, v_ref, kpe_ref, o_ref, do_ref, lse_ref, ct_ref, st_ref,
             dqn_ref, dqpe_ref, dkn_ref, dv_ref, dkpe_ref, kp_s, kn_s, v_s, dkn_s, dkp_s, dv_s):
        h = pl.program_id(1)
        lane = lax.broadcasted_iota(jnp.int32, (1, LANES), 1)
        first = (lane & (ROPE_DIM - 1)) < ROPE_DIM // 2
        mine = (lane // ROPE_DIM) == (h % 2)
        kp = _rope(kpe_ref[...], ct_ref[...], st_ref[...], first)
        kp_s[...] = jnp.where(h % 2 == 0, kp, pltpu.roll(kp, ROPE_DIM, 1)).astype(MMD)
        kn_s[...] = kn_ref[...].astype(MMD)
        v_s[...] = v_ref[...].astype(MMD)
        dkn_s[...] = jnp.zeros_like(dkn_s)
        dkp_s[...] = jnp.zeros_like(dkp_s)
        dv_s[...] = jnp.zeros_like(dv_s)
        @pl.when(h % 2 == 0)
        def _():
            dqpe_ref[...] = jnp.zeros_like(dqpe_ref)

        @pl.when(h == 0)
        def _():
            dkpe_ref[...] = jnp.zeros_like(dkpe_ref)

        def segment(lo, hi):
            ext = hi * tq
            kpos = lax.broadcasted_iota(jnp.int32, (1, ext), 1)

            def qtile(i, carry):
                rows = pl.ds(pl.multiple_of(i * tq, tq), tq)
                c_i, s_i = ct_ref[rows, :], st_ref[rows, :]
                q1 = qn_ref[rows, :].astype(MMD)
                q2 = _rope(qpe_ref[rows, :], c_i, s_i, first).astype(MMD)
                s = (_mm(q1, kn_s[:ext, :], ((1,), (1,))) + _mm(q2, kp_s[:ext, :], ((1,), (1,)))) * scale
                qpos = i * tq + lax.broadcasted_iota(jnp.int32, (tq, 1), 0)
                p = jnp.where(kpos <= qpos, jnp.exp(s - lse_ref[0, 0, rows, :]), 0.0)
                do_i = do_ref[rows, :]
                delta = jnp.sum(do_i * o_ref[rows, :], axis=1, keepdims=True)
                dp = _mm(do_i, v_s[:ext, :], ((1,), (1,)))
                ds = (p * (dp - delta) * scale).astype(MMD)
                dqn_ref[rows, :] = _mm(ds, kn_s[:ext, :])
                dq2 = jnp.where(mine, _mm(ds, kp_s[:ext, :]), 0.0)
                dqpe_ref[rows, :] += _unrope(dq2, c_i, s_i, first)
                dkn_s[:ext, :] += _mm(ds, q1, ((0,), (0,)))
                dkp_s[:ext, :] += _mm(ds, q2, ((0,), (0,)))
                dv_s[:ext, :] += _mm(p, do_i, ((0,), (0,)))
                return carry

            lax.fori_loop(lo, hi, qtile, 0)

        for lo, hi in _causal_segments(t // tq):
            segment(lo, hi)
        dkn_ref[...] = dkn_s[...]
        dv_ref[...] = dv_s[...]
        dkp = jnp.where(mine, dkp_s[...], 0.0)
        dkp = jnp.where(h % 2 == 0, dkp, pltpu.roll(dkp, ROPE_DIM, 1))
        dkpe_ref[...] += _unrope(dkp, ct_ref[...], st_ref[...], first)

    blk = lambda f: pl.BlockSpec((t, LANES), f)
    hd = lambda b, h: (b, h)
    shp = lambda wd: jax.ShapeDtypeStruct((n, wd), F32)
    return pl.pallas_call(
        body, grid=(bl, hm),
        in_specs=[blk(hd), blk(lambda b, h: (b, hm + h // 2)), blk(hd), blk(lambda b, h: (b, hm + h)),
                  blk(lambda b, h: (b, kpe_blk)), blk(hd), blk(hd), pl.BlockSpec((1, 1, t, 1), lambda b, h: (b, h, 0, 0)),
                  blk(lambda b, h: (0, 0)), blk(lambda b, h: (0, 0))],
        out_specs=[blk(hd), blk(lambda b, h: (b, h // 2)), blk(hd), blk(hd), blk(lambda b, h: (b, 0))],
        out_shape=[shp(hm * LANES), shp(hm * ROPE_DIM), shp(hm * LANES), shp(hm * LANES), shp(LANES)],
        scratch_shapes=[pltpu.VMEM((t, LANES), MMD)] * 3 + [pltpu.VMEM((t, LANES), F32)] * 3,
        compiler_params=_params(("parallel", "arbitrary")), name=name)(q, q, kv, kv, pm, o, do, lse, ct, st)


def _peer(k):
    mx, my, mc = lax.axis_index("x"), lax.axis_index("y"), lax.axis_index("c")
    px = 1 - mx if k & 4 else mx
    py = 1 - my if k & 2 else my
    pc = 1 - mc if k & 1 else mc
    return (px, py, pc), 4 * px + 2 * py + pc


def _chips():
    mx, my, mc = lax.axis_index("x"), lax.axis_index("y"), lax.axis_index("c")
    return (mx, my, mc), (mx, my, 1 - mc), [(1 - mx, my), (mx, 1 - my), (1 - mx, 1 - my)]


def _riders(refs, gathers, scatters):
    n = gathers + scatters
    ins, outs, sems = refs[:n], refs[n:2 * n], refs[2 * n:]
    copies = []
    for i in range(n):
        copies += _direct_copies(ins[i], outs[i], *sems[3 * i:3 * i + 3], i >= gathers)
    return copies


def _rider_shapes(gathers, scatters):
    shapes = [jax.ShapeDtypeStruct((N_DEV,) + a.shape, a.dtype) for a in gathers]
    return shapes + [jax.ShapeDtypeStruct(a.shape, a.dtype) for a in scatters]


def all_gather_two_level(x, name, also=()):
    na = len(also)

    def body(*refs):
        x_ref, o_ref = refs[0], refs[1 + na]
        send_sems, recv_sems, local_sem = refs[2 + 2 * na:5 + 2 * na]
        riders = _riders(refs[1:1 + na] + refs[2 + na:2 + 2 * na] + refs[5 + 2 * na:], na, 0)
        for cp in riders:
            cp.start()
        me, sibling, chips = _chips()
        blk = lambda px, py, pc: o_ref.at[4 * px + 2 * py + pc]

        def copy(k, block, to, src=None):
            return pltpu.make_async_remote_copy(src_ref=blk(*block) if src is None else src, dst_ref=blk(*block),
                                                send_sem=send_sems.at[k], recv_sem=recv_sems.at[k], device_id=to,
                                                device_id_type=MESH)

        mine = pltpu.make_async_copy(x_ref, blk(*me), local_sem)
        mine.start()
        first = [copy(0, me, sibling, src=x_ref)] + [copy(1 + j, me, (*chip, me[2]), src=x_ref) for j, chip in enumerate(chips)]
        for cp in first:
            cp.start()
        passed = [copy(4 + j, (*chip, me[2]), sibling) for j, chip in enumerate(chips)]
        for j, chip in enumerate(chips):
            copy(1 + j, (*chip, me[2]), me).wait_recv()
            passed[j].start()
        copy(0, sibling, me).wait_recv()
        for j, chip in enumerate(chips):
            copy(4 + j, (*chip, 1 - me[2]), me).wait_recv()
        for cp in first + passed:
            cp.wait_send()
        mine.wait()
        for cp in riders:
            cp.wait()

    hbm = pl.BlockSpec(memory_space=pl.ANY)
    return pl.pallas_call(
        body, in_specs=[hbm] * (1 + na), out_specs=[hbm] * (1 + na),
        out_shape=[jax.ShapeDtypeStruct((N_DEV,) + x.shape, x.dtype)] + _rider_shapes(also, ()),
        scratch_shapes=_RIDE_SCRATCH * (1 + na), name=name)(x, *also)


def exchange_sibling(x, name):
    def body(x_ref, o_ref, send_sems, recv_sems):
        me, sibling, _ = _chips()
        copies = []
        for q in range(N_DEV // 2):
            cp = pltpu.make_async_remote_copy(src_ref=x_ref.at[2 * q + 1 - me[2]], dst_ref=o_ref.at[q], send_sem=send_sems.at[q],
                                              recv_sem=recv_sems.at[q], device_id=sibling, device_id_type=MESH)
            cp.start()
            copies.append(cp)
        for cp in copies:
            cp.wait()

    return pl.pallas_call(
        body, in_specs=[pl.BlockSpec(memory_space=pl.ANY)], out_specs=pl.BlockSpec(memory_space=pl.ANY),
        out_shape=jax.ShapeDtypeStruct((N_DEV // 2,) + x.shape[1:], x.dtype),
        scratch_shapes=[pltpu.SemaphoreType.DMA((N_DEV // 2,)), pltpu.SemaphoreType.DMA((N_DEV // 2,))], name=name)(x)


def exchange_chips(x, name, gathers=(), scatters=()):
    na = len(gathers) + len(scatters)

    def body(*refs):
        x_ref, o_ref = refs[0], refs[1 + na]
        send_sems, recv_sems, local_sem = refs[2 + 2 * na:5 + 2 * na]
        riders = _riders(refs[1:1 + na] + refs[2 + na:2 + 2 * na] + refs[5 + 2 * na:], len(gathers), len(scatters))
        for cp in riders:
            cp.start()
        me, _, chips = _chips()
        here = 2 * me[0] + me[1]
        local = pltpu.make_async_copy(x_ref.at[here], o_ref.at[here], local_sem)
        local.start()
        copies = []
        for j, (px, py) in enumerate(chips):
            cp = pltpu.make_async_remote_copy(src_ref=x_ref.at[2 * px + py], dst_ref=o_ref.at[here], send_sem=send_sems.at[j],
                                              recv_sem=recv_sems.at[j], device_id=(px, py, me[2]), device_id_type=MESH)
            cp.start()
            copies.append(cp)
        for cp in copies:
            cp.wait()
        local.wait()
        for cp in riders:
            cp.wait()

    hbm = pl.BlockSpec(memory_space=pl.ANY)
    return pl.pallas_call(
        body, in_specs=[hbm] * (1 + na), out_specs=[hbm] * (1 + na),
        out_shape=[jax.ShapeDtypeStruct(x.shape, x.dtype)] + _rider_shapes(gathers, scatters),
        scratch_shapes=[pltpu.SemaphoreType.DMA((3,)), pltpu.SemaphoreType.DMA((3,)), pltpu.SemaphoreType.DMA] + _RIDE_SCRATCH * na,
        name=name)(x, *gathers, *scatters)


def add_blocks(a, b, name):
    q, r, c = a.shape
    tr = _tile(r, max(16, (2 << 20) // (c * a.dtype.itemsize)), 16)
    spec = pl.BlockSpec((1, tr, c), lambda i, j: (i, j, 0))

    def body(a_ref, b_ref, o_ref):
        o_ref[...] = (a_ref[...].astype(F32) + b_ref[...].astype(F32)).astype(o_ref.dtype)

    return pl.pallas_call(
        body, grid=(q, r // tr), in_specs=[spec, spec], out_specs=spec, out_shape=jax.ShapeDtypeStruct(a.shape, a.dtype),
        compiler_params=_params(("parallel", "parallel")), name=name)(a, b)


def reduce_scatter_two_level(x, tag, gathers=(), scatters=()):
    q = N_DEV // 2
    from_sibling = exchange_sibling(x, f"{tag}_sibling")
    mine = lax.dynamic_index_in_dim(x.reshape((q, 2) + x.shape[1:]), lax.axis_index("c"), axis=1, keepdims=False)
    chip_sums = add_blocks(mine, from_sibling, f"{tag}_pair_sum")
    from_chips, *small = exchange_chips(chip_sums, f"{tag}_chips", gathers, scatters)
    return (sum_blocks(from_chips, f"{tag}_sum"), *small)


def sum_blocks(x, name):
    nb, r, c = x.shape
    tr = _tile(r, max(16, (4 << 20) // (nb * c * x.dtype.itemsize)), 16)

    def body(x_ref, o_ref):
        acc = x_ref[0].astype(F32)
        for i in range(1, nb):
            acc = acc + x_ref[i].astype(F32)
        o_ref[...] = acc

    return pl.pallas_call(
        body, grid=(r // tr,), in_specs=[pl.BlockSpec((nb, tr, c), lambda i: (0, i, 0))],
        out_specs=pl.BlockSpec((tr, c), lambda i: (i, 0)), out_shape=jax.ShapeDtypeStruct((r, c), F32),
        compiler_params=_params(("parallel",)), name=name)(x)


def _adamw(w, g, m, v):
    m = ADAM_B1 * m + (1.0 - ADAM_B1) * g
    v = ADAM_B2 * v + (1.0 - ADAM_B2) * jnp.square(g)
    m_hat = m / (1.0 - ADAM_B1 ** ADAM_STEP)
    v_hat = v / (1.0 - ADAM_B2 ** ADAM_STEP)
    delta = -ADAM_LR * (m_hat / (jnp.sqrt(v_hat) + ADAM_EPS) + ADAM_WD * w)
    return delta, m, v


def adamw(w, g, m, v, name):
    r, c = w.shape
    tr = _tile(r, 256, 8)
    spec = pl.BlockSpec((tr, c), lambda i: (i, 0))

    def body(w_ref, g_ref, m_ref, v_ref, d_ref, nm_ref, nv_ref):
        d_ref[...], nm_ref[...], nv_ref[...] = _adamw(w_ref[...], g_ref[...], m_ref[...], v_ref[...])

    return pl.pallas_call(
        body, grid=(r // tr,), in_specs=[spec] * 4, out_specs=[spec] * 3,
        out_shape=[jax.ShapeDtypeStruct((r, c), F32)] * 3, compiler_params=_params(("parallel",)), name=name)(w, g, m, v)


def batch_sum_rows(dh, bl, t, rows, name):
    d = dh.shape[1]

    def body(x_ref, o_ref):
        @pl.when(pl.program_id(0) == 0)
        def _():
            o_ref[...] = jnp.zeros_like(o_ref)

        o_ref[...] += x_ref[...]

    return pl.pallas_call(
        body, grid=(bl,), in_specs=[pl.BlockSpec((rows, d), lambda b: (b * (t // rows), 0))],
        out_specs=pl.BlockSpec((rows, d), lambda b: (0, 0)), out_shape=jax.ShapeDtypeStruct((rows, d), F32),
        compiler_params=_params(("arbitrary",)), name=name)(dh)


class Dims:
    def __init__(self, x, w_up, g_up, q_norm, kv_norm, d_ff):
        self.bl, self.seq, self.d = x.shape
        self.n_meta = 16
        self.t_real = self.n_meta + self.seq
        self.t = -(-self.t_real // LANES) * LANES
        self.n = self.bl * self.t
        self.f = d_ff
        self.wl, self.gl = w_up.shape[-2], g_up.shape[-2]
        self.ql, self.kvl = q_norm.shape[-1], kv_norm.shape[-1]
        self.hm = self.d // V_DIM
        self.in_cols = 5 * self.d + 2 * self.wl + self.gl + self.ql + self.kvl + ROPE_DIM


def _pad_cols(a, width):
    return jnp.pad(a, ((0, 0), (0, width - a.shape[1])))


def _pad_rows(a, rows):
    return jnp.pad(a, ((0, rows - a.shape[0]), (0, 0)))


def split_in(a, dm, axis=1):
    d, wl, gl, ql, kvl = dm.d, dm.wl, dm.gl, dm.ql, dm.kvl
    size = a.shape[axis]
    cut = lambda lo, hi: lax.slice_in_dim(a, min(lo, size), min(hi, size), axis=axis)

    def pad(p, width):
        cfg = [(0, 0)] * a.ndim
        cfg[axis] = (0, width - p.shape[axis])
        return jnp.pad(p, cfg)

    o = 3 * d
    lora = jnp.concatenate([pad(cut(o, o + wl), LANES), pad(cut(o + wl, o + 2 * wl), LANES),
                            cut(o + 2 * wl, o + 2 * wl + gl)], axis=axis)
    o += 2 * wl + gl
    mla = pad(cut(o, o + ql + kvl + ROPE_DIM), ql + kvl + LANES)
    o += ql + kvl + ROPE_DIM
    return dict(r=cut(0, d), k=cut(d, 2 * d), v=cut(2 * d, 3 * d), l=lora, m=mla, ga=cut(o, o + d), gb=cut(o + d, o + 2 * d))


def merge_in(g, dm, axis=1):
    wl, gl, ql, kvl = dm.wl, dm.gl, dm.ql, dm.kvl
    cut = lambda p, lo, hi: lax.slice_in_dim(p, lo, hi, axis=axis)
    l, m = g["l"], g["m"]
    return jnp.concatenate([g["r"], g["k"], g["v"], cut(l, 0, wl), cut(l, LANES, LANES + wl), cut(l, 2 * LANES, 2 * LANES + gl),
                            cut(m, 0, ql + kvl + ROPE_DIM), g["ga"], g["gb"]], axis=axis)


def split_uq(w, dm):
    w3 = w.reshape(w.shape[0], dm.hm, QK_DIM)
    return jnp.concatenate([w3[:, :, :NOPE_DIM].reshape(w.shape[0], -1), w3[:, :, NOPE_DIM:].reshape(w.shape[0], -1)], axis=1)


def merge_uq(gn, gp, dm):
    r = gn.shape[0]
    return jnp.concatenate([gn.reshape(r, dm.hm, NOPE_DIM), gp.reshape(r, dm.hm, ROPE_DIM)], axis=2).reshape(r, -1)


def split_ukv(w, dm):
    w3 = w.reshape(w.shape[0], dm.hm, NOPE_DIM + V_DIM)
    return jnp.concatenate([w3[:, :, :NOPE_DIM].reshape(w.shape[0], -1), w3[:, :, NOPE_DIM:].reshape(w.shape[0], -1)], axis=1)


def merge_ukv(gk, gv, dm):
    r = gk.shape[0]
    return jnp.concatenate([gk.reshape(r, dm.hm, NOPE_DIM), gv.reshape(r, dm.hm, V_DIM)], axis=2).reshape(r, -1)


def head_matrices(d):
    heads = d // RWKV_HEAD
    e = (np.arange(d)[:, None] // RWKV_HEAD == np.arange(LANES)[None, :]) & (np.arange(LANES)[None, :] < heads)
    return jnp.asarray(e, BF16), jnp.asarray(e.T, BF16)


def rope_tables(t):
    pos = jnp.arange(t, dtype=F32)
    inv_freq = 1.0 / (ROPE_THETA ** (jnp.arange(0, ROPE_DIM, 2, dtype=F32) / ROPE_DIM))
    ang = pos[:, None] * inv_freq[None, :]
    cos, sin = jnp.cos(ang), jnp.sin(ang)
    return jnp.tile(jnp.concatenate([cos, cos], axis=1), (1, 2)), jnp.tile(jnp.concatenate([-sin, sin], axis=1), (1, 2))


def local_step(dm, x, loss_target, meta, wt, late_shards, late_rows, sp):
    bl, t, n, d, hm = dm.bl, dm.t, dm.n, dm.d, dm.hm
    e, et = head_matrices(d)
    ct, st = rope_tables(t)
    padz = jnp.zeros((bl, t - dm.t_real, d), F32)
    h0 = jnp.concatenate([jnp.broadcast_to(meta[None], (bl, dm.n_meta, d)), x, padz], axis=1).reshape(n, d)
    tgt = jnp.concatenate([jnp.zeros((bl, dm.n_meta, d), F32), loss_target, padz], axis=1).reshape(n, d)
    tpos = jnp.arange(t)
    mask = jnp.tile(((tpos >= dm.n_meta) & (tpos < dm.t_real)).astype(F32), bl).reshape(n, 1)

    win = split_in(wt["w_in"], dm, axis=0)
    mu = split_in(sp["tm_mu"], dm)
    wq, wkv = split_uq(wt["w_uq"], dm), split_ukv(wt["w_ukv"], dm)
    prm = dict(w0=sp["w0"], a0=sp["a0"], k_k=sp["k_k"], k_a=sp["k_a"], gn_w=sp["gn_w"], gn_b=sp["gn_b"], r_k=sp["r_k"],
               w_up=_pad_rows(wt["w_up"], LANES).astype(F32), a_up=_pad_rows(wt["a_up"], LANES).astype(F32),
               g_up=wt["g_up"].astype(F32))

    h1, ffn1 = ffn_forward(h0, sp["ffn1_norm"], wt["ffn1_w_gate"], wt["ffn1_w_up"], wt["ffn1_w_down"], "ffn1")
    u = rms_fwd(h1, sp["mix_norm"], "mix_rms")
    proj = {key: matmul([(u, win[key])], "nt", name=f"proj_{key}") for key in win}
    sh = {key: lerp_fwd(proj[key], mu[key], bl, t, f"shift_{key}") for key in ("r", "k", "v", "l")}
    decay, kmod, kneg, bvec, gate = rwkv_prep_fwd(sh["k"], sh["l"], prm, e, et, "rwkv_prep")
    pairs = min(SCAN_PAIRS, d // LANES)
    y, hist, late_all = scan_forward(sh["r"], decay, kmod, kneg, bvec, sh["v"], late_shards, bl, t, dm.t_real, d, "wkv_scan",
                                     min(SCAN_FWD_PAIRS, d // LANES), SCAN_FWD_STEPS)
    wt = dict(wt, **{key: late_all[:, lo:hi].reshape(-1, d) for key, lo, hi in zip(LATE, late_rows[:-1], late_rows[1:])})
    cqn, ckvn = mla_prep_fwd(proj["m"], sp["q_norm"], sp["kv_norm"], "mla_norms")
    q = matmul([(cqn, wq)], "nn", name="mla_q")
    kv = matmul([(ckvn, wkv)], "nn", name="mla_kv")
    o, lse = attn_fwd(q, kv, proj["m"], ct, st, bl, t, hm, "mla_attn")
    post_in = [y, sh["r"], kmod, sh["v"], gate, proj["ga"], proj["gb"], o]
    mix = rwkv_post_fwd(post_in, prm, e, et, "mix_gate")
    h2 = matmul([(mix, wt["w_out"])], "nn", res=h1, name="out_proj")
    h3, ffn2 = ffn_forward(h2, sp["ffn2_norm"], wt["ffn2_w_gate"], wt["ffn2_w_up"], wt["ffn2_w_down"], "ffn2")
    dh3, d_final, loss = loss_head(h3, tgt, mask, sp["final_norm"], "loss_head")

    gw, gs = {}, {"final_norm": d_final}
    dh2, gs["ffn2_norm"], gw["ffn2_w_gate"], gw["ffn2_w_up"], gw["ffn2_w_down"] = ffn_backward(
        dh3, h2, sp["ffn2_norm"], wt["ffn2_w_gate"], wt["ffn2_w_up"], wt["ffn2_w_down"], ffn2, "ffn2")
    dmix = matmul([(dh2, wt["w_out"])], "nt", name="out_proj_dx")
    gw["w_out"] = matmul([(mix, dh2)], "tn", out_dtype=MMD, name="out_proj_dw")
    late_grads = jnp.concatenate([gw.pop(key).reshape(N_DEV, hi - lo, d) for key, lo, hi in
                                  zip(LATE, late_rows[:-1], late_rows[1:])], axis=1).astype(MMD)
    (dy, dr_p, dkm_p, dv_p, dgate, dpga, dpgb, do, gs["gn_w"], gs["gn_b"], gs["r_k"]) = rwkv_post_bwd(
        post_in, prm, e, et, dmix, "mix_gate_bwd")
    dqn, dqpe, dkn, dv_att, dkpe = attn_bwd(q, kv, proj["m"], o, do, lse, ct, st, bl, t, hm, "mla_attn_bwd")
    nq = hm * NOPE_DIM
    dcqn = matmul([(dqn, wq[:, :nq])], "nt", name="mla_q_dx1")
    dcqn = matmul([(dqpe, wq[:, nq:])], "nt", res=dcqn, name="mla_q_dx2")
    gw["w_uq"] = merge_uq(matmul([(cqn, dqn)], "tn", name="mla_q_dw1"), matmul([(cqn, dqpe)], "tn", name="mla_q_dw2"), dm)
    dckvn = matmul([(dkn, wkv[:, :nq]), (dv_att, wkv[:, nq:])], "nt", name="mla_kv_dx", tk=1024)
    gw["w_ukv"] = merge_ukv(matmul([(ckvn, dkn)], "tn", name="mla_kv_dw1"), matmul([(ckvn, dv_att)], "tn", name="mla_kv_dw2"), dm)
    dproj = {"ga": dpga, "gb": dpgb}
    dproj["m"], gs["q_norm"], gs["kv_norm"] = mla_prep_bwd(proj["m"], sp["q_norm"], sp["kv_norm"], dcqn, dckvn, dkpe, "mla_norms_bwd")
    dr_s, ddecay, dk_s, dkneg, dbvec, dv_s, late_recv = scan_backward(
        sh["r"], decay, kmod, kneg, bvec, sh["v"], dy, hist, late_grads, bl, t, dm.t_real, d, "wkv_scan_bwd", pairs,
        SCAN_BWD_STEPS)
    late_sum = sum_blocks(late_recv, "sum_late")
    (dsh_k, dsh_l, gs["w0"], gs["a0"], gs["k_k"], gs["k_a"], g_wup, g_aup, gw["g_up"]) = rwkv_prep_bwd(
        sh["k"], sh["l"], prm, e, et, [ddecay, dk_s, dkm_p, dkneg, dbvec, dgate], "rwkv_prep_bwd")
    gw["w_up"], gw["a_up"] = g_wup[:dm.wl], g_aup[:dm.wl]
    dmu = {}
    for key, cts in (("r", [dr_s, dr_p]), ("k", [dsh_k]), ("v", [dv_s, dv_p]), ("l", [dsh_l])):
        dproj[key], dmu[key] = lerp_bwd(proj[key], mu[key], cts, bl, t, f"shift_{key}_bwd")
    zero_m = jnp.zeros((1, proj["m"].shape[1]), F32)
    gs["tm_mu"] = merge_in(dict(dmu, m=zero_m, ga=zero_m[:, :0], gb=zero_m[:, :0]), dm)[:, :3 * d + 2 * dm.wl + dm.gl]
    wide = ("r", "k", "v", "ga", "gb")
    du = matmul([(dproj[key], win[key]) for key in wide], "nn", name="proj_dx", tn=512, tk=512)
    du = matmul([(dproj["l"], win["l"])], "nn", res=du, name="proj_dx_l")
    du = matmul([(dproj["m"], win["m"])], "nn", res=du, name="proj_dx_m")
    gw["w_in"] = merge_in({key: matmul([(dproj[key], u)], "tn", out_dtype=MMD, name=f"proj_dw_{key}") for key in win},
                          dm, axis=0)
    dh1, gs["mix_norm"] = rms_bwd(h1, sp["mix_norm"], du, dh2, "mix_rms_bwd")
    dh0, gs["ffn1_norm"], gw["ffn1_w_gate"], gw["ffn1_w_up"], gw["ffn1_w_down"] = ffn_backward(
        dh1, h0, sp["ffn1_norm"], wt["ffn1_w_gate"], wt["ffn1_w_up"], wt["ffn1_w_down"], ffn1, "ffn1")
    grad_x = dh0.reshape(bl, t, d)[:, dm.n_meta:dm.t_real]
    dmeta = batch_sum_rows(dh0, bl, t, dm.n_meta, "meta_grad")
    return loss, grad_x, dmeta, gw, late_sum, gs


TRANSPOSED = ("ffn1_w_gate", "ffn1_w_up", "w_in", "ffn2_w_gate", "ffn2_w_up")
EARLY = ("ffn1_w_gate", "ffn1_w_up", "ffn1_w_down", "w_in")
LATE = ("w_out", "ffn2_w_gate", "ffn2_w_up", "ffn2_w_down")
NARROW = ("w_up", "a_up", "g_up", "w_uq", "w_ukv")
MATRICES = ("ffn1_w_gate", "ffn1_w_up", "ffn1_w_down", "w_in", "w_up", "a_up", "g_up", "w_uq", "w_ukv", "w_out",
            "ffn2_w_gate", "ffn2_w_up", "ffn2_w_down")
SMALL = ("ffn1_norm", "mix_norm", "tm_mu", "w0", "a0", "k_k", "k_a", "r_k", "gn_w", "gn_b", "q_norm", "kv_norm",
         "ffn2_norm", "final_norm")
WEIGHTS = ("meta_tokens", "ffn1_norm", "ffn1_w_gate", "ffn1_w_up", "ffn1_w_down", "mix_norm", "w_in", "tm_mu", "w0", "w_up",
           "a0", "a_up", "g_up", "k_k", "k_a", "r_k", "gn_w", "gn_b", "q_norm", "w_uq", "kv_norm", "w_ukv", "w_out",
           "ffn2_norm", "ffn2_w_gate", "ffn2_w_up", "ffn2_w_down", "final_norm")
PACK_COLS = 1024
PACK_ALIGN = 16 * PACK_COLS


def _pack(parts):
    offs, o = [], 0
    for p in parts:
        offs.append(o)
        o += p.shape[1]
    total = -(-o // PACK_ALIGN) * PACK_ALIGN
    flat = jnp.concatenate(list(parts) + [jnp.zeros((parts[0].shape[0], total - o), parts[0].dtype)], axis=1)
    return flat.reshape(parts[0].shape[0], total // PACK_COLS, PACK_COLS), offs


def kernel(x, meta_tokens, ffn1_norm, ffn1_w_gate, ffn1_w_up, ffn1_w_down, mix_norm, w_in, tm_mu, w0, w_up, a0, a_up, g_up, k_k, k_a, r_k, gn_w, gn_b, q_norm, w_uq, kv_norm, w_ukv, w_out, ffn2_norm, ffn2_w_gate, ffn2_w_up, ffn2_w_down, final_norm, loss_target, m_meta_tokens, m_ffn1_norm, m_ffn1_w_gate, m_ffn1_w_up, m_ffn1_w_down, m_mix_norm, m_w_in, m_tm_mu, m_w0, m_w_up, m_a0, m_a_up, m_g_up, m_k_k, m_k_a, m_r_k, m_gn_w, m_gn_b, m_q_norm, m_w_uq, m_kv_norm, m_w_ukv, m_w_out, m_ffn2_norm, m_ffn2_w_gate, m_ffn2_w_up, m_ffn2_w_down, m_final_norm, v_meta_tokens, v_ffn1_norm, v_ffn1_w_gate, v_ffn1_w_up, v_ffn1_w_down, v_mix_norm, v_w_in, v_tm_mu, v_w0, v_w_up, v_a0, v_a_up, v_g_up, v_k_k, v_k_a, v_r_k, v_gn_w, v_gn_b, v_q_norm, v_w_uq, v_kv_norm, v_w_ukv, v_w_out, v_ffn2_norm, v_ffn2_w_gate, v_ffn2_w_up, v_ffn2_w_down, v_final_norm):
    args = dict(locals())
    wts = {k: args[k] for k in WEIGHTS}
    ms = {k: args["m_" + k] for k in WEIGHTS}
    vs = {k: args["v_" + k] for k in WEIGHTS}
    dm = Dims(x, w_up, g_up, q_norm, kv_norm, ffn1_w_down.shape[1] * N_DEV)

    shard2d = {k: wts[k].reshape(wts[k].shape[-2], wts[k].shape[-1]) for k in MATRICES}
    sent = {k: shard2d[k].astype(MMD).T if k in TRANSPOSED else shard2d[k] for k in MATRICES}
    early_rows = np.cumsum([0] + [sent[k].shape[0] for k in EARLY])
    late_rows = np.cumsum([0] + [sent[k].shape[0] for k in LATE])
    send, offs = _pack([sent[k].astype(MMD).reshape(1, -1) for k in NARROW])
    got_early, got, got_meta = all_gather_two_level(jnp.concatenate([sent[k].astype(MMD) for k in EARLY], axis=0),
                                                    "gather_early", also=(send[0], meta_tokens))
    full = {k: got_early[:, lo:hi].reshape(-1, dm.d) for k, lo, hi in zip(EARLY, early_rows[:-1], early_rows[1:])}
    late_shards = jnp.concatenate([sent[k].astype(MMD) for k in LATE], axis=0)
    got = got.reshape(N_DEV, -1)
    for k, o in zip(NARROW, offs):
        r, c = sent[k].shape
        full[k] = got[:, o:o + r * c].reshape(N_DEV, r, c).transpose(1, 0, 2).reshape(r, N_DEV * c)
    mr, mc = meta_tokens.shape
    meta = got_meta.transpose(1, 0, 2).reshape(mr, N_DEV * mc)
    small = {k: wts[k].reshape(1, -1) for k in SMALL}

    loss, grad_x, dmeta, gw, gsum_late, gs = local_step(dm, x, loss_target, meta, full, late_shards, late_rows, small)

    def blocks(k, g):
        r, c = sent[k].shape
        return g.reshape(r, N_DEV, c).transpose(1, 0, 2).reshape(N_DEV, r * c)

    gsend, goffs = _pack([blocks(k, gw[k]).astype(MMD) for k in NARROW]
                         + [dmeta.reshape(mr, N_DEV, mc).transpose(1, 0, 2).reshape(N_DEV, mr * mc).astype(MMD)])
    ssend, soffs = _pack([gs[k].reshape(1, -1) for k in SMALL] + [loss])
    gearly = jnp.concatenate([gw[k].reshape(N_DEV, sent[k].shape[0], dm.d) for k in EARLY], axis=1).astype(MMD)
    gsum_early, small_parts, narrow_parts = reduce_scatter_two_level(gearly, "scatter_early", gathers=(ssend[0],),
                                                                     scatters=(gsend,))
    grads = {}
    for names, rows, gsum_rows in ((EARLY, early_rows, gsum_early), (LATE, late_rows, gsum_late)):
        for k, lo, hi in zip(names, rows[:-1], rows[1:]):
            grads[k] = gsum_rows[lo:hi].T if k in TRANSPOSED else gsum_rows[lo:hi]
    gsum = sum_blocks(narrow_parts, "sum_narrow").reshape(-1)
    for k, o in zip(NARROW, goffs):
        r, c = sent[k].shape
        grads[k] = gsum[o:o + r * c].reshape(r, c)
    grads["meta_tokens"] = gsum[goffs[-1]:goffs[-1] + mr * mc].reshape(mr, mc)
    ssum = sum_blocks(small_parts, "sum_small").reshape(-1)
    for k, o in zip(SMALL, soffs):
        grads[k] = ssum[o:o + small[k].shape[1]]
    loss_total = ssum[soffs[-1]]

    delta, new_m, new_v = {}, {}, {}
    for k in MATRICES + ("meta_tokens",):
        shp = wts[k].shape
        to2d = lambda a: a.reshape(shp[-2], shp[-1])
        dlt, nm, nv = adamw(to2d(wts[k]), grads[k], to2d(ms[k]), to2d(vs[k]), f"adamw_{k}")
        delta[k], new_m[k], new_v[k] = dlt.reshape(shp), nm.reshape(shp), nv.reshape(shp)
        grads[k] = grads[k].reshape(shp)
    pw, _ = _pack([wts[k].reshape(1, -1) for k in SMALL])
    pm_, _ = _pack([ms[k].reshape(1, -1) for k in SMALL])
    pv, _ = _pack([vs[k].reshape(1, -1) for k in SMALL])
    pg, poffs = _pack([grads[k].reshape(1, -1) for k in SMALL])
    dlt, nm, nv = adamw(pw[0], pg[0], pm_[0], pv[0], "adamw_small")
    for k, o in zip(SMALL, poffs):
        shp, sz = wts[k].shape, small[k].shape[1]
        cut = lambda a: a.reshape(-1)[o:o + sz].reshape(shp)
        delta[k], new_m[k], new_v[k] = cut(dlt), cut(nm), cut(nv)
        grads[k] = grads[k].reshape(shp)

    return (loss_total, grad_x, *[grads[k] for k in WEIGHTS], *[delta[k] for k in WEIGHTS],
            *[new_m[k] for k in WEIGHTS], *[new_v[k] for k in WEIGHTS])
```

```python
import functools

import numpy as np
import jax
import jax.numpy as jnp
from jax import lax
from jax.experimental import pallas as pl
from jax.experimental.pallas import tpu as pltpu

F32 = jnp.float32
BF16 = jnp.bfloat16
MMD = BF16

NORM_EPS = 1e-6
RWKV_HEAD = 64
GN_EPS = RWKV_HEAD * 1e-5
NOPE_DIM = 128
ROPE_DIM = 64
V_DIM = 128
QK_DIM = NOPE_DIM + ROPE_DIM
ROPE_THETA = 10000.0
ADAM_LR = 0.001
ADAM_B1 = 0.9
ADAM_B2 = 0.999
ADAM_EPS = 1e-08
ADAM_WD = 0.01
ADAM_STEP = 10

LANES = 128
SCAN_FWD_PAIRS = 16
SCAN_FWD_STEPS = 16
SCAN_PAIRS = 8
SCAN_BWD_STEPS = 16
N_DEV = 8
VMEM_LIMIT = 56 * 1024 * 1024
MESH = pl.DeviceIdType.MESH


def _tile(n, target, align):
    best = None
    for d in range(align, min(n, target) + 1, align):
        if n % d == 0:
            best = d
    return best if best is not None else n


def _params(sem=None):
    return pltpu.CompilerParams(dimension_semantics=sem, vmem_limit_bytes=VMEM_LIMIT)


def _mm(a, b, dims=((1,), (0,))):
    return lax.dot_general(a.astype(MMD), b.astype(MMD), (dims, ((), ())), preferred_element_type=F32)


@jax.custom_vjp
def mmdot(a, b):
    return _mm(a, b)


def _mmdot_fwd(a, b):
    return _mm(a, b), (a, b)


def _mmdot_bwd(res, g):
    a, b = res
    return _mm(g, b, ((1,), (1,))).astype(a.dtype), _mm(a, g, ((0,), (0,))).astype(b.dtype)


mmdot.defvjp(_mmdot_fwd, _mmdot_bwd)


def _dot2(x, m):
    hi = x.astype(BF16)
    lo = (x - hi.astype(F32)).astype(BF16)
    return (lax.dot_general(hi, m, (((1,), (0,)), ((), ())), preferred_element_type=F32)
            + lax.dot_general(lo, m, (((1,), (0,)), ((), ())), preferred_element_type=F32))


@jax.custom_vjp
def segsum(x, e, et):
    return _dot2(_dot2(x, e), et)


def _segsum_fwd(x, e, et):
    return segsum(x, e, et), (e, et)


def _segsum_bwd(res, g):
    e, et = res
    return segsum(g, e, et), jnp.zeros_like(e), jnp.zeros_like(et)


segsum.defvjp(_segsum_fwd, _segsum_bwd)


def _sigmoid(x):
    return 1.0 / (1.0 + jnp.exp(-x))


def _softplus(x):
    return jnp.maximum(x, 0.0) + jnp.log(1.0 + jnp.exp(-jnp.abs(x)))


def _rms(x, g):
    return x * lax.rsqrt(jnp.mean(x * x, axis=-1, keepdims=True) + NORM_EPS) * g


_DIMS = {"nn": ((1,), (0,)), "nt": ((1,), (1,)), "tn": ((0,), (0,))}


def matmul(pairs, mode, *, name, out_dtype=F32, res=None, alpha=1.0, tm=1088, tn=1024, tk=2048):
    a0, b0 = pairs[0]
    if mode == "nn":
        (m, k), n = a0.shape, b0.shape[1]
    elif mode == "nt":
        (m, k), n = a0.shape, b0.shape[0]
    else:
        (k, m), n = a0.shape, b0.shape[1]
    tm = _tile(m, 1408, 128) if mode == "tn" else _tile(m, tm, 16)
    tn = _tile(n, 2048 if mode == "tn" else tn, 128)
    tk = _tile(k, min(tk, 1024), 16) if mode == "tn" else _tile(k, tk, 128)
    nk = k // tk
    npair = len(pairs)
    if mode == "tn":
        a_spec = pl.BlockSpec((tk, tm), lambda i, j, kk: (kk, i))
    else:
        a_spec = pl.BlockSpec((tm, tk), lambda i, j, kk: (i, kk))
    if mode == "nt":
        b_spec = pl.BlockSpec((tn, tk), lambda i, j, kk: (j, kk))
    else:
        b_spec = pl.BlockSpec((tk, tn), lambda i, j, kk: (kk, j))
    o_spec = pl.BlockSpec((tm, tn), lambda i, j, kk: (i, j))
    dims = _DIMS[mode]

    def body(*refs):
        ab = refs[:2 * npair]
        res_ref = refs[2 * npair] if res is not None else None
        o_ref, acc_ref = refs[-2], refs[-1]
        kk = pl.program_id(2)

        @pl.when(kk == 0)
        def _():
            acc_ref[...] = jnp.zeros_like(acc_ref)

        part = _mm(ab[0][...], ab[1][...], dims)
        for p in range(1, npair):
            part = part + _mm(ab[2 * p][...], ab[2 * p + 1][...], dims)
        acc_ref[...] += part

        @pl.when(kk == nk - 1)
        def _():
            out = acc_ref[...] * alpha if alpha != 1.0 else acc_ref[...]
            if res_ref is not None:
                out = res_ref[...].astype(F32) + out
            o_ref[...] = out.astype(o_ref.dtype)

    args, specs = [], []
    for a, b in pairs:
        args += [a, b]
        specs += [a_spec, b_spec]
    if res is not None:
        args.append(res)
        specs.append(o_spec)
    return pl.pallas_call(
        body, grid=(m // tm, n // tn, nk), in_specs=specs, out_specs=o_spec,
        out_shape=jax.ShapeDtypeStruct((m, n), out_dtype), scratch_shapes=[pltpu.VMEM((tm, tn), F32)],
        compiler_params=_params(("parallel", "parallel", "arbitrary")), name=name)(*args)


def tilek(fn, ins, outs, *, n_rows, tr, name):
    tr = _tile(n_rows, tr, 16)
    n_in = len(ins)
    in_specs = []
    for arr, kind in ins:
        if kind == "r":
            in_specs.append(pl.BlockSpec((tr, arr.shape[1]), lambda i: (i, 0)))
        else:
            in_specs.append(pl.BlockSpec(arr.shape, lambda i, nd=arr.ndim: (0,) * nd))
    out_specs, out_shape = [], []
    has_acc = False
    for o in outs:
        if o[0] == "r":
            out_specs.append(pl.BlockSpec((tr, o[1]), lambda i: (i, 0)))
            out_shape.append(jax.ShapeDtypeStruct((n_rows, o[1]), o[2]))
        else:
            has_acc = True
            out_specs.append(pl.BlockSpec(o[1], lambda i, nd=len(o[1]): (0,) * nd))
            out_shape.append(jax.ShapeDtypeStruct(o[1], F32))

    def body(*refs):
        i = pl.program_id(0)
        vals = fn(*[r[...] for r in refs[:n_in]])
        for o, r, v in zip(outs, refs[n_in:], vals):
            if o[0] == "r":
                r[...] = v.astype(r.dtype)
            else:
                @pl.when(i == 0)
                def _(r=r):
                    r[...] = jnp.zeros_like(r)

                r[...] += v

    return pl.pallas_call(
        body, grid=(n_rows // tr,), in_specs=in_specs, out_specs=out_specs, out_shape=out_shape,
        compiler_params=_params(("arbitrary",) if has_acc else ("parallel",)), name=name)(*[a for a, _ in ins])


def rms_fwd(x, g, name):
    n, d = x.shape
    return tilek(lambda xv, gv: (_rms(xv, gv),), [(x, "r"), (g, "f")], [("r", d, MMD)], n_rows=n, tr=256, name=name)[0]


def rms_bwd(x, g, dy, dres, name):
    n, d = x.shape

    def fn(xv, gv, dyv, drv):
        _, vjp = jax.vjp(_rms, xv, gv)
        dx, dg = vjp(dyv.astype(F32))
        return drv + dx, drv + dx, dg

    return tilek(fn, [(x, "r"), (g, "f"), (dy, "r"), (dres, "r")], [("r", d, F32), ("r", d, MMD), ("acc", (1, d))],
                 n_rows=n, tr=128, name=name)


def loss_head(h, tgt, mask, g, name):
    n, d = h.shape

    def fn(hv, tv, mv, gv):
        def lossf(hh, gg):
            e = (_rms(hh, gg) - tv) * mv
            s = jnp.sum(jnp.sum(e * e, axis=1, keepdims=True), axis=0, keepdims=True)
            return s * (0.5 / d)

        l, vjp = jax.vjp(lossf, hv, gv)
        dh, dg = vjp(jnp.ones((1, 1), F32))
        return dh, dh, dg, jnp.broadcast_to(l, (1, LANES))

    return tilek(fn, [(h, "r"), (tgt, "r"), (mask, "r"), (g, "f")],
                 [("r", d, F32), ("r", d, MMD), ("acc", (1, d)), ("acc", (1, LANES))], n_rows=n, tr=128, name=name)


def ffn_up(hn, wg, wu, name):
    n, d = hn.shape
    f = wg.shape[0]
    tm, tn = _tile(n, 544, 16), _tile(f, 1408, 128)

    def body(a_ref, g_ref, u_ref, og_ref, ou_ref, oa_ref):
        a = a_ref[...]
        g = _mm(a, g_ref[...], ((1,), (1,)))
        u = _mm(a, u_ref[...], ((1,), (1,)))
        og_ref[...] = g.astype(og_ref.dtype)
        ou_ref[...] = u.astype(ou_ref.dtype)
        oa_ref[...] = (g * _sigmoid(g) * u).astype(oa_ref.dtype)

    o_spec = pl.BlockSpec((tm, tn), lambda i, j: (i, j))
    w_spec = pl.BlockSpec((tn, d), lambda i, j: (j, 0))
    return pl.pallas_call(
        body, grid=(n // tm, f // tn), in_specs=[pl.BlockSpec((tm, d), lambda i, j: (i, 0)), w_spec, w_spec],
        out_specs=[o_spec, o_spec, o_spec],
        out_shape=[jax.ShapeDtypeStruct((n, f), MMD)] * 3,
        compiler_params=_params(("parallel", "parallel")), name=name)(hn, wg, wu)


def ffn_down_bwd(dh, wd, gate, up, name):
    n, d = dh.shape
    f = wd.shape[0]
    tm, tn = _tile(n, 544, 16), _tile(f, 1408, 128)

    def body(dh_ref, w_ref, g_ref, u_ref, dg_ref, du_ref):
        da = 0.5 * _mm(dh_ref[...], w_ref[...], ((1,), (1,)))
        g, u = g_ref[...].astype(F32), u_ref[...].astype(F32)
        s = _sigmoid(g)
        dg_ref[...] = (da * u * (s * (1.0 + g * (1.0 - s)))).astype(dg_ref.dtype)
        du_ref[...] = (da * (g * s)).astype(du_ref.dtype)

    o_spec = pl.BlockSpec((tm, tn), lambda i, j: (i, j))
    return pl.pallas_call(
        body, grid=(n // tm, f // tn),
        in_specs=[pl.BlockSpec((tm, d), lambda i, j: (i, 0)), pl.BlockSpec((tn, d), lambda i, j: (j, 0)), o_spec, o_spec],
        out_specs=[o_spec, o_spec],
        out_shape=[jax.ShapeDtypeStruct((n, f), MMD), jax.ShapeDtypeStruct((n, f), MMD)],
        compiler_params=_params(("parallel", "parallel")), name=name)(dh, wd, gate, up)


def ffn_forward(h, g, wg, wu, wd, tag):
    hn = rms_fwd(h, g, f"{tag}_rms")
    gate, up, act = ffn_up(hn, wg, wu, f"{tag}_up")
    out = matmul([(act, wd)], "nn", res=h, alpha=0.5, name=f"{tag}_down")
    return out, (hn, gate, up, act)


def ffn_backward(dout, dout_lp, h, g, wg, wu, wd, saved, tag):
    hn, gate, up, act = saved
    dgate, dup = ffn_down_bwd(dout_lp, wd, gate, up, f"{tag}_dact")
    dwd = matmul([(act, dout_lp)], "tn", alpha=0.5, out_dtype=MMD, name=f"{tag}_dwd")
    dwg = matmul([(dgate, hn)], "tn", out_dtype=MMD, name=f"{tag}_dwg")
    dwu = matmul([(dup, hn)], "tn", out_dtype=MMD, name=f"{tag}_dwu")
    dhn = matmul([(dgate, wg), (dup, wu)], "nn", name=f"{tag}_dhn")
    dh, dh_lp, dg = rms_bwd(h, g, dhn, dout, f"{tag}_drms")
    return dh, dh_lp, dg, dwg, dwu, dwd


def lerp_fwd(p, mu, bl, t, name):
    n, w = p.shape
    cb = _tile(w, 256, 128)

    def body(p_ref, mu_ref, o_ref):
        x = p_ref[...]
        row = lax.broadcasted_iota(jnp.int32, x.shape, 0)
        prev = jnp.where(row == 0, 0.0, pltpu.roll(x, 1, 0))
        o_ref[...] = x + mu_ref[...] * (prev - x)

    spec = pl.BlockSpec((t, cb), lambda b, j: (b, j))
    return pl.pallas_call(
        body, grid=(bl, w // cb), in_specs=[spec, pl.BlockSpec((1, cb), lambda b, j: (0, j))], out_specs=spec,
        out_shape=jax.ShapeDtypeStruct((n, w), F32), compiler_params=_params(("parallel", "parallel")), name=name)(p, mu)


def lerp_bwd(p, mu, douts, bl, t, name):
    n, w = p.shape
    cb = _tile(w, 256, 128)
    nd = len(douts)

    def body(*refs):
        p_ref, mu_ref = refs[0], refs[1]
        dp_ref, dmu_ref = refs[2 + nd], refs[3 + nd]
        b = pl.program_id(1)
        x, m = p_ref[...], mu_ref[...]
        d = refs[2][...]
        for r in refs[3:2 + nd]:
            d = d + r[...]
        row = lax.broadcasted_iota(jnp.int32, x.shape, 0)
        prev = jnp.where(row == 0, 0.0, pltpu.roll(x, 1, 0))
        z = d * m
        nxt = jnp.where(row == t - 1, 0.0, pltpu.roll(z, t - 1, 0))
        dp_ref[...] = d - z + nxt

        @pl.when(b == 0)
        def _():
            dmu_ref[...] = jnp.zeros_like(dmu_ref)

        dmu_ref[...] += jnp.sum(d * (prev - x), axis=0, keepdims=True)

    spec = pl.BlockSpec((t, cb), lambda j, b: (b, j))
    cspec = pl.BlockSpec((1, cb), lambda j, b: (0, j))
    return pl.pallas_call(
        body, grid=(w // cb, bl), in_specs=[spec, cspec] + [spec] * nd, out_specs=[spec, cspec],
        out_shape=[jax.ShapeDtypeStruct((n, w), F32), jax.ShapeDtypeStruct((1, w), F32)],
        compiler_params=_params(("parallel", "arbitrary")), name=name)(p, mu, *douts)


def _prep(k, xw, xa, xg, w0, a0, k_k, k_a, w_up, a_up, g_up, e, et):
    w_pre = -_softplus(-(w0 + mmdot(jnp.tanh(xw), w_up))) - 0.5
    decay = jnp.exp(-jnp.exp(w_pre))
    a = _sigmoid(a0 + mmdot(xa, a_up))
    g = mmdot(_sigmoid(xg), g_up)
    kk = k * k_k
    kk = kk * lax.rsqrt(jnp.maximum(segsum(kk * kk, e, et), 1e-24))
    kmod = k * (1.0 + (a - 1.0) * k_a)
    return decay, kmod, -kk, kk * a, g


def _lora_parts(xl):
    return xl[:, :LANES], xl[:, LANES:2 * LANES], xl[:, 2 * LANES:]


def rwkv_prep_fwd(pk, pl_, prm, e, et, name):
    n, d = pk.shape
    small = [prm[k] for k in ("w0", "a0", "k_k", "k_a", "w_up", "a_up", "g_up")]
    ins = [(pk, "r"), (pl_, "r")] + [(s, "f") for s in small] + [(e, "f"), (et, "f")]
    return tilek(lambda k, xl, *rest: _prep(k, *_lora_parts(xl), *rest), ins, [("r", d, F32)] * 5, n_rows=n, tr=128, name=name)


def rwkv_prep_bwd(pk, pl_, prm, e, et, cts, name):
    n, d = pk.shape
    small = [prm[k] for k in ("w0", "a0", "k_k", "k_a", "w_up", "a_up", "g_up")]

    def fn(k, xl, w0, a0, k_k, k_a, w_up, a_up, g_up, ev, etv, dw, dkm1, dkm2, dkn, db, dg):
        _, vjp = jax.vjp(lambda *a: _prep(*a, ev, etv), k, *_lora_parts(xl), w0, a0, k_k, k_a, w_up, a_up, g_up)
        dk, dxw, dxa, dxg, *dsmall = vjp((dw, dkm1 + dkm2, dkn, db, dg))
        return (dk, jnp.concatenate([dxw, dxa, dxg], axis=1), *dsmall)

    ins = [(pk, "r"), (pl_, "r")] + [(s, "f") for s in small] + [(e, "f"), (et, "f")] + [(c, "r") for c in cts]
    outs = [("r", d, F32), ("r", pl_.shape[1], F32)] + [("acc", s.shape) for s in small]
    return tilek(fn, ins, outs, n_rows=n, tr=64, name=name)


def _post(y, r, km, v, g, pga, pgb, yb, gn_w, gn_b, r_k, e, et):
    inv = 1.0 / RWKV_HEAD
    yc = y - segsum(y, e, et) * inv
    var = segsum(yc * yc, e, et) * inv
    yn = yc * lax.rsqrt(var + GN_EPS) * gn_w + gn_b
    bonus = segsum(r * km * r_k, e, et) * v
    ya = (yn + bonus) * g
    return _sigmoid(pga) * ya + _sigmoid(pgb) * yb


def rwkv_post_fwd(acts, prm, e, et, name):
    n, d = acts[0].shape
    small = [prm[k] for k in ("gn_w", "gn_b", "r_k")]
    ins = [(a, "r") for a in acts] + [(s, "f") for s in small] + [(e, "f"), (et, "f")]
    return tilek(lambda *a: (_post(*a),), ins, [("r", d, MMD)], n_rows=n, tr=128, name=name)[0]


def rwkv_post_bwd(acts, prm, e, et, dm, name):
    n, d = acts[0].shape
    small = [prm[k] for k in ("gn_w", "gn_b", "r_k")]
    na = len(acts)

    def fn(*a):
        prim, ev, etv, dmv = a[:na + 3], a[na + 3], a[na + 4], a[na + 5]
        _, vjp = jax.vjp(lambda *z: _post(*z, ev, etv), *prim)
        return vjp(dmv.astype(F32))

    ins = [(x, "r") for x in acts] + [(s, "f") for s in small] + [(e, "f"), (et, "f"), (dm, "r")]
    outs = [("r", d, F32)] * na + [("acc", s.shape) for s in small]
    return tilek(fn, ins, outs, n_rows=n, tr=64, name=name)


def _head_sums(x, first_head):
    a = jnp.sum(jnp.where(first_head, x, 0.0), axis=1, keepdims=True)
    b = jnp.sum(jnp.where(first_head, 0.0, x), axis=1, keepdims=True)
    return jnp.where(first_head, a, b)


def _round1(x):
    return (x.astype(BF16), None) if MMD == BF16 else _split2(x)


def _split2(x):
    hi = x.astype(BF16)
    return hi, (x - hi.astype(F32)).astype(BF16)


def _spread(row, eye2):
    hi, lo = _split2(row)
    return eye2 * hi, eye2 * lo


def _ones_dot(tiles, ones_blk):
    dims = (((1,), (0,)), ((), ()))
    res = lax.dot_general(jnp.concatenate([t[0] for t in tiles], axis=0), ones_blk, dims, preferred_element_type=F32)
    out = [res[i * RWKV_HEAD:(i + 1) * RWKV_HEAD] for i in range(len(tiles))]
    two_term = [i for i, t in enumerate(tiles) if t[1] is not None]
    if two_term:
        low = lax.dot_general(jnp.concatenate([tiles[i][1] for i in two_term], axis=0), ones_blk, dims,
                              preferred_element_type=F32)
        for n, i in enumerate(two_term):
            out[i] = out[i] + low[n * RWKV_HEAD:(n + 1) * RWKV_HEAD]
    return out


def _scan_consts():
    lane = lax.broadcasted_iota(jnp.int32, (1, LANES), 1)
    rows = lax.broadcasted_iota(jnp.int32, (RWKV_HEAD, LANES), 0)
    cols = lax.broadcasted_iota(jnp.int32, (RWKV_HEAD, LANES), 1)
    eye2 = ((cols & (RWKV_HEAD - 1)) == rows).astype(BF16)
    r2 = lax.broadcasted_iota(jnp.int32, (LANES, LANES), 0)
    c2 = lax.broadcasted_iota(jnp.int32, (LANES, LANES), 1)
    ones_blk = ((r2 // RWKV_HEAD) == (c2 // RWKV_HEAD)).astype(BF16)
    return lane, lane < RWKV_HEAD, eye2, ones_blk


def _direct_copies(src_ref, dst_ref, send_sems, recv_sems, local_sem, scatter):
    _, me = _peer(0)
    out = [pltpu.make_async_copy(src_ref.at[me] if scatter else src_ref, dst_ref.at[me], local_sem)]
    for k in range(1, N_DEV):
        dev, idx = _peer(k)
        out.append(pltpu.make_async_remote_copy(src_ref=src_ref.at[idx] if scatter else src_ref, dst_ref=dst_ref.at[me],
                                                send_sem=send_sems.at[k - 1], recv_sem=recv_sems.at[k - 1],
                                                device_id=dev, device_id_type=MESH))
    return out


def _riding_exchange(src_ref, dst_ref, send_sems, recv_sems, local_sem, scatter, grid):
    copies = lambda: _direct_copies(src_ref, dst_ref, send_sems, recv_sems, local_sem, scatter)
    ids = [pl.program_id(a) for a in range(len(grid))]
    first = functools.reduce(jnp.logical_and, [i == 0 for i in ids])
    last = functools.reduce(jnp.logical_and, [i == n - 1 for i, n in zip(ids, grid)])

    def start():
        @pl.when(first)
        def _():
            for cp in copies():
                cp.start()

    def finish():
        @pl.when(last)
        def _():
            for cp in copies():
                cp.wait()

    return start, finish


_RIDE_SCRATCH = [pltpu.SemaphoreType.DMA((N_DEV - 1,)), pltpu.SemaphoreType.DMA((N_DEV - 1,)), pltpu.SemaphoreType.DMA]


def scan_forward(r, w, k, kn, b, v, ride, bl, t, t_real, d, name, pg, hch):
    npair, nst = d // LANES, t // hch
    grid = (bl, npair // pg, nst)

    def body(r_ref, w_ref, k_ref, kn_ref, b_ref, v_ref, ride_ref, y_ref, hist_ref, land_ref, s_ref, vb_ref, *sems):
        start, finish = _riding_exchange(ride_ref, land_ref, *sems, False, grid)
        start()
        _, first_head, eye2, ones_blk = _scan_consts()
        eye2f = eye2.astype(F32)
        diag = lambda tile: jnp.sum(tile * eye2f, axis=0, keepdims=True)

        @pl.when(pl.program_id(2) == 0)
        def _():
            s_ref[...] = jnp.zeros_like(s_ref)

        pair_cols = [slice(p * LANES, (p + 1) * LANES) for p in range(pg)]
        for p, tile in enumerate(_ones_dot([_spread(v_ref[0, :, cols], eye2) for cols in pair_cols], ones_blk)):
            vb_ref[p] = tile

        def step(ts, carry):
            prev, nxt = jnp.maximum(ts - 1, 0), jnp.minimum(ts + 1, hch - 1)
            states, tiles = [], []
            for p in range(pg):
                cols = slice(p * LANES, (p + 1) * LANES)
                s = s_ref[p]
                hist_ref[0, p, pl.ds(ts, 1)] = s[None]
                states.append(s)
                tiles.append(_round1(s * r_ref[prev, :, cols]))
                tiles.append(_spread(v_ref[nxt, :, cols], eye2))
            res = _ones_dot(tiles, ones_blk)
            for p in range(pg):
                cols = slice(p * LANES, (p + 1) * LANES)
                s = states[p]
                sa = _head_sums(s * kn_ref[ts, :, cols], first_head)
                s_ref[p] = s * w_ref[ts, :, cols] + sa * b_ref[ts, :, cols] + vb_ref[p] * k_ref[ts, :, cols]
            for p in range(pg):
                cols = slice(p * LANES, (p + 1) * LANES)
                y_ref[prev, :, cols] = diag(res[2 * p])
                vb_ref[p] = res[2 * p + 1]
            return carry

        real = pl.program_id(2) * hch < t_real
        lax.fori_loop(0, jnp.where(real, hch, 0), step, 0)
        last = _ones_dot([_round1(s_ref[p] * r_ref[hch - 1, :, cols]) for p, cols in enumerate(pair_cols)], ones_blk)
        for p, cols in enumerate(pair_cols):
            y_ref[hch - 1, :, cols] = diag(last[p])

        @pl.when(jnp.logical_not(real))
        def _():
            y_ref[...] = jnp.zeros_like(y_ref)
            hist_ref[...] = jnp.zeros_like(hist_ref)

        finish()

    row_spec = pl.BlockSpec((hch, 1, pg * LANES), lambda bb, g, c: (bb * nst + c, 0, g))
    hist_spec = pl.BlockSpec((1, pg, hch, RWKV_HEAD, LANES), lambda bb, g, c: (bb, g, c, 0, 0))
    hbm = pl.BlockSpec(memory_space=pl.ANY)
    rows3 = [a.reshape(bl * t, 1, d) for a in (r, w, k, kn, b, v)]
    y, hist, landed = pl.pallas_call(
        body, grid=grid, in_specs=[row_spec] * 6 + [hbm], out_specs=[row_spec, hist_spec, hbm],
        out_shape=[jax.ShapeDtypeStruct((bl * t, 1, d), F32), jax.ShapeDtypeStruct((bl, npair, t, RWKV_HEAD, LANES), F32),
                   jax.ShapeDtypeStruct((N_DEV,) + ride.shape, ride.dtype)],
        scratch_shapes=[pltpu.VMEM((pg, RWKV_HEAD, LANES), F32)] * 2 + _RIDE_SCRATCH,
        compiler_params=_params(("arbitrary", "arbitrary", "arbitrary")), name=name)(*rows3, ride)
    return y.reshape(bl * t, d), hist, landed


def scan_backward(r, w, k, kn, b, v, dy, hist, ride, bl, t, t_real, d, name, pg, hch):
    npair, nst = d // LANES, t // hch
    grid = (bl, npair // pg, nst)

    def body(r_ref, w_ref, k_ref, kn_ref, b_ref, v_ref, dy_ref, hist_ref, ride_ref,
             dr_ref, dw_ref, dk_ref, dkn_ref, db_ref, dv_ref, land_ref, ds_ref, cur_ref, *sems):
        start, finish = _riding_exchange(ride_ref, land_ref, *sems, True, grid)
        start()
        _, first_head, eye2, ones_blk = _scan_consts()
        eye2f = eye2.astype(F32)
        colsum = lambda x: jnp.sum(x, axis=0, keepdims=True)

        @pl.when(pl.program_id(2) == 0)
        def _():
            ds_ref[...] = jnp.zeros_like(ds_ref)

        tiles = []
        for p in range(pg):
            cols = slice(p * LANES, (p + 1) * LANES)
            tiles += [_spread(v_ref[hch - 1, :, cols], eye2), _spread(dy_ref[hch - 1, :, cols], eye2),
                      _split2(hist_ref[0, p, hch - 1] * kn_ref[hch - 1, :, cols])]
        first = _ones_dot(tiles, ones_blk)
        for p in range(pg):
            cols = slice(p * LANES, (p + 1) * LANES)
            row = lambda ref: ref[hch - 1, :, cols]
            s_prev = hist_ref[0, p, hch - 1]
            vb, dyb, sa = first[3 * p], first[3 * p + 1], first[3 * p + 2]
            cur_ref[0, p], cur_ref[1, p] = vb, sa
            dr_ref[hch - 1, :, cols] = colsum((s_prev * row(w_ref) + sa * row(b_ref) + vb * row(k_ref)) * dyb)
            ds_ref[p] += dyb * row(r_ref)

        def step(it, carry):
            ts = hch - 1 - it
            prev = jnp.maximum(ts - 1, 0)
            has_prev = ts > 0
            grads, tiles = [], []
            for p in range(pg):
                cols = slice(p * LANES, (p + 1) * LANES)
                ds = ds_ref[p]
                grads.append(ds)
                tiles.append(_spread(v_ref[prev, :, cols], eye2))
                tiles.append(_spread(dy_ref[prev, :, cols], eye2))
                tiles.append(_split2(hist_ref[0, p, pl.ds(prev, 1)][0] * kn_ref[prev, :, cols]))
                tiles.append(_round1(ds * k_ref[ts, :, cols]))
            res = _ones_dot(tiles, ones_blk)
            for p in range(pg):
                cols = slice(p * LANES, (p + 1) * LANES)
                row = lambda ref: ref[ts, :, cols]
                ds = grads[p]
                w_, kn_, b_ = row(w_ref), row(kn_ref), row(b_ref)
                dsa = _head_sums(ds * b_, first_head)
                s_prev = hist_ref[0, p, pl.ds(ts, 1)][0]
                vb, sa, dyb_prev = cur_ref[0, p], cur_ref[1, p], res[4 * p + 1]
                dk_ref[ts, :, cols] = colsum(ds * vb)
                db_ref[ts, :, cols] = colsum(ds * sa)
                dw_ref[ts, :, cols] = colsum(ds * s_prev)
                dkn_ref[ts, :, cols] = colsum(s_prev * dsa)
                dv_ref[ts, :, cols] = colsum(res[4 * p + 3] * eye2f)
                dr_ref[prev, :, cols] = jnp.where(has_prev, colsum(s_prev * dyb_prev), dr_ref[prev, :, cols])
                ds_ref[p] = ds * w_ + dsa * kn_ + jnp.where(has_prev, dyb_prev, 0.0) * r_ref[prev, :, cols]
            for p in range(pg):
                cur_ref[0, p] = res[4 * p]
                cur_ref[1, p] = res[4 * p + 2]
            return carry

        real = (nst - 1 - pl.program_id(2)) * hch < t_real
        lax.fori_loop(0, jnp.where(real, hch, 0), step, 0)

        @pl.when(jnp.logical_not(real))
        def _():
            for ref in (dr_ref, dw_ref, dk_ref, dkn_ref, db_ref, dv_ref):
                ref[...] = jnp.zeros_like(ref)

        finish()

    row_spec = pl.BlockSpec((hch, 1, pg * LANES), lambda bb, g, c: (bb * nst + nst - 1 - c, 0, g))
    hist_spec = pl.BlockSpec((1, pg, hch, RWKV_HEAD, LANES), lambda bb, g, c: (bb, g, nst - 1 - c, 0, 0))
    hbm = pl.BlockSpec(memory_space=pl.ANY)
    row_shape = jax.ShapeDtypeStruct((bl * t, 1, d), F32)
    rows3 = [a.reshape(bl * t, 1, d) for a in (r, w, k, kn, b, v, dy)]
    outs = pl.pallas_call(
        body, grid=grid, in_specs=[row_spec] * 7 + [hist_spec, hbm], out_specs=[row_spec] * 6 + [hbm],
        out_shape=[row_shape] * 6 + [jax.ShapeDtypeStruct(ride.shape, ride.dtype)],
        scratch_shapes=[pltpu.VMEM((pg, RWKV_HEAD, LANES), F32), pltpu.VMEM((2, pg, RWKV_HEAD, LANES), F32)] + _RIDE_SCRATCH,
        compiler_params=_params(("arbitrary", "arbitrary", "arbitrary")), name=name)(*rows3, hist, ride)
    return [o.reshape(bl * t, d) for o in outs[:6]] + [outs[6]]


def _mla_norms(pm, gq, gkv):
    ql = gq.shape[1]
    kvl = gkv.shape[1]
    return _rms(pm[:, :ql], gq), _rms(pm[:, ql:ql + kvl], gkv)


def mla_prep_fwd(pm, gq, gkv, name):
    n = pm.shape[0]
    return tilek(_mla_norms, [(pm, "r"), (gq, "f"), (gkv, "f")],
                 [("r", gq.shape[1], MMD), ("r", gkv.shape[1], MMD)], n_rows=n, tr=256, name=name)


def mla_prep_bwd(pm, gq, gkv, dcq, dckv, dkpe, name):
    n, wm = pm.shape
    ql, kvl = gq.shape[1], gkv.shape[1]

    def fn(pmv, gqv, gkvv, d1, d2, d3):
        _, vjp1 = jax.vjp(_rms, pmv[:, :ql], gqv)
        _, vjp2 = jax.vjp(_rms, pmv[:, ql:ql + kvl], gkvv)
        dcq_in, dgq = vjp1(d1)
        dckv_in, dgkv = vjp2(d2)
        return jnp.concatenate([dcq_in, dckv_in, d3], axis=1), dgq, dgkv

    return tilek(fn, [(pm, "r"), (gq, "f"), (gkv, "f"), (dcq, "r"), (dckv, "r"), (dkpe, "r")],
                 [("r", wm, F32), ("acc", gq.shape), ("acc", gkv.shape)], n_rows=n, tr=128, name=name)


def _rope(x, c, s, first):
    sw = jnp.where(first, pltpu.roll(x, LANES - ROPE_DIM // 2, 1), pltpu.roll(x, ROPE_DIM // 2, 1))
    return x * c + sw * s


def _unrope(d, c, s, first):
    z = d * s
    sw = jnp.where(first, pltpu.roll(z, LANES - ROPE_DIM // 2, 1), pltpu.roll(z, ROPE_DIM // 2, 1))
    return d * c + sw


def _causal_segments(n_tiles, parts=17):
    bounds = sorted({round(n_tiles * s / parts) for s in range(parts + 1)})
    return list(zip(bounds[:-1], bounds[1:]))


def attn_fwd(q, kv, pm, ct, st, bl, t, hm, name):
    n = q.shape[0]
    tq = LANES
    scale = QK_DIM ** -0.5
    kpe_blk = pm.shape[1] // LANES - 1

    def body(qn_ref, qpe_ref, kn_ref, v_ref, kpe_ref, ct_ref, st_ref, o_ref, lse_ref, kp_s, kn_s, v_s):
        h = pl.program_id(1)
        lane = lax.broadcasted_iota(jnp.int32, (1, LANES), 1)
        first = (lane & (ROPE_DIM - 1)) < ROPE_DIM // 2
        kp = _rope(kpe_ref[...], ct_ref[...], st_ref[...], first)
        kp_s[...] = jnp.where(h % 2 == 0, kp, pltpu.roll(kp, ROPE_DIM, 1)).astype(MMD)
        kn_s[...] = kn_ref[...].astype(MMD)
        v_s[...] = v_ref[...].astype(MMD)
        def segment(lo, hi):
            ext = hi * tq
            kpos = lax.broadcasted_iota(jnp.int32, (1, ext), 1)

            def qtile(i, carry):
                rows = pl.ds(pl.multiple_of(i * tq, tq), tq)
                q2 = _rope(qpe_ref[rows, :], ct_ref[rows, :], st_ref[rows, :], first)
                s = (_mm(qn_ref[rows, :], kn_s[:ext, :], ((1,), (1,))) + _mm(q2, kp_s[:ext, :], ((1,), (1,)))) * scale
                qpos = i * tq + lax.broadcasted_iota(jnp.int32, (tq, 1), 0)
                s = jnp.where(kpos <= qpos, s, -1e30)
                m = jnp.max(s, axis=1, keepdims=True)
                p = jnp.exp(s - m)
                l = jnp.sum(p, axis=1, keepdims=True)
                o_ref[rows, :] = _mm(p, v_s[:ext, :]) / l
                lse_ref[0, 0, rows, :] = m + jnp.log(l)
                return carry

            lax.fori_loop(lo, hi, qtile, 0)

        for lo, hi in _causal_segments(t // tq):
            segment(lo, hi)

    blk = lambda f: pl.BlockSpec((t, LANES), f)
    return pl.pallas_call(
        body, grid=(bl, hm),
        in_specs=[blk(lambda b, h: (b, h)), blk(lambda b, h: (b, hm + h // 2)), blk(lambda b, h: (b, h)),
                  blk(lambda b, h: (b, hm + h)), blk(lambda b, h: (b, kpe_blk)), blk(lambda b, h: (0, 0)), blk(lambda b, h: (0, 0))],
        out_specs=[blk(lambda b, h: (b, h)), pl.BlockSpec((1, 1, t, 1), lambda b, h: (b, h, 0, 0))],
        out_shape=[jax.ShapeDtypeStruct((n, hm * LANES), F32), jax.ShapeDtypeStruct((bl, hm, t, 1), F32)],
        scratch_shapes=[pltpu.VMEM((t, LANES), MMD)] * 3,
        compiler_params=_params(("parallel", "arbitrary")), name=name)(q, q, kv, kv, pm, ct, st)


def attn_bwd(q, kv, pm, o, do, lse, ct, st, bl, t, hm, name):
    n = q.shape[0]
    tq = LANES
    scale = QK_DIM ** -0.5
    kpe_blk = pm.shape[1] // LANES - 1

    def body(qn_ref, qpe_ref, kn_ref, v_ref, kpe_ref, o_ref, do_ref, lse_ref, ct_ref, st_ref,
             dqn_ref, dqpe_ref, dkn_ref, dv_ref, dkpe_ref, kp_s, kn_s, v_s, dkn_s, dkp_s, dv_s):
        h = pl.program_id(1)
        lane = lax.broadcasted_iota(jnp.int32, (1, LANES), 1)
        first = (lane & (ROPE_DIM - 1)) < ROPE_DIM // 2
        mine = (lane // ROPE_DIM) == (h % 2)
        kp = _rope(kpe_ref[...], ct_ref[...], st_ref[...], first)
        kp_s[...] = jnp.where(h % 2 == 0, kp, pltpu.roll(kp, ROPE_DIM, 1)).astype(MMD)
        kn_s[...] = kn_ref[...].astype(MMD)
        v_s[...] = v_ref[...].astype(MMD)
        dkn_s[...] = jnp.zeros_like(dkn_s)
        dkp_s[...] = jnp.zeros_like(dkp_s)
        dv_s[...] = jnp.zeros_like(dv_s)
        @pl.when(h % 2 == 0)
        def _():
            dqpe_ref[...] = jnp.zeros_like(dqpe_ref)

        @pl.when(h == 0)
        def _():
            dkpe_ref[...] = jnp.zeros_like(dkpe_ref)

        def segment(lo, hi):
            ext = hi * tq
            kpos = lax.broadcasted_iota(jnp.int32, (1, ext), 1)

            def qtile(i, carry):
                rows = pl.ds(pl.multiple_of(i * tq, tq), tq)
                c_i, s_i = ct_ref[rows, :], st_ref[rows, :]
                q1 = qn_ref[rows, :].astype(MMD)
                q2 = _rope(qpe_ref[rows, :], c_i, s_i, first).astype(MMD)
                s = (_mm(q1, kn_s[:ext, :], ((1,), (1,))) + _mm(q2, kp_s[:ext, :], ((1,), (1,)))) * scale
                qpos = i * tq + lax.broadcasted_iota(jnp.int32, (tq, 1), 0)
                p = jnp.where(kpos <= qpos, jnp.exp(s - lse_ref[0, 0, rows, :]), 0.0)
                do_i = do_ref[rows, :]
                delta = jnp.sum(do_i * o_ref[rows, :], axis=1, keepdims=True)
                dp = _mm(do_i, v_s[:ext, :], ((1,), (1,)))
                ds = (p * (dp - delta) * scale).astype(MMD)
                dqn_ref[rows, :] = _mm(ds, kn_s[:ext, :])
                dq2 = jnp.where(mine, _mm(ds, kp_s[:ext, :]), 0.0)
                dqpe_ref[rows, :] += _unrope(dq2, c_i, s_i, first)
                dkn_s[:ext, :] += _mm(ds, q1, ((0,), (0,)))
                dkp_s[:ext, :] += _mm(ds, q2, ((0,), (0,)))
                dv_s[:ext, :] += _mm(p, do_i, ((0,), (0,)))
                return carry

            lax.fori_loop(lo, hi, qtile, 0)

        for lo, hi in _causal_segments(t // tq):
            segment(lo, hi)
        dkn_ref[...] = dkn_s[...]
        dv_ref[...] = dv_s[...]
        dkp = jnp.where(mine, dkp_s[...], 0.0)
        dkp = jnp.where(h % 2 == 0, dkp, pltpu.roll(dkp, ROPE_DIM, 1))
        dkpe_ref[...] += _unrope(dkp, ct_ref[...], st_ref[...], first)

    blk = lambda f: pl.BlockSpec((t, LANES), f)
    hd = lambda b, h: (b, h)
    shp = lambda wd: jax.ShapeDtypeStruct((n, wd), F32)
    return pl.pallas_call(
        body, grid=(bl, hm),
        in_specs=[blk(hd), blk(lambda b, h: (b, hm + h // 2)), blk(hd), blk(lambda b, h: (b, hm + h)),
                  blk(lambda b, h: (b, kpe_blk)), blk(hd), blk(hd), pl.BlockSpec((1, 1, t, 1), lambda b, h: (b, h, 0, 0)),
                  blk(lambda b, h: (0, 0)), blk(lambda b, h: (0, 0))],
        out_specs=[blk(hd), blk(lambda b, h: (b, h // 2)), blk(hd), blk(hd), blk(lambda b, h: (b, 0))],
        out_shape=[shp(hm * LANES), shp(hm * ROPE_DIM), shp(hm * LANES), shp(hm * LANES), shp(LANES)],
        scratch_shapes=[pltpu.VMEM((t, LANES), MMD)] * 3 + [pltpu.VMEM((t, LANES), F32)] * 3,
        compiler_params=_params(("parallel", "arbitrary")), name=name)(q, q, kv, kv, pm, o, do, lse, ct, st)


def _peer(k):
    mx, my, mc = lax.axis_index("x"), lax.axis_index("y"), lax.axis_index("c")
    px = 1 - mx if k & 4 else mx
    py = 1 - my if k & 2 else my
    pc = 1 - mc if k & 1 else mc
    return (px, py, pc), 4 * px + 2 * py + pc


def _chips():
    mx, my, mc = lax.axis_index("x"), lax.axis_index("y"), lax.axis_index("c")
    return (mx, my, mc), (mx, my, 1 - mc), [(1 - mx, my), (mx, 1 - my), (1 - mx, 1 - my)]


def _riders(refs, gathers, scatters):
    n = gathers + scatters
    ins, outs, sems = refs[:n], refs[n:2 * n], refs[2 * n:]
    copies = []
    for i in range(n):
        copies += _direct_copies(ins[i], outs[i], *sems[3 * i:3 * i + 3], i >= gathers)
    return copies


def _rider_shapes(gathers, scatters):
    shapes = [jax.ShapeDtypeStruct((N_DEV,) + a.shape, a.dtype) for a in gathers]
    return shapes + [jax.ShapeDtypeStruct(a.shape, a.dtype) for a in scatters]


def all_gather_two_level(x, name, also=()):
    na = len(also)

    def body(*refs):
        x_ref, o_ref = refs[0], refs[1 + na]
        send_sems, recv_sems, local_sem = refs[2 + 2 * na:5 + 2 * na]
        riders = _riders(refs[1:1 + na] + refs[2 + na:2 + 2 * na] + refs[5 + 2 * na:], na, 0)
        for cp in riders:
            cp.start()
        me, sibling, chips = _chips()
        blk = lambda px, py, pc: o_ref.at[4 * px + 2 * py + pc]

        def copy(k, block, to, src=None):
            return pltpu.make_async_remote_copy(src_ref=blk(*block) if src is None else src, dst_ref=blk(*block),
                                                send_sem=send_sems.at[k], recv_sem=recv_sems.at[k], device_id=to,
                                                device_id_type=MESH)

        mine = pltpu.make_async_copy(x_ref, blk(*me), local_sem)
        mine.start()
        first = [copy(0, me, sibling, src=x_ref)] + [copy(1 + j, me, (*chip, me[2]), src=x_ref) for j, chip in enumerate(chips)]
        for cp in first:
            cp.start()
        passed = [copy(4 + j, (*chip, me[2]), sibling) for j, chip in enumerate(chips)]
        for j, chip in enumerate(chips):
            copy(1 + j, (*chip, me[2]), me).wait_recv()
            passed[j].start()
        copy(0, sibling, me).wait_recv()
        for j, chip in enumerate(chips):
            copy(4 + j, (*chip, 1 - me[2]), me).wait_recv()
        for cp in first + passed:
            cp.wait_send()
        mine.wait()
        for cp in riders:
            cp.wait()

    hbm = pl.BlockSpec(memory_space=pl.ANY)
    return pl.pallas_call(
        body, in_specs=[hbm] * (1 + na), out_specs=[hbm] * (1 + na),
        out_shape=[jax.ShapeDtypeStruct((N_DEV,) + x.shape, x.dtype)] + _rider_shapes(also, ()),
        scratch_shapes=_RIDE_SCRATCH * (1 + na), name=name)(x, *also)


def exchange_sibling(x, name):
    def body(x_ref, o_ref, send_sems, recv_sems):
        me, sibling, _ = _chips()
        copies = []
        for q in range(N_DEV // 2):
            cp = pltpu.make_async_remote_copy(src_ref=x_ref.at[2 * q + 1 - me[2]], dst_ref=o_ref.at[q], send_sem=send_sems.at[q],
                                              recv_sem=recv_sems.at[q], device_id=sibling, device_id_type=MESH)
            cp.start()
            copies.append(cp)
        for cp in copies:
            cp.wait()

    return pl.pallas_call(
        body, in_specs=[pl.BlockSpec(memory_space=pl.ANY)], out_specs=pl.BlockSpec(memory_space=pl.ANY),
        out_shape=jax.ShapeDtypeStruct((N_DEV // 2,) + x.shape[1:], x.dtype),
        scratch_shapes=[pltpu.SemaphoreType.DMA((N_DEV // 2,)), pltpu.SemaphoreType.DMA((N_DEV // 2,))], name=name)(x)


def exchange_chips(x, name, gathers=(), scatters=()):
    na = len(gathers) + len(scatters)

    def body(*refs):
        x_ref, o_ref = refs[0], refs[1 + na]
        send_sems, recv_sems, local_sem = refs[2 + 2 * na:5 + 2 * na]
        riders = _riders(refs[1:1 + na] + refs[2 + na:2 + 2 * na] + refs[5 + 2 * na:], len(gathers), len(scatters))
        for cp in riders:
            cp.start()
        me, _, chips = _chips()
        here = 2 * me[0] + me[1]
        local = pltpu.make_async_copy(x_ref.at[here], o_ref.at[here], local_sem)
        local.start()
        copies = []
        for j, (px, py) in enumerate(chips):
            cp = pltpu.make_async_remote_copy(src_ref=x_ref.at[2 * px + py], dst_ref=o_ref.at[here], send_sem=send_sems.at[j],
                                              recv_sem=recv_sems.at[j], device_id=(px, py, me[2]), device_id_type=MESH)
            cp.start()
            copies.append(cp)
        for cp in copies:
            cp.wait()
        local.wait()
        for cp in riders:
            cp.wait()

    hbm = pl.BlockSpec(memory_space=pl.ANY)
    return pl.pallas_call(
        body, in_specs=[hbm] * (1 + na), out_specs=[hbm] * (1 + na),
        out_shape=[jax.ShapeDtypeStruct(x.shape, x.dtype)] + _rider_shapes(gathers, scatters),
        scratch_shapes=[pltpu.SemaphoreType.DMA((3,)), pltpu.SemaphoreType.DMA((3,)), pltpu.SemaphoreType.DMA] + _RIDE_SCRATCH * na,
        name=name)(x, *gathers, *scatters)


def add_blocks(a, b, name):
    q, r, c = a.shape
    tr = _tile(r, max(16, (2 << 20) // (c * a.dtype.itemsize)), 16)
    spec = pl.BlockSpec((1, tr, c), lambda i, j: (i, j, 0))

    def body(a_ref, b_ref, o_ref):
        o_ref[...] = (a_ref[...].astype(F32) + b_ref[...].astype(F32)).astype(o_ref.dtype)

    return pl.pallas_call(
        body, grid=(q, r // tr), in_specs=[spec, spec], out_specs=spec, out_shape=jax.ShapeDtypeStruct(a.shape, a.dtype),
        compiler_params=_params(("parallel", "parallel")), name=name)(a, b)


def reduce_scatter_two_level(x, tag, gathers=(), scatters=()):
    q = N_DEV // 2
    from_sibling = exchange_sibling(x, f"{tag}_sibling")
    mine = lax.dynamic_index_in_dim(x.reshape((q, 2) + x.shape[1:]), lax.axis_index("c"), axis=1, keepdims=False)
    chip_sums = add_blocks(mine, from_sibling, f"{tag}_pair_sum")
    from_chips, *small = exchange_chips(chip_sums, f"{tag}_chips", gathers, scatters)
    return (sum_blocks(from_chips, f"{tag}_sum"), *small)


def sum_blocks(x, name):
    nb, r, c = x.shape
    tr = _tile(r, max(16, (4 << 20) // (nb * c * x.dtype.itemsize)), 16)

    def body(x_ref, o_ref):
        acc = x_ref[0].astype(F32)
        for i in range(1, nb):
            acc = acc + x_ref[i].astype(F32)
        o_ref[...] = acc

    return pl.pallas_call(
        body, grid=(r // tr,), in_specs=[pl.BlockSpec((nb, tr, c), lambda i: (0, i, 0))],
        out_specs=pl.BlockSpec((tr, c), lambda i: (i, 0)), out_shape=jax.ShapeDtypeStruct((r, c), F32),
        compiler_params=_params(("parallel",)), name=name)(x)


def _adamw(w, g, m, v):
    m = ADAM_B1 * m + (1.0 - ADAM_B1) * g
    v = ADAM_B2 * v + (1.0 - ADAM_B2) * jnp.square(g)
    m_hat = m / (1.0 - ADAM_B1 ** ADAM_STEP)
    v_hat = v / (1.0 - ADAM_B2 ** ADAM_STEP)
    delta = -ADAM_LR * (m_hat / (jnp.sqrt(v_hat) + ADAM_EPS) + ADAM_WD * w)
    return delta, m, v


def adamw(w, g, m, v, name):
    r, c = w.shape
    tr = _tile(r, 256, 8)
    spec = pl.BlockSpec((tr, c), lambda i: (i, 0))

    def body(w_ref, g_ref, m_ref, v_ref, d_ref, nm_ref, nv_ref):
        d_ref[...], nm_ref[...], nv_ref[...] = _adamw(w_ref[...], g_ref[...], m_ref[...], v_ref[...])

    return pl.pallas_call(
        body, grid=(r // tr,), in_specs=[spec] * 4, out_specs=[spec] * 3,
        out_shape=[jax.ShapeDtypeStruct((r, c), F32)] * 3, compiler_params=_params(("parallel",)), name=name)(w, g, m, v)


def batch_sum_rows(dh, bl, t, rows, name):
    d = dh.shape[1]

    def body(x_ref, o_ref):
        @pl.when(pl.program_id(0) == 0)
        def _():
            o_ref[...] = jnp.zeros_like(o_ref)

        o_ref[...] += x_ref[...]

    return pl.pallas_call(
        body, grid=(bl,), in_specs=[pl.BlockSpec((rows, d), lambda b: (b * (t // rows), 0))],
        out_specs=pl.BlockSpec((rows, d), lambda b: (0, 0)), out_shape=jax.ShapeDtypeStruct((rows, d), F32),
        compiler_params=_params(("arbitrary",)), name=name)(dh)


class Dims:
    def __init__(self, x, w_up, g_up, q_norm, kv_norm, d_ff):
        self.bl, self.seq, self.d = x.shape
        self.n_meta = 16
        self.t_real = self.n_meta + self.seq
        self.t = -(-self.t_real // LANES) * LANES
        self.n = self.bl * self.t
        self.f = d_ff
        self.wl, self.gl = w_up.shape[-2], g_up.shape[-2]
        self.ql, self.kvl = q_norm.shape[-1], kv_norm.shape[-1]
        self.hm = self.d // V_DIM
        self.in_cols = 5 * self.d + 2 * self.wl + self.gl + self.ql + self.kvl + ROPE_DIM


def _pad_cols(a, width):
    return jnp.pad(a, ((0, 0), (0, width - a.shape[1])))


def _pad_rows(a, rows):
    return jnp.pad(a, ((0, rows - a.shape[0]), (0, 0)))


def split_in(a, dm, axis=1):
    d, wl, gl, ql, kvl = dm.d, dm.wl, dm.gl, dm.ql, dm.kvl
    size = a.shape[axis]
    cut = lambda lo, hi: lax.slice_in_dim(a, min(lo, size), min(hi, size), axis=axis)

    def pad(p, width):
        cfg = [(0, 0)] * a.ndim
        cfg[axis] = (0, width - p.shape[axis])
        return jnp.pad(p, cfg)

    o = 3 * d
    lora = jnp.concatenate([pad(cut(o, o + wl), LANES), pad(cut(o + wl, o + 2 * wl), LANES),
                            cut(o + 2 * wl, o + 2 * wl + gl)], axis=axis)
    o += 2 * wl + gl
    mla = pad(cut(o, o + ql + kvl + ROPE_DIM), ql + kvl + LANES)
    o += ql + kvl + ROPE_DIM
    return dict(r=cut(0, d), k=cut(d, 2 * d), v=cut(2 * d, 3 * d), l=lora, m=mla, ga=cut(o, o + d), gb=cut(o + d, o + 2 * d))


def merge_in(g, dm, axis=1):
    wl, gl, ql, kvl = dm.wl, dm.gl, dm.ql, dm.kvl
    cut = lambda p, lo, hi: lax.slice_in_dim(p, lo, hi, axis=axis)
    l, m = g["l"], g["m"]
    return jnp.concatenate([g["r"], g["k"], g["v"], cut(l, 0, wl), cut(l, LANES, LANES + wl), cut(l, 2 * LANES, 2 * LANES + gl),
                            cut(m, 0, ql + kvl + ROPE_DIM), g["ga"], g["gb"]], axis=axis)


def split_uq(w, dm):
    w3 = w.reshape(w.shape[0], dm.hm, QK_DIM)
    return jnp.concatenate([w3[:, :, :NOPE_DIM].reshape(w.shape[0], -1), w3[:, :, NOPE_DIM:].reshape(w.shape[0], -1)], axis=1)


def merge_uq(gn, gp, dm):
    r = gn.shape[0]
    return jnp.concatenate([gn.reshape(r, dm.hm, NOPE_DIM), gp.reshape(r, dm.hm, ROPE_DIM)], axis=2).reshape(r, -1)


def split_ukv(w, dm):
    w3 = w.reshape(w.shape[0], dm.hm, NOPE_DIM + V_DIM)
    return jnp.concatenate([w3[:, :, :NOPE_DIM].reshape(w.shape[0], -1), w3[:, :, NOPE_DIM:].reshape(w.shape[0], -1)], axis=1)


def merge_ukv(gk, gv, dm):
    r = gk.shape[0]
    return jnp.concatenate([gk.reshape(r, dm.hm, NOPE_DIM), gv.reshape(r, dm.hm, V_DIM)], axis=2).reshape(r, -1)


def head_matrices(d):
    heads = d // RWKV_HEAD
    e = (np.arange(d)[:, None] // RWKV_HEAD == np.arange(LANES)[None, :]) & (np.arange(LANES)[None, :] < heads)
    return jnp.asarray(e, BF16), jnp.asarray(e.T, BF16)


def rope_tables(t):
    pos = jnp.arange(t, dtype=F32)
    inv_freq = 1.0 / (ROPE_THETA ** (jnp.arange(0, ROPE_DIM, 2, dtype=F32) / ROPE_DIM))
    ang = pos[:, None] * inv_freq[None, :]
    cos, sin = jnp.cos(ang), jnp.sin(ang)
    return jnp.tile(jnp.concatenate([cos, cos], axis=1), (1, 2)), jnp.tile(jnp.concatenate([-sin, sin], axis=1), (1, 2))


def local_step(dm, x, loss_target, meta, wt, late_shards, late_rows, sp):
    bl, t, n, d, hm = dm.bl, dm.t, dm.n, dm.d, dm.hm
    e, et = head_matrices(d)
    ct, st = rope_tables(t)
    padz = jnp.zeros((bl, t - dm.t_real, d), F32)
    h0 = jnp.concatenate([jnp.broadcast_to(meta[None], (bl, dm.n_meta, d)), x, padz], axis=1).reshape(n, d)
    tgt = jnp.concatenate([jnp.zeros((bl, dm.n_meta, d), F32), loss_target, padz], axis=1).reshape(n, d)
    tpos = jnp.arange(t)
    mask = jnp.tile(((tpos >= dm.n_meta) & (tpos < dm.t_real)).astype(F32), bl).reshape(n, 1)

    win = split_in(wt["w_in"], dm, axis=0)
    mu = split_in(sp["tm_mu"], dm)
    wq, wkv = split_uq(wt["w_uq"], dm), split_ukv(wt["w_ukv"], dm)
    prm = dict(w0=sp["w0"], a0=sp["a0"], k_k=sp["k_k"], k_a=sp["k_a"], gn_w=sp["gn_w"], gn_b=sp["gn_b"], r_k=sp["r_k"],
               w_up=_pad_rows(wt["w_up"], LANES).astype(F32), a_up=_pad_rows(wt["a_up"], LANES).astype(F32),
               g_up=wt["g_up"].astype(F32))

    h1, ffn1 = ffn_forward(h0, sp["ffn1_norm"], wt["ffn1_w_gate"], wt["ffn1_w_up"], wt["ffn1_w_down"], "ffn1")
    u = rms_fwd(h1, sp["mix_norm"], "mix_rms")
    proj = {key: matmul([(u, win[key])], "nt", name=f"proj_{key}") for key in win}
    sh = {key: lerp_fwd(proj[key], mu[key], bl, t, f"shift_{key}") for key in ("r", "k", "v", "l")}
    decay, kmod, kneg, bvec, gate = rwkv_prep_fwd(sh["k"], sh["l"], prm, e, et, "rwkv_prep")
    pairs = min(SCAN_PAIRS, d // LANES)
    y, hist, late_all = scan_forward(sh["r"], decay, kmod, kneg, bvec, sh["v"], late_shards, bl, t, dm.t_real, d, "wkv_scan",
                                     min(SCAN_FWD_PAIRS, d // LANES), SCAN_FWD_STEPS)
    wt = dict(wt, **{key: late_all[:, lo:hi].reshape(-1, d) for key, lo, hi in zip(LATE, late_rows[:-1], late_rows[1:])})
    cqn, ckvn = mla_prep_fwd(proj["m"], sp["q_norm"], sp["kv_norm"], "mla_norms")
    q = matmul([(cqn, wq)], "nn", name="mla_q")
    kv = matmul([(ckvn, wkv)], "nn", name="mla_kv")
    o, lse = attn_fwd(q, kv, proj["m"], ct, st, bl, t, hm, "mla_attn")
    post_in = [y, sh["r"], kmod, sh["v"], gate, proj["ga"], proj["gb"], o]
    mix = rwkv_post_fwd(post_in, prm, e, et, "mix_gate")
    h2 = matmul([(mix, wt["w_out"])], "nn", res=h1, name="out_proj")
    h3, ffn2 = ffn_forward(h2, sp["ffn2_norm"], wt["ffn2_w_gate"], wt["ffn2_w_up"], wt["ffn2_w_down"], "ffn2")
    dh3, dh3_lp, d_final, loss = loss_head(h3, tgt, mask, sp["final_norm"], "loss_head")

    gw, gs = {}, {"final_norm": d_final}
    dh2, dh2_lp, gs["ffn2_norm"], gw["ffn2_w_gate"], gw["ffn2_w_up"], gw["ffn2_w_down"] = ffn_backward(
        dh3, dh3_lp, h2, sp["ffn2_norm"], wt["ffn2_w_gate"], wt["ffn2_w_up"], wt["ffn2_w_down"], ffn2, "ffn2")
    dmix = matmul([(dh2_lp, wt["w_out"])], "nt", name="out_proj_dx")
    gw["w_out"] = matmul([(mix, dh2_lp)], "tn", out_dtype=MMD, name="out_proj_dw")
    late_grads = jnp.concatenate([gw.pop(key).reshape(N_DEV, hi - lo, d) for key, lo, hi in
                                  zip(LATE, late_rows[:-1], late_rows[1:])], axis=1).astype(MMD)
    (dy, dr_p, dkm_p, dv_p, dgate, dpga, dpgb, do, gs["gn_w"], gs["gn_b"], gs["r_k"]) = rwkv_post_bwd(
        post_in, prm, e, et, dmix, "mix_gate_bwd")
    dqn, dqpe, dkn, dv_att, dkpe = attn_bwd(q, kv, proj["m"], o, do, lse, ct, st, bl, t, hm, "mla_attn_bwd")
    nq = hm * NOPE_DIM
    dcqn = matmul([(dqn, wq[:, :nq])], "nt", name="mla_q_dx1")
    dcqn = matmul([(dqpe, wq[:, nq:])], "nt", res=dcqn, name="mla_q_dx2")
    gw["w_uq"] = merge_uq(matmul([(cqn, dqn)], "tn", name="mla_q_dw1"), matmul([(cqn, dqpe)], "tn", name="mla_q_dw2"), dm)
    dckvn = matmul([(dkn, wkv[:, :nq]), (dv_att, wkv[:, nq:])], "nt", name="mla_kv_dx", tk=1024)
    gw["w_ukv"] = merge_ukv(matmul([(ckvn, dkn)], "tn", name="mla_kv_dw1"), matmul([(ckvn, dv_att)], "tn", name="mla_kv_dw2"), dm)
    dproj = {"ga": dpga, "gb": dpgb}
    dproj["m"], gs["q_norm"], gs["kv_norm"] = mla_prep_bwd(proj["m"], sp["q_norm"], sp["kv_norm"], dcqn, dckvn, dkpe, "mla_norms_bwd")
    dr_s, ddecay, dk_s, dkneg, dbvec, dv_s, late_recv = scan_backward(
        sh["r"], decay, kmod, kneg, bvec, sh["v"], dy, hist, late_grads, bl, t, dm.t_real, d, "wkv_scan_bwd", pairs,
        SCAN_BWD_STEPS)
    late_sum = sum_blocks(late_recv, "sum_late")
    (dsh_k, dsh_l, gs["w0"], gs["a0"], gs["k_k"], gs["k_a"], g_wup, g_aup, gw["g_up"]) = rwkv_prep_bwd(
        sh["k"], sh["l"], prm, e, et, [ddecay, dk_s, dkm_p, dkneg, dbvec, dgate], "rwkv_prep_bwd")
    gw["w_up"], gw["a_up"] = g_wup[:dm.wl], g_aup[:dm.wl]
    dmu = {}
    for key, cts in (("r", [dr_s, dr_p]), ("k", [dsh_k]), ("v", [dv_s, dv_p]), ("l", [dsh_l])):
        dproj[key], dmu[key] = lerp_bwd(proj[key], mu[key], cts, bl, t, f"shift_{key}_bwd")
    zero_m = jnp.zeros((1, proj["m"].shape[1]), F32)
    gs["tm_mu"] = merge_in(dict(dmu, m=zero_m, ga=zero_m[:, :0], gb=zero_m[:, :0]), dm)[:, :3 * d + 2 * dm.wl + dm.gl]
    wide = ("r", "k", "v", "ga", "gb")
    du = matmul([(dproj[key], win[key]) for key in wide], "nn", name="proj_dx", tn=512, tk=512)
    du = matmul([(dproj["l"], win["l"])], "nn", res=du, name="proj_dx_l")
    du = matmul([(dproj["m"], win["m"])], "nn", res=du, name="proj_dx_m")
    gw["w_in"] = merge_in({key: matmul([(dproj[key], u)], "tn", out_dtype=MMD, name=f"proj_dw_{key}") for key in win},
                          dm, axis=0)
    dh1, dh1_lp, gs["mix_norm"] = rms_bwd(h1, sp["mix_norm"], du, dh2, "mix_rms_bwd")
    dh0, _, gs["ffn1_norm"], gw["ffn1_w_gate"], gw["ffn1_w_up"], gw["ffn1_w_down"] = ffn_backward(
        dh1, dh1_lp, h0, sp["ffn1_norm"], wt["ffn1_w_gate"], wt["ffn1_w_up"], wt["ffn1_w_down"], ffn1, "ffn1")
    grad_x = dh0.reshape(bl, t, d)[:, dm.n_meta:dm.t_real]
    dmeta = batch_sum_rows(dh0, bl, t, dm.n_meta, "meta_grad")
    return loss, grad_x, dmeta, gw, late_sum, gs


TRANSPOSED = ("ffn1_w_gate", "ffn1_w_up", "w_in", "ffn2_w_gate", "ffn2_w_up")
EARLY = ("ffn1_w_gate", "ffn1_w_up", "ffn1_w_down", "w_in")
LATE = ("w_out", "ffn2_w_gate", "ffn2_w_up", "ffn2_w_down")
NARROW = ("w_up", "a_up", "g_up", "w_uq", "w_ukv")
MATRICES = ("ffn1_w_gate", "ffn1_w_up", "ffn1_w_down", "w_in", "w_up", "a_up", "g_up", "w_uq", "w_ukv", "w_out",
            "ffn2_w_gate", "ffn2_w_up", "ffn2_w_down")
SMALL = ("ffn1_norm", "mix_norm", "tm_mu", "w0", "a0", "k_k", "k_a", "r_k", "gn_w", "gn_b", "q_norm", "kv_norm",
         "ffn2_norm", "final_norm")
WEIGHTS = ("meta_tokens", "ffn1_norm", "ffn1_w_gate", "ffn1_w_up", "ffn1_w_down", "mix_norm", "w_in", "tm_mu", "w0", "w_up",
           "a0", "a_up", "g_up", "k_k", "k_a", "r_k", "gn_w", "gn_b", "q_norm", "w_uq", "kv_norm", "w_ukv", "w_out",
           "ffn2_norm", "ffn2_w_gate", "ffn2_w_up", "ffn2_w_down", "final_norm")
PACK_COLS = 1024
PACK_ALIGN = 16 * PACK_COLS


def _pack(parts):
    offs, o = [], 0
    for p in parts:
        offs.append(o)
        o += p.shape[1]
    total = -(-o // PACK_ALIGN) * PACK_ALIGN
    flat = jnp.concatenate(list(parts) + [jnp.zeros((parts[0].shape[0], total - o), parts[0].dtype)], axis=1)
    return flat.reshape(parts[0].shape[0], total // PACK_COLS, PACK_COLS), offs


def kernel(x, meta_tokens, ffn1_norm, ffn1_w_gate, ffn1_w_up, ffn1_w_down, mix_norm, w_in, tm_mu, w0, w_up, a0, a_up, g_up, k_k, k_a, r_k, gn_w, gn_b, q_norm, w_uq, kv_norm, w_ukv, w_out, ffn2_norm, ffn2_w_gate, ffn2_w_up, ffn2_w_down, final_norm, loss_target, m_meta_tokens, m_ffn1_norm, m_ffn1_w_gate, m_ffn1_w_up, m_ffn1_w_down, m_mix_norm, m_w_in, m_tm_mu, m_w0, m_w_up, m_a0, m_a_up, m_g_up, m_k_k, m_k_a, m_r_k, m_gn_w, m_gn_b, m_q_norm, m_w_uq, m_kv_norm, m_w_ukv, m_w_out, m_ffn2_norm, m_ffn2_w_gate, m_ffn2_w_up, m_ffn2_w_down, m_final_norm, v_meta_tokens, v_ffn1_norm, v_ffn1_w_gate, v_ffn1_w_up, v_ffn1_w_down, v_mix_norm, v_w_in, v_tm_mu, v_w0, v_w_up, v_a0, v_a_up, v_g_up, v_k_k, v_k_a, v_r_k, v_gn_w, v_gn_b, v_q_norm, v_w_uq, v_kv_norm, v_w_ukv, v_w_out, v_ffn2_norm, v_ffn2_w_gate, v_ffn2_w_up, v_ffn2_w_down, v_final_norm):
    args = dict(locals())
    wts = {k: args[k] for k in WEIGHTS}
    ms = {k: args["m_" + k] for k in WEIGHTS}
    vs = {k: args["v_" + k] for k in WEIGHTS}
    dm = Dims(x, w_up, g_up, q_norm, kv_norm, ffn1_w_down.shape[1] * N_DEV)

    shard2d = {k: wts[k].reshape(wts[k].shape[-2], wts[k].shape[-1]) for k in MATRICES}
    sent = {k: shard2d[k].astype(MMD).T if k in TRANSPOSED else shard2d[k] for k in MATRICES}
    early_rows = np.cumsum([0] + [sent[k].shape[0] for k in EARLY])
    late_rows = np.cumsum([0] + [sent[k].shape[0] for k in LATE])
    send, offs = _pack([sent[k].astype(MMD).reshape(1, -1) for k in NARROW])
    got_early, got, got_meta = all_gather_two_level(jnp.concatenate([sent[k].astype(MMD) for k in EARLY], axis=0),
                                                    "gather_early", also=(send[0], meta_tokens))
    full = {k: got_early[:, lo:hi].reshape(-1, dm.d) for k, lo, hi in zip(EARLY, early_rows[:-1], early_rows[1:])}
    late_shards = jnp.concatenate([sent[k].astype(MMD) for k in LATE], axis=0)
    got = got.reshape(N_DEV, -1)
    for k, o in zip(NARROW, offs):
        r, c = sent[k].shape
        full[k] = got[:, o:o + r * c].reshape(N_DEV, r, c).transpose(1, 0, 2).reshape(r, N_DEV * c)
    mr, mc = meta_tokens.shape
    meta = got_meta.transpose(1, 0, 2).reshape(mr, N_DEV * mc)
    small = {k: wts[k].reshape(1, -1) for k in SMALL}

    loss, grad_x, dmeta, gw, gsum_late, gs = local_step(dm, x, loss_target, meta, full, late_shards, late_rows, small)

    def blocks(k, g):
        r, c = sent[k].shape
        return g.reshape(r, N_DEV, c).transpose(1, 0, 2).reshape(N_DEV, r * c)

    gsend, goffs = _pack([blocks(k, gw[k]).astype(MMD) for k in NARROW]
                         + [dmeta.reshape(mr, N_DEV, mc).transpose(1, 0, 2).reshape(N_DEV, mr * mc).astype(MMD)])
    ssend, soffs = _pack([gs[k].reshape(1, -1) for k in SMALL] + [loss])
    gearly = jnp.concatenate([gw[k].reshape(N_DEV, sent[k].shape[0], dm.d) for k in EARLY], axis=1).astype(MMD)
    gsum_early, small_parts, narrow_parts = reduce_scatter_two_level(gearly, "scatter_early", gathers=(ssend[0],),
                                                                     scatters=(gsend,))
    grads = {}
    for names, rows, gsum_rows in ((EARLY, early_rows, gsum_early), (LATE, late_rows, gsum_late)):
        for k, lo, hi in zip(names, rows[:-1], rows[1:]):
            grads[k] = gsum_rows[lo:hi].T if k in TRANSPOSED else gsum_rows[lo:hi]
    gsum = sum_blocks(narrow_parts, "sum_narrow").reshape(-1)
    for k, o in zip(NARROW, goffs):
        r, c = sent[k].shape
        grads[k] = gsum[o:o + r * c].reshape(r, c)
    grads["meta_tokens"] = gsum[goffs[-1]:goffs[-1] + mr * mc].reshape(mr, mc)
    ssum = sum_blocks(small_parts, "sum_small").reshape(-1)
    for k, o in zip(SMALL, soffs):
        grads[k] = ssum[o:o + small[k].shape[1]]
    loss_total = ssum[soffs[-1]]

    delta, new_m, new_v = {}, {}, {}
    for k in MATRICES + ("meta_tokens",):
        shp = wts[k].shape
        to2d = lambda a: a.reshape(shp[-2], shp[-1])
        dlt, nm, nv = adamw(to2d(wts[k]), grads[k], to2d(ms[k]), to2d(vs[k]), f"adamw_{k}")
        delta[k], new_m[k], new_v[k] = dlt.reshape(shp), nm.reshape(shp), nv.reshape(shp)
        grads[k] = grads[k].reshape(shp)
    pw, _ = _pack([wts[k].reshape(1, -1) for k in SMALL])
    pm_, _ = _pack([ms[k].reshape(1, -1) for k in SMALL])
    pv, _ = _pack([vs[k].reshape(1, -1) for k in SMALL])
    pg, poffs = _pack([grads[k].reshape(1, -1) for k in SMALL])
    dlt, nm, nv = adamw(pw[0], pg[0], pm_[0], pv[0], "adamw_small")
    for k, o in zip(SMALL, poffs):
        shp, sz = wts[k].shape, small[k].shape[1]
        cut = lambda a: a.reshape(-1)[o:o + sz].reshape(shp)
        delta[k], new_m[k], new_v[k] = cut(dlt), cut(nm), cut(nv)
        grads[k] = grads[k].reshape(shp)

    return (loss_total, grad_x, *[grads[k] for k in WEIGHTS], *[delta[k] for k in WEIGHTS],
            *[new_m[k] for k in WEIGHTS], *[new_v[k] for k in WEIGHTS])
```
